```python
import jax, jax.numpy as jnp
from jax import lax
import numpy as np

D_MODEL = 1024
BATCH = 8
SEQ = 8192
DEPTH = 2

CHUNK = 64
N_MIXERS = 2
N_CONV_LAYERS = (DEPTH + 1) // 2
N_MLA_LAYERS = DEPTH // 2
D_FF = 2816
FFN_RES_WEIGHT = 0.5
CONV_WIDTH = 31
N_HEADS = 8
QK_NOPE = 128
QK_ROPE = 64
V_HEAD = 128
Q_LORA = 512
KV_LORA = 256
ROPE_THETA = 10000.0
Q_BLOCK = 128
RMS_EPS = 1e-6
POS_OFFSET_MAX = 65536

kernel_name = "hybrid_conformer_conv_mla_macaron"


def rmsnorm(x, g):
    xf = x.astype(jnp.float32)
    y = xf * lax.rsqrt(jnp.mean(xf * xf, axis=-1, keepdims=True) + RMS_EPS) * g.astype(jnp.float32)
    return y.astype(x.dtype)


def swiglu_ffn(h, w1, w3, w2):
    return (jax.nn.silu(h @ w1) * (h @ w3)) @ w2


def conv_module(h, w_pw1, w_dw, g_norm, w_pw2):
    u = h @ w_pw1
    a, b = jnp.split(u, 2, axis=-1)
    u = a * jax.nn.sigmoid(b)
    u = lax.conv_general_dilated(
        u, w_dw[:, None, :].astype(u.dtype), window_strides=(1,),
        padding=[(CONV_WIDTH - 1, 0)],
        dimension_numbers=("NWC", "WIO", "NWC"),
        feature_group_count=D_MODEL)
    u = jax.nn.silu(rmsnorm(u, g_norm))
    return u @ w_pw2


def apply_rope(x, cos, sin):
    half = x.shape[-1] // 2
    xf = x.astype(jnp.float32)
    x1, x2 = xf[..., :half], xf[..., half:]
    out = jnp.concatenate([x1 * cos - x2 * sin, x2 * cos + x1 * sin], axis=-1)
    return out.astype(x.dtype)


def mla(h, positions, w_a, g_q, g_kv, w_uq, w_ukv, w_o):
    a = h @ w_a
    c_q = rmsnorm(a[..., :Q_LORA], g_q)
    c_kv = rmsnorm(a[..., Q_LORA:Q_LORA + KV_LORA], g_kv)
    k_rope = a[..., Q_LORA + KV_LORA:]
    q = jnp.einsum("bsc,chd->bshd", c_q, w_uq)
    q_nope, q_rope = q[..., :QK_NOPE], q[..., QK_NOPE:]
    kv = jnp.einsum("bsc,chd->bshd", c_kv, w_ukv)
    k_nope, v = kv[..., :QK_NOPE], kv[..., QK_NOPE:]

    inv_freq = ROPE_THETA ** (-2.0 * jnp.arange(QK_ROPE // 2, dtype=jnp.float32) / QK_ROPE)
    ang = positions.astype(jnp.float32)[..., None] * inv_freq
    cos, sin = jnp.cos(ang), jnp.sin(ang)
    q_rope = apply_rope(q_rope, cos[:, :, None, :], sin[:, :, None, :])
    k_rope = apply_rope(k_rope, cos, sin)

    seq = h.shape[1]
    scale = (QK_NOPE + QK_ROPE) ** -0.5
    chunk_id = jnp.arange(seq) // CHUNK
    outs = []
    for blk in range(seq // Q_BLOCK):
        q0, q1 = blk * Q_BLOCK, (blk + 1) * Q_BLOCK
        s = (jnp.einsum("bqhd,bkhd->bhqk", q_nope[:, q0:q1], k_nope[:, :q1],
                        preferred_element_type=jnp.float32)
             + jnp.einsum("bqhr,bkr->bhqk", q_rope[:, q0:q1], k_rope[:, :q1],
                          preferred_element_type=jnp.float32)) * scale
        mask = chunk_id[q0:q1, None] >= chunk_id[None, :q1]
        s = jnp.where(mask[None, None], s, -jnp.inf)
        p = jax.nn.softmax(s, axis=-1).astype(v.dtype)
        outs.append(jnp.einsum("bhqk,bkhd->bqhd", p, v[:, :q1]))
    o = jnp.concatenate(outs, axis=1)
    return o.reshape(o.shape[0], seq, N_HEADS * V_HEAD) @ w_o


def _fwd_setup_inputs(seed: int = 0) -> dict:
    key = jax.random.key(seed)
    ks = jax.random.split(key, 24)
    f32 = jnp.float32

    def w(k, shape, fan_in):
        return jax.random.normal(k, shape, f32) * (fan_in ** -0.5)

    def gain(k, shape):
        return 1.0 + 0.02 * jax.random.normal(k, shape, f32)

    x = jax.random.normal(ks[0], (BATCH, SEQ, D_MODEL), f32)
    offset = jax.random.randint(ks[1], (BATCH, 1), 0, POS_OFFSET_MAX, dtype=jnp.int32)
    positions = offset + jnp.arange(SEQ, dtype=jnp.int32)[None, :]
    return {
        "x": x,
        "positions": positions,
        "ffn_norm1": gain(ks[2], (DEPTH, D_MODEL)),
        "ffn1_w1": w(ks[3], (DEPTH, D_MODEL, D_FF), D_MODEL),
        "ffn1_w3": w(ks[4], (DEPTH, D_MODEL, D_FF), D_MODEL),
        "ffn1_w2": w(ks[5], (DEPTH, D_FF, D_MODEL), D_FF),
        "mix_norm": gain(ks[6], (DEPTH, D_MODEL)),
        "ffn_norm2": gain(ks[7], (DEPTH, D_MODEL)),
        "ffn2_w1": w(ks[8], (DEPTH, D_MODEL, D_FF), D_MODEL),
        "ffn2_w3": w(ks[9], (DEPTH, D_MODEL, D_FF), D_MODEL),
        "ffn2_w2": w(ks[10], (DEPTH, D_FF, D_MODEL), D_FF),
        "conv_w_pw1": w(ks[11], (N_CONV_LAYERS, D_MODEL, 2 * D_MODEL), D_MODEL),
        "conv_w_dw": w(ks[12], (N_CONV_LAYERS, CONV_WIDTH, D_MODEL), CONV_WIDTH),
        "conv_norm": gain(ks[13], (N_CONV_LAYERS, D_MODEL)),
        "conv_w_pw2": w(ks[14], (N_CONV_LAYERS, D_MODEL, D_MODEL), D_MODEL),
        "mla_w_a": w(ks[15], (N_MLA_LAYERS, D_MODEL, Q_LORA + KV_LORA + QK_ROPE), D_MODEL),
        "mla_q_norm": gain(ks[16], (N_MLA_LAYERS, Q_LORA)),
        "mla_kv_norm": gain(ks[17], (N_MLA_LAYERS, KV_LORA)),
        "mla_w_uq": w(ks[18], (N_MLA_LAYERS, Q_LORA, N_HEADS, QK_NOPE + QK_ROPE), Q_LORA),
        "mla_w_ukv": w(ks[19], (N_MLA_LAYERS, KV_LORA, N_HEADS, QK_NOPE + V_HEAD), KV_LORA),
        "mla_w_o": w(ks[20], (N_MLA_LAYERS, N_HEADS * V_HEAD, D_MODEL), N_HEADS * V_HEAD),
        "final_norm": gain(ks[21], (D_MODEL,)),
    }


def _fwd_reference(x, positions, ffn_norm1, ffn1_w1, ffn1_w3, ffn1_w2, mix_norm, ffn_norm2,
              ffn2_w1, ffn2_w3, ffn2_w2, conv_w_pw1, conv_w_dw, conv_norm, conv_w_pw2,
              mla_w_a, mla_q_norm, mla_kv_norm, mla_w_uq, mla_w_ukv, mla_w_o, final_norm):
    h = x
    for i in range(DEPTH):
        h = h + FFN_RES_WEIGHT * swiglu_ffn(rmsnorm(h, ffn_norm1[i]), ffn1_w1[i], ffn1_w3[i], ffn1_w2[i])
        m = rmsnorm(h, mix_norm[i])
        j = i // N_MIXERS
        if i % N_MIXERS == 0:
            h = h + conv_module(m, conv_w_pw1[j], conv_w_dw[j], conv_norm[j], conv_w_pw2[j])
        else:
            h = h + mla(m, positions, mla_w_a[j], mla_q_norm[j], mla_kv_norm[j],
                        mla_w_uq[j], mla_w_ukv[j], mla_w_o[j])
        h = h + FFN_RES_WEIGHT * swiglu_ffn(rmsnorm(h, ffn_norm2[i]), ffn2_w1[i], ffn2_w3[i], ffn2_w2[i])
    return rmsnorm(h, final_norm)


import jax as _jax
import jax.numpy as _jnp

TWIN_FORMAT = 'train_step'
FWD_PARAMS = ['x', 'positions', 'ffn_norm1', 'ffn1_w1', 'ffn1_w3', 'ffn1_w2', 'mix_norm', 'ffn_norm2', 'ffn2_w1', 'ffn2_w3', 'ffn2_w2', 'conv_w_pw1', 'conv_w_dw', 'conv_norm', 'conv_w_pw2', 'mla_w_a', 'mla_q_norm', 'mla_kv_norm', 'mla_w_uq', 'mla_w_ukv', 'mla_w_o', 'final_norm']
TWIN_WEIGHTS = ['ffn_norm1', 'ffn1_w1', 'ffn1_w3', 'ffn1_w2', 'mix_norm', 'ffn_norm2', 'ffn2_w1', 'ffn2_w3', 'ffn2_w2', 'conv_w_pw1', 'conv_w_dw', 'conv_norm', 'conv_w_pw2', 'mla_w_a', 'mla_q_norm', 'mla_kv_norm', 'mla_w_uq', 'mla_w_ukv', 'mla_w_o', 'final_norm']
TWIN_DIFF_INPUT = 'x'
TWIN_INPUTS = ['x', 'positions', 'ffn_norm1', 'ffn1_w1', 'ffn1_w3', 'ffn1_w2', 'mix_norm', 'ffn_norm2', 'ffn2_w1', 'ffn2_w3', 'ffn2_w2', 'conv_w_pw1', 'conv_w_dw', 'conv_norm', 'conv_w_pw2', 'mla_w_a', 'mla_q_norm', 'mla_kv_norm', 'mla_w_uq', 'mla_w_ukv', 'mla_w_o', 'final_norm', 'loss_target', 'm_ffn_norm1', 'm_ffn1_w1', 'm_ffn1_w3', 'm_ffn1_w2', 'm_mix_norm', 'm_ffn_norm2', 'm_ffn2_w1', 'm_ffn2_w3', 'm_ffn2_w2', 'm_conv_w_pw1', 'm_conv_w_dw', 'm_conv_norm', 'm_conv_w_pw2', 'm_mla_w_a', 'm_mla_q_norm', 'm_mla_kv_norm', 'm_mla_w_uq', 'm_mla_w_ukv', 'm_mla_w_o', 'm_final_norm', 'v_ffn_norm1', 'v_ffn1_w1', 'v_ffn1_w3', 'v_ffn1_w2', 'v_mix_norm', 'v_ffn_norm2', 'v_ffn2_w1', 'v_ffn2_w3', 'v_ffn2_w2', 'v_conv_w_pw1', 'v_conv_w_dw', 'v_conv_norm', 'v_conv_w_pw2', 'v_mla_w_a', 'v_mla_q_norm', 'v_mla_kv_norm', 'v_mla_w_uq', 'v_mla_w_ukv', 'v_mla_w_o', 'v_final_norm']
TWIN_OUTPUTS = ['loss', 'grad_x', 'grad_ffn_norm1', 'grad_ffn1_w1', 'grad_ffn1_w3', 'grad_ffn1_w2', 'grad_mix_norm', 'grad_ffn_norm2', 'grad_ffn2_w1', 'grad_ffn2_w3', 'grad_ffn2_w2', 'grad_conv_w_pw1', 'grad_conv_w_dw', 'grad_conv_norm', 'grad_conv_w_pw2', 'grad_mla_w_a', 'grad_mla_q_norm', 'grad_mla_kv_norm', 'grad_mla_w_uq', 'grad_mla_w_ukv', 'grad_mla_w_o', 'grad_final_norm', 'delta_ffn_norm1', 'delta_ffn1_w1', 'delta_ffn1_w3', 'delta_ffn1_w2', 'delta_mix_norm', 'delta_ffn_norm2', 'delta_ffn2_w1', 'delta_ffn2_w3', 'delta_ffn2_w2', 'delta_conv_w_pw1', 'delta_conv_w_dw', 'delta_conv_norm', 'delta_conv_w_pw2', 'delta_mla_w_a', 'delta_mla_q_norm', 'delta_mla_kv_norm', 'delta_mla_w_uq', 'delta_mla_w_ukv', 'delta_mla_w_o', 'delta_final_norm', 'new_m_ffn_norm1', 'new_m_ffn1_w1', 'new_m_ffn1_w3', 'new_m_ffn1_w2', 'new_m_mix_norm', 'new_m_ffn_norm2', 'new_m_ffn2_w1', 'new_m_ffn2_w3', 'new_m_ffn2_w2', 'new_m_conv_w_pw1', 'new_m_conv_w_dw', 'new_m_conv_norm', 'new_m_conv_w_pw2', 'new_m_mla_w_a', 'new_m_mla_q_norm', 'new_m_mla_kv_norm', 'new_m_mla_w_uq', 'new_m_mla_w_ukv', 'new_m_mla_w_o', 'new_m_final_norm', 'new_v_ffn_norm1', 'new_v_ffn1_w1', 'new_v_ffn1_w3', 'new_v_ffn1_w2', 'new_v_mix_norm', 'new_v_ffn_norm2', 'new_v_ffn2_w1', 'new_v_ffn2_w3', 'new_v_ffn2_w2', 'new_v_conv_w_pw1', 'new_v_conv_w_dw', 'new_v_conv_norm', 'new_v_conv_w_pw2', 'new_v_mla_w_a', 'new_v_mla_q_norm', 'new_v_mla_kv_norm', 'new_v_mla_w_uq', 'new_v_mla_w_ukv', 'new_v_mla_w_o', 'new_v_final_norm']
TWIN_LEAF_KINDS = {'loss': 'loss', 'grad_x': 'grad_x', 'grad_ffn_norm1': 'grad_w', 'grad_ffn1_w1': 'grad_w', 'grad_ffn1_w3': 'grad_w', 'grad_ffn1_w2': 'grad_w', 'grad_mix_norm': 'grad_w', 'grad_ffn_norm2': 'grad_w', 'grad_ffn2_w1': 'grad_w', 'grad_ffn2_w3': 'grad_w', 'grad_ffn2_w2': 'grad_w', 'grad_conv_w_pw1': 'grad_w', 'grad_conv_w_dw': 'grad_w', 'grad_conv_norm': 'grad_w', 'grad_conv_w_pw2': 'grad_w', 'grad_mla_w_a': 'grad_w', 'grad_mla_q_norm': 'grad_w', 'grad_mla_kv_norm': 'grad_w', 'grad_mla_w_uq': 'grad_w', 'grad_mla_w_ukv': 'grad_w', 'grad_mla_w_o': 'grad_w', 'grad_final_norm': 'grad_w', 'delta_ffn_norm1': 'delta_w', 'delta_ffn1_w1': 'delta_w', 'delta_ffn1_w3': 'delta_w', 'delta_ffn1_w2': 'delta_w', 'delta_mix_norm': 'delta_w', 'delta_ffn_norm2': 'delta_w', 'delta_ffn2_w1': 'delta_w', 'delta_ffn2_w3': 'delta_w', 'delta_ffn2_w2': 'delta_w', 'delta_conv_w_pw1': 'delta_w', 'delta_conv_w_dw': 'delta_w', 'delta_conv_norm': 'delta_w', 'delta_conv_w_pw2': 'delta_w', 'delta_mla_w_a': 'delta_w', 'delta_mla_q_norm': 'delta_w', 'delta_mla_kv_norm': 'delta_w', 'delta_mla_w_uq': 'delta_w', 'delta_mla_w_ukv': 'delta_w', 'delta_mla_w_o': 'delta_w', 'delta_final_norm': 'delta_w', 'new_m_ffn_norm1': 'new_m', 'new_m_ffn1_w1': 'new_m', 'new_m_ffn1_w3': 'new_m', 'new_m_ffn1_w2': 'new_m', 'new_m_mix_norm': 'new_m', 'new_m_ffn_norm2': 'new_m', 'new_m_ffn2_w1': 'new_m', 'new_m_ffn2_w3': 'new_m', 'new_m_ffn2_w2': 'new_m', 'new_m_conv_w_pw1': 'new_m', 'new_m_conv_w_dw': 'new_m', 'new_m_conv_norm': 'new_m', 'new_m_conv_w_pw2': 'new_m', 'new_m_mla_w_a': 'new_m', 'new_m_mla_q_norm': 'new_m', 'new_m_mla_kv_norm': 'new_m', 'new_m_mla_w_uq': 'new_m', 'new_m_mla_w_ukv': 'new_m', 'new_m_mla_w_o': 'new_m', 'new_m_final_norm': 'new_m', 'new_v_ffn_norm1': 'new_v', 'new_v_ffn1_w1': 'new_v', 'new_v_ffn1_w3': 'new_v', 'new_v_ffn1_w2': 'new_v', 'new_v_mix_norm': 'new_v', 'new_v_ffn_norm2': 'new_v', 'new_v_ffn2_w1': 'new_v', 'new_v_ffn2_w3': 'new_v', 'new_v_ffn2_w2': 'new_v', 'new_v_conv_w_pw1': 'new_v', 'new_v_conv_w_dw': 'new_v', 'new_v_conv_norm': 'new_v', 'new_v_conv_w_pw2': 'new_v', 'new_v_mla_w_a': 'new_v', 'new_v_mla_q_norm': 'new_v', 'new_v_mla_kv_norm': 'new_v', 'new_v_mla_w_uq': 'new_v', 'new_v_mla_w_ukv': 'new_v', 'new_v_mla_w_o': 'new_v', 'new_v_final_norm': 'new_v'}


def _forward(args):
    return _fwd_reference(*[args[k] for k in FWD_PARAMS])


def _output_shape():
    out = _jax.eval_shape(lambda: _forward(_fwd_setup_inputs(0)))
    return out.shape, out.dtype

N_MICROBATCH = 1
ADAM_LR = 0.001
ADAM_B1 = 0.9
ADAM_B2 = 0.999
ADAM_EPS = 1e-08
ADAM_WD = 0.01
ADAM_STEP = 10
PER_EXAMPLE_BATCH_AXIS = {'x': 0, 'positions': 0, 'loss_target': 0}
SHARED_INPUTS = []
_WEIGHT_DTYPES = {'ffn_norm1': _jnp.float32, 'ffn1_w1': _jnp.float32, 'ffn1_w3': _jnp.float32, 'ffn1_w2': _jnp.float32, 'mix_norm': _jnp.float32, 'ffn_norm2': _jnp.float32, 'ffn2_w1': _jnp.float32, 'ffn2_w3': _jnp.float32, 'ffn2_w2': _jnp.float32, 'conv_w_pw1': _jnp.float32, 'conv_w_dw': _jnp.float32, 'conv_norm': _jnp.float32, 'conv_w_pw2': _jnp.float32, 'mla_w_a': _jnp.float32, 'mla_q_norm': _jnp.float32, 'mla_kv_norm': _jnp.float32, 'mla_w_uq': _jnp.float32, 'mla_w_ukv': _jnp.float32, 'mla_w_o': _jnp.float32, 'final_norm': _jnp.float32}
MOMENT_SCALE = {'ffn_norm1': 1.035771e-01, 'ffn1_w1': 4.447469e-02, 'ffn1_w3': 4.310026e-02, 'ffn1_w2': 7.130828e-02, 'mix_norm': 1.155081e-01, 'ffn_norm2': 8.631854e-02, 'ffn2_w1': 3.747056e-02, 'ffn2_w3': 3.627037e-02, 'ffn2_w2': 6.018287e-02, 'conv_w_pw1': 1.078779e-01, 'conv_w_dw': 1.418902e-01, 'conv_norm': 1.813406e-01, 'conv_w_pw2': 1.417576e-01, 'mla_w_a': 6.585145e-02, 'mla_q_norm': 3.931015e-02, 'mla_kv_norm': 1.057072e-01, 'mla_w_uq': 2.317963e-02, 'mla_w_ukv': 3.613707e-02, 'mla_w_o': 4.658749e-02, 'final_norm': 6.404033e+01}


def _to_microbatches(a, axis):
    t = _jnp.moveaxis(a, axis, 0)
    t = t.reshape((N_MICROBATCH, t.shape[0] // N_MICROBATCH) + t.shape[1:])
    return _jnp.moveaxis(t, 1, axis + 1)


def setup_inputs(seed: int = 0) -> dict:
    inp = _fwd_setup_inputs(seed)
    key = _jax.random.fold_in(_jax.random.key(seed), 7919)
    shape, _ = _output_shape()
    out = dict(inp)
    out["loss_target"] = _jax.random.normal(_jax.random.fold_in(key, 0), shape, _jnp.float32)
    for i, name in enumerate(TWIN_WEIGHTS):
        w = inp[name].astype(_jnp.float32)
        if MOMENT_SCALE is None:
            s = _jnp.sqrt(_jnp.mean(_jnp.square(w)) + 1e-30)
        else:
            s = MOMENT_SCALE[name]
        km, kv = _jax.random.split(_jax.random.fold_in(key, i + 1))
        out[name] = w
        out["m_" + name] = s * _jax.random.normal(km, w.shape, _jnp.float32)
        out["v_" + name] = (s * s) * _jax.random.uniform(kv, w.shape, _jnp.float32, 0.5, 1.5)
    if N_MICROBATCH > 1:
        for name, axis in PER_EXAMPLE_BATCH_AXIS.items():
            out[name] = _to_microbatches(out[name], axis)
    return {'x': out['x'], 'positions': out['positions'], 'ffn_norm1': out['ffn_norm1'], 'ffn1_w1': out['ffn1_w1'], 'ffn1_w3': out['ffn1_w3'], 'ffn1_w2': out['ffn1_w2'], 'mix_norm': out['mix_norm'], 'ffn_norm2': out['ffn_norm2'], 'ffn2_w1': out['ffn2_w1'], 'ffn2_w3': out['ffn2_w3'], 'ffn2_w2': out['ffn2_w2'], 'conv_w_pw1': out['conv_w_pw1'], 'conv_w_dw': out['conv_w_dw'], 'conv_norm': out['conv_norm'], 'conv_w_pw2': out['conv_w_pw2'], 'mla_w_a': out['mla_w_a'], 'mla_q_norm': out['mla_q_norm'], 'mla_kv_norm': out['mla_kv_norm'], 'mla_w_uq': out['mla_w_uq'], 'mla_w_ukv': out['mla_w_ukv'], 'mla_w_o': out['mla_w_o'], 'final_norm': out['final_norm'], 'loss_target': out['loss_target'], 'm_ffn_norm1': out['m_ffn_norm1'], 'm_ffn1_w1': out['m_ffn1_w1'], 'm_ffn1_w3': out['m_ffn1_w3'], 'm_ffn1_w2': out['m_ffn1_w2'], 'm_mix_norm': out['m_mix_norm'], 'm_ffn_norm2': out['m_ffn_norm2'], 'm_ffn2_w1': out['m_ffn2_w1'], 'm_ffn2_w3': out['m_ffn2_w3'], 'm_ffn2_w2': out['m_ffn2_w2'], 'm_conv_w_pw1': out['m_conv_w_pw1'], 'm_conv_w_dw': out['m_conv_w_dw'], 'm_conv_norm': out['m_conv_norm'], 'm_conv_w_pw2': out['m_conv_w_pw2'], 'm_mla_w_a': out['m_mla_w_a'], 'm_mla_q_norm': out['m_mla_q_norm'], 'm_mla_kv_norm': out['m_mla_kv_norm'], 'm_mla_w_uq': out['m_mla_w_uq'], 'm_mla_w_ukv': out['m_mla_w_ukv'], 'm_mla_w_o': out['m_mla_w_o'], 'm_final_norm': out['m_final_norm'], 'v_ffn_norm1': out['v_ffn_norm1'], 'v_ffn1_w1': out['v_ffn1_w1'], 'v_ffn1_w3': out['v_ffn1_w3'], 'v_ffn1_w2': out['v_ffn1_w2'], 'v_mix_norm': out['v_mix_norm'], 'v_ffn_norm2': out['v_ffn_norm2'], 'v_ffn2_w1': out['v_ffn2_w1'], 'v_ffn2_w3': out['v_ffn2_w3'], 'v_ffn2_w2': out['v_ffn2_w2'], 'v_conv_w_pw1': out['v_conv_w_pw1'], 'v_conv_w_dw': out['v_conv_w_dw'], 'v_conv_norm': out['v_conv_norm'], 'v_conv_w_pw2': out['v_conv_w_pw2'], 'v_mla_w_a': out['v_mla_w_a'], 'v_mla_q_norm': out['v_mla_q_norm'], 'v_mla_kv_norm': out['v_mla_kv_norm'], 'v_mla_w_uq': out['v_mla_w_uq'], 'v_mla_w_ukv': out['v_mla_w_ukv'], 'v_mla_w_o': out['v_mla_w_o'], 'v_final_norm': out['v_final_norm']}


def _loss(weights, diff, rest, loss_target):
    with _jax.named_scope("forward"):
        args = {**rest, TWIN_DIFF_INPUT: diff, **{k: w.astype(_WEIGHT_DTYPES[k]) for k, w in weights.items()}}
        y = _forward(args)
    with _jax.named_scope("loss_head"):
        err = _jnp.square(y.astype(_jnp.float32) - loss_target)
        return 0.5 * _jnp.sum(_jnp.mean(err, axis=-1)) if err.ndim else 0.5 * err


def _adamw(w, g, m, v):
    m = ADAM_B1 * m + (1.0 - ADAM_B1) * g
    v = ADAM_B2 * v + (1.0 - ADAM_B2) * _jnp.square(g)
    m_hat = m / (1.0 - ADAM_B1 ** ADAM_STEP)
    v_hat = v / (1.0 - ADAM_B2 ** ADAM_STEP)
    delta = -ADAM_LR * (m_hat / (_jnp.sqrt(v_hat) + ADAM_EPS) + ADAM_WD * w)
    return delta, m, v


def reference(x, positions, ffn_norm1, ffn1_w1, ffn1_w3, ffn1_w2, mix_norm, ffn_norm2, ffn2_w1, ffn2_w3, ffn2_w2, conv_w_pw1, conv_w_dw, conv_norm, conv_w_pw2, mla_w_a, mla_q_norm, mla_kv_norm, mla_w_uq, mla_w_ukv, mla_w_o, final_norm, loss_target, m_ffn_norm1, m_ffn1_w1, m_ffn1_w3, m_ffn1_w2, m_mix_norm, m_ffn_norm2, m_ffn2_w1, m_ffn2_w3, m_ffn2_w2, m_conv_w_pw1, m_conv_w_dw, m_conv_norm, m_conv_w_pw2, m_mla_w_a, m_mla_q_norm, m_mla_kv_norm, m_mla_w_uq, m_mla_w_ukv, m_mla_w_o, m_final_norm, v_ffn_norm1, v_ffn1_w1, v_ffn1_w3, v_ffn1_w2, v_mix_norm, v_ffn_norm2, v_ffn2_w1, v_ffn2_w3, v_ffn2_w2, v_conv_w_pw1, v_conv_w_dw, v_conv_norm, v_conv_w_pw2, v_mla_w_a, v_mla_q_norm, v_mla_kv_norm, v_mla_w_uq, v_mla_w_ukv, v_mla_w_o, v_final_norm):
    given = dict(x=x, positions=positions, ffn_norm1=ffn_norm1, ffn1_w1=ffn1_w1, ffn1_w3=ffn1_w3, ffn1_w2=ffn1_w2, mix_norm=mix_norm, ffn_norm2=ffn_norm2, ffn2_w1=ffn2_w1, ffn2_w3=ffn2_w3, ffn2_w2=ffn2_w2, conv_w_pw1=conv_w_pw1, conv_w_dw=conv_w_dw, conv_norm=conv_norm, conv_w_pw2=conv_w_pw2, mla_w_a=mla_w_a, mla_q_norm=mla_q_norm, mla_kv_norm=mla_kv_norm, mla_w_uq=mla_w_uq, mla_w_ukv=mla_w_ukv, mla_w_o=mla_w_o, final_norm=final_norm, loss_target=loss_target, m_ffn_norm1=m_ffn_norm1, m_ffn1_w1=m_ffn1_w1, m_ffn1_w3=m_ffn1_w3, m_ffn1_w2=m_ffn1_w2, m_mix_norm=m_mix_norm, m_ffn_norm2=m_ffn_norm2, m_ffn2_w1=m_ffn2_w1, m_ffn2_w3=m_ffn2_w3, m_ffn2_w2=m_ffn2_w2, m_conv_w_pw1=m_conv_w_pw1, m_conv_w_dw=m_conv_w_dw, m_conv_norm=m_conv_norm, m_conv_w_pw2=m_conv_w_pw2, m_mla_w_a=m_mla_w_a, m_mla_q_norm=m_mla_q_norm, m_mla_kv_norm=m_mla_kv_norm, m_mla_w_uq=m_mla_w_uq, m_mla_w_ukv=m_mla_w_ukv, m_mla_w_o=m_mla_w_o, m_final_norm=m_final_norm, v_ffn_norm1=v_ffn_norm1, v_ffn1_w1=v_ffn1_w1, v_ffn1_w3=v_ffn1_w3, v_ffn1_w2=v_ffn1_w2, v_mix_norm=v_mix_norm, v_ffn_norm2=v_ffn_norm2, v_ffn2_w1=v_ffn2_w1, v_ffn2_w3=v_ffn2_w3, v_ffn2_w2=v_ffn2_w2, v_conv_w_pw1=v_conv_w_pw1, v_conv_w_dw=v_conv_w_dw, v_conv_norm=v_conv_norm, v_conv_w_pw2=v_conv_w_pw2, v_mla_w_a=v_mla_w_a, v_mla_q_norm=v_mla_q_norm, v_mla_kv_norm=v_mla_kv_norm, v_mla_w_uq=v_mla_w_uq, v_mla_w_ukv=v_mla_w_ukv, v_mla_w_o=v_mla_w_o, v_final_norm=v_final_norm)
    weights = {n: given[n] for n in TWIN_WEIGHTS}
    shared = {n: given[n] for n in SHARED_INPUTS}
    per_example = {n: given[n] for n in ['x', 'positions']}
    grad_fn = _jax.value_and_grad(_loss, argnums=(0, 1))

    def one_microbatch(ex, loss_target):
        ex = dict(ex)
        diff = ex.pop(TWIN_DIFF_INPUT)
        return grad_fn(weights, diff, {**shared, **ex}, loss_target)

    if N_MICROBATCH == 1:
        loss, (grad_w, grad_x) = one_microbatch(per_example, given["loss_target"])
    else:
        def body(carry, xs):
            loss_sum, grad_sum = carry
            l_k, (gw_k, gx_k) = one_microbatch(xs[0], xs[1])
            with _jax.named_scope("update"):
                return (loss_sum + l_k, _jax.tree.map(_jnp.add, grad_sum, gw_k)), gx_k

        init = (_jnp.zeros((), _jnp.float32), _jax.tree.map(_jnp.zeros_like, weights))
        (loss, grad_w), grad_x = _jax.lax.scan(body, init, (per_example, given["loss_target"]))
    with _jax.named_scope("update"):
        delta_w, new_m, new_v = {}, {}, {}
        for n in TWIN_WEIGHTS:
            delta_w[n], new_m[n], new_v[n] = _adamw(weights[n], grad_w[n], given["m_" + n], given["v_" + n])
    return (loss, grad_x, *[grad_w[n] for n in TWIN_WEIGHTS], *[delta_w[n] for n in TWIN_WEIGHTS],
            *[new_m[n] for n in TWIN_WEIGHTS], *[new_v[n] for n in TWIN_WEIGHTS])
```

```python
import functools

import jax
import jax.numpy as jnp
import numpy as np
from jax import lax
from jax.experimental import pallas as pl
from jax.experimental.pallas import tpu as pltpu

F32 = jnp.float32
BF16 = jnp.bfloat16
MESH = pl.DeviceIdType.MESH

RMS_EPS = 1e-6
HEADS = 8
NOPE = 128
ROPE = 64
HEAD_PAD = 256
V_HEAD = 128
Q_LORA = 512
KV_LORA = 256
A_PAD = 896
CHUNK = 64
CONV_WIDTH = 31
CONV_HALO = 32
CONV_ROWS = 16
ROPE_THETA = 10000.0
ATTN_SCALE = (NOPE + ROPE) ** -0.5
FFN_RES = 0.5

ADAM_LR = 0.001
ADAM_B1 = 0.9
ADAM_B2 = 0.999
ADAM_EPS = 1e-08
ADAM_WD = 0.01
ADAM_STEP = 10

VMEM_LIMIT_BYTES = 56 * 1024 * 1024

WEIGHTS = ['ffn_norm1', 'ffn1_w1', 'ffn1_w3', 'ffn1_w2', 'mix_norm', 'ffn_norm2', 'ffn2_w1', 'ffn2_w3', 'ffn2_w2',
           'conv_w_pw1', 'conv_w_dw', 'conv_norm', 'conv_w_pw2', 'mla_w_a', 'mla_q_norm', 'mla_kv_norm', 'mla_w_uq',
           'mla_w_ukv', 'mla_w_o', 'final_norm']
INPUTS = (['x', 'positions'] + WEIGHTS + ['loss_target'] + ['m_' + w for w in WEIGHTS] + ['v_' + w for w in WEIGHTS])
BIG = ['ffn1_w1', 'ffn1_w3', 'ffn1_w2', 'ffn2_w1', 'ffn2_w3', 'ffn2_w2', 'conv_w_pw1', 'conv_w_pw2', 'mla_w_a',
       'mla_w_uq', 'mla_w_ukv', 'mla_w_o']


def _params(*sem):
    return pltpu.CompilerParams(dimension_semantics=sem, vmem_limit_bytes=VMEM_LIMIT_BYTES)


def _bf(v):
    return v.astype(BF16)


def _rstd(x):
    return lax.rsqrt(jnp.mean(x * x, axis=-1, keepdims=True) + RMS_EPS)


def _sigmoid(x):
    return jax.nn.sigmoid(x)


def _rot(x):
    lane = lax.broadcasted_iota(jnp.int32, x.shape, 1)
    return jnp.where(lane < ROPE // 2, -pltpu.roll(x, 128 - ROPE // 2, 1), pltpu.roll(x, ROPE // 2, 1))


def _rot_t(y):
    lane = lax.broadcasted_iota(jnp.int32, y.shape, 1)
    return jnp.where(lane < ROPE // 2, pltpu.roll(y, 128 - ROPE // 2, 1), -pltpu.roll(y, ROPE // 2, 1))


def _pair_sum(a_refs, b_refs, trans_b):
    tot = None
    for a_r, b_r in zip(a_refs, b_refs):
        a, b = _bf(a_r[...]), _bf(b_r[...])
        if trans_b:
            d = lax.dot_general(a, b, (((1,), (1,)), ((), ())), preferred_element_type=F32)
        else:
            d = jnp.dot(a, b, preferred_element_type=F32)
        tot = d if tot is None else tot + d
    return tot


def _mm(pairs, out_dtype, name, *, trans_b=False, tm=512, tn=None, tk=None, res=None):
    m, k = pairs[0][0].shape
    n = pairs[0][1].shape[0] if trans_b else pairs[0][1].shape[1]
    tm, tn, tk = min(tm, m), tn or n, tk or k
    nk, npair = k // tk, len(pairs)

    def kern(*refs):
        a_refs, b_refs = refs[:npair], refs[npair:2 * npair]
        rest = list(refs[2 * npair:])
        res_ref = rest.pop(0) if res is not None else None
        o_ref = rest.pop(0)

        def finish(acc):
            if res_ref is not None:
                acc = res_ref[...] + acc
            o_ref[...] = acc.astype(o_ref.dtype)

        if nk == 1:
            finish(_pair_sum(a_refs, b_refs, trans_b))
        else:
            acc_ref = rest.pop(0)
            kk = pl.program_id(2)

            @pl.when(kk == 0)
            def _():
                acc_ref[...] = jnp.zeros_like(acc_ref)

            acc_ref[...] += _pair_sum(a_refs, b_refs, trans_b)

            @pl.when(kk == nk - 1)
            def _():
                finish(acc_ref[...])

    a_spec = pl.BlockSpec((tm, tk), lambda i, j, kk: (i, kk))
    b_spec = (pl.BlockSpec((tn, tk), lambda i, j, kk: (j, kk)) if trans_b
              else pl.BlockSpec((tk, tn), lambda i, j, kk: (kk, j)))
    io_spec = pl.BlockSpec((tm, tn), lambda i, j, kk: (i, j))
    in_specs = [a_spec] * npair + [b_spec] * npair + ([io_spec] if res is not None else [])
    args = [p[0] for p in pairs] + [p[1] for p in pairs] + ([res] if res is not None else [])
    return pl.pallas_call(
        kern, name=name, grid=(m // tm, n // tn, nk), in_specs=in_specs, out_specs=io_spec,
        out_shape=jax.ShapeDtypeStruct((m, n), out_dtype),
        scratch_shapes=[pltpu.VMEM((tm, tn), F32)] if nk > 1 else [],
        compiler_params=_params("parallel", "parallel", "arbitrary"))(*args)


def _mm_normbwd(pairs, h, g, dres, name, *, tm=512, tk=None):
    m, k = pairs[0][0].shape
    d = pairs[0][1].shape[0]
    tm, tk = min(tm, m), tk or k
    nk, npair = k // tk, len(pairs)

    def kern(*refs):
        a_refs, b_refs = refs[:npair], refs[npair:2 * npair]
        h_ref, g_ref, dres_ref, o_ref, dg_ref, acc_ref = refs[2 * npair:]
        i, kk = pl.program_id(0), pl.program_id(1)

        @pl.when(jnp.logical_and(i == 0, kk == 0))
        def _():
            dg_ref[...] = jnp.zeros_like(dg_ref)

        @pl.when(kk == 0)
        def _():
            acc_ref[...] = jnp.zeros_like(acc_ref)

        acc_ref[...] += _pair_sum(a_refs, b_refs, True)

        @pl.when(kk == nk - 1)
        def _():
            dn = acc_ref[...]
            x = h_ref[...]
            rstd = _rstd(x)
            xhat = x * rstd
            dg_ref[...] += jnp.broadcast_to(jnp.sum(dn * xhat, axis=0, keepdims=True), dg_ref.shape)
            dxh = dn * g_ref[...]
            dx = rstd * (dxh - xhat * jnp.mean(dxh * xhat, axis=-1, keepdims=True))
            o_ref[...] = dres_ref[...] + dx

    row = pl.BlockSpec((tm, d), lambda i, kk: (i, 0))
    in_specs = ([pl.BlockSpec((tm, tk), lambda i, kk: (i, kk))] * npair
                + [pl.BlockSpec((d, tk), lambda i, kk: (0, kk))] * npair
                + [row, pl.BlockSpec((1, d), lambda i, kk: (0, 0)), row])
    return pl.pallas_call(
        kern, name=name, grid=(m // tm, nk), in_specs=in_specs,
        out_specs=[row, pl.BlockSpec((8, d), lambda i, kk: (0, 0))],
        out_shape=[jax.ShapeDtypeStruct((m, d), F32), jax.ShapeDtypeStruct((8, d), F32)],
        scratch_shapes=[pltpu.VMEM((tm, d), F32)],
        compiler_params=_params("arbitrary", "arbitrary"))(
            *[p[0] for p in pairs], *[p[1] for p in pairs], h, g.reshape(1, d), dres)


def _mm_tn(a, b, out_dtype, name, *, bm=None, bn=None, tk=512):
    t, m = a.shape
    batched = b.ndim == 3
    n = b.shape[-1]
    nb = b.shape[0] if batched else 1
    bm, bn, tk = bm or m, bn or n, min(tk, t)
    nk = t // tk

    def kern(a_ref, b_ref, o_ref, acc_ref):
        kk = pl.program_id(3)

        @pl.when(kk == 0)
        def _():
            acc_ref[...] = jnp.zeros_like(acc_ref)

        acc_ref[...] += lax.dot_general(_bf(a_ref[...]), _bf(b_ref[...]), (((0,), (0,)), ((), ())),
                                        preferred_element_type=F32)

        @pl.when(kk == nk - 1)
        def _():
            o_ref[...] = acc_ref[...].astype(o_ref.dtype)

    a_spec = pl.BlockSpec((tk, bm), lambda h, i, j, kk: (kk, i))
    if batched:
        b_spec = pl.BlockSpec((None, tk, bn), lambda h, i, j, kk: (h, kk, j))
        o_spec = pl.BlockSpec((None, bm, bn), lambda h, i, j, kk: (h, i, j))
        out_shape = jax.ShapeDtypeStruct((nb, m, n), out_dtype)
    else:
        b_spec = pl.BlockSpec((tk, bn), lambda h, i, j, kk: (kk, j))
        o_spec = pl.BlockSpec((bm, bn), lambda h, i, j, kk: (i, j))
        out_shape = jax.ShapeDtypeStruct((m, n), out_dtype)
    return pl.pallas_call(
        kern, name=name, grid=(nb, m // bm, n // bn, nk), in_specs=[a_spec, b_spec], out_specs=o_spec,
        out_shape=out_shape, scratch_shapes=[pltpu.VMEM((bm, bn), F32)],
        compiler_params=_params("parallel", "parallel", "parallel", "arbitrary"))(a, b)


def _ffn_tile(f):
    return f // 2 if (f // 2) % 128 == 0 else f


def _ffn_fwd(h, g, w1, w3, w2, name):
    t, d = h.shape
    f = w1.shape[1]
    tm, tf = min(512, t), _ffn_tile(f)
    nf = f // tf

    def kern(h_ref, g_ref, w1_ref, w3_ref, w2_ref, ho_ref, n_ref, z1_ref, z3_ref, n_sc, acc_ref):
        j = pl.program_id(1)

        @pl.when(j == 0)
        def _():
            x = h_ref[...]
            n = _bf(x * _rstd(x) * g_ref[...])
            n_sc[...] = n
            n_ref[...] = n
            acc_ref[...] = jnp.zeros_like(acc_ref)

        n = n_sc[...]
        z1 = jnp.dot(n, w1_ref[...], preferred_element_type=F32)
        z3 = jnp.dot(n, w3_ref[...], preferred_element_type=F32)
        z1_ref[...] = _bf(z1)
        z3_ref[...] = _bf(z3)
        act = _bf(z1 * _sigmoid(z1) * z3)
        acc_ref[...] += jnp.dot(act, w2_ref[...], preferred_element_type=F32)

        @pl.when(j == nf - 1)
        def _():
            ho_ref[...] = h_ref[...] + FFN_RES * acc_ref[...]

    row = pl.BlockSpec((tm, d), lambda i, j: (i, 0))
    col = pl.BlockSpec((tm, tf), lambda i, j: (i, j))
    return pl.pallas_call(
        kern, name=name, grid=(t // tm, nf),
        in_specs=[row, pl.BlockSpec((1, d), lambda i, j: (0, 0)), pl.BlockSpec((d, tf), lambda i, j: (0, j)),
                  pl.BlockSpec((d, tf), lambda i, j: (0, j)), pl.BlockSpec((tf, d), lambda i, j: (j, 0))],
        out_specs=[row, row, col, col],
        out_shape=[jax.ShapeDtypeStruct((t, d), F32), jax.ShapeDtypeStruct((t, d), BF16),
                   jax.ShapeDtypeStruct((t, f), BF16), jax.ShapeDtypeStruct((t, f), BF16)],
        scratch_shapes=[pltpu.VMEM((tm, d), BF16), pltpu.VMEM((tm, d), F32)],
        compiler_params=_params("parallel", "arbitrary"))(h, g.reshape(1, d), w1, w3, w2)


def _ffn_bwd_act(dh, z1, z3, w2, name):
    t, d = dh.shape
    f = z1.shape[1]
    tm, tf = min(512, t), _ffn_tile(f)

    def kern(dh_ref, z1_ref, z3_ref, w2_ref, dz1_ref, dz3_ref, a_ref, df_ref, df_sc):
        @pl.when(pl.program_id(1) == 0)
        def _():
            df = _bf(FFN_RES * dh_ref[...])
            df_sc[...] = df
            df_ref[...] = df

        da = lax.dot_general(df_sc[...], w2_ref[...], (((1,), (1,)), ((), ())), preferred_element_type=F32)
        z1v, z3v = z1_ref[...].astype(F32), z3_ref[...].astype(F32)
        sig = _sigmoid(z1v)
        silu = z1v * sig
        a_ref[...] = _bf(silu * z3v)
        dz1_ref[...] = _bf(da * z3v * (sig * (1.0 + z1v * (1.0 - sig))))
        dz3_ref[...] = _bf(da * silu)

    row = pl.BlockSpec((tm, d), lambda i, j: (i, 0))
    col = pl.BlockSpec((tm, tf), lambda i, j: (i, j))
    colshape = jax.ShapeDtypeStruct((t, f), BF16)
    return pl.pallas_call(
        kern, name=name, grid=(t // tm, f // tf),
        in_specs=[row, col, col, pl.BlockSpec((tf, d), lambda i, j: (j, 0))],
        out_specs=[col, col, col, row],
        out_shape=[colshape, colshape, colshape, jax.ShapeDtypeStruct((t, d), BF16)],
        scratch_shapes=[pltpu.VMEM((tm, d), BF16)],
        compiler_params=_params("parallel", "arbitrary"))(dh, z1, z3, w2)


def _ffn_bwd(dh, h_in, g, n, z1, z3, w1, w3, w2, tag):
    f = w1.shape[1]
    dz1, dz3, act, df = _ffn_bwd_act(dh, z1, z3, w2, tag + "_bwd_act")
    dh_in, dg = _mm_normbwd([(dz1, w1), (dz3, w3)], h_in, g, dh, tag + "_bwd_dn", tk=_ffn_tile(f))
    dw1 = _mm_tn(n, dz1, BF16, tag + "_dw1", bn=_ffn_tile(f))
    dw3 = _mm_tn(n, dz3, BF16, tag + "_dw3", bn=_ffn_tile(f))
    dw2 = _mm_tn(act, df, BF16, tag + "_dw2", bm=_ffn_tile(f))
    return dh_in, dg, dw1, dw3, dw2


def _norm_fwd(h, g, name):
    t, d = h.shape
    tm = min(512, t)

    def kern(h_ref, g_ref, o_ref):
        x = h_ref[...]
        o_ref[...] = _bf(x * _rstd(x) * g_ref[...])

    row = pl.BlockSpec((tm, d), lambda i: (i, 0))
    return pl.pallas_call(
        kern, name=name, grid=(t // tm,), in_specs=[row, pl.BlockSpec((1, d), lambda i: (0, 0))], out_specs=row,
        out_shape=jax.ShapeDtypeStruct((t, d), BF16), compiler_params=_params("parallel"))(h, g.reshape(1, d))


def _loss_bwd(h, target, g):
    t, d = h.shape
    tm = min(512, t)

    def kern(h_ref, t_ref, g_ref, dh_ref, dg_ref, loss_ref):
        @pl.when(pl.program_id(0) == 0)
        def _():
            dg_ref[...] = jnp.zeros_like(dg_ref)
            loss_ref[...] = jnp.zeros_like(loss_ref)

        x = h_ref[...]
        rstd = _rstd(x)
        xhat = x * rstd
        err = xhat * g_ref[...] - t_ref[...]
        row_loss = jnp.sum(err * err, axis=-1, keepdims=True) * (0.5 / d)
        loss_ref[...] += jnp.broadcast_to(jnp.sum(row_loss, axis=0, keepdims=True), loss_ref.shape)
        dy = err * (1.0 / d)
        dg_ref[...] += jnp.broadcast_to(jnp.sum(dy * xhat, axis=0, keepdims=True), dg_ref.shape)
        dxh = dy * g_ref[...]
        dh_ref[...] = rstd * (dxh - xhat * jnp.mean(dxh * xhat, axis=-1, keepdims=True))

    row = pl.BlockSpec((tm, d), lambda i: (i, 0))
    return pl.pallas_call(
        kern, name="loss_bwd", grid=(t // tm,),
        in_specs=[row, row, pl.BlockSpec((1, d), lambda i: (0, 0))],
        out_specs=[row, pl.BlockSpec((8, d), lambda i: (0, 0)), pl.BlockSpec((8, 128), lambda i: (0, 0))],
        out_shape=[jax.ShapeDtypeStruct((t, d), F32), jax.ShapeDtypeStruct((8, d), F32),
                   jax.ShapeDtypeStruct((8, 128), F32)],
        compiler_params=_params("arbitrary"))(h, target, g.reshape(1, d))


def _glu_fwd(m, wa, wb):
    t, d = m.shape
    c = wa.shape[1]
    tm, tc = min(512, t), min(512, c)

    def kern(m_ref, wa_ref, wb_ref, a_ref, b_ref, glu_ref):
        mv = m_ref[...]
        a = jnp.dot(mv, wa_ref[...], preferred_element_type=F32)
        b = jnp.dot(mv, wb_ref[...], preferred_element_type=F32)
        a_ref[...] = _bf(a)
        b_ref[...] = _bf(b)
        glu_ref[...] = _bf(a * _sigmoid(b))

    col = pl.BlockSpec((tm, tc), lambda i, j: (i, j))
    wspec = pl.BlockSpec((d, tc), lambda i, j: (0, j))
    shape = jax.ShapeDtypeStruct((t, c), BF16)
    return pl.pallas_call(
        kern, name="conv_glu_fwd", grid=(t // tm, c // tc),
        in_specs=[pl.BlockSpec((tm, d), lambda i, j: (i, 0)), wspec, wspec], out_specs=[col, col, col],
        out_shape=[shape, shape, shape], compiler_params=_params("parallel", "parallel"))(m, wa, wb)


def _conv_tile(t):
    return min(256, t)


def _conv_fwd(glu, w_dw, g):
    t, c = glu.shape
    tm = _conv_tile(t)
    hb = tm // CONV_HALO

    def kern(cur_ref, halo_ref, w_ref, g_ref, cv_ref, s_ref, ext):
        i = pl.program_id(0)
        ext[0:CONV_HALO, :] = jnp.where(i > 0, halo_ref[...].astype(F32), 0.0)
        ext[CONV_HALO:, :] = cur_ref[...].astype(F32)
        gv = g_ref[...]
        for r0 in range(0, tm, CONV_ROWS):
            acc = jnp.zeros((CONV_ROWS, c), F32)
            for k in range(CONV_WIDTH):
                acc = acc + ext[pl.ds(r0 + 2 + k, CONV_ROWS), :] * w_ref[k:k + 1, :]
            cv_ref[r0:r0 + CONV_ROWS, :] = acc
            rn = acc * _rstd(acc) * gv
            s_ref[r0:r0 + CONV_ROWS, :] = _bf(rn * _sigmoid(rn))

    row = pl.BlockSpec((tm, c), lambda i: (i, 0))
    return pl.pallas_call(
        kern, name="conv_fwd", grid=(t // tm,),
        in_specs=[row, pl.BlockSpec((CONV_HALO, c), lambda i: (jnp.maximum(i * hb - 1, 0), 0)),
                  pl.BlockSpec((CONV_HALO, c), lambda i: (0, 0)), pl.BlockSpec((1, c), lambda i: (0, 0))],
        out_specs=[row, row],
        out_shape=[jax.ShapeDtypeStruct((t, c), F32), jax.ShapeDtypeStruct((t, c), BF16)],
        scratch_shapes=[pltpu.VMEM((tm + CONV_HALO, c), F32)],
        compiler_params=_params("parallel"))(glu, glu, w_dw, g.reshape(1, c))


def _conv_bwd_norm(dh, cv, w_pw2, g):
    t, c = cv.shape
    tm = min(512, t)

    def kern(dh_ref, cv_ref, w_ref, g_ref, dcv_ref, dg_ref):
        @pl.when(pl.program_id(0) == 0)
        def _():
            dg_ref[...] = jnp.zeros_like(dg_ref)

        ds = lax.dot_general(_bf(dh_ref[...]), w_ref[...], (((1,), (1,)), ((), ())), preferred_element_type=F32)
        x = cv_ref[...]
        rstd = _rstd(x)
        xhat = x * rstd
        rn = xhat * g_ref[...]
        sig = _sigmoid(rn)
        drn = ds * (sig * (1.0 + rn * (1.0 - sig)))
        dg_ref[...] += jnp.broadcast_to(jnp.sum(drn * xhat, axis=0, keepdims=True), dg_ref.shape)
        dxh = drn * g_ref[...]
        dcv_ref[...] = rstd * (dxh - xhat * jnp.mean(dxh * xhat, axis=-1, keepdims=True))

    row = pl.BlockSpec((tm, c), lambda i: (i, 0))
    return pl.pallas_call(
        kern, name="conv_bwd_norm", grid=(t // tm,),
        in_specs=[pl.BlockSpec((tm, dh.shape[1]), lambda i: (i, 0)), row,
                  pl.BlockSpec(w_pw2.shape, lambda i: (0, 0)), pl.BlockSpec((1, c), lambda i: (0, 0))],
        out_specs=[row, pl.BlockSpec((8, c), lambda i: (0, 0))],
        out_shape=[jax.ShapeDtypeStruct((t, c), F32), jax.ShapeDtypeStruct((8, c), F32)],
        compiler_params=_params("arbitrary"))(dh, cv, w_pw2, g.reshape(1, c))


def _conv_bwd_dw(dcv, glu, a, b, w_dw):
    t, c = dcv.shape
    tm = _conv_tile(t)
    hb = tm // CONV_HALO
    last = t // CONV_HALO - 1

    def kern(dcv_ref, dnext_ref, glu_ref, gprev_ref, a_ref, b_ref, w_ref, da_ref, db_ref, dw_ref, dext, gext):
        i = pl.program_id(0)

        @pl.when(i == 0)
        def _():
            dw_ref[...] = jnp.zeros_like(dw_ref)

        dext[0:tm, :] = dcv_ref[...]
        dext[tm:, :] = jnp.where(i < t // tm - 1, dnext_ref[...], 0.0)
        gext[0:CONV_HALO, :] = jnp.where(i > 0, gprev_ref[...].astype(F32), 0.0)
        gext[CONV_HALO:, :] = glu_ref[...].astype(F32)
        for r0 in range(0, tm, CONV_ROWS):
            acc = jnp.zeros((CONV_ROWS, c), F32)
            for k in range(CONV_WIDTH):
                acc = acc + dext[pl.ds(r0 + CONV_WIDTH - 1 - k, CONV_ROWS), :] * w_ref[k:k + 1, :]
            av = a_ref[r0:r0 + CONV_ROWS, :].astype(F32)
            sig = _sigmoid(b_ref[r0:r0 + CONV_ROWS, :].astype(F32))
            da_ref[r0:r0 + CONV_ROWS, :] = _bf(acc * sig)
            db_ref[r0:r0 + CONV_ROWS, :] = _bf(acc * av * sig * (1.0 - sig))
        for k in range(CONV_WIDTH):
            acc = jnp.zeros((CONV_ROWS, c), F32)
            for r0 in range(0, tm, CONV_ROWS):
                acc = acc + gext[pl.ds(r0 + 2 + k, CONV_ROWS), :] * dext[r0:r0 + CONV_ROWS, :]
            dw_ref[k:k + 1, :] += jnp.sum(acc, axis=0, keepdims=True)

    row = pl.BlockSpec((tm, c), lambda i: (i, 0))
    shape = jax.ShapeDtypeStruct((t, c), BF16)
    return pl.pallas_call(
        kern, name="conv_bwd_dw", grid=(t // tm,),
        in_specs=[row, pl.BlockSpec((CONV_HALO, c), lambda i: (jnp.minimum((i + 1) * hb, last), 0)),
                  row, pl.BlockSpec((CONV_HALO, c), lambda i: (jnp.maximum(i * hb - 1, 0), 0)),
                  row, row, pl.BlockSpec((CONV_HALO, c), lambda i: (0, 0))],
        out_specs=[row, row, pl.BlockSpec((CONV_HALO, c), lambda i: (0, 0))],
        out_shape=[shape, shape, jax.ShapeDtypeStruct((CONV_HALO, c), F32)],
        scratch_shapes=[pltpu.VMEM((tm + CONV_HALO, c), F32), pltpu.VMEM((tm + CONV_HALO, c), F32)],
        compiler_params=_params("arbitrary"))(dcv, dcv, glu, glu, a, b, w_dw)


def _rope_tables(pos):
    t = pos.shape[0]
    tm = min(512, t)
    freq = (np.float32(ROPE_THETA) ** (np.float32(-2.0) * np.arange(ROPE // 2, dtype=np.float32)
                                       / np.float32(ROPE))).astype(np.float32)
    row = np.zeros((2, 128), np.float32)
    row[0, :ROPE] = np.concatenate([freq, freq])
    row[1, :ROPE] = 1.0

    def kern(pos_ref, f_ref, c_ref, s_ref):
        ang = pos_ref[...].astype(F32) * f_ref[0:1, :]
        mask = f_ref[1:2, :]
        c_ref[...] = jnp.cos(ang) * mask
        s_ref[...] = jnp.sin(ang) * mask

    out = pl.BlockSpec((tm, 128), lambda i: (i, 0))
    shape = jax.ShapeDtypeStruct((t, 128), F32)
    return pl.pallas_call(
        kern, name="rope_tables", grid=(t // tm,),
        in_specs=[pl.BlockSpec((tm, 1), lambda i: (i, 0)), pl.BlockSpec((2, 128), lambda i: (0, 0))],
        out_specs=[out, out], out_shape=[shape, shape], compiler_params=_params("parallel"))(pos, jnp.asarray(row))


def _mla_prep(a, gq, gkv, cs_c, cs_s):
    t = a.shape[0]
    tm = min(512, t)
    kv0, r0 = Q_LORA, Q_LORA + KV_LORA

    def kern(a_ref, gq_ref, gkv_ref, c_ref, s_ref, cq_ref, ckv_ref, kr_ref):
        aq = a_ref[:, 0:kv0]
        akv = a_ref[:, kv0:r0]
        ar = a_ref[:, r0:A_PAD]
        cq_ref[...] = _bf(aq * _rstd(aq) * gq_ref[...])
        ckv_ref[...] = _bf(akv * _rstd(akv) * gkv_ref[...])
        kr_ref[...] = _bf(ar * c_ref[...] + _rot(ar) * s_ref[...])

    def row(w):
        return pl.BlockSpec((tm, w), lambda i: (i, 0))

    def vec(w):
        return pl.BlockSpec((1, w), lambda i: (0, 0))

    return pl.pallas_call(
        kern, name="mla_prep", grid=(t // tm,),
        in_specs=[row(A_PAD), vec(Q_LORA), vec(KV_LORA), row(128), row(128)],
        out_specs=[row(Q_LORA), row(KV_LORA), row(128)],
        out_shape=[jax.ShapeDtypeStruct((t, Q_LORA), BF16), jax.ShapeDtypeStruct((t, KV_LORA), BF16),
                   jax.ShapeDtypeStruct((t, 128), BF16)],
        compiler_params=_params("parallel"))(a, gq.reshape(1, -1), gkv.reshape(1, -1), cs_c, cs_s)


def _mla_prep_bwd(a, dcq, dckv, dar, gq, gkv):
    t = a.shape[0]
    tm = min(512, t)
    kv0, r0 = Q_LORA, Q_LORA + KV_LORA

    def kern(a_ref, dcq_ref, dckv_ref, dar_ref, gq_ref, gkv_ref, da_ref, dgq_ref, dgkv_ref):
        @pl.when(pl.program_id(0) == 0)
        def _():
            dgq_ref[...] = jnp.zeros_like(dgq_ref)
            dgkv_ref[...] = jnp.zeros_like(dgkv_ref)

        def back(x, dy, g_ref, dg_ref):
            rstd = _rstd(x)
            xhat = x * rstd
            dg_ref[...] += jnp.broadcast_to(jnp.sum(dy * xhat, axis=0, keepdims=True), dg_ref.shape)
            dxh = dy * g_ref[...]
            return rstd * (dxh - xhat * jnp.mean(dxh * xhat, axis=-1, keepdims=True))

        da_ref[:, 0:kv0] = _bf(back(a_ref[:, 0:kv0], dcq_ref[...], gq_ref, dgq_ref))
        da_ref[:, kv0:r0] = _bf(back(a_ref[:, kv0:r0], dckv_ref[...], gkv_ref, dgkv_ref))
        da_ref[:, r0:A_PAD] = _bf(dar_ref[...])

    def row(w):
        return pl.BlockSpec((tm, w), lambda i: (i, 0))

    def vec(r, w):
        return pl.BlockSpec((r, w), lambda i: (0, 0))

    return pl.pallas_call(
        kern, name="mla_prep_bwd", grid=(t // tm,),
        in_specs=[row(A_PAD), row(Q_LORA), row(KV_LORA), row(128), vec(1, Q_LORA), vec(1, KV_LORA)],
        out_specs=[row(A_PAD), vec(8, Q_LORA), vec(8, KV_LORA)],
        out_shape=[jax.ShapeDtypeStruct((t, A_PAD), BF16), jax.ShapeDtypeStruct((8, Q_LORA), F32),
                   jax.ShapeDtypeStruct((8, KV_LORA), F32)],
        compiler_params=_params("arbitrary"))(a, dcq, dckv, dar, gq.reshape(1, -1), gkv.reshape(1, -1))


def _mla_qkv(cq, ckv, kr, cs_c, cs_s, wuq, wukv):
    t = cq.shape[0]
    tm = min(512, t)

    def kern(cq_ref, ckv_ref, kr_ref, c_ref, s_ref, wq_ref, wkv_ref, q_ref, k_ref, v_ref):
        r = jnp.dot(cq_ref[...], wq_ref[...], preferred_element_type=F32)
        xr = r[:, NOPE:]
        q_ref[:, 0:NOPE] = _bf(r[:, 0:NOPE] * ATTN_SCALE)
        q_ref[:, NOPE:] = _bf((xr * c_ref[...] + _rot(xr) * s_ref[...]) * ATTN_SCALE)
        kv = jnp.dot(ckv_ref[...], wkv_ref[...], preferred_element_type=F32)
        k_ref[:, 0:NOPE] = _bf(kv[:, 0:NOPE])
        k_ref[:, NOPE:] = kr_ref[...]
        v_ref[...] = _bf(kv[:, NOPE:])

    def row(w):
        return pl.BlockSpec((tm, w), lambda i, h: (i, 0))

    def head(w):
        return pl.BlockSpec((None, tm, w), lambda i, h: (h, i, 0))

    return pl.pallas_call(
        kern, name="mla_qkv", grid=(t // tm, HEADS),
        in_specs=[row(Q_LORA), row(KV_LORA), row(128), row(128), row(128),
                  pl.BlockSpec((None, Q_LORA, HEAD_PAD), lambda i, h: (h, 0, 0)),
                  pl.BlockSpec((None, KV_LORA, NOPE + V_HEAD), lambda i, h: (h, 0, 0))],
        out_specs=[head(HEAD_PAD), head(HEAD_PAD), head(V_HEAD)],
        out_shape=[jax.ShapeDtypeStruct((HEADS, t, HEAD_PAD), BF16), jax.ShapeDtypeStruct((HEADS, t, HEAD_PAD), BF16),
                   jax.ShapeDtypeStruct((HEADS, t, V_HEAD), BF16)],
        compiler_params=_params("parallel", "arbitrary"))(cq, ckv, kr, cs_c, cs_s, wuq, wukv)


def _mla_qkv_bwd(dq, dk, dv, cs_c, cs_s, wuq, wukv):
    t = dq.shape[1]
    tm = min(512, t)

    def kern(dq_ref, dk_ref, dv_ref, c_ref, s_ref, wq_ref, wkv_ref, dr_ref, dkv_ref, dcq_ref, dckv_ref, dar_ref):
        @pl.when(pl.program_id(1) == 0)
        def _():
            dcq_ref[...] = jnp.zeros_like(dcq_ref)
            dckv_ref[...] = jnp.zeros_like(dckv_ref)
            dar_ref[...] = jnp.zeros_like(dar_ref)

        cv, sv = c_ref[...], s_ref[...]
        dqx = dq_ref[:, NOPE:]
        dr_ref[:, 0:NOPE] = _bf(dq_ref[:, 0:NOPE] * ATTN_SCALE)
        dr_ref[:, NOPE:] = _bf((dqx * cv + _rot_t(dqx * sv)) * ATTN_SCALE)
        dcq_ref[...] += lax.dot_general(dr_ref[...], wq_ref[...], (((1,), (1,)), ((), ())),
                                        preferred_element_type=F32)
        dkx = dk_ref[:, NOPE:]
        dar_ref[...] += dkx * cv + _rot_t(dkx * sv)
        dkv_ref[:, 0:NOPE] = _bf(dk_ref[:, 0:NOPE])
        dkv_ref[:, NOPE:] = _bf(dv_ref[...])
        dckv_ref[...] += lax.dot_general(dkv_ref[...], wkv_ref[...], (((1,), (1,)), ((), ())),
                                         preferred_element_type=F32)

    def row(w):
        return pl.BlockSpec((tm, w), lambda i, h: (i, 0))

    def head(w):
        return pl.BlockSpec((None, tm, w), lambda i, h: (h, i, 0))

    return pl.pallas_call(
        kern, name="mla_qkv_bwd", grid=(t // tm, HEADS),
        in_specs=[head(HEAD_PAD), head(HEAD_PAD), head(V_HEAD), row(128), row(128),
                  pl.BlockSpec((None, Q_LORA, HEAD_PAD), lambda i, h: (h, 0, 0)),
                  pl.BlockSpec((None, KV_LORA, NOPE + V_HEAD), lambda i, h: (h, 0, 0))],
        out_specs=[head(HEAD_PAD), head(NOPE + V_HEAD), row(Q_LORA), row(KV_LORA), row(128)],
        out_shape=[jax.ShapeDtypeStruct((HEADS, t, HEAD_PAD), BF16),
                   jax.ShapeDtypeStruct((HEADS, t, NOPE + V_HEAD), BF16),
                   jax.ShapeDtypeStruct((t, Q_LORA), F32), jax.ShapeDtypeStruct((t, KV_LORA), F32),
                   jax.ShapeDtypeStruct((t, 128), F32)],
        compiler_params=_params("parallel", "arbitrary"))(dq, dk, dv, cs_c, cs_s, wuq, wukv)


def _attn_block(t):
    return 512 if t >= 4096 else 128


def _chunk_mask(bk, bq):
    kc = lax.broadcasted_iota(jnp.int32, (bk, bq), 0) // CHUNK
    qc = lax.broadcasted_iota(jnp.int32, (bk, bq), 1) // CHUNK
    return qc >= kc


def _flash_fwd(q, k, v):
    t = q.shape[1]
    bq = _attn_block(t)
    nq = t // bq

    def kern(q_ref, k_ref, v_ref, o_ref, lse_ref):
        i = pl.program_id(1)
        qv = q_ref[...]

        def scores(j):
            kj = k_ref[pl.ds(pl.multiple_of(j * bq, bq), bq), :]
            return lax.dot_general(kj, qv, (((1,), (1,)), ((), ())), preferred_element_type=F32)

        def update(j, st, m, l, acc):
            m_new = jnp.maximum(m, jnp.max(st, axis=0, keepdims=True))
            alpha = jnp.exp(m - m_new)
            p = jnp.exp(st - m_new)
            vj = v_ref[pl.ds(pl.multiple_of(j * bq, bq), bq), :]
            pv = lax.dot_general(vj, _bf(p), (((0,), (0,)), ((), ())), preferred_element_type=F32)
            return m_new, alpha * l + jnp.sum(p, axis=0, keepdims=True), alpha * acc + pv

        st = jnp.where(_chunk_mask(bq, bq), scores(i), -1e30)
        carry = update(i, st, jnp.full((1, bq), -1e30, F32), jnp.zeros((1, bq), F32), jnp.zeros((V_HEAD, bq), F32))
        m, l, acc = lax.fori_loop(0, i, lambda j, c: update(j, scores(j), *c), carry)
        o_ref[...] = _bf((acc / l).T)
        lse_ref[...] = jnp.broadcast_to(m + jnp.log(l), (8, bq))

    return pl.pallas_call(
        kern, name="flash_fwd", grid=(HEADS, nq),
        in_specs=[pl.BlockSpec((None, bq, HEAD_PAD), lambda h, i: (h, i, 0)),
                  pl.BlockSpec((None, t, HEAD_PAD), lambda h, i: (h, 0, 0)),
                  pl.BlockSpec((None, t, V_HEAD), lambda h, i: (h, 0, 0))],
        out_specs=[pl.BlockSpec((bq, V_HEAD), lambda h, i: (i, h)),
                   pl.BlockSpec((None, None, 8, bq), lambda h, i: (h, i, 0, 0))],
        out_shape=[jax.ShapeDtypeStruct((t, HEADS * V_HEAD), BF16), jax.ShapeDtypeStruct((HEADS, nq, 8, bq), F32)],
        compiler_params=_params("parallel", "arbitrary"))(q, k, v)


def _attn_delta(do, o):
    t = do.shape[0]
    bq = _attn_block(t)

    def kern(do_ref, o_ref, d_ref):
        prod = do_ref[...].astype(F32) * o_ref[...].astype(F32)
        d_ref[...] = jnp.broadcast_to(jnp.sum(prod.T, axis=0, keepdims=True), (8, bq))

    blk = pl.BlockSpec((bq, V_HEAD), lambda h, i: (i, h))
    return pl.pallas_call(
        kern, name="attn_delta", grid=(HEADS, t // bq), in_specs=[blk, blk],
        out_specs=pl.BlockSpec((None, None, 8, bq), lambda h, i: (h, i, 0, 0)),
        out_shape=jax.ShapeDtypeStruct((HEADS, t // bq, 8, bq), F32),
        compiler_params=_params("parallel", "parallel"))(do, o)


def _flash_bwd(q, k, v, do, lse, delta):
    t = q.shape[1]
    bq = _attn_block(t)
    nq = t // bq

    def kern(q_ref, k_ref, v_ref, do_ref, lse_ref, del_ref, dq_ref, dk_ref, dv_ref):
        j = pl.program_id(1)

        @pl.when(j == 0)
        def _():
            dq_ref[...] = jnp.zeros_like(dq_ref)

        dk_ref[...] = jnp.zeros_like(dk_ref)
        dv_ref[...] = jnp.zeros_like(dv_ref)
        kj, vj = k_ref[...], v_ref[...]

        def step(i, masked):
            rows = pl.ds(pl.multiple_of(i * bq, bq), bq)
            qi, doi = q_ref[rows, :], do_ref[rows, :]
            st = lax.dot_general(kj, qi, (((1,), (1,)), ((), ())), preferred_element_type=F32)
            pt = jnp.exp(st - lse_ref[i][0:1, :])
            if masked:
                pt = jnp.where(_chunk_mask(bq, bq), pt, 0.0)
            dpt = lax.dot_general(vj, doi, (((1,), (1,)), ((), ())), preferred_element_type=F32)
            dst = _bf(pt * (dpt - del_ref[i][0:1, :]))
            dv_ref[...] += jnp.dot(_bf(pt), doi, preferred_element_type=F32)
            dk_ref[...] += jnp.dot(dst, qi, preferred_element_type=F32)
            dq_ref[rows, :] += lax.dot_general(dst, kj, (((0,), (0,)), ((), ())), preferred_element_type=F32)

        step(j, True)

        def body(i, carry):
            step(i, False)
            return carry

        lax.fori_loop(j + 1, nq, body, 0)

    stat = pl.BlockSpec((None, nq, 8, bq), lambda h, j: (h, 0, 0, 0))
    return pl.pallas_call(
        kern, name="flash_bwd", grid=(HEADS, nq),
        in_specs=[pl.BlockSpec((None, t, HEAD_PAD), lambda h, j: (h, 0, 0)),
                  pl.BlockSpec((None, bq, HEAD_PAD), lambda h, j: (h, j, 0)),
                  pl.BlockSpec((None, bq, V_HEAD), lambda h, j: (h, j, 0)),
                  pl.BlockSpec((t, V_HEAD), lambda h, j: (0, h)), stat, stat],
        out_specs=[pl.BlockSpec((None, t, HEAD_PAD), lambda h, j: (h, 0, 0)),
                   pl.BlockSpec((None, bq, HEAD_PAD), lambda h, j: (h, j, 0)),
                   pl.BlockSpec((None, bq, V_HEAD), lambda h, j: (h, j, 0))],
        out_shape=[jax.ShapeDtypeStruct((HEADS, t, HEAD_PAD), F32), jax.ShapeDtypeStruct((HEADS, t, HEAD_PAD), F32),
                   jax.ShapeDtypeStruct((HEADS, t, V_HEAD), F32)],
        compiler_params=_params("parallel", "arbitrary"))(q, k, v, do, lse, delta)


def _place():
    x, y, c = lax.axis_index("x"), lax.axis_index("y"), lax.axis_index("c")
    return x, y, c, [(1 - x, y), (x, 1 - y), (1 - x, 1 - y)]


def _all_gather_rows(block, name):
    m_per, n = block.shape

    def body(x_ref, out_ref, send_sems, recv_sems, local_sem):
        x, y, c, chips = _place()
        me, sibling = (x, y, c), (x, y, 1 - c)

        def rows(px, py, pc):
            return out_ref.at[pl.ds((4 * px + 2 * py + pc) * m_per, m_per), :]

        def copy(k, blk, to, src=None):
            return pltpu.make_async_remote_copy(
                src_ref=rows(*blk) if src is None else src, dst_ref=rows(*blk), send_sem=send_sems.at[k],
                recv_sem=recv_sems.at[k], device_id=to, device_id_type=MESH)

        mine = pltpu.make_async_copy(x_ref, rows(*me), local_sem)
        mine.start()
        first = [copy(0, me, sibling, src=x_ref)]
        first += [copy(1 + j, me, (*chip, c), src=x_ref) for j, chip in enumerate(chips)]
        for cp in first:
            cp.start()
        passed = [copy(4 + j, (*chip, c), sibling) for j, chip in enumerate(chips)]
        for j, chip in enumerate(chips):
            copy(1 + j, (*chip, c), me).wait_recv()
            passed[j].start()
        copy(0, sibling, me).wait_recv()
        for j, chip in enumerate(chips):
            copy(4 + j, (*chip, 1 - c), me).wait_recv()
        for cp in first + passed:
            cp.wait_send()
        mine.wait()

    return pl.pallas_call(
        body, name=name, out_shape=jax.ShapeDtypeStruct((8 * m_per, n), block.dtype),
        in_specs=[pl.BlockSpec(memory_space=pltpu.VMEM)], out_specs=pl.BlockSpec(memory_space=pltpu.VMEM),
        scratch_shapes=[pltpu.SemaphoreType.DMA((7,)), pltpu.SemaphoreType.DMA((7,)), pltpu.SemaphoreType.DMA],
        compiler_params=pltpu.CompilerParams(vmem_limit_bytes=VMEM_LIMIT_BYTES))(block)


def _gather_shards(shards):
    n = len(shards)

    def body(*refs):
        ins, outs = refs[:n], refs[n:2 * n]
        send_sems, recv_sems, local_sems = refs[2 * n:]
        x, y, c, chips = _place()
        me = 2 * x + y
        copies = []
        for w in range(n):
            cp = pltpu.make_async_copy(ins[w], outs[w].at[me], local_sems.at[w])
            cp.start()
            copies.append(cp)
            for p, chip in enumerate(chips):
                cp = pltpu.make_async_remote_copy(
                    src_ref=ins[w], dst_ref=outs[w].at[me], send_sem=send_sems.at[p * n + w],
                    recv_sem=recv_sems.at[p * n + w], device_id=(*chip, c), device_id_type=MESH)
                cp.start()
                copies.append(cp)
        for cp in copies:
            cp.wait()

    any_spec = pl.BlockSpec(memory_space=pl.ANY)
    return pl.pallas_call(
        body, name="gather_shards", out_shape=[jax.ShapeDtypeStruct((4, *s.shape), s.dtype) for s in shards],
        in_specs=[any_spec] * n, out_specs=[any_spec] * n,
        scratch_shapes=[pltpu.SemaphoreType.DMA((3 * n,)), pltpu.SemaphoreType.DMA((3 * n,)),
                        pltpu.SemaphoreType.DMA((n,))])(*shards)


def _scatter_grads(groups):
    flat = [(w, l, g) for w, layers in enumerate(groups) for l, g in enumerate(layers)]
    n = len(flat)

    def body(*refs):
        ins, outs = refs[:n], refs[n:n + len(groups)]
        send_sems, recv_sems, local_sems = refs[n + len(groups):]
        x, y, c, chips = _place()
        me = 2 * x + y
        copies = []
        for idx, (w, l, _) in enumerate(flat):
            cp = pltpu.make_async_copy(ins[idx].at[me], outs[w].at[me, l], local_sems.at[idx])
            cp.start()
            copies.append(cp)
            for p, chip in enumerate(chips):
                cp = pltpu.make_async_remote_copy(
                    src_ref=ins[idx].at[2 * chip[0] + chip[1]], dst_ref=outs[w].at[me, l],
                    send_sem=send_sems.at[p * n + idx], recv_sem=recv_sems.at[p * n + idx],
                    device_id=(*chip, c), device_id_type=MESH)
                cp.start()
                copies.append(cp)
        for cp in copies:
            cp.wait()

    any_spec = pl.BlockSpec(memory_space=pl.ANY)
    out_shape = [jax.ShapeDtypeStruct((4, len(layers), *layers[0].shape[1:]), layers[0].dtype) for layers in groups]
    return pl.pallas_call(
        body, name="scatter_grads", out_shape=out_shape, in_specs=[any_spec] * n, out_specs=[any_spec] * len(groups),
        scratch_shapes=[pltpu.SemaphoreType.DMA((3 * n,)), pltpu.SemaphoreType.DMA((3 * n,)),
                        pltpu.SemaphoreType.DMA((n,))])(*[g for _, _, g in flat])


def _swap_with_sibling(arrays):
    n = len(arrays)

    def body(*refs):
        ins, outs = refs[:n], refs[n:2 * n]
        send_sems, recv_sems = refs[2 * n:]
        x, y, c, _ = _place()
        copies = [pltpu.make_async_remote_copy(src_ref=ins[w], dst_ref=outs[w], send_sem=send_sems.at[w],
                                               recv_sem=recv_sems.at[w], device_id=(x, y, 1 - c), device_id_type=MESH)
                  for w in range(n)]
        for cp in copies:
            cp.start()
        for cp in copies:
            cp.wait()

    any_spec = pl.BlockSpec(memory_space=pl.ANY)
    return pl.pallas_call(
        body, name="swap_with_sibling", out_shape=[jax.ShapeDtypeStruct(a.shape, a.dtype) for a in arrays],
        in_specs=[any_spec] * n, out_specs=[any_spec] * n,
        scratch_shapes=[pltpu.SemaphoreType.DMA((n,)), pltpu.SemaphoreType.DMA((n,))])(*arrays)


def _as_rows(a):
    return a.reshape(-1, a.shape[-1])


def _row_tile(r, c, budget_bytes=1 << 20):
    tr = r
    while tr % 16 == 0 and tr * c * 4 > budget_bytes:
        tr //= 2
    return tr


def _sum_slots(r4, name):
    _, r, c = r4.shape
    tr = _row_tile(r, c)

    def kern(r_ref, o_ref):
        o_ref[...] = ((r_ref[0].astype(F32) + r_ref[1].astype(F32)) + r_ref[2].astype(F32)) + r_ref[3].astype(F32)

    return pl.pallas_call(
        kern, name=name, grid=(r // tr,), in_specs=[pl.BlockSpec((4, tr, c), lambda i: (0, i, 0))],
        out_specs=pl.BlockSpec((tr, c), lambda i: (i, 0)), out_shape=jax.ShapeDtypeStruct((r, c), F32),
        compiler_params=_params("parallel"))(r4)


def _adamw(w, m, v, parts, name):
    r, c = w.shape
    tr = _row_tile(r, c, 1 << 19)
    npart = len(parts)
    c1 = 1.0 - ADAM_B1 ** ADAM_STEP
    c2 = 1.0 - ADAM_B2 ** ADAM_STEP

    def kern(*refs):
        w_ref, m_ref, v_ref = refs[:3]
        p_refs = refs[3:3 + npart]
        g_ref, d_ref, mo_ref, vo_ref = refs[3 + npart:]
        g = p_refs[0][...]
        for p in p_refs[1:]:
            g = g + p[...]
        mn = ADAM_B1 * m_ref[...] + (1.0 - ADAM_B1) * g
        vn = ADAM_B2 * v_ref[...] + (1.0 - ADAM_B2) * (g * g)
        g_ref[...] = g
        mo_ref[...] = mn
        vo_ref[...] = vn
        d_ref[...] = -ADAM_LR * ((mn / c1) / (jnp.sqrt(vn / c2) + ADAM_EPS) + ADAM_WD * w_ref[...])

    blk = pl.BlockSpec((tr, c), lambda i: (i, 0))
    shape = jax.ShapeDtypeStruct((r, c), F32)
    return pl.pallas_call(
        kern, name=name, grid=(r // tr,), in_specs=[blk] * (3 + npart), out_specs=[blk] * 4, out_shape=[shape] * 4,
        compiler_params=_params("parallel"))(w, m, v, *parts)


def _sum_devices(g8, name):
    _, r, c = g8.shape

    def kern(g_ref, o_ref):
        tot = g_ref[0]
        for dev in range(1, 8):
            tot = tot + g_ref[dev]
        o_ref[...] = tot

    return pl.pallas_call(
        kern, name=name, grid=(1,), in_specs=[pl.BlockSpec((8, r, c), lambda i: (0, 0, 0))],
        out_specs=pl.BlockSpec((r, c), lambda i: (0, 0)), out_shape=jax.ShapeDtypeStruct((r, c), F32),
        compiler_params=_params("arbitrary"))(g8)


def _pad_lanes(a, width):
    return jnp.pad(a, [(0, 0)] * (a.ndim - 1) + [(0, width - a.shape[-1])])


def kernel(x, positions, ffn_norm1, ffn1_w1, ffn1_w3, ffn1_w2, mix_norm, ffn_norm2, ffn2_w1, ffn2_w3, ffn2_w2, conv_w_pw1, conv_w_dw, conv_norm, conv_w_pw2, mla_w_a, mla_q_norm, mla_kv_norm, mla_w_uq, mla_w_ukv, mla_w_o, final_norm, loss_target, m_ffn_norm1, m_ffn1_w1, m_ffn1_w3, m_ffn1_w2, m_mix_norm, m_ffn_norm2, m_ffn2_w1, m_ffn2_w3, m_ffn2_w2, m_conv_w_pw1, m_conv_w_dw, m_conv_norm, m_conv_w_pw2, m_mla_w_a, m_mla_q_norm, m_mla_kv_norm, m_mla_w_uq, m_mla_w_ukv, m_mla_w_o, m_final_norm, v_ffn_norm1, v_ffn1_w1, v_ffn1_w3, v_ffn1_w2, v_mix_norm, v_ffn_norm2, v_ffn2_w1, v_ffn2_w3, v_ffn2_w2, v_conv_w_pw1, v_conv_w_dw, v_conv_norm, v_conv_w_pw2, v_mla_w_a, v_mla_q_norm, v_mla_kv_norm, v_mla_w_uq, v_mla_w_ukv, v_mla_w_o, v_final_norm):
    given = locals()
    return _step({nm: given[nm] for nm in INPUTS})


def _step(A):
    x = A['x'][0]
    target = A['loss_target'][0]
    t, d = x.shape
    pos = A['positions'].reshape(t, 1)
    me = 2 * lax.axis_index("x") + lax.axis_index("y")

    big = dict(zip(BIG, _gather_shards([_bf(A[nm]) for nm in BIG])))
    dw_shard = A['conv_w_dw'][0]
    cw = dw_shard.shape[1]
    small = jnp.concatenate([
        jnp.pad(dw_shard, ((0, CONV_HALO - CONV_WIDTH), (0, 0))),
        jnp.pad(_pad_lanes(A['mla_q_norm'], cw), ((0, 7), (0, 0))),
        jnp.pad(_pad_lanes(A['mla_kv_norm'], cw), ((0, 7), (0, 0)))], axis=0)
    small = _all_gather_rows(small, "gather_small_weights").reshape(4, 2, 48, cw)[:, 0]
    w_dw = jnp.concatenate([small[j, :CONV_HALO] for j in range(4)], axis=1)
    gq = jnp.concatenate([small[j, CONV_HALO, :Q_LORA // 4] for j in range(4)])
    gkv = jnp.concatenate([small[j, CONV_HALO + 8, :KV_LORA // 4] for j in range(4)])

    def cols(nm, layer):
        return jnp.concatenate([big[nm][j, layer] for j in range(4)], axis=1)

    def rows(nm, layer):
        g = big[nm][:, layer]
        return g.reshape(-1, g.shape[-1])

    ffn_w = {(k, l): (cols(f'ffn{k}_w1', l), cols(f'ffn{k}_w3', l), rows(f'ffn{k}_w2', l))
             for k in (1, 2) for l in (0, 1)}
    pw1_a = jnp.concatenate([big['conv_w_pw1'][0, 0], big['conv_w_pw1'][1, 0]], axis=1)
    pw1_b = jnp.concatenate([big['conv_w_pw1'][2, 0], big['conv_w_pw1'][3, 0]], axis=1)
    pw2 = rows('conv_w_pw2', 0)
    w_a = _pad_lanes(rows('mla_w_a', 0), A_PAD)
    wuq = _pad_lanes(big['mla_w_uq'][:, 0].reshape(Q_LORA, HEADS, NOPE + ROPE).transpose(1, 0, 2), HEAD_PAD)
    wukv = big['mla_w_ukv'][:, 0].reshape(KV_LORA, HEADS, NOPE + V_HEAD).transpose(1, 0, 2)
    w_o = rows('mla_w_o', 0)

    cs_c, cs_s = _rope_tables(pos)
    h0 = x
    h1, n01, z01a, z01b = _ffn_fwd(h0, A['ffn_norm1'][0], *ffn_w[1, 0], "ffn1_l0_fwd")
    m0 = _norm_fwd(h1, A['mix_norm'][0], "mix_norm_l0")
    ca, cb, glu = _glu_fwd(m0, pw1_a, pw1_b)
    cv, cs = _conv_fwd(glu, w_dw, A['conv_norm'][0])
    h2 = _mm([(cs, pw2)], F32, "conv_pw2_fwd", res=h1)
    h3, n02, z02a, z02b = _ffn_fwd(h2, A['ffn_norm2'][0], *ffn_w[2, 0], "ffn2_l0_fwd")
    h4, n11, z11a, z11b = _ffn_fwd(h3, A['ffn_norm1'][1], *ffn_w[1, 1], "ffn1_l1_fwd")
    m1 = _norm_fwd(h4, A['mix_norm'][1], "mix_norm_l1")
    a_lat = _mm([(m1, w_a)], F32, "mla_down_fwd")
    cq, ckv, kr = _mla_prep(a_lat, gq, gkv, cs_c, cs_s)
    q, k, v = _mla_qkv(cq, ckv, kr, cs_c, cs_s, wuq, wukv)
    o, lse = _flash_fwd(q, k, v)
    h5 = _mm([(o, w_o)], F32, "mla_out_fwd", res=h4)
    h6, n12, z12a, z12b = _ffn_fwd(h5, A['ffn_norm2'][1], *ffn_w[2, 1], "ffn2_l1_fwd")

    dh6, dg_final, loss_part = _loss_bwd(h6, target, A['final_norm'])
    dh5, dg_n2_l1, d1_21, d3_21, d2_21 = _ffn_bwd(dh6, h5, A['ffn_norm2'][1], n12, z12a, z12b, *ffn_w[2, 1], "ffn2_l1")

    do = _mm([(dh5, w_o)], BF16, "mla_out_bwd", trans_b=True)
    dw_o = _mm_tn(o, dh5, BF16, "mla_dw_o")
    delta = _attn_delta(do, o)
    dq, dk, dv = _flash_bwd(q, k, v, do, lse, delta)
    dr, dkv, dcq, dckv, dar = _mla_qkv_bwd(dq, dk, dv, cs_c, cs_s, wuq, wukv)
    dwuq = _mm_tn(cq, dr, BF16, "mla_dw_uq")
    dwukv = _mm_tn(ckv, dkv, BF16, "mla_dw_ukv")
    da_lat, dgq, dgkv = _mla_prep_bwd(a_lat, dcq, dckv, dar, gq, gkv)
    dh4, dg_mix_l1 = _mm_normbwd([(da_lat, w_a)], h4, A['mix_norm'][1], dh5, "mla_down_bwd")
    dw_a = _mm_tn(m1, da_lat, BF16, "mla_dw_a")

    dh3, dg_n1_l1, d1_11, d3_11, d2_11 = _ffn_bwd(dh4, h3, A['ffn_norm1'][1], n11, z11a, z11b, *ffn_w[1, 1], "ffn1_l1")
    dh2, dg_n2_l0, d1_20, d3_20, d2_20 = _ffn_bwd(dh3, h2, A['ffn_norm2'][0], n02, z02a, z02b, *ffn_w[2, 0], "ffn2_l0")

    dcv, dg_conv = _conv_bwd_norm(dh2, cv, pw2, A['conv_norm'][0])
    dw_pw2 = _mm_tn(cs, dh2, BF16, "conv_dw_pw2")
    dca, dcb, ddw = _conv_bwd_dw(dcv, glu, ca, cb, w_dw)
    dh1, dg_mix_l0 = _mm_normbwd([(dca, pw1_a), (dcb, pw1_b)], h1, A['mix_norm'][0], dh2, "conv_pw1_bwd")
    dpw1_a = _mm_tn(m0, dca, BF16, "conv_dw_pw1a")
    dpw1_b = _mm_tn(m0, dcb, BF16, "conv_dw_pw1b")

    dx, dg_n1_l0, d1_10, d3_10, d2_10 = _ffn_bwd(dh1, h0, A['ffn_norm1'][0], n01, z01a, z01b, *ffn_w[1, 0], "ffn1_l0")

    def col_slots(g):
        r, c4 = g.shape
        return g.reshape(r, 4, c4 // 4).transpose(1, 0, 2)

    def row_slots(g):
        return g.reshape(4, g.shape[0] // 4, g.shape[1])

    half = dpw1_a.shape[1] // 2
    groups = {
        'ffn1_w1': [col_slots(d1_10), col_slots(d1_11)], 'ffn1_w3': [col_slots(d3_10), col_slots(d3_11)],
        'ffn1_w2': [row_slots(d2_10), row_slots(d2_11)],
        'ffn2_w1': [col_slots(d1_20), col_slots(d1_21)], 'ffn2_w3': [col_slots(d3_20), col_slots(d3_21)],
        'ffn2_w2': [row_slots(d2_20), row_slots(d2_21)],
        'conv_w_pw1': [jnp.stack([dpw1_a[:, :half], dpw1_a[:, half:], dpw1_b[:, :half], dpw1_b[:, half:]])],
        'conv_w_pw2': [row_slots(dw_pw2)],
        'mla_w_a': [row_slots(dw_a[:, :Q_LORA + KV_LORA + ROPE])],
        'mla_w_uq': [dwuq[:, :, :NOPE + ROPE].transpose(1, 0, 2).reshape(4, Q_LORA // 4, HEADS, NOPE + ROPE)],
        'mla_w_ukv': [dwukv.transpose(1, 0, 2).reshape(4, KV_LORA // 4, HEADS, NOPE + V_HEAD)],
        'mla_w_o': [row_slots(dw_o)],
    }
    received = _scatter_grads([groups[nm] for nm in BIG])
    sums = [_sum_slots(r.reshape(4, -1, r.shape[-1]), "sum_" + nm) for nm, r in zip(BIG, received)]
    sibling = _swap_with_sibling(sums)

    out = {}
    for nm, mine, theirs in zip(BIG, sums, sibling):
        res = _adamw(_as_rows(A[nm]), _as_rows(A['m_' + nm]), _as_rows(A['v_' + nm]), [mine, theirs], "adamw_" + nm)
        out[nm] = [r.reshape(A[nm].shape) for r in res]

    qkv_row = jnp.concatenate([dgq, dgkv, jnp.zeros((8, d - Q_LORA - KV_LORA), F32)], axis=1)
    loss_row = _pad_lanes(loss_part, d)
    small_g = jnp.concatenate([dg_n1_l0, dg_n1_l1, dg_mix_l0, dg_mix_l1, dg_n2_l0, dg_n2_l1, dg_conv, dg_final,
                               qkv_row, loss_row, ddw], axis=0)
    nrow = small_g.shape[0]
    tot = _sum_devices(_all_gather_rows(small_g, "gather_small_grads").reshape(8, nrow, d), "sum_small_grads")
    loss = tot[72, 0]
    q_shard = lax.dynamic_slice_in_dim(tot[64, :Q_LORA], me * (Q_LORA // 4), Q_LORA // 4)
    kv_shard = lax.dynamic_slice_in_dim(tot[64, Q_LORA:Q_LORA + KV_LORA], me * (KV_LORA // 4), KV_LORA // 4)
    dw_shard_g = lax.dynamic_slice_in_dim(tot[80:80 + CONV_WIDTH], me * cw, cw, axis=1)
    small_grads = {
        'ffn_norm1': jnp.stack([tot[0], tot[8]]), 'mix_norm': jnp.stack([tot[16], tot[24]]),
        'ffn_norm2': jnp.stack([tot[32], tot[40]]), 'conv_norm': tot[48][None], 'final_norm': tot[56],
        'mla_q_norm': q_shard[None], 'mla_kv_norm': kv_shard[None], 'conv_w_dw': dw_shard_g[None],
    }
    for nm, g in small_grads.items():
        res = _adamw(_as_rows(A[nm]) if A[nm].ndim > 1 else A[nm].reshape(1, -1),
                     A['m_' + nm].reshape(-1, A[nm].shape[-1]), A['v_' + nm].reshape(-1, A[nm].shape[-1]),
                     [g.reshape(-1, A[nm].shape[-1])], "adamw_" + nm)
        out[nm] = [r.reshape(A[nm].shape) for r in res]

    return (loss, dx[None], *[out[nm][0] for nm in WEIGHTS], *[out[nm][1] for nm in WEIGHTS],
            *[out[nm][2] for nm in WEIGHTS], *[out[nm][3] for nm in WEIGHTS])
```

```python
import functools

import jax
import jax.numpy as jnp
import numpy as np
from jax import lax
from jax.experimental import pallas as pl
from jax.experimental.pallas import tpu as pltpu

F32 = jnp.float32
BF16 = jnp.bfloat16
MESH = pl.DeviceIdType.MESH

RMS_EPS = 1e-6
HEADS = 8
NOPE = 128
ROPE = 64
HEAD_PAD = 256
V_HEAD = 128
Q_LORA = 512
KV_LORA = 256
A_PAD = 896
CHUNK = 64
CONV_WIDTH = 31
CONV_HALO = 32
CONV_ROWS = 16
ROPE_THETA = 10000.0
ATTN_SCALE = (NOPE + ROPE) ** -0.5
FFN_RES = 0.5

ADAM_LR = 0.001
ADAM_B1 = 0.9
ADAM_B2 = 0.999
ADAM_EPS = 1e-08
ADAM_WD = 0.01
ADAM_STEP = 10

VMEM_LIMIT_BYTES = 56 * 1024 * 1024

WEIGHTS = ['ffn_norm1', 'ffn1_w1', 'ffn1_w3', 'ffn1_w2', 'mix_norm', 'ffn_norm2', 'ffn2_w1', 'ffn2_w3', 'ffn2_w2',
           'conv_w_pw1', 'conv_w_dw', 'conv_norm', 'conv_w_pw2', 'mla_w_a', 'mla_q_norm', 'mla_kv_norm', 'mla_w_uq',
           'mla_w_ukv', 'mla_w_o', 'final_norm']
INPUTS = (['x', 'positions'] + WEIGHTS + ['loss_target'] + ['m_' + w for w in WEIGHTS] + ['v_' + w for w in WEIGHTS])
BIG = ['ffn1_w1', 'ffn1_w3', 'ffn1_w2', 'ffn2_w1', 'ffn2_w3', 'ffn2_w2', 'conv_w_pw1', 'conv_w_pw2', 'mla_w_a',
       'mla_w_uq', 'mla_w_ukv', 'mla_w_o']


def _params(*sem):
    return pltpu.CompilerParams(dimension_semantics=sem, vmem_limit_bytes=VMEM_LIMIT_BYTES)


def _bf(v):
    return v.astype(BF16)


def _rstd(x):
    return lax.rsqrt(jnp.mean(x * x, axis=-1, keepdims=True) + RMS_EPS)


def _sigmoid(x):
    return jax.nn.sigmoid(x)


def _rot(x):
    lane = lax.broadcasted_iota(jnp.int32, x.shape, 1)
    return jnp.where(lane < ROPE // 2, -pltpu.roll(x, 128 - ROPE // 2, 1), pltpu.roll(x, ROPE // 2, 1))


def _rot_t(y):
    lane = lax.broadcasted_iota(jnp.int32, y.shape, 1)
    return jnp.where(lane < ROPE // 2, pltpu.roll(y, 128 - ROPE // 2, 1), -pltpu.roll(y, ROPE // 2, 1))


def _pair_sum(a_refs, b_refs, trans_b):
    tot = None
    for a_r, b_r in zip(a_refs, b_refs):
        a, b = _bf(a_r[...]), _bf(b_r[...])
        if trans_b:
            d = lax.dot_general(a, b, (((1,), (1,)), ((), ())), preferred_element_type=F32)
        else:
            d = jnp.dot(a, b, preferred_element_type=F32)
        tot = d if tot is None else tot + d
    return tot


def _mm(pairs, out_dtype, name, *, trans_b=False, tm=512, tn=None, tk=None, res=None, dep=None):
    m, k = pairs[0][0].shape
    n = pairs[0][1].shape[0] if trans_b else pairs[0][1].shape[1]
    tm, tn, tk = min(tm, m), tn or n, tk or k
    nk, npair = k // tk, len(pairs)

    def kern(*refs):
        a_refs, b_refs = refs[:npair], refs[npair:2 * npair]
        rest = list(refs[2 * npair:])
        res_ref = rest.pop(0) if res is not None else None
        if dep is not None:
            rest.pop(0)
        o_ref = rest.pop(0)

        def finish(acc):
            if res_ref is not None:
                acc = res_ref[...] + acc
            o_ref[...] = acc.astype(o_ref.dtype)

        if nk == 1:
            finish(_pair_sum(a_refs, b_refs, trans_b))
        else:
            acc_ref = rest.pop(0)
            kk = pl.program_id(2)

            @pl.when(kk == 0)
            def _():
                acc_ref[...] = jnp.zeros_like(acc_ref)

            acc_ref[...] += _pair_sum(a_refs, b_refs, trans_b)

            @pl.when(kk == nk - 1)
            def _():
                finish(acc_ref[...])

    a_spec = pl.BlockSpec((tm, tk), lambda i, j, kk: (i, kk))
    b_spec = (pl.BlockSpec((tn, tk), lambda i, j, kk: (j, kk)) if trans_b
              else pl.BlockSpec((tk, tn), lambda i, j, kk: (kk, j)))
    io_spec = pl.BlockSpec((tm, tn), lambda i, j, kk: (i, j))
    in_specs = ([a_spec] * npair + [b_spec] * npair + ([io_spec] if res is not None else [])
                + ([pl.BlockSpec((8, 128), lambda i, j, kk: (0, 0))] if dep is not None else []))
    args = ([p[0] for p in pairs] + [p[1] for p in pairs] + ([res] if res is not None else [])
            + ([dep] if dep is not None else []))
    return pl.pallas_call(
        kern, name=name, grid=(m // tm, n // tn, nk), in_specs=in_specs, out_specs=io_spec,
        out_shape=jax.ShapeDtypeStruct((m, n), out_dtype),
        scratch_shapes=[pltpu.VMEM((tm, tn), F32)] if nk > 1 else [],
        compiler_params=_params("parallel", "parallel", "arbitrary"))(*args)


def _mm_normbwd(pairs, h, g, dres, name, *, tm=512, tk=None):
    m, k = pairs[0][0].shape
    d = pairs[0][1].shape[0]
    tm, tk = min(tm, m), tk or k
    nk, npair = k // tk, len(pairs)

    def kern(*refs):
        a_refs, b_refs = refs[:npair], refs[npair:2 * npair]
        h_ref, g_ref, dres_ref, o_ref, dg_ref, acc_ref = refs[2 * npair:]
        i, kk = pl.program_id(0), pl.program_id(1)

        @pl.when(jnp.logical_and(i == 0, kk == 0))
        def _():
            dg_ref[...] = jnp.zeros_like(dg_ref)

        @pl.when(kk == 0)
        def _():
            acc_ref[...] = jnp.zeros_like(acc_ref)

        acc_ref[...] += _pair_sum(a_refs, b_refs, True)

        @pl.when(kk == nk - 1)
        def _():
            dn = acc_ref[...]
            x = h_ref[...]
            rstd = _rstd(x)
            xhat = x * rstd
            dg_ref[...] += jnp.broadcast_to(jnp.sum(dn * xhat, axis=0, keepdims=True), dg_ref.shape)
            dxh = dn * g_ref[...]
            dx = rstd * (dxh - xhat * jnp.mean(dxh * xhat, axis=-1, keepdims=True))
            o_ref[...] = dres_ref[...] + dx

    row = pl.BlockSpec((tm, d), lambda i, kk: (i, 0))
    in_specs = ([pl.BlockSpec((tm, tk), lambda i, kk: (i, kk))] * npair
                + [pl.BlockSpec((d, tk), lambda i, kk: (0, kk))] * npair
                + [row, pl.BlockSpec((1, d), lambda i, kk: (0, 0)), row])
    return pl.pallas_call(
        kern, name=name, grid=(m // tm, nk), in_specs=in_specs,
        out_specs=[row, pl.BlockSpec((8, d), lambda i, kk: (0, 0))],
        out_shape=[jax.ShapeDtypeStruct((m, d), F32), jax.ShapeDtypeStruct((8, d), F32)],
        scratch_shapes=[pltpu.VMEM((tm, d), F32)],
        compiler_params=_params("arbitrary", "arbitrary"))(
            *[p[0] for p in pairs], *[p[1] for p in pairs], h, g.reshape(1, d), dres)


def _mm_tn(a, b, out_dtype, name, *, bm=None, bn=None, tk=512):
    t, m = a.shape
    batched = b.ndim == 3
    n = b.shape[-1]
    nb = b.shape[0] if batched else 1
    bm, bn, tk = bm or m, bn or n, min(tk, t)
    nk = t // tk

    def kern(a_ref, b_ref, o_ref, acc_ref):
        kk = pl.program_id(3)

        @pl.when(kk == 0)
        def _():
            acc_ref[...] = jnp.zeros_like(acc_ref)

        acc_ref[...] += lax.dot_general(_bf(a_ref[...]), _bf(b_ref[...]), (((0,), (0,)), ((), ())),
                                        preferred_element_type=F32)

        @pl.when(kk == nk - 1)
        def _():
            o_ref[...] = acc_ref[...].astype(o_ref.dtype)

    a_spec = pl.BlockSpec((tk, bm), lambda h, i, j, kk: (kk, i))
    if batched:
        b_spec = pl.BlockSpec((None, tk, bn), lambda h, i, j, kk: (h, kk, j))
        o_spec = pl.BlockSpec((None, bm, bn), lambda h, i, j, kk: (h, i, j))
        out_shape = jax.ShapeDtypeStruct((nb, m, n), out_dtype)
    else:
        b_spec = pl.BlockSpec((tk, bn), lambda h, i, j, kk: (kk, j))
        o_spec = pl.BlockSpec((bm, bn), lambda h, i, j, kk: (i, j))
        out_shape = jax.ShapeDtypeStruct((m, n), out_dtype)
    return pl.pallas_call(
        kern, name=name, grid=(nb, m // bm, n // bn, nk), in_specs=[a_spec, b_spec], out_specs=o_spec,
        out_shape=out_shape, scratch_shapes=[pltpu.VMEM((bm, bn), F32)],
        compiler_params=_params("parallel", "parallel", "parallel", "arbitrary"))(a, b)


def _ffn_tile(f):
    return f // 2 if (f // 2) % 128 == 0 else f


def _ffn_fwd(h, g, w1, w3, w2, name):
    t, d = h.shape
    f = w1.shape[1]
    tm, tf = min(512, t), _ffn_tile(f)
    nf = f // tf

    def kern(h_ref, g_ref, w1_ref, w3_ref, w2_ref, ho_ref, n_ref, z1_ref, z3_ref, n_sc, acc_ref):
        j = pl.program_id(1)

        @pl.when(j == 0)
        def _():
            x = h_ref[...]
            n = _bf(x * _rstd(x) * g_ref[...])
            n_sc[...] = n
            n_ref[...] = n
            acc_ref[...] = jnp.zeros_like(acc_ref)

        n = n_sc[...]
        z1 = jnp.dot(n, w1_ref[...], preferred_element_type=F32)
        z3 = jnp.dot(n, w3_ref[...], preferred_element_type=F32)
        z1_ref[...] = _bf(z1)
        z3_ref[...] = _bf(z3)
        act = _bf(z1 * _sigmoid(z1) * z3)
        acc_ref[...] += jnp.dot(act, w2_ref[...], preferred_element_type=F32)

        @pl.when(j == nf - 1)
        def _():
            ho_ref[...] = h_ref[...] + FFN_RES * acc_ref[...]

    row = pl.BlockSpec((tm, d), lambda i, j: (i, 0))
    col = pl.BlockSpec((tm, tf), lambda i, j: (i, j))
    return pl.pallas_call(
        kern, name=name, grid=(t // tm, nf),
        in_specs=[row, pl.BlockSpec((1, d), lambda i, j: (0, 0)), pl.BlockSpec((d, tf), lambda i, j: (0, j)),
                  pl.BlockSpec((d, tf), lambda i, j: (0, j)), pl.BlockSpec((tf, d), lambda i, j: (j, 0))],
        out_specs=[row, row, col, col],
        out_shape=[jax.ShapeDtypeStruct((t, d), F32), jax.ShapeDtypeStruct((t, d), BF16),
                   jax.ShapeDtypeStruct((t, f), BF16), jax.ShapeDtypeStruct((t, f), BF16)],
        scratch_shapes=[pltpu.VMEM((tm, d), BF16), pltpu.VMEM((tm, d), F32)],
        compiler_params=_params("parallel", "arbitrary"))(h, g.reshape(1, d), w1, w3, w2)


def _ffn_bwd_act(dh, z1, z3, w2, dep, name):
    t, d = dh.shape
    f = z1.shape[1]
    tm, tf = min(512, t), _ffn_tile(f)

    def kern(dh_ref, z1_ref, z3_ref, w2_ref, dep_ref, dz1_ref, dz3_ref, a_ref, df_ref, df_sc):
        @pl.when(pl.program_id(1) == 0)
        def _():
            df = _bf(FFN_RES * dh_ref[...])
            df_sc[...] = df
            df_ref[...] = df

        da = lax.dot_general(df_sc[...], w2_ref[...], (((1,), (1,)), ((), ())), preferred_element_type=F32)
        z1v, z3v = z1_ref[...].astype(F32), z3_ref[...].astype(F32)
        sig = _sigmoid(z1v)
        silu = z1v * sig
        a_ref[...] = _bf(silu * z3v)
        dz1_ref[...] = _bf(da * z3v * (sig * (1.0 + z1v * (1.0 - sig))))
        dz3_ref[...] = _bf(da * silu)

    row = pl.BlockSpec((tm, d), lambda i, j: (i, 0))
    col = pl.BlockSpec((tm, tf), lambda i, j: (i, j))
    colshape = jax.ShapeDtypeStruct((t, f), BF16)
    return pl.pallas_call(
        kern, name=name, grid=(t // tm, f // tf),
        in_specs=[row, col, col, pl.BlockSpec((tf, d), lambda i, j: (j, 0)),
                  pl.BlockSpec((8, 128), lambda i, j: (0, 0))],
        out_specs=[col, col, col, row],
        out_shape=[colshape, colshape, colshape, jax.ShapeDtypeStruct((t, d), BF16)],
        scratch_shapes=[pltpu.VMEM((tm, d), BF16)],
        compiler_params=_params("parallel", "arbitrary"))(dh, z1, z3, w2, dep)


def _ffn_bwd(dh, h_in, g, n, z1, z3, w1, w3, w2, dep, tag):
    f = w1.shape[1]
    dz1, dz3, act, df = _ffn_bwd_act(dh, z1, z3, w2, dep, tag + "_bwd_act")
    dh_in, dg = _mm_normbwd([(dz1, w1), (dz3, w3)], h_in, g, dh, tag + "_bwd_dn", tk=_ffn_tile(f))
    dw1 = _mm_tn(n, dz1, BF16, tag + "_dw1", bn=_ffn_tile(f))
    dw3 = _mm_tn(n, dz3, BF16, tag + "_dw3", bn=_ffn_tile(f))
    dw2 = _mm_tn(act, df, BF16, tag + "_dw2", bm=_ffn_tile(f))
    return dh_in, dg, dw1, dw3, dw2


def _norm_fwd(h, g, name):
    t, d = h.shape
    tm = min(512, t)

    def kern(h_ref, g_ref, o_ref):
        x = h_ref[...]
        o_ref[...] = _bf(x * _rstd(x) * g_ref[...])

    row = pl.BlockSpec((tm, d), lambda i: (i, 0))
    return pl.pallas_call(
        kern, name=name, grid=(t // tm,), in_specs=[row, pl.BlockSpec((1, d), lambda i: (0, 0))], out_specs=row,
        out_shape=jax.ShapeDtypeStruct((t, d), BF16), compiler_params=_params("parallel"))(h, g.reshape(1, d))


def _loss_bwd(h, target, g):
    t, d = h.shape
    tm = min(512, t)

    def kern(h_ref, t_ref, g_ref, dh_ref, dg_ref, loss_ref):
        @pl.when(pl.program_id(0) == 0)
        def _():
            dg_ref[...] = jnp.zeros_like(dg_ref)
            loss_ref[...] = jnp.zeros_like(loss_ref)

        x = h_ref[...]
        rstd = _rstd(x)
        xhat = x * rstd
        err = xhat * g_ref[...] - t_ref[...]
        row_loss = jnp.sum(err * err, axis=-1, keepdims=True) * (0.5 / d)
        loss_ref[...] += jnp.broadcast_to(jnp.sum(row_loss, axis=0, keepdims=True), loss_ref.shape)
        dy = err * (1.0 / d)
        dg_ref[...] += jnp.broadcast_to(jnp.sum(dy * xhat, axis=0, keepdims=True), dg_ref.shape)
        dxh = dy * g_ref[...]
        dh_ref[...] = rstd * (dxh - xhat * jnp.mean(dxh * xhat, axis=-1, keepdims=True))

    row = pl.BlockSpec((tm, d), lambda i: (i, 0))
    return pl.pallas_call(
        kern, name="loss_bwd", grid=(t // tm,),
        in_specs=[row, row, pl.BlockSpec((1, d), lambda i: (0, 0))],
        out_specs=[row, pl.BlockSpec((8, d), lambda i: (0, 0)), pl.BlockSpec((8, 128), lambda i: (0, 0))],
        out_shape=[jax.ShapeDtypeStruct((t, d), F32), jax.ShapeDtypeStruct((8, d), F32),
                   jax.ShapeDtypeStruct((8, 128), F32)],
        compiler_params=_params("arbitrary"))(h, target, g.reshape(1, d))


def _glu_fwd(m, wa, wb):
    t, d = m.shape
    c = wa.shape[1]
    tm, tc = min(512, t), min(512, c)

    def kern(m_ref, wa_ref, wb_ref, a_ref, b_ref, glu_ref):
        mv = m_ref[...]
        a = jnp.dot(mv, wa_ref[...], preferred_element_type=F32)
        b = jnp.dot(mv, wb_ref[...], preferred_element_type=F32)
        a_ref[...] = _bf(a)
        b_ref[...] = _bf(b)
        glu_ref[...] = _bf(a * _sigmoid(b))

    col = pl.BlockSpec((tm, tc), lambda i, j: (i, j))
    wspec = pl.BlockSpec((d, tc), lambda i, j: (0, j))
    shape = jax.ShapeDtypeStruct((t, c), BF16)
    return pl.pallas_call(
        kern, name="conv_glu_fwd", grid=(t // tm, c // tc),
        in_specs=[pl.BlockSpec((tm, d), lambda i, j: (i, 0)), wspec, wspec], out_specs=[col, col, col],
        out_shape=[shape, shape, shape], compiler_params=_params("parallel", "parallel"))(m, wa, wb)


def _conv_tile(t):
    return min(256, t)


def _conv_fwd(glu, w_dw, g):
    t, c = glu.shape
    tm = _conv_tile(t)
    hb = tm // CONV_HALO

    def kern(cur_ref, halo_ref, w_ref, g_ref, cv_ref, s_ref, ext):
        i = pl.program_id(0)
        ext[0:CONV_HALO, :] = jnp.where(i > 0, halo_ref[...].astype(F32), 0.0)
        ext[CONV_HALO:, :] = cur_ref[...].astype(F32)
        gv = g_ref[...]
        for r0 in range(0, tm, CONV_ROWS):
            acc = jnp.zeros((CONV_ROWS, c), F32)
            for k in range(CONV_WIDTH):
                acc = acc + ext[pl.ds(r0 + 2 + k, CONV_ROWS), :] * w_ref[k:k + 1, :]
            cv_ref[r0:r0 + CONV_ROWS, :] = acc
            rn = acc * _rstd(acc) * gv
            s_ref[r0:r0 + CONV_ROWS, :] = _bf(rn * _sigmoid(rn))

    row = pl.BlockSpec((tm, c), lambda i: (i, 0))
    return pl.pallas_call(
        kern, name="conv_fwd", grid=(t // tm,),
        in_specs=[row, pl.BlockSpec((CONV_HALO, c), lambda i: (jnp.maximum(i * hb - 1, 0), 0)),
                  pl.BlockSpec((CONV_HALO, c), lambda i: (0, 0)), pl.BlockSpec((1, c), lambda i: (0, 0))],
        out_specs=[row, row],
        out_shape=[jax.ShapeDtypeStruct((t, c), F32), jax.ShapeDtypeStruct((t, c), BF16)],
        scratch_shapes=[pltpu.VMEM((tm + CONV_HALO, c), F32)],
        compiler_params=_params("parallel"))(glu, glu, w_dw, g.reshape(1, c))


def _conv_bwd_norm(dh, cv, w_pw2, g, dep):
    t, c = cv.shape
    tm = min(512, t)

    def kern(dh_ref, cv_ref, w_ref, g_ref, dep_ref, dcv_ref, dg_ref):
        @pl.when(pl.program_id(0) == 0)
        def _():
            dg_ref[...] = jnp.zeros_like(dg_ref)

        ds = lax.dot_general(_bf(dh_ref[...]), w_ref[...], (((1,), (1,)), ((), ())), preferred_element_type=F32)
        x = cv_ref[...]
        rstd = _rstd(x)
        xhat = x * rstd
        rn = xhat * g_ref[...]
        sig = _sigmoid(rn)
        drn = ds * (sig * (1.0 + rn * (1.0 - sig)))
        dg_ref[...] += jnp.broadcast_to(jnp.sum(drn * xhat, axis=0, keepdims=True), dg_ref.shape)
        dxh = drn * g_ref[...]
        dcv_ref[...] = rstd * (dxh - xhat * jnp.mean(dxh * xhat, axis=-1, keepdims=True))

    row = pl.BlockSpec((tm, c), lambda i: (i, 0))
    return pl.pallas_call(
        kern, name="conv_bwd_norm", grid=(t // tm,),
        in_specs=[pl.BlockSpec((tm, dh.shape[1]), lambda i: (i, 0)), row,
                  pl.BlockSpec(w_pw2.shape, lambda i: (0, 0)), pl.BlockSpec((1, c), lambda i: (0, 0)),
                  pl.BlockSpec((8, 128), lambda i: (0, 0))],
        out_specs=[row, pl.BlockSpec((8, c), lambda i: (0, 0))],
        out_shape=[jax.ShapeDtypeStruct((t, c), F32), jax.ShapeDtypeStruct((8, c), F32)],
        compiler_params=_params("arbitrary"))(dh, cv, w_pw2, g.reshape(1, c), dep)


def _conv_bwd_dw(dcv, glu, a, b, w_dw):
    t, c = dcv.shape
    tm = _conv_tile(t)
    hb = tm // CONV_HALO
    last = t // CONV_HALO - 1

    def kern(dcv_ref, dnext_ref, glu_ref, gprev_ref, a_ref, b_ref, w_ref, da_ref, db_ref, dw_ref, dext, gext):
        i = pl.program_id(0)

        @pl.when(i == 0)
        def _():
            dw_ref[...] = jnp.zeros_like(dw_ref)

        dext[0:tm, :] = dcv_ref[...]
        dext[tm:, :] = jnp.where(i < t // tm - 1, dnext_ref[...], 0.0)
        gext[0:CONV_HALO, :] = jnp.where(i > 0, gprev_ref[...].astype(F32), 0.0)
        gext[CONV_HALO:, :] = glu_ref[...].astype(F32)
        for r0 in range(0, tm, CONV_ROWS):
            acc = jnp.zeros((CONV_ROWS, c), F32)
            for k in range(CONV_WIDTH):
                acc = acc + dext[pl.ds(r0 + CONV_WIDTH - 1 - k, CONV_ROWS), :] * w_ref[k:k + 1, :]
            av = a_ref[r0:r0 + CONV_ROWS, :].astype(F32)
            sig = _sigmoid(b_ref[r0:r0 + CONV_ROWS, :].astype(F32))
            da_ref[r0:r0 + CONV_ROWS, :] = _bf(acc * sig)
            db_ref[r0:r0 + CONV_ROWS, :] = _bf(acc * av * sig * (1.0 - sig))
        for k in range(CONV_WIDTH):
            acc = jnp.zeros((CONV_ROWS, c), F32)
            for r0 in range(0, tm, CONV_ROWS):
                acc = acc + gext[pl.ds(r0 + 2 + k, CONV_ROWS), :] * dext[r0:r0 + CONV_ROWS, :]
            dw_ref[k:k + 1, :] += jnp.sum(acc, axis=0, keepdims=True)

    row = pl.BlockSpec((tm, c), lambda i: (i, 0))
    shape = jax.ShapeDtypeStruct((t, c), BF16)
    return pl.pallas_call(
        kern, name="conv_bwd_dw", grid=(t // tm,),
        in_specs=[row, pl.BlockSpec((CONV_HALO, c), lambda i: (jnp.minimum((i + 1) * hb, last), 0)),
                  row, pl.BlockSpec((CONV_HALO, c), lambda i: (jnp.maximum(i * hb - 1, 0), 0)),
                  row, row, pl.BlockSpec((CONV_HALO, c), lambda i: (0, 0))],
        out_specs=[row, row, pl.BlockSpec((CONV_HALO, c), lambda i: (0, 0))],
        out_shape=[shape, shape, jax.ShapeDtypeStruct((CONV_HALO, c), F32)],
        scratch_shapes=[pltpu.VMEM((tm + CONV_HALO, c), F32), pltpu.VMEM((tm + CONV_HALO, c), F32)],
        compiler_params=_params("arbitrary"))(dcv, dcv, glu, glu, a, b, w_dw)


def _rope_tables(pos):
    t = pos.shape[0]
    tm = min(512, t)
    freq = (np.float32(ROPE_THETA) ** (np.float32(-2.0) * np.arange(ROPE // 2, dtype=np.float32)
                                       / np.float32(ROPE))).astype(np.float32)
    row = np.zeros((2, 128), np.float32)
    row[0, :ROPE] = np.concatenate([freq, freq])
    row[1, :ROPE] = 1.0

    def kern(pos_ref, f_ref, c_ref, s_ref):
        ang = pos_ref[...].astype(F32) * f_ref[0:1, :]
        mask = f_ref[1:2, :]
        c_ref[...] = jnp.cos(ang) * mask
        s_ref[...] = jnp.sin(ang) * mask

    out = pl.BlockSpec((tm, 128), lambda i: (i, 0))
    shape = jax.ShapeDtypeStruct((t, 128), F32)
    return pl.pallas_call(
        kern, name="rope_tables", grid=(t // tm,),
        in_specs=[pl.BlockSpec((tm, 1), lambda i: (i, 0)), pl.BlockSpec((2, 128), lambda i: (0, 0))],
        out_specs=[out, out], out_shape=[shape, shape], compiler_params=_params("parallel"))(pos, jnp.asarray(row))


def _mla_prep(a, gq, gkv, cs_c, cs_s):
    t = a.shape[0]
    tm = min(512, t)
    kv0, r0 = Q_LORA, Q_LORA + KV_LORA

    def kern(a_ref, gq_ref, gkv_ref, c_ref, s_ref, cq_ref, ckv_ref, kr_ref):
        aq = a_ref[:, 0:kv0]
        akv = a_ref[:, kv0:r0]
        ar = a_ref[:, r0:A_PAD]
        cq_ref[...] = _bf(aq * _rstd(aq) * gq_ref[...])
        ckv_ref[...] = _bf(akv * _rstd(akv) * gkv_ref[...])
        kr_ref[...] = _bf(ar * c_ref[...] + _rot(ar) * s_ref[...])

    def row(w):
        return pl.BlockSpec((tm, w), lambda i: (i, 0))

    def vec(w):
        return pl.BlockSpec((1, w), lambda i: (0, 0))

    return pl.pallas_call(
        kern, name="mla_prep", grid=(t // tm,),
        in_specs=[row(A_PAD), vec(Q_LORA), vec(KV_LORA), row(128), row(128)],
        out_specs=[row(Q_LORA), row(KV_LORA), row(128)],
        out_shape=[jax.ShapeDtypeStruct((t, Q_LORA), BF16), jax.ShapeDtypeStruct((t, KV_LORA), BF16),
                   jax.ShapeDtypeStruct((t, 128), BF16)],
        compiler_params=_params("parallel"))(a, gq.reshape(1, -1), gkv.reshape(1, -1), cs_c, cs_s)


def _mla_prep_bwd(a, dcq, dckv, dar, gq, gkv):
    t = a.shape[0]
    tm = min(512, t)
    kv0, r0 = Q_LORA, Q_LORA + KV_LORA

    def kern(a_ref, dcq_ref, dckv_ref, dar_ref, gq_ref, gkv_ref, da_ref, dgq_ref, dgkv_ref):
        @pl.when(pl.program_id(0) == 0)
        def _():
            dgq_ref[...] = jnp.zeros_like(dgq_ref)
            dgkv_ref[...] = jnp.zeros_like(dgkv_ref)

        def back(x, dy, g_ref, dg_ref):
            rstd = _rstd(x)
            xhat = x * rstd
            dg_ref[...] += jnp.broadcast_to(jnp.sum(dy * xhat, axis=0, keepdims=True), dg_ref.shape)
            dxh = dy * g_ref[...]
            return rstd * (dxh - xhat * jnp.mean(dxh * xhat, axis=-1, keepdims=True))

        da_ref[:, 0:kv0] = _bf(back(a_ref[:, 0:kv0], dcq_ref[...], gq_ref, dgq_ref))
        da_ref[:, kv0:r0] = _bf(back(a_ref[:, kv0:r0], dckv_ref[...], gkv_ref, dgkv_ref))
        da_ref[:, r0:A_PAD] = _bf(dar_ref[...])

    def row(w):
        return pl.BlockSpec((tm, w), lambda i: (i, 0))

    def vec(r, w):
        return pl.BlockSpec((r, w), lambda i: (0, 0))

    return pl.pallas_call(
        kern, name="mla_prep_bwd", grid=(t // tm,),
        in_specs=[row(A_PAD), row(Q_LORA), row(KV_LORA), row(128), vec(1, Q_LORA), vec(1, KV_LORA)],
        out_specs=[row(A_PAD), vec(8, Q_LORA), vec(8, KV_LORA)],
        out_shape=[jax.ShapeDtypeStruct((t, A_PAD), BF16), jax.ShapeDtypeStruct((8, Q_LORA), F32),
                   jax.ShapeDtypeStruct((8, KV_LORA), F32)],
        compiler_params=_params("arbitrary"))(a, dcq, dckv, dar, gq.reshape(1, -1), gkv.reshape(1, -1))


def _mla_qkv(cq, ckv, kr, cs_c, cs_s, wuq, wukv):
    t = cq.shape[0]
    tm = min(512, t)

    def kern(cq_ref, ckv_ref, kr_ref, c_ref, s_ref, wq_ref, wkv_ref, q_ref, k_ref, v_ref):
        r = jnp.dot(cq_ref[...], wq_ref[...], preferred_element_type=F32)
        xr = r[:, NOPE:]
        q_ref[:, 0:NOPE] = _bf(r[:, 0:NOPE] * ATTN_SCALE)
        q_ref[:, NOPE:] = _bf((xr * c_ref[...] + _rot(xr) * s_ref[...]) * ATTN_SCALE)
        kv = jnp.dot(ckv_ref[...], wkv_ref[...], preferred_element_type=F32)
        k_ref[:, 0:NOPE] = _bf(kv[:, 0:NOPE])
        k_ref[:, NOPE:] = kr_ref[...]
        v_ref[...] = _bf(kv[:, NOPE:])

    def row(w):
        return pl.BlockSpec((tm, w), lambda i, h: (i, 0))

    def head(w):
        return pl.BlockSpec((None, tm, w), lambda i, h: (h, i, 0))

    return pl.pallas_call(
        kern, name="mla_qkv", grid=(t // tm, HEADS),
        in_specs=[row(Q_LORA), row(KV_LORA), row(128), row(128), row(128),
                  pl.BlockSpec((None, Q_LORA, HEAD_PAD), lambda i, h: (h, 0, 0)),
                  pl.BlockSpec((None, KV_LORA, NOPE + V_HEAD), lambda i, h: (h, 0, 0))],
        out_specs=[head(HEAD_PAD), head(HEAD_PAD), head(V_HEAD)],
        out_shape=[jax.ShapeDtypeStruct((HEADS, t, HEAD_PAD), BF16), jax.ShapeDtypeStruct((HEADS, t, HEAD_PAD), BF16),
                   jax.ShapeDtypeStruct((HEADS, t, V_HEAD), BF16)],
        compiler_params=_params("parallel", "arbitrary"))(cq, ckv, kr, cs_c, cs_s, wuq, wukv)


def _mla_qkv_bwd(dq, dk, dv, cs_c, cs_s, wuq, wukv):
    t = dq.shape[1]
    tm = min(512, t)

    def kern(dq_ref, dk_ref, dv_ref, c_ref, s_ref, wq_ref, wkv_ref, dr_ref, dkv_ref, dcq_ref, dckv_ref, dar_ref):
        @pl.when(pl.program_id(1) == 0)
        def _():
            dcq_ref[...] = jnp.zeros_like(dcq_ref)
            dckv_ref[...] = jnp.zeros_like(dckv_ref)
            dar_ref[...] = jnp.zeros_like(dar_ref)

        cv, sv = c_ref[...], s_ref[...]
        dqx = dq_ref[:, NOPE:]
        dr_ref[:, 0:NOPE] = _bf(dq_ref[:, 0:NOPE] * ATTN_SCALE)
        dr_ref[:, NOPE:] = _bf((dqx * cv + _rot_t(dqx * sv)) * ATTN_SCALE)
        dcq_ref[...] += lax.dot_general(dr_ref[...], wq_ref[...], (((1,), (1,)), ((), ())),
                                        preferred_element_type=F32)
        dkx = dk_ref[:, NOPE:]
        dar_ref[...] += dkx * cv + _rot_t(dkx * sv)
        dkv_ref[:, 0:NOPE] = _bf(dk_ref[:, 0:NOPE])
        dkv_ref[:, NOPE:] = _bf(dv_ref[...])
        dckv_ref[...] += lax.dot_general(dkv_ref[...], wkv_ref[...], (((1,), (1,)), ((), ())),
                                         preferred_element_type=F32)

    def row(w):
        return pl.BlockSpec((tm, w), lambda i, h: (i, 0))

    def head(w):
        return pl.BlockSpec((None, tm, w), lambda i, h: (h, i, 0))

    return pl.pallas_call(
        kern, name="mla_qkv_bwd", grid=(t // tm, HEADS),
        in_specs=[head(HEAD_PAD), head(HEAD_PAD), head(V_HEAD), row(128), row(128),
                  pl.BlockSpec((None, Q_LORA, HEAD_PAD), lambda i, h: (h, 0, 0)),
                  pl.BlockSpec((None, KV_LORA, NOPE + V_HEAD), lambda i, h: (h, 0, 0))],
        out_specs=[head(HEAD_PAD), head(NOPE + V_HEAD), row(Q_LORA), row(KV_LORA), row(128)],
        out_shape=[jax.ShapeDtypeStruct((HEADS, t, HEAD_PAD), BF16),
                   jax.ShapeDtypeStruct((HEADS, t, NOPE + V_HEAD), BF16),
                   jax.ShapeDtypeStruct((t, Q_LORA), F32), jax.ShapeDtypeStruct((t, KV_LORA), F32),
                   jax.ShapeDtypeStruct((t, 128), F32)],
        compiler_params=_params("parallel", "arbitrary"))(dq, dk, dv, cs_c, cs_s, wuq, wukv)


def _attn_block(t):
    return 512 if t >= 4096 else 128


def _chunk_mask(bk, bq):
    kc = lax.broadcasted_iota(jnp.int32, (bk, bq), 0) // CHUNK
    qc = lax.broadcasted_iota(jnp.int32, (bk, bq), 1) // CHUNK
    return qc >= kc


def _flash_fwd(q, k, v):
    t = q.shape[1]
    bq = _attn_block(t)
    nq = t // bq

    def kern(q_ref, k_ref, v_ref, o_ref, lse_ref):
        i = pl.program_id(1)
        qv = q_ref[...]

        def scores(j):
            kj = k_ref[pl.ds(pl.multiple_of(j * bq, bq), bq), :]
            return lax.dot_general(kj, qv, (((1,), (1,)), ((), ())), preferred_element_type=F32)

        def update(j, st, m, l, acc):
            m_new = jnp.maximum(m, jnp.max(st, axis=0, keepdims=True))
            alpha = jnp.exp(m - m_new)
            p = jnp.exp(st - m_new)
            vj = v_ref[pl.ds(pl.multiple_of(j * bq, bq), bq), :]
            pv = lax.dot_general(vj, _bf(p), (((0,), (0,)), ((), ())), preferred_element_type=F32)
            return m_new, alpha * l + jnp.sum(p, axis=0, keepdims=True), alpha * acc + pv

        st = jnp.where(_chunk_mask(bq, bq), scores(i), -1e30)
        carry = update(i, st, jnp.full((1, bq), -1e30, F32), jnp.zeros((1, bq), F32), jnp.zeros((V_HEAD, bq), F32))
        m, l, acc = lax.fori_loop(0, i, lambda j, c: update(j, scores(j), *c), carry)
        o_ref[...] = _bf((acc / l).T)
        lse_ref[...] = jnp.broadcast_to(m + jnp.log(l), (8, bq))

    return pl.pallas_call(
        kern, name="flash_fwd", grid=(HEADS, nq),
        in_specs=[pl.BlockSpec((None, bq, HEAD_PAD), lambda h, i: (h, i, 0)),
                  pl.BlockSpec((None, t, HEAD_PAD), lambda h, i: (h, 0, 0)),
                  pl.BlockSpec((None, t, V_HEAD), lambda h, i: (h, 0, 0))],
        out_specs=[pl.BlockSpec((bq, V_HEAD), lambda h, i: (i, h)),
                   pl.BlockSpec((None, None, 8, bq), lambda h, i: (h, i, 0, 0))],
        out_shape=[jax.ShapeDtypeStruct((t, HEADS * V_HEAD), BF16), jax.ShapeDtypeStruct((HEADS, nq, 8, bq), F32)],
        compiler_params=_params("parallel", "arbitrary"))(q, k, v)


def _attn_delta(do, o):
    t = do.shape[0]
    bq = _attn_block(t)

    def kern(do_ref, o_ref, d_ref):
        prod = do_ref[...].astype(F32) * o_ref[...].astype(F32)
        d_ref[...] = jnp.broadcast_to(jnp.sum(prod.T, axis=0, keepdims=True), (8, bq))

    blk = pl.BlockSpec((bq, V_HEAD), lambda h, i: (i, h))
    return pl.pallas_call(
        kern, name="attn_delta", grid=(HEADS, t // bq), in_specs=[blk, blk],
        out_specs=pl.BlockSpec((None, None, 8, bq), lambda h, i: (h, i, 0, 0)),
        out_shape=jax.ShapeDtypeStruct((HEADS, t // bq, 8, bq), F32),
        compiler_params=_params("parallel", "parallel"))(do, o)


def _flash_bwd(q, k, v, do, lse, delta):
    t = q.shape[1]
    bq = _attn_block(t)
    nq = t // bq

    def kern(q_ref, k_ref, v_ref, do_ref, lse_ref, del_ref, dq_ref, dk_ref, dv_ref):
        j = pl.program_id(1)

        @pl.when(j == 0)
        def _():
            dq_ref[...] = jnp.zeros_like(dq_ref)

        dk_ref[...] = jnp.zeros_like(dk_ref)
        dv_ref[...] = jnp.zeros_like(dv_ref)
        kj, vj = k_ref[...], v_ref[...]

        def step(i, masked):
            rows = pl.ds(pl.multiple_of(i * bq, bq), bq)
            qi, doi = q_ref[rows, :], do_ref[rows, :]
            st = lax.dot_general(kj, qi, (((1,), (1,)), ((), ())), preferred_element_type=F32)
            pt = jnp.exp(st - lse_ref[i][0:1, :])
            if masked:
                pt = jnp.where(_chunk_mask(bq, bq), pt, 0.0)
            dpt = lax.dot_general(vj, doi, (((1,), (1,)), ((), ())), preferred_element_type=F32)
            dst = _bf(pt * (dpt - del_ref[i][0:1, :]))
            dv_ref[...] += jnp.dot(_bf(pt), doi, preferred_element_type=F32)
            dk_ref[...] += jnp.dot(dst, qi, preferred_element_type=F32)
            dq_ref[rows, :] += lax.dot_general(dst, kj, (((0,), (0,)), ((), ())), preferred_element_type=F32)

        step(j, True)

        def body(i, carry):
            step(i, False)
            return carry

        lax.fori_loop(j + 1, nq, body, 0)

    stat = pl.BlockSpec((None, nq, 8, bq), lambda h, j: (h, 0, 0, 0))
    return pl.pallas_call(
        kern, name="flash_bwd", grid=(HEADS, nq),
        in_specs=[pl.BlockSpec((None, t, HEAD_PAD), lambda h, j: (h, 0, 0)),
                  pl.BlockSpec((None, bq, HEAD_PAD), lambda h, j: (h, j, 0)),
                  pl.BlockSpec((None, bq, V_HEAD), lambda h, j: (h, j, 0)),
                  pl.BlockSpec((t, V_HEAD), lambda h, j: (0, h)), stat, stat],
        out_specs=[pl.BlockSpec((None, t, HEAD_PAD), lambda h, j: (h, 0, 0)),
                   pl.BlockSpec((None, bq, HEAD_PAD), lambda h, j: (h, j, 0)),
                   pl.BlockSpec((None, bq, V_HEAD), lambda h, j: (h, j, 0))],
        out_shape=[jax.ShapeDtypeStruct((HEADS, t, HEAD_PAD), F32), jax.ShapeDtypeStruct((HEADS, t, HEAD_PAD), F32),
                   jax.ShapeDtypeStruct((HEADS, t, V_HEAD), F32)],
        compiler_params=_params("parallel", "arbitrary"))(q, k, v, do, lse, delta)


def _place():
    x, y, c = lax.axis_index("x"), lax.axis_index("y"), lax.axis_index("c")
    return x, y, c, [(1 - x, y), (x, 1 - y), (1 - x, 1 - y)]


def _all_gather_rows(block, name):
    m_per, n = block.shape

    def body(x_ref, out_ref, send_sems, recv_sems, local_sem):
        x, y, c, chips = _place()
        me, sibling = (x, y, c), (x, y, 1 - c)

        def rows(px, py, pc):
            return out_ref.at[pl.ds((4 * px + 2 * py + pc) * m_per, m_per), :]

        def copy(k, blk, to, src=None):
            return pltpu.make_async_remote_copy(
                src_ref=rows(*blk) if src is None else src, dst_ref=rows(*blk), send_sem=send_sems.at[k],
                recv_sem=recv_sems.at[k], device_id=to, device_id_type=MESH)

        mine = pltpu.make_async_copy(x_ref, rows(*me), local_sem)
        mine.start()
        first = [copy(0, me, sibling, src=x_ref)]
        first += [copy(1 + j, me, (*chip, c), src=x_ref) for j, chip in enumerate(chips)]
        for cp in first:
            cp.start()
        passed = [copy(4 + j, (*chip, c), sibling) for j, chip in enumerate(chips)]
        for j, chip in enumerate(chips):
            copy(1 + j, (*chip, c), me).wait_recv()
            passed[j].start()
        copy(0, sibling, me).wait_recv()
        for j, chip in enumerate(chips):
            copy(4 + j, (*chip, 1 - c), me).wait_recv()
        for cp in first + passed:
            cp.wait_send()
        mine.wait()

    return pl.pallas_call(
        body, name=name, out_shape=jax.ShapeDtypeStruct((8 * m_per, n), block.dtype),
        in_specs=[pl.BlockSpec(memory_space=pltpu.VMEM)], out_specs=pl.BlockSpec(memory_space=pltpu.VMEM),
        scratch_shapes=[pltpu.SemaphoreType.DMA((7,)), pltpu.SemaphoreType.DMA((7,)), pltpu.SemaphoreType.DMA],
        compiler_params=pltpu.CompilerParams(vmem_limit_bytes=VMEM_LIMIT_BYTES))(block)


HBM_SPEC = pl.BlockSpec(memory_space=pltpu.HBM)
SEM_SPEC = pl.BlockSpec(memory_space=pltpu.SEMAPHORE)
DATAFLOW = pltpu.SideEffectType.DATAFLOW_SIDE_EFFECTING


def _in_hbm(a):
    return pltpu.with_memory_space_constraint(a, pltpu.HBM)


def _chip_copies(ins, lands, send_sems, recv_sems, src_slot):
    n = len(ins)
    x, y, c, chips = _place()
    me = 2 * x + y
    return [pltpu.make_async_remote_copy(
        src_ref=ins[w].at[2 * chip[0] + chip[1]] if src_slot else ins[w], dst_ref=lands[w].at[me],
        send_sem=send_sems.at[p * n + w], recv_sem=recv_sems.at[p * n + w], device_id=(*chip, c),
        device_id_type=MESH) for w in range(n) for p, chip in enumerate(chips)]


def _exchange_start(srcs, lands, src_slot, name):
    n = len(srcs)

    def body(*refs):
        for cp in _chip_copies(refs[:n], refs[n:2 * n], refs[2 * n], refs[2 * n + 1], src_slot):
            cp.start()
        token = refs[-1]
        token[...] = jnp.zeros_like(token)

    thru = [pltpu.HBM(a.shape, a.dtype) for a in list(srcs) + list(lands)]
    res = pl.pallas_call(
        body, name=name,
        out_shape=(pltpu.SemaphoreType.DMA((3 * n,)), pltpu.SemaphoreType.DMA((3 * n,)), *thru,
                   jax.ShapeDtypeStruct((8, 128), F32)),
        in_specs=[HBM_SPEC] * (2 * n),
        out_specs=(SEM_SPEC, SEM_SPEC, *[HBM_SPEC] * (2 * n), pl.BlockSpec(memory_space=pltpu.VMEM)),
        input_output_aliases={i: 2 + i for i in range(2 * n)},
        compiler_params=pltpu.CompilerParams(has_side_effects=DATAFLOW))(
            *[_in_hbm(a) for a in srcs], *[_in_hbm(a) for a in lands])
    return (res[0], res[1], list(res[2:2 + n]), list(res[2 + n:2 + 2 * n])), res[-1]


def _exchange_wait(flight, after, src_slot, name):
    send_sems, recv_sems, srcs, lands = flight
    n = len(srcs)

    def body(*refs):
        for cp in _chip_copies(refs[:n], refs[n:2 * n], refs[2 * n], refs[2 * n + 1], src_slot):
            cp.wait_send()
            cp.wait_recv()

    thru = [pltpu.HBM(a.shape, a.dtype) for a in list(srcs) + list(lands)]
    res = pl.pallas_call(
        body, name=name, out_shape=thru,
        in_specs=[HBM_SPEC] * (2 * n) + [SEM_SPEC, SEM_SPEC, pl.BlockSpec(memory_space=pl.ANY)],
        out_specs=[HBM_SPEC] * (2 * n), input_output_aliases={i: i for i in range(2 * n)},
        compiler_params=pltpu.CompilerParams(has_side_effects=DATAFLOW))(*srcs, *lands, send_sems, recv_sems, after)
    return list(res[n:])


def _landing(own, me):
    return lax.dynamic_update_index_in_dim(lax.empty((4, *own.shape), own.dtype), own, me, 0)


def _swap_with_sibling(arrays):
    n = len(arrays)

    def body(*refs):
        ins, outs = refs[:n], refs[n:2 * n]
        send_sems, recv_sems = refs[2 * n:]
        x, y, c, _ = _place()
        copies = [pltpu.make_async_remote_copy(src_ref=ins[w], dst_ref=outs[w], send_sem=send_sems.at[w],
                                               recv_sem=recv_sems.at[w], device_id=(x, y, 1 - c), device_id_type=MESH)
                  for w in range(n)]
        for cp in copies:
            cp.start()
        for cp in copies:
            cp.wait()

    any_spec = pl.BlockSpec(memory_space=pl.ANY)
    return pl.pallas_call(
        body, name="swap_with_sibling", out_shape=[jax.ShapeDtypeStruct(a.shape, a.dtype) for a in arrays],
        in_specs=[any_spec] * n, out_specs=[any_spec] * n,
        scratch_shapes=[pltpu.SemaphoreType.DMA((n,)), pltpu.SemaphoreType.DMA((n,))])(*arrays)


def _as_rows(a):
    return a.reshape(-1, a.shape[-1])


def _row_tile(r, c, budget_bytes=1 << 20):
    tr = r
    while tr % 16 == 0 and tr * c * 4 > budget_bytes:
        tr //= 2
    return tr


def _sum_slots(layers, name):
    _, r, c = layers[0].shape
    tr = _row_tile(r, c)
    nt = r // tr
    acc = None
    for l, r4 in enumerate(layers):
        def kern(r_ref, *rest):
            o_ref = rest[-1]
            o_ref[...] = (((r_ref[0].astype(F32) + r_ref[1].astype(F32)) + r_ref[2].astype(F32))
                          + r_ref[3].astype(F32))

        out_spec = pl.BlockSpec((tr, c), lambda i, l=l: (l * nt + i, 0))
        acc = pl.pallas_call(
            kern, name=f"{name}_l{l}", grid=(nt,),
            in_specs=[pl.BlockSpec((4, tr, c), lambda i: (0, i, 0))] + ([pl.BlockSpec(memory_space=pl.ANY)] if l else []),
            out_specs=out_spec, out_shape=jax.ShapeDtypeStruct((len(layers) * r, c), F32),
            input_output_aliases={1: 0} if l else {},
            compiler_params=_params("parallel"))(*([r4, acc] if l else [r4]))
    return acc


def _adamw(w, m, v, parts, name):
    r, c = w.shape
    tr = _row_tile(r, c, 1 << 19)
    npart = len(parts)
    c1 = 1.0 - ADAM_B1 ** ADAM_STEP
    c2 = 1.0 - ADAM_B2 ** ADAM_STEP

    def kern(*refs):
        w_ref, m_ref, v_ref = refs[:3]
        p_refs = refs[3:3 + npart]
        g_ref, d_ref, mo_ref, vo_ref = refs[3 + npart:]
        g = p_refs[0][...]
        for p in p_refs[1:]:
            g = g + p[...]
        mn = ADAM_B1 * m_ref[...] + (1.0 - ADAM_B1) * g
        vn = ADAM_B2 * v_ref[...] + (1.0 - ADAM_B2) * (g * g)
        g_ref[...] = g
        mo_ref[...] = mn
        vo_ref[...] = vn
        d_ref[...] = -ADAM_LR * ((mn / c1) / (jnp.sqrt(vn / c2) + ADAM_EPS) + ADAM_WD * w_ref[...])

    blk = pl.BlockSpec((tr, c), lambda i: (i, 0))
    shape = jax.ShapeDtypeStruct((r, c), F32)
    return pl.pallas_call(
        kern, name=name, grid=(r // tr,), in_specs=[blk] * (3 + npart), out_specs=[blk] * 4, out_shape=[shape] * 4,
        compiler_params=_params("parallel"))(w, m, v, *parts)


def _sum_devices(g8, name):
    _, r, c = g8.shape

    def kern(g_ref, o_ref):
        tot = g_ref[0]
        for dev in range(1, 8):
            tot = tot + g_ref[dev]
        o_ref[...] = tot

    return pl.pallas_call(
        kern, name=name, grid=(1,), in_specs=[pl.BlockSpec((8, r, c), lambda i: (0, 0, 0))],
        out_specs=pl.BlockSpec((r, c), lambda i: (0, 0)), out_shape=jax.ShapeDtypeStruct((r, c), F32),
        compiler_params=_params("arbitrary"))(g8)


def _pad_lanes(a, width):
    return jnp.pad(a, [(0, 0)] * (a.ndim - 1) + [(0, width - a.shape[-1])])


def kernel(x, positions, ffn_norm1, ffn1_w1, ffn1_w3, ffn1_w2, mix_norm, ffn_norm2, ffn2_w1, ffn2_w3, ffn2_w2, conv_w_pw1, conv_w_dw, conv_norm, conv_w_pw2, mla_w_a, mla_q_norm, mla_kv_norm, mla_w_uq, mla_w_ukv, mla_w_o, final_norm, loss_target, m_ffn_norm1, m_ffn1_w1, m_ffn1_w3, m_ffn1_w2, m_mix_norm, m_ffn_norm2, m_ffn2_w1, m_ffn2_w3, m_ffn2_w2, m_conv_w_pw1, m_conv_w_dw, m_conv_norm, m_conv_w_pw2, m_mla_w_a, m_mla_q_norm, m_mla_kv_norm, m_mla_w_uq, m_mla_w_ukv, m_mla_w_o, m_final_norm, v_ffn_norm1, v_ffn1_w1, v_ffn1_w3, v_ffn1_w2, v_mix_norm, v_ffn_norm2, v_ffn2_w1, v_ffn2_w3, v_ffn2_w2, v_conv_w_pw1, v_conv_w_dw, v_conv_norm, v_conv_w_pw2, v_mla_w_a, v_mla_q_norm, v_mla_kv_norm, v_mla_w_uq, v_mla_w_ukv, v_mla_w_o, v_final_norm):
    given = locals()
    return _step({nm: given[nm] for nm in INPUTS})


def _step(A):
    x = A['x'][0]
    target = A['loss_target'][0]
    t, d = x.shape
    pos = A['positions'].reshape(t, 1)
    me = 2 * lax.axis_index("x") + lax.axis_index("y")

    ffn = [f'ffn{k}_{w}' for k in (1, 2) for w in ('w1', 'w3', 'w2')]
    gather_groups = [[(nm, 0) for nm in ffn[:3]],
                     [('conv_w_pw1', 0), ('conv_w_pw2', 0)] + [(nm, 0) for nm in ffn[3:]],
                     [(nm, 1) for nm in ffn[:3]] + [('mla_w_a', 0), ('mla_w_uq', 0), ('mla_w_ukv', 0), ('mla_w_o', 0)],
                     [(nm, 1) for nm in ffn[3:]]]
    gather_flights = []
    for gi, group in enumerate(gather_groups):
        shards = [_bf(A[nm][l]) for nm, l in group]
        flight, _ = _exchange_start(shards, [_landing(s, me) for s in shards], False, f"gather_start_{gi}")
        gather_flights.append(flight)
    big = {}

    def gather_wait(gi, after):
        landed = _exchange_wait(gather_flights[gi], after, False, f"gather_wait_{gi}")
        big.update(zip(gather_groups[gi], landed))

    dw_shard = A['conv_w_dw'][0]
    cw = dw_shard.shape[1]
    small = jnp.concatenate([
        jnp.pad(dw_shard, ((0, CONV_HALO - CONV_WIDTH), (0, 0))),
        jnp.pad(_pad_lanes(A['mla_q_norm'], cw), ((0, 7), (0, 0))),
        jnp.pad(_pad_lanes(A['mla_kv_norm'], cw), ((0, 7), (0, 0)))], axis=0)
    small = _all_gather_rows(small, "gather_small_weights").reshape(4, 2, 48, cw)[:, 0]
    w_dw = jnp.concatenate([small[j, :CONV_HALO] for j in range(4)], axis=1)
    gq = jnp.concatenate([small[j, CONV_HALO, :Q_LORA // 4] for j in range(4)])
    gkv = jnp.concatenate([small[j, CONV_HALO + 8, :KV_LORA // 4] for j in range(4)])

    def cols(nm, layer):
        return jnp.concatenate([big[nm, layer][j] for j in range(4)], axis=1)

    def rows(nm, layer):
        g = big[nm, layer]
        return g.reshape(-1, g.shape[-1])

    ffn_w = {}

    def ffn_weights(k, l):
        ffn_w[k, l] = (cols(f'ffn{k}_w1', l), cols(f'ffn{k}_w3', l), rows(f'ffn{k}_w2', l))
        return ffn_w[k, l]

    cs_c, cs_s = _rope_tables(pos)
    h0 = x
    gather_wait(0, cs_c)
    h1, n01, z01a, z01b = _ffn_fwd(h0, A['ffn_norm1'][0], *ffn_weights(1, 0), "ffn1_l0_fwd")
    gather_wait(1, h1)
    pw1 = big['conv_w_pw1', 0]
    pw1_a = jnp.concatenate([pw1[0], pw1[1]], axis=1)
    pw1_b = jnp.concatenate([pw1[2], pw1[3]], axis=1)
    pw2 = rows('conv_w_pw2', 0)
    m0 = _norm_fwd(h1, A['mix_norm'][0], "mix_norm_l0")
    ca, cb, glu = _glu_fwd(m0, pw1_a, pw1_b)
    cv, cs = _conv_fwd(glu, w_dw, A['conv_norm'][0])
    h2 = _mm([(cs, pw2)], F32, "conv_pw2_fwd", res=h1)
    h3, n02, z02a, z02b = _ffn_fwd(h2, A['ffn_norm2'][0], *ffn_weights(2, 0), "ffn2_l0_fwd")
    gather_wait(2, h3)
    w_a = _pad_lanes(rows('mla_w_a', 0), A_PAD)
    wuq = _pad_lanes(big['mla_w_uq', 0].reshape(Q_LORA, HEADS, NOPE + ROPE).transpose(1, 0, 2), HEAD_PAD)
    wukv = big['mla_w_ukv', 0].reshape(KV_LORA, HEADS, NOPE + V_HEAD).transpose(1, 0, 2)
    w_o = rows('mla_w_o', 0)
    h4, n11, z11a, z11b = _ffn_fwd(h3, A['ffn_norm1'][1], *ffn_weights(1, 1), "ffn1_l1_fwd")
    m1 = _norm_fwd(h4, A['mix_norm'][1], "mix_norm_l1")
    a_lat = _mm([(m1, w_a)], F32, "mla_down_fwd")
    cq, ckv, kr = _mla_prep(a_lat, gq, gkv, cs_c, cs_s)
    q, k, v = _mla_qkv(cq, ckv, kr, cs_c, cs_s, wuq, wukv)
    o, lse = _flash_fwd(q, k, v)
    h5 = _mm([(o, w_o)], F32, "mla_out_fwd", res=h4)
    gather_wait(3, h5)
    h6, n12, z12a, z12b = _ffn_fwd(h5, A['ffn_norm2'][1], *ffn_weights(2, 1), "ffn2_l1_fwd")

    def col_slots(g):
        r, c4 = g.shape
        return g.reshape(r, 4, c4 // 4).transpose(1, 0, 2)

    def row_slots(g):
        return g.reshape(4, g.shape[0] // 4, g.shape[1])

    scatter_flights = []

    def scatter_start(named):
        srcs = [g for _, g in named]
        lands = [_landing(lax.dynamic_index_in_dim(g, me, 0, keepdims=False), me) for g in srcs]
        flight, token = _exchange_start(srcs, lands, True, f"scatter_start_{len(scatter_flights)}")
        scatter_flights.append(([key for key, _ in named], flight))
        return token

    def ffn_slots(k, l, dw1, dw3, dw2):
        return [((f'ffn{k}_w1', l), col_slots(dw1)), ((f'ffn{k}_w3', l), col_slots(dw3)),
                ((f'ffn{k}_w2', l), row_slots(dw2))]

    dh6, dg_final, loss_part = _loss_bwd(h6, target, A['final_norm'])
    dh5, dg_n2_l1, *dws = _ffn_bwd(dh6, h5, A['ffn_norm2'][1], n12, z12a, z12b, *ffn_w[2, 1], loss_part, "ffn2_l1")
    token = scatter_start(ffn_slots(2, 1, *dws))

    do = _mm([(dh5, w_o)], BF16, "mla_out_bwd", trans_b=True, dep=token)
    dw_o = _mm_tn(o, dh5, BF16, "mla_dw_o")
    delta = _attn_delta(do, o)
    dq, dk, dv = _flash_bwd(q, k, v, do, lse, delta)
    dr, dkv, dcq, dckv, dar = _mla_qkv_bwd(dq, dk, dv, cs_c, cs_s, wuq, wukv)
    dwuq = _mm_tn(cq, dr, BF16, "mla_dw_uq")
    dwukv = _mm_tn(ckv, dkv, BF16, "mla_dw_ukv")
    da_lat, dgq, dgkv = _mla_prep_bwd(a_lat, dcq, dckv, dar, gq, gkv)
    dh4, dg_mix_l1 = _mm_normbwd([(da_lat, w_a)], h4, A['mix_norm'][1], dh5, "mla_down_bwd")
    dw_a = _mm_tn(m1, da_lat, BF16, "mla_dw_a")
    token = scatter_start([
        (('mla_w_a', 0), row_slots(dw_a[:, :Q_LORA + KV_LORA + ROPE])),
        (('mla_w_uq', 0), dwuq[:, :, :NOPE + ROPE].transpose(1, 0, 2).reshape(4, Q_LORA // 4, HEADS, NOPE + ROPE)),
        (('mla_w_ukv', 0), dwukv.transpose(1, 0, 2).reshape(4, KV_LORA // 4, HEADS, NOPE + V_HEAD)),
        (('mla_w_o', 0), row_slots(dw_o))])

    dh3, dg_n1_l1, *dws = _ffn_bwd(dh4, h3, A['ffn_norm1'][1], n11, z11a, z11b, *ffn_w[1, 1], token, "ffn1_l1")
    token = scatter_start(ffn_slots(1, 1, *dws))
    dh2, dg_n2_l0, *dws = _ffn_bwd(dh3, h2, A['ffn_norm2'][0], n02, z02a, z02b, *ffn_w[2, 0], token, "ffn2_l0")
    token = scatter_start(ffn_slots(2, 0, *dws))

    dcv, dg_conv = _conv_bwd_norm(dh2, cv, pw2, A['conv_norm'][0], token)
    dw_pw2 = _mm_tn(cs, dh2, BF16, "conv_dw_pw2")
    dca, dcb, ddw = _conv_bwd_dw(dcv, glu, ca, cb, w_dw)
    dh1, dg_mix_l0 = _mm_normbwd([(dca, pw1_a), (dcb, pw1_b)], h1, A['mix_norm'][0], dh2, "conv_pw1_bwd")
    dpw1_a = _mm_tn(m0, dca, BF16, "conv_dw_pw1a")
    dpw1_b = _mm_tn(m0, dcb, BF16, "conv_dw_pw1b")
    half = dpw1_a.shape[1] // 2
    token = scatter_start([
        (('conv_w_pw1', 0), jnp.stack([dpw1_a[:, :half], dpw1_a[:, half:], dpw1_b[:, :half], dpw1_b[:, half:]])),
        (('conv_w_pw2', 0), row_slots(dw_pw2))])

    dx, dg_n1_l0, *dws = _ffn_bwd(dh1, h0, A['ffn_norm1'][0], n01, z01a, z01b, *ffn_w[1, 0], token, "ffn1_l0")
    after = scatter_start(ffn_slots(1, 0, *dws))

    received = {}
    for si, (keys, flight) in enumerate(scatter_flights):
        landed = _exchange_wait(flight, after, True, f"scatter_wait_{si}")
        received.update(zip(keys, landed))
        after = landed[0]
    sums = [_sum_slots([received[nm, l].reshape(4, -1, received[nm, l].shape[-1]) for l in range(A[nm].shape[0])],
                       "sum_" + nm) for nm in BIG]
    sibling = _swap_with_sibling(sums)

    out = {}
    for nm, mine, theirs in zip(BIG, sums, sibling):
        res = _adamw(_as_rows(A[nm]), _as_rows(A['m_' + nm]), _as_rows(A['v_' + nm]), [mine, theirs], "adamw_" + nm)
        out[nm] = [r.reshape(A[nm].shape) for r in res]

    qkv_row = jnp.concatenate([dgq, dgkv, jnp.zeros((8, d - Q_LORA - KV_LORA), F32)], axis=1)
    loss_row = _pad_lanes(loss_part, d)
    small_g = jnp.concatenate([dg_n1_l0, dg_n1_l1, dg_mix_l0, dg_mix_l1, dg_n2_l0, dg_n2_l1, dg_conv, dg_final,
                               qkv_row, loss_row, ddw], axis=0)
    nrow = small_g.shape[0]
    tot = _sum_devices(_all_gather_rows(small_g, "gather_small_grads").reshape(8, nrow, d), "sum_small_grads")
    loss = tot[72, 0]
    q_shard = lax.dynamic_slice_in_dim(tot[64, :Q_LORA], me * (Q_LORA // 4), Q_LORA // 4)
    kv_shard = lax.dynamic_slice_in_dim(tot[64, Q_LORA:Q_LORA + KV_LORA], me * (KV_LORA // 4), KV_LORA // 4)
    dw_shard_g = lax.dynamic_slice_in_dim(tot[80:80 + CONV_WIDTH], me * cw, cw, axis=1)
    small_grads = {
        'ffn_norm1': jnp.stack([tot[0], tot[8]]), 'mix_norm': jnp.stack([tot[16], tot[24]]),
        'ffn_norm2': jnp.stack([tot[32], tot[40]]), 'conv_norm': tot[48][None], 'final_norm': tot[56],
        'mla_q_norm': q_shard[None], 'mla_kv_norm': kv_shard[None], 'conv_w_dw': dw_shard_g[None],
    }
    for nm, g in small_grads.items():
        res = _adamw(_as_rows(A[nm]) if A[nm].ndim > 1 else A[nm].reshape(1, -1),
                     A['m_' + nm].reshape(-1, A[nm].shape[-1]), A['v_' + nm].reshape(-1, A[nm].shape[-1]),
                     [g.reshape(-1, A[nm].shape[-1])], "adamw_" + nm)
        out[nm] = [r.reshape(A[nm].shape) for r in res]

    return (loss, dx[None], *[out[nm][0] for nm in WEIGHTS], *[out[nm][1] for nm in WEIGHTS],
            *[out[nm][2] for nm in WEIGHTS], *[out[nm][3] for nm in WEIGHTS])
```

```python
import functools

import jax
import jax.numpy as jnp
import numpy as np
from jax import lax
from jax.experimental import pallas as pl
from jax.experimental.pallas import tpu as pltpu

F32 = jnp.float32
BF16 = jnp.bfloat16
MESH = pl.DeviceIdType.MESH

RMS_EPS = 1e-6
HEADS = 8
NOPE = 128
ROPE = 64
HEAD_PAD = 256
V_HEAD = 128
Q_LORA = 512
KV_LORA = 256
A_PAD = 896
CHUNK = 64
CONV_WIDTH = 31
CONV_HALO = 32
CONV_ROWS = 16
ROPE_THETA = 10000.0
ATTN_SCALE = (NOPE + ROPE) ** -0.5
FFN_RES = 0.5

ADAM_LR = 0.001
ADAM_B1 = 0.9
ADAM_B2 = 0.999
ADAM_EPS = 1e-08
ADAM_WD = 0.01
ADAM_STEP = 10

VMEM_LIMIT_BYTES = 56 * 1024 * 1024

WEIGHTS = ['ffn_norm1', 'ffn1_w1', 'ffn1_w3', 'ffn1_w2', 'mix_norm', 'ffn_norm2', 'ffn2_w1', 'ffn2_w3', 'ffn2_w2',
           'conv_w_pw1', 'conv_w_dw', 'conv_norm', 'conv_w_pw2', 'mla_w_a', 'mla_q_norm', 'mla_kv_norm', 'mla_w_uq',
           'mla_w_ukv', 'mla_w_o', 'final_norm']
INPUTS = (['x', 'positions'] + WEIGHTS + ['loss_target'] + ['m_' + w for w in WEIGHTS] + ['v_' + w for w in WEIGHTS])
BIG = ['ffn1_w1', 'ffn1_w3', 'ffn1_w2', 'ffn2_w1', 'ffn2_w3', 'ffn2_w2', 'conv_w_pw1', 'conv_w_pw2', 'mla_w_a',
       'mla_w_uq', 'mla_w_ukv', 'mla_w_o']


def _params(*sem):
    return pltpu.CompilerParams(dimension_semantics=sem, vmem_limit_bytes=VMEM_LIMIT_BYTES)


def _bf(v):
    return v.astype(BF16)


def _rstd(x):
    return lax.rsqrt(jnp.mean(x * x, axis=-1, keepdims=True) + RMS_EPS)


def _sigmoid(x):
    return jax.nn.sigmoid(x)


def _rot(x):
    lane = lax.broadcasted_iota(jnp.int32, x.shape, 1)
    return jnp.where(lane < ROPE // 2, -pltpu.roll(x, 128 - ROPE // 2, 1), pltpu.roll(x, ROPE // 2, 1))


def _rot_t(y):
    lane = lax.broadcasted_iota(jnp.int32, y.shape, 1)
    return jnp.where(lane < ROPE // 2, pltpu.roll(y, 128 - ROPE // 2, 1), -pltpu.roll(y, ROPE // 2, 1))


def _pair_sum(a_refs, b_refs, trans_b):
    tot = None
    for a_r, b_r in zip(a_refs, b_refs):
        a, b = _bf(a_r[...]), _bf(b_r[...])
        if trans_b:
            d = lax.dot_general(a, b, (((1,), (1,)), ((), ())), preferred_element_type=F32)
        else:
            d = jnp.dot(a, b, preferred_element_type=F32)
        tot = d if tot is None else tot + d
    return tot


def _mm(pairs, out_dtype, name, *, trans_b=False, tm=512, tn=None, tk=None, res=None, dep=None):
    m, k = pairs[0][0].shape
    n = pairs[0][1].shape[0] if trans_b else pairs[0][1].shape[1]
    tm, tn, tk = min(tm, m), tn or n, tk or k
    nk, npair = k // tk, len(pairs)

    def kern(*refs):
        a_refs, b_refs = refs[:npair], refs[npair:2 * npair]
        rest = list(refs[2 * npair:])
        res_ref = rest.pop(0) if res is not None else None
        if dep is not None:
            rest.pop(0)
        o_ref = rest.pop(0)

        def finish(acc):
            if res_ref is not None:
                acc = res_ref[...] + acc
            o_ref[...] = acc.astype(o_ref.dtype)

        if nk == 1:
            finish(_pair_sum(a_refs, b_refs, trans_b))
        else:
            acc_ref = rest.pop(0)
            kk = pl.program_id(2)

            @pl.when(kk == 0)
            def _():
                acc_ref[...] = jnp.zeros_like(acc_ref)

            acc_ref[...] += _pair_sum(a_refs, b_refs, trans_b)

            @pl.when(kk == nk - 1)
            def _():
                finish(acc_ref[...])

    a_spec = pl.BlockSpec((tm, tk), lambda i, j, kk: (i, kk))
    b_spec = (pl.BlockSpec((tn, tk), lambda i, j, kk: (j, kk)) if trans_b
              else pl.BlockSpec((tk, tn), lambda i, j, kk: (kk, j)))
    io_spec = pl.BlockSpec((tm, tn), lambda i, j, kk: (i, j))
    in_specs = ([a_spec] * npair + [b_spec] * npair + ([io_spec] if res is not None else [])
                + ([pl.BlockSpec((8, 128), lambda i, j, kk: (0, 0))] if dep is not None else []))
    args = ([p[0] for p in pairs] + [p[1] for p in pairs] + ([res] if res is not None else [])
            + ([dep] if dep is not None else []))
    return pl.pallas_call(
        kern, name=name, grid=(m // tm, n // tn, nk), in_specs=in_specs, out_specs=io_spec,
        out_shape=jax.ShapeDtypeStruct((m, n), out_dtype),
        scratch_shapes=[pltpu.VMEM((tm, tn), F32)] if nk > 1 else [],
        compiler_params=_params("parallel", "parallel", "arbitrary"))(*args)


def _mm_normbwd(pairs, h, g, dres, name, *, tm=512, tk=None):
    m, k = pairs[0][0].shape
    d = pairs[0][1].shape[0]
    tm, tk = min(tm, m), tk or k
    nk, npair = k // tk, len(pairs)

    def kern(*refs):
        a_refs, b_refs = refs[:npair], refs[npair:2 * npair]
        h_ref, g_ref, dres_ref, o_ref, dg_ref, acc_ref = refs[2 * npair:]
        i, kk = pl.program_id(0), pl.program_id(1)

        @pl.when(jnp.logical_and(i == 0, kk == 0))
        def _():
            dg_ref[...] = jnp.zeros_like(dg_ref)

        @pl.when(kk == 0)
        def _():
            acc_ref[...] = jnp.zeros_like(acc_ref)

        acc_ref[...] += _pair_sum(a_refs, b_refs, True)

        @pl.when(kk == nk - 1)
        def _():
            dn = acc_ref[...]
            x = h_ref[...]
            rstd = _rstd(x)
            xhat = x * rstd
            dg_ref[...] += jnp.broadcast_to(jnp.sum(dn * xhat, axis=0, keepdims=True), dg_ref.shape)
            dxh = dn * g_ref[...]
            dx = rstd * (dxh - xhat * jnp.mean(dxh * xhat, axis=-1, keepdims=True))
            o_ref[...] = dres_ref[...] + dx

    row = pl.BlockSpec((tm, d), lambda i, kk: (i, 0))
    in_specs = ([pl.BlockSpec((tm, tk), lambda i, kk: (i, kk))] * npair
                + [pl.BlockSpec((d, tk), lambda i, kk: (0, kk))] * npair
                + [row, pl.BlockSpec((1, d), lambda i, kk: (0, 0)), row])
    return pl.pallas_call(
        kern, name=name, grid=(m // tm, nk), in_specs=in_specs,
        out_specs=[row, pl.BlockSpec((8, d), lambda i, kk: (0, 0))],
        out_shape=[jax.ShapeDtypeStruct((m, d), F32), jax.ShapeDtypeStruct((8, d), F32)],
        scratch_shapes=[pltpu.VMEM((tm, d), F32)],
        compiler_params=_params("arbitrary", "arbitrary"))(
            *[p[0] for p in pairs], *[p[1] for p in pairs], h, g.reshape(1, d), dres)


def _mm_tn(a, b, out_dtype, name, *, bm=None, bn=None, tk=512):
    t, m = a.shape
    batched = b.ndim == 3
    n = b.shape[-1]
    nb = b.shape[0] if batched else 1
    bm, bn, tk = bm or m, bn or n, min(tk, t)
    nk = t // tk

    def kern(a_ref, b_ref, o_ref, acc_ref):
        kk = pl.program_id(3)

        @pl.when(kk == 0)
        def _():
            acc_ref[...] = jnp.zeros_like(acc_ref)

        acc_ref[...] += lax.dot_general(_bf(a_ref[...]), _bf(b_ref[...]), (((0,), (0,)), ((), ())),
                                        preferred_element_type=F32)

        @pl.when(kk == nk - 1)
        def _():
            o_ref[...] = acc_ref[...].astype(o_ref.dtype)

    a_spec = pl.BlockSpec((tk, bm), lambda h, i, j, kk: (kk, i))
    if batched:
        b_spec = pl.BlockSpec((None, tk, bn), lambda h, i, j, kk: (h, kk, j))
        o_spec = pl.BlockSpec((None, bm, bn), lambda h, i, j, kk: (h, i, j))
        out_shape = jax.ShapeDtypeStruct((nb, m, n), out_dtype)
    else:
        b_spec = pl.BlockSpec((tk, bn), lambda h, i, j, kk: (kk, j))
        o_spec = pl.BlockSpec((bm, bn), lambda h, i, j, kk: (i, j))
        out_shape = jax.ShapeDtypeStruct((m, n), out_dtype)
    return pl.pallas_call(
        kern, name=name, grid=(nb, m // bm, n // bn, nk), in_specs=[a_spec, b_spec], out_specs=o_spec,
        out_shape=out_shape, scratch_shapes=[pltpu.VMEM((bm, bn), F32)],
        compiler_params=_params("parallel", "parallel", "parallel", "arbitrary"))(a, b)


def _ffn_tile(f):
    return f // 2 if (f // 2) % 128 == 0 else f


def _ffn_fwd(h, g, w1, w3, w2, name):
    t, d = h.shape
    f = w1.shape[1]
    tm, tf = min(512, t), _ffn_tile(f)
    nf = f // tf

    def kern(h_ref, g_ref, w1_ref, w3_ref, w2_ref, ho_ref, n_ref, z1_ref, z3_ref, n_sc, acc_ref):
        j = pl.program_id(1)

        @pl.when(j == 0)
        def _():
            x = h_ref[...]
            n = _bf(x * _rstd(x) * g_ref[...])
            n_sc[...] = n
            n_ref[...] = n
            acc_ref[...] = jnp.zeros_like(acc_ref)

        n = n_sc[...]
        z1 = jnp.dot(n, w1_ref[...], preferred_element_type=F32)
        z3 = jnp.dot(n, w3_ref[...], preferred_element_type=F32)
        z1_ref[...] = _bf(z1)
        z3_ref[...] = _bf(z3)
        act = _bf(z1 * _sigmoid(z1) * z3)
        acc_ref[...] += jnp.dot(act, w2_ref[...], preferred_element_type=F32)

        @pl.when(j == nf - 1)
        def _():
            ho_ref[...] = h_ref[...] + FFN_RES * acc_ref[...]

    row = pl.BlockSpec((tm, d), lambda i, j: (i, 0))
    col = pl.BlockSpec((tm, tf), lambda i, j: (i, j))
    return pl.pallas_call(
        kern, name=name, grid=(t // tm, nf),
        in_specs=[row, pl.BlockSpec((1, d), lambda i, j: (0, 0)), pl.BlockSpec((d, tf), lambda i, j: (0, j)),
                  pl.BlockSpec((d, tf), lambda i, j: (0, j)), pl.BlockSpec((tf, d), lambda i, j: (j, 0))],
        out_specs=[row, row, col, col],
        out_shape=[jax.ShapeDtypeStruct((t, d), F32), jax.ShapeDtypeStruct((t, d), BF16),
                   jax.ShapeDtypeStruct((t, f), BF16), jax.ShapeDtypeStruct((t, f), BF16)],
        scratch_shapes=[pltpu.VMEM((tm, d), BF16), pltpu.VMEM((tm, d), F32)],
        compiler_params=_params("parallel", "arbitrary"))(h, g.reshape(1, d), w1, w3, w2)


def _ffn_bwd_act(dh, z1, z3, w2, dep, name):
    t, d = dh.shape
    f = z1.shape[1]
    tm, tf = min(512, t), _ffn_tile(f)

    def kern(dh_ref, z1_ref, z3_ref, w2_ref, dep_ref, dz1_ref, dz3_ref, a_ref, df_ref, df_sc):
        @pl.when(pl.program_id(1) == 0)
        def _():
            df = _bf(FFN_RES * dh_ref[...])
            df_sc[...] = df
            df_ref[...] = df

        da = lax.dot_general(df_sc[...], w2_ref[...], (((1,), (1,)), ((), ())), preferred_element_type=F32)
        z1v, z3v = z1_ref[...].astype(F32), z3_ref[...].astype(F32)
        sig = _sigmoid(z1v)
        silu = z1v * sig
        a_ref[...] = _bf(silu * z3v)
        dz1_ref[...] = _bf(da * z3v * (sig * (1.0 + z1v * (1.0 - sig))))
        dz3_ref[...] = _bf(da * silu)

    row = pl.BlockSpec((tm, d), lambda i, j: (i, 0))
    col = pl.BlockSpec((tm, tf), lambda i, j: (i, j))
    colshape = jax.ShapeDtypeStruct((t, f), BF16)
    return pl.pallas_call(
        kern, name=name, grid=(t // tm, f // tf),
        in_specs=[row, col, col, pl.BlockSpec((tf, d), lambda i, j: (j, 0)),
                  pl.BlockSpec((8, 128), lambda i, j: (0, 0))],
        out_specs=[col, col, col, row],
        out_shape=[colshape, colshape, colshape, jax.ShapeDtypeStruct((t, d), BF16)],
        scratch_shapes=[pltpu.VMEM((tm, d), BF16)],
        compiler_params=_params("parallel", "arbitrary"))(dh, z1, z3, w2, dep)


def _ffn_bwd(dh, h_in, g, n, z1, z3, w1, w3, w2, dep, tag):
    f = w1.shape[1]
    dz1, dz3, act, df = _ffn_bwd_act(dh, z1, z3, w2, dep, tag + "_bwd_act")
    dh_in, dg = _mm_normbwd([(dz1, w1), (dz3, w3)], h_in, g, dh, tag + "_bwd_dn", tk=_ffn_tile(f))
    dw1 = _mm_tn(n, dz1, BF16, tag + "_dw1", bn=_ffn_tile(f))
    dw3 = _mm_tn(n, dz3, BF16, tag + "_dw3", bn=_ffn_tile(f))
    dw2 = _mm_tn(act, df, BF16, tag + "_dw2", bm=_ffn_tile(f))
    return dh_in, dg, dw1, dw3, dw2


def _norm_fwd(h, g, name):
    t, d = h.shape
    tm = min(512, t)

    def kern(h_ref, g_ref, o_ref):
        x = h_ref[...]
        o_ref[...] = _bf(x * _rstd(x) * g_ref[...])

    row = pl.BlockSpec((tm, d), lambda i: (i, 0))
    return pl.pallas_call(
        kern, name=name, grid=(t // tm,), in_specs=[row, pl.BlockSpec((1, d), lambda i: (0, 0))], out_specs=row,
        out_shape=jax.ShapeDtypeStruct((t, d), BF16), compiler_params=_params("parallel"))(h, g.reshape(1, d))


def _loss_bwd(h, target, g):
    t, d = h.shape
    tm = min(512, t)

    def kern(h_ref, t_ref, g_ref, dh_ref, dg_ref, loss_ref):
        @pl.when(pl.program_id(0) == 0)
        def _():
            dg_ref[...] = jnp.zeros_like(dg_ref)
            loss_ref[...] = jnp.zeros_like(loss_ref)

        x = h_ref[...]
        rstd = _rstd(x)
        xhat = x * rstd
        err = xhat * g_ref[...] - t_ref[...]
        row_loss = jnp.sum(err * err, axis=-1, keepdims=True) * (0.5 / d)
        loss_ref[...] += jnp.broadcast_to(jnp.sum(row_loss, axis=0, keepdims=True), loss_ref.shape)
        dy = err * (1.0 / d)
        dg_ref[...] += jnp.broadcast_to(jnp.sum(dy * xhat, axis=0, keepdims=True), dg_ref.shape)
        dxh = dy * g_ref[...]
        dh_ref[...] = rstd * (dxh - xhat * jnp.mean(dxh * xhat, axis=-1, keepdims=True))

    row = pl.BlockSpec((tm, d), lambda i: (i, 0))
    return pl.pallas_call(
        kern, name="loss_bwd", grid=(t // tm,),
        in_specs=[row, row, pl.BlockSpec((1, d), lambda i: (0, 0))],
        out_specs=[row, pl.BlockSpec((8, d), lambda i: (0, 0)), pl.BlockSpec((8, 128), lambda i: (0, 0))],
        out_shape=[jax.ShapeDtypeStruct((t, d), F32), jax.ShapeDtypeStruct((8, d), F32),
                   jax.ShapeDtypeStruct((8, 128), F32)],
        compiler_params=_params("arbitrary"))(h, target, g.reshape(1, d))


def _glu_fwd(m, wa, wb):
    t, d = m.shape
    c = wa.shape[1]
    tm, tc = min(512, t), min(512, c)

    def kern(m_ref, wa_ref, wb_ref, a_ref, b_ref, glu_ref):
        mv = m_ref[...]
        a = jnp.dot(mv, wa_ref[...], preferred_element_type=F32)
        b = jnp.dot(mv, wb_ref[...], preferred_element_type=F32)
        a_ref[...] = _bf(a)
        b_ref[...] = _bf(b)
        glu_ref[...] = _bf(a * _sigmoid(b))

    col = pl.BlockSpec((tm, tc), lambda i, j: (i, j))
    wspec = pl.BlockSpec((d, tc), lambda i, j: (0, j))
    shape = jax.ShapeDtypeStruct((t, c), BF16)
    return pl.pallas_call(
        kern, name="conv_glu_fwd", grid=(t // tm, c // tc),
        in_specs=[pl.BlockSpec((tm, d), lambda i, j: (i, 0)), wspec, wspec], out_specs=[col, col, col],
        out_shape=[shape, shape, shape], compiler_params=_params("parallel", "parallel"))(m, wa, wb)


def _conv_tile(t):
    return min(256, t)


def _conv_fwd(glu, w_dw, g):
    t, c = glu.shape
    tm = _conv_tile(t)
    hb = tm // CONV_HALO

    def kern(cur_ref, halo_ref, w_ref, g_ref, cv_ref, s_ref, ext):
        i = pl.program_id(0)
        ext[0:CONV_HALO, :] = jnp.where(i > 0, halo_ref[...].astype(F32), 0.0)
        ext[CONV_HALO:, :] = cur_ref[...].astype(F32)
        gv = g_ref[...]
        for r0 in range(0, tm, CONV_ROWS):
            acc = jnp.zeros((CONV_ROWS, c), F32)
            for k in range(CONV_WIDTH):
                acc = acc + ext[pl.ds(r0 + 2 + k, CONV_ROWS), :] * w_ref[k:k + 1, :]
            cv_ref[r0:r0 + CONV_ROWS, :] = acc
            rn = acc * _rstd(acc) * gv
            s_ref[r0:r0 + CONV_ROWS, :] = _bf(rn * _sigmoid(rn))

    row = pl.BlockSpec((tm, c), lambda i: (i, 0))
    return pl.pallas_call(
        kern, name="conv_fwd", grid=(t // tm,),
        in_specs=[row, pl.BlockSpec((CONV_HALO, c), lambda i: (jnp.maximum(i * hb - 1, 0), 0)),
                  pl.BlockSpec((CONV_HALO, c), lambda i: (0, 0)), pl.BlockSpec((1, c), lambda i: (0, 0))],
        out_specs=[row, row],
        out_shape=[jax.ShapeDtypeStruct((t, c), F32), jax.ShapeDtypeStruct((t, c), BF16)],
        scratch_shapes=[pltpu.VMEM((tm + CONV_HALO, c), F32)],
        compiler_params=_params("parallel"))(glu, glu, w_dw, g.reshape(1, c))


def _conv_bwd_norm(dh, cv, w_pw2, g, dep):
    t, c = cv.shape
    tm = min(512, t)

    def kern(dh_ref, cv_ref, w_ref, g_ref, dep_ref, dcv_ref, dg_ref):
        @pl.when(pl.program_id(0) == 0)
        def _():
            dg_ref[...] = jnp.zeros_like(dg_ref)

        ds = lax.dot_general(_bf(dh_ref[...]), w_ref[...], (((1,), (1,)), ((), ())), preferred_element_type=F32)
        x = cv_ref[...]
        rstd = _rstd(x)
        xhat = x * rstd
        rn = xhat * g_ref[...]
        sig = _sigmoid(rn)
        drn = ds * (sig * (1.0 + rn * (1.0 - sig)))
        dg_ref[...] += jnp.broadcast_to(jnp.sum(drn * xhat, axis=0, keepdims=True), dg_ref.shape)
        dxh = drn * g_ref[...]
        dcv_ref[...] = rstd * (dxh - xhat * jnp.mean(dxh * xhat, axis=-1, keepdims=True))

    row = pl.BlockSpec((tm, c), lambda i: (i, 0))
    return pl.pallas_call(
        kern, name="conv_bwd_norm", grid=(t // tm,),
        in_specs=[pl.BlockSpec((tm, dh.shape[1]), lambda i: (i, 0)), row,
                  pl.BlockSpec(w_pw2.shape, lambda i: (0, 0)), pl.BlockSpec((1, c), lambda i: (0, 0)),
                  pl.BlockSpec((8, 128), lambda i: (0, 0))],
        out_specs=[row, pl.BlockSpec((8, c), lambda i: (0, 0))],
        out_shape=[jax.ShapeDtypeStruct((t, c), F32), jax.ShapeDtypeStruct((8, c), F32)],
        compiler_params=_params("arbitrary"))(dh, cv, w_pw2, g.reshape(1, c), dep)


def _conv_bwd_dw(dcv, glu, a, b, w_dw):
    t, c = dcv.shape
    tm = _conv_tile(t)
    hb = tm // CONV_HALO
    last = t // CONV_HALO - 1

    def kern(dcv_ref, dnext_ref, glu_ref, gprev_ref, a_ref, b_ref, w_ref, da_ref, db_ref, dw_ref, dext, gext):
        i = pl.program_id(0)

        @pl.when(i == 0)
        def _():
            dw_ref[...] = jnp.zeros_like(dw_ref)

        dext[0:tm, :] = dcv_ref[...]
        dext[tm:, :] = jnp.where(i < t // tm - 1, dnext_ref[...], 0.0)
        gext[0:CONV_HALO, :] = jnp.where(i > 0, gprev_ref[...].astype(F32), 0.0)
        gext[CONV_HALO:, :] = glu_ref[...].astype(F32)
        for r0 in range(0, tm, CONV_ROWS):
            acc = jnp.zeros((CONV_ROWS, c), F32)
            for k in range(CONV_WIDTH):
                acc = acc + dext[pl.ds(r0 + CONV_WIDTH - 1 - k, CONV_ROWS), :] * w_ref[k:k + 1, :]
            av = a_ref[r0:r0 + CONV_ROWS, :].astype(F32)
            sig = _sigmoid(b_ref[r0:r0 + CONV_ROWS, :].astype(F32))
            da_ref[r0:r0 + CONV_ROWS, :] = _bf(acc * sig)
            db_ref[r0:r0 + CONV_ROWS, :] = _bf(acc * av * sig * (1.0 - sig))
        for k in range(CONV_WIDTH):
            acc = jnp.zeros((CONV_ROWS, c), F32)
            for r0 in range(0, tm, CONV_ROWS):
                acc = acc + gext[pl.ds(r0 + 2 + k, CONV_ROWS), :] * dext[r0:r0 + CONV_ROWS, :]
            dw_ref[k:k + 1, :] += jnp.sum(acc, axis=0, keepdims=True)

    row = pl.BlockSpec((tm, c), lambda i: (i, 0))
    shape = jax.ShapeDtypeStruct((t, c), BF16)
    return pl.pallas_call(
        kern, name="conv_bwd_dw", grid=(t // tm,),
        in_specs=[row, pl.BlockSpec((CONV_HALO, c), lambda i: (jnp.minimum((i + 1) * hb, last), 0)),
                  row, pl.BlockSpec((CONV_HALO, c), lambda i: (jnp.maximum(i * hb - 1, 0), 0)),
                  row, row, pl.BlockSpec((CONV_HALO, c), lambda i: (0, 0))],
        out_specs=[row, row, pl.BlockSpec((CONV_HALO, c), lambda i: (0, 0))],
        out_shape=[shape, shape, jax.ShapeDtypeStruct((CONV_HALO, c), F32)],
        scratch_shapes=[pltpu.VMEM((tm + CONV_HALO, c), F32), pltpu.VMEM((tm + CONV_HALO, c), F32)],
        compiler_params=_params("arbitrary"))(dcv, dcv, glu, glu, a, b, w_dw)


def _rope_tables(pos):
    t = pos.shape[0]
    tm = min(512, t)
    freq = (np.float32(ROPE_THETA) ** (np.float32(-2.0) * np.arange(ROPE // 2, dtype=np.float32)
                                       / np.float32(ROPE))).astype(np.float32)
    row = np.zeros((2, 128), np.float32)
    row[0, :ROPE] = np.concatenate([freq, freq])
    row[1, :ROPE] = 1.0

    def kern(pos_ref, f_ref, c_ref, s_ref):
        ang = pos_ref[...].astype(F32) * f_ref[0:1, :]
        mask = f_ref[1:2, :]
        c_ref[...] = jnp.cos(ang) * mask
        s_ref[...] = jnp.sin(ang) * mask

    out = pl.BlockSpec((tm, 128), lambda i: (i, 0))
    shape = jax.ShapeDtypeStruct((t, 128), F32)
    return pl.pallas_call(
        kern, name="rope_tables", grid=(t // tm,),
        in_specs=[pl.BlockSpec((tm, 1), lambda i: (i, 0)), pl.BlockSpec((2, 128), lambda i: (0, 0))],
        out_specs=[out, out], out_shape=[shape, shape], compiler_params=_params("parallel"))(pos, jnp.asarray(row))


def _mla_prep(a, gq, gkv, cs_c, cs_s):
    t = a.shape[0]
    tm = min(512, t)
    kv0, r0 = Q_LORA, Q_LORA + KV_LORA

    def kern(a_ref, gq_ref, gkv_ref, c_ref, s_ref, cq_ref, ckv_ref, kr_ref):
        aq = a_ref[:, 0:kv0]
        akv = a_ref[:, kv0:r0]
        ar = a_ref[:, r0:A_PAD]
        cq_ref[...] = _bf(aq * _rstd(aq) * gq_ref[...])
        ckv_ref[...] = _bf(akv * _rstd(akv) * gkv_ref[...])
        kr_ref[...] = _bf(ar * c_ref[...] + _rot(ar) * s_ref[...])

    def row(w):
        return pl.BlockSpec((tm, w), lambda i: (i, 0))

    def vec(w):
        return pl.BlockSpec((1, w), lambda i: (0, 0))

    return pl.pallas_call(
        kern, name="mla_prep", grid=(t // tm,),
        in_specs=[row(A_PAD), vec(Q_LORA), vec(KV_LORA), row(128), row(128)],
        out_specs=[row(Q_LORA), row(KV_LORA), row(128)],
        out_shape=[jax.ShapeDtypeStruct((t, Q_LORA), BF16), jax.ShapeDtypeStruct((t, KV_LORA), BF16),
                   jax.ShapeDtypeStruct((t, 128), BF16)],
        compiler_params=_params("parallel"))(a, gq.reshape(1, -1), gkv.reshape(1, -1), cs_c, cs_s)


def _mla_prep_bwd(a, dcq, dckv, dar, gq, gkv):
    t = a.shape[0]
    tm = min(512, t)
    kv0, r0 = Q_LORA, Q_LORA + KV_LORA

    def kern(a_ref, dcq_ref, dckv_ref, dar_ref, gq_ref, gkv_ref, da_ref, dgq_ref, dgkv_ref):
        @pl.when(pl.program_id(0) == 0)
        def _():
            dgq_ref[...] = jnp.zeros_like(dgq_ref)
            dgkv_ref[...] = jnp.zeros_like(dgkv_ref)

        def back(x, dy, g_ref, dg_ref):
            rstd = _rstd(x)
            xhat = x * rstd
            dg_ref[...] += jnp.broadcast_to(jnp.sum(dy * xhat, axis=0, keepdims=True), dg_ref.shape)
            dxh = dy * g_ref[...]
            return rstd * (dxh - xhat * jnp.mean(dxh * xhat, axis=-1, keepdims=True))

        da_ref[:, 0:kv0] = _bf(back(a_ref[:, 0:kv0], dcq_ref[...], gq_ref, dgq_ref))
        da_ref[:, kv0:r0] = _bf(back(a_ref[:, kv0:r0], dckv_ref[...], gkv_ref, dgkv_ref))
        da_ref[:, r0:A_PAD] = _bf(dar_ref[...])

    def row(w):
        return pl.BlockSpec((tm, w), lambda i: (i, 0))

    def vec(r, w):
        return pl.BlockSpec((r, w), lambda i: (0, 0))

    return pl.pallas_call(
        kern, name="mla_prep_bwd", grid=(t // tm,),
        in_specs=[row(A_PAD), row(Q_LORA), row(KV_LORA), row(128), vec(1, Q_LORA), vec(1, KV_LORA)],
        out_specs=[row(A_PAD), vec(8, Q_LORA), vec(8, KV_LORA)],
        out_shape=[jax.ShapeDtypeStruct((t, A_PAD), BF16), jax.ShapeDtypeStruct((8, Q_LORA), F32),
                   jax.ShapeDtypeStruct((8, KV_LORA), F32)],
        compiler_params=_params("arbitrary"))(a, dcq, dckv, dar, gq.reshape(1, -1), gkv.reshape(1, -1))


def _mla_qkv(cq, ckv, kr, cs_c, cs_s, wuq, wukv):
    t = cq.shape[0]
    tm = min(512, t)

    def kern(cq_ref, ckv_ref, kr_ref, c_ref, s_ref, wq_ref, wkv_ref, q_ref, k_ref, v_ref):
        r = jnp.dot(cq_ref[...], wq_ref[...], preferred_element_type=F32)
        xr = r[:, NOPE:]
        q_ref[:, 0:NOPE] = _bf(r[:, 0:NOPE] * ATTN_SCALE)
        q_ref[:, NOPE:] = _bf((xr * c_ref[...] + _rot(xr) * s_ref[...]) * ATTN_SCALE)
        kv = jnp.dot(ckv_ref[...], wkv_ref[...], preferred_element_type=F32)
        k_ref[:, 0:NOPE] = _bf(kv[:, 0:NOPE])
        k_ref[:, NOPE:] = kr_ref[...]
        v_ref[...] = _bf(kv[:, NOPE:])

    def row(w):
        return pl.BlockSpec((tm, w), lambda i, h: (i, 0))

    def head(w):
        return pl.BlockSpec((None, tm, w), lambda i, h: (h, i, 0))

    return pl.pallas_call(
        kern, name="mla_qkv", grid=(t // tm, HEADS),
        in_specs=[row(Q_LORA), row(KV_LORA), row(128), row(128), row(128),
                  pl.BlockSpec((None, Q_LORA, HEAD_PAD), lambda i, h: (h, 0, 0)),
                  pl.BlockSpec((None, KV_LORA, NOPE + V_HEAD), lambda i, h: (h, 0, 0))],
        out_specs=[head(HEAD_PAD), head(HEAD_PAD), head(V_HEAD)],
        out_shape=[jax.ShapeDtypeStruct((HEADS, t, HEAD_PAD), BF16), jax.ShapeDtypeStruct((HEADS, t, HEAD_PAD), BF16),
                   jax.ShapeDtypeStruct((HEADS, t, V_HEAD), BF16)],
        compiler_params=_params("parallel", "arbitrary"))(cq, ckv, kr, cs_c, cs_s, wuq, wukv)


def _mla_qkv_bwd(dq, dk, dv, cs_c, cs_s, wuq, wukv):
    t = dq.shape[1]
    tm = min(512, t)

    def kern(dq_ref, dk_ref, dv_ref, c_ref, s_ref, wq_ref, wkv_ref, dr_ref, dkv_ref, dcq_ref, dckv_ref, dar_ref):
        @pl.when(pl.program_id(1) == 0)
        def _():
            dcq_ref[...] = jnp.zeros_like(dcq_ref)
            dckv_ref[...] = jnp.zeros_like(dckv_ref)
            dar_ref[...] = jnp.zeros_like(dar_ref)

        cv, sv = c_ref[...], s_ref[...]
        dqx = dq_ref[:, NOPE:]
        dr_ref[:, 0:NOPE] = _bf(dq_ref[:, 0:NOPE] * ATTN_SCALE)
        dr_ref[:, NOPE:] = _bf((dqx * cv + _rot_t(dqx * sv)) * ATTN_SCALE)
        dcq_ref[...] += lax.dot_general(dr_ref[...], wq_ref[...], (((1,), (1,)), ((), ())),
                                        preferred_element_type=F32)
        dkx = dk_ref[:, NOPE:]
        dar_ref[...] += dkx * cv + _rot_t(dkx * sv)
        dkv_ref[:, 0:NOPE] = _bf(dk_ref[:, 0:NOPE])
        dkv_ref[:, NOPE:] = _bf(dv_ref[...])
        dckv_ref[...] += lax.dot_general(dkv_ref[...], wkv_ref[...], (((1,), (1,)), ((), ())),
                                         preferred_element_type=F32)

    def row(w):
        return pl.BlockSpec((tm, w), lambda i, h: (i, 0))

    def head(w):
        return pl.BlockSpec((None, tm, w), lambda i, h: (h, i, 0))

    return pl.pallas_call(
        kern, name="mla_qkv_bwd", grid=(t // tm, HEADS),
        in_specs=[head(HEAD_PAD), head(HEAD_PAD), head(V_HEAD), row(128), row(128),
                  pl.BlockSpec((None, Q_LORA, HEAD_PAD), lambda i, h: (h, 0, 0)),
                  pl.BlockSpec((None, KV_LORA, NOPE + V_HEAD), lambda i, h: (h, 0, 0))],
        out_specs=[head(HEAD_PAD), head(NOPE + V_HEAD), row(Q_LORA), row(KV_LORA), row(128)],
        out_shape=[jax.ShapeDtypeStruct((HEADS, t, HEAD_PAD), BF16),
                   jax.ShapeDtypeStruct((HEADS, t, NOPE + V_HEAD), BF16),
                   jax.ShapeDtypeStruct((t, Q_LORA), F32), jax.ShapeDtypeStruct((t, KV_LORA), F32),
                   jax.ShapeDtypeStruct((t, 128), F32)],
        compiler_params=_params("parallel", "arbitrary"))(dq, dk, dv, cs_c, cs_s, wuq, wukv)


def _attn_block(t):
    return 512 if t >= 4096 else 128


def _chunk_mask(bk, bq):
    kc = lax.broadcasted_iota(jnp.int32, (bk, bq), 0) // CHUNK
    qc = lax.broadcasted_iota(jnp.int32, (bk, bq), 1) // CHUNK
    return qc >= kc


def _flash_fwd(q, k, v):
    t = q.shape[1]
    bq = _attn_block(t)
    nq = t // bq

    def kern(q_ref, k_ref, v_ref, o_ref, lse_ref):
        i = pl.program_id(1)
        qv = q_ref[...]

        def scores(j):
            kj = k_ref[pl.ds(pl.multiple_of(j * bq, bq), bq), :]
            return lax.dot_general(kj, qv, (((1,), (1,)), ((), ())), preferred_element_type=F32)

        def update(j, st, m, l, acc):
            m_new = jnp.maximum(m, jnp.max(st, axis=0, keepdims=True))
            alpha = jnp.exp(m - m_new)
            p = jnp.exp(st - m_new)
            vj = v_ref[pl.ds(pl.multiple_of(j * bq, bq), bq), :]
            pv = lax.dot_general(vj, _bf(p), (((0,), (0,)), ((), ())), preferred_element_type=F32)
            return m_new, alpha * l + jnp.sum(p, axis=0, keepdims=True), alpha * acc + pv

        st = jnp.where(_chunk_mask(bq, bq), scores(i), -1e30)
        carry = update(i, st, jnp.full((1, bq), -1e30, F32), jnp.zeros((1, bq), F32), jnp.zeros((V_HEAD, bq), F32))
        m, l, acc = lax.fori_loop(0, i, lambda j, c: update(j, scores(j), *c), carry)
        o_ref[...] = _bf((acc / l).T)
        lse_ref[...] = jnp.broadcast_to(m + jnp.log(l), (8, bq))

    return pl.pallas_call(
        kern, name="flash_fwd", grid=(HEADS, nq),
        in_specs=[pl.BlockSpec((None, bq, HEAD_PAD), lambda h, i: (h, i, 0)),
                  pl.BlockSpec((None, t, HEAD_PAD), lambda h, i: (h, 0, 0)),
                  pl.BlockSpec((None, t, V_HEAD), lambda h, i: (h, 0, 0))],
        out_specs=[pl.BlockSpec((bq, V_HEAD), lambda h, i: (i, h)),
                   pl.BlockSpec((None, None, 8, bq), lambda h, i: (h, i, 0, 0))],
        out_shape=[jax.ShapeDtypeStruct((t, HEADS * V_HEAD), BF16), jax.ShapeDtypeStruct((HEADS, nq, 8, bq), F32)],
        compiler_params=_params("parallel", "arbitrary"))(q, k, v)


def _attn_delta(do, o):
    t = do.shape[0]
    bq = _attn_block(t)

    def kern(do_ref, o_ref, d_ref):
        prod = do_ref[...].astype(F32) * o_ref[...].astype(F32)
        d_ref[...] = jnp.broadcast_to(jnp.sum(prod.T, axis=0, keepdims=True), (8, bq))

    blk = pl.BlockSpec((bq, V_HEAD), lambda h, i: (i, h))
    return pl.pallas_call(
        kern, name="attn_delta", grid=(HEADS, t // bq), in_specs=[blk, blk],
        out_specs=pl.BlockSpec((None, None, 8, bq), lambda h, i: (h, i, 0, 0)),
        out_shape=jax.ShapeDtypeStruct((HEADS, t // bq, 8, bq), F32),
        compiler_params=_params("parallel", "parallel"))(do, o)


def _flash_bwd(q, k, v, do, lse, delta):
    t = q.shape[1]
    bq = _attn_block(t)
    nq = t // bq

    def kern(q_ref, k_ref, v_ref, do_ref, lse_ref, del_ref, dq_ref, dk_ref, dv_ref):
        j = pl.program_id(1)

        @pl.when(j == 0)
        def _():
            dq_ref[...] = jnp.zeros_like(dq_ref)

        dk_ref[...] = jnp.zeros_like(dk_ref)
        dv_ref[...] = jnp.zeros_like(dv_ref)
        kj, vj = k_ref[...], v_ref[...]

        def step(i, masked):
            rows = pl.ds(pl.multiple_of(i * bq, bq), bq)
            qi, doi = q_ref[rows, :], do_ref[rows, :]
            st = lax.dot_general(kj, qi, (((1,), (1,)), ((), ())), preferred_element_type=F32)
            pt = jnp.exp(st - lse_ref[i][0:1, :])
            if masked:
                pt = jnp.where(_chunk_mask(bq, bq), pt, 0.0)
            dpt = lax.dot_general(vj, doi, (((1,), (1,)), ((), ())), preferred_element_type=F32)
            dst = _bf(pt * (dpt - del_ref[i][0:1, :]))
            dv_ref[...] += jnp.dot(_bf(pt), doi, preferred_element_type=F32)
            dk_ref[...] += jnp.dot(dst, qi, preferred_element_type=F32)
            dq_ref[rows, :] += lax.dot_general(dst, kj, (((0,), (0,)), ((), ())), preferred_element_type=F32)

        step(j, True)

        def body(i, carry):
            step(i, False)
            return carry

        lax.fori_loop(j + 1, nq, body, 0)

    stat = pl.BlockSpec((None, nq, 8, bq), lambda h, j: (h, 0, 0, 0))
    return pl.pallas_call(
        kern, name="flash_bwd", grid=(HEADS, nq),
        in_specs=[pl.BlockSpec((None, t, HEAD_PAD), lambda h, j: (h, 0, 0)),
                  pl.BlockSpec((None, bq, HEAD_PAD), lambda h, j: (h, j, 0)),
                  pl.BlockSpec((None, bq, V_HEAD), lambda h, j: (h, j, 0)),
                  pl.BlockSpec((t, V_HEAD), lambda h, j: (0, h)), stat, stat],
        out_specs=[pl.BlockSpec((None, t, HEAD_PAD), lambda h, j: (h, 0, 0)),
                   pl.BlockSpec((None, bq, HEAD_PAD), lambda h, j: (h, j, 0)),
                   pl.BlockSpec((None, bq, V_HEAD), lambda h, j: (h, j, 0))],
        out_shape=[jax.ShapeDtypeStruct((HEADS, t, HEAD_PAD), F32), jax.ShapeDtypeStruct((HEADS, t, HEAD_PAD), F32),
                   jax.ShapeDtypeStruct((HEADS, t, V_HEAD), F32)],
        compiler_params=_params("parallel", "arbitrary"))(q, k, v, do, lse, delta)


def _place():
    x, y, c = lax.axis_index("x"), lax.axis_index("y"), lax.axis_index("c")
    return x, y, c, [(1 - x, y), (x, 1 - y), (1 - x, 1 - y)]


def _all_gather_rows(block, name):
    m_per, n = block.shape

    def body(x_ref, out_ref, send_sems, recv_sems, local_sem):
        x, y, c, chips = _place()
        me, sibling = (x, y, c), (x, y, 1 - c)

        def rows(px, py, pc):
            return out_ref.at[pl.ds((4 * px + 2 * py + pc) * m_per, m_per), :]

        def copy(k, blk, to, src=None):
            return pltpu.make_async_remote_copy(
                src_ref=rows(*blk) if src is None else src, dst_ref=rows(*blk), send_sem=send_sems.at[k],
                recv_sem=recv_sems.at[k], device_id=to, device_id_type=MESH)

        mine = pltpu.make_async_copy(x_ref, rows(*me), local_sem)
        mine.start()
        first = [copy(0, me, sibling, src=x_ref)]
        first += [copy(1 + j, me, (*chip, c), src=x_ref) for j, chip in enumerate(chips)]
        for cp in first:
            cp.start()
        passed = [copy(4 + j, (*chip, c), sibling) for j, chip in enumerate(chips)]
        for j, chip in enumerate(chips):
            copy(1 + j, (*chip, c), me).wait_recv()
            passed[j].start()
        copy(0, sibling, me).wait_recv()
        for j, chip in enumerate(chips):
            copy(4 + j, (*chip, 1 - c), me).wait_recv()
        for cp in first + passed:
            cp.wait_send()
        mine.wait()

    return pl.pallas_call(
        body, name=name, out_shape=jax.ShapeDtypeStruct((8 * m_per, n), block.dtype),
        in_specs=[pl.BlockSpec(memory_space=pltpu.VMEM)], out_specs=pl.BlockSpec(memory_space=pltpu.VMEM),
        scratch_shapes=[pltpu.SemaphoreType.DMA((7,)), pltpu.SemaphoreType.DMA((7,)), pltpu.SemaphoreType.DMA],
        compiler_params=pltpu.CompilerParams(vmem_limit_bytes=VMEM_LIMIT_BYTES))(block)


HBM_SPEC = pl.BlockSpec(memory_space=pltpu.HBM)
SEM_SPEC = pl.BlockSpec(memory_space=pltpu.SEMAPHORE)
DATAFLOW = pltpu.SideEffectType.DATAFLOW_SIDE_EFFECTING


def _in_hbm(a):
    return pltpu.with_memory_space_constraint(a, pltpu.HBM)


def _chip_copies(ins, lands, send_sems, recv_sems, src_slot):
    n = len(ins)
    x, y, c, chips = _place()
    me = 2 * x + y
    return [pltpu.make_async_remote_copy(
        src_ref=ins[w].at[2 * chip[0] + chip[1]] if src_slot else ins[w], dst_ref=lands[w].at[me],
        send_sem=send_sems.at[p * n + w], recv_sem=recv_sems.at[p * n + w], device_id=(*chip, c),
        device_id_type=MESH) for w in range(n) for p, chip in enumerate(chips)]


def _exchange_start(srcs, lands, src_slot, name):
    n = len(srcs)

    def body(*refs):
        for cp in _chip_copies(refs[:n], refs[n:2 * n], refs[2 * n], refs[2 * n + 1], src_slot):
            cp.start()
        token = refs[-1]
        token[...] = jnp.zeros_like(token)

    thru = [pltpu.HBM(a.shape, a.dtype) for a in list(srcs) + list(lands)]
    res = pl.pallas_call(
        body, name=name,
        out_shape=(pltpu.SemaphoreType.DMA((3 * n,)), pltpu.SemaphoreType.DMA((3 * n,)), *thru,
                   jax.ShapeDtypeStruct((8, 128), F32)),
        in_specs=[HBM_SPEC] * (2 * n),
        out_specs=(SEM_SPEC, SEM_SPEC, *[HBM_SPEC] * (2 * n), pl.BlockSpec(memory_space=pltpu.VMEM)),
        input_output_aliases={i: 2 + i for i in range(2 * n)},
        compiler_params=pltpu.CompilerParams(has_side_effects=DATAFLOW))(
            *[_in_hbm(a) for a in srcs], *[_in_hbm(a) for a in lands])
    return (res[0], res[1], list(res[2:2 + n]), list(res[2 + n:2 + 2 * n])), res[-1]


def _exchange_wait(flight, after, src_slot, name):
    send_sems, recv_sems, srcs, lands = flight
    n = len(srcs)

    def body(*refs):
        for cp in _chip_copies(refs[:n], refs[n:2 * n], refs[2 * n], refs[2 * n + 1], src_slot):
            cp.wait_send()
            cp.wait_recv()

    thru = [pltpu.HBM(a.shape, a.dtype) for a in list(srcs) + list(lands)]
    res = pl.pallas_call(
        body, name=name, out_shape=thru,
        in_specs=[HBM_SPEC] * (2 * n) + [SEM_SPEC, SEM_SPEC, pl.BlockSpec(memory_space=pl.ANY)],
        out_specs=[HBM_SPEC] * (2 * n), input_output_aliases={i: i for i in range(2 * n)},
        compiler_params=pltpu.CompilerParams(has_side_effects=DATAFLOW))(*srcs, *lands, send_sems, recv_sems, after)
    return list(res[n:])


def _landing(own, me):
    return lax.dynamic_update_index_in_dim(lax.empty((4, *own.shape), own.dtype), own, me, 0)


def _swap_with_sibling(arrays):
    n = len(arrays)

    def body(*refs):
        ins, outs = refs[:n], refs[n:2 * n]
        send_sems, recv_sems = refs[2 * n:]
        x, y, c, _ = _place()
        copies = [pltpu.make_async_remote_copy(src_ref=ins[w], dst_ref=outs[w], send_sem=send_sems.at[w],
                                               recv_sem=recv_sems.at[w], device_id=(x, y, 1 - c), device_id_type=MESH)
                  for w in range(n)]
        for cp in copies:
            cp.start()
        for cp in copies:
            cp.wait()

    any_spec = pl.BlockSpec(memory_space=pl.ANY)
    return pl.pallas_call(
        body, name="swap_with_sibling", out_shape=[jax.ShapeDtypeStruct(a.shape, a.dtype) for a in arrays],
        in_specs=[any_spec] * n, out_specs=[any_spec] * n,
        scratch_shapes=[pltpu.SemaphoreType.DMA((n,)), pltpu.SemaphoreType.DMA((n,))])(*arrays)


def _as_rows(a):
    return a.reshape(-1, a.shape[-1])


def _row_tile(r, c, budget_bytes=1 << 20):
    tr = r
    while tr % 16 == 0 and tr * c * 4 > budget_bytes:
        tr //= 2
    return tr


def _sum_slots(layers, name):
    _, r, c = layers[0].shape
    tr = _row_tile(r, c)
    nt = r // tr
    acc = None
    for l, r4 in enumerate(layers):
        def kern(r_ref, *rest):
            o_ref = rest[-1]
            o_ref[...] = (((r_ref[0].astype(F32) + r_ref[1].astype(F32)) + r_ref[2].astype(F32))
                          + r_ref[3].astype(F32))

        out_spec = pl.BlockSpec((tr, c), lambda i, l=l: (l * nt + i, 0))
        acc = pl.pallas_call(
            kern, name=f"{name}_l{l}", grid=(nt,),
            in_specs=[pl.BlockSpec((4, tr, c), lambda i: (0, i, 0))] + ([pl.BlockSpec(memory_space=pl.ANY)] if l else []),
            out_specs=out_spec, out_shape=jax.ShapeDtypeStruct((len(layers) * r, c), F32),
            input_output_aliases={1: 0} if l else {},
            compiler_params=_params("parallel"))(*([r4, acc] if l else [r4]))
    return acc


def _adamw(w, m, v, parts, name):
    r, c = w.shape
    tr = _row_tile(r, c, 1 << 19)
    npart = len(parts)
    c1 = 1.0 - ADAM_B1 ** ADAM_STEP
    c2 = 1.0 - ADAM_B2 ** ADAM_STEP

    def kern(*refs):
        w_ref, m_ref, v_ref = refs[:3]
        p_refs = refs[3:3 + npart]
        g_ref, d_ref, mo_ref, vo_ref = refs[3 + npart:]
        g = p_refs[0][...]
        for p in p_refs[1:]:
            g = g + p[...]
        mn = ADAM_B1 * m_ref[...] + (1.0 - ADAM_B1) * g
        vn = ADAM_B2 * v_ref[...] + (1.0 - ADAM_B2) * (g * g)
        g_ref[...] = g
        mo_ref[...] = mn
        vo_ref[...] = vn
        d_ref[...] = -ADAM_LR * ((mn / c1) / (jnp.sqrt(vn / c2) + ADAM_EPS) + ADAM_WD * w_ref[...])

    blk = pl.BlockSpec((tr, c), lambda i: (i, 0))
    shape = jax.ShapeDtypeStruct((r, c), F32)
    return pl.pallas_call(
        kern, name=name, grid=(r // tr,), in_specs=[blk] * (3 + npart), out_specs=[blk] * 4, out_shape=[shape] * 4,
        compiler_params=_params("parallel"))(w, m, v, *parts)


def _sum_devices(g8, name):
    _, r, c = g8.shape

    def kern(g_ref, o_ref):
        tot = g_ref[0]
        for dev in range(1, 8):
            tot = tot + g_ref[dev]
        o_ref[...] = tot

    return pl.pallas_call(
        kern, name=name, grid=(1,), in_specs=[pl.BlockSpec((8, r, c), lambda i: (0, 0, 0))],
        out_specs=pl.BlockSpec((r, c), lambda i: (0, 0)), out_shape=jax.ShapeDtypeStruct((r, c), F32),
        compiler_params=_params("arbitrary"))(g8)


def _pad_lanes(a, width):
    return jnp.pad(a, [(0, 0)] * (a.ndim - 1) + [(0, width - a.shape[-1])])


def kernel(x, positions, ffn_norm1, ffn1_w1, ffn1_w3, ffn1_w2, mix_norm, ffn_norm2, ffn2_w1, ffn2_w3, ffn2_w2, conv_w_pw1, conv_w_dw, conv_norm, conv_w_pw2, mla_w_a, mla_q_norm, mla_kv_norm, mla_w_uq, mla_w_ukv, mla_w_o, final_norm, loss_target, m_ffn_norm1, m_ffn1_w1, m_ffn1_w3, m_ffn1_w2, m_mix_norm, m_ffn_norm2, m_ffn2_w1, m_ffn2_w3, m_ffn2_w2, m_conv_w_pw1, m_conv_w_dw, m_conv_norm, m_conv_w_pw2, m_mla_w_a, m_mla_q_norm, m_mla_kv_norm, m_mla_w_uq, m_mla_w_ukv, m_mla_w_o, m_final_norm, v_ffn_norm1, v_ffn1_w1, v_ffn1_w3, v_ffn1_w2, v_mix_norm, v_ffn_norm2, v_ffn2_w1, v_ffn2_w3, v_ffn2_w2, v_conv_w_pw1, v_conv_w_dw, v_conv_norm, v_conv_w_pw2, v_mla_w_a, v_mla_q_norm, v_mla_kv_norm, v_mla_w_uq, v_mla_w_ukv, v_mla_w_o, v_final_norm):
    given = locals()
    return _step({nm: given[nm] for nm in INPUTS})


def _step(A):
    x = A['x'][0]
    target = A['loss_target'][0]
    t, d = x.shape
    pos = A['positions'].reshape(t, 1)
    me = 2 * lax.axis_index("x") + lax.axis_index("y")

    ffn = [f'ffn{k}_{w}' for k in (1, 2) for w in ('w1', 'w3', 'w2')]
    gather_groups = [[(nm, 0) for nm in ffn[:3]],
                     [('conv_w_pw1', 0), ('conv_w_pw2', 0)] + [(nm, 0) for nm in ffn[3:]],
                     [(nm, 1) for nm in ffn[:3]] + [('mla_w_a', 0), ('mla_w_uq', 0), ('mla_w_ukv', 0), ('mla_w_o', 0)],
                     [(nm, 1) for nm in ffn[3:]]]
    gather_flights = []
    all_started = jnp.zeros((8, 128), F32)
    for gi, group in enumerate(gather_groups):
        shards = [_bf(A[nm][l]) for nm, l in group]
        flight, token = _exchange_start(shards, [_landing(s, me) for s in shards], False, f"gather_start_{gi}")
        gather_flights.append(flight)
        all_started = all_started + token
    big = {}

    def gather_wait(gi, after):
        landed = _exchange_wait(gather_flights[gi], after, False, f"gather_wait_{gi}")
        big.update(zip(gather_groups[gi], landed))

    dw_shard = A['conv_w_dw'][0]
    cw = dw_shard.shape[1]
    small = jnp.concatenate([
        jnp.pad(dw_shard, ((0, CONV_HALO - CONV_WIDTH), (0, 0))),
        jnp.pad(_pad_lanes(A['mla_q_norm'], cw), ((0, 7), (0, 0))),
        jnp.pad(_pad_lanes(A['mla_kv_norm'], cw), ((0, 7), (0, 0)))], axis=0)
    small = _all_gather_rows(small, "gather_small_weights").reshape(4, 2, 48, cw)[:, 0]
    w_dw = jnp.concatenate([small[j, :CONV_HALO] for j in range(4)], axis=1)
    gq = jnp.concatenate([small[j, CONV_HALO, :Q_LORA // 4] for j in range(4)])
    gkv = jnp.concatenate([small[j, CONV_HALO + 8, :KV_LORA // 4] for j in range(4)])

    def cols(nm, layer):
        return jnp.concatenate([big[nm, layer][j] for j in range(4)], axis=1)

    def rows(nm, layer):
        g = big[nm, layer]
        return g.reshape(-1, g.shape[-1])

    ffn_w = {}

    def ffn_weights(k, l):
        ffn_w[k, l] = (cols(f'ffn{k}_w1', l), cols(f'ffn{k}_w3', l), rows(f'ffn{k}_w2', l))
        return ffn_w[k, l]

    cs_c, cs_s = _rope_tables(pos)
    h0 = x
    gather_wait(0, all_started)
    h1, n01, z01a, z01b = _ffn_fwd(h0, A['ffn_norm1'][0], *ffn_weights(1, 0), "ffn1_l0_fwd")
    gather_wait(1, h1)
    pw1 = big['conv_w_pw1', 0]
    pw1_a = jnp.concatenate([pw1[0], pw1[1]], axis=1)
    pw1_b = jnp.concatenate([pw1[2], pw1[3]], axis=1)
    pw2 = rows('conv_w_pw2', 0)
    m0 = _norm_fwd(h1, A['mix_norm'][0], "mix_norm_l0")
    ca, cb, glu = _glu_fwd(m0, pw1_a, pw1_b)
    cv, cs = _conv_fwd(glu, w_dw, A['conv_norm'][0])
    h2 = _mm([(cs, pw2)], F32, "conv_pw2_fwd", res=h1)
    h3, n02, z02a, z02b = _ffn_fwd(h2, A['ffn_norm2'][0], *ffn_weights(2, 0), "ffn2_l0_fwd")
    gather_wait(2, h3)
    w_a = _pad_lanes(rows('mla_w_a', 0), A_PAD)
    wuq = _pad_lanes(big['mla_w_uq', 0].reshape(Q_LORA, HEADS, NOPE + ROPE).transpose(1, 0, 2), HEAD_PAD)
    wukv = big['mla_w_ukv', 0].reshape(KV_LORA, HEADS, NOPE + V_HEAD).transpose(1, 0, 2)
    w_o = rows('mla_w_o', 0)
    h4, n11, z11a, z11b = _ffn_fwd(h3, A['ffn_norm1'][1], *ffn_weights(1, 1), "ffn1_l1_fwd")
    m1 = _norm_fwd(h4, A['mix_norm'][1], "mix_norm_l1")
    a_lat = _mm([(m1, w_a)], F32, "mla_down_fwd")
    cq, ckv, kr = _mla_prep(a_lat, gq, gkv, cs_c, cs_s)
    q, k, v = _mla_qkv(cq, ckv, kr, cs_c, cs_s, wuq, wukv)
    o, lse = _flash_fwd(q, k, v)
    h5 = _mm([(o, w_o)], F32, "mla_out_fwd", res=h4)
    gather_wait(3, h5)
    h6, n12, z12a, z12b = _ffn_fwd(h5, A['ffn_norm2'][1], *ffn_weights(2, 1), "ffn2_l1_fwd")

    def col_slots(g):
        r, c4 = g.shape
        return g.reshape(r, 4, c4 // 4).transpose(1, 0, 2)

    def row_slots(g):
        return g.reshape(4, g.shape[0] // 4, g.shape[1])

    scatter_flights = []

    def scatter_start(named):
        srcs = [g for _, g in named]
        lands = [_landing(lax.dynamic_index_in_dim(g, me, 0, keepdims=False), me) for g in srcs]
        flight, token = _exchange_start(srcs, lands, True, f"scatter_start_{len(scatter_flights)}")
        scatter_flights.append(([key for key, _ in named], flight))
        return token

    def ffn_slots(k, l, dw1, dw3, dw2):
        return [((f'ffn{k}_w1', l), col_slots(dw1)), ((f'ffn{k}_w3', l), col_slots(dw3)),
                ((f'ffn{k}_w2', l), row_slots(dw2))]

    dh6, dg_final, loss_part = _loss_bwd(h6, target, A['final_norm'])
    dh5, dg_n2_l1, *dws = _ffn_bwd(dh6, h5, A['ffn_norm2'][1], n12, z12a, z12b, *ffn_w[2, 1], loss_part, "ffn2_l1")
    token = scatter_start(ffn_slots(2, 1, *dws))

    do = _mm([(dh5, w_o)], BF16, "mla_out_bwd", trans_b=True, dep=token)
    dw_o = _mm_tn(o, dh5, BF16, "mla_dw_o")
    delta = _attn_delta(do, o)
    dq, dk, dv = _flash_bwd(q, k, v, do, lse, delta)
    dr, dkv, dcq, dckv, dar = _mla_qkv_bwd(dq, dk, dv, cs_c, cs_s, wuq, wukv)
    dwuq = _mm_tn(cq, dr, BF16, "mla_dw_uq")
    dwukv = _mm_tn(ckv, dkv, BF16, "mla_dw_ukv")
    da_lat, dgq, dgkv = _mla_prep_bwd(a_lat, dcq, dckv, dar, gq, gkv)
    dh4, dg_mix_l1 = _mm_normbwd([(da_lat, w_a)], h4, A['mix_norm'][1], dh5, "mla_down_bwd")
    dw_a = _mm_tn(m1, da_lat, BF16, "mla_dw_a")
    token = scatter_start([
        (('mla_w_a', 0), row_slots(dw_a[:, :Q_LORA + KV_LORA + ROPE])),
        (('mla_w_uq', 0), dwuq[:, :, :NOPE + ROPE].transpose(1, 0, 2).reshape(4, Q_LORA // 4, HEADS, NOPE + ROPE)),
        (('mla_w_ukv', 0), dwukv.transpose(1, 0, 2).reshape(4, KV_LORA // 4, HEADS, NOPE + V_HEAD)),
        (('mla_w_o', 0), row_slots(dw_o))])

    dh3, dg_n1_l1, *dws = _ffn_bwd(dh4, h3, A['ffn_norm1'][1], n11, z11a, z11b, *ffn_w[1, 1], token, "ffn1_l1")
    token = scatter_start(ffn_slots(1, 1, *dws))
    dh2, dg_n2_l0, *dws = _ffn_bwd(dh3, h2, A['ffn_norm2'][0], n02, z02a, z02b, *ffn_w[2, 0], token, "ffn2_l0")
    token = scatter_start(ffn_slots(2, 0, *dws))

    dcv, dg_conv = _conv_bwd_norm(dh2, cv, pw2, A['conv_norm'][0], token)
    dw_pw2 = _mm_tn(cs, dh2, BF16, "conv_dw_pw2")
    dca, dcb, ddw = _conv_bwd_dw(dcv, glu, ca, cb, w_dw)
    dh1, dg_mix_l0 = _mm_normbwd([(dca, pw1_a), (dcb, pw1_b)], h1, A['mix_norm'][0], dh2, "conv_pw1_bwd")
    dpw1_a = _mm_tn(m0, dca, BF16, "conv_dw_pw1a")
    dpw1_b = _mm_tn(m0, dcb, BF16, "conv_dw_pw1b")
    half = dpw1_a.shape[1] // 2
    token = scatter_start([
        (('conv_w_pw1', 0), jnp.stack([dpw1_a[:, :half], dpw1_a[:, half:], dpw1_b[:, :half], dpw1_b[:, half:]])),
        (('conv_w_pw2', 0), row_slots(dw_pw2))])

    dx, dg_n1_l0, *dws = _ffn_bwd(dh1, h0, A['ffn_norm1'][0], n01, z01a, z01b, *ffn_w[1, 0], token, "ffn1_l0")
    after = scatter_start(ffn_slots(1, 0, *dws))

    received = {}
    for si, (keys, flight) in enumerate(scatter_flights):
        landed = _exchange_wait(flight, after, True, f"scatter_wait_{si}")
        received.update(zip(keys, landed))
        after = landed[0]
    sums = [_sum_slots([received[nm, l].reshape(4, -1, received[nm, l].shape[-1]) for l in range(A[nm].shape[0])],
                       "sum_" + nm) for nm in BIG]
    sibling = _swap_with_sibling(sums)

    out = {}
    for nm, mine, theirs in zip(BIG, sums, sibling):
        res = _adamw(_as_rows(A[nm]), _as_rows(A['m_' + nm]), _as_rows(A['v_' + nm]), [mine, theirs], "adamw_" + nm)
        out[nm] = [r.reshape(A[nm].shape) for r in res]

    qkv_row = jnp.concatenate([dgq, dgkv, jnp.zeros((8, d - Q_LORA - KV_LORA), F32)], axis=1)
    loss_row = _pad_lanes(loss_part, d)
    small_g = jnp.concatenate([dg_n1_l0, dg_n1_l1, dg_mix_l0, dg_mix_l1, dg_n2_l0, dg_n2_l1, dg_conv, dg_final,
                               qkv_row, loss_row, ddw], axis=0)
    nrow = small_g.shape[0]
    tot = _sum_devices(_all_gather_rows(small_g, "gather_small_grads").reshape(8, nrow, d), "sum_small_grads")
    loss = tot[72, 0]
    q_shard = lax.dynamic_slice_in_dim(tot[64, :Q_LORA], me * (Q_LORA // 4), Q_LORA // 4)
    kv_shard = lax.dynamic_slice_in_dim(tot[64, Q_LORA:Q_LORA + KV_LORA], me * (KV_LORA // 4), KV_LORA // 4)
    dw_shard_g = lax.dynamic_slice_in_dim(tot[80:80 + CONV_WIDTH], me * cw, cw, axis=1)
    small_grads = {
        'ffn_norm1': jnp.stack([tot[0], tot[8]]), 'mix_norm': jnp.stack([tot[16], tot[24]]),
        'ffn_norm2': jnp.stack([tot[32], tot[40]]), 'conv_norm': tot[48][None], 'final_norm': tot[56],
        'mla_q_norm': q_shard[None], 'mla_kv_norm': kv_shard[None], 'conv_w_dw': dw_shard_g[None],
    }
    for nm, g in small_grads.items():
        res = _adamw(_as_rows(A[nm]) if A[nm].ndim > 1 else A[nm].reshape(1, -1),
                     A['m_' + nm].reshape(-1, A[nm].shape[-1]), A['v_' + nm].reshape(-1, A[nm].shape[-1]),
                     [g.reshape(-1, A[nm].shape[-1])], "adamw_" + nm)
        out[nm] = [r.reshape(A[nm].shape) for r in res]

    return (loss, dx[None], *[out[nm][0] for nm in WEIGHTS], *[out[nm][1] for nm in WEIGHTS],
            *[out[nm][2] for nm in WEIGHTS], *[out[nm][3] for nm in WEIGHTS])
```

```python
import functools

import jax
import jax.numpy as jnp
import numpy as np
from jax import lax
from jax.experimental import pallas as pl
from jax.experimental.pallas import tpu as pltpu

F32 = jnp.float32
BF16 = jnp.bfloat16
MESH = pl.DeviceIdType.MESH

RMS_EPS = 1e-6
HEADS = 8
NOPE = 128
ROPE = 64
HEAD_PAD = 256
V_HEAD = 128
Q_LORA = 512
KV_LORA = 256
A_PAD = 896
CHUNK = 64
CONV_WIDTH = 31
CONV_HALO = 32
CONV_ROWS = 16
ROPE_THETA = 10000.0
ATTN_SCALE = (NOPE + ROPE) ** -0.5
FFN_RES = 0.5

ADAM_LR = 0.001
ADAM_B1 = 0.9
ADAM_B2 = 0.999
ADAM_EPS = 1e-08
ADAM_WD = 0.01
ADAM_STEP = 10

VMEM_LIMIT_BYTES = 56 * 1024 * 1024

WEIGHTS = ['ffn_norm1', 'ffn1_w1', 'ffn1_w3', 'ffn1_w2', 'mix_norm', 'ffn_norm2', 'ffn2_w1', 'ffn2_w3', 'ffn2_w2',
           'conv_w_pw1', 'conv_w_dw', 'conv_norm', 'conv_w_pw2', 'mla_w_a', 'mla_q_norm', 'mla_kv_norm', 'mla_w_uq',
           'mla_w_ukv', 'mla_w_o', 'final_norm']
INPUTS = (['x', 'positions'] + WEIGHTS + ['loss_target'] + ['m_' + w for w in WEIGHTS] + ['v_' + w for w in WEIGHTS])
BIG = ['ffn1_w1', 'ffn1_w3', 'ffn1_w2', 'ffn2_w1', 'ffn2_w3', 'ffn2_w2', 'conv_w_pw1', 'conv_w_pw2', 'mla_w_a',
       'mla_w_uq', 'mla_w_ukv', 'mla_w_o']


def _params(*sem):
    return pltpu.CompilerParams(dimension_semantics=sem, vmem_limit_bytes=VMEM_LIMIT_BYTES)


def _bf(v):
    return v.astype(BF16)


def _rstd(x):
    return lax.rsqrt(jnp.mean(x * x, axis=-1, keepdims=True) + RMS_EPS)


def _sigmoid(x):
    return jax.nn.sigmoid(x)


def _rot(x):
    lane = lax.broadcasted_iota(jnp.int32, x.shape, 1)
    return jnp.where(lane < ROPE // 2, -pltpu.roll(x, 128 - ROPE // 2, 1), pltpu.roll(x, ROPE // 2, 1))


def _rot_t(y):
    lane = lax.broadcasted_iota(jnp.int32, y.shape, 1)
    return jnp.where(lane < ROPE // 2, pltpu.roll(y, 128 - ROPE // 2, 1), -pltpu.roll(y, ROPE // 2, 1))


def _pair_sum(a_refs, b_refs, trans_b):
    tot = None
    for a_r, b_r in zip(a_refs, b_refs):
        a, b = _bf(a_r[...]), _bf(b_r[...])
        if trans_b:
            d = lax.dot_general(a, b, (((1,), (1,)), ((), ())), preferred_element_type=F32)
        else:
            d = jnp.dot(a, b, preferred_element_type=F32)
        tot = d if tot is None else tot + d
    return tot


def _mm(pairs, out_dtype, name, *, trans_b=False, tm=512, tn=None, tk=None, res=None, dep=None):
    m, k = pairs[0][0].shape
    n = pairs[0][1].shape[0] if trans_b else pairs[0][1].shape[1]
    tm, tn, tk = min(tm, m), tn or n, tk or k
    nk, npair = k // tk, len(pairs)

    def kern(*refs):
        a_refs, b_refs = refs[:npair], refs[npair:2 * npair]
        rest = list(refs[2 * npair:])
        res_ref = rest.pop(0) if res is not None else None
        if dep is not None:
            rest.pop(0)
        o_ref = rest.pop(0)

        def finish(acc):
            if res_ref is not None:
                acc = res_ref[...] + acc
            o_ref[...] = acc.astype(o_ref.dtype)

        if nk == 1:
            finish(_pair_sum(a_refs, b_refs, trans_b))
        else:
            acc_ref = rest.pop(0)
            kk = pl.program_id(2)

            @pl.when(kk == 0)
            def _():
                acc_ref[...] = jnp.zeros_like(acc_ref)

            acc_ref[...] += _pair_sum(a_refs, b_refs, trans_b)

            @pl.when(kk == nk - 1)
            def _():
                finish(acc_ref[...])

    a_spec = pl.BlockSpec((tm, tk), lambda i, j, kk: (i, kk))
    b_spec = (pl.BlockSpec((tn, tk), lambda i, j, kk: (j, kk)) if trans_b
              else pl.BlockSpec((tk, tn), lambda i, j, kk: (kk, j)))
    io_spec = pl.BlockSpec((tm, tn), lambda i, j, kk: (i, j))
    in_specs = ([a_spec] * npair + [b_spec] * npair + ([io_spec] if res is not None else [])
                + ([pl.BlockSpec((8, 128), lambda i, j, kk: (0, 0))] if dep is not None else []))
    args = ([p[0] for p in pairs] + [p[1] for p in pairs] + ([res] if res is not None else [])
            + ([dep] if dep is not None else []))
    return pl.pallas_call(
        kern, name=name, grid=(m // tm, n // tn, nk), in_specs=in_specs, out_specs=io_spec,
        out_shape=jax.ShapeDtypeStruct((m, n), out_dtype),
        scratch_shapes=[pltpu.VMEM((tm, tn), F32)] if nk > 1 else [],
        compiler_params=_params("parallel", "parallel", "arbitrary"))(*args)


def _mm_normbwd(pairs, h, g, dres, dep, name, *, tm=512, tk=None):
    m, k = pairs[0][0].shape
    d = pairs[0][1].shape[0]
    tm, tk = min(tm, m), tk or k
    nk, npair = k // tk, len(pairs)

    def kern(*refs):
        a_refs, b_refs = refs[:npair], refs[npair:2 * npair]
        h_ref, g_ref, dres_ref, _, o_ref, dg_ref, acc_ref = refs[2 * npair:]
        i, kk = pl.program_id(0), pl.program_id(1)

        @pl.when(jnp.logical_and(i == 0, kk == 0))
        def _():
            dg_ref[...] = jnp.zeros_like(dg_ref)

        @pl.when(kk == 0)
        def _():
            acc_ref[...] = jnp.zeros_like(acc_ref)

        acc_ref[...] += _pair_sum(a_refs, b_refs, True)

        @pl.when(kk == nk - 1)
        def _():
            dn = acc_ref[...]
            x = h_ref[...]
            rstd = _rstd(x)
            xhat = x * rstd
            dg_ref[...] += jnp.broadcast_to(jnp.sum(dn * xhat, axis=0, keepdims=True), dg_ref.shape)
            dxh = dn * g_ref[...]
            dx = rstd * (dxh - xhat * jnp.mean(dxh * xhat, axis=-1, keepdims=True))
            o_ref[...] = dres_ref[...] + dx

    row = pl.BlockSpec((tm, d), lambda i, kk: (i, 0))
    in_specs = ([pl.BlockSpec((tm, tk), lambda i, kk: (i, kk))] * npair
                + [pl.BlockSpec((d, tk), lambda i, kk: (0, kk))] * npair
                + [row, pl.BlockSpec((1, d), lambda i, kk: (0, 0)), row, pl.BlockSpec((8, 128), lambda i, kk: (0, 0))])
    return pl.pallas_call(
        kern, name=name, grid=(m // tm, nk), in_specs=in_specs,
        out_specs=[row, pl.BlockSpec((8, d), lambda i, kk: (0, 0))],
        out_shape=[jax.ShapeDtypeStruct((m, d), F32), jax.ShapeDtypeStruct((8, d), F32)],
        scratch_shapes=[pltpu.VMEM((tm, d), F32)],
        compiler_params=_params("arbitrary", "arbitrary"))(
            *[p[0] for p in pairs], *[p[1] for p in pairs], h, g.reshape(1, d), dres, dep)


def _mm_tn(a, b, out_dtype, name, *, bm=None, bn=None, tk=512):
    t, m = a.shape
    batched = b.ndim == 3
    n = b.shape[-1]
    nb = b.shape[0] if batched else 1
    bm, bn, tk = bm or m, bn or n, min(tk, t)
    nk = t // tk

    def kern(a_ref, b_ref, o_ref, acc_ref):
        kk = pl.program_id(3)

        @pl.when(kk == 0)
        def _():
            acc_ref[...] = jnp.zeros_like(acc_ref)

        acc_ref[...] += lax.dot_general(_bf(a_ref[...]), _bf(b_ref[...]), (((0,), (0,)), ((), ())),
                                        preferred_element_type=F32)

        @pl.when(kk == nk - 1)
        def _():
            o_ref[...] = acc_ref[...].astype(o_ref.dtype)

    a_spec = pl.BlockSpec((tk, bm), lambda h, i, j, kk: (kk, i))
    if batched:
        b_spec = pl.BlockSpec((None, tk, bn), lambda h, i, j, kk: (h, kk, j))
        o_spec = pl.BlockSpec((None, bm, bn), lambda h, i, j, kk: (h, i, j))
        out_shape = jax.ShapeDtypeStruct((nb, m, n), out_dtype)
    else:
        b_spec = pl.BlockSpec((tk, bn), lambda h, i, j, kk: (kk, j))
        o_spec = pl.BlockSpec((bm, bn), lambda h, i, j, kk: (i, j))
        out_shape = jax.ShapeDtypeStruct((m, n), out_dtype)
    return pl.pallas_call(
        kern, name=name, grid=(nb, m // bm, n // bn, nk), in_specs=[a_spec, b_spec], out_specs=o_spec,
        out_shape=out_shape, scratch_shapes=[pltpu.VMEM((bm, bn), F32)],
        compiler_params=_params("parallel", "parallel", "parallel", "arbitrary"))(a, b)


def _ffn_tile(f):
    return f // 2 if (f // 2) % 128 == 0 else f


def _ffn_fwd(h, g, w1, w3, w2, dep, name):
    t, d = h.shape
    f = w1.shape[1]
    tm, tf = min(512, t), _ffn_tile(f)
    nf = f // tf

    def kern(h_ref, g_ref, w1_ref, w3_ref, w2_ref, dep_ref, ho_ref, n_ref, z1_ref, z3_ref, n_sc, acc_ref):
        j = pl.program_id(1)

        @pl.when(j == 0)
        def _():
            x = h_ref[...]
            n = _bf(x * _rstd(x) * g_ref[...])
            n_sc[...] = n
            n_ref[...] = n
            acc_ref[...] = jnp.zeros_like(acc_ref)

        n = n_sc[...]
        z1 = jnp.dot(n, w1_ref[...], preferred_element_type=F32)
        z3 = jnp.dot(n, w3_ref[...], preferred_element_type=F32)
        z1_ref[...] = _bf(z1)
        z3_ref[...] = _bf(z3)
        act = _bf(z1 * _sigmoid(z1) * z3)
        acc_ref[...] += jnp.dot(act, w2_ref[...], preferred_element_type=F32)

        @pl.when(j == nf - 1)
        def _():
            ho_ref[...] = h_ref[...] + FFN_RES * acc_ref[...]

    row = pl.BlockSpec((tm, d), lambda i, j: (i, 0))
    col = pl.BlockSpec((tm, tf), lambda i, j: (i, j))
    return pl.pallas_call(
        kern, name=name, grid=(t // tm, nf),
        in_specs=[row, pl.BlockSpec((1, d), lambda i, j: (0, 0)), pl.BlockSpec((d, tf), lambda i, j: (0, j)),
                  pl.BlockSpec((d, tf), lambda i, j: (0, j)), pl.BlockSpec((tf, d), lambda i, j: (j, 0)),
                  pl.BlockSpec((8, 128), lambda i, j: (0, 0))],
        out_specs=[row, row, col, col],
        out_shape=[jax.ShapeDtypeStruct((t, d), F32), jax.ShapeDtypeStruct((t, d), BF16),
                   jax.ShapeDtypeStruct((t, f), BF16), jax.ShapeDtypeStruct((t, f), BF16)],
        scratch_shapes=[pltpu.VMEM((tm, d), BF16), pltpu.VMEM((tm, d), F32)],
        compiler_params=_params("parallel", "arbitrary"))(h, g.reshape(1, d), w1, w3, w2, dep)


def _ffn_bwd_act(dh, z1, z3, w2, dep, name):
    t, d = dh.shape
    f = z1.shape[1]
    tm, tf = min(512, t), _ffn_tile(f)

    def kern(dh_ref, z1_ref, z3_ref, w2_ref, dep_ref, dz1_ref, dz3_ref, a_ref, df_ref, df_sc):
        @pl.when(pl.program_id(1) == 0)
        def _():
            df = _bf(FFN_RES * dh_ref[...])
            df_sc[...] = df
            df_ref[...] = df

        da = lax.dot_general(df_sc[...], w2_ref[...], (((1,), (1,)), ((), ())), preferred_element_type=F32)
        z1v, z3v = z1_ref[...].astype(F32), z3_ref[...].astype(F32)
        sig = _sigmoid(z1v)
        silu = z1v * sig
        a_ref[...] = _bf(silu * z3v)
        dz1_ref[...] = _bf(da * z3v * (sig * (1.0 + z1v * (1.0 - sig))))
        dz3_ref[...] = _bf(da * silu)

    row = pl.BlockSpec((tm, d), lambda i, j: (i, 0))
    col = pl.BlockSpec((tm, tf), lambda i, j: (i, j))
    colshape = jax.ShapeDtypeStruct((t, f), BF16)
    return pl.pallas_call(
        kern, name=name, grid=(t // tm, f // tf),
        in_specs=[row, col, col, pl.BlockSpec((tf, d), lambda i, j: (j, 0)),
                  pl.BlockSpec((8, 128), lambda i, j: (0, 0))],
        out_specs=[col, col, col, row],
        out_shape=[colshape, colshape, colshape, jax.ShapeDtypeStruct((t, d), BF16)],
        scratch_shapes=[pltpu.VMEM((tm, d), BF16)],
        compiler_params=_params("parallel", "arbitrary"))(dh, z1, z3, w2, dep)


def _ffn_bwd(dh, h_in, g, n, z1, z3, w1, w3, w2, dep, send, tag):
    f = w1.shape[1]
    dz1, dz3, act, df = _ffn_bwd_act(dh, z1, z3, w2, dep, tag + "_bwd_act")
    dw1 = _mm_tn(n, dz1, BF16, tag + "_dw1", bn=_ffn_tile(f))
    dw3 = _mm_tn(n, dz3, BF16, tag + "_dw3", bn=_ffn_tile(f))
    dw2 = _mm_tn(act, df, BF16, tag + "_dw2", bm=_ffn_tile(f))
    token = send(dw1, dw3, dw2)
    dh_in, dg = _mm_normbwd([(dz1, w1), (dz3, w3)], h_in, g, dh, token, tag + "_bwd_dn", tk=_ffn_tile(f))
    return dh_in, dg, token


def _norm_fwd(h, g, dep, name):
    t, d = h.shape
    tm = min(512, t)

    def kern(h_ref, g_ref, dep_ref, o_ref):
        x = h_ref[...]
        o_ref[...] = _bf(x * _rstd(x) * g_ref[...])

    row = pl.BlockSpec((tm, d), lambda i: (i, 0))
    return pl.pallas_call(
        kern, name=name, grid=(t // tm,),
        in_specs=[row, pl.BlockSpec((1, d), lambda i: (0, 0)), pl.BlockSpec((8, 128), lambda i: (0, 0))],
        out_specs=row, out_shape=jax.ShapeDtypeStruct((t, d), BF16),
        compiler_params=_params("parallel"))(h, g.reshape(1, d), dep)


def _loss_bwd(h, target, g):
    t, d = h.shape
    tm = min(512, t)

    def kern(h_ref, t_ref, g_ref, dh_ref, dg_ref, loss_ref):
        @pl.when(pl.program_id(0) == 0)
        def _():
            dg_ref[...] = jnp.zeros_like(dg_ref)
            loss_ref[...] = jnp.zeros_like(loss_ref)

        x = h_ref[...]
        rstd = _rstd(x)
        xhat = x * rstd
        err = xhat * g_ref[...] - t_ref[...]
        row_loss = jnp.sum(err * err, axis=-1, keepdims=True) * (0.5 / d)
        loss_ref[...] += jnp.broadcast_to(jnp.sum(row_loss, axis=0, keepdims=True), loss_ref.shape)
        dy = err * (1.0 / d)
        dg_ref[...] += jnp.broadcast_to(jnp.sum(dy * xhat, axis=0, keepdims=True), dg_ref.shape)
        dxh = dy * g_ref[...]
        dh_ref[...] = rstd * (dxh - xhat * jnp.mean(dxh * xhat, axis=-1, keepdims=True))

    row = pl.BlockSpec((tm, d), lambda i: (i, 0))
    return pl.pallas_call(
        kern, name="loss_bwd", grid=(t // tm,),
        in_specs=[row, row, pl.BlockSpec((1, d), lambda i: (0, 0))],
        out_specs=[row, pl.BlockSpec((8, d), lambda i: (0, 0)), pl.BlockSpec((8, 128), lambda i: (0, 0))],
        out_shape=[jax.ShapeDtypeStruct((t, d), F32), jax.ShapeDtypeStruct((8, d), F32),
                   jax.ShapeDtypeStruct((8, 128), F32)],
        compiler_params=_params("arbitrary"))(h, target, g.reshape(1, d))


def _glu_fwd(m, wa, wb):
    t, d = m.shape
    c = wa.shape[1]
    tm, tc = min(512, t), min(512, c)

    def kern(m_ref, wa_ref, wb_ref, a_ref, b_ref, glu_ref):
        mv = m_ref[...]
        a = jnp.dot(mv, wa_ref[...], preferred_element_type=F32)
        b = jnp.dot(mv, wb_ref[...], preferred_element_type=F32)
        a_ref[...] = _bf(a)
        b_ref[...] = _bf(b)
        glu_ref[...] = _bf(a * _sigmoid(b))

    col = pl.BlockSpec((tm, tc), lambda i, j: (i, j))
    wspec = pl.BlockSpec((d, tc), lambda i, j: (0, j))
    shape = jax.ShapeDtypeStruct((t, c), BF16)
    return pl.pallas_call(
        kern, name="conv_glu_fwd", grid=(t // tm, c // tc),
        in_specs=[pl.BlockSpec((tm, d), lambda i, j: (i, 0)), wspec, wspec], out_specs=[col, col, col],
        out_shape=[shape, shape, shape], compiler_params=_params("parallel", "parallel"))(m, wa, wb)


def _conv_tile(t):
    return min(256, t)


def _conv_fwd(glu, w_dw, g):
    t, c = glu.shape
    tm = _conv_tile(t)
    hb = tm // CONV_HALO

    def kern(cur_ref, halo_ref, w_ref, g_ref, cv_ref, s_ref, ext):
        i = pl.program_id(0)
        ext[0:CONV_HALO, :] = jnp.where(i > 0, halo_ref[...].astype(F32), 0.0)
        ext[CONV_HALO:, :] = cur_ref[...].astype(F32)
        gv = g_ref[...]
        for r0 in range(0, tm, CONV_ROWS):
            acc = jnp.zeros((CONV_ROWS, c), F32)
            for k in range(CONV_WIDTH):
                acc = acc + ext[pl.ds(r0 + 2 + k, CONV_ROWS), :] * w_ref[k:k + 1, :]
            cv_ref[r0:r0 + CONV_ROWS, :] = acc
            rn = acc * _rstd(acc) * gv
            s_ref[r0:r0 + CONV_ROWS, :] = _bf(rn * _sigmoid(rn))

    row = pl.BlockSpec((tm, c), lambda i: (i, 0))
    return pl.pallas_call(
        kern, name="conv_fwd", grid=(t // tm,),
        in_specs=[row, pl.BlockSpec((CONV_HALO, c), lambda i: (jnp.maximum(i * hb - 1, 0), 0)),
                  pl.BlockSpec((CONV_HALO, c), lambda i: (0, 0)), pl.BlockSpec((1, c), lambda i: (0, 0))],
        out_specs=[row, row],
        out_shape=[jax.ShapeDtypeStruct((t, c), F32), jax.ShapeDtypeStruct((t, c), BF16)],
        scratch_shapes=[pltpu.VMEM((tm + CONV_HALO, c), F32)],
        compiler_params=_params("parallel"))(glu, glu, w_dw, g.reshape(1, c))


def _conv_bwd_norm(dh, cv, w_pw2, g, dep):
    t, c = cv.shape
    tm = min(512, t)

    def kern(dh_ref, cv_ref, w_ref, g_ref, dep_ref, dcv_ref, dg_ref):
        @pl.when(pl.program_id(0) == 0)
        def _():
            dg_ref[...] = jnp.zeros_like(dg_ref)

        ds = lax.dot_general(_bf(dh_ref[...]), w_ref[...], (((1,), (1,)), ((), ())), preferred_element_type=F32)
        x = cv_ref[...]
        rstd = _rstd(x)
        xhat = x * rstd
        rn = xhat * g_ref[...]
        sig = _sigmoid(rn)
        drn = ds * (sig * (1.0 + rn * (1.0 - sig)))
        dg_ref[...] += jnp.broadcast_to(jnp.sum(drn * xhat, axis=0, keepdims=True), dg_ref.shape)
        dxh = drn * g_ref[...]
        dcv_ref[...] = rstd * (dxh - xhat * jnp.mean(dxh * xhat, axis=-1, keepdims=True))

    row = pl.BlockSpec((tm, c), lambda i: (i, 0))
    return pl.pallas_call(
        kern, name="conv_bwd_norm", grid=(t // tm,),
        in_specs=[pl.BlockSpec((tm, dh.shape[1]), lambda i: (i, 0)), row,
                  pl.BlockSpec(w_pw2.shape, lambda i: (0, 0)), pl.BlockSpec((1, c), lambda i: (0, 0)),
                  pl.BlockSpec((8, 128), lambda i: (0, 0))],
        out_specs=[row, pl.BlockSpec((8, c), lambda i: (0, 0))],
        out_shape=[jax.ShapeDtypeStruct((t, c), F32), jax.ShapeDtypeStruct((8, c), F32)],
        compiler_params=_params("arbitrary"))(dh, cv, w_pw2, g.reshape(1, c), dep)


def _conv_bwd_dw(dcv, glu, a, b, w_dw):
    t, c = dcv.shape
    tm = _conv_tile(t)
    hb = tm // CONV_HALO
    last = t // CONV_HALO - 1

    def kern(dcv_ref, dnext_ref, glu_ref, gprev_ref, a_ref, b_ref, w_ref, da_ref, db_ref, dw_ref, dext, gext):
        i = pl.program_id(0)

        @pl.when(i == 0)
        def _():
            dw_ref[...] = jnp.zeros_like(dw_ref)

        dext[0:tm, :] = dcv_ref[...]
        dext[tm:, :] = jnp.where(i < t // tm - 1, dnext_ref[...], 0.0)
        gext[0:CONV_HALO, :] = jnp.where(i > 0, gprev_ref[...].astype(F32), 0.0)
        gext[CONV_HALO:, :] = glu_ref[...].astype(F32)
        for r0 in range(0, tm, CONV_ROWS):
            acc = jnp.zeros((CONV_ROWS, c), F32)
            for k in range(CONV_WIDTH):
                acc = acc + dext[pl.ds(r0 + CONV_WIDTH - 1 - k, CONV_ROWS), :] * w_ref[k:k + 1, :]
            av = a_ref[r0:r0 + CONV_ROWS, :].astype(F32)
            sig = _sigmoid(b_ref[r0:r0 + CONV_ROWS, :].astype(F32))
            da_ref[r0:r0 + CONV_ROWS, :] = _bf(acc * sig)
            db_ref[r0:r0 + CONV_ROWS, :] = _bf(acc * av * sig * (1.0 - sig))
        for k in range(CONV_WIDTH):
            acc = jnp.zeros((CONV_ROWS, c), F32)
            for r0 in range(0, tm, CONV_ROWS):
                acc = acc + gext[pl.ds(r0 + 2 + k, CONV_ROWS), :] * dext[r0:r0 + CONV_ROWS, :]
            dw_ref[k:k + 1, :] += jnp.sum(acc, axis=0, keepdims=True)

    row = pl.BlockSpec((tm, c), lambda i: (i, 0))
    shape = jax.ShapeDtypeStruct((t, c), BF16)
    return pl.pallas_call(
        kern, name="conv_bwd_dw", grid=(t // tm,),
        in_specs=[row, pl.BlockSpec((CONV_HALO, c), lambda i: (jnp.minimum((i + 1) * hb, last), 0)),
                  row, pl.BlockSpec((CONV_HALO, c), lambda i: (jnp.maximum(i * hb - 1, 0), 0)),
                  row, row, pl.BlockSpec((CONV_HALO, c), lambda i: (0, 0))],
        out_specs=[row, row, pl.BlockSpec((CONV_HALO, c), lambda i: (0, 0))],
        out_shape=[shape, shape, jax.ShapeDtypeStruct((CONV_HALO, c), F32)],
        scratch_shapes=[pltpu.VMEM((tm + CONV_HALO, c), F32), pltpu.VMEM((tm + CONV_HALO, c), F32)],
        compiler_params=_params("arbitrary"))(dcv, dcv, glu, glu, a, b, w_dw)


def _rope_tables(pos):
    t = pos.shape[0]
    tm = min(512, t)
    freq = (np.float32(ROPE_THETA) ** (np.float32(-2.0) * np.arange(ROPE // 2, dtype=np.float32)
                                       / np.float32(ROPE))).astype(np.float32)
    row = np.zeros((2, 128), np.float32)
    row[0, :ROPE] = np.concatenate([freq, freq])
    row[1, :ROPE] = 1.0

    def kern(pos_ref, f_ref, c_ref, s_ref):
        ang = pos_ref[...].astype(F32) * f_ref[0:1, :]
        mask = f_ref[1:2, :]
        c_ref[...] = jnp.cos(ang) * mask
        s_ref[...] = jnp.sin(ang) * mask

    out = pl.BlockSpec((tm, 128), lambda i: (i, 0))
    shape = jax.ShapeDtypeStruct((t, 128), F32)
    return pl.pallas_call(
        kern, name="rope_tables", grid=(t // tm,),
        in_specs=[pl.BlockSpec((tm, 1), lambda i: (i, 0)), pl.BlockSpec((2, 128), lambda i: (0, 0))],
        out_specs=[out, out], out_shape=[shape, shape], compiler_params=_params("parallel"))(pos, jnp.asarray(row))


def _mla_prep(a, gq, gkv, cs_c, cs_s):
    t = a.shape[0]
    tm = min(512, t)
    kv0, r0 = Q_LORA, Q_LORA + KV_LORA

    def kern(a_ref, gq_ref, gkv_ref, c_ref, s_ref, cq_ref, ckv_ref, kr_ref):
        aq = a_ref[:, 0:kv0]
        akv = a_ref[:, kv0:r0]
        ar = a_ref[:, r0:A_PAD]
        cq_ref[...] = _bf(aq * _rstd(aq) * gq_ref[...])
        ckv_ref[...] = _bf(akv * _rstd(akv) * gkv_ref[...])
        kr_ref[...] = _bf(ar * c_ref[...] + _rot(ar) * s_ref[...])

    def row(w):
        return pl.BlockSpec((tm, w), lambda i: (i, 0))

    def vec(w):
        return pl.BlockSpec((1, w), lambda i: (0, 0))

    return pl.pallas_call(
        kern, name="mla_prep", grid=(t // tm,),
        in_specs=[row(A_PAD), vec(Q_LORA), vec(KV_LORA), row(128), row(128)],
        out_specs=[row(Q_LORA), row(KV_LORA), row(128)],
        out_shape=[jax.ShapeDtypeStruct((t, Q_LORA), BF16), jax.ShapeDtypeStruct((t, KV_LORA), BF16),
                   jax.ShapeDtypeStruct((t, 128), BF16)],
        compiler_params=_params("parallel"))(a, gq.reshape(1, -1), gkv.reshape(1, -1), cs_c, cs_s)


def _mla_prep_bwd(a, dcq, dckv, dar, gq, gkv):
    t = a.shape[0]
    tm = min(512, t)
    kv0, r0 = Q_LORA, Q_LORA + KV_LORA

    def kern(a_ref, dcq_ref, dckv_ref, dar_ref, gq_ref, gkv_ref, da_ref, dgq_ref, dgkv_ref):
        @pl.when(pl.program_id(0) == 0)
        def _():
            dgq_ref[...] = jnp.zeros_like(dgq_ref)
            dgkv_ref[...] = jnp.zeros_like(dgkv_ref)

        def back(x, dy, g_ref, dg_ref):
            rstd = _rstd(x)
            xhat = x * rstd
            dg_ref[...] += jnp.broadcast_to(jnp.sum(dy * xhat, axis=0, keepdims=True), dg_ref.shape)
            dxh = dy * g_ref[...]
            return rstd * (dxh - xhat * jnp.mean(dxh * xhat, axis=-1, keepdims=True))

        da_ref[:, 0:kv0] = _bf(back(a_ref[:, 0:kv0], dcq_ref[...], gq_ref, dgq_ref))
        da_ref[:, kv0:r0] = _bf(back(a_ref[:, kv0:r0], dckv_ref[...], gkv_ref, dgkv_ref))
        da_ref[:, r0:A_PAD] = _bf(dar_ref[...])

    def row(w):
        return pl.BlockSpec((tm, w), lambda i: (i, 0))

    def vec(r, w):
        return pl.BlockSpec((r, w), lambda i: (0, 0))

    return pl.pallas_call(
        kern, name="mla_prep_bwd", grid=(t // tm,),
        in_specs=[row(A_PAD), row(Q_LORA), row(KV_LORA), row(128), vec(1, Q_LORA), vec(1, KV_LORA)],
        out_specs=[row(A_PAD), vec(8, Q_LORA), vec(8, KV_LORA)],
        out_shape=[jax.ShapeDtypeStruct((t, A_PAD), BF16), jax.ShapeDtypeStruct((8, Q_LORA), F32),
                   jax.ShapeDtypeStruct((8, KV_LORA), F32)],
        compiler_params=_params("arbitrary"))(a, dcq, dckv, dar, gq.reshape(1, -1), gkv.reshape(1, -1))


def _mla_qkv(cq, ckv, kr, cs_c, cs_s, wuq, wukv):
    t = cq.shape[0]
    tm = min(512, t)

    def kern(cq_ref, ckv_ref, kr_ref, c_ref, s_ref, wq_ref, wkv_ref, q_ref, k_ref, v_ref):
        r = jnp.dot(cq_ref[...], wq_ref[...], preferred_element_type=F32)
        xr = r[:, NOPE:]
        q_ref[:, 0:NOPE] = _bf(r[:, 0:NOPE] * ATTN_SCALE)
        q_ref[:, NOPE:] = _bf((xr * c_ref[...] + _rot(xr) * s_ref[...]) * ATTN_SCALE)
        kv = jnp.dot(ckv_ref[...], wkv_ref[...], preferred_element_type=F32)
        k_ref[:, 0:NOPE] = _bf(kv[:, 0:NOPE])
        k_ref[:, NOPE:] = kr_ref[...]
        v_ref[...] = _bf(kv[:, NOPE:])

    def row(w):
        return pl.BlockSpec((tm, w), lambda i, h: (i, 0))

    def head(w):
        return pl.BlockSpec((None, tm, w), lambda i, h: (h, i, 0))

    return pl.pallas_call(
        kern, name="mla_qkv", grid=(t // tm, HEADS),
        in_specs=[row(Q_LORA), row(KV_LORA), row(128), row(128), row(128),
                  pl.BlockSpec((None, Q_LORA, HEAD_PAD), lambda i, h: (h, 0, 0)),
                  pl.BlockSpec((None, KV_LORA, NOPE + V_HEAD), lambda i, h: (h, 0, 0))],
        out_specs=[head(HEAD_PAD), head(HEAD_PAD), head(V_HEAD)],
        out_shape=[jax.ShapeDtypeStruct((HEADS, t, HEAD_PAD), BF16), jax.ShapeDtypeStruct((HEADS, t, HEAD_PAD), BF16),
                   jax.ShapeDtypeStruct((HEADS, t, V_HEAD), BF16)],
        compiler_params=_params("parallel", "arbitrary"))(cq, ckv, kr, cs_c, cs_s, wuq, wukv)


def _mla_qkv_bwd(dq, dk, dv, cs_c, cs_s, wuq, wukv):
    t = dq.shape[1]
    tm = min(512, t)

    def kern(dq_ref, dk_ref, dv_ref, c_ref, s_ref, wq_ref, wkv_ref, dr_ref, dkv_ref, dcq_ref, dckv_ref, dar_ref):
        @pl.when(pl.program_id(1) == 0)
        def _():
            dcq_ref[...] = jnp.zeros_like(dcq_ref)
            dckv_ref[...] = jnp.zeros_like(dckv_ref)
            dar_ref[...] = jnp.zeros_like(dar_ref)

        cv, sv = c_ref[...], s_ref[...]
        dqx = dq_ref[:, NOPE:]
        dr_ref[:, 0:NOPE] = _bf(dq_ref[:, 0:NOPE] * ATTN_SCALE)
        dr_ref[:, NOPE:] = _bf((dqx * cv + _rot_t(dqx * sv)) * ATTN_SCALE)
        dcq_ref[...] += lax.dot_general(dr_ref[...], wq_ref[...], (((1,), (1,)), ((), ())),
                                        preferred_element_type=F32)
        dkx = dk_ref[:, NOPE:]
        dar_ref[...] += dkx * cv + _rot_t(dkx * sv)
        dkv_ref[:, 0:NOPE] = _bf(dk_ref[:, 0:NOPE])
        dkv_ref[:, NOPE:] = _bf(dv_ref[...])
        dckv_ref[...] += lax.dot_general(dkv_ref[...], wkv_ref[...], (((1,), (1,)), ((), ())),
                                         preferred_element_type=F32)

    def row(w):
        return pl.BlockSpec((tm, w), lambda i, h: (i, 0))

    def head(w):
        return pl.BlockSpec((None, tm, w), lambda i, h: (h, i, 0))

    return pl.pallas_call(
        kern, name="mla_qkv_bwd", grid=(t // tm, HEADS),
        in_specs=[head(HEAD_PAD), head(HEAD_PAD), head(V_HEAD), row(128), row(128),
                  pl.BlockSpec((None, Q_LORA, HEAD_PAD), lambda i, h: (h, 0, 0)),
                  pl.BlockSpec((None, KV_LORA, NOPE + V_HEAD), lambda i, h: (h, 0, 0))],
        out_specs=[head(HEAD_PAD), head(NOPE + V_HEAD), row(Q_LORA), row(KV_LORA), row(128)],
        out_shape=[jax.ShapeDtypeStruct((HEADS, t, HEAD_PAD), BF16),
                   jax.ShapeDtypeStruct((HEADS, t, NOPE + V_HEAD), BF16),
                   jax.ShapeDtypeStruct((t, Q_LORA), F32), jax.ShapeDtypeStruct((t, KV_LORA), F32),
                   jax.ShapeDtypeStruct((t, 128), F32)],
        compiler_params=_params("parallel", "arbitrary"))(dq, dk, dv, cs_c, cs_s, wuq, wukv)


def _attn_block(t):
    return 512 if t >= 4096 else 128


def _chunk_mask(bk, bq):
    kc = lax.broadcasted_iota(jnp.int32, (bk, bq), 0) // CHUNK
    qc = lax.broadcasted_iota(jnp.int32, (bk, bq), 1) // CHUNK
    return qc >= kc


def _flash_fwd(q, k, v):
    t = q.shape[1]
    bq = _attn_block(t)
    nq = t // bq

    def kern(q_ref, k_ref, v_ref, o_ref, lse_ref):
        i = pl.program_id(1)
        qv = q_ref[...]

        def scores(j):
            kj = k_ref[pl.ds(pl.multiple_of(j * bq, bq), bq), :]
            return lax.dot_general(kj, qv, (((1,), (1,)), ((), ())), preferred_element_type=F32)

        def update(j, st, m, l, acc):
            m_new = jnp.maximum(m, jnp.max(st, axis=0, keepdims=True))
            alpha = jnp.exp(m - m_new)
            p = jnp.exp(st - m_new)
            vj = v_ref[pl.ds(pl.multiple_of(j * bq, bq), bq), :]
            pv = lax.dot_general(vj, _bf(p), (((0,), (0,)), ((), ())), preferred_element_type=F32)
            return m_new, alpha * l + jnp.sum(p, axis=0, keepdims=True), alpha * acc + pv

        st = jnp.where(_chunk_mask(bq, bq), scores(i), -1e30)
        carry = update(i, st, jnp.full((1, bq), -1e30, F32), jnp.zeros((1, bq), F32), jnp.zeros((V_HEAD, bq), F32))
        m, l, acc = lax.fori_loop(0, i, lambda j, c: update(j, scores(j), *c), carry)
        o_ref[...] = _bf((acc / l).T)
        lse_ref[...] = jnp.broadcast_to(m + jnp.log(l), (8, bq))

    return pl.pallas_call(
        kern, name="flash_fwd", grid=(HEADS, nq),
        in_specs=[pl.BlockSpec((None, bq, HEAD_PAD), lambda h, i: (h, i, 0)),
                  pl.BlockSpec((None, t, HEAD_PAD), lambda h, i: (h, 0, 0)),
                  pl.BlockSpec((None, t, V_HEAD), lambda h, i: (h, 0, 0))],
        out_specs=[pl.BlockSpec((bq, V_HEAD), lambda h, i: (i, h)),
                   pl.BlockSpec((None, None, 8, bq), lambda h, i: (h, i, 0, 0))],
        out_shape=[jax.ShapeDtypeStruct((t, HEADS * V_HEAD), BF16), jax.ShapeDtypeStruct((HEADS, nq, 8, bq), F32)],
        compiler_params=_params("parallel", "arbitrary"))(q, k, v)


def _attn_delta(do, o):
    t = do.shape[0]
    bq = _attn_block(t)

    def kern(do_ref, o_ref, d_ref):
        prod = do_ref[...].astype(F32) * o_ref[...].astype(F32)
        d_ref[...] = jnp.broadcast_to(jnp.sum(prod.T, axis=0, keepdims=True), (8, bq))

    blk = pl.BlockSpec((bq, V_HEAD), lambda h, i: (i, h))
    return pl.pallas_call(
        kern, name="attn_delta", grid=(HEADS, t // bq), in_specs=[blk, blk],
        out_specs=pl.BlockSpec((None, None, 8, bq), lambda h, i: (h, i, 0, 0)),
        out_shape=jax.ShapeDtypeStruct((HEADS, t // bq, 8, bq), F32),
        compiler_params=_params("parallel", "parallel"))(do, o)


def _flash_bwd(q, k, v, do, lse, delta):
    t = q.shape[1]
    bq = _attn_block(t)
    nq = t // bq

    def kern(q_ref, k_ref, v_ref, do_ref, lse_ref, del_ref, dq_ref, dk_ref, dv_ref):
        j = pl.program_id(1)

        @pl.when(j == 0)
        def _():
            dq_ref[...] = jnp.zeros_like(dq_ref)

        dk_ref[...] = jnp.zeros_like(dk_ref)
        dv_ref[...] = jnp.zeros_like(dv_ref)
        kj, vj = k_ref[...], v_ref[...]

        def step(i, masked):
            rows = pl.ds(pl.multiple_of(i * bq, bq), bq)
            qi, doi = q_ref[rows, :], do_ref[rows, :]
            st = lax.dot_general(kj, qi, (((1,), (1,)), ((), ())), preferred_element_type=F32)
            pt = jnp.exp(st - lse_ref[i][0:1, :])
            if masked:
                pt = jnp.where(_chunk_mask(bq, bq), pt, 0.0)
            dpt = lax.dot_general(vj, doi, (((1,), (1,)), ((), ())), preferred_element_type=F32)
            dst = _bf(pt * (dpt - del_ref[i][0:1, :]))
            dv_ref[...] += jnp.dot(_bf(pt), doi, preferred_element_type=F32)
            dk_ref[...] += jnp.dot(dst, qi, preferred_element_type=F32)
            dq_ref[rows, :] += lax.dot_general(dst, kj, (((0,), (0,)), ((), ())), preferred_element_type=F32)

        step(j, True)

        def body(i, carry):
            step(i, False)
            return carry

        lax.fori_loop(j + 1, nq, body, 0)

    stat = pl.BlockSpec((None, nq, 8, bq), lambda h, j: (h, 0, 0, 0))
    return pl.pallas_call(
        kern, name="flash_bwd", grid=(HEADS, nq),
        in_specs=[pl.BlockSpec((None, t, HEAD_PAD), lambda h, j: (h, 0, 0)),
                  pl.BlockSpec((None, bq, HEAD_PAD), lambda h, j: (h, j, 0)),
                  pl.BlockSpec((None, bq, V_HEAD), lambda h, j: (h, j, 0)),
                  pl.BlockSpec((t, V_HEAD), lambda h, j: (0, h)), stat, stat],
        out_specs=[pl.BlockSpec((None, t, HEAD_PAD), lambda h, j: (h, 0, 0)),
                   pl.BlockSpec((None, bq, HEAD_PAD), lambda h, j: (h, j, 0)),
                   pl.BlockSpec((None, bq, V_HEAD), lambda h, j: (h, j, 0))],
        out_shape=[jax.ShapeDtypeStruct((HEADS, t, HEAD_PAD), F32), jax.ShapeDtypeStruct((HEADS, t, HEAD_PAD), F32),
                   jax.ShapeDtypeStruct((HEADS, t, V_HEAD), F32)],
        compiler_params=_params("parallel", "arbitrary"))(q, k, v, do, lse, delta)


def _place():
    x, y, c = lax.axis_index("x"), lax.axis_index("y"), lax.axis_index("c")
    return x, y, c, [(1 - x, y), (x, 1 - y), (1 - x, 1 - y)]


def _all_gather_rows(block, name):
    m_per, n = block.shape

    def body(x_ref, out_ref, send_sems, recv_sems, local_sem):
        x, y, c, chips = _place()
        me, sibling = (x, y, c), (x, y, 1 - c)

        def rows(px, py, pc):
            return out_ref.at[pl.ds((4 * px + 2 * py + pc) * m_per, m_per), :]

        def copy(k, blk, to, src=None):
            return pltpu.make_async_remote_copy(
                src_ref=rows(*blk) if src is None else src, dst_ref=rows(*blk), send_sem=send_sems.at[k],
                recv_sem=recv_sems.at[k], device_id=to, device_id_type=MESH)

        mine = pltpu.make_async_copy(x_ref, rows(*me), local_sem)
        mine.start()
        first = [copy(0, me, sibling, src=x_ref)]
        first += [copy(1 + j, me, (*chip, c), src=x_ref) for j, chip in enumerate(chips)]
        for cp in first:
            cp.start()
        passed = [copy(4 + j, (*chip, c), sibling) for j, chip in enumerate(chips)]
        for j, chip in enumerate(chips):
            copy(1 + j, (*chip, c), me).wait_recv()
            passed[j].start()
        copy(0, sibling, me).wait_recv()
        for j, chip in enumerate(chips):
            copy(4 + j, (*chip, 1 - c), me).wait_recv()
        for cp in first + passed:
            cp.wait_send()
        mine.wait()

    return pl.pallas_call(
        body, name=name, out_shape=jax.ShapeDtypeStruct((8 * m_per, n), block.dtype),
        in_specs=[pl.BlockSpec(memory_space=pltpu.VMEM)], out_specs=pl.BlockSpec(memory_space=pltpu.VMEM),
        scratch_shapes=[pltpu.SemaphoreType.DMA((7,)), pltpu.SemaphoreType.DMA((7,)), pltpu.SemaphoreType.DMA],
        compiler_params=pltpu.CompilerParams(vmem_limit_bytes=VMEM_LIMIT_BYTES))(block)


HBM_SPEC = pl.BlockSpec(memory_space=pltpu.HBM)
SEM_SPEC = pl.BlockSpec(memory_space=pltpu.SEMAPHORE)
DATAFLOW = pltpu.SideEffectType.DATAFLOW_SIDE_EFFECTING


def _in_hbm(a):
    return pltpu.with_memory_space_constraint(a, pltpu.HBM)


def _chip_copies(ins, lands, send_sems, recv_sems, src_slot):
    n = len(ins)
    x, y, c, chips = _place()
    me = 2 * x + y
    return [pltpu.make_async_remote_copy(
        src_ref=ins[w].at[2 * chip[0] + chip[1]] if src_slot else ins[w], dst_ref=lands[w].at[me],
        send_sem=send_sems.at[p * n + w], recv_sem=recv_sems.at[p * n + w], device_id=(*chip, c),
        device_id_type=MESH) for w in range(n) for p, chip in enumerate(chips)]


def _exchange_start(srcs, lands, src_slot, name, dep=None):
    n = len(srcs)
    first_out = 2 * n + (dep is not None)

    def body(*refs):
        for cp in _chip_copies(refs[:n], refs[n:2 * n], refs[first_out], refs[first_out + 1], src_slot):
            cp.start()
        token = refs[-1]
        token[...] = jnp.zeros_like(token)

    thru = [pltpu.HBM(a.shape, a.dtype) for a in list(srcs) + list(lands)]
    res = pl.pallas_call(
        body, name=name,
        out_shape=(pltpu.SemaphoreType.DMA((3 * n,)), pltpu.SemaphoreType.DMA((3 * n,)), *thru,
                   jax.ShapeDtypeStruct((8, 128), F32)),
        in_specs=[HBM_SPEC] * (2 * n) + ([pl.BlockSpec(memory_space=pl.ANY)] if dep is not None else []),
        out_specs=(SEM_SPEC, SEM_SPEC, *[HBM_SPEC] * (2 * n), pl.BlockSpec(memory_space=pltpu.VMEM)),
        input_output_aliases={i: 2 + i for i in range(2 * n)},
        compiler_params=pltpu.CompilerParams(has_side_effects=DATAFLOW))(
            *[_in_hbm(a) for a in srcs], *[_in_hbm(a) for a in lands], *([dep] if dep is not None else []))
    return (res[0], res[1], list(res[2:2 + n]), list(res[2 + n:2 + 2 * n])), res[-1]


def _exchange_wait(flight, after, src_slot, name):
    send_sems, recv_sems, srcs, lands = flight
    n = len(srcs)

    def body(*refs):
        for cp in _chip_copies(refs[:n], refs[n:2 * n], refs[2 * n], refs[2 * n + 1], src_slot):
            cp.wait_send()
            cp.wait_recv()

    thru = [pltpu.HBM(a.shape, a.dtype) for a in list(srcs) + list(lands)]
    res = pl.pallas_call(
        body, name=name, out_shape=thru,
        in_specs=[HBM_SPEC] * (2 * n) + [SEM_SPEC, SEM_SPEC, pl.BlockSpec(memory_space=pl.ANY)],
        out_specs=[HBM_SPEC] * (2 * n), input_output_aliases={i: i for i in range(2 * n)},
        compiler_params=pltpu.CompilerParams(has_side_effects=DATAFLOW))(*srcs, *lands, send_sems, recv_sems, after)
    return list(res[n:])


def _landing(own, me):
    return lax.dynamic_update_index_in_dim(lax.empty((4, *own.shape), own.dtype), own, me, 0)


def _swap_with_sibling(arrays):
    n = len(arrays)

    def body(*refs):
        ins, outs = refs[:n], refs[n:2 * n]
        send_sems, recv_sems = refs[2 * n:]
        x, y, c, _ = _place()
        copies = [pltpu.make_async_remote_copy(src_ref=ins[w], dst_ref=outs[w], send_sem=send_sems.at[w],
                                               recv_sem=recv_sems.at[w], device_id=(x, y, 1 - c), device_id_type=MESH)
                  for w in range(n)]
        for cp in copies:
            cp.start()
        for cp in copies:
            cp.wait()

    any_spec = pl.BlockSpec(memory_space=pl.ANY)
    return pl.pallas_call(
        body, name="swap_with_sibling", out_shape=[jax.ShapeDtypeStruct(a.shape, a.dtype) for a in arrays],
        in_specs=[any_spec] * n, out_specs=[any_spec] * n,
        scratch_shapes=[pltpu.SemaphoreType.DMA((n,)), pltpu.SemaphoreType.DMA((n,))])(*arrays)


def _as_rows(a):
    return a.reshape(-1, a.shape[-1])


def _row_tile(r, c, budget_bytes=1 << 20):
    tr = r
    while tr % 16 == 0 and tr * c * 4 > budget_bytes:
        tr //= 2
    return tr


def _sum_slots(layers, name):
    _, r, c = layers[0].shape
    tr = _row_tile(r, c)
    nt = r // tr
    acc = None
    for l, r4 in enumerate(layers):
        def kern(r_ref, *rest):
            o_ref = rest[-1]
            o_ref[...] = (((r_ref[0].astype(F32) + r_ref[1].astype(F32)) + r_ref[2].astype(F32))
                          + r_ref[3].astype(F32))

        out_spec = pl.BlockSpec((tr, c), lambda i, l=l: (l * nt + i, 0))
        acc = pl.pallas_call(
            kern, name=f"{name}_l{l}", grid=(nt,),
            in_specs=[pl.BlockSpec((4, tr, c), lambda i: (0, i, 0))] + ([pl.BlockSpec(memory_space=pl.ANY)] if l else []),
            out_specs=out_spec, out_shape=jax.ShapeDtypeStruct((len(layers) * r, c), F32),
            input_output_aliases={1: 0} if l else {},
            compiler_params=_params("parallel"))(*([r4, acc] if l else [r4]))
    return acc


def _adamw(w, m, v, parts, name):
    r, c = w.shape
    tr = _row_tile(r, c, 1 << 19)
    npart = len(parts)
    c1 = 1.0 - ADAM_B1 ** ADAM_STEP
    c2 = 1.0 - ADAM_B2 ** ADAM_STEP

    def kern(*refs):
        w_ref, m_ref, v_ref = refs[:3]
        p_refs = refs[3:3 + npart]
        g_ref, d_ref, mo_ref, vo_ref = refs[3 + npart:]
        g = p_refs[0][...]
        for p in p_refs[1:]:
            g = g + p[...]
        mn = ADAM_B1 * m_ref[...] + (1.0 - ADAM_B1) * g
        vn = ADAM_B2 * v_ref[...] + (1.0 - ADAM_B2) * (g * g)
        g_ref[...] = g
        mo_ref[...] = mn
        vo_ref[...] = vn
        d_ref[...] = -ADAM_LR * ((mn / c1) / (jnp.sqrt(vn / c2) + ADAM_EPS) + ADAM_WD * w_ref[...])

    blk = pl.BlockSpec((tr, c), lambda i: (i, 0))
    shape = jax.ShapeDtypeStruct((r, c), F32)
    return pl.pallas_call(
        kern, name=name, grid=(r // tr,), in_specs=[blk] * (3 + npart), out_specs=[blk] * 4, out_shape=[shape] * 4,
        compiler_params=_params("parallel"))(w, m, v, *parts)


def _sum_devices(g8, name):
    _, r, c = g8.shape

    def kern(g_ref, o_ref):
        tot = g_ref[0]
        for dev in range(1, 8):
            tot = tot + g_ref[dev]
        o_ref[...] = tot

    return pl.pallas_call(
        kern, name=name, grid=(1,), in_specs=[pl.BlockSpec((8, r, c), lambda i: (0, 0, 0))],
        out_specs=pl.BlockSpec((r, c), lambda i: (0, 0)), out_shape=jax.ShapeDtypeStruct((r, c), F32),
        compiler_params=_params("arbitrary"))(g8)


def _pad_lanes(a, width):
    return jnp.pad(a, [(0, 0)] * (a.ndim - 1) + [(0, width - a.shape[-1])])


def kernel(x, positions, ffn_norm1, ffn1_w1, ffn1_w3, ffn1_w2, mix_norm, ffn_norm2, ffn2_w1, ffn2_w3, ffn2_w2, conv_w_pw1, conv_w_dw, conv_norm, conv_w_pw2, mla_w_a, mla_q_norm, mla_kv_norm, mla_w_uq, mla_w_ukv, mla_w_o, final_norm, loss_target, m_ffn_norm1, m_ffn1_w1, m_ffn1_w3, m_ffn1_w2, m_mix_norm, m_ffn_norm2, m_ffn2_w1, m_ffn2_w3, m_ffn2_w2, m_conv_w_pw1, m_conv_w_dw, m_conv_norm, m_conv_w_pw2, m_mla_w_a, m_mla_q_norm, m_mla_kv_norm, m_mla_w_uq, m_mla_w_ukv, m_mla_w_o, m_final_norm, v_ffn_norm1, v_ffn1_w1, v_ffn1_w3, v_ffn1_w2, v_mix_norm, v_ffn_norm2, v_ffn2_w1, v_ffn2_w3, v_ffn2_w2, v_conv_w_pw1, v_conv_w_dw, v_conv_norm, v_conv_w_pw2, v_mla_w_a, v_mla_q_norm, v_mla_kv_norm, v_mla_w_uq, v_mla_w_ukv, v_mla_w_o, v_final_norm):
    given = locals()
    return _step({nm: given[nm] for nm in INPUTS})


def _step(A):
    x = A['x'][0]
    target = A['loss_target'][0]
    t, d = x.shape
    pos = A['positions'].reshape(t, 1)
    me = 2 * lax.axis_index("x") + lax.axis_index("y")

    ffn = [f'ffn{k}_{w}' for k in (1, 2) for w in ('w1', 'w3', 'w2')]
    gather_groups = [[(nm, 0) for nm in ffn[:3]],
                     [('conv_w_pw1', 0), ('conv_w_pw2', 0)] + [(nm, 0) for nm in ffn[3:]],
                     [(nm, 1) for nm in ffn[:3]] + [('mla_w_a', 0), ('mla_w_uq', 0), ('mla_w_ukv', 0), ('mla_w_o', 0)],
                     [(nm, 1) for nm in ffn[3:]]]
    gather_flights = {}
    big = {}

    def gather_start(gi, dep):
        shards = [_bf(A[nm][l]) for nm, l in gather_groups[gi]]
        gather_flights[gi], token = _exchange_start(shards, [_landing(s, me) for s in shards], False,
                                                    f"gather_start_{gi}", dep)
        return token

    def gather_wait(gi, after):
        landed = _exchange_wait(gather_flights[gi], after, False, f"gather_wait_{gi}")
        big.update(zip(gather_groups[gi], landed))
        return landed[0]

    dw_shard = A['conv_w_dw'][0]
    cw = dw_shard.shape[1]
    small = jnp.concatenate([
        jnp.pad(dw_shard, ((0, CONV_HALO - CONV_WIDTH), (0, 0))),
        jnp.pad(_pad_lanes(A['mla_q_norm'], cw), ((0, 7), (0, 0))),
        jnp.pad(_pad_lanes(A['mla_kv_norm'], cw), ((0, 7), (0, 0)))], axis=0)
    small = _all_gather_rows(small, "gather_small_weights").reshape(4, 2, 48, cw)[:, 0]
    w_dw = jnp.concatenate([small[j, :CONV_HALO] for j in range(4)], axis=1)
    gq = jnp.concatenate([small[j, CONV_HALO, :Q_LORA // 4] for j in range(4)])
    gkv = jnp.concatenate([small[j, CONV_HALO + 8, :KV_LORA // 4] for j in range(4)])

    def cols(nm, layer):
        return jnp.concatenate([big[nm, layer][j] for j in range(4)], axis=1)

    def rows(nm, layer):
        g = big[nm, layer]
        return g.reshape(-1, g.shape[-1])

    ffn_w = {}

    def ffn_weights(k, l):
        ffn_w[k, l] = (cols(f'ffn{k}_w1', l), cols(f'ffn{k}_w3', l), rows(f'ffn{k}_w2', l))
        return ffn_w[k, l]

    token = gather_start(0, None)
    cs_c, cs_s = _rope_tables(pos)
    h0 = x
    token = gather_start(1, gather_wait(0, token))
    h1, n01, z01a, z01b = _ffn_fwd(h0, A['ffn_norm1'][0], *ffn_weights(1, 0), token, "ffn1_l0_fwd")
    token = gather_start(3, gather_start(2, gather_wait(1, h1)))
    pw1 = big['conv_w_pw1', 0]
    pw1_a = jnp.concatenate([pw1[0], pw1[1]], axis=1)
    pw1_b = jnp.concatenate([pw1[2], pw1[3]], axis=1)
    pw2 = rows('conv_w_pw2', 0)
    m0 = _norm_fwd(h1, A['mix_norm'][0], token, "mix_norm_l0")
    ca, cb, glu = _glu_fwd(m0, pw1_a, pw1_b)
    cv, cs = _conv_fwd(glu, w_dw, A['conv_norm'][0])
    h2 = _mm([(cs, pw2)], F32, "conv_pw2_fwd", res=h1)
    h3, n02, z02a, z02b = _ffn_fwd(h2, A['ffn_norm2'][0], *ffn_weights(2, 0), token, "ffn2_l0_fwd")
    gather_wait(2, h3)
    w_a = _pad_lanes(rows('mla_w_a', 0), A_PAD)
    wuq = _pad_lanes(big['mla_w_uq', 0].reshape(Q_LORA, HEADS, NOPE + ROPE).transpose(1, 0, 2), HEAD_PAD)
    wukv = big['mla_w_ukv', 0].reshape(KV_LORA, HEADS, NOPE + V_HEAD).transpose(1, 0, 2)
    w_o = rows('mla_w_o', 0)
    h4, n11, z11a, z11b = _ffn_fwd(h3, A['ffn_norm1'][1], *ffn_weights(1, 1), token, "ffn1_l1_fwd")
    m1 = _norm_fwd(h4, A['mix_norm'][1], token, "mix_norm_l1")
    a_lat = _mm([(m1, w_a)], F32, "mla_down_fwd")
    cq, ckv, kr = _mla_prep(a_lat, gq, gkv, cs_c, cs_s)
    q, k, v = _mla_qkv(cq, ckv, kr, cs_c, cs_s, wuq, wukv)
    o, lse = _flash_fwd(q, k, v)
    h5 = _mm([(o, w_o)], F32, "mla_out_fwd", res=h4)
    gather_wait(3, h5)
    h6, n12, z12a, z12b = _ffn_fwd(h5, A['ffn_norm2'][1], *ffn_weights(2, 1), token, "ffn2_l1_fwd")

    def col_slots(g):
        r, c4 = g.shape
        return g.reshape(r, 4, c4 // 4).transpose(1, 0, 2)

    def row_slots(g):
        return g.reshape(4, g.shape[0] // 4, g.shape[1])

    scatter_flights = []

    def scatter_start(named):
        srcs = [g for _, g in named]
        lands = [_landing(lax.dynamic_index_in_dim(g, me, 0, keepdims=False), me) for g in srcs]
        flight, token = _exchange_start(srcs, lands, True, f"scatter_start_{len(scatter_flights)}")
        scatter_flights.append(([key for key, _ in named], flight))
        return token

    def send_ffn(k, l):
        return lambda dw1, dw3, dw2: scatter_start([((f'ffn{k}_w1', l), col_slots(dw1)),
                                                    ((f'ffn{k}_w3', l), col_slots(dw3)),
                                                    ((f'ffn{k}_w2', l), row_slots(dw2))])

    dh6, dg_final, loss_part = _loss_bwd(h6, target, A['final_norm'])
    dh5, dg_n2_l1, token = _ffn_bwd(dh6, h5, A['ffn_norm2'][1], n12, z12a, z12b, *ffn_w[2, 1], loss_part,
                                    send_ffn(2, 1), "ffn2_l1")

    do = _mm([(dh5, w_o)], BF16, "mla_out_bwd", trans_b=True)
    dw_o = _mm_tn(o, dh5, BF16, "mla_dw_o")
    delta = _attn_delta(do, o)
    dq, dk, dv = _flash_bwd(q, k, v, do, lse, delta)
    dr, dkv, dcq, dckv, dar = _mla_qkv_bwd(dq, dk, dv, cs_c, cs_s, wuq, wukv)
    dwuq = _mm_tn(cq, dr, BF16, "mla_dw_uq")
    dwukv = _mm_tn(ckv, dkv, BF16, "mla_dw_ukv")
    da_lat, dgq, dgkv = _mla_prep_bwd(a_lat, dcq, dckv, dar, gq, gkv)
    dw_a = _mm_tn(m1, da_lat, BF16, "mla_dw_a")
    token = scatter_start([
        (('mla_w_a', 0), row_slots(dw_a[:, :Q_LORA + KV_LORA + ROPE])),
        (('mla_w_uq', 0), dwuq[:, :, :NOPE + ROPE].transpose(1, 0, 2).reshape(4, Q_LORA // 4, HEADS, NOPE + ROPE)),
        (('mla_w_ukv', 0), dwukv.transpose(1, 0, 2).reshape(4, KV_LORA // 4, HEADS, NOPE + V_HEAD)),
        (('mla_w_o', 0), row_slots(dw_o))])
    dh4, dg_mix_l1 = _mm_normbwd([(da_lat, w_a)], h4, A['mix_norm'][1], dh5, token, "mla_down_bwd")

    dh3, dg_n1_l1, token = _ffn_bwd(dh4, h3, A['ffn_norm1'][1], n11, z11a, z11b, *ffn_w[1, 1], token,
                                    send_ffn(1, 1), "ffn1_l1")
    dh2, dg_n2_l0, token = _ffn_bwd(dh3, h2, A['ffn_norm2'][0], n02, z02a, z02b, *ffn_w[2, 0], token,
                                    send_ffn(2, 0), "ffn2_l0")

    dcv, dg_conv = _conv_bwd_norm(dh2, cv, pw2, A['conv_norm'][0], token)
    dw_pw2 = _mm_tn(cs, dh2, BF16, "conv_dw_pw2")
    dca, dcb, ddw = _conv_bwd_dw(dcv, glu, ca, cb, w_dw)
    dpw1_a = _mm_tn(m0, dca, BF16, "conv_dw_pw1a")
    dpw1_b = _mm_tn(m0, dcb, BF16, "conv_dw_pw1b")
    half = dpw1_a.shape[1] // 2
    token = scatter_start([
        (('conv_w_pw1', 0), jnp.stack([dpw1_a[:, :half], dpw1_a[:, half:], dpw1_b[:, :half], dpw1_b[:, half:]])),
        (('conv_w_pw2', 0), row_slots(dw_pw2))])
    dh1, dg_mix_l0 = _mm_normbwd([(dca, pw1_a), (dcb, pw1_b)], h1, A['mix_norm'][0], dh2, token, "conv_pw1_bwd")

    dx, dg_n1_l0, _ = _ffn_bwd(dh1, h0, A['ffn_norm1'][0], n01, z01a, z01b, *ffn_w[1, 0], token,
                               send_ffn(1, 0), "ffn1_l0")
    out = {}

    qkv_row = jnp.concatenate([dgq, dgkv, jnp.zeros((8, d - Q_LORA - KV_LORA), F32)], axis=1)
    loss_row = _pad_lanes(loss_part, d)
    small_g = jnp.concatenate([dg_n1_l0, dg_n1_l1, dg_mix_l0, dg_mix_l1, dg_n2_l0, dg_n2_l1, dg_conv, dg_final,
                               qkv_row, loss_row, ddw], axis=0)
    nrow = small_g.shape[0]
    tot = _sum_devices(_all_gather_rows(small_g, "gather_small_grads").reshape(8, nrow, d), "sum_small_grads")
    loss = tot[72, 0]
    q_shard = lax.dynamic_slice_in_dim(tot[64, :Q_LORA], me * (Q_LORA // 4), Q_LORA // 4)
    kv_shard = lax.dynamic_slice_in_dim(tot[64, Q_LORA:Q_LORA + KV_LORA], me * (KV_LORA // 4), KV_LORA // 4)
    dw_shard_g = lax.dynamic_slice_in_dim(tot[80:80 + CONV_WIDTH], me * cw, cw, axis=1)
    small_grads = {
        'ffn_norm1': jnp.stack([tot[0], tot[8]]), 'mix_norm': jnp.stack([tot[16], tot[24]]),
        'ffn_norm2': jnp.stack([tot[32], tot[40]]), 'conv_norm': tot[48][None], 'final_norm': tot[56],
        'mla_q_norm': q_shard[None], 'mla_kv_norm': kv_shard[None], 'conv_w_dw': dw_shard_g[None],
    }
    for nm, g in small_grads.items():
        res = _adamw(_as_rows(A[nm]) if A[nm].ndim > 1 else A[nm].reshape(1, -1),
                     A['m_' + nm].reshape(-1, A[nm].shape[-1]), A['v_' + nm].reshape(-1, A[nm].shape[-1]),
                     [g.reshape(-1, A[nm].shape[-1])], "adamw_" + nm)
        out[nm] = [r.reshape(A[nm].shape) for r in res]

    received = {}
    after = tot
    for si, (keys, flight) in enumerate(scatter_flights):
        landed = _exchange_wait(flight, after, True, f"scatter_wait_{si}")
        received.update(zip(keys, landed))
        after = landed[0]
    sums = [_sum_slots([received[nm, l].reshape(4, -1, received[nm, l].shape[-1]) for l in range(A[nm].shape[0])],
                       "sum_" + nm) for nm in BIG]
    sibling = _swap_with_sibling(sums)
    for nm, mine, theirs in zip(BIG, sums, sibling):
        res = _adamw(_as_rows(A[nm]), _as_rows(A['m_' + nm]), _as_rows(A['v_' + nm]), [mine, theirs], "adamw_" + nm)
        out[nm] = [r.reshape(A[nm].shape) for r in res]

    return (loss, dx[None], *[out[nm][0] for nm in WEIGHTS], *[out[nm][1] for nm in WEIGHTS],
            *[out[nm][2] for nm in WEIGHTS], *[out[nm][3] for nm in WEIGHTS])
```

```python
import functools

import jax
import jax.numpy as jnp
import numpy as np
from jax import lax
from jax.experimental import pallas as pl
from jax.experimental.pallas import tpu as pltpu

F32 = jnp.float32
BF16 = jnp.bfloat16
MESH = pl.DeviceIdType.MESH

RMS_EPS = 1e-6
HEADS = 8
NOPE = 128
ROPE = 64
HEAD_PAD = 256
V_HEAD = 128
Q_LORA = 512
KV_LORA = 256
A_PAD = 896
CHUNK = 64
CONV_WIDTH = 31
CONV_HALO = 32
CONV_ROWS = 16
ROPE_THETA = 10000.0
ATTN_SCALE = (NOPE + ROPE) ** -0.5
FFN_RES = 0.5

ADAM_LR = 0.001
ADAM_B1 = 0.9
ADAM_B2 = 0.999
ADAM_EPS = 1e-08
ADAM_WD = 0.01
ADAM_STEP = 10

VMEM_LIMIT_BYTES = 56 * 1024 * 1024

WEIGHTS = ['ffn_norm1', 'ffn1_w1', 'ffn1_w3', 'ffn1_w2', 'mix_norm', 'ffn_norm2', 'ffn2_w1', 'ffn2_w3', 'ffn2_w2',
           'conv_w_pw1', 'conv_w_dw', 'conv_norm', 'conv_w_pw2', 'mla_w_a', 'mla_q_norm', 'mla_kv_norm', 'mla_w_uq',
           'mla_w_ukv', 'mla_w_o', 'final_norm']
INPUTS = (['x', 'positions'] + WEIGHTS + ['loss_target'] + ['m_' + w for w in WEIGHTS] + ['v_' + w for w in WEIGHTS])
BIG = ['ffn1_w1', 'ffn1_w3', 'ffn1_w2', 'ffn2_w1', 'ffn2_w3', 'ffn2_w2', 'conv_w_pw1', 'conv_w_pw2', 'mla_w_a',
       'mla_w_uq', 'mla_w_ukv', 'mla_w_o']


def _params(*sem):
    return pltpu.CompilerParams(dimension_semantics=sem, vmem_limit_bytes=VMEM_LIMIT_BYTES)


def _bf(v):
    return v.astype(BF16)


def _rstd(x):
    return lax.rsqrt(jnp.mean(x * x, axis=-1, keepdims=True) + RMS_EPS)


def _sigmoid(x):
    return jax.nn.sigmoid(x)


def _rot(x):
    lane = lax.broadcasted_iota(jnp.int32, x.shape, 1)
    return jnp.where(lane < ROPE // 2, -pltpu.roll(x, 128 - ROPE // 2, 1), pltpu.roll(x, ROPE // 2, 1))


def _rot_t(y):
    lane = lax.broadcasted_iota(jnp.int32, y.shape, 1)
    return jnp.where(lane < ROPE // 2, pltpu.roll(y, 128 - ROPE // 2, 1), -pltpu.roll(y, ROPE // 2, 1))


def _pair_sum(a_refs, b_refs, trans_b):
    tot = None
    for a_r, b_r in zip(a_refs, b_refs):
        a, b = _bf(a_r[...]), _bf(b_r[...])
        if trans_b:
            d = lax.dot_general(a, b, (((1,), (1,)), ((), ())), preferred_element_type=F32)
        else:
            d = jnp.dot(a, b, preferred_element_type=F32)
        tot = d if tot is None else tot + d
    return tot


def _mm(pairs, out_dtype, name, *, trans_b=False, tm=512, tn=None, tk=None, res=None, dep=None):
    m, k = pairs[0][0].shape
    n = pairs[0][1].shape[0] if trans_b else pairs[0][1].shape[1]
    tm, tn, tk = min(tm, m), tn or n, tk or k
    nk, npair = k // tk, len(pairs)

    def kern(*refs):
        a_refs, b_refs = refs[:npair], refs[npair:2 * npair]
        rest = list(refs[2 * npair:])
        res_ref = rest.pop(0) if res is not None else None
        if dep is not None:
            rest.pop(0)
        o_ref = rest.pop(0)

        def finish(acc):
            if res_ref is not None:
                acc = res_ref[...] + acc
            o_ref[...] = acc.astype(o_ref.dtype)

        if nk == 1:
            finish(_pair_sum(a_refs, b_refs, trans_b))
        else:
            acc_ref = rest.pop(0)
            kk = pl.program_id(2)

            @pl.when(kk == 0)
            def _():
                acc_ref[...] = jnp.zeros_like(acc_ref)

            acc_ref[...] += _pair_sum(a_refs, b_refs, trans_b)

            @pl.when(kk == nk - 1)
            def _():
                finish(acc_ref[...])

    a_spec = pl.BlockSpec((tm, tk), lambda i, j, kk: (i, kk))
    b_spec = (pl.BlockSpec((tn, tk), lambda i, j, kk: (j, kk)) if trans_b
              else pl.BlockSpec((tk, tn), lambda i, j, kk: (kk, j)))
    io_spec = pl.BlockSpec((tm, tn), lambda i, j, kk: (i, j))
    in_specs = ([a_spec] * npair + [b_spec] * npair + ([io_spec] if res is not None else [])
                + ([pl.BlockSpec((8, 128), lambda i, j, kk: (0, 0))] if dep is not None else []))
    args = ([p[0] for p in pairs] + [p[1] for p in pairs] + ([res] if res is not None else [])
            + ([dep] if dep is not None else []))
    return pl.pallas_call(
        kern, name=name, grid=(m // tm, n // tn, nk), in_specs=in_specs, out_specs=io_spec,
        out_shape=jax.ShapeDtypeStruct((m, n), out_dtype),
        scratch_shapes=[pltpu.VMEM((tm, tn), F32)] if nk > 1 else [],
        compiler_params=_params("parallel", "parallel", "arbitrary"))(*args)


def _mm_normbwd(pairs, h, g, dres, dep, name, *, tm=512, tk=None):
    m, k = pairs[0][0].shape
    d = pairs[0][1].shape[0]
    tm, tk = min(tm, m), tk or k
    nk, npair = k // tk, len(pairs)

    def kern(*refs):
        a_refs, b_refs = refs[:npair], refs[npair:2 * npair]
        h_ref, g_ref, dres_ref, _, o_ref, dg_ref, acc_ref = refs[2 * npair:]
        i, kk = pl.program_id(0), pl.program_id(1)

        @pl.when(jnp.logical_and(i == 0, kk == 0))
        def _():
            dg_ref[...] = jnp.zeros_like(dg_ref)

        @pl.when(kk == 0)
        def _():
            acc_ref[...] = jnp.zeros_like(acc_ref)

        acc_ref[...] += _pair_sum(a_refs, b_refs, True)

        @pl.when(kk == nk - 1)
        def _():
            dn = acc_ref[...]
            x = h_ref[...]
            rstd = _rstd(x)
            xhat = x * rstd
            dg_ref[...] += jnp.broadcast_to(jnp.sum(dn * xhat, axis=0, keepdims=True), dg_ref.shape)
            dxh = dn * g_ref[...]
            dx = rstd * (dxh - xhat * jnp.mean(dxh * xhat, axis=-1, keepdims=True))
            o_ref[...] = dres_ref[...] + dx

    row = pl.BlockSpec((tm, d), lambda i, kk: (i, 0))
    in_specs = ([pl.BlockSpec((tm, tk), lambda i, kk: (i, kk))] * npair
                + [pl.BlockSpec((d, tk), lambda i, kk: (0, kk))] * npair
                + [row, pl.BlockSpec((1, d), lambda i, kk: (0, 0)), row, pl.BlockSpec((8, 128), lambda i, kk: (0, 0))])
    return pl.pallas_call(
        kern, name=name, grid=(m // tm, nk), in_specs=in_specs,
        out_specs=[row, pl.BlockSpec((8, d), lambda i, kk: (0, 0))],
        out_shape=[jax.ShapeDtypeStruct((m, d), F32), jax.ShapeDtypeStruct((8, d), F32)],
        scratch_shapes=[pltpu.VMEM((tm, d), F32)],
        compiler_params=_params("arbitrary", "arbitrary"))(
            *[p[0] for p in pairs], *[p[1] for p in pairs], h, g.reshape(1, d), dres, dep)


def _mm_tn(a, b, out_dtype, name, *, bm=None, bn=None, tk=512):
    t, m = a.shape
    batched = b.ndim == 3
    n = b.shape[-1]
    nb = b.shape[0] if batched else 1
    bm, bn, tk = bm or m, bn or n, min(tk, t)
    nk = t // tk

    def kern(a_ref, b_ref, o_ref, acc_ref):
        kk = pl.program_id(3)

        @pl.when(kk == 0)
        def _():
            acc_ref[...] = jnp.zeros_like(acc_ref)

        acc_ref[...] += lax.dot_general(_bf(a_ref[...]), _bf(b_ref[...]), (((0,), (0,)), ((), ())),
                                        preferred_element_type=F32)

        @pl.when(kk == nk - 1)
        def _():
            o_ref[...] = acc_ref[...].astype(o_ref.dtype)

    a_spec = pl.BlockSpec((tk, bm), lambda h, i, j, kk: (kk, i))
    if batched:
        b_spec = pl.BlockSpec((None, tk, bn), lambda h, i, j, kk: (h, kk, j))
        o_spec = pl.BlockSpec((None, bm, bn), lambda h, i, j, kk: (h, i, j))
        out_shape = jax.ShapeDtypeStruct((nb, m, n), out_dtype)
    else:
        b_spec = pl.BlockSpec((tk, bn), lambda h, i, j, kk: (kk, j))
        o_spec = pl.BlockSpec((bm, bn), lambda h, i, j, kk: (i, j))
        out_shape = jax.ShapeDtypeStruct((m, n), out_dtype)
    return pl.pallas_call(
        kern, name=name, grid=(nb, m // bm, n // bn, nk), in_specs=[a_spec, b_spec], out_specs=o_spec,
        out_shape=out_shape, scratch_shapes=[pltpu.VMEM((bm, bn), F32)],
        compiler_params=_params("parallel", "parallel", "parallel", "arbitrary"))(a, b)


def _ffn_tile(f):
    return f // 2 if (f // 2) % 128 == 0 else f


def _ffn_fwd(h, g, w1, w3, w2, dep, name):
    t, d = h.shape
    f = w1.shape[1]
    tm, tf = min(512, t), _ffn_tile(f)
    nf = f // tf

    def kern(h_ref, g_ref, w1_ref, w3_ref, w2_ref, dep_ref, ho_ref, n_ref, z1_ref, z3_ref, n_sc, acc_ref):
        j = pl.program_id(1)

        @pl.when(j == 0)
        def _():
            x = h_ref[...]
            n = _bf(x * _rstd(x) * g_ref[...])
            n_sc[...] = n
            n_ref[...] = n
            acc_ref[...] = jnp.zeros_like(acc_ref)

        n = n_sc[...]
        z1 = jnp.dot(n, w1_ref[...], preferred_element_type=F32)
        z3 = jnp.dot(n, w3_ref[...], preferred_element_type=F32)
        z1_ref[...] = _bf(z1)
        z3_ref[...] = _bf(z3)
        act = _bf(z1 * _sigmoid(z1) * z3)
        acc_ref[...] += jnp.dot(act, w2_ref[...], preferred_element_type=F32)

        @pl.when(j == nf - 1)
        def _():
            ho_ref[...] = h_ref[...] + FFN_RES * acc_ref[...]

    row = pl.BlockSpec((tm, d), lambda i, j: (i, 0))
    col = pl.BlockSpec((tm, tf), lambda i, j: (i, j))
    return pl.pallas_call(
        kern, name=name, grid=(t // tm, nf),
        in_specs=[row, pl.BlockSpec((1, d), lambda i, j: (0, 0)), pl.BlockSpec((d, tf), lambda i, j: (0, j)),
                  pl.BlockSpec((d, tf), lambda i, j: (0, j)), pl.BlockSpec((tf, d), lambda i, j: (j, 0)),
                  pl.BlockSpec((8, 128), lambda i, j: (0, 0))],
        out_specs=[row, row, col, col],
        out_shape=[jax.ShapeDtypeStruct((t, d), F32), jax.ShapeDtypeStruct((t, d), BF16),
                   jax.ShapeDtypeStruct((t, f), BF16), jax.ShapeDtypeStruct((t, f), BF16)],
        scratch_shapes=[pltpu.VMEM((tm, d), BF16), pltpu.VMEM((tm, d), F32)],
        compiler_params=_params("parallel", "arbitrary"))(h, g.reshape(1, d), w1, w3, w2, dep)


def _ffn_bwd_x(dh, h_in, g, z1, z3, w1, w3, w2, dep, name):
    t, d = dh.shape
    f = z1.shape[1]
    tm, tf = min(1024, t), 256
    nf = f // tf

    def kern(dh_ref, h_ref, g_ref, z1_ref, z3_ref, w2_ref, w1_ref, w3_ref, dep_ref,
             o_ref, dg_ref, dz1_ref, dz3_ref, a_ref, df_ref, df_sc, acc_ref):
        i, j = pl.program_id(0), pl.program_id(1)

        @pl.when(jnp.logical_and(i == 0, j == 0))
        def _():
            dg_ref[...] = jnp.zeros_like(dg_ref)

        @pl.when(j == 0)
        def _():
            df = _bf(FFN_RES * dh_ref[...])
            df_sc[...] = df
            df_ref[...] = df
            acc_ref[...] = jnp.zeros_like(acc_ref)

        da = lax.dot_general(df_sc[...], w2_ref[...], (((1,), (1,)), ((), ())), preferred_element_type=F32)
        z1v, z3v = z1_ref[...].astype(F32), z3_ref[...].astype(F32)
        sig = _sigmoid(z1v)
        silu = z1v * sig
        a_ref[...] = _bf(silu * z3v)
        dz1 = _bf(da * z3v * (sig * (1.0 + z1v * (1.0 - sig))))
        dz3 = _bf(da * silu)
        dz1_ref[...] = dz1
        dz3_ref[...] = dz3
        acc_ref[...] += (lax.dot_general(dz1, w1_ref[...], (((1,), (1,)), ((), ())), preferred_element_type=F32)
                         + lax.dot_general(dz3, w3_ref[...], (((1,), (1,)), ((), ())), preferred_element_type=F32))

        @pl.when(j == nf - 1)
        def _():
            dn = acc_ref[...]
            x = h_ref[...]
            rstd = _rstd(x)
            xhat = x * rstd
            dg_ref[...] += jnp.broadcast_to(jnp.sum(dn * xhat, axis=0, keepdims=True), dg_ref.shape)
            dxh = dn * g_ref[...]
            o_ref[...] = dh_ref[...] + rstd * (dxh - xhat * jnp.mean(dxh * xhat, axis=-1, keepdims=True))

    row = pl.BlockSpec((tm, d), lambda i, j: (i, 0))
    col = pl.BlockSpec((tm, tf), lambda i, j: (i, j))
    wcol = pl.BlockSpec((d, tf), lambda i, j: (0, j))
    colshape = jax.ShapeDtypeStruct((t, f), BF16)
    return pl.pallas_call(
        kern, name=name, grid=(t // tm, nf),
        in_specs=[row, row, pl.BlockSpec((1, d), lambda i, j: (0, 0)), col, col,
                  pl.BlockSpec((tf, d), lambda i, j: (j, 0)), wcol, wcol, pl.BlockSpec((8, 128), lambda i, j: (0, 0))],
        out_specs=[row, pl.BlockSpec((8, d), lambda i, j: (0, 0)), col, col, col, row],
        out_shape=[jax.ShapeDtypeStruct((t, d), F32), jax.ShapeDtypeStruct((8, d), F32), colshape, colshape, colshape,
                   jax.ShapeDtypeStruct((t, d), BF16)],
        scratch_shapes=[pltpu.VMEM((tm, d), BF16), pltpu.VMEM((tm, d), F32)],
        compiler_params=_params("arbitrary", "arbitrary"))(dh, h_in, g.reshape(1, d), z1, z3, w2, w1, w3, dep)


def _ffn_bwd(dh, h_in, g, n, z1, z3, w1, w3, w2, dep, tag):
    f = w1.shape[1]
    dh_in, dg, dz1, dz3, act, df = _ffn_bwd_x(dh, h_in, g, z1, z3, w1, w3, w2, dep, tag + "_bwd_x")
    dw1 = _mm_tn(n, dz1, BF16, tag + "_dw1", bn=_ffn_tile(f))
    dw3 = _mm_tn(n, dz3, BF16, tag + "_dw3", bn=_ffn_tile(f))
    dw2 = _mm_tn(act, df, BF16, tag + "_dw2", bm=_ffn_tile(f))
    return dh_in, dg, dw1, dw3, dw2


def _norm_fwd(h, g, dep, name):
    t, d = h.shape
    tm = min(512, t)

    def kern(h_ref, g_ref, dep_ref, o_ref):
        x = h_ref[...]
        o_ref[...] = _bf(x * _rstd(x) * g_ref[...])

    row = pl.BlockSpec((tm, d), lambda i: (i, 0))
    return pl.pallas_call(
        kern, name=name, grid=(t // tm,),
        in_specs=[row, pl.BlockSpec((1, d), lambda i: (0, 0)), pl.BlockSpec((8, 128), lambda i: (0, 0))],
        out_specs=row, out_shape=jax.ShapeDtypeStruct((t, d), BF16),
        compiler_params=_params("parallel"))(h, g.reshape(1, d), dep)


def _loss_bwd(h, target, g):
    t, d = h.shape
    tm = min(512, t)

    def kern(h_ref, t_ref, g_ref, dh_ref, dg_ref, loss_ref):
        @pl.when(pl.program_id(0) == 0)
        def _():
            dg_ref[...] = jnp.zeros_like(dg_ref)
            loss_ref[...] = jnp.zeros_like(loss_ref)

        x = h_ref[...]
        rstd = _rstd(x)
        xhat = x * rstd
        err = xhat * g_ref[...] - t_ref[...]
        row_loss = jnp.sum(err * err, axis=-1, keepdims=True) * (0.5 / d)
        loss_ref[...] += jnp.broadcast_to(jnp.sum(row_loss, axis=0, keepdims=True), loss_ref.shape)
        dy = err * (1.0 / d)
        dg_ref[...] += jnp.broadcast_to(jnp.sum(dy * xhat, axis=0, keepdims=True), dg_ref.shape)
        dxh = dy * g_ref[...]
        dh_ref[...] = rstd * (dxh - xhat * jnp.mean(dxh * xhat, axis=-1, keepdims=True))

    row = pl.BlockSpec((tm, d), lambda i: (i, 0))
    return pl.pallas_call(
        kern, name="loss_bwd", grid=(t // tm,),
        in_specs=[row, row, pl.BlockSpec((1, d), lambda i: (0, 0))],
        out_specs=[row, pl.BlockSpec((8, d), lambda i: (0, 0)), pl.BlockSpec((8, 128), lambda i: (0, 0))],
        out_shape=[jax.ShapeDtypeStruct((t, d), F32), jax.ShapeDtypeStruct((8, d), F32),
                   jax.ShapeDtypeStruct((8, 128), F32)],
        compiler_params=_params("arbitrary"))(h, target, g.reshape(1, d))


def _glu_fwd(m, wa, wb):
    t, d = m.shape
    c = wa.shape[1]
    tm, tc = min(512, t), min(512, c)

    def kern(m_ref, wa_ref, wb_ref, a_ref, b_ref, glu_ref):
        mv = m_ref[...]
        a = jnp.dot(mv, wa_ref[...], preferred_element_type=F32)
        b = jnp.dot(mv, wb_ref[...], preferred_element_type=F32)
        a_ref[...] = _bf(a)
        b_ref[...] = _bf(b)
        glu_ref[...] = _bf(a * _sigmoid(b))

    col = pl.BlockSpec((tm, tc), lambda i, j: (i, j))
    wspec = pl.BlockSpec((d, tc), lambda i, j: (0, j))
    shape = jax.ShapeDtypeStruct((t, c), BF16)
    return pl.pallas_call(
        kern, name="conv_glu_fwd", grid=(t // tm, c // tc),
        in_specs=[pl.BlockSpec((tm, d), lambda i, j: (i, 0)), wspec, wspec], out_specs=[col, col, col],
        out_shape=[shape, shape, shape], compiler_params=_params("parallel", "parallel"))(m, wa, wb)


def _conv_tile(t):
    return min(256, t)


def _conv_fwd(glu, w_dw, g):
    t, c = glu.shape
    tm = _conv_tile(t)
    hb = tm // CONV_HALO

    def kern(cur_ref, halo_ref, w_ref, g_ref, cv_ref, s_ref, ext):
        i = pl.program_id(0)
        ext[0:CONV_HALO, :] = jnp.where(i > 0, halo_ref[...].astype(F32), 0.0)
        ext[CONV_HALO:, :] = cur_ref[...].astype(F32)
        gv = g_ref[...]
        for r0 in range(0, tm, CONV_ROWS):
            acc = jnp.zeros((CONV_ROWS, c), F32)
            for k in range(CONV_WIDTH):
                acc = acc + ext[pl.ds(r0 + 2 + k, CONV_ROWS), :] * w_ref[k:k + 1, :]
            cv_ref[r0:r0 + CONV_ROWS, :] = acc
            rn = acc * _rstd(acc) * gv
            s_ref[r0:r0 + CONV_ROWS, :] = _bf(rn * _sigmoid(rn))

    row = pl.BlockSpec((tm, c), lambda i: (i, 0))
    return pl.pallas_call(
        kern, name="conv_fwd", grid=(t // tm,),
        in_specs=[row, pl.BlockSpec((CONV_HALO, c), lambda i: (jnp.maximum(i * hb - 1, 0), 0)),
                  pl.BlockSpec((CONV_HALO, c), lambda i: (0, 0)), pl.BlockSpec((1, c), lambda i: (0, 0))],
        out_specs=[row, row],
        out_shape=[jax.ShapeDtypeStruct((t, c), F32), jax.ShapeDtypeStruct((t, c), BF16)],
        scratch_shapes=[pltpu.VMEM((tm + CONV_HALO, c), F32)],
        compiler_params=_params("parallel"))(glu, glu, w_dw, g.reshape(1, c))


def _conv_bwd_norm(dh, cv, w_pw2, g, dep):
    t, c = cv.shape
    tm = min(512, t)

    def kern(dh_ref, cv_ref, w_ref, g_ref, dep_ref, dcv_ref, dg_ref):
        @pl.when(pl.program_id(0) == 0)
        def _():
            dg_ref[...] = jnp.zeros_like(dg_ref)

        ds = lax.dot_general(_bf(dh_ref[...]), w_ref[...], (((1,), (1,)), ((), ())), preferred_element_type=F32)
        x = cv_ref[...]
        rstd = _rstd(x)
        xhat = x * rstd
        rn = xhat * g_ref[...]
        sig = _sigmoid(rn)
        drn = ds * (sig * (1.0 + rn * (1.0 - sig)))
        dg_ref[...] += jnp.broadcast_to(jnp.sum(drn * xhat, axis=0, keepdims=True), dg_ref.shape)
        dxh = drn * g_ref[...]
        dcv_ref[...] = rstd * (dxh - xhat * jnp.mean(dxh * xhat, axis=-1, keepdims=True))

    row = pl.BlockSpec((tm, c), lambda i: (i, 0))
    return pl.pallas_call(
        kern, name="conv_bwd_norm", grid=(t // tm,),
        in_specs=[pl.BlockSpec((tm, dh.shape[1]), lambda i: (i, 0)), row,
                  pl.BlockSpec(w_pw2.shape, lambda i: (0, 0)), pl.BlockSpec((1, c), lambda i: (0, 0)),
                  pl.BlockSpec((8, 128), lambda i: (0, 0))],
        out_specs=[row, pl.BlockSpec((8, c), lambda i: (0, 0))],
        out_shape=[jax.ShapeDtypeStruct((t, c), F32), jax.ShapeDtypeStruct((8, c), F32)],
        compiler_params=_params("arbitrary"))(dh, cv, w_pw2, g.reshape(1, c), dep)


def _conv_bwd_dw(dcv, glu, a, b, w_dw):
    t, c = dcv.shape
    tm = _conv_tile(t)
    hb = tm // CONV_HALO
    last = t // CONV_HALO - 1

    def kern(dcv_ref, dnext_ref, glu_ref, gprev_ref, a_ref, b_ref, w_ref, da_ref, db_ref, dw_ref, dext, gext):
        i = pl.program_id(0)

        @pl.when(i == 0)
        def _():
            dw_ref[...] = jnp.zeros_like(dw_ref)

        dext[0:tm, :] = dcv_ref[...]
        dext[tm:, :] = jnp.where(i < t // tm - 1, dnext_ref[...], 0.0)
        gext[0:CONV_HALO, :] = jnp.where(i > 0, gprev_ref[...].astype(F32), 0.0)
        gext[CONV_HALO:, :] = glu_ref[...].astype(F32)
        for r0 in range(0, tm, CONV_ROWS):
            acc = jnp.zeros((CONV_ROWS, c), F32)
            for k in range(CONV_WIDTH):
                acc = acc + dext[pl.ds(r0 + CONV_WIDTH - 1 - k, CONV_ROWS), :] * w_ref[k:k + 1, :]
            av = a_ref[r0:r0 + CONV_ROWS, :].astype(F32)
            sig = _sigmoid(b_ref[r0:r0 + CONV_ROWS, :].astype(F32))
            da_ref[r0:r0 + CONV_ROWS, :] = _bf(acc * sig)
            db_ref[r0:r0 + CONV_ROWS, :] = _bf(acc * av * sig * (1.0 - sig))
        for k in range(CONV_WIDTH):
            acc = jnp.zeros((CONV_ROWS, c), F32)
            for r0 in range(0, tm, CONV_ROWS):
                acc = acc + gext[pl.ds(r0 + 2 + k, CONV_ROWS), :] * dext[r0:r0 + CONV_ROWS, :]
            dw_ref[k:k + 1, :] += jnp.sum(acc, axis=0, keepdims=True)

    row = pl.BlockSpec((tm, c), lambda i: (i, 0))
    shape = jax.ShapeDtypeStruct((t, c), BF16)
    return pl.pallas_call(
        kern, name="conv_bwd_dw", grid=(t // tm,),
        in_specs=[row, pl.BlockSpec((CONV_HALO, c), lambda i: (jnp.minimum((i + 1) * hb, last), 0)),
                  row, pl.BlockSpec((CONV_HALO, c), lambda i: (jnp.maximum(i * hb - 1, 0), 0)),
                  row, row, pl.BlockSpec((CONV_HALO, c), lambda i: (0, 0))],
        out_specs=[row, row, pl.BlockSpec((CONV_HALO, c), lambda i: (0, 0))],
        out_shape=[shape, shape, jax.ShapeDtypeStruct((CONV_HALO, c), F32)],
        scratch_shapes=[pltpu.VMEM((tm + CONV_HALO, c), F32), pltpu.VMEM((tm + CONV_HALO, c), F32)],
        compiler_params=_params("arbitrary"))(dcv, dcv, glu, glu, a, b, w_dw)


def _rope_tables(pos):
    t = pos.shape[0]
    tm = min(512, t)
    freq = (np.float32(ROPE_THETA) ** (np.float32(-2.0) * np.arange(ROPE // 2, dtype=np.float32)
                                       / np.float32(ROPE))).astype(np.float32)
    row = np.zeros((2, 128), np.float32)
    row[0, :ROPE] = np.concatenate([freq, freq])
    row[1, :ROPE] = 1.0

    def kern(pos_ref, f_ref, c_ref, s_ref):
        ang = pos_ref[...].astype(F32) * f_ref[0:1, :]
        mask = f_ref[1:2, :]
        c_ref[...] = jnp.cos(ang) * mask
        s_ref[...] = jnp.sin(ang) * mask

    out = pl.BlockSpec((tm, 128), lambda i: (i, 0))
    shape = jax.ShapeDtypeStruct((t, 128), F32)
    return pl.pallas_call(
        kern, name="rope_tables", grid=(t // tm,),
        in_specs=[pl.BlockSpec((tm, 1), lambda i: (i, 0)), pl.BlockSpec((2, 128), lambda i: (0, 0))],
        out_specs=[out, out], out_shape=[shape, shape], compiler_params=_params("parallel"))(pos, jnp.asarray(row))


def _mla_prep(a, gq, gkv, cs_c, cs_s):
    t = a.shape[0]
    tm = min(512, t)
    kv0, r0 = Q_LORA, Q_LORA + KV_LORA

    def kern(a_ref, gq_ref, gkv_ref, c_ref, s_ref, cq_ref, ckv_ref, kr_ref):
        aq = a_ref[:, 0:kv0]
        akv = a_ref[:, kv0:r0]
        ar = a_ref[:, r0:A_PAD]
        cq_ref[...] = _bf(aq * _rstd(aq) * gq_ref[...])
        ckv_ref[...] = _bf(akv * _rstd(akv) * gkv_ref[...])
        kr_ref[...] = _bf(ar * c_ref[...] + _rot(ar) * s_ref[...])

    def row(w):
        return pl.BlockSpec((tm, w), lambda i: (i, 0))

    def vec(w):
        return pl.BlockSpec((1, w), lambda i: (0, 0))

    return pl.pallas_call(
        kern, name="mla_prep", grid=(t // tm,),
        in_specs=[row(A_PAD), vec(Q_LORA), vec(KV_LORA), row(128), row(128)],
        out_specs=[row(Q_LORA), row(KV_LORA), row(128)],
        out_shape=[jax.ShapeDtypeStruct((t, Q_LORA), BF16), jax.ShapeDtypeStruct((t, KV_LORA), BF16),
                   jax.ShapeDtypeStruct((t, 128), BF16)],
        compiler_params=_params("parallel"))(a, gq.reshape(1, -1), gkv.reshape(1, -1), cs_c, cs_s)


def _mla_prep_bwd(a, dcq, dckv, dar, gq, gkv):
    t = a.shape[0]
    tm = min(512, t)
    kv0, r0 = Q_LORA, Q_LORA + KV_LORA

    def kern(a_ref, dcq_ref, dckv_ref, dar_ref, gq_ref, gkv_ref, da_ref, dgq_ref, dgkv_ref):
        @pl.when(pl.program_id(0) == 0)
        def _():
            dgq_ref[...] = jnp.zeros_like(dgq_ref)
            dgkv_ref[...] = jnp.zeros_like(dgkv_ref)

        def back(x, dy, g_ref, dg_ref):
            rstd = _rstd(x)
            xhat = x * rstd
            dg_ref[...] += jnp.broadcast_to(jnp.sum(dy * xhat, axis=0, keepdims=True), dg_ref.shape)
            dxh = dy * g_ref[...]
            return rstd * (dxh - xhat * jnp.mean(dxh * xhat, axis=-1, keepdims=True))

        da_ref[:, 0:kv0] = _bf(back(a_ref[:, 0:kv0], dcq_ref[...], gq_ref, dgq_ref))
        da_ref[:, kv0:r0] = _bf(back(a_ref[:, kv0:r0], dckv_ref[...], gkv_ref, dgkv_ref))
        da_ref[:, r0:A_PAD] = _bf(dar_ref[...])

    def row(w):
        return pl.BlockSpec((tm, w), lambda i: (i, 0))

    def vec(r, w):
        return pl.BlockSpec((r, w), lambda i: (0, 0))

    return pl.pallas_call(
        kern, name="mla_prep_bwd", grid=(t // tm,),
        in_specs=[row(A_PAD), row(Q_LORA), row(KV_LORA), row(128), vec(1, Q_LORA), vec(1, KV_LORA)],
        out_specs=[row(A_PAD), vec(8, Q_LORA), vec(8, KV_LORA)],
        out_shape=[jax.ShapeDtypeStruct((t, A_PAD), BF16), jax.ShapeDtypeStruct((8, Q_LORA), F32),
                   jax.ShapeDtypeStruct((8, KV_LORA), F32)],
        compiler_params=_params("arbitrary"))(a, dcq, dckv, dar, gq.reshape(1, -1), gkv.reshape(1, -1))


def _mla_qkv(cq, ckv, kr, cs_c, cs_s, wuq, wukv):
    t = cq.shape[0]
    tm = min(512, t)

    def kern(cq_ref, ckv_ref, kr_ref, c_ref, s_ref, wq_ref, wkv_ref, q_ref, k_ref, v_ref):
        r = jnp.dot(cq_ref[...], wq_ref[...], preferred_element_type=F32)
        xr = r[:, NOPE:]
        q_ref[:, 0:NOPE] = _bf(r[:, 0:NOPE] * ATTN_SCALE)
        q_ref[:, NOPE:] = _bf((xr * c_ref[...] + _rot(xr) * s_ref[...]) * ATTN_SCALE)
        kv = jnp.dot(ckv_ref[...], wkv_ref[...], preferred_element_type=F32)
        k_ref[:, 0:NOPE] = _bf(kv[:, 0:NOPE])
        k_ref[:, NOPE:] = kr_ref[...]
        v_ref[...] = _bf(kv[:, NOPE:])

    def row(w):
        return pl.BlockSpec((tm, w), lambda i, h: (i, 0))

    def head(w):
        return pl.BlockSpec((None, tm, w), lambda i, h: (h, i, 0))

    return pl.pallas_call(
        kern, name="mla_qkv", grid=(t // tm, HEADS),
        in_specs=[row(Q_LORA), row(KV_LORA), row(128), row(128), row(128),
                  pl.BlockSpec((None, Q_LORA, HEAD_PAD), lambda i, h: (h, 0, 0)),
                  pl.BlockSpec((None, KV_LORA, NOPE + V_HEAD), lambda i, h: (h, 0, 0))],
        out_specs=[head(HEAD_PAD), head(HEAD_PAD), head(V_HEAD)],
        out_shape=[jax.ShapeDtypeStruct((HEADS, t, HEAD_PAD), BF16), jax.ShapeDtypeStruct((HEADS, t, HEAD_PAD), BF16),
                   jax.ShapeDtypeStruct((HEADS, t, V_HEAD), BF16)],
        compiler_params=_params("parallel", "arbitrary"))(cq, ckv, kr, cs_c, cs_s, wuq, wukv)


def _mla_qkv_bwd(dq, dk, dv, cs_c, cs_s, wuq, wukv):
    t = dq.shape[1]
    tm = min(512, t)

    def kern(dq_ref, dk_ref, dv_ref, c_ref, s_ref, wq_ref, wkv_ref, dr_ref, dkv_ref, dcq_ref, dckv_ref, dar_ref):
        @pl.when(pl.program_id(1) == 0)
        def _():
            dcq_ref[...] = jnp.zeros_like(dcq_ref)
            dckv_ref[...] = jnp.zeros_like(dckv_ref)
            dar_ref[...] = jnp.zeros_like(dar_ref)

        cv, sv = c_ref[...], s_ref[...]
        dqx = dq_ref[:, NOPE:]
        dr_ref[:, 0:NOPE] = _bf(dq_ref[:, 0:NOPE] * ATTN_SCALE)
        dr_ref[:, NOPE:] = _bf((dqx * cv + _rot_t(dqx * sv)) * ATTN_SCALE)
        dcq_ref[...] += lax.dot_general(dr_ref[...], wq_ref[...], (((1,), (1,)), ((), ())),
                                        preferred_element_type=F32)
        dkx = dk_ref[:, NOPE:]
        dar_ref[...] += dkx * cv + _rot_t(dkx * sv)
        dkv_ref[:, 0:NOPE] = _bf(dk_ref[:, 0:NOPE])
        dkv_ref[:, NOPE:] = _bf(dv_ref[...])
        dckv_ref[...] += lax.dot_general(dkv_ref[...], wkv_ref[...], (((1,), (1,)), ((), ())),
                                         preferred_element_type=F32)

    def row(w):
        return pl.BlockSpec((tm, w), lambda i, h: (i, 0))

    def head(w):
        return pl.BlockSpec((None, tm, w), lambda i, h: (h, i, 0))

    return pl.pallas_call(
        kern, name="mla_qkv_bwd", grid=(t // tm, HEADS),
        in_specs=[head(HEAD_PAD), head(HEAD_PAD), head(V_HEAD), row(128), row(128),
                  pl.BlockSpec((None, Q_LORA, HEAD_PAD), lambda i, h: (h, 0, 0)),
                  pl.BlockSpec((None, KV_LORA, NOPE + V_HEAD), lambda i, h: (h, 0, 0))],
        out_specs=[head(HEAD_PAD), head(NOPE + V_HEAD), row(Q_LORA), row(KV_LORA), row(128)],
        out_shape=[jax.ShapeDtypeStruct((HEADS, t, HEAD_PAD), BF16),
                   jax.ShapeDtypeStruct((HEADS, t, NOPE + V_HEAD), BF16),
                   jax.ShapeDtypeStruct((t, Q_LORA), F32), jax.ShapeDtypeStruct((t, KV_LORA), F32),
                   jax.ShapeDtypeStruct((t, 128), F32)],
        compiler_params=_params("parallel", "arbitrary"))(dq, dk, dv, cs_c, cs_s, wuq, wukv)


def _attn_block(t):
    return 512 if t >= 4096 else 128


def _chunk_mask(bk, bq):
    kc = lax.broadcasted_iota(jnp.int32, (bk, bq), 0) // CHUNK
    qc = lax.broadcasted_iota(jnp.int32, (bk, bq), 1) // CHUNK
    return qc >= kc


def _flash_fwd(q, k, v):
    t = q.shape[1]
    bq = _attn_block(t)
    nq = t // bq

    def kern(q_ref, k_ref, v_ref, o_ref, lse_ref):
        i = pl.program_id(1)
        qa, qb = q_ref[0:bq, :], q_ref[bq:2 * bq, :]

        def block(j):
            rows = pl.ds(pl.multiple_of(j * bq, bq), bq)
            return k_ref[rows, :], v_ref[rows, :]

        def scores(kj, qv):
            return lax.dot_general(kj, qv, (((1,), (1,)), ((), ())), preferred_element_type=F32)

        def update(st, vj, m, l, acc):
            m_new = jnp.maximum(m, jnp.max(st, axis=0, keepdims=True))
            alpha = jnp.exp(m - m_new)
            p = jnp.exp(st - m_new)
            pv = lax.dot_general(vj, _bf(p), (((0,), (0,)), ((), ())), preferred_element_type=F32)
            return m_new, alpha * l + jnp.sum(p, axis=0, keepdims=True), alpha * acc + pv

        start = (jnp.full((1, bq), -1e30, F32), jnp.zeros((1, bq), F32), jnp.zeros((V_HEAD, bq), F32))
        mask = _chunk_mask(bq, bq)
        k0, v0 = block(2 * i)
        k1, v1 = block(2 * i + 1)
        ca = update(jnp.where(mask, scores(k0, qa), -1e30), v0, *start)
        cb = update(scores(k0, qb), v0, *start)
        cb = update(jnp.where(mask, scores(k1, qb), -1e30), v1, *cb)

        def body(j, carry):
            kj, vj = block(j)
            return (update(scores(kj, qa), vj, *carry[0]), update(scores(kj, qb), vj, *carry[1]))

        ca, cb = lax.fori_loop(0, 2 * i, body, (ca, cb))
        for half, (m, l, acc) in enumerate((ca, cb)):
            o_ref[half * bq:(half + 1) * bq, :] = _bf((acc / l).T)
            lse_ref[half] = jnp.broadcast_to(m + jnp.log(l), (8, bq))

    return pl.pallas_call(
        kern, name="flash_fwd", grid=(HEADS, nq // 2),
        in_specs=[pl.BlockSpec((None, 2 * bq, HEAD_PAD), lambda h, i: (h, i, 0)),
                  pl.BlockSpec((None, t, HEAD_PAD), lambda h, i: (h, 0, 0)),
                  pl.BlockSpec((None, t, V_HEAD), lambda h, i: (h, 0, 0))],
        out_specs=[pl.BlockSpec((2 * bq, V_HEAD), lambda h, i: (i, h)),
                   pl.BlockSpec((None, 2, 8, bq), lambda h, i: (h, i, 0, 0))],
        out_shape=[jax.ShapeDtypeStruct((t, HEADS * V_HEAD), BF16), jax.ShapeDtypeStruct((HEADS, nq, 8, bq), F32)],
        compiler_params=_params("parallel", "arbitrary"))(q, k, v)


def _attn_delta(do, o):
    t = do.shape[0]
    bq = _attn_block(t)

    def kern(do_ref, o_ref, d_ref):
        prod = do_ref[...].astype(F32) * o_ref[...].astype(F32)
        d_ref[...] = jnp.broadcast_to(jnp.sum(prod.T, axis=0, keepdims=True), (8, bq))

    blk = pl.BlockSpec((bq, V_HEAD), lambda h, i: (i, h))
    return pl.pallas_call(
        kern, name="attn_delta", grid=(HEADS, t // bq), in_specs=[blk, blk],
        out_specs=pl.BlockSpec((None, None, 8, bq), lambda h, i: (h, i, 0, 0)),
        out_shape=jax.ShapeDtypeStruct((HEADS, t // bq, 8, bq), F32),
        compiler_params=_params("parallel", "parallel"))(do, o)


def _flash_bwd(q, k, v, do, lse, delta):
    t = q.shape[1]
    bq = _attn_block(t)
    nq = t // bq

    def kern(q_ref, k_ref, v_ref, do_ref, lse_ref, del_ref, dq_ref, dk_ref, dv_ref):
        j = pl.program_id(1)

        @pl.when(j == 0)
        def _():
            dq_ref[...] = jnp.zeros_like(dq_ref)

        dk_ref[...] = jnp.zeros_like(dk_ref)
        dv_ref[...] = jnp.zeros_like(dv_ref)
        kj, vj = k_ref[...], v_ref[...]

        def step(i, masked):
            rows = pl.ds(pl.multiple_of(i * bq, bq), bq)
            qi, doi = q_ref[rows, :], do_ref[rows, :]
            st = lax.dot_general(kj, qi, (((1,), (1,)), ((), ())), preferred_element_type=F32)
            pt = jnp.exp(st - lse_ref[i][0:1, :])
            if masked:
                pt = jnp.where(_chunk_mask(bq, bq), pt, 0.0)
            dpt = lax.dot_general(vj, doi, (((1,), (1,)), ((), ())), preferred_element_type=F32)
            dst = _bf(pt * (dpt - del_ref[i][0:1, :]))
            dv_ref[...] += jnp.dot(_bf(pt), doi, preferred_element_type=F32)
            dk_ref[...] += jnp.dot(dst, qi, preferred_element_type=F32)
            dq_ref[rows, :] += lax.dot_general(dst, kj, (((0,), (0,)), ((), ())), preferred_element_type=F32)

        step(j, True)

        def body(i, carry):
            step(i, False)
            return carry

        lax.fori_loop(j + 1, nq, body, 0)

    stat = pl.BlockSpec((None, nq, 8, bq), lambda h, j: (h, 0, 0, 0))
    return pl.pallas_call(
        kern, name="flash_bwd", grid=(HEADS, nq),
        in_specs=[pl.BlockSpec((None, t, HEAD_PAD), lambda h, j: (h, 0, 0)),
                  pl.BlockSpec((None, bq, HEAD_PAD), lambda h, j: (h, j, 0)),
                  pl.BlockSpec((None, bq, V_HEAD), lambda h, j: (h, j, 0)),
                  pl.BlockSpec((t, V_HEAD), lambda h, j: (0, h)), stat, stat],
        out_specs=[pl.BlockSpec((None, t, HEAD_PAD), lambda h, j: (h, 0, 0)),
                   pl.BlockSpec((None, bq, HEAD_PAD), lambda h, j: (h, j, 0)),
                   pl.BlockSpec((None, bq, V_HEAD), lambda h, j: (h, j, 0))],
        out_shape=[jax.ShapeDtypeStruct((HEADS, t, HEAD_PAD), F32), jax.ShapeDtypeStruct((HEADS, t, HEAD_PAD), F32),
                   jax.ShapeDtypeStruct((HEADS, t, V_HEAD), F32)],
        compiler_params=_params("parallel", "arbitrary"))(q, k, v, do, lse, delta)


def _place():
    x, y, c = lax.axis_index("x"), lax.axis_index("y"), lax.axis_index("c")
    return x, y, c, [(1 - x, y), (x, 1 - y), (1 - x, 1 - y)]


def _all_gather_rows(block, name):
    m_per, n = block.shape

    def body(x_ref, out_ref, send_sems, recv_sems, local_sem):
        x, y, c, chips = _place()
        me, sibling = (x, y, c), (x, y, 1 - c)

        def rows(px, py, pc):
            return out_ref.at[pl.ds((4 * px + 2 * py + pc) * m_per, m_per), :]

        def copy(k, blk, to, src=None):
            return pltpu.make_async_remote_copy(
                src_ref=rows(*blk) if src is None else src, dst_ref=rows(*blk), send_sem=send_sems.at[k],
                recv_sem=recv_sems.at[k], device_id=to, device_id_type=MESH)

        mine = pltpu.make_async_copy(x_ref, rows(*me), local_sem)
        mine.start()
        first = [copy(0, me, sibling, src=x_ref)]
        first += [copy(1 + j, me, (*chip, c), src=x_ref) for j, chip in enumerate(chips)]
        for cp in first:
            cp.start()
        passed = [copy(4 + j, (*chip, c), sibling) for j, chip in enumerate(chips)]
        for j, chip in enumerate(chips):
            copy(1 + j, (*chip, c), me).wait_recv()
            passed[j].start()
        copy(0, sibling, me).wait_recv()
        for j, chip in enumerate(chips):
            copy(4 + j, (*chip, 1 - c), me).wait_recv()
        for cp in first + passed:
            cp.wait_send()
        mine.wait()

    return pl.pallas_call(
        body, name=name, out_shape=jax.ShapeDtypeStruct((8 * m_per, n), block.dtype),
        in_specs=[pl.BlockSpec(memory_space=pltpu.VMEM)], out_specs=pl.BlockSpec(memory_space=pltpu.VMEM),
        scratch_shapes=[pltpu.SemaphoreType.DMA((7,)), pltpu.SemaphoreType.DMA((7,)), pltpu.SemaphoreType.DMA],
        compiler_params=pltpu.CompilerParams(vmem_limit_bytes=VMEM_LIMIT_BYTES))(block)


HBM_SPEC = pl.BlockSpec(memory_space=pltpu.HBM)
SEM_SPEC = pl.BlockSpec(memory_space=pltpu.SEMAPHORE)
DATAFLOW = pltpu.SideEffectType.DATAFLOW_SIDE_EFFECTING


def _in_hbm(a):
    return pltpu.with_memory_space_constraint(a, pltpu.HBM)


def _chip_copies(ins, lands, send_sems, recv_sems, src_slot):
    n = len(ins)
    x, y, c, chips = _place()
    me = 2 * x + y
    return [pltpu.make_async_remote_copy(
        src_ref=ins[w].at[2 * chip[0] + chip[1]] if src_slot else ins[w], dst_ref=lands[w].at[me],
        send_sem=send_sems.at[p * n + w], recv_sem=recv_sems.at[p * n + w], device_id=(*chip, c),
        device_id_type=MESH) for w in range(n) for p, chip in enumerate(chips)]


def _exchange_start(srcs, lands, src_slot, name, dep=None):
    n = len(srcs)
    first_out = 2 * n + (dep is not None)

    def body(*refs):
        for cp in _chip_copies(refs[:n], refs[n:2 * n], refs[first_out], refs[first_out + 1], src_slot):
            cp.start()
        token = refs[-1]
        token[...] = jnp.zeros_like(token)

    thru = [pltpu.HBM(a.shape, a.dtype) for a in list(srcs) + list(lands)]
    res = pl.pallas_call(
        body, name=name,
        out_shape=(pltpu.SemaphoreType.DMA((3 * n,)), pltpu.SemaphoreType.DMA((3 * n,)), *thru,
                   jax.ShapeDtypeStruct((8, 128), F32)),
        in_specs=[HBM_SPEC] * (2 * n) + ([pl.BlockSpec(memory_space=pl.ANY)] if dep is not None else []),
        out_specs=(SEM_SPEC, SEM_SPEC, *[HBM_SPEC] * (2 * n), pl.BlockSpec(memory_space=pltpu.VMEM)),
        input_output_aliases={i: 2 + i for i in range(2 * n)},
        compiler_params=pltpu.CompilerParams(has_side_effects=DATAFLOW))(
            *[_in_hbm(a) for a in srcs], *[_in_hbm(a) for a in lands], *([dep] if dep is not None else []))
    return (res[0], res[1], list(res[2:2 + n]), list(res[2 + n:2 + 2 * n])), res[-1]


def _exchange_wait(flight, after, src_slot, name):
    send_sems, recv_sems, srcs, lands = flight
    n = len(srcs)

    def body(*refs):
        for cp in _chip_copies(refs[:n], refs[n:2 * n], refs[2 * n], refs[2 * n + 1], src_slot):
            cp.wait_send()
            cp.wait_recv()

    thru = [pltpu.HBM(a.shape, a.dtype) for a in list(srcs) + list(lands)]
    res = pl.pallas_call(
        body, name=name, out_shape=thru,
        in_specs=[HBM_SPEC] * (2 * n) + [SEM_SPEC, SEM_SPEC, pl.BlockSpec(memory_space=pl.ANY)],
        out_specs=[HBM_SPEC] * (2 * n), input_output_aliases={i: i for i in range(2 * n)},
        compiler_params=pltpu.CompilerParams(has_side_effects=DATAFLOW))(*srcs, *lands, send_sems, recv_sems, after)
    return list(res[n:])


def _landing(own, me):
    return lax.dynamic_update_index_in_dim(lax.empty((4, *own.shape), own.dtype), own, me, 0)


def _swap_with_sibling(arrays, name):
    n = len(arrays)

    def body(*refs):
        ins, outs = refs[:n], refs[n:2 * n]
        send_sems, recv_sems = refs[2 * n:]
        x, y, c, _ = _place()
        copies = [pltpu.make_async_remote_copy(src_ref=ins[w], dst_ref=outs[w], send_sem=send_sems.at[w],
                                               recv_sem=recv_sems.at[w], device_id=(x, y, 1 - c), device_id_type=MESH)
                  for w in range(n)]
        for cp in copies:
            cp.start()
        for cp in copies:
            cp.wait()

    any_spec = pl.BlockSpec(memory_space=pl.ANY)
    return pl.pallas_call(
        body, name=name, out_shape=[jax.ShapeDtypeStruct(a.shape, a.dtype) for a in arrays],
        in_specs=[any_spec] * n, out_specs=[any_spec] * n,
        scratch_shapes=[pltpu.SemaphoreType.DMA((n,)), pltpu.SemaphoreType.DMA((n,))])(*arrays)


def _as_rows(a):
    return a.reshape(-1, a.shape[-1])


def _row_tile(r, c, budget_bytes=1 << 20):
    tr = r
    while tr % 16 == 0 and tr * c * 4 > budget_bytes:
        tr //= 2
    return tr


def _sum_slots(layers, nlayer, name, into=None):
    _, r, c = layers[0][1].shape
    tr = _row_tile(r, c)
    nt = r // tr
    acc = into
    for l, r4 in layers:
        def kern(r_ref, *rest):
            o_ref = rest[-1]
            o_ref[...] = (((r_ref[0].astype(F32) + r_ref[1].astype(F32)) + r_ref[2].astype(F32))
                          + r_ref[3].astype(F32))

        out_spec = pl.BlockSpec((tr, c), lambda i, l=l: (l * nt + i, 0))
        first = acc is None
        acc = pl.pallas_call(
            kern, name=f"{name}_l{l}", grid=(nt,),
            in_specs=[pl.BlockSpec((4, tr, c), lambda i: (0, i, 0))]
            + ([] if first else [pl.BlockSpec(memory_space=pl.ANY)]),
            out_specs=out_spec, out_shape=jax.ShapeDtypeStruct((nlayer * r, c), F32),
            input_output_aliases={} if first else {1: 0},
            compiler_params=_params("parallel"))(*([r4] if first else [r4, acc]))
    return acc


def _adamw(w, m, v, parts, name):
    r, c = w.shape
    tr = _row_tile(r, c, 1 << 19)
    npart = len(parts)
    c1 = 1.0 - ADAM_B1 ** ADAM_STEP
    c2 = 1.0 - ADAM_B2 ** ADAM_STEP

    def kern(*refs):
        w_ref, m_ref, v_ref = refs[:3]
        p_refs = refs[3:3 + npart]
        g_ref, d_ref, mo_ref, vo_ref = refs[3 + npart:]
        g = p_refs[0][...]
        for p in p_refs[1:]:
            g = g + p[...]
        mn = ADAM_B1 * m_ref[...] + (1.0 - ADAM_B1) * g
        vn = ADAM_B2 * v_ref[...] + (1.0 - ADAM_B2) * (g * g)
        g_ref[...] = g
        mo_ref[...] = mn
        vo_ref[...] = vn
        d_ref[...] = -ADAM_LR * ((mn / c1) / (jnp.sqrt(vn / c2) + ADAM_EPS) + ADAM_WD * w_ref[...])

    blk = pl.BlockSpec((tr, c), lambda i: (i, 0))
    shape = jax.ShapeDtypeStruct((r, c), F32)
    return pl.pallas_call(
        kern, name=name, grid=(r // tr,), in_specs=[blk] * (3 + npart), out_specs=[blk] * 4, out_shape=[shape] * 4,
        compiler_params=_params("parallel"))(w, m, v, *parts)


def _sum_devices(g8, name):
    _, r, c = g8.shape

    def kern(g_ref, o_ref):
        tot = g_ref[0]
        for dev in range(1, 8):
            tot = tot + g_ref[dev]
        o_ref[...] = tot

    return pl.pallas_call(
        kern, name=name, grid=(1,), in_specs=[pl.BlockSpec((8, r, c), lambda i: (0, 0, 0))],
        out_specs=pl.BlockSpec((r, c), lambda i: (0, 0)), out_shape=jax.ShapeDtypeStruct((r, c), F32),
        compiler_params=_params("arbitrary"))(g8)


def _pad_lanes(a, width):
    return jnp.pad(a, [(0, 0)] * (a.ndim - 1) + [(0, width - a.shape[-1])])


def kernel(x, positions, ffn_norm1, ffn1_w1, ffn1_w3, ffn1_w2, mix_norm, ffn_norm2, ffn2_w1, ffn2_w3, ffn2_w2, conv_w_pw1, conv_w_dw, conv_norm, conv_w_pw2, mla_w_a, mla_q_norm, mla_kv_norm, mla_w_uq, mla_w_ukv, mla_w_o, final_norm, loss_target, m_ffn_norm1, m_ffn1_w1, m_ffn1_w3, m_ffn1_w2, m_mix_norm, m_ffn_norm2, m_ffn2_w1, m_ffn2_w3, m_ffn2_w2, m_conv_w_pw1, m_conv_w_dw, m_conv_norm, m_conv_w_pw2, m_mla_w_a, m_mla_q_norm, m_mla_kv_norm, m_mla_w_uq, m_mla_w_ukv, m_mla_w_o, m_final_norm, v_ffn_norm1, v_ffn1_w1, v_ffn1_w3, v_ffn1_w2, v_mix_norm, v_ffn_norm2, v_ffn2_w1, v_ffn2_w3, v_ffn2_w2, v_conv_w_pw1, v_conv_w_dw, v_conv_norm, v_conv_w_pw2, v_mla_w_a, v_mla_q_norm, v_mla_kv_norm, v_mla_w_uq, v_mla_w_ukv, v_mla_w_o, v_final_norm):
    given = locals()
    return _step({nm: given[nm] for nm in INPUTS})


def _step(A):
    x = A['x'][0]
    target = A['loss_target'][0]
    t, d = x.shape
    pos = A['positions'].reshape(t, 1)
    me = 2 * lax.axis_index("x") + lax.axis_index("y")

    ffn = [f'ffn{k}_{w}' for k in (1, 2) for w in ('w1', 'w3', 'w2')]
    gather_groups = [[(nm, 0) for nm in ffn[:3]],
                     [('conv_w_pw1', 0), ('conv_w_pw2', 0)] + [(nm, 0) for nm in ffn[3:]],
                     [(nm, 1) for nm in ffn[:3]] + [('mla_w_a', 0), ('mla_w_uq', 0), ('mla_w_ukv', 0), ('mla_w_o', 0)],
                     [(nm, 1) for nm in ffn[3:]]]
    gather_flights = {}
    big = {}

    def gather_start(gi, dep):
        shards = [_bf(A[nm][l]) for nm, l in gather_groups[gi]]
        gather_flights[gi], token = _exchange_start(shards, [_landing(s, me) for s in shards], False,
                                                    f"gather_start_{gi}", dep)
        return token

    def gather_wait(gi, after):
        landed = _exchange_wait(gather_flights[gi], after, False, f"gather_wait_{gi}")
        big.update(zip(gather_groups[gi], landed))
        return landed[0]

    dw_shard = A['conv_w_dw'][0]
    cw = dw_shard.shape[1]
    small = jnp.concatenate([
        jnp.pad(dw_shard, ((0, CONV_HALO - CONV_WIDTH), (0, 0))),
        jnp.pad(_pad_lanes(A['mla_q_norm'], cw), ((0, 7), (0, 0))),
        jnp.pad(_pad_lanes(A['mla_kv_norm'], cw), ((0, 7), (0, 0)))], axis=0)
    small = _all_gather_rows(small, "gather_small_weights").reshape(4, 2, 48, cw)[:, 0]
    w_dw = jnp.concatenate([small[j, :CONV_HALO] for j in range(4)], axis=1)
    gq = jnp.concatenate([small[j, CONV_HALO, :Q_LORA // 4] for j in range(4)])
    gkv = jnp.concatenate([small[j, CONV_HALO + 8, :KV_LORA // 4] for j in range(4)])

    def cols(nm, layer):
        return jnp.concatenate([big[nm, layer][j] for j in range(4)], axis=1)

    def rows(nm, layer):
        g = big[nm, layer]
        return g.reshape(-1, g.shape[-1])

    ffn_w = {}

    def ffn_weights(k, l):
        ffn_w[k, l] = (cols(f'ffn{k}_w1', l), cols(f'ffn{k}_w3', l), rows(f'ffn{k}_w2', l))
        return ffn_w[k, l]

    token = gather_start(0, None)
    cs_c, cs_s = _rope_tables(pos)
    h0 = x
    token = gather_start(1, gather_wait(0, token))
    h1, n01, z01a, z01b = _ffn_fwd(h0, A['ffn_norm1'][0], *ffn_weights(1, 0), token, "ffn1_l0_fwd")
    token = gather_start(3, gather_start(2, gather_wait(1, h1)))
    pw1 = big['conv_w_pw1', 0]
    pw1_a = jnp.concatenate([pw1[0], pw1[1]], axis=1)
    pw1_b = jnp.concatenate([pw1[2], pw1[3]], axis=1)
    pw2 = rows('conv_w_pw2', 0)
    m0 = _norm_fwd(h1, A['mix_norm'][0], token, "mix_norm_l0")
    ca, cb, glu = _glu_fwd(m0, pw1_a, pw1_b)
    cv, cs = _conv_fwd(glu, w_dw, A['conv_norm'][0])
    h2 = _mm([(cs, pw2)], F32, "conv_pw2_fwd", res=h1)
    h3, n02, z02a, z02b = _ffn_fwd(h2, A['ffn_norm2'][0], *ffn_weights(2, 0), token, "ffn2_l0_fwd")
    gather_wait(2, h3)
    w_a = _pad_lanes(rows('mla_w_a', 0), A_PAD)
    wuq = _pad_lanes(big['mla_w_uq', 0].reshape(Q_LORA, HEADS, NOPE + ROPE).transpose(1, 0, 2), HEAD_PAD)
    wukv = big['mla_w_ukv', 0].reshape(KV_LORA, HEADS, NOPE + V_HEAD).transpose(1, 0, 2)
    w_o = rows('mla_w_o', 0)
    h4, n11, z11a, z11b = _ffn_fwd(h3, A['ffn_norm1'][1], *ffn_weights(1, 1), token, "ffn1_l1_fwd")
    m1 = _norm_fwd(h4, A['mix_norm'][1], token, "mix_norm_l1")
    a_lat = _mm([(m1, w_a)], F32, "mla_down_fwd")
    cq, ckv, kr = _mla_prep(a_lat, gq, gkv, cs_c, cs_s)
    q, k, v = _mla_qkv(cq, ckv, kr, cs_c, cs_s, wuq, wukv)
    o, lse = _flash_fwd(q, k, v)
    h5 = _mm([(o, w_o)], F32, "mla_out_fwd", res=h4)
    gather_wait(3, h5)
    h6, n12, z12a, z12b = _ffn_fwd(h5, A['ffn_norm2'][1], *ffn_weights(2, 1), token, "ffn2_l1_fwd")

    def col_slots(g):
        r, c4 = g.shape
        return g.reshape(r, 4, c4 // 4).transpose(1, 0, 2)

    def row_slots(g):
        return g.reshape(4, g.shape[0] // 4, g.shape[1])

    scatter_flights = []

    def scatter_start(named):
        srcs = [g for _, g in named]
        lands = [_landing(lax.dynamic_index_in_dim(g, me, 0, keepdims=False), me) for g in srcs]
        flight, token = _exchange_start(srcs, lands, True, f"scatter_start_{len(scatter_flights)}")
        scatter_flights.append(([key for key, _ in named], flight))
        return token

    def send_ffn(k, l, dw1, dw3, dw2):
        return scatter_start([((f'ffn{k}_w1', l), col_slots(dw1)), ((f'ffn{k}_w3', l), col_slots(dw3)),
                              ((f'ffn{k}_w2', l), row_slots(dw2))])

    dh6, dg_final, loss_part = _loss_bwd(h6, target, A['final_norm'])
    dh5, dg_n2_l1, *dws = _ffn_bwd(dh6, h5, A['ffn_norm2'][1], n12, z12a, z12b, *ffn_w[2, 1], loss_part, "ffn2_l1")
    token = send_ffn(2, 1, *dws)

    do = _mm([(dh5, w_o)], BF16, "mla_out_bwd", trans_b=True, dep=token)
    dw_o = _mm_tn(o, dh5, BF16, "mla_dw_o")
    delta = _attn_delta(do, o)
    dq, dk, dv = _flash_bwd(q, k, v, do, lse, delta)
    dr, dkv, dcq, dckv, dar = _mla_qkv_bwd(dq, dk, dv, cs_c, cs_s, wuq, wukv)
    dwuq = _mm_tn(cq, dr, BF16, "mla_dw_uq")
    dwukv = _mm_tn(ckv, dkv, BF16, "mla_dw_ukv")
    da_lat, dgq, dgkv = _mla_prep_bwd(a_lat, dcq, dckv, dar, gq, gkv)
    dw_a = _mm_tn(m1, da_lat, BF16, "mla_dw_a")
    token = scatter_start([
        (('mla_w_a', 0), row_slots(dw_a[:, :Q_LORA + KV_LORA + ROPE])),
        (('mla_w_uq', 0), dwuq[:, :, :NOPE + ROPE].transpose(1, 0, 2).reshape(4, Q_LORA // 4, HEADS, NOPE + ROPE)),
        (('mla_w_ukv', 0), dwukv.transpose(1, 0, 2).reshape(4, KV_LORA // 4, HEADS, NOPE + V_HEAD)),
        (('mla_w_o', 0), row_slots(dw_o))])
    dh4, dg_mix_l1 = _mm_normbwd([(da_lat, w_a)], h4, A['mix_norm'][1], dh5, token, "mla_down_bwd")

    dh3, dg_n1_l1, *dws = _ffn_bwd(dh4, h3, A['ffn_norm1'][1], n11, z11a, z11b, *ffn_w[1, 1], token, "ffn1_l1")
    token = send_ffn(1, 1, *dws)
    dh2, dg_n2_l0, *dws = _ffn_bwd(dh3, h2, A['ffn_norm2'][0], n02, z02a, z02b, *ffn_w[2, 0], token, "ffn2_l0")
    token = send_ffn(2, 0, *dws)

    dcv, dg_conv = _conv_bwd_norm(dh2, cv, pw2, A['conv_norm'][0], token)
    dw_pw2 = _mm_tn(cs, dh2, BF16, "conv_dw_pw2")
    dca, dcb, ddw = _conv_bwd_dw(dcv, glu, ca, cb, w_dw)
    dpw1_a = _mm_tn(m0, dca, BF16, "conv_dw_pw1a")
    dpw1_b = _mm_tn(m0, dcb, BF16, "conv_dw_pw1b")
    half = dpw1_a.shape[1] // 2
    token = scatter_start([
        (('conv_w_pw1', 0), jnp.stack([dpw1_a[:, :half], dpw1_a[:, half:], dpw1_b[:, :half], dpw1_b[:, half:]])),
        (('conv_w_pw2', 0), row_slots(dw_pw2))])
    dh1, dg_mix_l0 = _mm_normbwd([(dca, pw1_a), (dcb, pw1_b)], h1, A['mix_norm'][0], dh2, token, "conv_pw1_bwd")

    dx, dg_n1_l0, *dws = _ffn_bwd(dh1, h0, A['ffn_norm1'][0], n01, z01a, z01b, *ffn_w[1, 0], token, "ffn1_l0")
    last_sent = send_ffn(1, 0, *dws)
    out = {}

    qkv_row = jnp.concatenate([dgq, dgkv, jnp.zeros((8, d - Q_LORA - KV_LORA), F32)], axis=1)
    loss_row = _pad_lanes(loss_part, d)
    small_g = jnp.concatenate([dg_n1_l0, dg_n1_l1, dg_mix_l0, dg_mix_l1, dg_n2_l0, dg_n2_l1, dg_conv, dg_final,
                               qkv_row, loss_row, ddw], axis=0)
    nrow = small_g.shape[0]
    tot = _sum_devices(_all_gather_rows(small_g, "gather_small_grads").reshape(8, nrow, d), "sum_small_grads")
    loss = tot[72, 0]
    q_shard = lax.dynamic_slice_in_dim(tot[64, :Q_LORA], me * (Q_LORA // 4), Q_LORA // 4)
    kv_shard = lax.dynamic_slice_in_dim(tot[64, Q_LORA:Q_LORA + KV_LORA], me * (KV_LORA // 4), KV_LORA // 4)
    dw_shard_g = lax.dynamic_slice_in_dim(tot[80:80 + CONV_WIDTH], me * cw, cw, axis=1)
    small_grads = {
        'ffn_norm1': jnp.stack([tot[0], tot[8]]), 'mix_norm': jnp.stack([tot[16], tot[24]]),
        'ffn_norm2': jnp.stack([tot[32], tot[40]]), 'conv_norm': tot[48][None], 'final_norm': tot[56],
        'mla_q_norm': q_shard[None], 'mla_kv_norm': kv_shard[None], 'conv_w_dw': dw_shard_g[None],
    }
    for nm, g in small_grads.items():
        res = _adamw(_as_rows(A[nm]) if A[nm].ndim > 1 else A[nm].reshape(1, -1),
                     A['m_' + nm].reshape(-1, A[nm].shape[-1]), A['v_' + nm].reshape(-1, A[nm].shape[-1]),
                     [g.reshape(-1, A[nm].shape[-1])], "adamw_" + nm)
        out[nm] = [r.reshape(A[nm].shape) for r in res]

    received = {}
    after = last_sent

    def scatter_wait(si, after):
        keys, flight = scatter_flights[si]
        landed = _exchange_wait(flight, after, True, f"scatter_wait_{si}")
        received.update(zip(keys, landed))
        return landed[0]

    def slots(nm, l):
        return received[nm, l].reshape(4, -1, received[nm, l].shape[-1])

    def finish(names, sums, tag):
        for nm, mine, theirs in zip(names, sums, _swap_with_sibling(sums, "swap_with_sibling_" + tag)):
            res = _adamw(_as_rows(A[nm]), _as_rows(A['m_' + nm]), _as_rows(A['v_' + nm]), [mine, theirs],
                         "adamw_" + nm)
            out[nm] = [r.reshape(A[nm].shape) for r in res]
        return res[1]

    last = len(scatter_flights) - 1
    for si in range(last):
        after = scatter_wait(si, after)
    late = ffn[:3]
    early = [nm for nm in BIG if nm not in late]
    late_l1 = [_sum_slots([(1, slots(nm, 1))], 2, "sum_" + nm) for nm in late]
    after = finish(early, [_sum_slots([(l, slots(nm, l)) for l in range(A[nm].shape[0])], A[nm].shape[0],
                                      "sum_" + nm) for nm in early], "early")
    scatter_wait(last, after)
    finish(late, [_sum_slots([(0, slots(nm, 0))], 2, "sum_" + nm, into=part) for nm, part in zip(late, late_l1)],
           "late")

    return (loss, dx[None], *[out[nm][0] for nm in WEIGHTS], *[out[nm][1] for nm in WEIGHTS],
            *[out[nm][2] for nm in WEIGHTS], *[out[nm][3] for nm in WEIGHTS])
```

```python
import functools

import jax
import jax.numpy as jnp
import numpy as np
from jax import lax
from jax.experimental import pallas as pl
from jax.experimental.pallas import tpu as pltpu

F32 = jnp.float32
BF16 = jnp.bfloat16
MESH = pl.DeviceIdType.MESH

RMS_EPS = 1e-6
HEADS = 8
NOPE = 128
ROPE = 64
HEAD_PAD = 256
V_HEAD = 128
Q_LORA = 512
KV_LORA = 256
A_PAD = 896
CHUNK = 64
CONV_WIDTH = 31
CONV_HALO = 32
CONV_ROWS = 16
ROPE_THETA = 10000.0
ATTN_SCALE = (NOPE + ROPE) ** -0.5
FFN_RES = 0.5

ADAM_LR = 0.001
ADAM_B1 = 0.9
ADAM_B2 = 0.999
ADAM_EPS = 1e-08
ADAM_WD = 0.01
ADAM_STEP = 10

VMEM_LIMIT_BYTES = 56 * 1024 * 1024

WEIGHTS = ['ffn_norm1', 'ffn1_w1', 'ffn1_w3', 'ffn1_w2', 'mix_norm', 'ffn_norm2', 'ffn2_w1', 'ffn2_w3', 'ffn2_w2',
           'conv_w_pw1', 'conv_w_dw', 'conv_norm', 'conv_w_pw2', 'mla_w_a', 'mla_q_norm', 'mla_kv_norm', 'mla_w_uq',
           'mla_w_ukv', 'mla_w_o', 'final_norm']
INPUTS = (['x', 'positions'] + WEIGHTS + ['loss_target'] + ['m_' + w for w in WEIGHTS] + ['v_' + w for w in WEIGHTS])
BIG = ['ffn1_w1', 'ffn1_w3', 'ffn1_w2', 'ffn2_w1', 'ffn2_w3', 'ffn2_w2', 'conv_w_pw1', 'conv_w_pw2', 'mla_w_a',
       'mla_w_uq', 'mla_w_ukv', 'mla_w_o']


def _params(*sem):
    return pltpu.CompilerParams(dimension_semantics=sem, vmem_limit_bytes=VMEM_LIMIT_BYTES)


def _bf(v):
    return v.astype(BF16)


def _rstd(x):
    return lax.rsqrt(jnp.mean(x * x, axis=-1, keepdims=True) + RMS_EPS)


def _sigmoid(x):
    return jax.nn.sigmoid(x)


def _rot(x):
    lane = lax.broadcasted_iota(jnp.int32, x.shape, 1)
    return jnp.where(lane < ROPE // 2, -pltpu.roll(x, 128 - ROPE // 2, 1), pltpu.roll(x, ROPE // 2, 1))


def _rot_t(y):
    lane = lax.broadcasted_iota(jnp.int32, y.shape, 1)
    return jnp.where(lane < ROPE // 2, pltpu.roll(y, 128 - ROPE // 2, 1), -pltpu.roll(y, ROPE // 2, 1))


def _pair_sum(a_refs, b_refs, trans_b):
    tot = None
    for a_r, b_r in zip(a_refs, b_refs):
        a, b = _bf(a_r[...]), _bf(b_r[...])
        if trans_b:
            d = lax.dot_general(a, b, (((1,), (1,)), ((), ())), preferred_element_type=F32)
        else:
            d = jnp.dot(a, b, preferred_element_type=F32)
        tot = d if tot is None else tot + d
    return tot


def _mm(pairs, out_dtype, name, *, trans_b=False, tm=512, tn=None, tk=None, res=None, dep=None):
    m, k = pairs[0][0].shape
    n = pairs[0][1].shape[0] if trans_b else pairs[0][1].shape[1]
    tm, tn, tk = min(tm, m), tn or n, tk or k
    nk, npair = k // tk, len(pairs)

    def kern(*refs):
        a_refs, b_refs = refs[:npair], refs[npair:2 * npair]
        rest = list(refs[2 * npair:])
        res_ref = rest.pop(0) if res is not None else None
        if dep is not None:
            rest.pop(0)
        o_ref = rest.pop(0)

        def finish(acc):
            if res_ref is not None:
                acc = res_ref[...] + acc
            o_ref[...] = acc.astype(o_ref.dtype)

        if nk == 1:
            finish(_pair_sum(a_refs, b_refs, trans_b))
        else:
            acc_ref = rest.pop(0)
            kk = pl.program_id(2)

            @pl.when(kk == 0)
            def _():
                acc_ref[...] = jnp.zeros_like(acc_ref)

            acc_ref[...] += _pair_sum(a_refs, b_refs, trans_b)

            @pl.when(kk == nk - 1)
            def _():
                finish(acc_ref[...])

    a_spec = pl.BlockSpec((tm, tk), lambda i, j, kk: (i, kk))
    b_spec = (pl.BlockSpec((tn, tk), lambda i, j, kk: (j, kk)) if trans_b
              else pl.BlockSpec((tk, tn), lambda i, j, kk: (kk, j)))
    io_spec = pl.BlockSpec((tm, tn), lambda i, j, kk: (i, j))
    in_specs = ([a_spec] * npair + [b_spec] * npair + ([io_spec] if res is not None else [])
                + ([pl.BlockSpec((8, 128), lambda i, j, kk: (0, 0))] if dep is not None else []))
    args = ([p[0] for p in pairs] + [p[1] for p in pairs] + ([res] if res is not None else [])
            + ([dep] if dep is not None else []))
    return pl.pallas_call(
        kern, name=name, grid=(m // tm, n // tn, nk), in_specs=in_specs, out_specs=io_spec,
        out_shape=jax.ShapeDtypeStruct((m, n), out_dtype),
        scratch_shapes=[pltpu.VMEM((tm, tn), F32)] if nk > 1 else [],
        compiler_params=_params("parallel", "parallel", "arbitrary"))(*args)


def _mm_normbwd(pairs, h, g, dres, dep, name, *, tm=512, tk=None):
    m, k = pairs[0][0].shape
    d = pairs[0][1].shape[0]
    tm, tk = min(tm, m), tk or k
    nk, npair = k // tk, len(pairs)

    def kern(*refs):
        a_refs, b_refs = refs[:npair], refs[npair:2 * npair]
        h_ref, g_ref, dres_ref, _, o_ref, dg_ref, acc_ref = refs[2 * npair:]
        i, kk = pl.program_id(0), pl.program_id(1)

        @pl.when(jnp.logical_and(i == 0, kk == 0))
        def _():
            dg_ref[...] = jnp.zeros_like(dg_ref)

        @pl.when(kk == 0)
        def _():
            acc_ref[...] = jnp.zeros_like(acc_ref)

        acc_ref[...] += _pair_sum(a_refs, b_refs, True)

        @pl.when(kk == nk - 1)
        def _():
            dn = acc_ref[...]
            x = h_ref[...]
            rstd = _rstd(x)
            xhat = x * rstd
            dg_ref[...] += jnp.broadcast_to(jnp.sum(dn * xhat, axis=0, keepdims=True), dg_ref.shape)
            dxh = dn * g_ref[...]
            dx = rstd * (dxh - xhat * jnp.mean(dxh * xhat, axis=-1, keepdims=True))
            o_ref[...] = dres_ref[...] + dx

    row = pl.BlockSpec((tm, d), lambda i, kk: (i, 0))
    in_specs = ([pl.BlockSpec((tm, tk), lambda i, kk: (i, kk))] * npair
                + [pl.BlockSpec((d, tk), lambda i, kk: (0, kk))] * npair
                + [row, pl.BlockSpec((1, d), lambda i, kk: (0, 0)), row, pl.BlockSpec((8, 128), lambda i, kk: (0, 0))])
    return pl.pallas_call(
        kern, name=name, grid=(m // tm, nk), in_specs=in_specs,
        out_specs=[row, pl.BlockSpec((8, d), lambda i, kk: (0, 0))],
        out_shape=[jax.ShapeDtypeStruct((m, d), F32), jax.ShapeDtypeStruct((8, d), F32)],
        scratch_shapes=[pltpu.VMEM((tm, d), F32)],
        compiler_params=_params("arbitrary", "arbitrary"))(
            *[p[0] for p in pairs], *[p[1] for p in pairs], h, g.reshape(1, d), dres, dep)


def _mm_tn(a, b, out_dtype, name, *, bm=None, bn=None, tk=512):
    t, m = a.shape
    batched = b.ndim == 3
    n = b.shape[-1]
    nb = b.shape[0] if batched else 1
    bm, bn, tk = bm or m, bn or n, min(tk, t)
    nk = t // tk

    def kern(a_ref, b_ref, o_ref, acc_ref):
        kk = pl.program_id(3)

        @pl.when(kk == 0)
        def _():
            acc_ref[...] = jnp.zeros_like(acc_ref)

        acc_ref[...] += lax.dot_general(_bf(a_ref[...]), _bf(b_ref[...]), (((0,), (0,)), ((), ())),
                                        preferred_element_type=F32)

        @pl.when(kk == nk - 1)
        def _():
            o_ref[...] = acc_ref[...].astype(o_ref.dtype)

    a_spec = pl.BlockSpec((tk, bm), lambda h, i, j, kk: (kk, i))
    if batched:
        b_spec = pl.BlockSpec((None, tk, bn), lambda h, i, j, kk: (h, kk, j))
        o_spec = pl.BlockSpec((None, bm, bn), lambda h, i, j, kk: (h, i, j))
        out_shape = jax.ShapeDtypeStruct((nb, m, n), out_dtype)
    else:
        b_spec = pl.BlockSpec((tk, bn), lambda h, i, j, kk: (kk, j))
        o_spec = pl.BlockSpec((bm, bn), lambda h, i, j, kk: (i, j))
        out_shape = jax.ShapeDtypeStruct((m, n), out_dtype)
    return pl.pallas_call(
        kern, name=name, grid=(nb, m // bm, n // bn, nk), in_specs=[a_spec, b_spec], out_specs=o_spec,
        out_shape=out_shape, scratch_shapes=[pltpu.VMEM((bm, bn), F32)],
        compiler_params=_params("parallel", "parallel", "parallel", "arbitrary"))(a, b)


def _ffn_tile(f):
    return f // 2 if (f // 2) % 128 == 0 else f


def _ffn_fwd(h, g, w1, w3, w2, dep, name):
    t, d = h.shape
    f = w1.shape[1]
    tm, tf = min(512, t), _ffn_tile(f)
    nf = f // tf

    def kern(h_ref, g_ref, w1_ref, w3_ref, w2_ref, dep_ref, ho_ref, n_ref, z1_ref, z3_ref, n_sc, acc_ref):
        j = pl.program_id(1)

        @pl.when(j == 0)
        def _():
            x = h_ref[...]
            n = _bf(x * _rstd(x) * g_ref[...])
            n_sc[...] = n
            n_ref[...] = n
            acc_ref[...] = jnp.zeros_like(acc_ref)

        n = n_sc[...]
        z1 = jnp.dot(n, w1_ref[...], preferred_element_type=F32)
        z3 = jnp.dot(n, w3_ref[...], preferred_element_type=F32)
        z1_ref[...] = _bf(z1)
        z3_ref[...] = _bf(z3)
        act = _bf(z1 * _sigmoid(z1) * z3)
        acc_ref[...] += jnp.dot(act, w2_ref[...], preferred_element_type=F32)

        @pl.when(j == nf - 1)
        def _():
            ho_ref[...] = h_ref[...] + FFN_RES * acc_ref[...]

    row = pl.BlockSpec((tm, d), lambda i, j: (i, 0))
    col = pl.BlockSpec((tm, tf), lambda i, j: (i, j))
    return pl.pallas_call(
        kern, name=name, grid=(t // tm, nf),
        in_specs=[row, pl.BlockSpec((1, d), lambda i, j: (0, 0)), pl.BlockSpec((d, tf), lambda i, j: (0, j)),
                  pl.BlockSpec((d, tf), lambda i, j: (0, j)), pl.BlockSpec((tf, d), lambda i, j: (j, 0)),
                  pl.BlockSpec((8, 128), lambda i, j: (0, 0))],
        out_specs=[row, row, col, col],
        out_shape=[jax.ShapeDtypeStruct((t, d), F32), jax.ShapeDtypeStruct((t, d), BF16),
                   jax.ShapeDtypeStruct((t, f), BF16), jax.ShapeDtypeStruct((t, f), BF16)],
        scratch_shapes=[pltpu.VMEM((tm, d), BF16), pltpu.VMEM((tm, d), F32)],
        compiler_params=_params("parallel", "arbitrary"))(h, g.reshape(1, d), w1, w3, w2, dep)


def _ffn_bwd_x(dh, h_in, g, z1, z3, w1, w3, w2, dep, name):
    t, d = dh.shape
    f = z1.shape[1]
    tm = min(256, t)

    def kern(dh_ref, h_ref, g_ref, z1_ref, z3_ref, w2_hbm, w1_hbm, w3_hbm, dep_ref,
             o_ref, dg_ref, dz1_ref, dz3_ref, a_ref, df_ref, w2_ref, w1_ref, w3_ref, sems):
        @pl.when(pl.program_id(0) == 0)
        def _():
            copies = [pltpu.make_async_copy(src, dst, sems.at[k]) for k, (src, dst) in
                      enumerate(((w2_hbm, w2_ref), (w1_hbm, w1_ref), (w3_hbm, w3_ref)))]
            for cp in copies:
                cp.start()
            dg_ref[...] = jnp.zeros_like(dg_ref)
            for cp in copies:
                cp.wait()

        df = _bf(FFN_RES * dh_ref[...])
        df_ref[...] = df
        da = lax.dot_general(df, w2_ref[...], (((1,), (1,)), ((), ())), preferred_element_type=F32)
        z1v, z3v = z1_ref[...].astype(F32), z3_ref[...].astype(F32)
        sig = _sigmoid(z1v)
        silu = z1v * sig
        a_ref[...] = _bf(silu * z3v)
        dz1 = _bf(da * z3v * (sig * (1.0 + z1v * (1.0 - sig))))
        dz3 = _bf(da * silu)
        dz1_ref[...] = dz1
        dz3_ref[...] = dz3
        dn = (lax.dot_general(dz1, w1_ref[...], (((1,), (1,)), ((), ())), preferred_element_type=F32)
              + lax.dot_general(dz3, w3_ref[...], (((1,), (1,)), ((), ())), preferred_element_type=F32))
        x = h_ref[...]
        rstd = _rstd(x)
        xhat = x * rstd
        dg_ref[...] += jnp.broadcast_to(jnp.sum(dn * xhat, axis=0, keepdims=True), dg_ref.shape)
        dxh = dn * g_ref[...]
        o_ref[...] = dh_ref[...] + rstd * (dxh - xhat * jnp.mean(dxh * xhat, axis=-1, keepdims=True))

    row = pl.BlockSpec((tm, d), lambda i: (i, 0))
    col = pl.BlockSpec((tm, f), lambda i: (i, 0))
    whole = pl.BlockSpec(memory_space=pl.ANY)
    colshape = jax.ShapeDtypeStruct((t, f), BF16)
    return pl.pallas_call(
        kern, name=name, grid=(t // tm,),
        in_specs=[row, row, pl.BlockSpec((1, d), lambda i: (0, 0)), col, col, whole, whole, whole,
                  pl.BlockSpec((8, 128), lambda i: (0, 0))],
        out_specs=[row, pl.BlockSpec((8, d), lambda i: (0, 0)), col, col, col, row],
        out_shape=[jax.ShapeDtypeStruct((t, d), F32), jax.ShapeDtypeStruct((8, d), F32), colshape, colshape, colshape,
                   jax.ShapeDtypeStruct((t, d), BF16)],
        scratch_shapes=[pltpu.VMEM((f, d), BF16), pltpu.VMEM((d, f), BF16), pltpu.VMEM((d, f), BF16),
                        pltpu.SemaphoreType.DMA((3,))],
        compiler_params=_params("arbitrary"))(dh, h_in, g.reshape(1, d), z1, z3, w2, w1, w3, dep)


def _ffn_bwd(dh, h_in, g, n, z1, z3, w1, w3, w2, dep, tag):
    f = w1.shape[1]
    dh_in, dg, dz1, dz3, act, df = _ffn_bwd_x(dh, h_in, g, z1, z3, w1, w3, w2, dep, tag + "_bwd_x")
    dw1 = _mm_tn(n, dz1, BF16, tag + "_dw1", bn=_ffn_tile(f))
    dw3 = _mm_tn(n, dz3, BF16, tag + "_dw3", bn=_ffn_tile(f))
    dw2 = _mm_tn(act, df, BF16, tag + "_dw2", bm=_ffn_tile(f))
    return dh_in, dg, dw1, dw3, dw2


def _norm_fwd(h, g, dep, name):
    t, d = h.shape
    tm = min(512, t)

    def kern(h_ref, g_ref, dep_ref, o_ref):
        x = h_ref[...]
        o_ref[...] = _bf(x * _rstd(x) * g_ref[...])

    row = pl.BlockSpec((tm, d), lambda i: (i, 0))
    return pl.pallas_call(
        kern, name=name, grid=(t // tm,),
        in_specs=[row, pl.BlockSpec((1, d), lambda i: (0, 0)), pl.BlockSpec((8, 128), lambda i: (0, 0))],
        out_specs=row, out_shape=jax.ShapeDtypeStruct((t, d), BF16),
        compiler_params=_params("parallel"))(h, g.reshape(1, d), dep)


def _loss_bwd(h, target, g):
    t, d = h.shape
    tm = min(512, t)

    def kern(h_ref, t_ref, g_ref, dh_ref, dg_ref, loss_ref):
        @pl.when(pl.program_id(0) == 0)
        def _():
            dg_ref[...] = jnp.zeros_like(dg_ref)
            loss_ref[...] = jnp.zeros_like(loss_ref)

        x = h_ref[...]
        rstd = _rstd(x)
        xhat = x * rstd
        err = xhat * g_ref[...] - t_ref[...]
        row_loss = jnp.sum(err * err, axis=-1, keepdims=True) * (0.5 / d)
        loss_ref[...] += jnp.broadcast_to(jnp.sum(row_loss, axis=0, keepdims=True), loss_ref.shape)
        dy = err * (1.0 / d)
        dg_ref[...] += jnp.broadcast_to(jnp.sum(dy * xhat, axis=0, keepdims=True), dg_ref.shape)
        dxh = dy * g_ref[...]
        dh_ref[...] = rstd * (dxh - xhat * jnp.mean(dxh * xhat, axis=-1, keepdims=True))

    row = pl.BlockSpec((tm, d), lambda i: (i, 0))
    return pl.pallas_call(
        kern, name="loss_bwd", grid=(t // tm,),
        in_specs=[row, row, pl.BlockSpec((1, d), lambda i: (0, 0))],
        out_specs=[row, pl.BlockSpec((8, d), lambda i: (0, 0)), pl.BlockSpec((8, 128), lambda i: (0, 0))],
        out_shape=[jax.ShapeDtypeStruct((t, d), F32), jax.ShapeDtypeStruct((8, d), F32),
                   jax.ShapeDtypeStruct((8, 128), F32)],
        compiler_params=_params("arbitrary"))(h, target, g.reshape(1, d))


def _glu_fwd(m, wa, wb):
    t, d = m.shape
    c = wa.shape[1]
    tm, tc = min(512, t), min(512, c)

    def kern(m_ref, wa_ref, wb_ref, a_ref, b_ref, glu_ref):
        mv = m_ref[...]
        a = jnp.dot(mv, wa_ref[...], preferred_element_type=F32)
        b = jnp.dot(mv, wb_ref[...], preferred_element_type=F32)
        a_ref[...] = _bf(a)
        b_ref[...] = _bf(b)
        glu_ref[...] = _bf(a * _sigmoid(b))

    col = pl.BlockSpec((tm, tc), lambda i, j: (i, j))
    wspec = pl.BlockSpec((d, tc), lambda i, j: (0, j))
    shape = jax.ShapeDtypeStruct((t, c), BF16)
    return pl.pallas_call(
        kern, name="conv_glu_fwd", grid=(t // tm, c // tc),
        in_specs=[pl.BlockSpec((tm, d), lambda i, j: (i, 0)), wspec, wspec], out_specs=[col, col, col],
        out_shape=[shape, shape, shape], compiler_params=_params("parallel", "parallel"))(m, wa, wb)


def _conv_tile(t):
    return min(256, t)


def _conv_fwd(glu, w_dw, g):
    t, c = glu.shape
    tm = _conv_tile(t)
    hb = tm // CONV_HALO

    def kern(cur_ref, halo_ref, w_ref, g_ref, cv_ref, s_ref, ext):
        i = pl.program_id(0)
        ext[0:CONV_HALO, :] = jnp.where(i > 0, halo_ref[...].astype(F32), 0.0)
        ext[CONV_HALO:, :] = cur_ref[...].astype(F32)
        gv = g_ref[...]
        for r0 in range(0, tm, CONV_ROWS):
            acc = jnp.zeros((CONV_ROWS, c), F32)
            for k in range(CONV_WIDTH):
                acc = acc + ext[pl.ds(r0 + 2 + k, CONV_ROWS), :] * w_ref[k:k + 1, :]
            cv_ref[r0:r0 + CONV_ROWS, :] = acc
            rn = acc * _rstd(acc) * gv
            s_ref[r0:r0 + CONV_ROWS, :] = _bf(rn * _sigmoid(rn))

    row = pl.BlockSpec((tm, c), lambda i: (i, 0))
    return pl.pallas_call(
        kern, name="conv_fwd", grid=(t // tm,),
        in_specs=[row, pl.BlockSpec((CONV_HALO, c), lambda i: (jnp.maximum(i * hb - 1, 0), 0)),
                  pl.BlockSpec((CONV_HALO, c), lambda i: (0, 0)), pl.BlockSpec((1, c), lambda i: (0, 0))],
        out_specs=[row, row],
        out_shape=[jax.ShapeDtypeStruct((t, c), F32), jax.ShapeDtypeStruct((t, c), BF16)],
        scratch_shapes=[pltpu.VMEM((tm + CONV_HALO, c), F32)],
        compiler_params=_params("parallel"))(glu, glu, w_dw, g.reshape(1, c))


def _conv_bwd_norm(dh, cv, w_pw2, g, dep):
    t, c = cv.shape
    tm = min(512, t)

    def kern(dh_ref, cv_ref, w_ref, g_ref, dep_ref, dcv_ref, dg_ref):
        @pl.when(pl.program_id(0) == 0)
        def _():
            dg_ref[...] = jnp.zeros_like(dg_ref)

        ds = lax.dot_general(_bf(dh_ref[...]), w_ref[...], (((1,), (1,)), ((), ())), preferred_element_type=F32)
        x = cv_ref[...]
        rstd = _rstd(x)
        xhat = x * rstd
        rn = xhat * g_ref[...]
        sig = _sigmoid(rn)
        drn = ds * (sig * (1.0 + rn * (1.0 - sig)))
        dg_ref[...] += jnp.broadcast_to(jnp.sum(drn * xhat, axis=0, keepdims=True), dg_ref.shape)
        dxh = drn * g_ref[...]
        dcv_ref[...] = rstd * (dxh - xhat * jnp.mean(dxh * xhat, axis=-1, keepdims=True))

    row = pl.BlockSpec((tm, c), lambda i: (i, 0))
    return pl.pallas_call(
        kern, name="conv_bwd_norm", grid=(t // tm,),
        in_specs=[pl.BlockSpec((tm, dh.shape[1]), lambda i: (i, 0)), row,
                  pl.BlockSpec(w_pw2.shape, lambda i: (0, 0)), pl.BlockSpec((1, c), lambda i: (0, 0)),
                  pl.BlockSpec((8, 128), lambda i: (0, 0))],
        out_specs=[row, pl.BlockSpec((8, c), lambda i: (0, 0))],
        out_shape=[jax.ShapeDtypeStruct((t, c), F32), jax.ShapeDtypeStruct((8, c), F32)],
        compiler_params=_params("arbitrary"))(dh, cv, w_pw2, g.reshape(1, c), dep)


def _conv_bwd_dw(dcv, glu, a, b, w_dw):
    t, c = dcv.shape
    tm = _conv_tile(t)
    hb = tm // CONV_HALO
    last = t // CONV_HALO - 1

    def kern(dcv_ref, dnext_ref, glu_ref, gprev_ref, a_ref, b_ref, w_ref, da_ref, db_ref, dw_ref, dext, gext):
        i = pl.program_id(0)

        @pl.when(i == 0)
        def _():
            dw_ref[...] = jnp.zeros_like(dw_ref)

        dext[0:tm, :] = dcv_ref[...]
        dext[tm:, :] = jnp.where(i < t // tm - 1, dnext_ref[...], 0.0)
        gext[0:CONV_HALO, :] = jnp.where(i > 0, gprev_ref[...].astype(F32), 0.0)
        gext[CONV_HALO:, :] = glu_ref[...].astype(F32)
        for r0 in range(0, tm, CONV_ROWS):
            acc = jnp.zeros((CONV_ROWS, c), F32)
            for k in range(CONV_WIDTH):
                acc = acc + dext[pl.ds(r0 + CONV_WIDTH - 1 - k, CONV_ROWS), :] * w_ref[k:k + 1, :]
            av = a_ref[r0:r0 + CONV_ROWS, :].astype(F32)
            sig = _sigmoid(b_ref[r0:r0 + CONV_ROWS, :].astype(F32))
            da_ref[r0:r0 + CONV_ROWS, :] = _bf(acc * sig)
            db_ref[r0:r0 + CONV_ROWS, :] = _bf(acc * av * sig * (1.0 - sig))
        for k in range(CONV_WIDTH):
            acc = jnp.zeros((CONV_ROWS, c), F32)
            for r0 in range(0, tm, CONV_ROWS):
                acc = acc + gext[pl.ds(r0 + 2 + k, CONV_ROWS), :] * dext[r0:r0 + CONV_ROWS, :]
            dw_ref[k:k + 1, :] += jnp.sum(acc, axis=0, keepdims=True)

    row = pl.BlockSpec((tm, c), lambda i: (i, 0))
    shape = jax.ShapeDtypeStruct((t, c), BF16)
    return pl.pallas_call(
        kern, name="conv_bwd_dw", grid=(t // tm,),
        in_specs=[row, pl.BlockSpec((CONV_HALO, c), lambda i: (jnp.minimum((i + 1) * hb, last), 0)),
                  row, pl.BlockSpec((CONV_HALO, c), lambda i: (jnp.maximum(i * hb - 1, 0), 0)),
                  row, row, pl.BlockSpec((CONV_HALO, c), lambda i: (0, 0))],
        out_specs=[row, row, pl.BlockSpec((CONV_HALO, c), lambda i: (0, 0))],
        out_shape=[shape, shape, jax.ShapeDtypeStruct((CONV_HALO, c), F32)],
        scratch_shapes=[pltpu.VMEM((tm + CONV_HALO, c), F32), pltpu.VMEM((tm + CONV_HALO, c), F32)],
        compiler_params=_params("arbitrary"))(dcv, dcv, glu, glu, a, b, w_dw)


def _rope_tables(pos):
    t = pos.shape[0]
    tm = min(512, t)
    freq = (np.float32(ROPE_THETA) ** (np.float32(-2.0) * np.arange(ROPE // 2, dtype=np.float32)
                                       / np.float32(ROPE))).astype(np.float32)
    row = np.zeros((2, 128), np.float32)
    row[0, :ROPE] = np.concatenate([freq, freq])
    row[1, :ROPE] = 1.0

    def kern(pos_ref, f_ref, c_ref, s_ref):
        ang = pos_ref[...].astype(F32) * f_ref[0:1, :]
        mask = f_ref[1:2, :]
        c_ref[...] = jnp.cos(ang) * mask
        s_ref[...] = jnp.sin(ang) * mask

    out = pl.BlockSpec((tm, 128), lambda i: (i, 0))
    shape = jax.ShapeDtypeStruct((t, 128), F32)
    return pl.pallas_call(
        kern, name="rope_tables", grid=(t // tm,),
        in_specs=[pl.BlockSpec((tm, 1), lambda i: (i, 0)), pl.BlockSpec((2, 128), lambda i: (0, 0))],
        out_specs=[out, out], out_shape=[shape, shape], compiler_params=_params("parallel"))(pos, jnp.asarray(row))


def _mla_prep(a, gq, gkv, cs_c, cs_s):
    t = a.shape[0]
    tm = min(512, t)
    kv0, r0 = Q_LORA, Q_LORA + KV_LORA

    def kern(a_ref, gq_ref, gkv_ref, c_ref, s_ref, cq_ref, ckv_ref, kr_ref):
        aq = a_ref[:, 0:kv0]
        akv = a_ref[:, kv0:r0]
        ar = a_ref[:, r0:A_PAD]
        cq_ref[...] = _bf(aq * _rstd(aq) * gq_ref[...])
        ckv_ref[...] = _bf(akv * _rstd(akv) * gkv_ref[...])
        kr_ref[...] = _bf(ar * c_ref[...] + _rot(ar) * s_ref[...])

    def row(w):
        return pl.BlockSpec((tm, w), lambda i: (i, 0))

    def vec(w):
        return pl.BlockSpec((1, w), lambda i: (0, 0))

    return pl.pallas_call(
        kern, name="mla_prep", grid=(t // tm,),
        in_specs=[row(A_PAD), vec(Q_LORA), vec(KV_LORA), row(128), row(128)],
        out_specs=[row(Q_LORA), row(KV_LORA), row(128)],
        out_shape=[jax.ShapeDtypeStruct((t, Q_LORA), BF16), jax.ShapeDtypeStruct((t, KV_LORA), BF16),
                   jax.ShapeDtypeStruct((t, 128), BF16)],
        compiler_params=_params("parallel"))(a, gq.reshape(1, -1), gkv.reshape(1, -1), cs_c, cs_s)


def _mla_prep_bwd(a, dcq, dckv, dar, gq, gkv):
    t = a.shape[0]
    tm = min(512, t)
    kv0, r0 = Q_LORA, Q_LORA + KV_LORA

    def kern(a_ref, dcq_ref, dckv_ref, dar_ref, gq_ref, gkv_ref, da_ref, dgq_ref, dgkv_ref):
        @pl.when(pl.program_id(0) == 0)
        def _():
            dgq_ref[...] = jnp.zeros_like(dgq_ref)
            dgkv_ref[...] = jnp.zeros_like(dgkv_ref)

        def back(x, dy, g_ref, dg_ref):
            rstd = _rstd(x)
            xhat = x * rstd
            dg_ref[...] += jnp.broadcast_to(jnp.sum(dy * xhat, axis=0, keepdims=True), dg_ref.shape)
            dxh = dy * g_ref[...]
            return rstd * (dxh - xhat * jnp.mean(dxh * xhat, axis=-1, keepdims=True))

        da_ref[:, 0:kv0] = _bf(back(a_ref[:, 0:kv0], dcq_ref[...], gq_ref, dgq_ref))
        da_ref[:, kv0:r0] = _bf(back(a_ref[:, kv0:r0], dckv_ref[...], gkv_ref, dgkv_ref))
        da_ref[:, r0:A_PAD] = _bf(dar_ref[...])

    def row(w):
        return pl.BlockSpec((tm, w), lambda i: (i, 0))

    def vec(r, w):
        return pl.BlockSpec((r, w), lambda i: (0, 0))

    return pl.pallas_call(
        kern, name="mla_prep_bwd", grid=(t // tm,),
        in_specs=[row(A_PAD), row(Q_LORA), row(KV_LORA), row(128), vec(1, Q_LORA), vec(1, KV_LORA)],
        out_specs=[row(A_PAD), vec(8, Q_LORA), vec(8, KV_LORA)],
        out_shape=[jax.ShapeDtypeStruct((t, A_PAD), BF16), jax.ShapeDtypeStruct((8, Q_LORA), F32),
                   jax.ShapeDtypeStruct((8, KV_LORA), F32)],
        compiler_params=_params("arbitrary"))(a, dcq, dckv, dar, gq.reshape(1, -1), gkv.reshape(1, -1))


def _mla_qkv(cq, ckv, kr, cs_c, cs_s, wuq, wukv):
    t = cq.shape[0]
    tm = min(512, t)

    def kern(cq_ref, ckv_ref, kr_ref, c_ref, s_ref, wq_ref, wkv_ref, q_ref, k_ref, v_ref):
        r = jnp.dot(cq_ref[...], wq_ref[...], preferred_element_type=F32)
        xr = r[:, NOPE:]
        q_ref[:, 0:NOPE] = _bf(r[:, 0:NOPE] * ATTN_SCALE)
        q_ref[:, NOPE:] = _bf((xr * c_ref[...] + _rot(xr) * s_ref[...]) * ATTN_SCALE)
        kv = jnp.dot(ckv_ref[...], wkv_ref[...], preferred_element_type=F32)
        k_ref[:, 0:NOPE] = _bf(kv[:, 0:NOPE])
        k_ref[:, NOPE:] = kr_ref[...]
        v_ref[...] = _bf(kv[:, NOPE:])

    def row(w):
        return pl.BlockSpec((tm, w), lambda i, h: (i, 0))

    def head(w):
        return pl.BlockSpec((None, tm, w), lambda i, h: (h, i, 0))

    return pl.pallas_call(
        kern, name="mla_qkv", grid=(t // tm, HEADS),
        in_specs=[row(Q_LORA), row(KV_LORA), row(128), row(128), row(128),
                  pl.BlockSpec((None, Q_LORA, HEAD_PAD), lambda i, h: (h, 0, 0)),
                  pl.BlockSpec((None, KV_LORA, NOPE + V_HEAD), lambda i, h: (h, 0, 0))],
        out_specs=[head(HEAD_PAD), head(HEAD_PAD), head(V_HEAD)],
        out_shape=[jax.ShapeDtypeStruct((HEADS, t, HEAD_PAD), BF16), jax.ShapeDtypeStruct((HEADS, t, HEAD_PAD), BF16),
                   jax.ShapeDtypeStruct((HEADS, t, V_HEAD), BF16)],
        compiler_params=_params("parallel", "arbitrary"))(cq, ckv, kr, cs_c, cs_s, wuq, wukv)


def _mla_qkv_bwd(dq, dk, dv, cs_c, cs_s, wuq, wukv):
    t = dq.shape[1]
    tm = min(512, t)

    def kern(dq_ref, dk_ref, dv_ref, c_ref, s_ref, wq_ref, wkv_ref, dr_ref, dkv_ref, dcq_ref, dckv_ref, dar_ref):
        @pl.when(pl.program_id(1) == 0)
        def _():
            dcq_ref[...] = jnp.zeros_like(dcq_ref)
            dckv_ref[...] = jnp.zeros_like(dckv_ref)
            dar_ref[...] = jnp.zeros_like(dar_ref)

        cv, sv = c_ref[...], s_ref[...]
        dqx = dq_ref[:, NOPE:]
        dr_ref[:, 0:NOPE] = _bf(dq_ref[:, 0:NOPE] * ATTN_SCALE)
        dr_ref[:, NOPE:] = _bf((dqx * cv + _rot_t(dqx * sv)) * ATTN_SCALE)
        dcq_ref[...] += lax.dot_general(dr_ref[...], wq_ref[...], (((1,), (1,)), ((), ())),
                                        preferred_element_type=F32)
        dkx = dk_ref[:, NOPE:]
        dar_ref[...] += dkx * cv + _rot_t(dkx * sv)
        dkv_ref[:, 0:NOPE] = _bf(dk_ref[:, 0:NOPE])
        dkv_ref[:, NOPE:] = _bf(dv_ref[...])
        dckv_ref[...] += lax.dot_general(dkv_ref[...], wkv_ref[...], (((1,), (1,)), ((), ())),
                                         preferred_element_type=F32)

    def row(w):
        return pl.BlockSpec((tm, w), lambda i, h: (i, 0))

    def head(w):
        return pl.BlockSpec((None, tm, w), lambda i, h: (h, i, 0))

    return pl.pallas_call(
        kern, name="mla_qkv_bwd", grid=(t // tm, HEADS),
        in_specs=[head(HEAD_PAD), head(HEAD_PAD), head(V_HEAD), row(128), row(128),
                  pl.BlockSpec((None, Q_LORA, HEAD_PAD), lambda i, h: (h, 0, 0)),
                  pl.BlockSpec((None, KV_LORA, NOPE + V_HEAD), lambda i, h: (h, 0, 0))],
        out_specs=[head(HEAD_PAD), head(NOPE + V_HEAD), row(Q_LORA), row(KV_LORA), row(128)],
        out_shape=[jax.ShapeDtypeStruct((HEADS, t, HEAD_PAD), BF16),
                   jax.ShapeDtypeStruct((HEADS, t, NOPE + V_HEAD), BF16),
                   jax.ShapeDtypeStruct((t, Q_LORA), F32), jax.ShapeDtypeStruct((t, KV_LORA), F32),
                   jax.ShapeDtypeStruct((t, 128), F32)],
        compiler_params=_params("parallel", "arbitrary"))(dq, dk, dv, cs_c, cs_s, wuq, wukv)


def _attn_block(t):
    return 512 if t >= 4096 else 128


def _chunk_mask(bk, bq):
    kc = lax.broadcasted_iota(jnp.int32, (bk, bq), 0) // CHUNK
    qc = lax.broadcasted_iota(jnp.int32, (bk, bq), 1) // CHUNK
    return qc >= kc


def _flash_fwd(q, k, v):
    t = q.shape[1]
    bq = _attn_block(t)
    nq = t // bq

    def kern(q_ref, k_ref, v_ref, o_ref, lse_ref):
        i = pl.program_id(1)
        qa, qb = q_ref[0:bq, :], q_ref[bq:2 * bq, :]

        def block(j):
            rows = pl.ds(pl.multiple_of(j * bq, bq), bq)
            return k_ref[rows, :], v_ref[rows, :]

        def scores(kj, qv):
            return lax.dot_general(kj, qv, (((1,), (1,)), ((), ())), preferred_element_type=F32)

        def update(st, vj, m, l, acc):
            m_new = jnp.maximum(m, jnp.max(st, axis=0, keepdims=True))
            alpha = jnp.exp(m - m_new)
            p = jnp.exp(st - m_new)
            pv = lax.dot_general(vj, _bf(p), (((0,), (0,)), ((), ())), preferred_element_type=F32)
            return m_new, alpha * l + jnp.sum(p, axis=0, keepdims=True), alpha * acc + pv

        start = (jnp.full((1, bq), -1e30, F32), jnp.zeros((1, bq), F32), jnp.zeros((V_HEAD, bq), F32))
        mask = _chunk_mask(bq, bq)
        k0, v0 = block(2 * i)
        k1, v1 = block(2 * i + 1)
        ca = update(jnp.where(mask, scores(k0, qa), -1e30), v0, *start)
        cb = update(scores(k0, qb), v0, *start)
        cb = update(jnp.where(mask, scores(k1, qb), -1e30), v1, *cb)

        def body(j, carry):
            kj, vj = block(j)
            return (update(scores(kj, qa), vj, *carry[0]), update(scores(kj, qb), vj, *carry[1]))

        ca, cb = lax.fori_loop(0, 2 * i, body, (ca, cb))
        for half, (m, l, acc) in enumerate((ca, cb)):
            o_ref[half * bq:(half + 1) * bq, :] = _bf((acc / l).T)
            lse_ref[half] = jnp.broadcast_to(m + jnp.log(l), (8, bq))

    return pl.pallas_call(
        kern, name="flash_fwd", grid=(HEADS, nq // 2),
        in_specs=[pl.BlockSpec((None, 2 * bq, HEAD_PAD), lambda h, i: (h, i, 0)),
                  pl.BlockSpec((None, t, HEAD_PAD), lambda h, i: (h, 0, 0)),
                  pl.BlockSpec((None, t, V_HEAD), lambda h, i: (h, 0, 0))],
        out_specs=[pl.BlockSpec((2 * bq, V_HEAD), lambda h, i: (i, h)),
                   pl.BlockSpec((None, 2, 8, bq), lambda h, i: (h, i, 0, 0))],
        out_shape=[jax.ShapeDtypeStruct((t, HEADS * V_HEAD), BF16), jax.ShapeDtypeStruct((HEADS, nq, 8, bq), F32)],
        compiler_params=_params("parallel", "arbitrary"))(q, k, v)


def _attn_delta(do, o):
    t = do.shape[0]
    bq = _attn_block(t)

    def kern(do_ref, o_ref, d_ref):
        prod = do_ref[...].astype(F32) * o_ref[...].astype(F32)
        d_ref[...] = jnp.broadcast_to(jnp.sum(prod.T, axis=0, keepdims=True), (8, bq))

    blk = pl.BlockSpec((bq, V_HEAD), lambda h, i: (i, h))
    return pl.pallas_call(
        kern, name="attn_delta", grid=(HEADS, t // bq), in_specs=[blk, blk],
        out_specs=pl.BlockSpec((None, None, 8, bq), lambda h, i: (h, i, 0, 0)),
        out_shape=jax.ShapeDtypeStruct((HEADS, t // bq, 8, bq), F32),
        compiler_params=_params("parallel", "parallel"))(do, o)


def _flash_bwd(q, k, v, do, lse, delta):
    t = q.shape[1]
    bq = _attn_block(t)
    nq = t // bq

    def kern(q_ref, k_ref, v_ref, do_ref, lse_ref, del_ref, dq_ref, dk_ref, dv_ref):
        j = pl.program_id(1)

        @pl.when(j == 0)
        def _():
            dq_ref[...] = jnp.zeros_like(dq_ref)

        dk_ref[...] = jnp.zeros_like(dk_ref)
        dv_ref[...] = jnp.zeros_like(dv_ref)
        kj, vj = k_ref[...], v_ref[...]

        def step(i, masked):
            rows = pl.ds(pl.multiple_of(i * bq, bq), bq)
            qi, doi = q_ref[rows, :], do_ref[rows, :]
            st = lax.dot_general(kj, qi, (((1,), (1,)), ((), ())), preferred_element_type=F32)
            pt = jnp.exp(st - lse_ref[i][0:1, :])
            if masked:
                pt = jnp.where(_chunk_mask(bq, bq), pt, 0.0)
            dpt = lax.dot_general(vj, doi, (((1,), (1,)), ((), ())), preferred_element_type=F32)
            dst = _bf(pt * (dpt - del_ref[i][0:1, :]))
            dv_ref[...] += jnp.dot(_bf(pt), doi, preferred_element_type=F32)
            dk_ref[...] += jnp.dot(dst, qi, preferred_element_type=F32)
            dq_ref[rows, :] += lax.dot_general(dst, kj, (((0,), (0,)), ((), ())), preferred_element_type=F32)

        step(j, True)

        def body(i, carry):
            step(i, False)
            return carry

        lax.fori_loop(j + 1, nq, body, 0)

    stat = pl.BlockSpec((None, nq, 8, bq), lambda h, j: (h, 0, 0, 0))
    return pl.pallas_call(
        kern, name="flash_bwd", grid=(HEADS, nq),
        in_specs=[pl.BlockSpec((None, t, HEAD_PAD), lambda h, j: (h, 0, 0)),
                  pl.BlockSpec((None, bq, HEAD_PAD), lambda h, j: (h, j, 0)),
                  pl.BlockSpec((None, bq, V_HEAD), lambda h, j: (h, j, 0)),
                  pl.BlockSpec((t, V_HEAD), lambda h, j: (0, h)), stat, stat],
        out_specs=[pl.BlockSpec((None, t, HEAD_PAD), lambda h, j: (h, 0, 0)),
                   pl.BlockSpec((None, bq, HEAD_PAD), lambda h, j: (h, j, 0)),
                   pl.BlockSpec((None, bq, V_HEAD), lambda h, j: (h, j, 0))],
        out_shape=[jax.ShapeDtypeStruct((HEADS, t, HEAD_PAD), F32), jax.ShapeDtypeStruct((HEADS, t, HEAD_PAD), F32),
                   jax.ShapeDtypeStruct((HEADS, t, V_HEAD), F32)],
        compiler_params=_params("parallel", "arbitrary"))(q, k, v, do, lse, delta)


def _place():
    x, y, c = lax.axis_index("x"), lax.axis_index("y"), lax.axis_index("c")
    return x, y, c, [(1 - x, y), (x, 1 - y), (1 - x, 1 - y)]


def _all_gather_rows(block, name):
    m_per, n = block.shape

    def body(x_ref, out_ref, send_sems, recv_sems, local_sem):
        x, y, c, chips = _place()
        me, sibling = (x, y, c), (x, y, 1 - c)

        def rows(px, py, pc):
            return out_ref.at[pl.ds((4 * px + 2 * py + pc) * m_per, m_per), :]

        def copy(k, blk, to, src=None):
            return pltpu.make_async_remote_copy(
                src_ref=rows(*blk) if src is None else src, dst_ref=rows(*blk), send_sem=send_sems.at[k],
                recv_sem=recv_sems.at[k], device_id=to, device_id_type=MESH)

        mine = pltpu.make_async_copy(x_ref, rows(*me), local_sem)
        mine.start()
        first = [copy(0, me, sibling, src=x_ref)]
        first += [copy(1 + j, me, (*chip, c), src=x_ref) for j, chip in enumerate(chips)]
        for cp in first:
            cp.start()
        passed = [copy(4 + j, (*chip, c), sibling) for j, chip in enumerate(chips)]
        for j, chip in enumerate(chips):
            copy(1 + j, (*chip, c), me).wait_recv()
            passed[j].start()
        copy(0, sibling, me).wait_recv()
        for j, chip in enumerate(chips):
            copy(4 + j, (*chip, 1 - c), me).wait_recv()
        for cp in first + passed:
            cp.wait_send()
        mine.wait()

    return pl.pallas_call(
        body, name=name, out_shape=jax.ShapeDtypeStruct((8 * m_per, n), block.dtype),
        in_specs=[pl.BlockSpec(memory_space=pltpu.VMEM)], out_specs=pl.BlockSpec(memory_space=pltpu.VMEM),
        scratch_shapes=[pltpu.SemaphoreType.DMA((7,)), pltpu.SemaphoreType.DMA((7,)), pltpu.SemaphoreType.DMA],
        compiler_params=pltpu.CompilerParams(vmem_limit_bytes=VMEM_LIMIT_BYTES))(block)


HBM_SPEC = pl.BlockSpec(memory_space=pltpu.HBM)
SEM_SPEC = pl.BlockSpec(memory_space=pltpu.SEMAPHORE)
DATAFLOW = pltpu.SideEffectType.DATAFLOW_SIDE_EFFECTING


def _in_hbm(a):
    return pltpu.with_memory_space_constraint(a, pltpu.HBM)


def _chip_copies(ins, lands, send_sems, recv_sems, src_slot):
    n = len(ins)
    x, y, c, chips = _place()
    me = 2 * x + y
    return [pltpu.make_async_remote_copy(
        src_ref=ins[w].at[2 * chip[0] + chip[1]] if src_slot else ins[w], dst_ref=lands[w].at[me],
        send_sem=send_sems.at[p * n + w], recv_sem=recv_sems.at[p * n + w], device_id=(*chip, c),
        device_id_type=MESH) for w in range(n) for p, chip in enumerate(chips)]


def _exchange_start(srcs, lands, src_slot, name, dep=None):
    n = len(srcs)
    first_out = 2 * n + (dep is not None)

    def body(*refs):
        for cp in _chip_copies(refs[:n], refs[n:2 * n], refs[first_out], refs[first_out + 1], src_slot):
            cp.start()
        token = refs[-1]
        token[...] = jnp.zeros_like(token)

    thru = [pltpu.HBM(a.shape, a.dtype) for a in list(srcs) + list(lands)]
    res = pl.pallas_call(
        body, name=name,
        out_shape=(pltpu.SemaphoreType.DMA((3 * n,)), pltpu.SemaphoreType.DMA((3 * n,)), *thru,
                   jax.ShapeDtypeStruct((8, 128), F32)),
        in_specs=[HBM_SPEC] * (2 * n) + ([pl.BlockSpec(memory_space=pl.ANY)] if dep is not None else []),
        out_specs=(SEM_SPEC, SEM_SPEC, *[HBM_SPEC] * (2 * n), pl.BlockSpec(memory_space=pltpu.VMEM)),
        input_output_aliases={i: 2 + i for i in range(2 * n)},
        compiler_params=pltpu.CompilerParams(has_side_effects=DATAFLOW))(
            *[_in_hbm(a) for a in srcs], *[_in_hbm(a) for a in lands], *([dep] if dep is not None else []))
    return (res[0], res[1], list(res[2:2 + n]), list(res[2 + n:2 + 2 * n])), res[-1]


def _exchange_wait(flight, after, src_slot, name):
    send_sems, recv_sems, srcs, lands = flight
    n = len(srcs)

    def body(*refs):
        for cp in _chip_copies(refs[:n], refs[n:2 * n], refs[2 * n], refs[2 * n + 1], src_slot):
            cp.wait_send()
            cp.wait_recv()

    thru = [pltpu.HBM(a.shape, a.dtype) for a in list(srcs) + list(lands)]
    res = pl.pallas_call(
        body, name=name, out_shape=thru,
        in_specs=[HBM_SPEC] * (2 * n) + [SEM_SPEC, SEM_SPEC, pl.BlockSpec(memory_space=pl.ANY)],
        out_specs=[HBM_SPEC] * (2 * n), input_output_aliases={i: i for i in range(2 * n)},
        compiler_params=pltpu.CompilerParams(has_side_effects=DATAFLOW))(*srcs, *lands, send_sems, recv_sems, after)
    return list(res[n:])


def _landing(own, me):
    return lax.dynamic_update_index_in_dim(lax.empty((4, *own.shape), own.dtype), own, me, 0)


def _swap_with_sibling(arrays, name):
    n = len(arrays)

    def body(*refs):
        ins, outs = refs[:n], refs[n:2 * n]
        send_sems, recv_sems = refs[2 * n:]
        x, y, c, _ = _place()
        copies = [pltpu.make_async_remote_copy(src_ref=ins[w], dst_ref=outs[w], send_sem=send_sems.at[w],
                                               recv_sem=recv_sems.at[w], device_id=(x, y, 1 - c), device_id_type=MESH)
                  for w in range(n)]
        for cp in copies:
            cp.start()
        for cp in copies:
            cp.wait()

    any_spec = pl.BlockSpec(memory_space=pl.ANY)
    return pl.pallas_call(
        body, name=name, out_shape=[jax.ShapeDtypeStruct(a.shape, a.dtype) for a in arrays],
        in_specs=[any_spec] * n, out_specs=[any_spec] * n,
        scratch_shapes=[pltpu.SemaphoreType.DMA((n,)), pltpu.SemaphoreType.DMA((n,))])(*arrays)


def _as_rows(a):
    return a.reshape(-1, a.shape[-1])


def _row_tile(r, c, budget_bytes=1 << 20):
    tr = r
    while tr % 16 == 0 and tr * c * 4 > budget_bytes:
        tr //= 2
    return tr


def _sum_slots(layers, nlayer, name, into=None):
    _, r, c = layers[0][1].shape
    tr = _row_tile(r, c)
    nt = r // tr
    acc = into
    for l, r4 in layers:
        def kern(r_ref, *rest):
            o_ref = rest[-1]
            o_ref[...] = (((r_ref[0].astype(F32) + r_ref[1].astype(F32)) + r_ref[2].astype(F32))
                          + r_ref[3].astype(F32))

        out_spec = pl.BlockSpec((tr, c), lambda i, l=l: (l * nt + i, 0))
        first = acc is None
        acc = pl.pallas_call(
            kern, name=f"{name}_l{l}", grid=(nt,),
            in_specs=[pl.BlockSpec((4, tr, c), lambda i: (0, i, 0))]
            + ([] if first else [pl.BlockSpec(memory_space=pl.ANY)]),
            out_specs=out_spec, out_shape=jax.ShapeDtypeStruct((nlayer * r, c), F32),
            input_output_aliases={} if first else {1: 0},
            compiler_params=_params("parallel"))(*([r4] if first else [r4, acc]))
    return acc


def _adamw(w, m, v, parts, name):
    r, c = w.shape
    tr = _row_tile(r, c, 1 << 19)
    npart = len(parts)
    c1 = 1.0 - ADAM_B1 ** ADAM_STEP
    c2 = 1.0 - ADAM_B2 ** ADAM_STEP

    def kern(*refs):
        w_ref, m_ref, v_ref = refs[:3]
        p_refs = refs[3:3 + npart]
        g_ref, d_ref, mo_ref, vo_ref = refs[3 + npart:]
        g = p_refs[0][...]
        for p in p_refs[1:]:
            g = g + p[...]
        mn = ADAM_B1 * m_ref[...] + (1.0 - ADAM_B1) * g
        vn = ADAM_B2 * v_ref[...] + (1.0 - ADAM_B2) * (g * g)
        g_ref[...] = g
        mo_ref[...] = mn
        vo_ref[...] = vn
        d_ref[...] = -ADAM_LR * ((mn / c1) / (jnp.sqrt(vn / c2) + ADAM_EPS) + ADAM_WD * w_ref[...])

    blk = pl.BlockSpec((tr, c), lambda i: (i, 0))
    shape = jax.ShapeDtypeStruct((r, c), F32)
    return pl.pallas_call(
        kern, name=name, grid=(r // tr,), in_specs=[blk] * (3 + npart), out_specs=[blk] * 4, out_shape=[shape] * 4,
        compiler_params=_params("parallel"))(w, m, v, *parts)


def _sum_devices(g8, name):
    _, r, c = g8.shape

    def kern(g_ref, o_ref):
        tot = g_ref[0]
        for dev in range(1, 8):
            tot = tot + g_ref[dev]
        o_ref[...] = tot

    return pl.pallas_call(
        kern, name=name, grid=(1,), in_specs=[pl.BlockSpec((8, r, c), lambda i: (0, 0, 0))],
        out_specs=pl.BlockSpec((r, c), lambda i: (0, 0)), out_shape=jax.ShapeDtypeStruct((r, c), F32),
        compiler_params=_params("arbitrary"))(g8)


def _pad_lanes(a, width):
    return jnp.pad(a, [(0, 0)] * (a.ndim - 1) + [(0, width - a.shape[-1])])


def kernel(x, positions, ffn_norm1, ffn1_w1, ffn1_w3, ffn1_w2, mix_norm, ffn_norm2, ffn2_w1, ffn2_w3, ffn2_w2, conv_w_pw1, conv_w_dw, conv_norm, conv_w_pw2, mla_w_a, mla_q_norm, mla_kv_norm, mla_w_uq, mla_w_ukv, mla_w_o, final_norm, loss_target, m_ffn_norm1, m_ffn1_w1, m_ffn1_w3, m_ffn1_w2, m_mix_norm, m_ffn_norm2, m_ffn2_w1, m_ffn2_w3, m_ffn2_w2, m_conv_w_pw1, m_conv_w_dw, m_conv_norm, m_conv_w_pw2, m_mla_w_a, m_mla_q_norm, m_mla_kv_norm, m_mla_w_uq, m_mla_w_ukv, m_mla_w_o, m_final_norm, v_ffn_norm1, v_ffn1_w1, v_ffn1_w3, v_ffn1_w2, v_mix_norm, v_ffn_norm2, v_ffn2_w1, v_ffn2_w3, v_ffn2_w2, v_conv_w_pw1, v_conv_w_dw, v_conv_norm, v_conv_w_pw2, v_mla_w_a, v_mla_q_norm, v_mla_kv_norm, v_mla_w_uq, v_mla_w_ukv, v_mla_w_o, v_final_norm):
    given = locals()
    return _step({nm: given[nm] for nm in INPUTS})


def _step(A):
    x = A['x'][0]
    target = A['loss_target'][0]
    t, d = x.shape
    pos = A['positions'].reshape(t, 1)
    me = 2 * lax.axis_index("x") + lax.axis_index("y")

    ffn = [f'ffn{k}_{w}' for k in (1, 2) for w in ('w1', 'w3', 'w2')]
    gather_groups = [[(nm, 0) for nm in ffn[:3]],
                     [('conv_w_pw1', 0), ('conv_w_pw2', 0)] + [(nm, 0) for nm in ffn[3:]],
                     [(nm, 1) for nm in ffn[:3]] + [('mla_w_a', 0), ('mla_w_uq', 0), ('mla_w_ukv', 0), ('mla_w_o', 0)],
                     [(nm, 1) for nm in ffn[3:]]]
    gather_flights = {}
    big = {}

    def gather_start(gi, dep):
        shards = [_bf(A[nm][l]) for nm, l in gather_groups[gi]]
        gather_flights[gi], token = _exchange_start(shards, [_landing(s, me) for s in shards], False,
                                                    f"gather_start_{gi}", dep)
        return token

    def gather_wait(gi, after):
        landed = _exchange_wait(gather_flights[gi], after, False, f"gather_wait_{gi}")
        big.update(zip(gather_groups[gi], landed))
        return landed[0]

    dw_shard = A['conv_w_dw'][0]
    cw = dw_shard.shape[1]
    small = jnp.concatenate([
        jnp.pad(dw_shard, ((0, CONV_HALO - CONV_WIDTH), (0, 0))),
        jnp.pad(_pad_lanes(A['mla_q_norm'], cw), ((0, 7), (0, 0))),
        jnp.pad(_pad_lanes(A['mla_kv_norm'], cw), ((0, 7), (0, 0)))], axis=0)
    small = _all_gather_rows(small, "gather_small_weights").reshape(4, 2, 48, cw)[:, 0]
    w_dw = jnp.concatenate([small[j, :CONV_HALO] for j in range(4)], axis=1)
    gq = jnp.concatenate([small[j, CONV_HALO, :Q_LORA // 4] for j in range(4)])
    gkv = jnp.concatenate([small[j, CONV_HALO + 8, :KV_LORA // 4] for j in range(4)])

    def cols(nm, layer):
        return jnp.concatenate([big[nm, layer][j] for j in range(4)], axis=1)

    def rows(nm, layer):
        g = big[nm, layer]
        return g.reshape(-1, g.shape[-1])

    ffn_w = {}

    def ffn_weights(k, l):
        ffn_w[k, l] = (cols(f'ffn{k}_w1', l), cols(f'ffn{k}_w3', l), rows(f'ffn{k}_w2', l))
        return ffn_w[k, l]

    token = gather_start(0, None)
    cs_c, cs_s = _rope_tables(pos)
    h0 = x
    token = gather_start(1, gather_wait(0, token))
    h1, n01, z01a, z01b = _ffn_fwd(h0, A['ffn_norm1'][0], *ffn_weights(1, 0), token, "ffn1_l0_fwd")
    token = gather_start(3, gather_start(2, gather_wait(1, h1)))
    pw1 = big['conv_w_pw1', 0]
    pw1_a = jnp.concatenate([pw1[0], pw1[1]], axis=1)
    pw1_b = jnp.concatenate([pw1[2], pw1[3]], axis=1)
    pw2 = rows('conv_w_pw2', 0)
    m0 = _norm_fwd(h1, A['mix_norm'][0], token, "mix_norm_l0")
    ca, cb, glu = _glu_fwd(m0, pw1_a, pw1_b)
    cv, cs = _conv_fwd(glu, w_dw, A['conv_norm'][0])
    h2 = _mm([(cs, pw2)], F32, "conv_pw2_fwd", res=h1)
    h3, n02, z02a, z02b = _ffn_fwd(h2, A['ffn_norm2'][0], *ffn_weights(2, 0), token, "ffn2_l0_fwd")
    gather_wait(2, h3)
    w_a = _pad_lanes(rows('mla_w_a', 0), A_PAD)
    wuq = _pad_lanes(big['mla_w_uq', 0].reshape(Q_LORA, HEADS, NOPE + ROPE).transpose(1, 0, 2), HEAD_PAD)
    wukv = big['mla_w_ukv', 0].reshape(KV_LORA, HEADS, NOPE + V_HEAD).transpose(1, 0, 2)
    w_o = rows('mla_w_o', 0)
    h4, n11, z11a, z11b = _ffn_fwd(h3, A['ffn_norm1'][1], *ffn_weights(1, 1), token, "ffn1_l1_fwd")
    m1 = _norm_fwd(h4, A['mix_norm'][1], token, "mix_norm_l1")
    a_lat = _mm([(m1, w_a)], F32, "mla_down_fwd")
    cq, ckv, kr = _mla_prep(a_lat, gq, gkv, cs_c, cs_s)
    q, k, v = _mla_qkv(cq, ckv, kr, cs_c, cs_s, wuq, wukv)
    o, lse = _flash_fwd(q, k, v)
    h5 = _mm([(o, w_o)], F32, "mla_out_fwd", res=h4)
    gather_wait(3, h5)
    h6, n12, z12a, z12b = _ffn_fwd(h5, A['ffn_norm2'][1], *ffn_weights(2, 1), token, "ffn2_l1_fwd")

    def col_slots(g):
        r, c4 = g.shape
        return g.reshape(r, 4, c4 // 4).transpose(1, 0, 2)

    def row_slots(g):
        return g.reshape(4, g.shape[0] // 4, g.shape[1])

    scatter_flights = []

    def scatter_start(named):
        srcs = [g for _, g in named]
        lands = [_landing(lax.dynamic_index_in_dim(g, me, 0, keepdims=False), me) for g in srcs]
        flight, token = _exchange_start(srcs, lands, True, f"scatter_start_{len(scatter_flights)}")
        scatter_flights.append(([key for key, _ in named], flight))
        return token

    def send_ffn(k, l, dw1, dw3, dw2):
        return scatter_start([((f'ffn{k}_w1', l), col_slots(dw1)), ((f'ffn{k}_w3', l), col_slots(dw3)),
                              ((f'ffn{k}_w2', l), row_slots(dw2))])

    dh6, dg_final, loss_part = _loss_bwd(h6, target, A['final_norm'])
    dh5, dg_n2_l1, *dws = _ffn_bwd(dh6, h5, A['ffn_norm2'][1], n12, z12a, z12b, *ffn_w[2, 1], loss_part, "ffn2_l1")
    token = send_ffn(2, 1, *dws)

    do = _mm([(dh5, w_o)], BF16, "mla_out_bwd", trans_b=True, dep=token)
    dw_o = _mm_tn(o, dh5, BF16, "mla_dw_o")
    delta = _attn_delta(do, o)
    dq, dk, dv = _flash_bwd(q, k, v, do, lse, delta)
    dr, dkv, dcq, dckv, dar = _mla_qkv_bwd(dq, dk, dv, cs_c, cs_s, wuq, wukv)
    dwuq = _mm_tn(cq, dr, BF16, "mla_dw_uq")
    dwukv = _mm_tn(ckv, dkv, BF16, "mla_dw_ukv")
    da_lat, dgq, dgkv = _mla_prep_bwd(a_lat, dcq, dckv, dar, gq, gkv)
    dw_a = _mm_tn(m1, da_lat, BF16, "mla_dw_a")
    token = scatter_start([
        (('mla_w_a', 0), row_slots(dw_a[:, :Q_LORA + KV_LORA + ROPE])),
        (('mla_w_uq', 0), dwuq[:, :, :NOPE + ROPE].transpose(1, 0, 2).reshape(4, Q_LORA // 4, HEADS, NOPE + ROPE)),
        (('mla_w_ukv', 0), dwukv.transpose(1, 0, 2).reshape(4, KV_LORA // 4, HEADS, NOPE + V_HEAD)),
        (('mla_w_o', 0), row_slots(dw_o))])
    dh4, dg_mix_l1 = _mm_normbwd([(da_lat, w_a)], h4, A['mix_norm'][1], dh5, token, "mla_down_bwd")

    dh3, dg_n1_l1, *dws = _ffn_bwd(dh4, h3, A['ffn_norm1'][1], n11, z11a, z11b, *ffn_w[1, 1], token, "ffn1_l1")
    token = send_ffn(1, 1, *dws)
    dh2, dg_n2_l0, *dws = _ffn_bwd(dh3, h2, A['ffn_norm2'][0], n02, z02a, z02b, *ffn_w[2, 0], token, "ffn2_l0")
    token = send_ffn(2, 0, *dws)

    dcv, dg_conv = _conv_bwd_norm(dh2, cv, pw2, A['conv_norm'][0], token)
    dw_pw2 = _mm_tn(cs, dh2, BF16, "conv_dw_pw2")
    dca, dcb, ddw = _conv_bwd_dw(dcv, glu, ca, cb, w_dw)
    dpw1_a = _mm_tn(m0, dca, BF16, "conv_dw_pw1a")
    dpw1_b = _mm_tn(m0, dcb, BF16, "conv_dw_pw1b")
    half = dpw1_a.shape[1] // 2
    token = scatter_start([
        (('conv_w_pw1', 0), jnp.stack([dpw1_a[:, :half], dpw1_a[:, half:], dpw1_b[:, :half], dpw1_b[:, half:]])),
        (('conv_w_pw2', 0), row_slots(dw_pw2))])
    dh1, dg_mix_l0 = _mm_normbwd([(dca, pw1_a), (dcb, pw1_b)], h1, A['mix_norm'][0], dh2, token, "conv_pw1_bwd")

    dx, dg_n1_l0, *dws = _ffn_bwd(dh1, h0, A['ffn_norm1'][0], n01, z01a, z01b, *ffn_w[1, 0], token, "ffn1_l0")
    last_sent = send_ffn(1, 0, *dws)
    out = {}

    qkv_row = jnp.concatenate([dgq, dgkv, jnp.zeros((8, d - Q_LORA - KV_LORA), F32)], axis=1)
    loss_row = _pad_lanes(loss_part, d)
    small_g = jnp.concatenate([dg_n1_l0, dg_n1_l1, dg_mix_l0, dg_mix_l1, dg_n2_l0, dg_n2_l1, dg_conv, dg_final,
                               qkv_row, loss_row, ddw], axis=0)
    nrow = small_g.shape[0]
    tot = _sum_devices(_all_gather_rows(small_g, "gather_small_grads").reshape(8, nrow, d), "sum_small_grads")
    loss = tot[72, 0]
    q_shard = lax.dynamic_slice_in_dim(tot[64, :Q_LORA], me * (Q_LORA // 4), Q_LORA // 4)
    kv_shard = lax.dynamic_slice_in_dim(tot[64, Q_LORA:Q_LORA + KV_LORA], me * (KV_LORA // 4), KV_LORA // 4)
    dw_shard_g = lax.dynamic_slice_in_dim(tot[80:80 + CONV_WIDTH], me * cw, cw, axis=1)
    small_grads = {
        'ffn_norm1': jnp.stack([tot[0], tot[8]]), 'mix_norm': jnp.stack([tot[16], tot[24]]),
        'ffn_norm2': jnp.stack([tot[32], tot[40]]), 'conv_norm': tot[48][None], 'final_norm': tot[56],
        'mla_q_norm': q_shard[None], 'mla_kv_norm': kv_shard[None], 'conv_w_dw': dw_shard_g[None],
    }
    for nm, g in small_grads.items():
        res = _adamw(_as_rows(A[nm]) if A[nm].ndim > 1 else A[nm].reshape(1, -1),
                     A['m_' + nm].reshape(-1, A[nm].shape[-1]), A['v_' + nm].reshape(-1, A[nm].shape[-1]),
                     [g.reshape(-1, A[nm].shape[-1])], "adamw_" + nm)
        out[nm] = [r.reshape(A[nm].shape) for r in res]

    received = {}
    after = last_sent

    def scatter_wait(si, after):
        keys, flight = scatter_flights[si]
        landed = _exchange_wait(flight, after, True, f"scatter_wait_{si}")
        received.update(zip(keys, landed))
        return landed[0]

    def slots(nm, l):
        return received[nm, l].reshape(4, -1, received[nm, l].shape[-1])

    def finish(names, sums, tag):
        for nm, mine, theirs in zip(names, sums, _swap_with_sibling(sums, "swap_with_sibling_" + tag)):
            res = _adamw(_as_rows(A[nm]), _as_rows(A['m_' + nm]), _as_rows(A['v_' + nm]), [mine, theirs],
                         "adamw_" + nm)
            out[nm] = [r.reshape(A[nm].shape) for r in res]
        return res[1]

    last = len(scatter_flights) - 1
    for si in range(last):
        after = scatter_wait(si, after)
    late = ffn[:3]
    early = [nm for nm in BIG if nm not in late]
    late_l1 = [_sum_slots([(1, slots(nm, 1))], 2, "sum_" + nm) for nm in late]
    after = finish(early, [_sum_slots([(l, slots(nm, l)) for l in range(A[nm].shape[0])], A[nm].shape[0],
                                      "sum_" + nm) for nm in early], "early")
    scatter_wait(last, after)
    finish(late, [_sum_slots([(0, slots(nm, 0))], 2, "sum_" + nm, into=part) for nm, part in zip(late, late_l1)],
           "late")

    return (loss, dx[None], *[out[nm][0] for nm in WEIGHTS], *[out[nm][1] for nm in WEIGHTS],
            *[out[nm][2] for nm in WEIGHTS], *[out[nm][3] for nm in WEIGHTS])
```

```python
import functools

import jax
import jax.numpy as jnp
import numpy as np
from jax import lax
from jax.experimental import pallas as pl
from jax.experimental.pallas import tpu as pltpu

F32 = jnp.float32
BF16 = jnp.bfloat16
MESH = pl.DeviceIdType.MESH

RMS_EPS = 1e-6
HEADS = 8
NOPE = 128
ROPE = 64
HEAD_PAD = 256
V_HEAD = 128
Q_LORA = 512
KV_LORA = 256
A_PAD = 896
CHUNK = 64
CONV_WIDTH = 31
CONV_HALO = 32
CONV_ROWS = 16
ROPE_THETA = 10000.0
ATTN_SCALE = (NOPE + ROPE) ** -0.5
FFN_RES = 0.5

ADAM_LR = 0.001
ADAM_B1 = 0.9
ADAM_B2 = 0.999
ADAM_EPS = 1e-08
ADAM_WD = 0.01
ADAM_STEP = 10

VMEM_LIMIT_BYTES = 56 * 1024 * 1024

WEIGHTS = ['ffn_norm1', 'ffn1_w1', 'ffn1_w3', 'ffn1_w2', 'mix_norm', 'ffn_norm2', 'ffn2_w1', 'ffn2_w3', 'ffn2_w2',
           'conv_w_pw1', 'conv_w_dw', 'conv_norm', 'conv_w_pw2', 'mla_w_a', 'mla_q_norm', 'mla_kv_norm', 'mla_w_uq',
           'mla_w_ukv', 'mla_w_o', 'final_norm']
INPUTS = (['x', 'positions'] + WEIGHTS + ['loss_target'] + ['m_' + w for w in WEIGHTS] + ['v_' + w for w in WEIGHTS])
BIG = ['ffn1_w1', 'ffn1_w3', 'ffn1_w2', 'ffn2_w1', 'ffn2_w3', 'ffn2_w2', 'conv_w_pw1', 'conv_w_pw2', 'mla_w_a',
       'mla_w_uq', 'mla_w_ukv', 'mla_w_o']


def _params(*sem):
    return pltpu.CompilerParams(dimension_semantics=sem, vmem_limit_bytes=VMEM_LIMIT_BYTES)


def _bf(v):
    return v.astype(BF16)


def _rstd(x):
    return lax.rsqrt(jnp.mean(x * x, axis=-1, keepdims=True) + RMS_EPS)


def _sigmoid(x):
    return jax.nn.sigmoid(x)


def _rot(x):
    lane = lax.broadcasted_iota(jnp.int32, x.shape, 1)
    return jnp.where(lane < ROPE // 2, -pltpu.roll(x, 128 - ROPE // 2, 1), pltpu.roll(x, ROPE // 2, 1))


def _rot_t(y):
    lane = lax.broadcasted_iota(jnp.int32, y.shape, 1)
    return jnp.where(lane < ROPE // 2, pltpu.roll(y, 128 - ROPE // 2, 1), -pltpu.roll(y, ROPE // 2, 1))


def _pair_sum(a_refs, b_refs, trans_b):
    tot = None
    for a_r, b_r in zip(a_refs, b_refs):
        a, b = _bf(a_r[...]), _bf(b_r[...])
        if trans_b:
            d = lax.dot_general(a, b, (((1,), (1,)), ((), ())), preferred_element_type=F32)
        else:
            d = jnp.dot(a, b, preferred_element_type=F32)
        tot = d if tot is None else tot + d
    return tot


def _mm(pairs, out_dtype, name, *, trans_b=False, tm=512, tn=None, tk=None, res=None, dep=None):
    m, k = pairs[0][0].shape
    n = pairs[0][1].shape[0] if trans_b else pairs[0][1].shape[1]
    tm, tn, tk = min(tm, m), tn or n, tk or k
    nk, npair = k // tk, len(pairs)

    def kern(*refs):
        a_refs, b_refs = refs[:npair], refs[npair:2 * npair]
        rest = list(refs[2 * npair:])
        res_ref = rest.pop(0) if res is not None else None
        if dep is not None:
            rest.pop(0)
        o_ref = rest.pop(0)

        def finish(acc):
            if res_ref is not None:
                acc = res_ref[...] + acc
            o_ref[...] = acc.astype(o_ref.dtype)

        if nk == 1:
            finish(_pair_sum(a_refs, b_refs, trans_b))
        else:
            acc_ref = rest.pop(0)
            kk = pl.program_id(2)

            @pl.when(kk == 0)
            def _():
                acc_ref[...] = jnp.zeros_like(acc_ref)

            acc_ref[...] += _pair_sum(a_refs, b_refs, trans_b)

            @pl.when(kk == nk - 1)
            def _():
                finish(acc_ref[...])

    a_spec = pl.BlockSpec((tm, tk), lambda i, j, kk: (i, kk))
    b_spec = (pl.BlockSpec((tn, tk), lambda i, j, kk: (j, kk)) if trans_b
              else pl.BlockSpec((tk, tn), lambda i, j, kk: (kk, j)))
    io_spec = pl.BlockSpec((tm, tn), lambda i, j, kk: (i, j))
    in_specs = ([a_spec] * npair + [b_spec] * npair + ([io_spec] if res is not None else [])
                + ([pl.BlockSpec((8, 128), lambda i, j, kk: (0, 0))] if dep is not None else []))
    args = ([p[0] for p in pairs] + [p[1] for p in pairs] + ([res] if res is not None else [])
            + ([dep] if dep is not None else []))
    return pl.pallas_call(
        kern, name=name, grid=(m // tm, n // tn, nk), in_specs=in_specs, out_specs=io_spec,
        out_shape=jax.ShapeDtypeStruct((m, n), out_dtype),
        scratch_shapes=[pltpu.VMEM((tm, tn), F32)] if nk > 1 else [],
        compiler_params=_params("parallel", "parallel", "arbitrary"))(*args)


def _mm_normbwd(pairs, h, g, dres, dep, name, *, tm=512, tk=None):
    m, k = pairs[0][0].shape
    d = pairs[0][1].shape[0]
    tm, tk = min(tm, m), tk or k
    nk, npair = k // tk, len(pairs)

    def kern(*refs):
        a_refs, b_refs = refs[:npair], refs[npair:2 * npair]
        h_ref, g_ref, dres_ref, _, o_ref, dg_ref, acc_ref = refs[2 * npair:]
        i, kk = pl.program_id(0), pl.program_id(1)

        @pl.when(jnp.logical_and(i == 0, kk == 0))
        def _():
            dg_ref[...] = jnp.zeros_like(dg_ref)

        @pl.when(kk == 0)
        def _():
            acc_ref[...] = jnp.zeros_like(acc_ref)

        acc_ref[...] += _pair_sum(a_refs, b_refs, True)

        @pl.when(kk == nk - 1)
        def _():
            dn = acc_ref[...]
            x = h_ref[...]
            rstd = _rstd(x)
            xhat = x * rstd
            dg_ref[...] += jnp.broadcast_to(jnp.sum(dn * xhat, axis=0, keepdims=True), dg_ref.shape)
            dxh = dn * g_ref[...]
            dx = rstd * (dxh - xhat * jnp.mean(dxh * xhat, axis=-1, keepdims=True))
            o_ref[...] = dres_ref[...] + dx

    row = pl.BlockSpec((tm, d), lambda i, kk: (i, 0))
    in_specs = ([pl.BlockSpec((tm, tk), lambda i, kk: (i, kk))] * npair
                + [pl.BlockSpec((d, tk), lambda i, kk: (0, kk))] * npair
                + [row, pl.BlockSpec((1, d), lambda i, kk: (0, 0)), row, pl.BlockSpec((8, 128), lambda i, kk: (0, 0))])
    return pl.pallas_call(
        kern, name=name, grid=(m // tm, nk), in_specs=in_specs,
        out_specs=[row, pl.BlockSpec((8, d), lambda i, kk: (0, 0))],
        out_shape=[jax.ShapeDtypeStruct((m, d), F32), jax.ShapeDtypeStruct((8, d), F32)],
        scratch_shapes=[pltpu.VMEM((tm, d), F32)],
        compiler_params=_params("arbitrary", "arbitrary"))(
            *[p[0] for p in pairs], *[p[1] for p in pairs], h, g.reshape(1, d), dres, dep)


def _mm_tn(a, b, out_dtype, name, *, bm=None, bn=None, tk=512):
    t, m = a.shape
    batched = b.ndim == 3
    n = b.shape[-1]
    nb = b.shape[0] if batched else 1
    bm, bn, tk = bm or m, bn or n, min(tk, t)
    nk = t // tk

    def kern(a_ref, b_ref, o_ref, acc_ref):
        kk = pl.program_id(3)

        @pl.when(kk == 0)
        def _():
            acc_ref[...] = jnp.zeros_like(acc_ref)

        acc_ref[...] += lax.dot_general(_bf(a_ref[...]), _bf(b_ref[...]), (((0,), (0,)), ((), ())),
                                        preferred_element_type=F32)

        @pl.when(kk == nk - 1)
        def _():
            o_ref[...] = acc_ref[...].astype(o_ref.dtype)

    a_spec = pl.BlockSpec((tk, bm), lambda h, i, j, kk: (kk, i))
    if batched:
        b_spec = pl.BlockSpec((None, tk, bn), lambda h, i, j, kk: (h, kk, j))
        o_spec = pl.BlockSpec((None, bm, bn), lambda h, i, j, kk: (h, i, j))
        out_shape = jax.ShapeDtypeStruct((nb, m, n), out_dtype)
    else:
        b_spec = pl.BlockSpec((tk, bn), lambda h, i, j, kk: (kk, j))
        o_spec = pl.BlockSpec((bm, bn), lambda h, i, j, kk: (i, j))
        out_shape = jax.ShapeDtypeStruct((m, n), out_dtype)
    return pl.pallas_call(
        kern, name=name, grid=(nb, m // bm, n // bn, nk), in_specs=[a_spec, b_spec], out_specs=o_spec,
        out_shape=out_shape, scratch_shapes=[pltpu.VMEM((bm, bn), F32)],
        compiler_params=_params("parallel", "parallel", "parallel", "arbitrary"))(a, b)


def _ffn_tile(f):
    return f // 2 if (f // 2) % 128 == 0 else f


def _ffn_fwd(h, g, w1, w3, w2, dep, name):
    t, d = h.shape
    f = w1.shape[1]
    tm, tf = min(512, t), _ffn_tile(f)
    nf = f // tf

    def kern(h_ref, g_ref, w1_ref, w3_ref, w2_ref, dep_ref, ho_ref, n_ref, z1_ref, z3_ref, n_sc, acc_ref):
        j = pl.program_id(1)

        @pl.when(j == 0)
        def _():
            x = h_ref[...]
            n = _bf(x * _rstd(x) * g_ref[...])
            n_sc[...] = n
            n_ref[...] = n
            acc_ref[...] = jnp.zeros_like(acc_ref)

        n = n_sc[...]
        z1 = jnp.dot(n, w1_ref[...], preferred_element_type=F32)
        z3 = jnp.dot(n, w3_ref[...], preferred_element_type=F32)
        z1_ref[...] = _bf(z1)
        z3_ref[...] = _bf(z3)
        act = _bf(z1 * _sigmoid(z1) * z3)
        acc_ref[...] += jnp.dot(act, w2_ref[...], preferred_element_type=F32)

        @pl.when(j == nf - 1)
        def _():
            ho_ref[...] = h_ref[...] + FFN_RES * acc_ref[...]

    row = pl.BlockSpec((tm, d), lambda i, j: (i, 0))
    col = pl.BlockSpec((tm, tf), lambda i, j: (i, j))
    return pl.pallas_call(
        kern, name=name, grid=(t // tm, nf),
        in_specs=[row, pl.BlockSpec((1, d), lambda i, j: (0, 0)), pl.BlockSpec((d, tf), lambda i, j: (0, j)),
                  pl.BlockSpec((d, tf), lambda i, j: (0, j)), pl.BlockSpec((tf, d), lambda i, j: (j, 0)),
                  pl.BlockSpec((8, 128), lambda i, j: (0, 0))],
        out_specs=[row, row, col, col],
        out_shape=[jax.ShapeDtypeStruct((t, d), F32), jax.ShapeDtypeStruct((t, d), BF16),
                   jax.ShapeDtypeStruct((t, f), BF16), jax.ShapeDtypeStruct((t, f), BF16)],
        scratch_shapes=[pltpu.VMEM((tm, d), BF16), pltpu.VMEM((tm, d), F32)],
        compiler_params=_params("parallel", "arbitrary"))(h, g.reshape(1, d), w1, w3, w2, dep)


def _ffn_bwd_x(dh, h_in, g, z1, z3, w1, w3, w2, dep, name):
    t, d = dh.shape
    f = z1.shape[1]
    tm = min(256, t)

    def kern(dh_ref, h_ref, g_ref, z1_ref, z3_ref, w2_hbm, w1_hbm, w3_hbm, dep_ref,
             o_ref, dg_ref, dz1_ref, dz3_ref, a_ref, df_ref, w2_ref, w1_ref, w3_ref, sems):
        @pl.when(pl.program_id(0) == 0)
        def _():
            copies = [pltpu.make_async_copy(src, dst, sems.at[k]) for k, (src, dst) in
                      enumerate(((w2_hbm, w2_ref), (w1_hbm, w1_ref), (w3_hbm, w3_ref)))]
            for cp in copies:
                cp.start()
            dg_ref[...] = jnp.zeros_like(dg_ref)
            for cp in copies:
                cp.wait()

        df = _bf(FFN_RES * dh_ref[...])
        df_ref[...] = df
        da = lax.dot_general(df, w2_ref[...], (((1,), (1,)), ((), ())), preferred_element_type=F32)
        z1v, z3v = z1_ref[...].astype(F32), z3_ref[...].astype(F32)
        sig = _sigmoid(z1v)
        silu = z1v * sig
        a_ref[...] = _bf(silu * z3v)
        dz1 = _bf(da * z3v * (sig * (1.0 + z1v * (1.0 - sig))))
        dz3 = _bf(da * silu)
        dz1_ref[...] = dz1
        dz3_ref[...] = dz3
        dn = (lax.dot_general(dz1, w1_ref[...], (((1,), (1,)), ((), ())), preferred_element_type=F32)
              + lax.dot_general(dz3, w3_ref[...], (((1,), (1,)), ((), ())), preferred_element_type=F32))
        x = h_ref[...]
        rstd = _rstd(x)
        xhat = x * rstd
        dg_ref[...] += jnp.broadcast_to(jnp.sum(dn * xhat, axis=0, keepdims=True), dg_ref.shape)
        dxh = dn * g_ref[...]
        o_ref[...] = dh_ref[...] + rstd * (dxh - xhat * jnp.mean(dxh * xhat, axis=-1, keepdims=True))

    row = pl.BlockSpec((tm, d), lambda i: (i, 0))
    col = pl.BlockSpec((tm, f), lambda i: (i, 0))
    whole = pl.BlockSpec(memory_space=pl.ANY)
    colshape = jax.ShapeDtypeStruct((t, f), BF16)
    return pl.pallas_call(
        kern, name=name, grid=(t // tm,),
        in_specs=[row, row, pl.BlockSpec((1, d), lambda i: (0, 0)), col, col, whole, whole, whole,
                  pl.BlockSpec((8, 128), lambda i: (0, 0))],
        out_specs=[row, pl.BlockSpec((8, d), lambda i: (0, 0)), col, col, col, row],
        out_shape=[jax.ShapeDtypeStruct((t, d), F32), jax.ShapeDtypeStruct((8, d), F32), colshape, colshape, colshape,
                   jax.ShapeDtypeStruct((t, d), BF16)],
        scratch_shapes=[pltpu.VMEM((f, d), BF16), pltpu.VMEM((d, f), BF16), pltpu.VMEM((d, f), BF16),
                        pltpu.SemaphoreType.DMA((3,))],
        compiler_params=_params("arbitrary"))(dh, h_in, g.reshape(1, d), z1, z3, w2, w1, w3, dep)


def _ffn_bwd(dh, h_in, g, n, z1, z3, w1, w3, w2, dep, tag):
    f = w1.shape[1]
    dh_in, dg, dz1, dz3, act, df = _ffn_bwd_x(dh, h_in, g, z1, z3, w1, w3, w2, dep, tag + "_bwd_x")
    dw1 = _mm_tn(n, dz1, BF16, tag + "_dw1", bn=_ffn_tile(f))
    dw3 = _mm_tn(n, dz3, BF16, tag + "_dw3", bn=_ffn_tile(f))
    dw2 = _mm_tn(act, df, BF16, tag + "_dw2", bm=_ffn_tile(f))
    return dh_in, dg, dw1, dw3, dw2


def _norm_fwd(h, g, dep, name):
    t, d = h.shape
    tm = min(512, t)

    def kern(h_ref, g_ref, dep_ref, o_ref):
        x = h_ref[...]
        o_ref[...] = _bf(x * _rstd(x) * g_ref[...])

    row = pl.BlockSpec((tm, d), lambda i: (i, 0))
    return pl.pallas_call(
        kern, name=name, grid=(t // tm,),
        in_specs=[row, pl.BlockSpec((1, d), lambda i: (0, 0)), pl.BlockSpec((8, 128), lambda i: (0, 0))],
        out_specs=row, out_shape=jax.ShapeDtypeStruct((t, d), BF16),
        compiler_params=_params("parallel"))(h, g.reshape(1, d), dep)


def _loss_bwd(h, target, g):
    t, d = h.shape
    tm = min(512, t)

    def kern(h_ref, t_ref, g_ref, dh_ref, dg_ref, loss_ref):
        @pl.when(pl.program_id(0) == 0)
        def _():
            dg_ref[...] = jnp.zeros_like(dg_ref)
            loss_ref[...] = jnp.zeros_like(loss_ref)

        x = h_ref[...]
        rstd = _rstd(x)
        xhat = x * rstd
        err = xhat * g_ref[...] - t_ref[...]
        row_loss = jnp.sum(err * err, axis=-1, keepdims=True) * (0.5 / d)
        loss_ref[...] += jnp.broadcast_to(jnp.sum(row_loss, axis=0, keepdims=True), loss_ref.shape)
        dy = err * (1.0 / d)
        dg_ref[...] += jnp.broadcast_to(jnp.sum(dy * xhat, axis=0, keepdims=True), dg_ref.shape)
        dxh = dy * g_ref[...]
        dh_ref[...] = rstd * (dxh - xhat * jnp.mean(dxh * xhat, axis=-1, keepdims=True))

    row = pl.BlockSpec((tm, d), lambda i: (i, 0))
    return pl.pallas_call(
        kern, name="loss_bwd", grid=(t // tm,),
        in_specs=[row, row, pl.BlockSpec((1, d), lambda i: (0, 0))],
        out_specs=[row, pl.BlockSpec((8, d), lambda i: (0, 0)), pl.BlockSpec((8, 128), lambda i: (0, 0))],
        out_shape=[jax.ShapeDtypeStruct((t, d), F32), jax.ShapeDtypeStruct((8, d), F32),
                   jax.ShapeDtypeStruct((8, 128), F32)],
        compiler_params=_params("arbitrary"))(h, target, g.reshape(1, d))


def _glu_fwd(m, wa, wb):
    t, d = m.shape
    c = wa.shape[1]
    tm, tc = min(512, t), min(512, c)

    def kern(m_ref, wa_ref, wb_ref, a_ref, b_ref, glu_ref):
        mv = m_ref[...]
        a = jnp.dot(mv, wa_ref[...], preferred_element_type=F32)
        b = jnp.dot(mv, wb_ref[...], preferred_element_type=F32)
        a_ref[...] = _bf(a)
        b_ref[...] = _bf(b)
        glu_ref[...] = _bf(a * _sigmoid(b))

    col = pl.BlockSpec((tm, tc), lambda i, j: (i, j))
    wspec = pl.BlockSpec((d, tc), lambda i, j: (0, j))
    shape = jax.ShapeDtypeStruct((t, c), BF16)
    return pl.pallas_call(
        kern, name="conv_glu_fwd", grid=(t // tm, c // tc),
        in_specs=[pl.BlockSpec((tm, d), lambda i, j: (i, 0)), wspec, wspec], out_specs=[col, col, col],
        out_shape=[shape, shape, shape], compiler_params=_params("parallel", "parallel"))(m, wa, wb)


def _conv_tile(t):
    return min(256, t)


def _shift_copies(ext, shifted, rows):
    for s in range(8):
        shifted[s] = ext[pl.ds(s, rows), :]


def _shifted_rows(shifted, start, nrows):
    return shifted[start % 8, pl.ds(start - start % 8, nrows), :]


def _conv_fwd(glu, w_dw, g):
    t, c = glu.shape
    tm = _conv_tile(t)
    hb = tm // CONV_HALO

    def kern(cur_ref, halo_ref, w_ref, g_ref, cv_ref, s_ref, ext, shifted):
        i = pl.program_id(0)
        ext[0:CONV_HALO, :] = jnp.where(i > 0, halo_ref[...].astype(F32), 0.0)
        ext[CONV_HALO:tm + CONV_HALO, :] = cur_ref[...].astype(F32)
        ext[tm + CONV_HALO:, :] = jnp.zeros((8, c), F32)
        _shift_copies(ext, shifted, tm + CONV_HALO)
        gv = g_ref[...]
        for r0 in range(0, tm, CONV_ROWS):
            acc = jnp.zeros((CONV_ROWS, c), F32)
            for k in range(CONV_WIDTH):
                acc = acc + _shifted_rows(shifted, r0 + 2 + k, CONV_ROWS) * w_ref[k:k + 1, :]
            cv_ref[r0:r0 + CONV_ROWS, :] = acc
            rn = acc * _rstd(acc) * gv
            s_ref[r0:r0 + CONV_ROWS, :] = _bf(rn * _sigmoid(rn))

    row = pl.BlockSpec((tm, c), lambda i: (i, 0))
    return pl.pallas_call(
        kern, name="conv_fwd", grid=(t // tm,),
        in_specs=[row, pl.BlockSpec((CONV_HALO, c), lambda i: (jnp.maximum(i * hb - 1, 0), 0)),
                  pl.BlockSpec((CONV_HALO, c), lambda i: (0, 0)), pl.BlockSpec((1, c), lambda i: (0, 0))],
        out_specs=[row, row],
        out_shape=[jax.ShapeDtypeStruct((t, c), F32), jax.ShapeDtypeStruct((t, c), BF16)],
        scratch_shapes=[pltpu.VMEM((tm + CONV_HALO + 8, c), F32), pltpu.VMEM((8, tm + CONV_HALO, c), F32)],
        compiler_params=_params("parallel"))(glu, glu, w_dw, g.reshape(1, c))


def _conv_bwd_norm(dh, cv, w_pw2, g, dep):
    t, c = cv.shape
    tm = min(512, t)

    def kern(dh_ref, cv_ref, w_ref, g_ref, dep_ref, dcv_ref, dg_ref):
        @pl.when(pl.program_id(0) == 0)
        def _():
            dg_ref[...] = jnp.zeros_like(dg_ref)

        ds = lax.dot_general(_bf(dh_ref[...]), w_ref[...], (((1,), (1,)), ((), ())), preferred_element_type=F32)
        x = cv_ref[...]
        rstd = _rstd(x)
        xhat = x * rstd
        rn = xhat * g_ref[...]
        sig = _sigmoid(rn)
        drn = ds * (sig * (1.0 + rn * (1.0 - sig)))
        dg_ref[...] += jnp.broadcast_to(jnp.sum(drn * xhat, axis=0, keepdims=True), dg_ref.shape)
        dxh = drn * g_ref[...]
        dcv_ref[...] = rstd * (dxh - xhat * jnp.mean(dxh * xhat, axis=-1, keepdims=True))

    row = pl.BlockSpec((tm, c), lambda i: (i, 0))
    return pl.pallas_call(
        kern, name="conv_bwd_norm", grid=(t // tm,),
        in_specs=[pl.BlockSpec((tm, dh.shape[1]), lambda i: (i, 0)), row,
                  pl.BlockSpec(w_pw2.shape, lambda i: (0, 0)), pl.BlockSpec((1, c), lambda i: (0, 0)),
                  pl.BlockSpec((8, 128), lambda i: (0, 0))],
        out_specs=[row, pl.BlockSpec((8, c), lambda i: (0, 0))],
        out_shape=[jax.ShapeDtypeStruct((t, c), F32), jax.ShapeDtypeStruct((8, c), F32)],
        compiler_params=_params("arbitrary"))(dh, cv, w_pw2, g.reshape(1, c), dep)


def _conv_bwd_dw(dcv, glu, a, b, w_dw):
    t, c = dcv.shape
    tm = _conv_tile(t)
    hb = tm // CONV_HALO
    last = t // CONV_HALO - 1

    def kern(dcv_ref, dnext_ref, glu_ref, gprev_ref, a_ref, b_ref, w_ref, da_ref, db_ref, dw_ref,
             dext, gext, dshift, gshift):
        i = pl.program_id(0)

        @pl.when(i == 0)
        def _():
            dw_ref[...] = jnp.zeros_like(dw_ref)

        dext[0:tm, :] = dcv_ref[...]
        dext[tm:tm + CONV_HALO, :] = jnp.where(i < t // tm - 1, dnext_ref[...], 0.0)
        dext[tm + CONV_HALO:, :] = jnp.zeros((8, c), F32)
        gext[0:CONV_HALO, :] = jnp.where(i > 0, gprev_ref[...].astype(F32), 0.0)
        gext[CONV_HALO:tm + CONV_HALO, :] = glu_ref[...].astype(F32)
        gext[tm + CONV_HALO:, :] = jnp.zeros((8, c), F32)
        _shift_copies(dext, dshift, tm + CONV_HALO)
        _shift_copies(gext, gshift, tm + CONV_HALO)
        for r0 in range(0, tm, CONV_ROWS):
            acc = jnp.zeros((CONV_ROWS, c), F32)
            for k in range(CONV_WIDTH):
                acc = acc + _shifted_rows(dshift, r0 + CONV_WIDTH - 1 - k, CONV_ROWS) * w_ref[k:k + 1, :]
            av = a_ref[r0:r0 + CONV_ROWS, :].astype(F32)
            sig = _sigmoid(b_ref[r0:r0 + CONV_ROWS, :].astype(F32))
            da_ref[r0:r0 + CONV_ROWS, :] = _bf(acc * sig)
            db_ref[r0:r0 + CONV_ROWS, :] = _bf(acc * av * sig * (1.0 - sig))
        for k in range(CONV_WIDTH):
            acc = jnp.zeros((CONV_ROWS, c), F32)
            for r0 in range(0, tm, CONV_ROWS):
                acc = acc + _shifted_rows(gshift, r0 + 2 + k, CONV_ROWS) * dext[r0:r0 + CONV_ROWS, :]
            dw_ref[k:k + 1, :] += jnp.sum(acc, axis=0, keepdims=True)

    row = pl.BlockSpec((tm, c), lambda i: (i, 0))
    shape = jax.ShapeDtypeStruct((t, c), BF16)
    return pl.pallas_call(
        kern, name="conv_bwd_dw", grid=(t // tm,),
        in_specs=[row, pl.BlockSpec((CONV_HALO, c), lambda i: (jnp.minimum((i + 1) * hb, last), 0)),
                  row, pl.BlockSpec((CONV_HALO, c), lambda i: (jnp.maximum(i * hb - 1, 0), 0)),
                  row, row, pl.BlockSpec((CONV_HALO, c), lambda i: (0, 0))],
        out_specs=[row, row, pl.BlockSpec((CONV_HALO, c), lambda i: (0, 0))],
        out_shape=[shape, shape, jax.ShapeDtypeStruct((CONV_HALO, c), F32)],
        scratch_shapes=[pltpu.VMEM((tm + CONV_HALO + 8, c), F32), pltpu.VMEM((tm + CONV_HALO + 8, c), F32),
                        pltpu.VMEM((8, tm + CONV_HALO, c), F32), pltpu.VMEM((8, tm + CONV_HALO, c), F32)],
        compiler_params=_params("arbitrary"))(dcv, dcv, glu, glu, a, b, w_dw)


def _rope_tables(pos):
    t = pos.shape[0]
    tm = min(512, t)
    freq = (np.float32(ROPE_THETA) ** (np.float32(-2.0) * np.arange(ROPE // 2, dtype=np.float32)
                                       / np.float32(ROPE))).astype(np.float32)
    row = np.zeros((2, 128), np.float32)
    row[0, :ROPE] = np.concatenate([freq, freq])
    row[1, :ROPE] = 1.0

    def kern(pos_ref, f_ref, c_ref, s_ref):
        ang = pos_ref[...].astype(F32) * f_ref[0:1, :]
        mask = f_ref[1:2, :]
        c_ref[...] = jnp.cos(ang) * mask
        s_ref[...] = jnp.sin(ang) * mask

    out = pl.BlockSpec((tm, 128), lambda i: (i, 0))
    shape = jax.ShapeDtypeStruct((t, 128), F32)
    return pl.pallas_call(
        kern, name="rope_tables", grid=(t // tm,),
        in_specs=[pl.BlockSpec((tm, 1), lambda i: (i, 0)), pl.BlockSpec((2, 128), lambda i: (0, 0))],
        out_specs=[out, out], out_shape=[shape, shape], compiler_params=_params("parallel"))(pos, jnp.asarray(row))


def _mla_prep(a, gq, gkv, cs_c, cs_s):
    t = a.shape[0]
    tm = min(512, t)
    kv0, r0 = Q_LORA, Q_LORA + KV_LORA

    def kern(a_ref, gq_ref, gkv_ref, c_ref, s_ref, cq_ref, ckv_ref, kr_ref):
        aq = a_ref[:, 0:kv0]
        akv = a_ref[:, kv0:r0]
        ar = a_ref[:, r0:A_PAD]
        cq_ref[...] = _bf(aq * _rstd(aq) * gq_ref[...])
        ckv_ref[...] = _bf(akv * _rstd(akv) * gkv_ref[...])
        kr_ref[...] = _bf(ar * c_ref[...] + _rot(ar) * s_ref[...])

    def row(w):
        return pl.BlockSpec((tm, w), lambda i: (i, 0))

    def vec(w):
        return pl.BlockSpec((1, w), lambda i: (0, 0))

    return pl.pallas_call(
        kern, name="mla_prep", grid=(t // tm,),
        in_specs=[row(A_PAD), vec(Q_LORA), vec(KV_LORA), row(128), row(128)],
        out_specs=[row(Q_LORA), row(KV_LORA), row(128)],
        out_shape=[jax.ShapeDtypeStruct((t, Q_LORA), BF16), jax.ShapeDtypeStruct((t, KV_LORA), BF16),
                   jax.ShapeDtypeStruct((t, 128), BF16)],
        compiler_params=_params("parallel"))(a, gq.reshape(1, -1), gkv.reshape(1, -1), cs_c, cs_s)


def _mla_prep_bwd(a, dcq, dckv, dar, gq, gkv):
    t = a.shape[0]
    tm = min(512, t)
    kv0, r0 = Q_LORA, Q_LORA + KV_LORA

    def kern(a_ref, dcq_ref, dckv_ref, dar_ref, gq_ref, gkv_ref, da_ref, dgq_ref, dgkv_ref):
        @pl.when(pl.program_id(0) == 0)
        def _():
            dgq_ref[...] = jnp.zeros_like(dgq_ref)
            dgkv_ref[...] = jnp.zeros_like(dgkv_ref)

        def back(x, dy, g_ref, dg_ref):
            rstd = _rstd(x)
            xhat = x * rstd
            dg_ref[...] += jnp.broadcast_to(jnp.sum(dy * xhat, axis=0, keepdims=True), dg_ref.shape)
            dxh = dy * g_ref[...]
            return rstd * (dxh - xhat * jnp.mean(dxh * xhat, axis=-1, keepdims=True))

        da_ref[:, 0:kv0] = _bf(back(a_ref[:, 0:kv0], dcq_ref[...], gq_ref, dgq_ref))
        da_ref[:, kv0:r0] = _bf(back(a_ref[:, kv0:r0], dckv_ref[...], gkv_ref, dgkv_ref))
        da_ref[:, r0:A_PAD] = _bf(dar_ref[...])

    def row(w):
        return pl.BlockSpec((tm, w), lambda i: (i, 0))

    def vec(r, w):
        return pl.BlockSpec((r, w), lambda i: (0, 0))

    return pl.pallas_call(
        kern, name="mla_prep_bwd", grid=(t // tm,),
        in_specs=[row(A_PAD), row(Q_LORA), row(KV_LORA), row(128), vec(1, Q_LORA), vec(1, KV_LORA)],
        out_specs=[row(A_PAD), vec(8, Q_LORA), vec(8, KV_LORA)],
        out_shape=[jax.ShapeDtypeStruct((t, A_PAD), BF16), jax.ShapeDtypeStruct((8, Q_LORA), F32),
                   jax.ShapeDtypeStruct((8, KV_LORA), F32)],
        compiler_params=_params("arbitrary"))(a, dcq, dckv, dar, gq.reshape(1, -1), gkv.reshape(1, -1))


def _mla_qkv(cq, ckv, kr, cs_c, cs_s, wuq, wukv):
    t = cq.shape[0]
    tm = min(512, t)

    def kern(cq_ref, ckv_ref, kr_ref, c_ref, s_ref, wq_ref, wkv_ref, q_ref, k_ref, v_ref):
        r = jnp.dot(cq_ref[...], wq_ref[...], preferred_element_type=F32)
        xr = r[:, NOPE:]
        q_ref[:, 0:NOPE] = _bf(r[:, 0:NOPE] * ATTN_SCALE)
        q_ref[:, NOPE:] = _bf((xr * c_ref[...] + _rot(xr) * s_ref[...]) * ATTN_SCALE)
        kv = jnp.dot(ckv_ref[...], wkv_ref[...], preferred_element_type=F32)
        k_ref[:, 0:NOPE] = _bf(kv[:, 0:NOPE])
        k_ref[:, NOPE:] = kr_ref[...]
        v_ref[...] = _bf(kv[:, NOPE:])

    def row(w):
        return pl.BlockSpec((tm, w), lambda i, h: (i, 0))

    def head(w):
        return pl.BlockSpec((None, tm, w), lambda i, h: (h, i, 0))

    return pl.pallas_call(
        kern, name="mla_qkv", grid=(t // tm, HEADS),
        in_specs=[row(Q_LORA), row(KV_LORA), row(128), row(128), row(128),
                  pl.BlockSpec((None, Q_LORA, HEAD_PAD), lambda i, h: (h, 0, 0)),
                  pl.BlockSpec((None, KV_LORA, NOPE + V_HEAD), lambda i, h: (h, 0, 0))],
        out_specs=[head(HEAD_PAD), head(HEAD_PAD), head(V_HEAD)],
        out_shape=[jax.ShapeDtypeStruct((HEADS, t, HEAD_PAD), BF16), jax.ShapeDtypeStruct((HEADS, t, HEAD_PAD), BF16),
                   jax.ShapeDtypeStruct((HEADS, t, V_HEAD), BF16)],
        compiler_params=_params("parallel", "arbitrary"))(cq, ckv, kr, cs_c, cs_s, wuq, wukv)


def _mla_qkv_bwd(dq, dk, dv, cs_c, cs_s, wuq, wukv):
    t = dq.shape[1]
    tm = min(512, t)

    def kern(dq_ref, dk_ref, dv_ref, c_ref, s_ref, wq_ref, wkv_ref, dr_ref, dkv_ref, dcq_ref, dckv_ref, dar_ref):
        @pl.when(pl.program_id(1) == 0)
        def _():
            dcq_ref[...] = jnp.zeros_like(dcq_ref)
            dckv_ref[...] = jnp.zeros_like(dckv_ref)
            dar_ref[...] = jnp.zeros_like(dar_ref)

        cv, sv = c_ref[...], s_ref[...]
        dqx = dq_ref[:, NOPE:]
        dr_ref[:, 0:NOPE] = _bf(dq_ref[:, 0:NOPE] * ATTN_SCALE)
        dr_ref[:, NOPE:] = _bf((dqx * cv + _rot_t(dqx * sv)) * ATTN_SCALE)
        dcq_ref[...] += lax.dot_general(dr_ref[...], wq_ref[...], (((1,), (1,)), ((), ())),
                                        preferred_element_type=F32)
        dkx = dk_ref[:, NOPE:]
        dar_ref[...] += dkx * cv + _rot_t(dkx * sv)
        dkv_ref[:, 0:NOPE] = _bf(dk_ref[:, 0:NOPE])
        dkv_ref[:, NOPE:] = _bf(dv_ref[...])
        dckv_ref[...] += lax.dot_general(dkv_ref[...], wkv_ref[...], (((1,), (1,)), ((), ())),
                                         preferred_element_type=F32)

    def row(w):
        return pl.BlockSpec((tm, w), lambda i, h: (i, 0))

    def head(w):
        return pl.BlockSpec((None, tm, w), lambda i, h: (h, i, 0))

    return pl.pallas_call(
        kern, name="mla_qkv_bwd", grid=(t // tm, HEADS),
        in_specs=[head(HEAD_PAD), head(HEAD_PAD), head(V_HEAD), row(128), row(128),
                  pl.BlockSpec((None, Q_LORA, HEAD_PAD), lambda i, h: (h, 0, 0)),
                  pl.BlockSpec((None, KV_LORA, NOPE + V_HEAD), lambda i, h: (h, 0, 0))],
        out_specs=[pl.BlockSpec((tm, HEAD_PAD), lambda i, h: (i, h)),
                   pl.BlockSpec((tm, NOPE + V_HEAD), lambda i, h: (i, h)), row(Q_LORA), row(KV_LORA), row(128)],
        out_shape=[jax.ShapeDtypeStruct((t, HEADS * HEAD_PAD), BF16),
                   jax.ShapeDtypeStruct((t, HEADS * (NOPE + V_HEAD)), BF16),
                   jax.ShapeDtypeStruct((t, Q_LORA), F32), jax.ShapeDtypeStruct((t, KV_LORA), F32),
                   jax.ShapeDtypeStruct((t, 128), F32)],
        compiler_params=_params("parallel", "arbitrary"))(dq, dk, dv, cs_c, cs_s, wuq, wukv)


def _attn_block(t):
    return 512 if t >= 4096 else 128


def _fold_rows(x, op):
    r = x.shape[0]
    while r > 8:
        r //= 2
        x = op(x[:r], x[r:])
    return x


def _chunk_mask(bk, bq):
    kc = lax.broadcasted_iota(jnp.int32, (bk, bq), 0) // CHUNK
    qc = lax.broadcasted_iota(jnp.int32, (bk, bq), 1) // CHUNK
    return qc >= kc


def _flash_fwd(q, k, v):
    t = q.shape[1]
    bq = _attn_block(t)
    nq = t // bq

    def kern(q_ref, k_ref, v_ref, o_ref, lse_ref):
        i = pl.program_id(1)
        qa, qb = q_ref[0:bq, :], q_ref[bq:2 * bq, :]

        def block(j):
            rows = pl.ds(pl.multiple_of(j * bq, bq), bq)
            return k_ref[rows, :], v_ref[rows, :]

        def scores(kj, qv):
            return lax.dot_general(kj, qv, (((1,), (1,)), ((), ())), preferred_element_type=F32)

        def update(st, vj, m, l, acc):
            m_new = jnp.maximum(m, jnp.max(_fold_rows(st, jnp.maximum), axis=0, keepdims=True))
            alpha = jnp.exp(m - m_new)
            p = jnp.exp(st - m_new)
            pv = lax.dot_general(vj, _bf(p), (((0,), (0,)), ((), ())), preferred_element_type=F32)
            return m_new, alpha * l + jnp.sum(_fold_rows(p, jnp.add), axis=0, keepdims=True), alpha * acc + pv

        start = (jnp.full((1, bq), -1e30, F32), jnp.zeros((1, bq), F32), jnp.zeros((V_HEAD, bq), F32))
        mask = _chunk_mask(bq, bq)
        k0, v0 = block(2 * i)
        k1, v1 = block(2 * i + 1)
        ca = update(jnp.where(mask, scores(k0, qa), -1e30), v0, *start)
        cb = update(scores(k0, qb), v0, *start)
        cb = update(jnp.where(mask, scores(k1, qb), -1e30), v1, *cb)

        def body(j, carry):
            kj, vj = block(j)
            return (update(scores(kj, qa), vj, *carry[0]), update(scores(kj, qb), vj, *carry[1]))

        ca, cb = lax.fori_loop(0, 2 * i, body, (ca, cb))
        for half, (m, l, acc) in enumerate((ca, cb)):
            o_ref[half * bq:(half + 1) * bq, :] = _bf((acc / l).T)
            lse_ref[half] = jnp.broadcast_to(m + jnp.log(l), (8, bq))

    return pl.pallas_call(
        kern, name="flash_fwd", grid=(HEADS, nq // 2),
        in_specs=[pl.BlockSpec((None, 2 * bq, HEAD_PAD), lambda h, i: (h, i, 0)),
                  pl.BlockSpec((None, t, HEAD_PAD), lambda h, i: (h, 0, 0)),
                  pl.BlockSpec((None, t, V_HEAD), lambda h, i: (h, 0, 0))],
        out_specs=[pl.BlockSpec((2 * bq, V_HEAD), lambda h, i: (i, h)),
                   pl.BlockSpec((None, 2, 8, bq), lambda h, i: (h, i, 0, 0))],
        out_shape=[jax.ShapeDtypeStruct((t, HEADS * V_HEAD), BF16), jax.ShapeDtypeStruct((HEADS, nq, 8, bq), F32)],
        compiler_params=_params("parallel", "arbitrary"))(q, k, v)


def _attn_delta(do, o):
    t = do.shape[0]
    bq = _attn_block(t)

    def kern(do_ref, o_ref, d_ref):
        for h in range(HEADS):
            cols = slice(h * V_HEAD, (h + 1) * V_HEAD)
            prod = do_ref[:, cols].astype(F32) * o_ref[:, cols].astype(F32)
            d_ref[h] = jnp.broadcast_to(jnp.sum(prod.T, axis=0, keepdims=True), (8, bq))

    blk = pl.BlockSpec((bq, HEADS * V_HEAD), lambda i: (i, 0))
    return pl.pallas_call(
        kern, name="attn_delta", grid=(t // bq,), in_specs=[blk, blk],
        out_specs=pl.BlockSpec((HEADS, None, 8, bq), lambda i: (0, i, 0, 0)),
        out_shape=jax.ShapeDtypeStruct((HEADS, t // bq, 8, bq), F32),
        compiler_params=_params("parallel"))(do, o)


def _flash_bwd(q, k, v, do, lse, delta):
    t = q.shape[1]
    bq = _attn_block(t)
    nq = t // bq

    def kern(q_ref, k_ref, v_ref, do_ref, lse_ref, del_ref, dq_ref, dk_ref, dv_ref):
        j = pl.program_id(1)

        @pl.when(j == 0)
        def _():
            dq_ref[...] = jnp.zeros_like(dq_ref)

        dk_ref[...] = jnp.zeros_like(dk_ref)
        dv_ref[...] = jnp.zeros_like(dv_ref)
        kj, vj = k_ref[...], v_ref[...]

        def step(i, masked):
            rows = pl.ds(pl.multiple_of(i * bq, bq), bq)
            qi, doi = q_ref[rows, :], do_ref[rows, :]
            st = lax.dot_general(kj, qi, (((1,), (1,)), ((), ())), preferred_element_type=F32)
            pt = jnp.exp(st - lse_ref[i][0:1, :])
            if masked:
                pt = jnp.where(_chunk_mask(bq, bq), pt, 0.0)
            dpt = lax.dot_general(vj, doi, (((1,), (1,)), ((), ())), preferred_element_type=F32)
            dst = _bf(pt * (dpt - del_ref[i][0:1, :]))
            dv_ref[...] += jnp.dot(_bf(pt), doi, preferred_element_type=F32)
            dk_ref[...] += jnp.dot(dst, qi, preferred_element_type=F32)
            dq_ref[rows, :] += lax.dot_general(dst, kj, (((0,), (0,)), ((), ())), preferred_element_type=F32)

        step(j, True)

        def body(i, carry):
            step(i, False)
            return carry

        lax.fori_loop(j + 1, nq, body, 0)

    stat = pl.BlockSpec((None, nq, 8, bq), lambda h, j: (h, 0, 0, 0))
    return pl.pallas_call(
        kern, name="flash_bwd", grid=(HEADS, nq),
        in_specs=[pl.BlockSpec((None, t, HEAD_PAD), lambda h, j: (h, 0, 0)),
                  pl.BlockSpec((None, bq, HEAD_PAD), lambda h, j: (h, j, 0)),
                  pl.BlockSpec((None, bq, V_HEAD), lambda h, j: (h, j, 0)),
                  pl.BlockSpec((t, V_HEAD), lambda h, j: (0, h)), stat, stat],
        out_specs=[pl.BlockSpec((None, t, HEAD_PAD), lambda h, j: (h, 0, 0)),
                   pl.BlockSpec((None, bq, HEAD_PAD), lambda h, j: (h, j, 0)),
                   pl.BlockSpec((None, bq, V_HEAD), lambda h, j: (h, j, 0))],
        out_shape=[jax.ShapeDtypeStruct((HEADS, t, HEAD_PAD), F32), jax.ShapeDtypeStruct((HEADS, t, HEAD_PAD), F32),
                   jax.ShapeDtypeStruct((HEADS, t, V_HEAD), F32)],
        compiler_params=_params("parallel", "arbitrary"))(q, k, v, do, lse, delta)


def _place():
    x, y, c = lax.axis_index("x"), lax.axis_index("y"), lax.axis_index("c")
    return x, y, c, [(1 - x, y), (x, 1 - y), (1 - x, 1 - y)]


def _all_gather_rows(block, name):
    m_per, n = block.shape

    def body(x_ref, out_ref, send_sems, recv_sems, local_sem):
        x, y, c, chips = _place()
        me, sibling = (x, y, c), (x, y, 1 - c)

        def rows(px, py, pc):
            return out_ref.at[pl.ds((4 * px + 2 * py + pc) * m_per, m_per), :]

        def copy(k, blk, to, src=None):
            return pltpu.make_async_remote_copy(
                src_ref=rows(*blk) if src is None else src, dst_ref=rows(*blk), send_sem=send_sems.at[k],
                recv_sem=recv_sems.at[k], device_id=to, device_id_type=MESH)

        mine = pltpu.make_async_copy(x_ref, rows(*me), local_sem)
        mine.start()
        first = [copy(0, me, sibling, src=x_ref)]
        first += [copy(1 + j, me, (*chip, c), src=x_ref) for j, chip in enumerate(chips)]
        for cp in first:
            cp.start()
        passed = [copy(4 + j, (*chip, c), sibling) for j, chip in enumerate(chips)]
        for j, chip in enumerate(chips):
            copy(1 + j, (*chip, c), me).wait_recv()
            passed[j].start()
        copy(0, sibling, me).wait_recv()
        for j, chip in enumerate(chips):
            copy(4 + j, (*chip, 1 - c), me).wait_recv()
        for cp in first + passed:
            cp.wait_send()
        mine.wait()

    return pl.pallas_call(
        body, name=name, out_shape=jax.ShapeDtypeStruct((8 * m_per, n), block.dtype),
        in_specs=[pl.BlockSpec(memory_space=pltpu.VMEM)], out_specs=pl.BlockSpec(memory_space=pltpu.VMEM),
        scratch_shapes=[pltpu.SemaphoreType.DMA((7,)), pltpu.SemaphoreType.DMA((7,)), pltpu.SemaphoreType.DMA],
        compiler_params=pltpu.CompilerParams(vmem_limit_bytes=VMEM_LIMIT_BYTES))(block)


HBM_SPEC = pl.BlockSpec(memory_space=pltpu.HBM)
SEM_SPEC = pl.BlockSpec(memory_space=pltpu.SEMAPHORE)
DATAFLOW = pltpu.SideEffectType.DATAFLOW_SIDE_EFFECTING


def _in_hbm(a):
    return pltpu.with_memory_space_constraint(a, pltpu.HBM)


def _chip_copies(ins, lands, send_sems, recv_sems, src_slot):
    n = len(ins)
    x, y, c, chips = _place()
    me = 2 * x + y
    return [pltpu.make_async_remote_copy(
        src_ref=ins[w].at[2 * chip[0] + chip[1]] if src_slot else ins[w], dst_ref=lands[w].at[me],
        send_sem=send_sems.at[p * n + w], recv_sem=recv_sems.at[p * n + w], device_id=(*chip, c),
        device_id_type=MESH) for w in range(n) for p, chip in enumerate(chips)]


def _exchange_start(srcs, lands, src_slot, name, dep=None):
    n = len(srcs)
    first_out = 2 * n + (dep is not None)

    def body(*refs):
        for cp in _chip_copies(refs[:n], refs[n:2 * n], refs[first_out], refs[first_out + 1], src_slot):
            cp.start()
        token = refs[-1]
        token[...] = jnp.zeros_like(token)

    thru = [pltpu.HBM(a.shape, a.dtype) for a in list(srcs) + list(lands)]
    res = pl.pallas_call(
        body, name=name,
        out_shape=(pltpu.SemaphoreType.DMA((3 * n,)), pltpu.SemaphoreType.DMA((3 * n,)), *thru,
                   jax.ShapeDtypeStruct((8, 128), F32)),
        in_specs=[HBM_SPEC] * (2 * n) + ([pl.BlockSpec(memory_space=pl.ANY)] if dep is not None else []),
        out_specs=(SEM_SPEC, SEM_SPEC, *[HBM_SPEC] * (2 * n), pl.BlockSpec(memory_space=pltpu.VMEM)),
        input_output_aliases={i: 2 + i for i in range(2 * n)},
        compiler_params=pltpu.CompilerParams(has_side_effects=DATAFLOW))(
            *[_in_hbm(a) for a in srcs], *[_in_hbm(a) for a in lands], *([dep] if dep is not None else []))
    return (res[0], res[1], list(res[2:2 + n]), list(res[2 + n:2 + 2 * n])), res[-1]


def _exchange_wait(flight, after, src_slot, name):
    send_sems, recv_sems, srcs, lands = flight
    n = len(srcs)

    def body(*refs):
        for cp in _chip_copies(refs[:n], refs[n:2 * n], refs[2 * n], refs[2 * n + 1], src_slot):
            cp.wait_send()
            cp.wait_recv()

    thru = [pltpu.HBM(a.shape, a.dtype) for a in list(srcs) + list(lands)]
    res = pl.pallas_call(
        body, name=name, out_shape=thru,
        in_specs=[HBM_SPEC] * (2 * n) + [SEM_SPEC, SEM_SPEC, pl.BlockSpec(memory_space=pl.ANY)],
        out_specs=[HBM_SPEC] * (2 * n), input_output_aliases={i: i for i in range(2 * n)},
        compiler_params=pltpu.CompilerParams(has_side_effects=DATAFLOW))(*srcs, *lands, send_sems, recv_sems, after)
    return list(res[n:])


def _landing(own, me):
    return lax.dynamic_update_index_in_dim(lax.empty((4, *own.shape), own.dtype), own, me, 0)


def _swap_with_sibling(arrays, name):
    n = len(arrays)

    def body(*refs):
        ins, outs = refs[:n], refs[n:2 * n]
        send_sems, recv_sems = refs[2 * n:]
        x, y, c, _ = _place()
        copies = [pltpu.make_async_remote_copy(src_ref=ins[w], dst_ref=outs[w], send_sem=send_sems.at[w],
                                               recv_sem=recv_sems.at[w], device_id=(x, y, 1 - c), device_id_type=MESH)
                  for w in range(n)]
        for cp in copies:
            cp.start()
        for cp in copies:
            cp.wait()

    any_spec = pl.BlockSpec(memory_space=pl.ANY)
    return pl.pallas_call(
        body, name=name, out_shape=[jax.ShapeDtypeStruct(a.shape, a.dtype) for a in arrays],
        in_specs=[any_spec] * n, out_specs=[any_spec] * n,
        scratch_shapes=[pltpu.SemaphoreType.DMA((n,)), pltpu.SemaphoreType.DMA((n,))])(*arrays)


def _as_rows(a):
    return a.reshape(-1, a.shape[-1])


def _row_tile(r, c, budget_bytes=1 << 20):
    tr = r
    while tr % 16 == 0 and tr * c * 4 > budget_bytes:
        tr //= 2
    return tr


def _sum_slots(layers, nlayer, name, into=None):
    _, r, c = layers[0][1].shape
    tr = _row_tile(r, c)
    nt = r // tr
    acc = into
    for l, r4 in layers:
        def kern(r_ref, *rest):
            o_ref = rest[-1]
            o_ref[...] = (((r_ref[0].astype(F32) + r_ref[1].astype(F32)) + r_ref[2].astype(F32))
                          + r_ref[3].astype(F32))

        out_spec = pl.BlockSpec((tr, c), lambda i, l=l: (l * nt + i, 0))
        first = acc is None
        acc = pl.pallas_call(
            kern, name=f"{name}_l{l}", grid=(nt,),
            in_specs=[pl.BlockSpec((4, tr, c), lambda i: (0, i, 0))]
            + ([] if first else [pl.BlockSpec(memory_space=pl.ANY)]),
            out_specs=out_spec, out_shape=jax.ShapeDtypeStruct((nlayer * r, c), F32),
            input_output_aliases={} if first else {1: 0},
            compiler_params=_params("parallel"))(*([r4] if first else [r4, acc]))
    return acc


def _adamw(w, m, v, parts, name):
    r, c = w.shape
    tr = _row_tile(r, c, 1 << 19)
    npart = len(parts)
    c1 = 1.0 - ADAM_B1 ** ADAM_STEP
    c2 = 1.0 - ADAM_B2 ** ADAM_STEP

    def kern(*refs):
        w_ref, m_ref, v_ref = refs[:3]
        p_refs = refs[3:3 + npart]
        g_ref, d_ref, mo_ref, vo_ref = refs[3 + npart:]
        g = p_refs[0][...]
        for p in p_refs[1:]:
            g = g + p[...]
        mn = ADAM_B1 * m_ref[...] + (1.0 - ADAM_B1) * g
        vn = ADAM_B2 * v_ref[...] + (1.0 - ADAM_B2) * (g * g)
        g_ref[...] = g
        mo_ref[...] = mn
        vo_ref[...] = vn
        d_ref[...] = -ADAM_LR * ((mn / c1) / (jnp.sqrt(vn / c2) + ADAM_EPS) + ADAM_WD * w_ref[...])

    blk = pl.BlockSpec((tr, c), lambda i: (i, 0))
    shape = jax.ShapeDtypeStruct((r, c), F32)
    return pl.pallas_call(
        kern, name=name, grid=(r // tr,), in_specs=[blk] * (3 + npart), out_specs=[blk] * 4, out_shape=[shape] * 4,
        compiler_params=_params("parallel"))(w, m, v, *parts)


def _sum_devices(g8, name):
    _, r, c = g8.shape

    def kern(g_ref, o_ref):
        tot = g_ref[0]
        for dev in range(1, 8):
            tot = tot + g_ref[dev]
        o_ref[...] = tot

    return pl.pallas_call(
        kern, name=name, grid=(1,), in_specs=[pl.BlockSpec((8, r, c), lambda i: (0, 0, 0))],
        out_specs=pl.BlockSpec((r, c), lambda i: (0, 0)), out_shape=jax.ShapeDtypeStruct((r, c), F32),
        compiler_params=_params("arbitrary"))(g8)


def _pad_lanes(a, width):
    return jnp.pad(a, [(0, 0)] * (a.ndim - 1) + [(0, width - a.shape[-1])])


def kernel(x, positions, ffn_norm1, ffn1_w1, ffn1_w3, ffn1_w2, mix_norm, ffn_norm2, ffn2_w1, ffn2_w3, ffn2_w2, conv_w_pw1, conv_w_dw, conv_norm, conv_w_pw2, mla_w_a, mla_q_norm, mla_kv_norm, mla_w_uq, mla_w_ukv, mla_w_o, final_norm, loss_target, m_ffn_norm1, m_ffn1_w1, m_ffn1_w3, m_ffn1_w2, m_mix_norm, m_ffn_norm2, m_ffn2_w1, m_ffn2_w3, m_ffn2_w2, m_conv_w_pw1, m_conv_w_dw, m_conv_norm, m_conv_w_pw2, m_mla_w_a, m_mla_q_norm, m_mla_kv_norm, m_mla_w_uq, m_mla_w_ukv, m_mla_w_o, m_final_norm, v_ffn_norm1, v_ffn1_w1, v_ffn1_w3, v_ffn1_w2, v_mix_norm, v_ffn_norm2, v_ffn2_w1, v_ffn2_w3, v_ffn2_w2, v_conv_w_pw1, v_conv_w_dw, v_conv_norm, v_conv_w_pw2, v_mla_w_a, v_mla_q_norm, v_mla_kv_norm, v_mla_w_uq, v_mla_w_ukv, v_mla_w_o, v_final_norm):
    given = locals()
    return _step({nm: given[nm] for nm in INPUTS})


def _step(A):
    x = A['x'][0]
    target = A['loss_target'][0]
    t, d = x.shape
    pos = A['positions'].reshape(t, 1)
    me = 2 * lax.axis_index("x") + lax.axis_index("y")

    ffn = [f'ffn{k}_{w}' for k in (1, 2) for w in ('w1', 'w3', 'w2')]
    gather_groups = [[(nm, 0) for nm in ffn[:3]],
                     [('conv_w_pw1', 0), ('conv_w_pw2', 0)] + [(nm, 0) for nm in ffn[3:]],
                     [(nm, 1) for nm in ffn[:3]] + [('mla_w_a', 0), ('mla_w_uq', 0), ('mla_w_ukv', 0), ('mla_w_o', 0)],
                     [(nm, 1) for nm in ffn[3:]]]
    gather_flights = {}
    big = {}

    def gather_start(gi, dep):
        shards = [_bf(A[nm][l]) for nm, l in gather_groups[gi]]
        gather_flights[gi], token = _exchange_start(shards, [_landing(s, me) for s in shards], False,
                                                    f"gather_start_{gi}", dep)
        return token

    def gather_wait(gi, after):
        landed = _exchange_wait(gather_flights[gi], after, False, f"gather_wait_{gi}")
        big.update(zip(gather_groups[gi], landed))
        return landed[0]

    dw_shard = A['conv_w_dw'][0]
    cw = dw_shard.shape[1]
    small = jnp.concatenate([
        jnp.pad(dw_shard, ((0, CONV_HALO - CONV_WIDTH), (0, 0))),
        jnp.pad(_pad_lanes(A['mla_q_norm'], cw), ((0, 7), (0, 0))),
        jnp.pad(_pad_lanes(A['mla_kv_norm'], cw), ((0, 7), (0, 0)))], axis=0)
    small = _all_gather_rows(small, "gather_small_weights").reshape(4, 2, 48, cw)[:, 0]
    w_dw = jnp.concatenate([small[j, :CONV_HALO] for j in range(4)], axis=1)
    gq = jnp.concatenate([small[j, CONV_HALO, :Q_LORA // 4] for j in range(4)])
    gkv = jnp.concatenate([small[j, CONV_HALO + 8, :KV_LORA // 4] for j in range(4)])

    def cols(nm, layer):
        return jnp.concatenate([big[nm, layer][j] for j in range(4)], axis=1)

    def rows(nm, layer):
        g = big[nm, layer]
        return g.reshape(-1, g.shape[-1])

    ffn_w = {}

    def ffn_weights(k, l):
        ffn_w[k, l] = (cols(f'ffn{k}_w1', l), cols(f'ffn{k}_w3', l), rows(f'ffn{k}_w2', l))
        return ffn_w[k, l]

    token = gather_start(0, None)
    cs_c, cs_s = _rope_tables(pos)
    h0 = x
    token = gather_start(1, gather_wait(0, token))
    h1, n01, z01a, z01b = _ffn_fwd(h0, A['ffn_norm1'][0], *ffn_weights(1, 0), token, "ffn1_l0_fwd")
    token = gather_start(3, gather_start(2, gather_wait(1, h1)))
    pw1 = big['conv_w_pw1', 0]
    pw1_a = jnp.concatenate([pw1[0], pw1[1]], axis=1)
    pw1_b = jnp.concatenate([pw1[2], pw1[3]], axis=1)
    pw2 = rows('conv_w_pw2', 0)
    m0 = _norm_fwd(h1, A['mix_norm'][0], token, "mix_norm_l0")
    ca, cb, glu = _glu_fwd(m0, pw1_a, pw1_b)
    cv, cs = _conv_fwd(glu, w_dw, A['conv_norm'][0])
    h2 = _mm([(cs, pw2)], F32, "conv_pw2_fwd", res=h1)
    h3, n02, z02a, z02b = _ffn_fwd(h2, A['ffn_norm2'][0], *ffn_weights(2, 0), token, "ffn2_l0_fwd")
    gather_wait(2, h3)
    w_a = _pad_lanes(rows('mla_w_a', 0), A_PAD)
    wuq = _pad_lanes(big['mla_w_uq', 0].reshape(Q_LORA, HEADS, NOPE + ROPE).transpose(1, 0, 2), HEAD_PAD)
    wukv = big['mla_w_ukv', 0].reshape(KV_LORA, HEADS, NOPE + V_HEAD).transpose(1, 0, 2)
    w_o = rows('mla_w_o', 0)
    h4, n11, z11a, z11b = _ffn_fwd(h3, A['ffn_norm1'][1], *ffn_weights(1, 1), token, "ffn1_l1_fwd")
    m1 = _norm_fwd(h4, A['mix_norm'][1], token, "mix_norm_l1")
    a_lat = _mm([(m1, w_a)], F32, "mla_down_fwd")
    cq, ckv, kr = _mla_prep(a_lat, gq, gkv, cs_c, cs_s)
    q, k, v = _mla_qkv(cq, ckv, kr, cs_c, cs_s, wuq, wukv)
    o, lse = _flash_fwd(q, k, v)
    h5 = _mm([(o, w_o)], F32, "mla_out_fwd", res=h4)
    gather_wait(3, h5)
    h6, n12, z12a, z12b = _ffn_fwd(h5, A['ffn_norm2'][1], *ffn_weights(2, 1), token, "ffn2_l1_fwd")

    def col_slots(g):
        r, c4 = g.shape
        return g.reshape(r, 4, c4 // 4).transpose(1, 0, 2)

    def row_slots(g):
        return g.reshape(4, g.shape[0] // 4, g.shape[1])

    scatter_flights = []

    def scatter_start(named):
        srcs = [g for _, g in named]
        lands = [_landing(lax.dynamic_index_in_dim(g, me, 0, keepdims=False), me) for g in srcs]
        flight, token = _exchange_start(srcs, lands, True, f"scatter_start_{len(scatter_flights)}")
        scatter_flights.append(([key for key, _ in named], flight))
        return token

    def send_ffn(k, l, dw1, dw3, dw2):
        return scatter_start([((f'ffn{k}_w1', l), col_slots(dw1)), ((f'ffn{k}_w3', l), col_slots(dw3)),
                              ((f'ffn{k}_w2', l), row_slots(dw2))])

    dh6, dg_final, loss_part = _loss_bwd(h6, target, A['final_norm'])
    dh5, dg_n2_l1, *dws = _ffn_bwd(dh6, h5, A['ffn_norm2'][1], n12, z12a, z12b, *ffn_w[2, 1], loss_part, "ffn2_l1")
    token = send_ffn(2, 1, *dws)

    do = _mm([(dh5, w_o)], BF16, "mla_out_bwd", trans_b=True, dep=token)
    dw_o = _mm_tn(o, dh5, BF16, "mla_dw_o")
    delta = _attn_delta(do, o)
    dq, dk, dv = _flash_bwd(q, k, v, do, lse, delta)
    dr, dkv, dcq, dckv, dar = _mla_qkv_bwd(dq, dk, dv, cs_c, cs_s, wuq, wukv)
    dwuq = _mm_tn(cq, dr, BF16, "mla_dw_uq", bn=dr.shape[1] // 2)
    dwukv = _mm_tn(ckv, dkv, BF16, "mla_dw_ukv", bn=dkv.shape[1] // 2)
    da_lat, dgq, dgkv = _mla_prep_bwd(a_lat, dcq, dckv, dar, gq, gkv)
    dw_a = _mm_tn(m1, da_lat, BF16, "mla_dw_a")
    token = scatter_start([
        (('mla_w_a', 0), row_slots(dw_a[:, :Q_LORA + KV_LORA + ROPE])),
        (('mla_w_uq', 0), dwuq.reshape(4, Q_LORA // 4, HEADS, HEAD_PAD)[..., :NOPE + ROPE]),
        (('mla_w_ukv', 0), dwukv.reshape(4, KV_LORA // 4, HEADS, NOPE + V_HEAD)),
        (('mla_w_o', 0), row_slots(dw_o))])
    dh4, dg_mix_l1 = _mm_normbwd([(da_lat, w_a)], h4, A['mix_norm'][1], dh5, token, "mla_down_bwd")

    dh3, dg_n1_l1, *dws = _ffn_bwd(dh4, h3, A['ffn_norm1'][1], n11, z11a, z11b, *ffn_w[1, 1], token, "ffn1_l1")
    token = send_ffn(1, 1, *dws)
    dh2, dg_n2_l0, *dws = _ffn_bwd(dh3, h2, A['ffn_norm2'][0], n02, z02a, z02b, *ffn_w[2, 0], token, "ffn2_l0")
    token = send_ffn(2, 0, *dws)

    dcv, dg_conv = _conv_bwd_norm(dh2, cv, pw2, A['conv_norm'][0], token)
    dw_pw2 = _mm_tn(cs, dh2, BF16, "conv_dw_pw2")
    dca, dcb, ddw = _conv_bwd_dw(dcv, glu, ca, cb, w_dw)
    dpw1_a = _mm_tn(m0, dca, BF16, "conv_dw_pw1a")
    dpw1_b = _mm_tn(m0, dcb, BF16, "conv_dw_pw1b")
    half = dpw1_a.shape[1] // 2
    token = scatter_start([
        (('conv_w_pw1', 0), jnp.stack([dpw1_a[:, :half], dpw1_a[:, half:], dpw1_b[:, :half], dpw1_b[:, half:]])),
        (('conv_w_pw2', 0), row_slots(dw_pw2))])
    dh1, dg_mix_l0 = _mm_normbwd([(dca, pw1_a), (dcb, pw1_b)], h1, A['mix_norm'][0], dh2, token, "conv_pw1_bwd")

    dx, dg_n1_l0, *dws = _ffn_bwd(dh1, h0, A['ffn_norm1'][0], n01, z01a, z01b, *ffn_w[1, 0], token, "ffn1_l0")
    last_sent = send_ffn(1, 0, *dws)
    out = {}

    qkv_row = jnp.concatenate([dgq, dgkv, jnp.zeros((8, d - Q_LORA - KV_LORA), F32)], axis=1)
    loss_row = _pad_lanes(loss_part, d)
    small_g = jnp.concatenate([dg_n1_l0, dg_n1_l1, dg_mix_l0, dg_mix_l1, dg_n2_l0, dg_n2_l1, dg_conv, dg_final,
                               qkv_row, loss_row, ddw], axis=0)
    nrow = small_g.shape[0]
    tot = _sum_devices(_all_gather_rows(small_g, "gather_small_grads").reshape(8, nrow, d), "sum_small_grads")
    loss = tot[72, 0]
    q_shard = lax.dynamic_slice_in_dim(tot[64, :Q_LORA], me * (Q_LORA // 4), Q_LORA // 4)
    kv_shard = lax.dynamic_slice_in_dim(tot[64, Q_LORA:Q_LORA + KV_LORA], me * (KV_LORA // 4), KV_LORA // 4)
    dw_shard_g = lax.dynamic_slice_in_dim(tot[80:80 + CONV_WIDTH], me * cw, cw, axis=1)
    small_grads = {
        'ffn_norm1': jnp.stack([tot[0], tot[8]]), 'mix_norm': jnp.stack([tot[16], tot[24]]),
        'ffn_norm2': jnp.stack([tot[32], tot[40]]), 'conv_norm': tot[48][None], 'final_norm': tot[56],
        'mla_q_norm': q_shard[None], 'mla_kv_norm': kv_shard[None], 'conv_w_dw': dw_shard_g[None],
    }
    for nm, g in small_grads.items():
        res = _adamw(_as_rows(A[nm]) if A[nm].ndim > 1 else A[nm].reshape(1, -1),
                     A['m_' + nm].reshape(-1, A[nm].shape[-1]), A['v_' + nm].reshape(-1, A[nm].shape[-1]),
                     [g.reshape(-1, A[nm].shape[-1])], "adamw_" + nm)
        out[nm] = [r.reshape(A[nm].shape) for r in res]

    received = {}
    after = last_sent

    def scatter_wait(si, after):
        keys, flight = scatter_flights[si]
        landed = _exchange_wait(flight, after, True, f"scatter_wait_{si}")
        received.update(zip(keys, landed))
        return landed[0]

    def slots(nm, l):
        return received[nm, l].reshape(4, -1, received[nm, l].shape[-1])

    def finish(names, sums, tag):
        for nm, mine, theirs in zip(names, sums, _swap_with_sibling(sums, "swap_with_sibling_" + tag)):
            res = _adamw(_as_rows(A[nm]), _as_rows(A['m_' + nm]), _as_rows(A['v_' + nm]), [mine, theirs],
                         "adamw_" + nm)
            out[nm] = [r.reshape(A[nm].shape) for r in res]
        return res[1]

    last = len(scatter_flights) - 1
    for si in range(last):
        after = scatter_wait(si, after)
    late = ffn[:3]
    early = [nm for nm in BIG if nm not in late]
    late_l1 = [_sum_slots([(1, slots(nm, 1))], 2, "sum_" + nm) for nm in late]
    after = finish(early, [_sum_slots([(l, slots(nm, l)) for l in range(A[nm].shape[0])], A[nm].shape[0],
                                      "sum_" + nm) for nm in early], "early")
    scatter_wait(last, after)
    finish(late, [_sum_slots([(0, slots(nm, 0))], 2, "sum_" + nm, into=part) for nm, part in zip(late, late_l1)],
           "late")

    return (loss, dx[None], *[out[nm][0] for nm in WEIGHTS], *[out[nm][1] for nm in WEIGHTS],
            *[out[nm][2] for nm in WEIGHTS], *[out[nm][3] for nm in WEIGHTS])
```

```python
import functools

import jax
import jax.numpy as jnp
import numpy as np
from jax import lax
from jax.experimental import pallas as pl
from jax.experimental.pallas import tpu as pltpu

F32 = jnp.float32
BF16 = jnp.bfloat16
MESH = pl.DeviceIdType.MESH

RMS_EPS = 1e-6
HEADS = 8
NOPE = 128
ROPE = 64
HEAD_PAD = 256
V_HEAD = 128
Q_LORA = 512
KV_LORA = 256
A_PAD = 896
CHUNK = 64
CONV_WIDTH = 31
CONV_HALO = 32
CONV_ROWS = 16
ROPE_THETA = 10000.0
ATTN_SCALE = (NOPE + ROPE) ** -0.5
FFN_RES = 0.5

ADAM_LR = 0.001
ADAM_B1 = 0.9
ADAM_B2 = 0.999
ADAM_EPS = 1e-08
ADAM_WD = 0.01
ADAM_STEP = 10

VMEM_LIMIT_BYTES = 56 * 1024 * 1024

WEIGHTS = ['ffn_norm1', 'ffn1_w1', 'ffn1_w3', 'ffn1_w2', 'mix_norm', 'ffn_norm2', 'ffn2_w1', 'ffn2_w3', 'ffn2_w2',
           'conv_w_pw1', 'conv_w_dw', 'conv_norm', 'conv_w_pw2', 'mla_w_a', 'mla_q_norm', 'mla_kv_norm', 'mla_w_uq',
           'mla_w_ukv', 'mla_w_o', 'final_norm']
INPUTS = (['x', 'positions'] + WEIGHTS + ['loss_target'] + ['m_' + w for w in WEIGHTS] + ['v_' + w for w in WEIGHTS])
BIG = ['ffn1_w1', 'ffn1_w3', 'ffn1_w2', 'ffn2_w1', 'ffn2_w3', 'ffn2_w2', 'conv_w_pw1', 'conv_w_pw2', 'mla_w_a',
       'mla_w_uq', 'mla_w_ukv', 'mla_w_o']


def _params(*sem):
    return pltpu.CompilerParams(dimension_semantics=sem, vmem_limit_bytes=VMEM_LIMIT_BYTES)


def _bf(v):
    return v.astype(BF16)


def _rstd(x):
    return lax.rsqrt(jnp.mean(x * x, axis=-1, keepdims=True) + RMS_EPS)


def _sigmoid(x):
    return jax.nn.sigmoid(x)


def _rot(x):
    lane = lax.broadcasted_iota(jnp.int32, x.shape, 1)
    return jnp.where(lane < ROPE // 2, -pltpu.roll(x, 128 - ROPE // 2, 1), pltpu.roll(x, ROPE // 2, 1))


def _rot_t(y):
    lane = lax.broadcasted_iota(jnp.int32, y.shape, 1)
    return jnp.where(lane < ROPE // 2, pltpu.roll(y, 128 - ROPE // 2, 1), -pltpu.roll(y, ROPE // 2, 1))


def _pair_sum(a_refs, b_refs, trans_b):
    tot = None
    for a_r, b_r in zip(a_refs, b_refs):
        a, b = _bf(a_r[...]), _bf(b_r[...])
        if trans_b:
            d = lax.dot_general(a, b, (((1,), (1,)), ((), ())), preferred_element_type=F32)
        else:
            d = jnp.dot(a, b, preferred_element_type=F32)
        tot = d if tot is None else tot + d
    return tot


def _mm(pairs, out_dtype, name, *, trans_b=False, tm=512, tn=None, tk=None, res=None, dep=None):
    m, k = pairs[0][0].shape
    n = pairs[0][1].shape[0] if trans_b else pairs[0][1].shape[1]
    tm, tn, tk = min(tm, m), tn or n, tk or k
    nk, npair = k // tk, len(pairs)

    def kern(*refs):
        a_refs, b_refs = refs[:npair], refs[npair:2 * npair]
        rest = list(refs[2 * npair:])
        res_ref = rest.pop(0) if res is not None else None
        if dep is not None:
            rest.pop(0)
        o_ref = rest.pop(0)

        def finish(acc):
            if res_ref is not None:
                acc = res_ref[...] + acc
            o_ref[...] = acc.astype(o_ref.dtype)

        if nk == 1:
            finish(_pair_sum(a_refs, b_refs, trans_b))
        else:
            acc_ref = rest.pop(0)
            kk = pl.program_id(2)

            @pl.when(kk == 0)
            def _():
                acc_ref[...] = jnp.zeros_like(acc_ref)

            acc_ref[...] += _pair_sum(a_refs, b_refs, trans_b)

            @pl.when(kk == nk - 1)
            def _():
                finish(acc_ref[...])

    a_spec = pl.BlockSpec((tm, tk), lambda i, j, kk: (i, kk))
    b_spec = (pl.BlockSpec((tn, tk), lambda i, j, kk: (j, kk)) if trans_b
              else pl.BlockSpec((tk, tn), lambda i, j, kk: (kk, j)))
    io_spec = pl.BlockSpec((tm, tn), lambda i, j, kk: (i, j))
    in_specs = ([a_spec] * npair + [b_spec] * npair + ([io_spec] if res is not None else [])
                + ([pl.BlockSpec((8, 128), lambda i, j, kk: (0, 0))] if dep is not None else []))
    args = ([p[0] for p in pairs] + [p[1] for p in pairs] + ([res] if res is not None else [])
            + ([dep] if dep is not None else []))
    return pl.pallas_call(
        kern, name=name, grid=(m // tm, n // tn, nk), in_specs=in_specs, out_specs=io_spec,
        out_shape=jax.ShapeDtypeStruct((m, n), out_dtype),
        scratch_shapes=[pltpu.VMEM((tm, tn), F32)] if nk > 1 else [],
        compiler_params=_params("parallel", "parallel", "arbitrary"))(*args)


def _mm_normbwd(pairs, h, g, dres, dep, name, *, tm=512, tk=None):
    m, k = pairs[0][0].shape
    d = pairs[0][1].shape[0]
    tm, tk = min(tm, m), tk or k
    nk, npair = k // tk, len(pairs)

    def kern(*refs):
        a_refs, b_refs = refs[:npair], refs[npair:2 * npair]
        h_ref, g_ref, dres_ref, _, o_ref, dg_ref, acc_ref = refs[2 * npair:]
        i, kk = pl.program_id(0), pl.program_id(1)

        @pl.when(jnp.logical_and(i == 0, kk == 0))
        def _():
            dg_ref[...] = jnp.zeros_like(dg_ref)

        @pl.when(kk == 0)
        def _():
            acc_ref[...] = jnp.zeros_like(acc_ref)

        acc_ref[...] += _pair_sum(a_refs, b_refs, True)

        @pl.when(kk == nk - 1)
        def _():
            dn = acc_ref[...]
            x = h_ref[...]
            rstd = _rstd(x)
            xhat = x * rstd
            dg_ref[...] += jnp.broadcast_to(jnp.sum(dn * xhat, axis=0, keepdims=True), dg_ref.shape)
            dxh = dn * g_ref[...]
            dx = rstd * (dxh - xhat * jnp.mean(dxh * xhat, axis=-1, keepdims=True))
            o_ref[...] = dres_ref[...] + dx

    row = pl.BlockSpec((tm, d), lambda i, kk: (i, 0))
    in_specs = ([pl.BlockSpec((tm, tk), lambda i, kk: (i, kk))] * npair
                + [pl.BlockSpec((d, tk), lambda i, kk: (0, kk))] * npair
                + [row, pl.BlockSpec((1, d), lambda i, kk: (0, 0)), row, pl.BlockSpec((8, 128), lambda i, kk: (0, 0))])
    return pl.pallas_call(
        kern, name=name, grid=(m // tm, nk), in_specs=in_specs,
        out_specs=[row, pl.BlockSpec((8, d), lambda i, kk: (0, 0))],
        out_shape=[jax.ShapeDtypeStruct((m, d), F32), jax.ShapeDtypeStruct((8, d), F32)],
        scratch_shapes=[pltpu.VMEM((tm, d), F32)],
        compiler_params=_params("arbitrary", "arbitrary"))(
            *[p[0] for p in pairs], *[p[1] for p in pairs], h, g.reshape(1, d), dres, dep)


def _mm_tn(a, b, out_dtype, name, *, bm=None, bn=None, tk=1024):
    t, m = a.shape
    batched = b.ndim == 3
    n = b.shape[-1]
    nb = b.shape[0] if batched else 1
    bm, bn, tk = bm or m, bn or n, min(tk, t)
    nk = t // tk

    def kern(a_ref, b_ref, o_ref, acc_ref):
        kk = pl.program_id(3)

        @pl.when(kk == 0)
        def _():
            acc_ref[...] = jnp.zeros_like(acc_ref)

        acc_ref[...] += lax.dot_general(_bf(a_ref[...]), _bf(b_ref[...]), (((0,), (0,)), ((), ())),
                                        preferred_element_type=F32)

        @pl.when(kk == nk - 1)
        def _():
            o_ref[...] = acc_ref[...].astype(o_ref.dtype)

    a_spec = pl.BlockSpec((tk, bm), lambda h, i, j, kk: (kk, i))
    if batched:
        b_spec = pl.BlockSpec((None, tk, bn), lambda h, i, j, kk: (h, kk, j))
        o_spec = pl.BlockSpec((None, bm, bn), lambda h, i, j, kk: (h, i, j))
        out_shape = jax.ShapeDtypeStruct((nb, m, n), out_dtype)
    else:
        b_spec = pl.BlockSpec((tk, bn), lambda h, i, j, kk: (kk, j))
        o_spec = pl.BlockSpec((bm, bn), lambda h, i, j, kk: (i, j))
        out_shape = jax.ShapeDtypeStruct((m, n), out_dtype)
    return pl.pallas_call(
        kern, name=name, grid=(nb, m // bm, n // bn, nk), in_specs=[a_spec, b_spec], out_specs=o_spec,
        out_shape=out_shape, scratch_shapes=[pltpu.VMEM((bm, bn), F32)],
        compiler_params=_params("parallel", "parallel", "parallel", "arbitrary"))(a, b)


def _ffn_tile(f):
    return f // 2 if (f // 2) % 128 == 0 else f


def _ffn_fwd(h, g, w1, w3, w2, dep, name):
    t, d = h.shape
    f = w1.shape[1]
    tm, tf = min(512, t), _ffn_tile(f)
    nf = f // tf

    def kern(h_ref, g_ref, w1_ref, w3_ref, w2_ref, dep_ref, ho_ref, n_ref, z1_ref, z3_ref, n_sc, acc_ref):
        j = pl.program_id(1)

        @pl.when(j == 0)
        def _():
            x = h_ref[...]
            n = _bf(x * _rstd(x) * g_ref[...])
            n_sc[...] = n
            n_ref[...] = n
            acc_ref[...] = jnp.zeros_like(acc_ref)

        n = n_sc[...]
        z1 = jnp.dot(n, w1_ref[...], preferred_element_type=F32)
        z3 = jnp.dot(n, w3_ref[...], preferred_element_type=F32)
        z1_ref[...] = _bf(z1)
        z3_ref[...] = _bf(z3)
        act = _bf(z1 * _sigmoid(z1) * z3)
        acc_ref[...] += jnp.dot(act, w2_ref[...], preferred_element_type=F32)

        @pl.when(j == nf - 1)
        def _():
            ho_ref[...] = h_ref[...] + FFN_RES * acc_ref[...]

    row = pl.BlockSpec((tm, d), lambda i, j: (i, 0))
    col = pl.BlockSpec((tm, tf), lambda i, j: (i, j))
    return pl.pallas_call(
        kern, name=name, grid=(t // tm, nf),
        in_specs=[row, pl.BlockSpec((1, d), lambda i, j: (0, 0)), pl.BlockSpec((d, tf), lambda i, j: (0, j)),
                  pl.BlockSpec((d, tf), lambda i, j: (0, j)), pl.BlockSpec((tf, d), lambda i, j: (j, 0)),
                  pl.BlockSpec((8, 128), lambda i, j: (0, 0))],
        out_specs=[row, row, col, col],
        out_shape=[jax.ShapeDtypeStruct((t, d), F32), jax.ShapeDtypeStruct((t, d), BF16),
                   jax.ShapeDtypeStruct((t, f), BF16), jax.ShapeDtypeStruct((t, f), BF16)],
        scratch_shapes=[pltpu.VMEM((tm, d), BF16), pltpu.VMEM((tm, d), F32)],
        compiler_params=_params("parallel", "arbitrary"))(h, g.reshape(1, d), w1, w3, w2, dep)


def _ffn_bwd_x(dh, h_in, g, z1, z3, w1, w3, w2, dep, name):
    t, d = dh.shape
    f = z1.shape[1]
    tm = min(256, t)

    def kern(dh_ref, h_ref, g_ref, z1_ref, z3_ref, w2_hbm, w1_hbm, w3_hbm, dep_ref,
             o_ref, dg_ref, dz1_ref, dz3_ref, a_ref, df_ref, w2_ref, w1_ref, w3_ref, sems):
        @pl.when(pl.program_id(0) == 0)
        def _():
            copies = [pltpu.make_async_copy(src, dst, sems.at[k]) for k, (src, dst) in
                      enumerate(((w2_hbm, w2_ref), (w1_hbm, w1_ref), (w3_hbm, w3_ref)))]
            for cp in copies:
                cp.start()
            dg_ref[...] = jnp.zeros_like(dg_ref)
            for cp in copies:
                cp.wait()

        df = _bf(FFN_RES * dh_ref[...])
        df_ref[...] = df
        da = lax.dot_general(df, w2_ref[...], (((1,), (1,)), ((), ())), preferred_element_type=F32)
        z1v, z3v = z1_ref[...].astype(F32), z3_ref[...].astype(F32)
        sig = _sigmoid(z1v)
        silu = z1v * sig
        a_ref[...] = _bf(silu * z3v)
        dz1 = _bf(da * z3v * (sig * (1.0 + z1v * (1.0 - sig))))
        dz3 = _bf(da * silu)
        dz1_ref[...] = dz1
        dz3_ref[...] = dz3
        dn = (lax.dot_general(dz1, w1_ref[...], (((1,), (1,)), ((), ())), preferred_element_type=F32)
              + lax.dot_general(dz3, w3_ref[...], (((1,), (1,)), ((), ())), preferred_element_type=F32))
        x = h_ref[...]
        rstd = _rstd(x)
        xhat = x * rstd
        dg_ref[...] += jnp.broadcast_to(jnp.sum(dn * xhat, axis=0, keepdims=True), dg_ref.shape)
        dxh = dn * g_ref[...]
        o_ref[...] = dh_ref[...] + rstd * (dxh - xhat * jnp.mean(dxh * xhat, axis=-1, keepdims=True))

    row = pl.BlockSpec((tm, d), lambda i: (i, 0))
    col = pl.BlockSpec((tm, f), lambda i: (i, 0))
    whole = pl.BlockSpec(memory_space=pl.ANY)
    colshape = jax.ShapeDtypeStruct((t, f), BF16)
    return pl.pallas_call(
        kern, name=name, grid=(t // tm,),
        in_specs=[row, row, pl.BlockSpec((1, d), lambda i: (0, 0)), col, col, whole, whole, whole,
                  pl.BlockSpec((8, 128), lambda i: (0, 0))],
        out_specs=[row, pl.BlockSpec((8, d), lambda i: (0, 0)), col, col, col, row],
        out_shape=[jax.ShapeDtypeStruct((t, d), F32), jax.ShapeDtypeStruct((8, d), F32), colshape, colshape, colshape,
                   jax.ShapeDtypeStruct((t, d), BF16)],
        scratch_shapes=[pltpu.VMEM((f, d), BF16), pltpu.VMEM((d, f), BF16), pltpu.VMEM((d, f), BF16),
                        pltpu.SemaphoreType.DMA((3,))],
        compiler_params=_params("arbitrary"))(dh, h_in, g.reshape(1, d), z1, z3, w2, w1, w3, dep)


def _ffn_bwd(dh, h_in, g, n, z1, z3, w1, w3, w2, dep, tag):
    f = w1.shape[1]
    dh_in, dg, dz1, dz3, act, df = _ffn_bwd_x(dh, h_in, g, z1, z3, w1, w3, w2, dep, tag + "_bwd_x")
    dw1 = _mm_tn(n, dz1, BF16, tag + "_dw1", bn=_ffn_tile(f))
    dw3 = _mm_tn(n, dz3, BF16, tag + "_dw3", bn=_ffn_tile(f))
    dw2 = _mm_tn(act, df, BF16, tag + "_dw2", bm=_ffn_tile(f))
    return dh_in, dg, dw1, dw3, dw2


def _norm_fwd(h, g, dep, name):
    t, d = h.shape
    tm = min(512, t)

    def kern(h_ref, g_ref, dep_ref, o_ref):
        x = h_ref[...]
        o_ref[...] = _bf(x * _rstd(x) * g_ref[...])

    row = pl.BlockSpec((tm, d), lambda i: (i, 0))
    return pl.pallas_call(
        kern, name=name, grid=(t // tm,),
        in_specs=[row, pl.BlockSpec((1, d), lambda i: (0, 0)), pl.BlockSpec((8, 128), lambda i: (0, 0))],
        out_specs=row, out_shape=jax.ShapeDtypeStruct((t, d), BF16),
        compiler_params=_params("parallel"))(h, g.reshape(1, d), dep)


def _loss_bwd(h, target, g):
    t, d = h.shape
    tm = min(512, t)

    def kern(h_ref, t_ref, g_ref, dh_ref, dg_ref, loss_ref):
        @pl.when(pl.program_id(0) == 0)
        def _():
            dg_ref[...] = jnp.zeros_like(dg_ref)
            loss_ref[...] = jnp.zeros_like(loss_ref)

        x = h_ref[...]
        rstd = _rstd(x)
        xhat = x * rstd
        err = xhat * g_ref[...] - t_ref[...]
        row_loss = jnp.sum(err * err, axis=-1, keepdims=True) * (0.5 / d)
        loss_ref[...] += jnp.broadcast_to(jnp.sum(row_loss, axis=0, keepdims=True), loss_ref.shape)
        dy = err * (1.0 / d)
        dg_ref[...] += jnp.broadcast_to(jnp.sum(dy * xhat, axis=0, keepdims=True), dg_ref.shape)
        dxh = dy * g_ref[...]
        dh_ref[...] = rstd * (dxh - xhat * jnp.mean(dxh * xhat, axis=-1, keepdims=True))

    row = pl.BlockSpec((tm, d), lambda i: (i, 0))
    return pl.pallas_call(
        kern, name="loss_bwd", grid=(t // tm,),
        in_specs=[row, row, pl.BlockSpec((1, d), lambda i: (0, 0))],
        out_specs=[row, pl.BlockSpec((8, d), lambda i: (0, 0)), pl.BlockSpec((8, 128), lambda i: (0, 0))],
        out_shape=[jax.ShapeDtypeStruct((t, d), F32), jax.ShapeDtypeStruct((8, d), F32),
                   jax.ShapeDtypeStruct((8, 128), F32)],
        compiler_params=_params("arbitrary"))(h, target, g.reshape(1, d))


def _glu_fwd(m, wa, wb):
    t, d = m.shape
    c = wa.shape[1]
    tm, tc = min(512, t), min(512, c)

    def kern(m_ref, wa_ref, wb_ref, a_ref, b_ref, glu_ref):
        mv = m_ref[...]
        a = jnp.dot(mv, wa_ref[...], preferred_element_type=F32)
        b = jnp.dot(mv, wb_ref[...], preferred_element_type=F32)
        a_ref[...] = _bf(a)
        b_ref[...] = _bf(b)
        glu_ref[...] = _bf(a * _sigmoid(b))

    col = pl.BlockSpec((tm, tc), lambda i, j: (i, j))
    wspec = pl.BlockSpec((d, tc), lambda i, j: (0, j))
    shape = jax.ShapeDtypeStruct((t, c), BF16)
    return pl.pallas_call(
        kern, name="conv_glu_fwd", grid=(t // tm, c // tc),
        in_specs=[pl.BlockSpec((tm, d), lambda i, j: (i, 0)), wspec, wspec], out_specs=[col, col, col],
        out_shape=[shape, shape, shape], compiler_params=_params("parallel", "parallel"))(m, wa, wb)


def _conv_tile(t):
    return min(256, t)


def _shift_copies(ext, shifted, rows):
    for s in range(8):
        shifted[s] = ext[pl.ds(s, rows), :]


def _shifted_rows(shifted, start, nrows):
    return shifted[start % 8, pl.ds(start - start % 8, nrows), :]


def _conv_fwd(glu, w_dw, g):
    t, c = glu.shape
    tm = _conv_tile(t)
    hb = tm // CONV_HALO

    def kern(cur_ref, halo_ref, w_ref, g_ref, cv_ref, s_ref, ext, shifted):
        i = pl.program_id(0)
        ext[0:CONV_HALO, :] = jnp.where(i > 0, halo_ref[...].astype(F32), 0.0)
        ext[CONV_HALO:tm + CONV_HALO, :] = cur_ref[...].astype(F32)
        ext[tm + CONV_HALO:, :] = jnp.zeros((8, c), F32)
        _shift_copies(ext, shifted, tm + CONV_HALO)
        gv = g_ref[...]
        for r0 in range(0, tm, CONV_ROWS):
            acc = jnp.zeros((CONV_ROWS, c), F32)
            for k in range(CONV_WIDTH):
                acc = acc + _shifted_rows(shifted, r0 + 2 + k, CONV_ROWS) * w_ref[k:k + 1, :]
            cv_ref[r0:r0 + CONV_ROWS, :] = acc
            rn = acc * _rstd(acc) * gv
            s_ref[r0:r0 + CONV_ROWS, :] = _bf(rn * _sigmoid(rn))

    row = pl.BlockSpec((tm, c), lambda i: (i, 0))
    return pl.pallas_call(
        kern, name="conv_fwd", grid=(t // tm,),
        in_specs=[row, pl.BlockSpec((CONV_HALO, c), lambda i: (jnp.maximum(i * hb - 1, 0), 0)),
                  pl.BlockSpec((CONV_HALO, c), lambda i: (0, 0)), pl.BlockSpec((1, c), lambda i: (0, 0))],
        out_specs=[row, row],
        out_shape=[jax.ShapeDtypeStruct((t, c), F32), jax.ShapeDtypeStruct((t, c), BF16)],
        scratch_shapes=[pltpu.VMEM((tm + CONV_HALO + 8, c), F32), pltpu.VMEM((8, tm + CONV_HALO, c), F32)],
        compiler_params=_params("parallel"))(glu, glu, w_dw, g.reshape(1, c))


def _conv_bwd_norm(dh, cv, w_pw2, g, dep):
    t, c = cv.shape
    tm = min(512, t)

    def kern(dh_ref, cv_ref, w_ref, g_ref, dep_ref, dcv_ref, dg_ref):
        @pl.when(pl.program_id(0) == 0)
        def _():
            dg_ref[...] = jnp.zeros_like(dg_ref)

        ds = lax.dot_general(_bf(dh_ref[...]), w_ref[...], (((1,), (1,)), ((), ())), preferred_element_type=F32)
        x = cv_ref[...]
        rstd = _rstd(x)
        xhat = x * rstd
        rn = xhat * g_ref[...]
        sig = _sigmoid(rn)
        drn = ds * (sig * (1.0 + rn * (1.0 - sig)))
        dg_ref[...] += jnp.broadcast_to(jnp.sum(drn * xhat, axis=0, keepdims=True), dg_ref.shape)
        dxh = drn * g_ref[...]
        dcv_ref[...] = rstd * (dxh - xhat * jnp.mean(dxh * xhat, axis=-1, keepdims=True))

    row = pl.BlockSpec((tm, c), lambda i: (i, 0))
    return pl.pallas_call(
        kern, name="conv_bwd_norm", grid=(t // tm,),
        in_specs=[pl.BlockSpec((tm, dh.shape[1]), lambda i: (i, 0)), row,
                  pl.BlockSpec(w_pw2.shape, lambda i: (0, 0)), pl.BlockSpec((1, c), lambda i: (0, 0)),
                  pl.BlockSpec((8, 128), lambda i: (0, 0))],
        out_specs=[row, pl.BlockSpec((8, c), lambda i: (0, 0))],
        out_shape=[jax.ShapeDtypeStruct((t, c), F32), jax.ShapeDtypeStruct((8, c), F32)],
        compiler_params=_params("arbitrary"))(dh, cv, w_pw2, g.reshape(1, c), dep)


def _conv_bwd_dw(dcv, glu, a, b, w_dw):
    t, c = dcv.shape
    tm = _conv_tile(t)
    hb = tm // CONV_HALO
    last = t // CONV_HALO - 1

    def kern(dcv_ref, dnext_ref, glu_ref, gprev_ref, a_ref, b_ref, w_ref, da_ref, db_ref, dw_ref,
             dext, gext, dshift, gshift):
        i = pl.program_id(0)

        @pl.when(i == 0)
        def _():
            dw_ref[...] = jnp.zeros_like(dw_ref)

        dext[0:tm, :] = dcv_ref[...]
        dext[tm:tm + CONV_HALO, :] = jnp.where(i < t // tm - 1, dnext_ref[...], 0.0)
        dext[tm + CONV_HALO:, :] = jnp.zeros((8, c), F32)
        gext[0:CONV_HALO, :] = jnp.where(i > 0, gprev_ref[...].astype(F32), 0.0)
        gext[CONV_HALO:tm + CONV_HALO, :] = glu_ref[...].astype(F32)
        gext[tm + CONV_HALO:, :] = jnp.zeros((8, c), F32)
        _shift_copies(dext, dshift, tm + CONV_HALO)
        _shift_copies(gext, gshift, tm + CONV_HALO)
        for r0 in range(0, tm, CONV_ROWS):
            acc = jnp.zeros((CONV_ROWS, c), F32)
            for k in range(CONV_WIDTH):
                acc = acc + _shifted_rows(dshift, r0 + CONV_WIDTH - 1 - k, CONV_ROWS) * w_ref[k:k + 1, :]
            av = a_ref[r0:r0 + CONV_ROWS, :].astype(F32)
            sig = _sigmoid(b_ref[r0:r0 + CONV_ROWS, :].astype(F32))
            da_ref[r0:r0 + CONV_ROWS, :] = _bf(acc * sig)
            db_ref[r0:r0 + CONV_ROWS, :] = _bf(acc * av * sig * (1.0 - sig))
        for k in range(CONV_WIDTH):
            acc = jnp.zeros((CONV_ROWS, c), F32)
            for r0 in range(0, tm, CONV_ROWS):
                acc = acc + _shifted_rows(gshift, r0 + 2 + k, CONV_ROWS) * dext[r0:r0 + CONV_ROWS, :]
            dw_ref[k:k + 1, :] += jnp.sum(acc, axis=0, keepdims=True)

    row = pl.BlockSpec((tm, c), lambda i: (i, 0))
    shape = jax.ShapeDtypeStruct((t, c), BF16)
    return pl.pallas_call(
        kern, name="conv_bwd_dw", grid=(t // tm,),
        in_specs=[row, pl.BlockSpec((CONV_HALO, c), lambda i: (jnp.minimum((i + 1) * hb, last), 0)),
                  row, pl.BlockSpec((CONV_HALO, c), lambda i: (jnp.maximum(i * hb - 1, 0), 0)),
                  row, row, pl.BlockSpec((CONV_HALO, c), lambda i: (0, 0))],
        out_specs=[row, row, pl.BlockSpec((CONV_HALO, c), lambda i: (0, 0))],
        out_shape=[shape, shape, jax.ShapeDtypeStruct((CONV_HALO, c), F32)],
        scratch_shapes=[pltpu.VMEM((tm + CONV_HALO + 8, c), F32), pltpu.VMEM((tm + CONV_HALO + 8, c), F32),
                        pltpu.VMEM((8, tm + CONV_HALO, c), F32), pltpu.VMEM((8, tm + CONV_HALO, c), F32)],
        compiler_params=_params("arbitrary"))(dcv, dcv, glu, glu, a, b, w_dw)


def _rope_tables(pos):
    t = pos.shape[0]
    tm = min(512, t)
    freq = (np.float32(ROPE_THETA) ** (np.float32(-2.0) * np.arange(ROPE // 2, dtype=np.float32)
                                       / np.float32(ROPE))).astype(np.float32)
    row = np.zeros((2, 128), np.float32)
    row[0, :ROPE] = np.concatenate([freq, freq])
    row[1, :ROPE] = 1.0

    def kern(pos_ref, f_ref, c_ref, s_ref):
        ang = pos_ref[...].astype(F32) * f_ref[0:1, :]
        mask = f_ref[1:2, :]
        c_ref[...] = jnp.cos(ang) * mask
        s_ref[...] = jnp.sin(ang) * mask

    out = pl.BlockSpec((tm, 128), lambda i: (i, 0))
    shape = jax.ShapeDtypeStruct((t, 128), F32)
    return pl.pallas_call(
        kern, name="rope_tables", grid=(t // tm,),
        in_specs=[pl.BlockSpec((tm, 1), lambda i: (i, 0)), pl.BlockSpec((2, 128), lambda i: (0, 0))],
        out_specs=[out, out], out_shape=[shape, shape], compiler_params=_params("parallel"))(pos, jnp.asarray(row))


def _mla_prep(a, gq, gkv, cs_c, cs_s):
    t = a.shape[0]
    tm = min(512, t)
    kv0, r0 = Q_LORA, Q_LORA + KV_LORA

    def kern(a_ref, gq_ref, gkv_ref, c_ref, s_ref, cq_ref, ckv_ref, kr_ref):
        aq = a_ref[:, 0:kv0]
        akv = a_ref[:, kv0:r0]
        ar = a_ref[:, r0:A_PAD]
        cq_ref[...] = _bf(aq * _rstd(aq) * gq_ref[...])
        ckv_ref[...] = _bf(akv * _rstd(akv) * gkv_ref[...])
        kr_ref[...] = _bf(ar * c_ref[...] + _rot(ar) * s_ref[...])

    def row(w):
        return pl.BlockSpec((tm, w), lambda i: (i, 0))

    def vec(w):
        return pl.BlockSpec((1, w), lambda i: (0, 0))

    return pl.pallas_call(
        kern, name="mla_prep", grid=(t // tm,),
        in_specs=[row(A_PAD), vec(Q_LORA), vec(KV_LORA), row(128), row(128)],
        out_specs=[row(Q_LORA), row(KV_LORA), row(128)],
        out_shape=[jax.ShapeDtypeStruct((t, Q_LORA), BF16), jax.ShapeDtypeStruct((t, KV_LORA), BF16),
                   jax.ShapeDtypeStruct((t, 128), BF16)],
        compiler_params=_params("parallel"))(a, gq.reshape(1, -1), gkv.reshape(1, -1), cs_c, cs_s)


def _mla_prep_bwd(a, dcq, dckv, dar, gq, gkv):
    t = a.shape[0]
    tm = min(512, t)
    kv0, r0 = Q_LORA, Q_LORA + KV_LORA

    def kern(a_ref, dcq_ref, dckv_ref, dar_ref, gq_ref, gkv_ref, da_ref, dgq_ref, dgkv_ref):
        @pl.when(pl.program_id(0) == 0)
        def _():
            dgq_ref[...] = jnp.zeros_like(dgq_ref)
            dgkv_ref[...] = jnp.zeros_like(dgkv_ref)

        def back(x, dy, g_ref, dg_ref):
            rstd = _rstd(x)
            xhat = x * rstd
            dg_ref[...] += jnp.broadcast_to(jnp.sum(dy * xhat, axis=0, keepdims=True), dg_ref.shape)
            dxh = dy * g_ref[...]
            return rstd * (dxh - xhat * jnp.mean(dxh * xhat, axis=-1, keepdims=True))

        da_ref[:, 0:kv0] = _bf(back(a_ref[:, 0:kv0], dcq_ref[...], gq_ref, dgq_ref))
        da_ref[:, kv0:r0] = _bf(back(a_ref[:, kv0:r0], dckv_ref[...], gkv_ref, dgkv_ref))
        da_ref[:, r0:A_PAD] = _bf(dar_ref[...])

    def row(w):
        return pl.BlockSpec((tm, w), lambda i: (i, 0))

    def vec(r, w):
        return pl.BlockSpec((r, w), lambda i: (0, 0))

    return pl.pallas_call(
        kern, name="mla_prep_bwd", grid=(t // tm,),
        in_specs=[row(A_PAD), row(Q_LORA), row(KV_LORA), row(128), vec(1, Q_LORA), vec(1, KV_LORA)],
        out_specs=[row(A_PAD), vec(8, Q_LORA), vec(8, KV_LORA)],
        out_shape=[jax.ShapeDtypeStruct((t, A_PAD), BF16), jax.ShapeDtypeStruct((8, Q_LORA), F32),
                   jax.ShapeDtypeStruct((8, KV_LORA), F32)],
        compiler_params=_params("arbitrary"))(a, dcq, dckv, dar, gq.reshape(1, -1), gkv.reshape(1, -1))


def _mla_qkv(cq, ckv, kr, cs_c, cs_s, wuq, wukv):
    t = cq.shape[0]
    tm = min(512, t)
    kvw = NOPE + V_HEAD

    def kern(cq_ref, ckv_ref, kr_ref, c_ref, s_ref, wq_ref, wkv_ref, q_ref, k_ref, v_ref):
        r = jnp.dot(cq_ref[...], wq_ref[...], preferred_element_type=F32)
        kv = jnp.dot(ckv_ref[...], wkv_ref[...], preferred_element_type=F32)
        cv, sv, krv = c_ref[...], s_ref[...], kr_ref[...]
        for h in range(HEADS):
            xr = r[:, h * HEAD_PAD + NOPE:(h + 1) * HEAD_PAD]
            q_ref[h, :, 0:NOPE] = _bf(r[:, h * HEAD_PAD:h * HEAD_PAD + NOPE] * ATTN_SCALE)
            q_ref[h, :, NOPE:] = _bf((xr * cv + _rot(xr) * sv) * ATTN_SCALE)
            k_ref[h, :, 0:NOPE] = _bf(kv[:, h * kvw:h * kvw + NOPE])
            k_ref[h, :, NOPE:] = krv
            v_ref[h] = _bf(kv[:, h * kvw + NOPE:(h + 1) * kvw])

    def row(w):
        return pl.BlockSpec((tm, w), lambda i: (i, 0))

    def heads(w):
        return pl.BlockSpec((HEADS, tm, w), lambda i: (0, i, 0))

    return pl.pallas_call(
        kern, name="mla_qkv", grid=(t // tm,),
        in_specs=[row(Q_LORA), row(KV_LORA), row(128), row(128), row(128),
                  pl.BlockSpec(wuq.shape, lambda i: (0, 0)), pl.BlockSpec(wukv.shape, lambda i: (0, 0))],
        out_specs=[heads(HEAD_PAD), heads(HEAD_PAD), heads(V_HEAD)],
        out_shape=[jax.ShapeDtypeStruct((HEADS, t, HEAD_PAD), BF16), jax.ShapeDtypeStruct((HEADS, t, HEAD_PAD), BF16),
                   jax.ShapeDtypeStruct((HEADS, t, V_HEAD), BF16)],
        compiler_params=_params("parallel"))(cq, ckv, kr, cs_c, cs_s, wuq, wukv)


def _mla_qkv_bwd(dq, dk, dv, cs_c, cs_s, wuq, wukv):
    t = dq.shape[1]
    tm = min(256, t)
    kvw = NOPE + V_HEAD

    def kern(dq_ref, dk_ref, dv_ref, c_ref, s_ref, wq_ref, wkv_ref, dr_ref, dkv_ref, dcq_ref, dckv_ref, dar_ref):
        cv, sv = c_ref[...], s_ref[...]
        dar = jnp.zeros_like(cv)
        for h in range(HEADS):
            dqx = dq_ref[h, :, NOPE:]
            dr_ref[:, h * HEAD_PAD:h * HEAD_PAD + NOPE] = _bf(dq_ref[h, :, 0:NOPE] * ATTN_SCALE)
            dr_ref[:, h * HEAD_PAD + NOPE:(h + 1) * HEAD_PAD] = _bf((dqx * cv + _rot_t(dqx * sv)) * ATTN_SCALE)
            dkx = dk_ref[h, :, NOPE:]
            dar = dar + (dkx * cv + _rot_t(dkx * sv))
            dkv_ref[:, h * kvw:h * kvw + NOPE] = _bf(dk_ref[h, :, 0:NOPE])
            dkv_ref[:, h * kvw + NOPE:(h + 1) * kvw] = _bf(dv_ref[h])
        dar_ref[...] = dar
        dcq_ref[...] = lax.dot_general(dr_ref[...], wq_ref[...], (((1,), (1,)), ((), ())),
                                       preferred_element_type=F32)
        dckv_ref[...] = lax.dot_general(dkv_ref[...], wkv_ref[...], (((1,), (1,)), ((), ())),
                                        preferred_element_type=F32)

    def row(w):
        return pl.BlockSpec((tm, w), lambda i: (i, 0))

    def heads(w):
        return pl.BlockSpec((HEADS, tm, w), lambda i: (0, i, 0))

    return pl.pallas_call(
        kern, name="mla_qkv_bwd", grid=(t // tm,),
        in_specs=[heads(HEAD_PAD), heads(HEAD_PAD), heads(V_HEAD), row(128), row(128),
                  pl.BlockSpec(wuq.shape, lambda i: (0, 0)), pl.BlockSpec(wukv.shape, lambda i: (0, 0))],
        out_specs=[row(HEADS * HEAD_PAD), row(HEADS * kvw), row(Q_LORA), row(KV_LORA), row(128)],
        out_shape=[jax.ShapeDtypeStruct((t, HEADS * HEAD_PAD), BF16), jax.ShapeDtypeStruct((t, HEADS * kvw), BF16),
                   jax.ShapeDtypeStruct((t, Q_LORA), F32), jax.ShapeDtypeStruct((t, KV_LORA), F32),
                   jax.ShapeDtypeStruct((t, 128), F32)],
        compiler_params=_params("parallel"))(dq, dk, dv, cs_c, cs_s, wuq, wukv)


def _attn_block(t):
    return 512 if t >= 4096 else 128


def _fold_rows(x, op):
    r = x.shape[0]
    while r > 8:
        r //= 2
        x = op(x[:r], x[r:])
    return x


def _chunk_mask(bk, bq):
    kc = lax.broadcasted_iota(jnp.int32, (bk, bq), 0) // CHUNK
    qc = lax.broadcasted_iota(jnp.int32, (bk, bq), 1) // CHUNK
    return qc >= kc


def _flash_fwd(q, k, v):
    t = q.shape[1]
    bq = _attn_block(t)
    nq = t // bq

    def kern(q_ref, k_ref, v_ref, o_ref, lse_ref):
        i = pl.program_id(1)
        qa, qb = q_ref[0:bq, :], q_ref[bq:2 * bq, :]

        def block(j):
            rows = pl.ds(pl.multiple_of(j * bq, bq), bq)
            return k_ref[rows, :], v_ref[rows, :]

        def scores(kj, qv):
            return lax.dot_general(kj, qv, (((1,), (1,)), ((), ())), preferred_element_type=F32)

        def update(st, vj, m, l, acc):
            m_new = jnp.maximum(m, jnp.max(_fold_rows(st, jnp.maximum), axis=0, keepdims=True))
            alpha = jnp.exp(m - m_new)
            p = jnp.exp(st - m_new)
            pv = lax.dot_general(vj, _bf(p), (((0,), (0,)), ((), ())), preferred_element_type=F32)
            return m_new, alpha * l + jnp.sum(_fold_rows(p, jnp.add), axis=0, keepdims=True), alpha * acc + pv

        start = (jnp.full((1, bq), -1e30, F32), jnp.zeros((1, bq), F32), jnp.zeros((V_HEAD, bq), F32))
        mask = _chunk_mask(bq, bq)
        k0, v0 = block(2 * i)
        k1, v1 = block(2 * i + 1)
        ca = update(jnp.where(mask, scores(k0, qa), -1e30), v0, *start)
        cb = update(scores(k0, qb), v0, *start)
        cb = update(jnp.where(mask, scores(k1, qb), -1e30), v1, *cb)

        def body(j, carry):
            sa, sb, ca, cb = carry
            kn = block(jnp.minimum(j + 1, jnp.maximum(2 * i - 1, 0)))[0]
            sa_next, sb_next = scores(kn, qa), scores(kn, qb)
            vj = block(j)[1]
            return sa_next, sb_next, update(sa, vj, *ca), update(sb, vj, *cb)

        kf = block(0)[0]
        _, _, ca, cb = lax.fori_loop(0, 2 * i, body, (scores(kf, qa), scores(kf, qb), ca, cb))
        for half, (m, l, acc) in enumerate((ca, cb)):
            o_ref[half * bq:(half + 1) * bq, :] = _bf((acc / l).T)
            lse_ref[half] = jnp.broadcast_to(m + jnp.log(l), (8, bq))

    return pl.pallas_call(
        kern, name="flash_fwd", grid=(HEADS, nq // 2),
        in_specs=[pl.BlockSpec((None, 2 * bq, HEAD_PAD), lambda h, i: (h, i, 0)),
                  pl.BlockSpec((None, t, HEAD_PAD), lambda h, i: (h, 0, 0)),
                  pl.BlockSpec((None, t, V_HEAD), lambda h, i: (h, 0, 0))],
        out_specs=[pl.BlockSpec((2 * bq, V_HEAD), lambda h, i: (i, h)),
                   pl.BlockSpec((None, 2, 8, bq), lambda h, i: (h, i, 0, 0))],
        out_shape=[jax.ShapeDtypeStruct((t, HEADS * V_HEAD), BF16), jax.ShapeDtypeStruct((HEADS, nq, 8, bq), F32)],
        compiler_params=_params("parallel", "arbitrary"))(q, k, v)


def _attn_delta(do, o):
    t = do.shape[0]
    bq = _attn_block(t)

    def kern(do_ref, o_ref, d_ref):
        for h in range(HEADS):
            cols = slice(h * V_HEAD, (h + 1) * V_HEAD)
            prod = do_ref[:, cols].astype(F32) * o_ref[:, cols].astype(F32)
            d_ref[h] = jnp.broadcast_to(jnp.sum(prod.T, axis=0, keepdims=True), (8, bq))

    blk = pl.BlockSpec((bq, HEADS * V_HEAD), lambda i: (i, 0))
    return pl.pallas_call(
        kern, name="attn_delta", grid=(t // bq,), in_specs=[blk, blk],
        out_specs=pl.BlockSpec((HEADS, None, 8, bq), lambda i: (0, i, 0, 0)),
        out_shape=jax.ShapeDtypeStruct((HEADS, t // bq, 8, bq), F32),
        compiler_params=_params("parallel"))(do, o)


def _flash_bwd(q, k, v, do, lse, delta):
    t = q.shape[1]
    bq = _attn_block(t)
    nq = t // bq

    def kern(q_ref, k_ref, v_ref, do_ref, lse_ref, del_ref, dq_ref, dk_ref, dv_ref):
        j = pl.program_id(1)

        @pl.when(j == 0)
        def _():
            dq_ref[...] = jnp.zeros_like(dq_ref)

        dk_ref[...] = jnp.zeros_like(dk_ref)
        dv_ref[...] = jnp.zeros_like(dv_ref)
        kj, vj = k_ref[...], v_ref[...]

        def step(i, masked):
            rows = pl.ds(pl.multiple_of(i * bq, bq), bq)
            qi, doi = q_ref[rows, :], do_ref[rows, :]
            st = lax.dot_general(kj, qi, (((1,), (1,)), ((), ())), preferred_element_type=F32)
            pt = jnp.exp(st - lse_ref[i][0:1, :])
            if masked:
                pt = jnp.where(_chunk_mask(bq, bq), pt, 0.0)
            dpt = lax.dot_general(vj, doi, (((1,), (1,)), ((), ())), preferred_element_type=F32)
            dst = _bf(pt * (dpt - del_ref[i][0:1, :]))
            dv_ref[...] += jnp.dot(_bf(pt), doi, preferred_element_type=F32)
            dk_ref[...] += jnp.dot(dst, qi, preferred_element_type=F32)
            dq_ref[rows, :] += lax.dot_general(dst, kj, (((0,), (0,)), ((), ())), preferred_element_type=F32)

        step(j, True)

        def body(i, carry):
            step(i, False)
            return carry

        lax.fori_loop(j + 1, nq, body, 0)

    stat = pl.BlockSpec((None, nq, 8, bq), lambda h, j: (h, 0, 0, 0))
    return pl.pallas_call(
        kern, name="flash_bwd", grid=(HEADS, nq),
        in_specs=[pl.BlockSpec((None, t, HEAD_PAD), lambda h, j: (h, 0, 0)),
                  pl.BlockSpec((None, bq, HEAD_PAD), lambda h, j: (h, j, 0)),
                  pl.BlockSpec((None, bq, V_HEAD), lambda h, j: (h, j, 0)),
                  pl.BlockSpec((t, V_HEAD), lambda h, j: (0, h)), stat, stat],
        out_specs=[pl.BlockSpec((None, t, HEAD_PAD), lambda h, j: (h, 0, 0)),
                   pl.BlockSpec((None, bq, HEAD_PAD), lambda h, j: (h, j, 0)),
                   pl.BlockSpec((None, bq, V_HEAD), lambda h, j: (h, j, 0))],
        out_shape=[jax.ShapeDtypeStruct((HEADS, t, HEAD_PAD), F32), jax.ShapeDtypeStruct((HEADS, t, HEAD_PAD), F32),
                   jax.ShapeDtypeStruct((HEADS, t, V_HEAD), F32)],
        compiler_params=_params("parallel", "arbitrary"))(q, k, v, do, lse, delta)


def _place():
    x, y, c = lax.axis_index("x"), lax.axis_index("y"), lax.axis_index("c")
    return x, y, c, [(1 - x, y), (x, 1 - y), (1 - x, 1 - y)]


def _all_gather_rows(block, name):
    m_per, n = block.shape

    def body(x_ref, out_ref, send_sems, recv_sems, local_sem):
        x, y, c, chips = _place()
        me, sibling = (x, y, c), (x, y, 1 - c)

        def rows(px, py, pc):
            return out_ref.at[pl.ds((4 * px + 2 * py + pc) * m_per, m_per), :]

        def copy(k, blk, to, src=None):
            return pltpu.make_async_remote_copy(
                src_ref=rows(*blk) if src is None else src, dst_ref=rows(*blk), send_sem=send_sems.at[k],
                recv_sem=recv_sems.at[k], device_id=to, device_id_type=MESH)

        mine = pltpu.make_async_copy(x_ref, rows(*me), local_sem)
        mine.start()
        first = [copy(0, me, sibling, src=x_ref)]
        first += [copy(1 + j, me, (*chip, c), src=x_ref) for j, chip in enumerate(chips)]
        for cp in first:
            cp.start()
        passed = [copy(4 + j, (*chip, c), sibling) for j, chip in enumerate(chips)]
        for j, chip in enumerate(chips):
            copy(1 + j, (*chip, c), me).wait_recv()
            passed[j].start()
        copy(0, sibling, me).wait_recv()
        for j, chip in enumerate(chips):
            copy(4 + j, (*chip, 1 - c), me).wait_recv()
        for cp in first + passed:
            cp.wait_send()
        mine.wait()

    return pl.pallas_call(
        body, name=name, out_shape=jax.ShapeDtypeStruct((8 * m_per, n), block.dtype),
        in_specs=[pl.BlockSpec(memory_space=pltpu.VMEM)], out_specs=pl.BlockSpec(memory_space=pltpu.VMEM),
        scratch_shapes=[pltpu.SemaphoreType.DMA((7,)), pltpu.SemaphoreType.DMA((7,)), pltpu.SemaphoreType.DMA],
        compiler_params=pltpu.CompilerParams(vmem_limit_bytes=VMEM_LIMIT_BYTES))(block)


HBM_SPEC = pl.BlockSpec(memory_space=pltpu.HBM)
SEM_SPEC = pl.BlockSpec(memory_space=pltpu.SEMAPHORE)
DATAFLOW = pltpu.SideEffectType.DATAFLOW_SIDE_EFFECTING


def _in_hbm(a):
    return pltpu.with_memory_space_constraint(a, pltpu.HBM)


def _chip_copies(ins, lands, send_sems, recv_sems, src_slot):
    n = len(ins)
    x, y, c, chips = _place()
    me = 2 * x + y
    return [pltpu.make_async_remote_copy(
        src_ref=ins[w].at[2 * chip[0] + chip[1]] if src_slot else ins[w], dst_ref=lands[w].at[me],
        send_sem=send_sems.at[p * n + w], recv_sem=recv_sems.at[p * n + w], device_id=(*chip, c),
        device_id_type=MESH) for w in range(n) for p, chip in enumerate(chips)]


def _exchange_start(srcs, lands, src_slot, name, dep=None):
    n = len(srcs)
    first_out = 2 * n + (dep is not None)

    def body(*refs):
        for cp in _chip_copies(refs[:n], refs[n:2 * n], refs[first_out], refs[first_out + 1], src_slot):
            cp.start()
        token = refs[-1]
        token[...] = jnp.zeros_like(token)

    thru = [pltpu.HBM(a.shape, a.dtype) for a in list(srcs) + list(lands)]
    res = pl.pallas_call(
        body, name=name,
        out_shape=(pltpu.SemaphoreType.DMA((3 * n,)), pltpu.SemaphoreType.DMA((3 * n,)), *thru,
                   jax.ShapeDtypeStruct((8, 128), F32)),
        in_specs=[HBM_SPEC] * (2 * n) + ([pl.BlockSpec(memory_space=pl.ANY)] if dep is not None else []),
        out_specs=(SEM_SPEC, SEM_SPEC, *[HBM_SPEC] * (2 * n), pl.BlockSpec(memory_space=pltpu.VMEM)),
        input_output_aliases={i: 2 + i for i in range(2 * n)},
        compiler_params=pltpu.CompilerParams(has_side_effects=DATAFLOW))(
            *[_in_hbm(a) for a in srcs], *[_in_hbm(a) for a in lands], *([dep] if dep is not None else []))
    return (res[0], res[1], list(res[2:2 + n]), list(res[2 + n:2 + 2 * n])), res[-1]


def _exchange_wait(flight, after, src_slot, name):
    send_sems, recv_sems, srcs, lands = flight
    n = len(srcs)

    def body(*refs):
        for cp in _chip_copies(refs[:n], refs[n:2 * n], refs[2 * n], refs[2 * n + 1], src_slot):
            cp.wait_send()
            cp.wait_recv()

    thru = [pltpu.HBM(a.shape, a.dtype) for a in list(srcs) + list(lands)]
    res = pl.pallas_call(
        body, name=name, out_shape=thru,
        in_specs=[HBM_SPEC] * (2 * n) + [SEM_SPEC, SEM_SPEC, pl.BlockSpec(memory_space=pl.ANY)],
        out_specs=[HBM_SPEC] * (2 * n), input_output_aliases={i: i for i in range(2 * n)},
        compiler_params=pltpu.CompilerParams(has_side_effects=DATAFLOW))(*srcs, *lands, send_sems, recv_sems, after)
    return list(res[n:])


def _landing(own, me):
    return lax.dynamic_update_index_in_dim(lax.empty((4, *own.shape), own.dtype), own, me, 0)


def _swap_with_sibling(arrays, name):
    n = len(arrays)

    def body(*refs):
        ins, outs = refs[:n], refs[n:2 * n]
        send_sems, recv_sems = refs[2 * n:]
        x, y, c, _ = _place()
        copies = [pltpu.make_async_remote_copy(src_ref=ins[w], dst_ref=outs[w], send_sem=send_sems.at[w],
                                               recv_sem=recv_sems.at[w], device_id=(x, y, 1 - c), device_id_type=MESH)
                  for w in range(n)]
        for cp in copies:
            cp.start()
        for cp in copies:
            cp.wait()

    any_spec = pl.BlockSpec(memory_space=pl.ANY)
    return pl.pallas_call(
        body, name=name, out_shape=[jax.ShapeDtypeStruct(a.shape, a.dtype) for a in arrays],
        in_specs=[any_spec] * n, out_specs=[any_spec] * n,
        scratch_shapes=[pltpu.SemaphoreType.DMA((n,)), pltpu.SemaphoreType.DMA((n,))])(*arrays)


def _as_rows(a):
    return a.reshape(-1, a.shape[-1])


def _row_tile(r, c, budget_bytes=1 << 20):
    tr = r
    while tr % 16 == 0 and tr * c * 4 > budget_bytes:
        tr //= 2
    return tr


def _sum_slots(layers, nlayer, name, into=None):
    _, r, c = layers[0][1].shape
    tr = _row_tile(r, c)
    nt = r // tr
    acc = into
    for l, r4 in layers:
        def kern(r_ref, *rest):
            o_ref = rest[-1]
            o_ref[...] = (((r_ref[0].astype(F32) + r_ref[1].astype(F32)) + r_ref[2].astype(F32))
                          + r_ref[3].astype(F32))

        out_spec = pl.BlockSpec((tr, c), lambda i, l=l: (l * nt + i, 0))
        first = acc is None
        acc = pl.pallas_call(
            kern, name=f"{name}_l{l}", grid=(nt,),
            in_specs=[pl.BlockSpec((4, tr, c), lambda i: (0, i, 0))]
            + ([] if first else [pl.BlockSpec(memory_space=pl.ANY)]),
            out_specs=out_spec, out_shape=jax.ShapeDtypeStruct((nlayer * r, c), F32),
            input_output_aliases={} if first else {1: 0},
            compiler_params=_params("parallel"))(*([r4] if first else [r4, acc]))
    return acc


def _adamw(w, m, v, parts, name):
    r, c = w.shape
    tr = _row_tile(r, c, 3 << 19)
    npart = len(parts)
    c1 = 1.0 - ADAM_B1 ** ADAM_STEP
    c2 = 1.0 - ADAM_B2 ** ADAM_STEP

    def kern(*refs):
        w_ref, m_ref, v_ref = refs[:3]
        p_refs = refs[3:3 + npart]
        g_ref, d_ref, mo_ref, vo_ref = refs[3 + npart:]
        g = p_refs[0][...]
        for p in p_refs[1:]:
            g = g + p[...]
        mn = ADAM_B1 * m_ref[...] + (1.0 - ADAM_B1) * g
        vn = ADAM_B2 * v_ref[...] + (1.0 - ADAM_B2) * (g * g)
        g_ref[...] = g
        mo_ref[...] = mn
        vo_ref[...] = vn
        d_ref[...] = -ADAM_LR * ((mn / c1) / (jnp.sqrt(vn / c2) + ADAM_EPS) + ADAM_WD * w_ref[...])

    blk = pl.BlockSpec((tr, c), lambda i: (i, 0))
    shape = jax.ShapeDtypeStruct((r, c), F32)
    return pl.pallas_call(
        kern, name=name, grid=(r // tr,), in_specs=[blk] * (3 + npart), out_specs=[blk] * 4, out_shape=[shape] * 4,
        compiler_params=_params("parallel"))(w, m, v, *parts)


def _sum_devices(g8, name):
    _, r, c = g8.shape

    def kern(g_ref, o_ref):
        tot = g_ref[0]
        for dev in range(1, 8):
            tot = tot + g_ref[dev]
        o_ref[...] = tot

    return pl.pallas_call(
        kern, name=name, grid=(1,), in_specs=[pl.BlockSpec((8, r, c), lambda i: (0, 0, 0))],
        out_specs=pl.BlockSpec((r, c), lambda i: (0, 0)), out_shape=jax.ShapeDtypeStruct((r, c), F32),
        compiler_params=_params("arbitrary"))(g8)


def _pad_lanes(a, width):
    return jnp.pad(a, [(0, 0)] * (a.ndim - 1) + [(0, width - a.shape[-1])])


def kernel(x, positions, ffn_norm1, ffn1_w1, ffn1_w3, ffn1_w2, mix_norm, ffn_norm2, ffn2_w1, ffn2_w3, ffn2_w2, conv_w_pw1, conv_w_dw, conv_norm, conv_w_pw2, mla_w_a, mla_q_norm, mla_kv_norm, mla_w_uq, mla_w_ukv, mla_w_o, final_norm, loss_target, m_ffn_norm1, m_ffn1_w1, m_ffn1_w3, m_ffn1_w2, m_mix_norm, m_ffn_norm2, m_ffn2_w1, m_ffn2_w3, m_ffn2_w2, m_conv_w_pw1, m_conv_w_dw, m_conv_norm, m_conv_w_pw2, m_mla_w_a, m_mla_q_norm, m_mla_kv_norm, m_mla_w_uq, m_mla_w_ukv, m_mla_w_o, m_final_norm, v_ffn_norm1, v_ffn1_w1, v_ffn1_w3, v_ffn1_w2, v_mix_norm, v_ffn_norm2, v_ffn2_w1, v_ffn2_w3, v_ffn2_w2, v_conv_w_pw1, v_conv_w_dw, v_conv_norm, v_conv_w_pw2, v_mla_w_a, v_mla_q_norm, v_mla_kv_norm, v_mla_w_uq, v_mla_w_ukv, v_mla_w_o, v_final_norm):
    given = locals()
    return _step({nm: given[nm] for nm in INPUTS})


def _step(A):
    x = A['x'][0]
    target = A['loss_target'][0]
    t, d = x.shape
    pos = A['positions'].reshape(t, 1)
    me = 2 * lax.axis_index("x") + lax.axis_index("y")

    ffn = [f'ffn{k}_{w}' for k in (1, 2) for w in ('w1', 'w3', 'w2')]
    gather_groups = [[(nm, 0) for nm in ffn[:3]],
                     [('conv_w_pw1', 0), ('conv_w_pw2', 0)] + [(nm, 0) for nm in ffn[3:]],
                     [(nm, 1) for nm in ffn[:3]] + [('mla_w_a', 0), ('mla_w_uq', 0), ('mla_w_ukv', 0), ('mla_w_o', 0)],
                     [(nm, 1) for nm in ffn[3:]]]
    gather_flights = {}
    big = {}

    def gather_start(gi, dep):
        shards = [_bf(A[nm][l]) for nm, l in gather_groups[gi]]
        gather_flights[gi], token = _exchange_start(shards, [_landing(s, me) for s in shards], False,
                                                    f"gather_start_{gi}", dep)
        return token

    def gather_wait(gi, after):
        landed = _exchange_wait(gather_flights[gi], after, False, f"gather_wait_{gi}")
        big.update(zip(gather_groups[gi], landed))
        return landed[0]

    dw_shard = A['conv_w_dw'][0]
    cw = dw_shard.shape[1]
    small = jnp.concatenate([
        jnp.pad(dw_shard, ((0, CONV_HALO - CONV_WIDTH), (0, 0))),
        jnp.pad(_pad_lanes(A['mla_q_norm'], cw), ((0, 7), (0, 0))),
        jnp.pad(_pad_lanes(A['mla_kv_norm'], cw), ((0, 7), (0, 0)))], axis=0)
    small = _all_gather_rows(small, "gather_small_weights").reshape(4, 2, 48, cw)[:, 0]
    w_dw = jnp.concatenate([small[j, :CONV_HALO] for j in range(4)], axis=1)
    gq = jnp.concatenate([small[j, CONV_HALO, :Q_LORA // 4] for j in range(4)])
    gkv = jnp.concatenate([small[j, CONV_HALO + 8, :KV_LORA // 4] for j in range(4)])

    def cols(nm, layer):
        return jnp.concatenate([big[nm, layer][j] for j in range(4)], axis=1)

    def rows(nm, layer):
        g = big[nm, layer]
        return g.reshape(-1, g.shape[-1])

    ffn_w = {}

    def ffn_weights(k, l):
        ffn_w[k, l] = (cols(f'ffn{k}_w1', l), cols(f'ffn{k}_w3', l), rows(f'ffn{k}_w2', l))
        return ffn_w[k, l]

    token = gather_start(0, None)
    cs_c, cs_s = _rope_tables(pos)
    h0 = x
    token = gather_start(1, gather_wait(0, token))
    h1, n01, z01a, z01b = _ffn_fwd(h0, A['ffn_norm1'][0], *ffn_weights(1, 0), token, "ffn1_l0_fwd")
    token = gather_start(3, gather_start(2, gather_wait(1, h1)))
    pw1 = big['conv_w_pw1', 0]
    pw1_a = jnp.concatenate([pw1[0], pw1[1]], axis=1)
    pw1_b = jnp.concatenate([pw1[2], pw1[3]], axis=1)
    pw2 = rows('conv_w_pw2', 0)
    m0 = _norm_fwd(h1, A['mix_norm'][0], token, "mix_norm_l0")
    ca, cb, glu = _glu_fwd(m0, pw1_a, pw1_b)
    cv, cs = _conv_fwd(glu, w_dw, A['conv_norm'][0])
    h2 = _mm([(cs, pw2)], F32, "conv_pw2_fwd", res=h1)
    h3, n02, z02a, z02b = _ffn_fwd(h2, A['ffn_norm2'][0], *ffn_weights(2, 0), token, "ffn2_l0_fwd")
    gather_wait(2, h3)
    w_a = _pad_lanes(rows('mla_w_a', 0), A_PAD)
    wuq = _pad_lanes(big['mla_w_uq', 0].reshape(Q_LORA, HEADS, NOPE + ROPE), HEAD_PAD).reshape(Q_LORA, -1)
    wukv = big['mla_w_ukv', 0].reshape(KV_LORA, HEADS * (NOPE + V_HEAD))
    w_o = rows('mla_w_o', 0)
    h4, n11, z11a, z11b = _ffn_fwd(h3, A['ffn_norm1'][1], *ffn_weights(1, 1), token, "ffn1_l1_fwd")
    m1 = _norm_fwd(h4, A['mix_norm'][1], token, "mix_norm_l1")
    a_lat = _mm([(m1, w_a)], F32, "mla_down_fwd")
    cq, ckv, kr = _mla_prep(a_lat, gq, gkv, cs_c, cs_s)
    q, k, v = _mla_qkv(cq, ckv, kr, cs_c, cs_s, wuq, wukv)
    o, lse = _flash_fwd(q, k, v)
    h5 = _mm([(o, w_o)], F32, "mla_out_fwd", res=h4)
    gather_wait(3, h5)
    h6, n12, z12a, z12b = _ffn_fwd(h5, A['ffn_norm2'][1], *ffn_weights(2, 1), token, "ffn2_l1_fwd")

    def col_slots(g):
        r, c4 = g.shape
        return g.reshape(r, 4, c4 // 4).transpose(1, 0, 2)

    def row_slots(g):
        return g.reshape(4, g.shape[0] // 4, g.shape[1])

    scatter_flights = []

    def scatter_start(named):
        srcs = [g for _, g in named]
        lands = [_landing(lax.dynamic_index_in_dim(g, me, 0, keepdims=False), me) for g in srcs]
        flight, token = _exchange_start(srcs, lands, True, f"scatter_start_{len(scatter_flights)}")
        scatter_flights.append(([key for key, _ in named], flight))
        return token

    def send_ffn(k, l, dw1, dw3, dw2):
        return scatter_start([((f'ffn{k}_w1', l), col_slots(dw1)), ((f'ffn{k}_w3', l), col_slots(dw3)),
                              ((f'ffn{k}_w2', l), row_slots(dw2))])

    dh6, dg_final, loss_part = _loss_bwd(h6, target, A['final_norm'])
    dh5, dg_n2_l1, *dws = _ffn_bwd(dh6, h5, A['ffn_norm2'][1], n12, z12a, z12b, *ffn_w[2, 1], loss_part, "ffn2_l1")
    token = send_ffn(2, 1, *dws)

    do = _mm([(dh5, w_o)], BF16, "mla_out_bwd", trans_b=True, dep=token)
    dw_o = _mm_tn(o, dh5, BF16, "mla_dw_o")
    delta = _attn_delta(do, o)
    dq, dk, dv = _flash_bwd(q, k, v, do, lse, delta)
    dr, dkv, dcq, dckv, dar = _mla_qkv_bwd(dq, dk, dv, cs_c, cs_s, wuq, wukv)
    dwuq = _mm_tn(cq, dr, BF16, "mla_dw_uq", bn=dr.shape[1] // 2)
    dwukv = _mm_tn(ckv, dkv, BF16, "mla_dw_ukv", bn=dkv.shape[1] // 2)
    da_lat, dgq, dgkv = _mla_prep_bwd(a_lat, dcq, dckv, dar, gq, gkv)
    dw_a = _mm_tn(m1, da_lat, BF16, "mla_dw_a")
    token = scatter_start([
        (('mla_w_a', 0), row_slots(dw_a[:, :Q_LORA + KV_LORA + ROPE])),
        (('mla_w_uq', 0), dwuq.reshape(4, Q_LORA // 4, HEADS, HEAD_PAD)[..., :NOPE + ROPE]),
        (('mla_w_ukv', 0), dwukv.reshape(4, KV_LORA // 4, HEADS, NOPE + V_HEAD)),
        (('mla_w_o', 0), row_slots(dw_o))])
    dh4, dg_mix_l1 = _mm_normbwd([(da_lat, w_a)], h4, A['mix_norm'][1], dh5, token, "mla_down_bwd")

    dh3, dg_n1_l1, *dws = _ffn_bwd(dh4, h3, A['ffn_norm1'][1], n11, z11a, z11b, *ffn_w[1, 1], token, "ffn1_l1")
    token = send_ffn(1, 1, *dws)
    dh2, dg_n2_l0, *dws = _ffn_bwd(dh3, h2, A['ffn_norm2'][0], n02, z02a, z02b, *ffn_w[2, 0], token, "ffn2_l0")
    token = send_ffn(2, 0, *dws)

    dcv, dg_conv = _conv_bwd_norm(dh2, cv, pw2, A['conv_norm'][0], token)
    dw_pw2 = _mm_tn(cs, dh2, BF16, "conv_dw_pw2")
    dca, dcb, ddw = _conv_bwd_dw(dcv, glu, ca, cb, w_dw)
    dpw1_a = _mm_tn(m0, dca, BF16, "conv_dw_pw1a")
    dpw1_b = _mm_tn(m0, dcb, BF16, "conv_dw_pw1b")
    half = dpw1_a.shape[1] // 2
    token = scatter_start([
        (('conv_w_pw1', 0), jnp.stack([dpw1_a[:, :half], dpw1_a[:, half:], dpw1_b[:, :half], dpw1_b[:, half:]])),
        (('conv_w_pw2', 0), row_slots(dw_pw2))])
    dh1, dg_mix_l0 = _mm_normbwd([(dca, pw1_a), (dcb, pw1_b)], h1, A['mix_norm'][0], dh2, token, "conv_pw1_bwd")

    dx, dg_n1_l0, *dws = _ffn_bwd(dh1, h0, A['ffn_norm1'][0], n01, z01a, z01b, *ffn_w[1, 0], token, "ffn1_l0")
    last_sent = send_ffn(1, 0, *dws)
    out = {}

    qkv_row = jnp.concatenate([dgq, dgkv, jnp.zeros((8, d - Q_LORA - KV_LORA), F32)], axis=1)
    loss_row = _pad_lanes(loss_part, d)
    small_g = jnp.concatenate([dg_n1_l0, dg_n1_l1, dg_mix_l0, dg_mix_l1, dg_n2_l0, dg_n2_l1, dg_conv, dg_final,
                               qkv_row, loss_row, ddw], axis=0)
    nrow = small_g.shape[0]
    tot = _sum_devices(_all_gather_rows(small_g, "gather_small_grads").reshape(8, nrow, d), "sum_small_grads")
    loss = tot[72, 0]
    q_shard = lax.dynamic_slice_in_dim(tot[64, :Q_LORA], me * (Q_LORA // 4), Q_LORA // 4)
    kv_shard = lax.dynamic_slice_in_dim(tot[64, Q_LORA:Q_LORA + KV_LORA], me * (KV_LORA // 4), KV_LORA // 4)
    dw_shard_g = lax.dynamic_slice_in_dim(tot[80:80 + CONV_WIDTH], me * cw, cw, axis=1)
    small_grads = {
        'ffn_norm1': jnp.stack([tot[0], tot[8]]), 'mix_norm': jnp.stack([tot[16], tot[24]]),
        'ffn_norm2': jnp.stack([tot[32], tot[40]]), 'conv_norm': tot[48][None], 'final_norm': tot[56],
        'mla_q_norm': q_shard[None], 'mla_kv_norm': kv_shard[None], 'conv_w_dw': dw_shard_g[None],
    }
    for nm, g in small_grads.items():
        res = _adamw(_as_rows(A[nm]) if A[nm].ndim > 1 else A[nm].reshape(1, -1),
                     A['m_' + nm].reshape(-1, A[nm].shape[-1]), A['v_' + nm].reshape(-1, A[nm].shape[-1]),
                     [g.reshape(-1, A[nm].shape[-1])], "adamw_" + nm)
        out[nm] = [r.reshape(A[nm].shape) for r in res]

    received = {}
    after = last_sent

    def scatter_wait(si, after):
        keys, flight = scatter_flights[si]
        landed = _exchange_wait(flight, after, True, f"scatter_wait_{si}")
        received.update(zip(keys, landed))
        return landed[0]

    def slots(nm, l):
        return received[nm, l].reshape(4, -1, received[nm, l].shape[-1])

    def finish(names, sums, tag):
        for nm, mine, theirs in zip(names, sums, _swap_with_sibling(sums, "swap_with_sibling_" + tag)):
            res = _adamw(_as_rows(A[nm]), _as_rows(A['m_' + nm]), _as_rows(A['v_' + nm]), [mine, theirs],
                         "adamw_" + nm)
            out[nm] = [r.reshape(A[nm].shape) for r in res]
        return res[1]

    last = len(scatter_flights) - 1
    for si in range(last):
        after = scatter_wait(si, after)
    late = ffn[:3]
    early = [nm for nm in BIG if nm not in late]
    late_l1 = [_sum_slots([(1, slots(nm, 1))], 2, "sum_" + nm) for nm in late]
    after = finish(early, [_sum_slots([(l, slots(nm, l)) for l in range(A[nm].shape[0])], A[nm].shape[0],
                                      "sum_" + nm) for nm in early], "early")
    scatter_wait(last, after)
    finish(late, [_sum_slots([(0, slots(nm, 0))], 2, "sum_" + nm, into=part) for nm, part in zip(late, late_l1)],
           "late")

    return (loss, dx[None], *[out[nm][0] for nm in WEIGHTS], *[out[nm][1] for nm in WEIGHTS],
            *[out[nm][2] for nm in WEIGHTS], *[out[nm][3] for nm in WEIGHTS])
```

```python
import functools

import jax
import jax.numpy as jnp
import numpy as np
from jax import lax
from jax.experimental import pallas as pl
from jax.experimental.pallas import tpu as pltpu

F32 = jnp.float32
BF16 = jnp.bfloat16
MESH = pl.DeviceIdType.MESH

RMS_EPS = 1e-6
HEADS = 8
NOPE = 128
ROPE = 64
HEAD_PAD = 256
V_HEAD = 128
Q_LORA = 512
KV_LORA = 256
A_PAD = 896
CHUNK = 64
CONV_WIDTH = 31
CONV_HALO = 32
CONV_ROWS = 16
ROPE_THETA = 10000.0
ATTN_SCALE = (NOPE + ROPE) ** -0.5
FFN_RES = 0.5

ADAM_LR = 0.001
ADAM_B1 = 0.9
ADAM_B2 = 0.999
ADAM_EPS = 1e-08
ADAM_WD = 0.01
ADAM_STEP = 10

VMEM_LIMIT_BYTES = 56 * 1024 * 1024

WEIGHTS = ['ffn_norm1', 'ffn1_w1', 'ffn1_w3', 'ffn1_w2', 'mix_norm', 'ffn_norm2', 'ffn2_w1', 'ffn2_w3', 'ffn2_w2',
           'conv_w_pw1', 'conv_w_dw', 'conv_norm', 'conv_w_pw2', 'mla_w_a', 'mla_q_norm', 'mla_kv_norm', 'mla_w_uq',
           'mla_w_ukv', 'mla_w_o', 'final_norm']
INPUTS = (['x', 'positions'] + WEIGHTS + ['loss_target'] + ['m_' + w for w in WEIGHTS] + ['v_' + w for w in WEIGHTS])
BIG = ['ffn1_w1', 'ffn1_w3', 'ffn1_w2', 'ffn2_w1', 'ffn2_w3', 'ffn2_w2', 'conv_w_pw1', 'conv_w_pw2', 'mla_w_a',
       'mla_w_uq', 'mla_w_ukv', 'mla_w_o']


def _params(*sem):
    return pltpu.CompilerParams(dimension_semantics=sem, vmem_limit_bytes=VMEM_LIMIT_BYTES)


def _bf(v):
    return v.astype(BF16)


def _rstd(x):
    return lax.rsqrt(jnp.mean(x * x, axis=-1, keepdims=True) + RMS_EPS)


def _sigmoid(x):
    return jax.nn.sigmoid(x)


def _rot(x):
    lane = lax.broadcasted_iota(jnp.int32, x.shape, 1)
    return jnp.where(lane < ROPE // 2, -pltpu.roll(x, 128 - ROPE // 2, 1), pltpu.roll(x, ROPE // 2, 1))


def _rot_t(y):
    lane = lax.broadcasted_iota(jnp.int32, y.shape, 1)
    return jnp.where(lane < ROPE // 2, pltpu.roll(y, 128 - ROPE // 2, 1), -pltpu.roll(y, ROPE // 2, 1))


def _pair_sum(a_refs, b_refs, trans_b):
    tot = None
    for a_r, b_r in zip(a_refs, b_refs):
        a, b = _bf(a_r[...]), _bf(b_r[...])
        if trans_b:
            d = lax.dot_general(a, b, (((1,), (1,)), ((), ())), preferred_element_type=F32)
        else:
            d = jnp.dot(a, b, preferred_element_type=F32)
        tot = d if tot is None else tot + d
    return tot


def _mm(pairs, out_dtype, name, *, trans_b=False, tm=512, tn=None, tk=None, res=None, dep=None):
    m, k = pairs[0][0].shape
    n = pairs[0][1].shape[0] if trans_b else pairs[0][1].shape[1]
    tm, tn, tk = min(tm, m), tn or n, tk or k
    nk, npair = k // tk, len(pairs)

    def kern(*refs):
        a_refs, b_refs = refs[:npair], refs[npair:2 * npair]
        rest = list(refs[2 * npair:])
        res_ref = rest.pop(0) if res is not None else None
        if dep is not None:
            rest.pop(0)
        o_ref = rest.pop(0)

        def finish(acc):
            if res_ref is not None:
                acc = res_ref[...] + acc
            o_ref[...] = acc.astype(o_ref.dtype)

        if nk == 1:
            finish(_pair_sum(a_refs, b_refs, trans_b))
        else:
            acc_ref = rest.pop(0)
            kk = pl.program_id(2)

            @pl.when(kk == 0)
            def _():
                acc_ref[...] = jnp.zeros_like(acc_ref)

            acc_ref[...] += _pair_sum(a_refs, b_refs, trans_b)

            @pl.when(kk == nk - 1)
            def _():
                finish(acc_ref[...])

    a_spec = pl.BlockSpec((tm, tk), lambda i, j, kk: (i, kk))
    b_spec = (pl.BlockSpec((tn, tk), lambda i, j, kk: (j, kk)) if trans_b
              else pl.BlockSpec((tk, tn), lambda i, j, kk: (kk, j)))
    io_spec = pl.BlockSpec((tm, tn), lambda i, j, kk: (i, j))
    in_specs = ([a_spec] * npair + [b_spec] * npair + ([io_spec] if res is not None else [])
                + ([pl.BlockSpec((8, 128), lambda i, j, kk: (0, 0))] if dep is not None else []))
    args = ([p[0] for p in pairs] + [p[1] for p in pairs] + ([res] if res is not None else [])
            + ([dep] if dep is not None else []))
    return pl.pallas_call(
        kern, name=name, grid=(m // tm, n // tn, nk), in_specs=in_specs, out_specs=io_spec,
        out_shape=jax.ShapeDtypeStruct((m, n), out_dtype),
        scratch_shapes=[pltpu.VMEM((tm, tn), F32)] if nk > 1 else [],
        compiler_params=_params("parallel", "parallel", "arbitrary"))(*args)


def _mm_normbwd(pairs, h, g, dres, dep, name, *, tm=512, tk=None):
    m, k = pairs[0][0].shape
    d = pairs[0][1].shape[0]
    tm, tk = min(tm, m), tk or k
    nk, npair = k // tk, len(pairs)

    def kern(*refs):
        a_refs, b_refs = refs[:npair], refs[npair:2 * npair]
        h_ref, g_ref, dres_ref, _, o_ref, dg_ref, acc_ref = refs[2 * npair:]
        i, kk = pl.program_id(0), pl.program_id(1)

        @pl.when(jnp.logical_and(i == 0, kk == 0))
        def _():
            dg_ref[...] = jnp.zeros_like(dg_ref)

        @pl.when(kk == 0)
        def _():
            acc_ref[...] = jnp.zeros_like(acc_ref)

        acc_ref[...] += _pair_sum(a_refs, b_refs, True)

        @pl.when(kk == nk - 1)
        def _():
            dn = acc_ref[...]
            x = h_ref[...]
            rstd = _rstd(x)
            xhat = x * rstd
            dg_ref[...] += jnp.broadcast_to(jnp.sum(dn * xhat, axis=0, keepdims=True), dg_ref.shape)
            dxh = dn * g_ref[...]
            dx = rstd * (dxh - xhat * jnp.mean(dxh * xhat, axis=-1, keepdims=True))
            o_ref[...] = dres_ref[...] + dx

    row = pl.BlockSpec((tm, d), lambda i, kk: (i, 0))
    in_specs = ([pl.BlockSpec((tm, tk), lambda i, kk: (i, kk))] * npair
                + [pl.BlockSpec((d, tk), lambda i, kk: (0, kk))] * npair
                + [row, pl.BlockSpec((1, d), lambda i, kk: (0, 0)), row, pl.BlockSpec((8, 128), lambda i, kk: (0, 0))])
    return pl.pallas_call(
        kern, name=name, grid=(m // tm, nk), in_specs=in_specs,
        out_specs=[row, pl.BlockSpec((8, d), lambda i, kk: (0, 0))],
        out_shape=[jax.ShapeDtypeStruct((m, d), F32), jax.ShapeDtypeStruct((8, d), F32)],
        scratch_shapes=[pltpu.VMEM((tm, d), F32)],
        compiler_params=_params("arbitrary", "arbitrary"))(
            *[p[0] for p in pairs], *[p[1] for p in pairs], h, g.reshape(1, d), dres, dep)


def _mm_tn(a, b, out_dtype, name, *, bm=None, bn=None, tk=1024):
    t, m = a.shape
    batched = b.ndim == 3
    n = b.shape[-1]
    nb = b.shape[0] if batched else 1
    bm, bn, tk = bm or m, bn or n, min(tk, t)
    nk = t // tk

    def kern(a_ref, b_ref, o_ref, acc_ref):
        kk = pl.program_id(3)

        @pl.when(kk == 0)
        def _():
            acc_ref[...] = jnp.zeros_like(acc_ref)

        acc_ref[...] += lax.dot_general(_bf(a_ref[...]), _bf(b_ref[...]), (((0,), (0,)), ((), ())),
                                        preferred_element_type=F32)

        @pl.when(kk == nk - 1)
        def _():
            o_ref[...] = acc_ref[...].astype(o_ref.dtype)

    a_spec = pl.BlockSpec((tk, bm), lambda h, i, j, kk: (kk, i))
    if batched:
        b_spec = pl.BlockSpec((None, tk, bn), lambda h, i, j, kk: (h, kk, j))
        o_spec = pl.BlockSpec((None, bm, bn), lambda h, i, j, kk: (h, i, j))
        out_shape = jax.ShapeDtypeStruct((nb, m, n), out_dtype)
    else:
        b_spec = pl.BlockSpec((tk, bn), lambda h, i, j, kk: (kk, j))
        o_spec = pl.BlockSpec((bm, bn), lambda h, i, j, kk: (i, j))
        out_shape = jax.ShapeDtypeStruct((m, n), out_dtype)
    return pl.pallas_call(
        kern, name=name, grid=(nb, m // bm, n // bn, nk), in_specs=[a_spec, b_spec], out_specs=o_spec,
        out_shape=out_shape, scratch_shapes=[pltpu.VMEM((bm, bn), F32)],
        compiler_params=_params("parallel", "parallel", "parallel", "arbitrary"))(a, b)


def _ffn_tile(f):
    return f // 2 if (f // 2) % 128 == 0 else f


def _ffn_fwd(h, g, w1, w3, w2, dep, name):
    t, d = h.shape
    f = w1.shape[1]
    tm, tf = min(512, t), _ffn_tile(f)
    nf = f // tf

    def kern(h_ref, g_ref, w1_ref, w3_ref, w2_ref, dep_ref, ho_ref, n_ref, z1_ref, z3_ref, n_sc, acc_ref):
        j = pl.program_id(1)

        @pl.when(j == 0)
        def _():
            x = h_ref[...]
            n = _bf(x * _rstd(x) * g_ref[...])
            n_sc[...] = n
            n_ref[...] = n
            acc_ref[...] = jnp.zeros_like(acc_ref)

        n = n_sc[...]
        z1 = jnp.dot(n, w1_ref[...], preferred_element_type=F32)
        z3 = jnp.dot(n, w3_ref[...], preferred_element_type=F32)
        z1_ref[...] = _bf(z1)
        z3_ref[...] = _bf(z3)
        act = _bf(z1 * _sigmoid(z1) * z3)
        acc_ref[...] += jnp.dot(act, w2_ref[...], preferred_element_type=F32)

        @pl.when(j == nf - 1)
        def _():
            ho_ref[...] = h_ref[...] + FFN_RES * acc_ref[...]

    row = pl.BlockSpec((tm, d), lambda i, j: (i, 0))
    col = pl.BlockSpec((tm, tf), lambda i, j: (i, j))
    return pl.pallas_call(
        kern, name=name, grid=(t // tm, nf),
        in_specs=[row, pl.BlockSpec((1, d), lambda i, j: (0, 0)), pl.BlockSpec((d, tf), lambda i, j: (0, j)),
                  pl.BlockSpec((d, tf), lambda i, j: (0, j)), pl.BlockSpec((tf, d), lambda i, j: (j, 0)),
                  pl.BlockSpec((8, 128), lambda i, j: (0, 0))],
        out_specs=[row, row, col, col],
        out_shape=[jax.ShapeDtypeStruct((t, d), F32), jax.ShapeDtypeStruct((t, d), BF16),
                   jax.ShapeDtypeStruct((t, f), BF16), jax.ShapeDtypeStruct((t, f), BF16)],
        scratch_shapes=[pltpu.VMEM((tm, d), BF16), pltpu.VMEM((tm, d), F32)],
        compiler_params=_params("parallel", "arbitrary"))(h, g.reshape(1, d), w1, w3, w2, dep)


def _ffn_bwd_x(dh, h_in, g, z1, z3, w1, w3, w2, dep, name):
    t, d = dh.shape
    f = z1.shape[1]
    tm = min(256, t)

    def kern(dh_ref, h_ref, g_ref, z1_ref, z3_ref, w2_hbm, w1_hbm, w3_hbm, dep_ref,
             o_ref, dg_ref, dz1_ref, dz3_ref, a_ref, df_ref, w2_ref, w1_ref, w3_ref, sems):
        @pl.when(pl.program_id(0) == 0)
        def _():
            copies = [pltpu.make_async_copy(src, dst, sems.at[k]) for k, (src, dst) in
                      enumerate(((w2_hbm, w2_ref), (w1_hbm, w1_ref), (w3_hbm, w3_ref)))]
            for cp in copies:
                cp.start()
            dg_ref[...] = jnp.zeros_like(dg_ref)
            for cp in copies:
                cp.wait()

        df = _bf(FFN_RES * dh_ref[...])
        df_ref[...] = df
        da = lax.dot_general(df, w2_ref[...], (((1,), (1,)), ((), ())), preferred_element_type=F32)
        z1v, z3v = z1_ref[...].astype(F32), z3_ref[...].astype(F32)
        sig = _sigmoid(z1v)
        silu = z1v * sig
        a_ref[...] = _bf(silu * z3v)
        dz1 = _bf(da * z3v * (sig * (1.0 + z1v * (1.0 - sig))))
        dz3 = _bf(da * silu)
        dz1_ref[...] = dz1
        dz3_ref[...] = dz3
        dn = (lax.dot_general(dz1, w1_ref[...], (((1,), (1,)), ((), ())), preferred_element_type=F32)
              + lax.dot_general(dz3, w3_ref[...], (((1,), (1,)), ((), ())), preferred_element_type=F32))
        x = h_ref[...]
        rstd = _rstd(x)
        xhat = x * rstd
        dg_ref[...] += jnp.broadcast_to(jnp.sum(dn * xhat, axis=0, keepdims=True), dg_ref.shape)
        dxh = dn * g_ref[...]
        o_ref[...] = dh_ref[...] + rstd * (dxh - xhat * jnp.mean(dxh * xhat, axis=-1, keepdims=True))

    row = pl.BlockSpec((tm, d), lambda i: (i, 0))
    col = pl.BlockSpec((tm, f), lambda i: (i, 0))
    whole = pl.BlockSpec(memory_space=pl.ANY)
    colshape = jax.ShapeDtypeStruct((t, f), BF16)
    return pl.pallas_call(
        kern, name=name, grid=(t // tm,),
        in_specs=[row, row, pl.BlockSpec((1, d), lambda i: (0, 0)), col, col, whole, whole, whole,
                  pl.BlockSpec((8, 128), lambda i: (0, 0))],
        out_specs=[row, pl.BlockSpec((8, d), lambda i: (0, 0)), col, col, col, row],
        out_shape=[jax.ShapeDtypeStruct((t, d), F32), jax.ShapeDtypeStruct((8, d), F32), colshape, colshape, colshape,
                   jax.ShapeDtypeStruct((t, d), BF16)],
        scratch_shapes=[pltpu.VMEM((f, d), BF16), pltpu.VMEM((d, f), BF16), pltpu.VMEM((d, f), BF16),
                        pltpu.SemaphoreType.DMA((3,))],
        compiler_params=_params("arbitrary"))(dh, h_in, g.reshape(1, d), z1, z3, w2, w1, w3, dep)


def _ffn_bwd(dh, h_in, g, n, z1, z3, w1, w3, w2, dep, tag):
    f = w1.shape[1]
    dh_in, dg, dz1, dz3, act, df = _ffn_bwd_x(dh, h_in, g, z1, z3, w1, w3, w2, dep, tag + "_bwd_x")
    dw1 = _mm_tn(n, dz1, BF16, tag + "_dw1", bn=_ffn_tile(f))
    dw3 = _mm_tn(n, dz3, BF16, tag + "_dw3", bn=_ffn_tile(f))
    dw2 = _mm_tn(act, df, BF16, tag + "_dw2", bm=_ffn_tile(f))
    return dh_in, dg, dw1, dw3, dw2


def _norm_fwd(h, g, dep, name):
    t, d = h.shape
    tm = min(512, t)

    def kern(h_ref, g_ref, dep_ref, o_ref):
        x = h_ref[...]
        o_ref[...] = _bf(x * _rstd(x) * g_ref[...])

    row = pl.BlockSpec((tm, d), lambda i: (i, 0))
    return pl.pallas_call(
        kern, name=name, grid=(t // tm,),
        in_specs=[row, pl.BlockSpec((1, d), lambda i: (0, 0)), pl.BlockSpec((8, 128), lambda i: (0, 0))],
        out_specs=row, out_shape=jax.ShapeDtypeStruct((t, d), BF16),
        compiler_params=_params("parallel"))(h, g.reshape(1, d), dep)


def _loss_bwd(h, target, g):
    t, d = h.shape
    tm = min(512, t)

    def kern(h_ref, t_ref, g_ref, dh_ref, dg_ref, loss_ref):
        @pl.when(pl.program_id(0) == 0)
        def _():
            dg_ref[...] = jnp.zeros_like(dg_ref)
            loss_ref[...] = jnp.zeros_like(loss_ref)

        x = h_ref[...]
        rstd = _rstd(x)
        xhat = x * rstd
        err = xhat * g_ref[...] - t_ref[...]
        row_loss = jnp.sum(err * err, axis=-1, keepdims=True) * (0.5 / d)
        loss_ref[...] += jnp.broadcast_to(jnp.sum(row_loss, axis=0, keepdims=True), loss_ref.shape)
        dy = err * (1.0 / d)
        dg_ref[...] += jnp.broadcast_to(jnp.sum(dy * xhat, axis=0, keepdims=True), dg_ref.shape)
        dxh = dy * g_ref[...]
        dh_ref[...] = rstd * (dxh - xhat * jnp.mean(dxh * xhat, axis=-1, keepdims=True))

    row = pl.BlockSpec((tm, d), lambda i: (i, 0))
    return pl.pallas_call(
        kern, name="loss_bwd", grid=(t // tm,),
        in_specs=[row, row, pl.BlockSpec((1, d), lambda i: (0, 0))],
        out_specs=[row, pl.BlockSpec((8, d), lambda i: (0, 0)), pl.BlockSpec((8, 128), lambda i: (0, 0))],
        out_shape=[jax.ShapeDtypeStruct((t, d), F32), jax.ShapeDtypeStruct((8, d), F32),
                   jax.ShapeDtypeStruct((8, 128), F32)],
        compiler_params=_params("arbitrary"))(h, target, g.reshape(1, d))


def _glu_fwd(m, wa, wb):
    t, d = m.shape
    c = wa.shape[1]
    tm, tc = min(512, t), min(512, c)

    def kern(m_ref, wa_ref, wb_ref, a_ref, b_ref, glu_ref):
        mv = m_ref[...]
        a = jnp.dot(mv, wa_ref[...], preferred_element_type=F32)
        b = jnp.dot(mv, wb_ref[...], preferred_element_type=F32)
        a_ref[...] = _bf(a)
        b_ref[...] = _bf(b)
        glu_ref[...] = _bf(a * _sigmoid(b))

    col = pl.BlockSpec((tm, tc), lambda i, j: (i, j))
    wspec = pl.BlockSpec((d, tc), lambda i, j: (0, j))
    shape = jax.ShapeDtypeStruct((t, c), BF16)
    return pl.pallas_call(
        kern, name="conv_glu_fwd", grid=(t // tm, c // tc),
        in_specs=[pl.BlockSpec((tm, d), lambda i, j: (i, 0)), wspec, wspec], out_specs=[col, col, col],
        out_shape=[shape, shape, shape], compiler_params=_params("parallel", "parallel"))(m, wa, wb)


def _conv_tile(t):
    return min(256, t)


def _shift_copies(ext, shifted, rows):
    for s in range(8):
        shifted[s] = ext[pl.ds(s, rows), :]


def _shifted_rows(shifted, start, nrows):
    return shifted[start % 8, pl.ds(start - start % 8, nrows), :]


def _conv_fwd(glu, w_dw, g):
    t, c = glu.shape
    tm = _conv_tile(t)
    hb = tm // CONV_HALO

    def kern(cur_ref, halo_ref, w_ref, g_ref, cv_ref, s_ref, ext, shifted):
        i = pl.program_id(0)
        ext[0:CONV_HALO, :] = jnp.where(i > 0, halo_ref[...].astype(F32), 0.0)
        ext[CONV_HALO:tm + CONV_HALO, :] = cur_ref[...].astype(F32)
        ext[tm + CONV_HALO:, :] = jnp.zeros((8, c), F32)
        _shift_copies(ext, shifted, tm + CONV_HALO)
        gv = g_ref[...]
        for r0 in range(0, tm, CONV_ROWS):
            acc = jnp.zeros((CONV_ROWS, c), F32)
            for k in range(CONV_WIDTH):
                acc = acc + _shifted_rows(shifted, r0 + 2 + k, CONV_ROWS) * w_ref[k:k + 1, :]
            cv_ref[r0:r0 + CONV_ROWS, :] = acc
            rn = acc * _rstd(acc) * gv
            s_ref[r0:r0 + CONV_ROWS, :] = _bf(rn * _sigmoid(rn))

    row = pl.BlockSpec((tm, c), lambda i: (i, 0))
    return pl.pallas_call(
        kern, name="conv_fwd", grid=(t // tm,),
        in_specs=[row, pl.BlockSpec((CONV_HALO, c), lambda i: (jnp.maximum(i * hb - 1, 0), 0)),
                  pl.BlockSpec((CONV_HALO, c), lambda i: (0, 0)), pl.BlockSpec((1, c), lambda i: (0, 0))],
        out_specs=[row, row],
        out_shape=[jax.ShapeDtypeStruct((t, c), F32), jax.ShapeDtypeStruct((t, c), BF16)],
        scratch_shapes=[pltpu.VMEM((tm + CONV_HALO + 8, c), F32), pltpu.VMEM((8, tm + CONV_HALO, c), F32)],
        compiler_params=_params("parallel"))(glu, glu, w_dw, g.reshape(1, c))


def _conv_bwd_norm(dh, cv, w_pw2, g, dep):
    t, c = cv.shape
    tm = min(512, t)

    def kern(dh_ref, cv_ref, w_ref, g_ref, dep_ref, dcv_ref, dg_ref):
        @pl.when(pl.program_id(0) == 0)
        def _():
            dg_ref[...] = jnp.zeros_like(dg_ref)

        ds = lax.dot_general(_bf(dh_ref[...]), w_ref[...], (((1,), (1,)), ((), ())), preferred_element_type=F32)
        x = cv_ref[...]
        rstd = _rstd(x)
        xhat = x * rstd
        rn = xhat * g_ref[...]
        sig = _sigmoid(rn)
        drn = ds * (sig * (1.0 + rn * (1.0 - sig)))
        dg_ref[...] += jnp.broadcast_to(jnp.sum(drn * xhat, axis=0, keepdims=True), dg_ref.shape)
        dxh = drn * g_ref[...]
        dcv_ref[...] = rstd * (dxh - xhat * jnp.mean(dxh * xhat, axis=-1, keepdims=True))

    row = pl.BlockSpec((tm, c), lambda i: (i, 0))
    return pl.pallas_call(
        kern, name="conv_bwd_norm", grid=(t // tm,),
        in_specs=[pl.BlockSpec((tm, dh.shape[1]), lambda i: (i, 0)), row,
                  pl.BlockSpec(w_pw2.shape, lambda i: (0, 0)), pl.BlockSpec((1, c), lambda i: (0, 0)),
                  pl.BlockSpec((8, 128), lambda i: (0, 0))],
        out_specs=[row, pl.BlockSpec((8, c), lambda i: (0, 0))],
        out_shape=[jax.ShapeDtypeStruct((t, c), F32), jax.ShapeDtypeStruct((8, c), F32)],
        compiler_params=_params("arbitrary"))(dh, cv, w_pw2, g.reshape(1, c), dep)


def _conv_bwd_dw(dcv, glu, a, b, w_dw):
    t, c = dcv.shape
    tm = _conv_tile(t)
    hb = tm // CONV_HALO
    last = t // CONV_HALO - 1

    def kern(dcv_ref, dnext_ref, glu_ref, gprev_ref, a_ref, b_ref, w_ref, da_ref, db_ref, dw_ref,
             dext, gext, dshift, gshift):
        i = pl.program_id(0)

        @pl.when(i == 0)
        def _():
            dw_ref[...] = jnp.zeros_like(dw_ref)

        dext[0:tm, :] = dcv_ref[...]
        dext[tm:tm + CONV_HALO, :] = jnp.where(i < t // tm - 1, dnext_ref[...], 0.0)
        dext[tm + CONV_HALO:, :] = jnp.zeros((8, c), F32)
        gext[0:CONV_HALO, :] = jnp.where(i > 0, gprev_ref[...].astype(F32), 0.0)
        gext[CONV_HALO:tm + CONV_HALO, :] = glu_ref[...].astype(F32)
        gext[tm + CONV_HALO:, :] = jnp.zeros((8, c), F32)
        _shift_copies(dext, dshift, tm + CONV_HALO)
        _shift_copies(gext, gshift, tm + CONV_HALO)
        for r0 in range(0, tm, CONV_ROWS):
            acc = jnp.zeros((CONV_ROWS, c), F32)
            for k in range(CONV_WIDTH):
                acc = acc + _shifted_rows(dshift, r0 + CONV_WIDTH - 1 - k, CONV_ROWS) * w_ref[k:k + 1, :]
            av = a_ref[r0:r0 + CONV_ROWS, :].astype(F32)
            sig = _sigmoid(b_ref[r0:r0 + CONV_ROWS, :].astype(F32))
            da_ref[r0:r0 + CONV_ROWS, :] = _bf(acc * sig)
            db_ref[r0:r0 + CONV_ROWS, :] = _bf(acc * av * sig * (1.0 - sig))
        for k in range(CONV_WIDTH):
            acc = jnp.zeros((CONV_ROWS, c), F32)
            for r0 in range(0, tm, CONV_ROWS):
                acc = acc + _shifted_rows(gshift, r0 + 2 + k, CONV_ROWS) * dext[r0:r0 + CONV_ROWS, :]
            dw_ref[k:k + 1, :] += jnp.sum(acc, axis=0, keepdims=True)

    row = pl.BlockSpec((tm, c), lambda i: (i, 0))
    shape = jax.ShapeDtypeStruct((t, c), BF16)
    return pl.pallas_call(
        kern, name="conv_bwd_dw", grid=(t // tm,),
        in_specs=[row, pl.BlockSpec((CONV_HALO, c), lambda i: (jnp.minimum((i + 1) * hb, last), 0)),
                  row, pl.BlockSpec((CONV_HALO, c), lambda i: (jnp.maximum(i * hb - 1, 0), 0)),
                  row, row, pl.BlockSpec((CONV_HALO, c), lambda i: (0, 0))],
        out_specs=[row, row, pl.BlockSpec((CONV_HALO, c), lambda i: (0, 0))],
        out_shape=[shape, shape, jax.ShapeDtypeStruct((CONV_HALO, c), F32)],
        scratch_shapes=[pltpu.VMEM((tm + CONV_HALO + 8, c), F32), pltpu.VMEM((tm + CONV_HALO + 8, c), F32),
                        pltpu.VMEM((8, tm + CONV_HALO, c), F32), pltpu.VMEM((8, tm + CONV_HALO, c), F32)],
        compiler_params=_params("arbitrary"))(dcv, dcv, glu, glu, a, b, w_dw)


def _rope_tables(pos):
    t = pos.shape[0]
    tm = min(512, t)
    freq = (np.float32(ROPE_THETA) ** (np.float32(-2.0) * np.arange(ROPE // 2, dtype=np.float32)
                                       / np.float32(ROPE))).astype(np.float32)
    row = np.zeros((2, 128), np.float32)
    row[0, :ROPE] = np.concatenate([freq, freq])
    row[1, :ROPE] = 1.0

    def kern(pos_ref, f_ref, c_ref, s_ref):
        ang = pos_ref[...].astype(F32) * f_ref[0:1, :]
        mask = f_ref[1:2, :]
        c_ref[...] = jnp.cos(ang) * mask
        s_ref[...] = jnp.sin(ang) * mask

    out = pl.BlockSpec((tm, 128), lambda i: (i, 0))
    shape = jax.ShapeDtypeStruct((t, 128), F32)
    return pl.pallas_call(
        kern, name="rope_tables", grid=(t // tm,),
        in_specs=[pl.BlockSpec((tm, 1), lambda i: (i, 0)), pl.BlockSpec((2, 128), lambda i: (0, 0))],
        out_specs=[out, out], out_shape=[shape, shape], compiler_params=_params("parallel"))(pos, jnp.asarray(row))


def _mla_prep(a, gq, gkv, cs_c, cs_s):
    t = a.shape[0]
    tm = min(512, t)
    kv0, r0 = Q_LORA, Q_LORA + KV_LORA

    def kern(a_ref, gq_ref, gkv_ref, c_ref, s_ref, cq_ref, ckv_ref, kr_ref):
        aq = a_ref[:, 0:kv0]
        akv = a_ref[:, kv0:r0]
        ar = a_ref[:, r0:A_PAD]
        cq_ref[...] = _bf(aq * _rstd(aq) * gq_ref[...])
        ckv_ref[...] = _bf(akv * _rstd(akv) * gkv_ref[...])
        kr_ref[...] = _bf(ar * c_ref[...] + _rot(ar) * s_ref[...])

    def row(w):
        return pl.BlockSpec((tm, w), lambda i: (i, 0))

    def vec(w):
        return pl.BlockSpec((1, w), lambda i: (0, 0))

    return pl.pallas_call(
        kern, name="mla_prep", grid=(t // tm,),
        in_specs=[row(A_PAD), vec(Q_LORA), vec(KV_LORA), row(128), row(128)],
        out_specs=[row(Q_LORA), row(KV_LORA), row(128)],
        out_shape=[jax.ShapeDtypeStruct((t, Q_LORA), BF16), jax.ShapeDtypeStruct((t, KV_LORA), BF16),
                   jax.ShapeDtypeStruct((t, 128), BF16)],
        compiler_params=_params("parallel"))(a, gq.reshape(1, -1), gkv.reshape(1, -1), cs_c, cs_s)


def _mla_prep_bwd(a, dcq, dckv, dar, gq, gkv):
    t = a.shape[0]
    tm = min(512, t)
    kv0, r0 = Q_LORA, Q_LORA + KV_LORA

    def kern(a_ref, dcq_ref, dckv_ref, dar_ref, gq_ref, gkv_ref, da_ref, dgq_ref, dgkv_ref):
        @pl.when(pl.program_id(0) == 0)
        def _():
            dgq_ref[...] = jnp.zeros_like(dgq_ref)
            dgkv_ref[...] = jnp.zeros_like(dgkv_ref)

        def back(x, dy, g_ref, dg_ref):
            rstd = _rstd(x)
            xhat = x * rstd
            dg_ref[...] += jnp.broadcast_to(jnp.sum(dy * xhat, axis=0, keepdims=True), dg_ref.shape)
            dxh = dy * g_ref[...]
            return rstd * (dxh - xhat * jnp.mean(dxh * xhat, axis=-1, keepdims=True))

        da_ref[:, 0:kv0] = _bf(back(a_ref[:, 0:kv0], dcq_ref[...], gq_ref, dgq_ref))
        da_ref[:, kv0:r0] = _bf(back(a_ref[:, kv0:r0], dckv_ref[...], gkv_ref, dgkv_ref))
        da_ref[:, r0:A_PAD] = _bf(dar_ref[...])

    def row(w):
        return pl.BlockSpec((tm, w), lambda i: (i, 0))

    def vec(r, w):
        return pl.BlockSpec((r, w), lambda i: (0, 0))

    return pl.pallas_call(
        kern, name="mla_prep_bwd", grid=(t // tm,),
        in_specs=[row(A_PAD), row(Q_LORA), row(KV_LORA), row(128), vec(1, Q_LORA), vec(1, KV_LORA)],
        out_specs=[row(A_PAD), vec(8, Q_LORA), vec(8, KV_LORA)],
        out_shape=[jax.ShapeDtypeStruct((t, A_PAD), BF16), jax.ShapeDtypeStruct((8, Q_LORA), F32),
                   jax.ShapeDtypeStruct((8, KV_LORA), F32)],
        compiler_params=_params("arbitrary"))(a, dcq, dckv, dar, gq.reshape(1, -1), gkv.reshape(1, -1))


def _mla_qkv(cq, ckv, kr, cs_c, cs_s, wuq, wukv):
    t = cq.shape[0]
    tm = min(512, t)
    kvw = NOPE + V_HEAD

    def kern(cq_ref, ckv_ref, kr_ref, c_ref, s_ref, wq_ref, wkv_ref, q_ref, k_ref, v_ref):
        r = jnp.dot(cq_ref[...], wq_ref[...], preferred_element_type=F32)
        kv = jnp.dot(ckv_ref[...], wkv_ref[...], preferred_element_type=F32)
        cv, sv, krv = c_ref[...], s_ref[...], kr_ref[...]
        for h in range(HEADS):
            xr = r[:, h * HEAD_PAD + NOPE:(h + 1) * HEAD_PAD]
            q_ref[h, :, 0:NOPE] = _bf(r[:, h * HEAD_PAD:h * HEAD_PAD + NOPE] * ATTN_SCALE)
            q_ref[h, :, NOPE:] = _bf((xr * cv + _rot(xr) * sv) * ATTN_SCALE)
            k_ref[h, :, 0:NOPE] = _bf(kv[:, h * kvw:h * kvw + NOPE])
            k_ref[h, :, NOPE:] = krv
            v_ref[h] = _bf(kv[:, h * kvw + NOPE:(h + 1) * kvw])

    def row(w):
        return pl.BlockSpec((tm, w), lambda i: (i, 0))

    def heads(w):
        return pl.BlockSpec((HEADS, tm, w), lambda i: (0, i, 0))

    return pl.pallas_call(
        kern, name="mla_qkv", grid=(t // tm,),
        in_specs=[row(Q_LORA), row(KV_LORA), row(128), row(128), row(128),
                  pl.BlockSpec(wuq.shape, lambda i: (0, 0)), pl.BlockSpec(wukv.shape, lambda i: (0, 0))],
        out_specs=[heads(HEAD_PAD), heads(HEAD_PAD), heads(V_HEAD)],
        out_shape=[jax.ShapeDtypeStruct((HEADS, t, HEAD_PAD), BF16), jax.ShapeDtypeStruct((HEADS, t, HEAD_PAD), BF16),
                   jax.ShapeDtypeStruct((HEADS, t, V_HEAD), BF16)],
        compiler_params=_params("parallel"))(cq, ckv, kr, cs_c, cs_s, wuq, wukv)


def _mla_qkv_bwd(dq, dk, dv, cs_c, cs_s, wuq, wukv):
    t = dq.shape[1]
    tm = min(256, t)
    kvw = NOPE + V_HEAD

    def kern(dq_ref, dk_ref, dv_ref, c_ref, s_ref, wq_ref, wkv_ref, dr_ref, dkv_ref, dcq_ref, dckv_ref, dar_ref):
        cv, sv = c_ref[...], s_ref[...]
        dar = jnp.zeros_like(cv)
        for h in range(HEADS):
            dqx = dq_ref[h, :, NOPE:]
            dr_ref[:, h * HEAD_PAD:h * HEAD_PAD + NOPE] = _bf(dq_ref[h, :, 0:NOPE] * ATTN_SCALE)
            dr_ref[:, h * HEAD_PAD + NOPE:(h + 1) * HEAD_PAD] = _bf((dqx * cv + _rot_t(dqx * sv)) * ATTN_SCALE)
            dkx = dk_ref[h, :, NOPE:]
            dar = dar + (dkx * cv + _rot_t(dkx * sv))
            dkv_ref[:, h * kvw:h * kvw + NOPE] = _bf(dk_ref[h, :, 0:NOPE])
            dkv_ref[:, h * kvw + NOPE:(h + 1) * kvw] = _bf(dv_ref[h])
        dar_ref[...] = dar
        dcq_ref[...] = lax.dot_general(dr_ref[...], wq_ref[...], (((1,), (1,)), ((), ())),
                                       preferred_element_type=F32)
        dckv_ref[...] = lax.dot_general(dkv_ref[...], wkv_ref[...], (((1,), (1,)), ((), ())),
                                        preferred_element_type=F32)

    def row(w):
        return pl.BlockSpec((tm, w), lambda i: (i, 0))

    def heads(w):
        return pl.BlockSpec((HEADS, tm, w), lambda i: (0, i, 0))

    return pl.pallas_call(
        kern, name="mla_qkv_bwd", grid=(t // tm,),
        in_specs=[heads(HEAD_PAD), heads(HEAD_PAD), heads(V_HEAD), row(128), row(128),
                  pl.BlockSpec(wuq.shape, lambda i: (0, 0)), pl.BlockSpec(wukv.shape, lambda i: (0, 0))],
        out_specs=[row(HEADS * HEAD_PAD), row(HEADS * kvw), row(Q_LORA), row(KV_LORA), row(128)],
        out_shape=[jax.ShapeDtypeStruct((t, HEADS * HEAD_PAD), BF16), jax.ShapeDtypeStruct((t, HEADS * kvw), BF16),
                   jax.ShapeDtypeStruct((t, Q_LORA), F32), jax.ShapeDtypeStruct((t, KV_LORA), F32),
                   jax.ShapeDtypeStruct((t, 128), F32)],
        compiler_params=_params("parallel"))(dq, dk, dv, cs_c, cs_s, wuq, wukv)


def _attn_block(t):
    return 512 if t >= 4096 else 128


def _fold_rows(x, op):
    r = x.shape[0]
    while r > 8:
        r //= 2
        x = op(x[:r], x[r:])
    return x


def _chunk_mask(bk, bq):
    kc = lax.broadcasted_iota(jnp.int32, (bk, bq), 0) // CHUNK
    qc = lax.broadcasted_iota(jnp.int32, (bk, bq), 1) // CHUNK
    return qc >= kc


def _flash_fwd(q, k, v):
    t = q.shape[1]
    bq = _attn_block(t)
    nq = t // bq

    def kern(q_ref, k_ref, v_ref, o_ref, lse_ref, s_buf, p_buf, m_ref, l_ref, acc_ref):
        i = pl.program_id(1)
        queries = (q_ref[0:bq, :], q_ref[bq:2 * bq, :])

        def block(j):
            rows = pl.ds(pl.multiple_of(j * bq, bq), bq)
            return k_ref[rows, :], v_ref[rows, :]

        def scores(kj, chain):
            return lax.dot_general(kj, queries[chain], (((1,), (1,)), ((), ())), preferred_element_type=F32)

        def softmax_block(chain, slot, vj):
            for c0 in range(0, bq, 128):
                cols = slice(c0, c0 + 128)
                s = s_buf[slot, chain, :, cols]
                m_old = m_ref[chain, 0:1, cols]
                m_new = jnp.maximum(m_old, jnp.max(s, axis=0, keepdims=True))
                alpha = jnp.exp(m_old - m_new)
                p = jnp.exp(s - m_new)
                l_ref[chain, 0:1, cols] = alpha * l_ref[chain, 0:1, cols] + jnp.sum(p, axis=0, keepdims=True)
                m_ref[chain, 0:1, cols] = m_new
                p_buf[chain, :, cols] = _bf(p)
                acc_ref[chain, :, cols] = acc_ref[chain, :, cols] * alpha
            acc_ref[chain] += lax.dot_general(vj, p_buf[chain], (((0,), (0,)), ((), ())),
                                              preferred_element_type=F32)

        m_ref[...] = jnp.full(m_ref.shape, -1e30, F32)
        l_ref[...] = jnp.zeros_like(l_ref)
        acc_ref[...] = jnp.zeros_like(acc_ref)
        mask = _chunk_mask(bq, bq)
        k0, v0 = block(2 * i)
        k1, v1 = block(2 * i + 1)
        s_buf[0, 0] = jnp.where(mask, scores(k0, 0), -1e30)
        s_buf[0, 1] = scores(k0, 1)
        s_buf[1, 1] = jnp.where(mask, scores(k1, 1), -1e30)
        softmax_block(0, 0, v0)
        softmax_block(1, 0, v0)
        softmax_block(1, 1, v1)
        kf = block(0)[0]
        s_buf[0, 0] = scores(kf, 0)
        s_buf[0, 1] = scores(kf, 1)

        def body(pair, carry):
            for cur in range(2):
                j = 2 * pair + cur
                kn = block(jnp.minimum(j + 1, jnp.maximum(2 * i - 1, 0)))[0]
                s_buf[1 - cur, 0] = scores(kn, 0)
                s_buf[1 - cur, 1] = scores(kn, 1)
                vj = block(j)[1]
                softmax_block(0, cur, vj)
                softmax_block(1, cur, vj)
            return carry

        lax.fori_loop(0, i, body, 0)
        for chain in range(2):
            l = l_ref[chain, 0:1, :]
            o_ref[chain * bq:(chain + 1) * bq, :] = _bf((acc_ref[chain] / l).T)
            lse_ref[chain] = jnp.broadcast_to(m_ref[chain, 0:1, :] + jnp.log(l), (8, bq))

    return pl.pallas_call(
        kern, name="flash_fwd", grid=(HEADS, nq // 2),
        in_specs=[pl.BlockSpec((None, 2 * bq, HEAD_PAD), lambda h, i: (h, i, 0)),
                  pl.BlockSpec((None, t, HEAD_PAD), lambda h, i: (h, 0, 0)),
                  pl.BlockSpec((None, t, V_HEAD), lambda h, i: (h, 0, 0))],
        out_specs=[pl.BlockSpec((2 * bq, V_HEAD), lambda h, i: (i, h)),
                   pl.BlockSpec((None, 2, 8, bq), lambda h, i: (h, i, 0, 0))],
        out_shape=[jax.ShapeDtypeStruct((t, HEADS * V_HEAD), BF16), jax.ShapeDtypeStruct((HEADS, nq, 8, bq), F32)],
        scratch_shapes=[pltpu.VMEM((2, 2, bq, bq), F32), pltpu.VMEM((2, bq, bq), BF16), pltpu.VMEM((2, 8, bq), F32),
                        pltpu.VMEM((2, 8, bq), F32), pltpu.VMEM((2, V_HEAD, bq), F32)],
        compiler_params=_params("parallel", "arbitrary"))(q, k, v)


def _attn_delta(do, o):
    t = do.shape[0]
    bq = _attn_block(t)

    def kern(do_ref, o_ref, d_ref):
        for h in range(HEADS):
            cols = slice(h * V_HEAD, (h + 1) * V_HEAD)
            prod = do_ref[:, cols].astype(F32) * o_ref[:, cols].astype(F32)
            d_ref[h] = jnp.broadcast_to(jnp.sum(prod.T, axis=0, keepdims=True), (8, bq))

    blk = pl.BlockSpec((bq, HEADS * V_HEAD), lambda i: (i, 0))
    return pl.pallas_call(
        kern, name="attn_delta", grid=(t // bq,), in_specs=[blk, blk],
        out_specs=pl.BlockSpec((HEADS, None, 8, bq), lambda i: (0, i, 0, 0)),
        out_shape=jax.ShapeDtypeStruct((HEADS, t // bq, 8, bq), F32),
        compiler_params=_params("parallel"))(do, o)


def _flash_bwd(q, k, v, do, lse, delta):
    t = q.shape[1]
    bq = _attn_block(t)
    nq = t // bq

    def kern(q_ref, k_ref, v_ref, do_ref, lse_ref, del_ref, dq_ref, dk_ref, dv_ref):
        j = pl.program_id(1)

        @pl.when(j == 0)
        def _():
            dq_ref[...] = jnp.zeros_like(dq_ref)

        dk_ref[...] = jnp.zeros_like(dk_ref)
        dv_ref[...] = jnp.zeros_like(dv_ref)
        kj, vj = k_ref[...], v_ref[...]

        def step(i, masked):
            rows = pl.ds(pl.multiple_of(i * bq, bq), bq)
            qi, doi = q_ref[rows, :], do_ref[rows, :]
            st = lax.dot_general(kj, qi, (((1,), (1,)), ((), ())), preferred_element_type=F32)
            pt = jnp.exp(st - lse_ref[i][0:1, :])
            if masked:
                pt = jnp.where(_chunk_mask(bq, bq), pt, 0.0)
            dpt = lax.dot_general(vj, doi, (((1,), (1,)), ((), ())), preferred_element_type=F32)
            dst = _bf(pt * (dpt - del_ref[i][0:1, :]))
            dv_ref[...] += jnp.dot(_bf(pt), doi, preferred_element_type=F32)
            dk_ref[...] += jnp.dot(dst, qi, preferred_element_type=F32)
            dq_ref[rows, :] += lax.dot_general(dst, kj, (((0,), (0,)), ((), ())), preferred_element_type=F32)

        step(j, True)

        def body(i, carry):
            step(i, False)
            return carry

        lax.fori_loop(j + 1, nq, body, 0)

    stat = pl.BlockSpec((None, nq, 8, bq), lambda h, j: (h, 0, 0, 0))
    return pl.pallas_call(
        kern, name="flash_bwd", grid=(HEADS, nq),
        in_specs=[pl.BlockSpec((None, t, HEAD_PAD), lambda h, j: (h, 0, 0)),
                  pl.BlockSpec((None, bq, HEAD_PAD), lambda h, j: (h, j, 0)),
                  pl.BlockSpec((None, bq, V_HEAD), lambda h, j: (h, j, 0)),
                  pl.BlockSpec((t, V_HEAD), lambda h, j: (0, h)), stat, stat],
        out_specs=[pl.BlockSpec((None, t, HEAD_PAD), lambda h, j: (h, 0, 0)),
                   pl.BlockSpec((None, bq, HEAD_PAD), lambda h, j: (h, j, 0)),
                   pl.BlockSpec((None, bq, V_HEAD), lambda h, j: (h, j, 0))],
        out_shape=[jax.ShapeDtypeStruct((HEADS, t, HEAD_PAD), F32), jax.ShapeDtypeStruct((HEADS, t, HEAD_PAD), F32),
                   jax.ShapeDtypeStruct((HEADS, t, V_HEAD), F32)],
        compiler_params=_params("parallel", "arbitrary"))(q, k, v, do, lse, delta)


def _place():
    x, y, c = lax.axis_index("x"), lax.axis_index("y"), lax.axis_index("c")
    return x, y, c, [(1 - x, y), (x, 1 - y), (1 - x, 1 - y)]


def _all_gather_rows(block, name):
    m_per, n = block.shape

    def body(x_ref, out_ref, send_sems, recv_sems, local_sem):
        x, y, c, chips = _place()
        me, sibling = (x, y, c), (x, y, 1 - c)

        def rows(px, py, pc):
            return out_ref.at[pl.ds((4 * px + 2 * py + pc) * m_per, m_per), :]

        def copy(k, blk, to, src=None):
            return pltpu.make_async_remote_copy(
                src_ref=rows(*blk) if src is None else src, dst_ref=rows(*blk), send_sem=send_sems.at[k],
                recv_sem=recv_sems.at[k], device_id=to, device_id_type=MESH)

        mine = pltpu.make_async_copy(x_ref, rows(*me), local_sem)
        mine.start()
        first = [copy(0, me, sibling, src=x_ref)]
        first += [copy(1 + j, me, (*chip, c), src=x_ref) for j, chip in enumerate(chips)]
        for cp in first:
            cp.start()
        passed = [copy(4 + j, (*chip, c), sibling) for j, chip in enumerate(chips)]
        for j, chip in enumerate(chips):
            copy(1 + j, (*chip, c), me).wait_recv()
            passed[j].start()
        copy(0, sibling, me).wait_recv()
        for j, chip in enumerate(chips):
            copy(4 + j, (*chip, 1 - c), me).wait_recv()
        for cp in first + passed:
            cp.wait_send()
        mine.wait()

    return pl.pallas_call(
        body, name=name, out_shape=jax.ShapeDtypeStruct((8 * m_per, n), block.dtype),
        in_specs=[pl.BlockSpec(memory_space=pltpu.VMEM)], out_specs=pl.BlockSpec(memory_space=pltpu.VMEM),
        scratch_shapes=[pltpu.SemaphoreType.DMA((7,)), pltpu.SemaphoreType.DMA((7,)), pltpu.SemaphoreType.DMA],
        compiler_params=pltpu.CompilerParams(vmem_limit_bytes=VMEM_LIMIT_BYTES))(block)


HBM_SPEC = pl.BlockSpec(memory_space=pltpu.HBM)
SEM_SPEC = pl.BlockSpec(memory_space=pltpu.SEMAPHORE)
DATAFLOW = pltpu.SideEffectType.DATAFLOW_SIDE_EFFECTING


def _in_hbm(a):
    return pltpu.with_memory_space_constraint(a, pltpu.HBM)


def _chip_copies(ins, lands, send_sems, recv_sems, src_slot):
    n = len(ins)
    x, y, c, chips = _place()
    me = 2 * x + y
    return [pltpu.make_async_remote_copy(
        src_ref=ins[w].at[2 * chip[0] + chip[1]] if src_slot else ins[w], dst_ref=lands[w].at[me],
        send_sem=send_sems.at[p * n + w], recv_sem=recv_sems.at[p * n + w], device_id=(*chip, c),
        device_id_type=MESH) for w in range(n) for p, chip in enumerate(chips)]


def _exchange_start(srcs, lands, src_slot, name, dep=None):
    n = len(srcs)
    first_out = 2 * n + (dep is not None)

    def body(*refs):
        for cp in _chip_copies(refs[:n], refs[n:2 * n], refs[first_out], refs[first_out + 1], src_slot):
            cp.start()
        token = refs[-1]
        token[...] = jnp.zeros_like(token)

    thru = [pltpu.HBM(a.shape, a.dtype) for a in list(srcs) + list(lands)]
    res = pl.pallas_call(
        body, name=name,
        out_shape=(pltpu.SemaphoreType.DMA((3 * n,)), pltpu.SemaphoreType.DMA((3 * n,)), *thru,
                   jax.ShapeDtypeStruct((8, 128), F32)),
        in_specs=[HBM_SPEC] * (2 * n) + ([pl.BlockSpec(memory_space=pl.ANY)] if dep is not None else []),
        out_specs=(SEM_SPEC, SEM_SPEC, *[HBM_SPEC] * (2 * n), pl.BlockSpec(memory_space=pltpu.VMEM)),
        input_output_aliases={i: 2 + i for i in range(2 * n)},
        compiler_params=pltpu.CompilerParams(has_side_effects=DATAFLOW))(
            *[_in_hbm(a) for a in srcs], *[_in_hbm(a) for a in lands], *([dep] if dep is not None else []))
    return (res[0], res[1], list(res[2:2 + n]), list(res[2 + n:2 + 2 * n])), res[-1]


def _exchange_wait(flight, after, src_slot, name):
    send_sems, recv_sems, srcs, lands = flight
    n = len(srcs)

    def body(*refs):
        for cp in _chip_copies(refs[:n], refs[n:2 * n], refs[2 * n], refs[2 * n + 1], src_slot):
            cp.wait_send()
            cp.wait_recv()

    thru = [pltpu.HBM(a.shape, a.dtype) for a in list(srcs) + list(lands)]
    res = pl.pallas_call(
        body, name=name, out_shape=thru,
        in_specs=[HBM_SPEC] * (2 * n) + [SEM_SPEC, SEM_SPEC, pl.BlockSpec(memory_space=pl.ANY)],
        out_specs=[HBM_SPEC] * (2 * n), input_output_aliases={i: i for i in range(2 * n)},
        compiler_params=pltpu.CompilerParams(has_side_effects=DATAFLOW))(*srcs, *lands, send_sems, recv_sems, after)
    return list(res[n:])


def _landing(own, me):
    return lax.dynamic_update_index_in_dim(lax.empty((4, *own.shape), own.dtype), own, me, 0)


def _swap_with_sibling(arrays, name):
    n = len(arrays)

    def body(*refs):
        ins, outs = refs[:n], refs[n:2 * n]
        send_sems, recv_sems = refs[2 * n:]
        x, y, c, _ = _place()
        copies = [pltpu.make_async_remote_copy(src_ref=ins[w], dst_ref=outs[w], send_sem=send_sems.at[w],
                                               recv_sem=recv_sems.at[w], device_id=(x, y, 1 - c), device_id_type=MESH)
                  for w in range(n)]
        for cp in copies:
            cp.start()
        for cp in copies:
            cp.wait()

    any_spec = pl.BlockSpec(memory_space=pl.ANY)
    return pl.pallas_call(
        body, name=name, out_shape=[jax.ShapeDtypeStruct(a.shape, a.dtype) for a in arrays],
        in_specs=[any_spec] * n, out_specs=[any_spec] * n,
        scratch_shapes=[pltpu.SemaphoreType.DMA((n,)), pltpu.SemaphoreType.DMA((n,))])(*arrays)


def _as_rows(a):
    return a.reshape(-1, a.shape[-1])


def _row_tile(r, c, budget_bytes=1 << 20):
    tr = r
    while tr % 16 == 0 and tr * c * 4 > budget_bytes:
        tr //= 2
    return tr


def _sum_slots(layers, nlayer, name, into=None):
    _, r, c = layers[0][1].shape
    tr = _row_tile(r, c)
    nt = r // tr
    acc = into
    for l, r4 in layers:
        def kern(r_ref, *rest):
            o_ref = rest[-1]
            o_ref[...] = (((r_ref[0].astype(F32) + r_ref[1].astype(F32)) + r_ref[2].astype(F32))
                          + r_ref[3].astype(F32))

        out_spec = pl.BlockSpec((tr, c), lambda i, l=l: (l * nt + i, 0))
        first = acc is None
        acc = pl.pallas_call(
            kern, name=f"{name}_l{l}", grid=(nt,),
            in_specs=[pl.BlockSpec((4, tr, c), lambda i: (0, i, 0))]
            + ([] if first else [pl.BlockSpec(memory_space=pl.ANY)]),
            out_specs=out_spec, out_shape=jax.ShapeDtypeStruct((nlayer * r, c), F32),
            input_output_aliases={} if first else {1: 0},
            compiler_params=_params("parallel"))(*([r4] if first else [r4, acc]))
    return acc


def _adamw(w, m, v, parts, name):
    r, c = w.shape
    tr = _row_tile(r, c, 3 << 19)
    npart = len(parts)
    c1 = 1.0 - ADAM_B1 ** ADAM_STEP
    c2 = 1.0 - ADAM_B2 ** ADAM_STEP

    def kern(*refs):
        w_ref, m_ref, v_ref = refs[:3]
        p_refs = refs[3:3 + npart]
        g_ref, d_ref, mo_ref, vo_ref = refs[3 + npart:]
        g = p_refs[0][...]
        for p in p_refs[1:]:
            g = g + p[...]
        mn = ADAM_B1 * m_ref[...] + (1.0 - ADAM_B1) * g
        vn = ADAM_B2 * v_ref[...] + (1.0 - ADAM_B2) * (g * g)
        g_ref[...] = g
        mo_ref[...] = mn
        vo_ref[...] = vn
        d_ref[...] = -ADAM_LR * ((mn / c1) / (jnp.sqrt(vn / c2) + ADAM_EPS) + ADAM_WD * w_ref[...])

    blk = pl.BlockSpec((tr, c), lambda i: (i, 0))
    shape = jax.ShapeDtypeStruct((r, c), F32)
    return pl.pallas_call(
        kern, name=name, grid=(r // tr,), in_specs=[blk] * (3 + npart), out_specs=[blk] * 4, out_shape=[shape] * 4,
        compiler_params=_params("parallel"))(w, m, v, *parts)


def _sum_devices(g8, name):
    _, r, c = g8.shape

    def kern(g_ref, o_ref):
        tot = g_ref[0]
        for dev in range(1, 8):
            tot = tot + g_ref[dev]
        o_ref[...] = tot

    return pl.pallas_call(
        kern, name=name, grid=(1,), in_specs=[pl.BlockSpec((8, r, c), lambda i: (0, 0, 0))],
        out_specs=pl.BlockSpec((r, c), lambda i: (0, 0)), out_shape=jax.ShapeDtypeStruct((r, c), F32),
        compiler_params=_params("arbitrary"))(g8)


def _pad_lanes(a, width):
    return jnp.pad(a, [(0, 0)] * (a.ndim - 1) + [(0, width - a.shape[-1])])


def kernel(x, positions, ffn_norm1, ffn1_w1, ffn1_w3, ffn1_w2, mix_norm, ffn_norm2, ffn2_w1, ffn2_w3, ffn2_w2, conv_w_pw1, conv_w_dw, conv_norm, conv_w_pw2, mla_w_a, mla_q_norm, mla_kv_norm, mla_w_uq, mla_w_ukv, mla_w_o, final_norm, loss_target, m_ffn_norm1, m_ffn1_w1, m_ffn1_w3, m_ffn1_w2, m_mix_norm, m_ffn_norm2, m_ffn2_w1, m_ffn2_w3, m_ffn2_w2, m_conv_w_pw1, m_conv_w_dw, m_conv_norm, m_conv_w_pw2, m_mla_w_a, m_mla_q_norm, m_mla_kv_norm, m_mla_w_uq, m_mla_w_ukv, m_mla_w_o, m_final_norm, v_ffn_norm1, v_ffn1_w1, v_ffn1_w3, v_ffn1_w2, v_mix_norm, v_ffn_norm2, v_ffn2_w1, v_ffn2_w3, v_ffn2_w2, v_conv_w_pw1, v_conv_w_dw, v_conv_norm, v_conv_w_pw2, v_mla_w_a, v_mla_q_norm, v_mla_kv_norm, v_mla_w_uq, v_mla_w_ukv, v_mla_w_o, v_final_norm):
    given = locals()
    return _step({nm: given[nm] for nm in INPUTS})


def _step(A):
    x = A['x'][0]
    target = A['loss_target'][0]
    t, d = x.shape
    pos = A['positions'].reshape(t, 1)
    me = 2 * lax.axis_index("x") + lax.axis_index("y")

    ffn = [f'ffn{k}_{w}' for k in (1, 2) for w in ('w1', 'w3', 'w2')]
    gather_groups = [[(nm, 0) for nm in ffn[:3]],
                     [('conv_w_pw1', 0), ('conv_w_pw2', 0)] + [(nm, 0) for nm in ffn[3:]],
                     [(nm, 1) for nm in ffn[:3]] + [('mla_w_a', 0), ('mla_w_uq', 0), ('mla_w_ukv', 0), ('mla_w_o', 0)],
                     [(nm, 1) for nm in ffn[3:]]]
    gather_flights = {}
    big = {}

    def gather_start(gi, dep):
        shards = [_bf(A[nm][l]) for nm, l in gather_groups[gi]]
        gather_flights[gi], token = _exchange_start(shards, [_landing(s, me) for s in shards], False,
                                                    f"gather_start_{gi}", dep)
        return token

    def gather_wait(gi, after):
        landed = _exchange_wait(gather_flights[gi], after, False, f"gather_wait_{gi}")
        big.update(zip(gather_groups[gi], landed))
        return landed[0]

    dw_shard = A['conv_w_dw'][0]
    cw = dw_shard.shape[1]
    small = jnp.concatenate([
        jnp.pad(dw_shard, ((0, CONV_HALO - CONV_WIDTH), (0, 0))),
        jnp.pad(_pad_lanes(A['mla_q_norm'], cw), ((0, 7), (0, 0))),
        jnp.pad(_pad_lanes(A['mla_kv_norm'], cw), ((0, 7), (0, 0)))], axis=0)
    small = _all_gather_rows(small, "gather_small_weights").reshape(4, 2, 48, cw)[:, 0]
    w_dw = jnp.concatenate([small[j, :CONV_HALO] for j in range(4)], axis=1)
    gq = jnp.concatenate([small[j, CONV_HALO, :Q_LORA // 4] for j in range(4)])
    gkv = jnp.concatenate([small[j, CONV_HALO + 8, :KV_LORA // 4] for j in range(4)])

    def cols(nm, layer):
        return jnp.concatenate([big[nm, layer][j] for j in range(4)], axis=1)

    def rows(nm, layer):
        g = big[nm, layer]
        return g.reshape(-1, g.shape[-1])

    ffn_w = {}

    def ffn_weights(k, l):
        ffn_w[k, l] = (cols(f'ffn{k}_w1', l), cols(f'ffn{k}_w3', l), rows(f'ffn{k}_w2', l))
        return ffn_w[k, l]

    token = gather_start(0, None)
    cs_c, cs_s = _rope_tables(pos)
    h0 = x
    token = gather_start(1, gather_wait(0, token))
    h1, n01, z01a, z01b = _ffn_fwd(h0, A['ffn_norm1'][0], *ffn_weights(1, 0), token, "ffn1_l0_fwd")
    token = gather_start(3, gather_start(2, gather_wait(1, h1)))
    pw1 = big['conv_w_pw1', 0]
    pw1_a = jnp.concatenate([pw1[0], pw1[1]], axis=1)
    pw1_b = jnp.concatenate([pw1[2], pw1[3]], axis=1)
    pw2 = rows('conv_w_pw2', 0)
    m0 = _norm_fwd(h1, A['mix_norm'][0], token, "mix_norm_l0")
    ca, cb, glu = _glu_fwd(m0, pw1_a, pw1_b)
    cv, cs = _conv_fwd(glu, w_dw, A['conv_norm'][0])
    h2 = _mm([(cs, pw2)], F32, "conv_pw2_fwd", res=h1)
    h3, n02, z02a, z02b = _ffn_fwd(h2, A['ffn_norm2'][0], *ffn_weights(2, 0), token, "ffn2_l0_fwd")
    gather_wait(2, h3)
    w_a = _pad_lanes(rows('mla_w_a', 0), A_PAD)
    wuq = _pad_lanes(big['mla_w_uq', 0].reshape(Q_LORA, HEADS, NOPE + ROPE), HEAD_PAD).reshape(Q_LORA, -1)
    wukv = big['mla_w_ukv', 0].reshape(KV_LORA, HEADS * (NOPE + V_HEAD))
    w_o = rows('mla_w_o', 0)
    h4, n11, z11a, z11b = _ffn_fwd(h3, A['ffn_norm1'][1], *ffn_weights(1, 1), token, "ffn1_l1_fwd")
    m1 = _norm_fwd(h4, A['mix_norm'][1], token, "mix_norm_l1")
    a_lat = _mm([(m1, w_a)], F32, "mla_down_fwd")
    cq, ckv, kr = _mla_prep(a_lat, gq, gkv, cs_c, cs_s)
    q, k, v = _mla_qkv(cq, ckv, kr, cs_c, cs_s, wuq, wukv)
    o, lse = _flash_fwd(q, k, v)
    h5 = _mm([(o, w_o)], F32, "mla_out_fwd", res=h4)
    gather_wait(3, h5)
    h6, n12, z12a, z12b = _ffn_fwd(h5, A['ffn_norm2'][1], *ffn_weights(2, 1), token, "ffn2_l1_fwd")

    def col_slots(g):
        r, c4 = g.shape
        return g.reshape(r, 4, c4 // 4).transpose(1, 0, 2)

    def row_slots(g):
        return g.reshape(4, g.shape[0] // 4, g.shape[1])

    scatter_flights = []

    def scatter_start(named):
        srcs = [g for _, g in named]
        lands = [_landing(lax.dynamic_index_in_dim(g, me, 0, keepdims=False), me) for g in srcs]
        flight, token = _exchange_start(srcs, lands, True, f"scatter_start_{len(scatter_flights)}")
        scatter_flights.append(([key for key, _ in named], flight))
        return token

    def send_ffn(k, l, dw1, dw3, dw2):
        return scatter_start([((f'ffn{k}_w1', l), col_slots(dw1)), ((f'ffn{k}_w3', l), col_slots(dw3)),
                              ((f'ffn{k}_w2', l), row_slots(dw2))])

    dh6, dg_final, loss_part = _loss_bwd(h6, target, A['final_norm'])
    dh5, dg_n2_l1, *dws = _ffn_bwd(dh6, h5, A['ffn_norm2'][1], n12, z12a, z12b, *ffn_w[2, 1], loss_part, "ffn2_l1")
    token = send_ffn(2, 1, *dws)

    do = _mm([(dh5, w_o)], BF16, "mla_out_bwd", trans_b=True, dep=token)
    dw_o = _mm_tn(o, dh5, BF16, "mla_dw_o")
    delta = _attn_delta(do, o)
    dq, dk, dv = _flash_bwd(q, k, v, do, lse, delta)
    dr, dkv, dcq, dckv, dar = _mla_qkv_bwd(dq, dk, dv, cs_c, cs_s, wuq, wukv)
    dwuq = _mm_tn(cq, dr, BF16, "mla_dw_uq", bn=dr.shape[1] // 2)
    dwukv = _mm_tn(ckv, dkv, BF16, "mla_dw_ukv", bn=dkv.shape[1] // 2)
    da_lat, dgq, dgkv = _mla_prep_bwd(a_lat, dcq, dckv, dar, gq, gkv)
    dw_a = _mm_tn(m1, da_lat, BF16, "mla_dw_a")
    token = scatter_start([
        (('mla_w_a', 0), row_slots(dw_a[:, :Q_LORA + KV_LORA + ROPE])),
        (('mla_w_uq', 0), dwuq.reshape(4, Q_LORA // 4, HEADS, HEAD_PAD)[..., :NOPE + ROPE]),
        (('mla_w_ukv', 0), dwukv.reshape(4, KV_LORA // 4, HEADS, NOPE + V_HEAD)),
        (('mla_w_o', 0), row_slots(dw_o))])
    dh4, dg_mix_l1 = _mm_normbwd([(da_lat, w_a)], h4, A['mix_norm'][1], dh5, token, "mla_down_bwd")

    dh3, dg_n1_l1, *dws = _ffn_bwd(dh4, h3, A['ffn_norm1'][1], n11, z11a, z11b, *ffn_w[1, 1], token, "ffn1_l1")
    token = send_ffn(1, 1, *dws)
    dh2, dg_n2_l0, *dws = _ffn_bwd(dh3, h2, A['ffn_norm2'][0], n02, z02a, z02b, *ffn_w[2, 0], token, "ffn2_l0")
    token = send_ffn(2, 0, *dws)

    dcv, dg_conv = _conv_bwd_norm(dh2, cv, pw2, A['conv_norm'][0], token)
    dw_pw2 = _mm_tn(cs, dh2, BF16, "conv_dw_pw2")
    dca, dcb, ddw = _conv_bwd_dw(dcv, glu, ca, cb, w_dw)
    dpw1_a = _mm_tn(m0, dca, BF16, "conv_dw_pw1a")
    dpw1_b = _mm_tn(m0, dcb, BF16, "conv_dw_pw1b")
    half = dpw1_a.shape[1] // 2
    token = scatter_start([
        (('conv_w_pw1', 0), jnp.stack([dpw1_a[:, :half], dpw1_a[:, half:], dpw1_b[:, :half], dpw1_b[:, half:]])),
        (('conv_w_pw2', 0), row_slots(dw_pw2))])
    dh1, dg_mix_l0 = _mm_normbwd([(dca, pw1_a), (dcb, pw1_b)], h1, A['mix_norm'][0], dh2, token, "conv_pw1_bwd")

    dx, dg_n1_l0, *dws = _ffn_bwd(dh1, h0, A['ffn_norm1'][0], n01, z01a, z01b, *ffn_w[1, 0], token, "ffn1_l0")
    last_sent = send_ffn(1, 0, *dws)
    out = {}

    qkv_row = jnp.concatenate([dgq, dgkv, jnp.zeros((8, d - Q_LORA - KV_LORA), F32)], axis=1)
    loss_row = _pad_lanes(loss_part, d)
    small_g = jnp.concatenate([dg_n1_l0, dg_n1_l1, dg_mix_l0, dg_mix_l1, dg_n2_l0, dg_n2_l1, dg_conv, dg_final,
                               qkv_row, loss_row, ddw], axis=0)
    nrow = small_g.shape[0]
    tot = _sum_devices(_all_gather_rows(small_g, "gather_small_grads").reshape(8, nrow, d), "sum_small_grads")
    loss = tot[72, 0]
    q_shard = lax.dynamic_slice_in_dim(tot[64, :Q_LORA], me * (Q_LORA // 4), Q_LORA // 4)
    kv_shard = lax.dynamic_slice_in_dim(tot[64, Q_LORA:Q_LORA + KV_LORA], me * (KV_LORA // 4), KV_LORA // 4)
    dw_shard_g = lax.dynamic_slice_in_dim(tot[80:80 + CONV_WIDTH], me * cw, cw, axis=1)
    small_grads = {
        'ffn_norm1': jnp.stack([tot[0], tot[8]]), 'mix_norm': jnp.stack([tot[16], tot[24]]),
        'ffn_norm2': jnp.stack([tot[32], tot[40]]), 'conv_norm': tot[48][None], 'final_norm': tot[56],
        'mla_q_norm': q_shard[None], 'mla_kv_norm': kv_shard[None], 'conv_w_dw': dw_shard_g[None],
    }
    for nm, g in small_grads.items():
        res = _adamw(_as_rows(A[nm]) if A[nm].ndim > 1 else A[nm].reshape(1, -1),
                     A['m_' + nm].reshape(-1, A[nm].shape[-1]), A['v_' + nm].reshape(-1, A[nm].shape[-1]),
                     [g.reshape(-1, A[nm].shape[-1])], "adamw_" + nm)
        out[nm] = [r.reshape(A[nm].shape) for r in res]

    received = {}
    after = last_sent

    def scatter_wait(si, after):
        keys, flight = scatter_flights[si]
        landed = _exchange_wait(flight, after, True, f"scatter_wait_{si}")
        received.update(zip(keys, landed))
        return landed[0]

    def slots(nm, l):
        return received[nm, l].reshape(4, -1, received[nm, l].shape[-1])

    def finish(names, sums, tag):
        for nm, mine, theirs in zip(names, sums, _swap_with_sibling(sums, "swap_with_sibling_" + tag)):
            res = _adamw(_as_rows(A[nm]), _as_rows(A['m_' + nm]), _as_rows(A['v_' + nm]), [mine, theirs],
                         "adamw_" + nm)
            out[nm] = [r.reshape(A[nm].shape) for r in res]
        return res[1]

    last = len(scatter_flights) - 1
    for si in range(last):
        after = scatter_wait(si, after)
    late = ffn[:3]
    early = [nm for nm in BIG if nm not in late]
    late_l1 = [_sum_slots([(1, slots(nm, 1))], 2, "sum_" + nm) for nm in late]
    after = finish(early, [_sum_slots([(l, slots(nm, l)) for l in range(A[nm].shape[0])], A[nm].shape[0],
                                      "sum_" + nm) for nm in early], "early")
    scatter_wait(last, after)
    finish(late, [_sum_slots([(0, slots(nm, 0))], 2, "sum_" + nm, into=part) for nm, part in zip(late, late_l1)],
           "late")

    return (loss, dx[None], *[out[nm][0] for nm in WEIGHTS], *[out[nm][1] for nm in WEIGHTS],
            *[out[nm][2] for nm in WEIGHTS], *[out[nm][3] for nm in WEIGHTS])
```

```python
import functools

import jax
import jax.numpy as jnp
import numpy as np
from jax import lax
from jax.experimental import pallas as pl
from jax.experimental.pallas import tpu as pltpu

F32 = jnp.float32
BF16 = jnp.bfloat16
MESH = pl.DeviceIdType.MESH

RMS_EPS = 1e-6
HEADS = 8
NOPE = 128
ROPE = 64
HEAD_PAD = 256
V_HEAD = 128
Q_LORA = 512
KV_LORA = 256
A_PAD = 896
CHUNK = 64
CONV_WIDTH = 31
CONV_HALO = 32
CONV_ROWS = 16
ROPE_THETA = 10000.0
ATTN_SCALE = (NOPE + ROPE) ** -0.5
FFN_RES = 0.5

ADAM_LR = 0.001
ADAM_B1 = 0.9
ADAM_B2 = 0.999
ADAM_EPS = 1e-08
ADAM_WD = 0.01
ADAM_STEP = 10

VMEM_LIMIT_BYTES = 56 * 1024 * 1024

WEIGHTS = ['ffn_norm1', 'ffn1_w1', 'ffn1_w3', 'ffn1_w2', 'mix_norm', 'ffn_norm2', 'ffn2_w1', 'ffn2_w3', 'ffn2_w2',
           'conv_w_pw1', 'conv_w_dw', 'conv_norm', 'conv_w_pw2', 'mla_w_a', 'mla_q_norm', 'mla_kv_norm', 'mla_w_uq',
           'mla_w_ukv', 'mla_w_o', 'final_norm']
INPUTS = (['x', 'positions'] + WEIGHTS + ['loss_target'] + ['m_' + w for w in WEIGHTS] + ['v_' + w for w in WEIGHTS])
BIG = ['ffn1_w1', 'ffn1_w3', 'ffn1_w2', 'ffn2_w1', 'ffn2_w3', 'ffn2_w2', 'conv_w_pw1', 'conv_w_pw2', 'mla_w_a',
       'mla_w_uq', 'mla_w_ukv', 'mla_w_o']


def _params(*sem):
    return pltpu.CompilerParams(dimension_semantics=sem, vmem_limit_bytes=VMEM_LIMIT_BYTES)


def _bf(v):
    return v.astype(BF16)


def _rstd(x):
    return lax.rsqrt(jnp.mean(x * x, axis=-1, keepdims=True) + RMS_EPS)


def _sigmoid(x):
    return jax.nn.sigmoid(x)


def _rot(x):
    lane = lax.broadcasted_iota(jnp.int32, x.shape, 1)
    return jnp.where(lane < ROPE // 2, -pltpu.roll(x, 128 - ROPE // 2, 1), pltpu.roll(x, ROPE // 2, 1))


def _rot_t(y):
    lane = lax.broadcasted_iota(jnp.int32, y.shape, 1)
    return jnp.where(lane < ROPE // 2, pltpu.roll(y, 128 - ROPE // 2, 1), -pltpu.roll(y, ROPE // 2, 1))


def _pair_sum(a_refs, b_refs, trans_b):
    tot = None
    for a_r, b_r in zip(a_refs, b_refs):
        a, b = _bf(a_r[...]), _bf(b_r[...])
        if trans_b:
            d = lax.dot_general(a, b, (((1,), (1,)), ((), ())), preferred_element_type=F32)
        else:
            d = jnp.dot(a, b, preferred_element_type=F32)
        tot = d if tot is None else tot + d
    return tot


def _mm(pairs, out_dtype, name, *, trans_b=False, tm=512, tn=None, tk=None, res=None, dep=None):
    m, k = pairs[0][0].shape
    n = pairs[0][1].shape[0] if trans_b else pairs[0][1].shape[1]
    tm, tn, tk = min(tm, m), tn or n, tk or k
    nk, npair = k // tk, len(pairs)

    def kern(*refs):
        a_refs, b_refs = refs[:npair], refs[npair:2 * npair]
        rest = list(refs[2 * npair:])
        res_ref = rest.pop(0) if res is not None else None
        if dep is not None:
            rest.pop(0)
        o_ref = rest.pop(0)

        def finish(acc):
            if res_ref is not None:
                acc = res_ref[...] + acc
            o_ref[...] = acc.astype(o_ref.dtype)

        if nk == 1:
            finish(_pair_sum(a_refs, b_refs, trans_b))
        else:
            acc_ref = rest.pop(0)
            kk = pl.program_id(2)

            @pl.when(kk == 0)
            def _():
                acc_ref[...] = jnp.zeros_like(acc_ref)

            acc_ref[...] += _pair_sum(a_refs, b_refs, trans_b)

            @pl.when(kk == nk - 1)
            def _():
                finish(acc_ref[...])

    a_spec = pl.BlockSpec((tm, tk), lambda i, j, kk: (i, kk))
    b_spec = (pl.BlockSpec((tn, tk), lambda i, j, kk: (j, kk)) if trans_b
              else pl.BlockSpec((tk, tn), lambda i, j, kk: (kk, j)))
    io_spec = pl.BlockSpec((tm, tn), lambda i, j, kk: (i, j))
    in_specs = ([a_spec] * npair + [b_spec] * npair + ([io_spec] if res is not None else [])
                + ([pl.BlockSpec((8, 128), lambda i, j, kk: (0, 0))] if dep is not None else []))
    args = ([p[0] for p in pairs] + [p[1] for p in pairs] + ([res] if res is not None else [])
            + ([dep] if dep is not None else []))
    return pl.pallas_call(
        kern, name=name, grid=(m // tm, n // tn, nk), in_specs=in_specs, out_specs=io_spec,
        out_shape=jax.ShapeDtypeStruct((m, n), out_dtype),
        scratch_shapes=[pltpu.VMEM((tm, tn), F32)] if nk > 1 else [],
        compiler_params=_params("parallel", "parallel", "arbitrary"))(*args)


def _mm_normbwd(pairs, h, g, dres, dep, name, *, tm=512, tk=None):
    m, k = pairs[0][0].shape
    d = pairs[0][1].shape[0]
    tm, tk = min(tm, m), tk or k
    nk, npair = k // tk, len(pairs)

    def kern(*refs):
        a_refs, b_refs = refs[:npair], refs[npair:2 * npair]
        h_ref, g_ref, dres_ref, _, o_ref, dg_ref, acc_ref = refs[2 * npair:]
        i, kk = pl.program_id(0), pl.program_id(1)

        @pl.when(jnp.logical_and(i == 0, kk == 0))
        def _():
            dg_ref[...] = jnp.zeros_like(dg_ref)

        @pl.when(kk == 0)
        def _():
            acc_ref[...] = jnp.zeros_like(acc_ref)

        acc_ref[...] += _pair_sum(a_refs, b_refs, True)

        @pl.when(kk == nk - 1)
        def _():
            dn = acc_ref[...]
            x = h_ref[...]
            rstd = _rstd(x)
            xhat = x * rstd
            dg_ref[...] += jnp.broadcast_to(jnp.sum(dn * xhat, axis=0, keepdims=True), dg_ref.shape)
            dxh = dn * g_ref[...]
            dx = rstd * (dxh - xhat * jnp.mean(dxh * xhat, axis=-1, keepdims=True))
            o_ref[...] = dres_ref[...] + dx

    row = pl.BlockSpec((tm, d), lambda i, kk: (i, 0))
    in_specs = ([pl.BlockSpec((tm, tk), lambda i, kk: (i, kk))] * npair
                + [pl.BlockSpec((d, tk), lambda i, kk: (0, kk))] * npair
                + [row, pl.BlockSpec((1, d), lambda i, kk: (0, 0)), row, pl.BlockSpec((8, 128), lambda i, kk: (0, 0))])
    return pl.pallas_call(
        kern, name=name, grid=(m // tm, nk), in_specs=in_specs,
        out_specs=[row, pl.BlockSpec((8, d), lambda i, kk: (0, 0))],
        out_shape=[jax.ShapeDtypeStruct((m, d), F32), jax.ShapeDtypeStruct((8, d), F32)],
        scratch_shapes=[pltpu.VMEM((tm, d), F32)],
        compiler_params=_params("arbitrary", "arbitrary"))(
            *[p[0] for p in pairs], *[p[1] for p in pairs], h, g.reshape(1, d), dres, dep)


def _mm_tn(a, b, out_dtype, name, *, bm=None, bn=None, tk=1024):
    t, m = a.shape
    batched = b.ndim == 3
    n = b.shape[-1]
    nb = b.shape[0] if batched else 1
    bm, bn, tk = bm or m, bn or n, min(tk, t)
    nk = t // tk

    def kern(a_ref, b_ref, o_ref, acc_ref):
        kk = pl.program_id(3)

        @pl.when(kk == 0)
        def _():
            acc_ref[...] = jnp.zeros_like(acc_ref)

        acc_ref[...] += lax.dot_general(_bf(a_ref[...]), _bf(b_ref[...]), (((0,), (0,)), ((), ())),
                                        preferred_element_type=F32)

        @pl.when(kk == nk - 1)
        def _():
            o_ref[...] = acc_ref[...].astype(o_ref.dtype)

    a_spec = pl.BlockSpec((tk, bm), lambda h, i, j, kk: (kk, i))
    if batched:
        b_spec = pl.BlockSpec((None, tk, bn), lambda h, i, j, kk: (h, kk, j))
        o_spec = pl.BlockSpec((None, bm, bn), lambda h, i, j, kk: (h, i, j))
        out_shape = jax.ShapeDtypeStruct((nb, m, n), out_dtype)
    else:
        b_spec = pl.BlockSpec((tk, bn), lambda h, i, j, kk: (kk, j))
        o_spec = pl.BlockSpec((bm, bn), lambda h, i, j, kk: (i, j))
        out_shape = jax.ShapeDtypeStruct((m, n), out_dtype)
    return pl.pallas_call(
        kern, name=name, grid=(nb, m // bm, n // bn, nk), in_specs=[a_spec, b_spec], out_specs=o_spec,
        out_shape=out_shape, scratch_shapes=[pltpu.VMEM((bm, bn), F32)],
        compiler_params=_params("parallel", "parallel", "parallel", "arbitrary"))(a, b)


def _ffn_tile(f):
    return f // 2 if (f // 2) % 128 == 0 else f


def _ffn_fwd(h, g, w1, w3, w2, dep, name):
    t, d = h.shape
    f = w1.shape[1]
    tm = min(256, t)

    def kern(h_ref, g_ref, w1_hbm, w3_hbm, w2_hbm, dep_ref, ho_ref, n_ref, z1_ref, z3_ref,
             w1_ref, w3_ref, w2_ref, sems):
        @pl.when(pl.program_id(0) == 0)
        def _():
            copies = [pltpu.make_async_copy(src, dst, sems.at[k]) for k, (src, dst) in
                      enumerate(((w1_hbm, w1_ref), (w3_hbm, w3_ref), (w2_hbm, w2_ref)))]
            for cp in copies:
                cp.start()
            for cp in copies:
                cp.wait()

        x = h_ref[...]
        n = _bf(x * _rstd(x) * g_ref[...])
        n_ref[...] = n
        z1 = jnp.dot(n, w1_ref[...], preferred_element_type=F32)
        z3 = jnp.dot(n, w3_ref[...], preferred_element_type=F32)
        z1_ref[...] = _bf(z1)
        z3_ref[...] = _bf(z3)
        act = _bf(z1 * _sigmoid(z1) * z3)
        ho_ref[...] = x + FFN_RES * jnp.dot(act, w2_ref[...], preferred_element_type=F32)

    row = pl.BlockSpec((tm, d), lambda i: (i, 0))
    col = pl.BlockSpec((tm, f), lambda i: (i, 0))
    whole = pl.BlockSpec(memory_space=pl.ANY)
    return pl.pallas_call(
        kern, name=name, grid=(t // tm,),
        in_specs=[row, pl.BlockSpec((1, d), lambda i: (0, 0)), whole, whole, whole,
                  pl.BlockSpec((8, 128), lambda i: (0, 0))],
        out_specs=[row, row, col, col],
        out_shape=[jax.ShapeDtypeStruct((t, d), F32), jax.ShapeDtypeStruct((t, d), BF16),
                   jax.ShapeDtypeStruct((t, f), BF16), jax.ShapeDtypeStruct((t, f), BF16)],
        scratch_shapes=[pltpu.VMEM((d, f), BF16), pltpu.VMEM((d, f), BF16), pltpu.VMEM((f, d), BF16),
                        pltpu.SemaphoreType.DMA((3,))],
        compiler_params=_params("arbitrary"))(h, g.reshape(1, d), w1, w3, w2, dep)


def _ffn_bwd_x(dh, h_in, g, z1, z3, w1, w3, w2, dep, name):
    t, d = dh.shape
    f = z1.shape[1]
    tm = min(256, t)

    def kern(dh_ref, h_ref, g_ref, z1_ref, z3_ref, w2_hbm, w1_hbm, w3_hbm, dep_ref,
             o_ref, dg_ref, dz1_ref, dz3_ref, a_ref, df_ref, w2_ref, w1_ref, w3_ref, sems):
        @pl.when(pl.program_id(0) == 0)
        def _():
            copies = [pltpu.make_async_copy(src, dst, sems.at[k]) for k, (src, dst) in
                      enumerate(((w2_hbm, w2_ref), (w1_hbm, w1_ref), (w3_hbm, w3_ref)))]
            for cp in copies:
                cp.start()
            dg_ref[...] = jnp.zeros_like(dg_ref)
            for cp in copies:
                cp.wait()

        df = _bf(FFN_RES * dh_ref[...])
        df_ref[...] = df
        da = lax.dot_general(df, w2_ref[...], (((1,), (1,)), ((), ())), preferred_element_type=F32)
        z1v, z3v = z1_ref[...].astype(F32), z3_ref[...].astype(F32)
        sig = _sigmoid(z1v)
        silu = z1v * sig
        a_ref[...] = _bf(silu * z3v)
        dz1 = _bf(da * z3v * (sig * (1.0 + z1v * (1.0 - sig))))
        dz3 = _bf(da * silu)
        dz1_ref[...] = dz1
        dz3_ref[...] = dz3
        dn = (lax.dot_general(dz1, w1_ref[...], (((1,), (1,)), ((), ())), preferred_element_type=F32)
              + lax.dot_general(dz3, w3_ref[...], (((1,), (1,)), ((), ())), preferred_element_type=F32))
        x = h_ref[...]
        rstd = _rstd(x)
        xhat = x * rstd
        dg_ref[...] += jnp.broadcast_to(jnp.sum(dn * xhat, axis=0, keepdims=True), dg_ref.shape)
        dxh = dn * g_ref[...]
        o_ref[...] = dh_ref[...] + rstd * (dxh - xhat * jnp.mean(dxh * xhat, axis=-1, keepdims=True))

    row = pl.BlockSpec((tm, d), lambda i: (i, 0))
    col = pl.BlockSpec((tm, f), lambda i: (i, 0))
    whole = pl.BlockSpec(memory_space=pl.ANY)
    colshape = jax.ShapeDtypeStruct((t, f), BF16)
    return pl.pallas_call(
        kern, name=name, grid=(t // tm,),
        in_specs=[row, row, pl.BlockSpec((1, d), lambda i: (0, 0)), col, col, whole, whole, whole,
                  pl.BlockSpec((8, 128), lambda i: (0, 0))],
        out_specs=[row, pl.BlockSpec((8, d), lambda i: (0, 0)), col, col, col, row],
        out_shape=[jax.ShapeDtypeStruct((t, d), F32), jax.ShapeDtypeStruct((8, d), F32), colshape, colshape, colshape,
                   jax.ShapeDtypeStruct((t, d), BF16)],
        scratch_shapes=[pltpu.VMEM((f, d), BF16), pltpu.VMEM((d, f), BF16), pltpu.VMEM((d, f), BF16),
                        pltpu.SemaphoreType.DMA((3,))],
        compiler_params=_params("arbitrary"))(dh, h_in, g.reshape(1, d), z1, z3, w2, w1, w3, dep)


def _ffn_bwd(dh, h_in, g, n, z1, z3, w1, w3, w2, dep, tag):
    f = w1.shape[1]
    dh_in, dg, dz1, dz3, act, df = _ffn_bwd_x(dh, h_in, g, z1, z3, w1, w3, w2, dep, tag + "_bwd_x")
    dw1 = _mm_tn(n, dz1, BF16, tag + "_dw1", bn=_ffn_tile(f))
    dw3 = _mm_tn(n, dz3, BF16, tag + "_dw3", bn=_ffn_tile(f))
    dw2 = _mm_tn(act, df, BF16, tag + "_dw2", bm=_ffn_tile(f))
    return dh_in, dg, dw1, dw3, dw2


def _norm_fwd(h, g, dep, name):
    t, d = h.shape
    tm = min(512, t)

    def kern(h_ref, g_ref, dep_ref, o_ref):
        x = h_ref[...]
        o_ref[...] = _bf(x * _rstd(x) * g_ref[...])

    row = pl.BlockSpec((tm, d), lambda i: (i, 0))
    return pl.pallas_call(
        kern, name=name, grid=(t // tm,),
        in_specs=[row, pl.BlockSpec((1, d), lambda i: (0, 0)), pl.BlockSpec((8, 128), lambda i: (0, 0))],
        out_specs=row, out_shape=jax.ShapeDtypeStruct((t, d), BF16),
        compiler_params=_params("parallel"))(h, g.reshape(1, d), dep)


def _loss_bwd(h, target, g):
    t, d = h.shape
    tm = min(512, t)

    def kern(h_ref, t_ref, g_ref, dh_ref, dg_ref, loss_ref):
        @pl.when(pl.program_id(0) == 0)
        def _():
            dg_ref[...] = jnp.zeros_like(dg_ref)
            loss_ref[...] = jnp.zeros_like(loss_ref)

        x = h_ref[...]
        rstd = _rstd(x)
        xhat = x * rstd
        err = xhat * g_ref[...] - t_ref[...]
        row_loss = jnp.sum(err * err, axis=-1, keepdims=True) * (0.5 / d)
        loss_ref[...] += jnp.broadcast_to(jnp.sum(row_loss, axis=0, keepdims=True), loss_ref.shape)
        dy = err * (1.0 / d)
        dg_ref[...] += jnp.broadcast_to(jnp.sum(dy * xhat, axis=0, keepdims=True), dg_ref.shape)
        dxh = dy * g_ref[...]
        dh_ref[...] = rstd * (dxh - xhat * jnp.mean(dxh * xhat, axis=-1, keepdims=True))

    row = pl.BlockSpec((tm, d), lambda i: (i, 0))
    return pl.pallas_call(
        kern, name="loss_bwd", grid=(t // tm,),
        in_specs=[row, row, pl.BlockSpec((1, d), lambda i: (0, 0))],
        out_specs=[row, pl.BlockSpec((8, d), lambda i: (0, 0)), pl.BlockSpec((8, 128), lambda i: (0, 0))],
        out_shape=[jax.ShapeDtypeStruct((t, d), F32), jax.ShapeDtypeStruct((8, d), F32),
                   jax.ShapeDtypeStruct((8, 128), F32)],
        compiler_params=_params("arbitrary"))(h, target, g.reshape(1, d))


def _glu_fwd(m, wa, wb):
    t, d = m.shape
    c = wa.shape[1]
    tm, tc = min(512, t), min(512, c)

    def kern(m_ref, wa_ref, wb_ref, a_ref, b_ref, glu_ref):
        mv = m_ref[...]
        a = jnp.dot(mv, wa_ref[...], preferred_element_type=F32)
        b = jnp.dot(mv, wb_ref[...], preferred_element_type=F32)
        a_ref[...] = _bf(a)
        b_ref[...] = _bf(b)
        glu_ref[...] = _bf(a * _sigmoid(b))

    col = pl.BlockSpec((tm, tc), lambda i, j: (i, j))
    wspec = pl.BlockSpec((d, tc), lambda i, j: (0, j))
    shape = jax.ShapeDtypeStruct((t, c), BF16)
    return pl.pallas_call(
        kern, name="conv_glu_fwd", grid=(t // tm, c // tc),
        in_specs=[pl.BlockSpec((tm, d), lambda i, j: (i, 0)), wspec, wspec], out_specs=[col, col, col],
        out_shape=[shape, shape, shape], compiler_params=_params("parallel", "parallel"))(m, wa, wb)


def _conv_tile(t):
    return min(256, t)


def _shift_copies(ext, shifted, rows):
    for s in range(8):
        shifted[s] = ext[pl.ds(s, rows), :]


def _shifted_rows(shifted, start, nrows):
    return shifted[start % 8, pl.ds(start - start % 8, nrows), :]


def _conv_fwd(glu, w_dw, g):
    t, c = glu.shape
    tm = _conv_tile(t)
    hb = tm // CONV_HALO

    def kern(cur_ref, halo_ref, w_ref, g_ref, cv_ref, s_ref, ext, shifted):
        i = pl.program_id(0)
        ext[0:CONV_HALO, :] = jnp.where(i > 0, halo_ref[...].astype(F32), 0.0)
        ext[CONV_HALO:tm + CONV_HALO, :] = cur_ref[...].astype(F32)
        ext[tm + CONV_HALO:, :] = jnp.zeros((8, c), F32)
        _shift_copies(ext, shifted, tm + CONV_HALO)
        gv = g_ref[...]
        for r0 in range(0, tm, CONV_ROWS):
            acc = jnp.zeros((CONV_ROWS, c), F32)
            for k in range(CONV_WIDTH):
                acc = acc + _shifted_rows(shifted, r0 + 2 + k, CONV_ROWS) * w_ref[k:k + 1, :]
            cv_ref[r0:r0 + CONV_ROWS, :] = acc
            rn = acc * _rstd(acc) * gv
            s_ref[r0:r0 + CONV_ROWS, :] = _bf(rn * _sigmoid(rn))

    row = pl.BlockSpec((tm, c), lambda i: (i, 0))
    return pl.pallas_call(
        kern, name="conv_fwd", grid=(t // tm,),
        in_specs=[row, pl.BlockSpec((CONV_HALO, c), lambda i: (jnp.maximum(i * hb - 1, 0), 0)),
                  pl.BlockSpec((CONV_HALO, c), lambda i: (0, 0)), pl.BlockSpec((1, c), lambda i: (0, 0))],
        out_specs=[row, row],
        out_shape=[jax.ShapeDtypeStruct((t, c), F32), jax.ShapeDtypeStruct((t, c), BF16)],
        scratch_shapes=[pltpu.VMEM((tm + CONV_HALO + 8, c), F32), pltpu.VMEM((8, tm + CONV_HALO, c), F32)],
        compiler_params=_params("parallel"))(glu, glu, w_dw, g.reshape(1, c))


def _conv_bwd_norm(dh, cv, w_pw2, g, dep):
    t, c = cv.shape
    tm = min(512, t)

    def kern(dh_ref, cv_ref, w_ref, g_ref, dep_ref, dcv_ref, dg_ref):
        @pl.when(pl.program_id(0) == 0)
        def _():
            dg_ref[...] = jnp.zeros_like(dg_ref)

        ds = lax.dot_general(_bf(dh_ref[...]), w_ref[...], (((1,), (1,)), ((), ())), preferred_element_type=F32)
        x = cv_ref[...]
        rstd = _rstd(x)
        xhat = x * rstd
        rn = xhat * g_ref[...]
        sig = _sigmoid(rn)
        drn = ds * (sig * (1.0 + rn * (1.0 - sig)))
        dg_ref[...] += jnp.broadcast_to(jnp.sum(drn * xhat, axis=0, keepdims=True), dg_ref.shape)
        dxh = drn * g_ref[...]
        dcv_ref[...] = rstd * (dxh - xhat * jnp.mean(dxh * xhat, axis=-1, keepdims=True))

    row = pl.BlockSpec((tm, c), lambda i: (i, 0))
    return pl.pallas_call(
        kern, name="conv_bwd_norm", grid=(t // tm,),
        in_specs=[pl.BlockSpec((tm, dh.shape[1]), lambda i: (i, 0)), row,
                  pl.BlockSpec(w_pw2.shape, lambda i: (0, 0)), pl.BlockSpec((1, c), lambda i: (0, 0)),
                  pl.BlockSpec((8, 128), lambda i: (0, 0))],
        out_specs=[row, pl.BlockSpec((8, c), lambda i: (0, 0))],
        out_shape=[jax.ShapeDtypeStruct((t, c), F32), jax.ShapeDtypeStruct((8, c), F32)],
        compiler_params=_params("arbitrary"))(dh, cv, w_pw2, g.reshape(1, c), dep)


def _conv_bwd_dw(dcv, glu, a, b, w_dw):
    t, c = dcv.shape
    tm = _conv_tile(t)
    hb = tm // CONV_HALO
    last = t // CONV_HALO - 1

    def kern(dcv_ref, dnext_ref, glu_ref, gprev_ref, a_ref, b_ref, w_ref, da_ref, db_ref, dw_ref,
             dext, gext, dshift, gshift):
        i = pl.program_id(0)

        @pl.when(i == 0)
        def _():
            dw_ref[...] = jnp.zeros_like(dw_ref)

        dext[0:tm, :] = dcv_ref[...]
        dext[tm:tm + CONV_HALO, :] = jnp.where(i < t // tm - 1, dnext_ref[...], 0.0)
        dext[tm + CONV_HALO:, :] = jnp.zeros((8, c), F32)
        gext[0:CONV_HALO, :] = jnp.where(i > 0, gprev_ref[...].astype(F32), 0.0)
        gext[CONV_HALO:tm + CONV_HALO, :] = glu_ref[...].astype(F32)
        gext[tm + CONV_HALO:, :] = jnp.zeros((8, c), F32)
        _shift_copies(dext, dshift, tm + CONV_HALO)
        _shift_copies(gext, gshift, tm + CONV_HALO)
        for r0 in range(0, tm, CONV_ROWS):
            acc = jnp.zeros((CONV_ROWS, c), F32)
            for k in range(CONV_WIDTH):
                acc = acc + _shifted_rows(dshift, r0 + CONV_WIDTH - 1 - k, CONV_ROWS) * w_ref[k:k + 1, :]
            av = a_ref[r0:r0 + CONV_ROWS, :].astype(F32)
            sig = _sigmoid(b_ref[r0:r0 + CONV_ROWS, :].astype(F32))
            da_ref[r0:r0 + CONV_ROWS, :] = _bf(acc * sig)
            db_ref[r0:r0 + CONV_ROWS, :] = _bf(acc * av * sig * (1.0 - sig))
        for k in range(CONV_WIDTH):
            acc = jnp.zeros((CONV_ROWS, c), F32)
            for r0 in range(0, tm, CONV_ROWS):
                acc = acc + _shifted_rows(gshift, r0 + 2 + k, CONV_ROWS) * dext[r0:r0 + CONV_ROWS, :]
            dw_ref[k:k + 1, :] += jnp.sum(acc, axis=0, keepdims=True)

    row = pl.BlockSpec((tm, c), lambda i: (i, 0))
    shape = jax.ShapeDtypeStruct((t, c), BF16)
    return pl.pallas_call(
        kern, name="conv_bwd_dw", grid=(t // tm,),
        in_specs=[row, pl.BlockSpec((CONV_HALO, c), lambda i: (jnp.minimum((i + 1) * hb, last), 0)),
                  row, pl.BlockSpec((CONV_HALO, c), lambda i: (jnp.maximum(i * hb - 1, 0), 0)),
                  row, row, pl.BlockSpec((CONV_HALO, c), lambda i: (0, 0))],
        out_specs=[row, row, pl.BlockSpec((CONV_HALO, c), lambda i: (0, 0))],
        out_shape=[shape, shape, jax.ShapeDtypeStruct((CONV_HALO, c), F32)],
        scratch_shapes=[pltpu.VMEM((tm + CONV_HALO + 8, c), F32), pltpu.VMEM((tm + CONV_HALO + 8, c), F32),
                        pltpu.VMEM((8, tm + CONV_HALO, c), F32), pltpu.VMEM((8, tm + CONV_HALO, c), F32)],
        compiler_params=_params("arbitrary"))(dcv, dcv, glu, glu, a, b, w_dw)


def _rope_tables(pos):
    t = pos.shape[0]
    tm = min(512, t)
    freq = (np.float32(ROPE_THETA) ** (np.float32(-2.0) * np.arange(ROPE // 2, dtype=np.float32)
                                       / np.float32(ROPE))).astype(np.float32)
    row = np.zeros((2, 128), np.float32)
    row[0, :ROPE] = np.concatenate([freq, freq])
    row[1, :ROPE] = 1.0

    def kern(pos_ref, f_ref, c_ref, s_ref):
        ang = pos_ref[...].astype(F32) * f_ref[0:1, :]
        mask = f_ref[1:2, :]
        c_ref[...] = jnp.cos(ang) * mask
        s_ref[...] = jnp.sin(ang) * mask

    out = pl.BlockSpec((tm, 128), lambda i: (i, 0))
    shape = jax.ShapeDtypeStruct((t, 128), F32)
    return pl.pallas_call(
        kern, name="rope_tables", grid=(t // tm,),
        in_specs=[pl.BlockSpec((tm, 1), lambda i: (i, 0)), pl.BlockSpec((2, 128), lambda i: (0, 0))],
        out_specs=[out, out], out_shape=[shape, shape], compiler_params=_params("parallel"))(pos, jnp.asarray(row))


def _mla_prep(a, gq, gkv, cs_c, cs_s):
    t = a.shape[0]
    tm = min(512, t)
    kv0, r0 = Q_LORA, Q_LORA + KV_LORA

    def kern(a_ref, gq_ref, gkv_ref, c_ref, s_ref, cq_ref, ckv_ref, kr_ref):
        aq = a_ref[:, 0:kv0]
        akv = a_ref[:, kv0:r0]
        ar = a_ref[:, r0:A_PAD]
        cq_ref[...] = _bf(aq * _rstd(aq) * gq_ref[...])
        ckv_ref[...] = _bf(akv * _rstd(akv) * gkv_ref[...])
        kr_ref[...] = _bf(ar * c_ref[...] + _rot(ar) * s_ref[...])

    def row(w):
        return pl.BlockSpec((tm, w), lambda i: (i, 0))

    def vec(w):
        return pl.BlockSpec((1, w), lambda i: (0, 0))

    return pl.pallas_call(
        kern, name="mla_prep", grid=(t // tm,),
        in_specs=[row(A_PAD), vec(Q_LORA), vec(KV_LORA), row(128), row(128)],
        out_specs=[row(Q_LORA), row(KV_LORA), row(128)],
        out_shape=[jax.ShapeDtypeStruct((t, Q_LORA), BF16), jax.ShapeDtypeStruct((t, KV_LORA), BF16),
                   jax.ShapeDtypeStruct((t, 128), BF16)],
        compiler_params=_params("parallel"))(a, gq.reshape(1, -1), gkv.reshape(1, -1), cs_c, cs_s)


def _mla_prep_bwd(a, dcq, dckv, dar, gq, gkv):
    t = a.shape[0]
    tm = min(512, t)
    kv0, r0 = Q_LORA, Q_LORA + KV_LORA

    def kern(a_ref, dcq_ref, dckv_ref, dar_ref, gq_ref, gkv_ref, da_ref, dgq_ref, dgkv_ref):
        @pl.when(pl.program_id(0) == 0)
        def _():
            dgq_ref[...] = jnp.zeros_like(dgq_ref)
            dgkv_ref[...] = jnp.zeros_like(dgkv_ref)

        def back(x, dy, g_ref, dg_ref):
            rstd = _rstd(x)
            xhat = x * rstd
            dg_ref[...] += jnp.broadcast_to(jnp.sum(dy * xhat, axis=0, keepdims=True), dg_ref.shape)
            dxh = dy * g_ref[...]
            return rstd * (dxh - xhat * jnp.mean(dxh * xhat, axis=-1, keepdims=True))

        da_ref[:, 0:kv0] = _bf(back(a_ref[:, 0:kv0], dcq_ref[...], gq_ref, dgq_ref))
        da_ref[:, kv0:r0] = _bf(back(a_ref[:, kv0:r0], dckv_ref[...], gkv_ref, dgkv_ref))
        da_ref[:, r0:A_PAD] = _bf(dar_ref[...])

    def row(w):
        return pl.BlockSpec((tm, w), lambda i: (i, 0))

    def vec(r, w):
        return pl.BlockSpec((r, w), lambda i: (0, 0))

    return pl.pallas_call(
        kern, name="mla_prep_bwd", grid=(t // tm,),
        in_specs=[row(A_PAD), row(Q_LORA), row(KV_LORA), row(128), vec(1, Q_LORA), vec(1, KV_LORA)],
        out_specs=[row(A_PAD), vec(8, Q_LORA), vec(8, KV_LORA)],
        out_shape=[jax.ShapeDtypeStruct((t, A_PAD), BF16), jax.ShapeDtypeStruct((8, Q_LORA), F32),
                   jax.ShapeDtypeStruct((8, KV_LORA), F32)],
        compiler_params=_params("arbitrary"))(a, dcq, dckv, dar, gq.reshape(1, -1), gkv.reshape(1, -1))


def _mla_qkv(cq, ckv, kr, cs_c, cs_s, wuq, wukv):
    t = cq.shape[0]
    tm = min(512, t)
    kvw = NOPE + V_HEAD

    def kern(cq_ref, ckv_ref, kr_ref, c_ref, s_ref, wq_ref, wkv_ref, q_ref, k_ref, v_ref):
        r = jnp.dot(cq_ref[...], wq_ref[...], preferred_element_type=F32)
        kv = jnp.dot(ckv_ref[...], wkv_ref[...], preferred_element_type=F32)
        cv, sv, krv = c_ref[...], s_ref[...], kr_ref[...]
        for h in range(HEADS):
            xr = r[:, h * HEAD_PAD + NOPE:(h + 1) * HEAD_PAD]
            q_ref[h, :, 0:NOPE] = _bf(r[:, h * HEAD_PAD:h * HEAD_PAD + NOPE] * ATTN_SCALE)
            q_ref[h, :, NOPE:] = _bf((xr * cv + _rot(xr) * sv) * ATTN_SCALE)
            k_ref[h, :, 0:NOPE] = _bf(kv[:, h * kvw:h * kvw + NOPE])
            k_ref[h, :, NOPE:] = krv
            v_ref[h] = _bf(kv[:, h * kvw + NOPE:(h + 1) * kvw])

    def row(w):
        return pl.BlockSpec((tm, w), lambda i: (i, 0))

    def heads(w):
        return pl.BlockSpec((HEADS, tm, w), lambda i: (0, i, 0))

    return pl.pallas_call(
        kern, name="mla_qkv", grid=(t // tm,),
        in_specs=[row(Q_LORA), row(KV_LORA), row(128), row(128), row(128),
                  pl.BlockSpec(wuq.shape, lambda i: (0, 0)), pl.BlockSpec(wukv.shape, lambda i: (0, 0))],
        out_specs=[heads(HEAD_PAD), heads(HEAD_PAD), heads(V_HEAD)],
        out_shape=[jax.ShapeDtypeStruct((HEADS, t, HEAD_PAD), BF16), jax.ShapeDtypeStruct((HEADS, t, HEAD_PAD), BF16),
                   jax.ShapeDtypeStruct((HEADS, t, V_HEAD), BF16)],
        compiler_params=_params("parallel"))(cq, ckv, kr, cs_c, cs_s, wuq, wukv)


def _mla_qkv_bwd(dq, dk, dv, cs_c, cs_s, wuq, wukv):
    t = dq.shape[1]
    tm = min(256, t)
    kvw = NOPE + V_HEAD

    def kern(dq_ref, dk_ref, dv_ref, c_ref, s_ref, wq_ref, wkv_ref, dr_ref, dkv_ref, dcq_ref, dckv_ref, dar_ref):
        cv, sv = c_ref[...], s_ref[...]
        dar = jnp.zeros_like(cv)
        for h in range(HEADS):
            dqx = dq_ref[h, :, NOPE:]
            dr_ref[:, h * HEAD_PAD:h * HEAD_PAD + NOPE] = _bf(dq_ref[h, :, 0:NOPE] * ATTN_SCALE)
            dr_ref[:, h * HEAD_PAD + NOPE:(h + 1) * HEAD_PAD] = _bf((dqx * cv + _rot_t(dqx * sv)) * ATTN_SCALE)
            dkx = dk_ref[h, :, NOPE:]
            dar = dar + (dkx * cv + _rot_t(dkx * sv))
            dkv_ref[:, h * kvw:h * kvw + NOPE] = _bf(dk_ref[h, :, 0:NOPE])
            dkv_ref[:, h * kvw + NOPE:(h + 1) * kvw] = _bf(dv_ref[h])
        dar_ref[...] = dar
        dcq_ref[...] = lax.dot_general(dr_ref[...], wq_ref[...], (((1,), (1,)), ((), ())),
                                       preferred_element_type=F32)
        dckv_ref[...] = lax.dot_general(dkv_ref[...], wkv_ref[...], (((1,), (1,)), ((), ())),
                                        preferred_element_type=F32)

    def row(w):
        return pl.BlockSpec((tm, w), lambda i: (i, 0))

    def heads(w):
        return pl.BlockSpec((HEADS, tm, w), lambda i: (0, i, 0))

    return pl.pallas_call(
        kern, name="mla_qkv_bwd", grid=(t // tm,),
        in_specs=[heads(HEAD_PAD), heads(HEAD_PAD), heads(V_HEAD), row(128), row(128),
                  pl.BlockSpec(wuq.shape, lambda i: (0, 0)), pl.BlockSpec(wukv.shape, lambda i: (0, 0))],
        out_specs=[row(HEADS * HEAD_PAD), row(HEADS * kvw), row(Q_LORA), row(KV_LORA), row(128)],
        out_shape=[jax.ShapeDtypeStruct((t, HEADS * HEAD_PAD), BF16), jax.ShapeDtypeStruct((t, HEADS * kvw), BF16),
                   jax.ShapeDtypeStruct((t, Q_LORA), F32), jax.ShapeDtypeStruct((t, KV_LORA), F32),
                   jax.ShapeDtypeStruct((t, 128), F32)],
        compiler_params=_params("parallel"))(dq, dk, dv, cs_c, cs_s, wuq, wukv)


def _attn_block(t):
    return 512 if t >= 4096 else 128


def _fold_rows(x, op):
    r = x.shape[0]
    while r > 8:
        r //= 2
        x = op(x[:r], x[r:])
    return x


def _chunk_mask(bk, bq):
    kc = lax.broadcasted_iota(jnp.int32, (bk, bq), 0) // CHUNK
    qc = lax.broadcasted_iota(jnp.int32, (bk, bq), 1) // CHUNK
    return qc >= kc


def _flash_fwd(q, k, v):
    t = q.shape[1]
    bq = _attn_block(t)
    nq = t // bq

    def kern(q_ref, k_ref, v_ref, o_ref, lse_ref, s_buf, p_buf, m_ref, l_ref, acc_ref):
        i = pl.program_id(1)
        queries = (q_ref[0:bq, :], q_ref[bq:2 * bq, :])

        def block(j):
            rows = pl.ds(pl.multiple_of(j * bq, bq), bq)
            return k_ref[rows, :], v_ref[rows, :]

        def scores(kj, chain):
            return lax.dot_general(kj, queries[chain], (((1,), (1,)), ((), ())), preferred_element_type=F32)

        def softmax_block(chain, slot, vj):
            for c0 in range(0, bq, 128):
                cols = slice(c0, c0 + 128)
                s = s_buf[slot, chain, :, cols]
                m_old = m_ref[chain, 0:1, cols]
                m_new = jnp.maximum(m_old, jnp.max(s, axis=0, keepdims=True))
                alpha = jnp.exp(m_old - m_new)
                p = jnp.exp(s - m_new)
                l_ref[chain, 0:1, cols] = alpha * l_ref[chain, 0:1, cols] + jnp.sum(p, axis=0, keepdims=True)
                m_ref[chain, 0:1, cols] = m_new
                p_buf[chain, :, cols] = _bf(p)
                acc_ref[chain, :, cols] = acc_ref[chain, :, cols] * alpha
            acc_ref[chain] += lax.dot_general(vj, p_buf[chain], (((0,), (0,)), ((), ())),
                                              preferred_element_type=F32)

        m_ref[...] = jnp.full(m_ref.shape, -1e30, F32)
        l_ref[...] = jnp.zeros_like(l_ref)
        acc_ref[...] = jnp.zeros_like(acc_ref)
        mask = _chunk_mask(bq, bq)
        k0, v0 = block(2 * i)
        k1, v1 = block(2 * i + 1)
        s_buf[0, 0] = jnp.where(mask, scores(k0, 0), -1e30)
        s_buf[0, 1] = scores(k0, 1)
        s_buf[1, 1] = jnp.where(mask, scores(k1, 1), -1e30)
        softmax_block(0, 0, v0)
        softmax_block(1, 0, v0)
        softmax_block(1, 1, v1)
        kf = block(0)[0]
        s_buf[0, 0] = scores(kf, 0)
        s_buf[0, 1] = scores(kf, 1)

        def body(pair, carry):
            for cur in range(2):
                j = 2 * pair + cur
                kn = block(jnp.minimum(j + 1, jnp.maximum(2 * i - 1, 0)))[0]
                s_buf[1 - cur, 0] = scores(kn, 0)
                s_buf[1 - cur, 1] = scores(kn, 1)
                vj = block(j)[1]
                softmax_block(0, cur, vj)
                softmax_block(1, cur, vj)
            return carry

        lax.fori_loop(0, i, body, 0)
        for chain in range(2):
            l = l_ref[chain, 0:1, :]
            o_ref[chain * bq:(chain + 1) * bq, :] = _bf((acc_ref[chain] / l).T)
            lse_ref[chain] = jnp.broadcast_to(m_ref[chain, 0:1, :] + jnp.log(l), (8, bq))

    return pl.pallas_call(
        kern, name="flash_fwd", grid=(HEADS, nq // 2),
        in_specs=[pl.BlockSpec((None, 2 * bq, HEAD_PAD), lambda h, i: (h, i, 0)),
                  pl.BlockSpec((None, t, HEAD_PAD), lambda h, i: (h, 0, 0)),
                  pl.BlockSpec((None, t, V_HEAD), lambda h, i: (h, 0, 0))],
        out_specs=[pl.BlockSpec((2 * bq, V_HEAD), lambda h, i: (i, h)),
                   pl.BlockSpec((None, 2, 8, bq), lambda h, i: (h, i, 0, 0))],
        out_shape=[jax.ShapeDtypeStruct((t, HEADS * V_HEAD), BF16), jax.ShapeDtypeStruct((HEADS, nq, 8, bq), F32)],
        scratch_shapes=[pltpu.VMEM((2, 2, bq, bq), F32), pltpu.VMEM((2, bq, bq), BF16), pltpu.VMEM((2, 8, bq), F32),
                        pltpu.VMEM((2, 8, bq), F32), pltpu.VMEM((2, V_HEAD, bq), F32)],
        compiler_params=_params("parallel", "arbitrary"))(q, k, v)


def _attn_delta(do, o):
    t = do.shape[0]
    bq = _attn_block(t)

    def kern(do_ref, o_ref, d_ref):
        for h in range(HEADS):
            cols = slice(h * V_HEAD, (h + 1) * V_HEAD)
            prod = do_ref[:, cols].astype(F32) * o_ref[:, cols].astype(F32)
            d_ref[h] = jnp.broadcast_to(jnp.sum(prod.T, axis=0, keepdims=True), (8, bq))

    blk = pl.BlockSpec((bq, HEADS * V_HEAD), lambda i: (i, 0))
    return pl.pallas_call(
        kern, name="attn_delta", grid=(t // bq,), in_specs=[blk, blk],
        out_specs=pl.BlockSpec((HEADS, None, 8, bq), lambda i: (0, i, 0, 0)),
        out_shape=jax.ShapeDtypeStruct((HEADS, t // bq, 8, bq), F32),
        compiler_params=_params("parallel"))(do, o)


def _flash_bwd(q, k, v, do, lse, delta):
    t = q.shape[1]
    bq = _attn_block(t)
    nq = t // bq

    def kern(q_ref, k_ref, v_ref, do_ref, lse_ref, del_ref, dq_ref, dk_ref, dv_ref, dvt_ref):
        j = pl.program_id(1)

        @pl.when(j == 0)
        def _():
            dq_ref[...] = jnp.zeros_like(dq_ref)

        dk_ref[...] = jnp.zeros_like(dk_ref)
        dvt_ref[...] = jnp.zeros_like(dvt_ref)
        kj, vj = k_ref[...], v_ref[...]

        def step(i, masked):
            rows = pl.ds(pl.multiple_of(i * bq, bq), bq)
            qi, doi = q_ref[rows, :], do_ref[rows, :]
            st = lax.dot_general(kj, qi, (((1,), (1,)), ((), ())), preferred_element_type=F32)
            pt = jnp.exp(st - lse_ref[i][0:1, :])
            if masked:
                pt = jnp.where(_chunk_mask(bq, bq), pt, 0.0)
            dpt = lax.dot_general(vj, doi, (((1,), (1,)), ((), ())), preferred_element_type=F32)
            dst = _bf(pt * (dpt - del_ref[i][0:1, :]))
            dvt_ref[...] += lax.dot_general(doi, _bf(pt), (((0,), (1,)), ((), ())), preferred_element_type=F32)
            dk_ref[...] += jnp.dot(dst, qi, preferred_element_type=F32)
            dq_ref[rows, :] += lax.dot_general(dst, kj, (((0,), (0,)), ((), ())), preferred_element_type=F32)

        step(j, True)

        def body(pair, carry):
            step(j + 1 + 2 * pair, False)
            step(j + 2 + 2 * pair, False)
            return carry

        rest = nq - 1 - j
        lax.fori_loop(0, rest // 2, body, 0)

        @pl.when(rest % 2 == 1)
        def _():
            step(nq - 1, False)

        dv_ref[...] = dvt_ref[...].T

    stat = pl.BlockSpec((None, nq, 8, bq), lambda h, j: (h, 0, 0, 0))
    return pl.pallas_call(
        kern, name="flash_bwd", grid=(HEADS, nq),
        in_specs=[pl.BlockSpec((None, t, HEAD_PAD), lambda h, j: (h, 0, 0)),
                  pl.BlockSpec((None, bq, HEAD_PAD), lambda h, j: (h, j, 0)),
                  pl.BlockSpec((None, bq, V_HEAD), lambda h, j: (h, j, 0)),
                  pl.BlockSpec((t, V_HEAD), lambda h, j: (0, h)), stat, stat],
        out_specs=[pl.BlockSpec((None, t, HEAD_PAD), lambda h, j: (h, 0, 0)),
                   pl.BlockSpec((None, bq, HEAD_PAD), lambda h, j: (h, j, 0)),
                   pl.BlockSpec((None, bq, V_HEAD), lambda h, j: (h, j, 0))],
        out_shape=[jax.ShapeDtypeStruct((HEADS, t, HEAD_PAD), F32), jax.ShapeDtypeStruct((HEADS, t, HEAD_PAD), F32),
                   jax.ShapeDtypeStruct((HEADS, t, V_HEAD), F32)],
        scratch_shapes=[pltpu.VMEM((V_HEAD, bq), F32)],
        compiler_params=_params("parallel", "arbitrary"))(q, k, v, do, lse, delta)


def _place():
    x, y, c = lax.axis_index("x"), lax.axis_index("y"), lax.axis_index("c")
    return x, y, c, [(1 - x, y), (x, 1 - y), (1 - x, 1 - y)]


def _all_gather_rows(block, name):
    m_per, n = block.shape

    def body(x_ref, out_ref, send_sems, recv_sems, local_sem):
        x, y, c, chips = _place()
        me, sibling = (x, y, c), (x, y, 1 - c)

        def rows(px, py, pc):
            return out_ref.at[pl.ds((4 * px + 2 * py + pc) * m_per, m_per), :]

        def copy(k, blk, to, src=None):
            return pltpu.make_async_remote_copy(
                src_ref=rows(*blk) if src is None else src, dst_ref=rows(*blk), send_sem=send_sems.at[k],
                recv_sem=recv_sems.at[k], device_id=to, device_id_type=MESH)

        mine = pltpu.make_async_copy(x_ref, rows(*me), local_sem)
        mine.start()
        first = [copy(0, me, sibling, src=x_ref)]
        first += [copy(1 + j, me, (*chip, c), src=x_ref) for j, chip in enumerate(chips)]
        for cp in first:
            cp.start()
        passed = [copy(4 + j, (*chip, c), sibling) for j, chip in enumerate(chips)]
        for j, chip in enumerate(chips):
            copy(1 + j, (*chip, c), me).wait_recv()
            passed[j].start()
        copy(0, sibling, me).wait_recv()
        for j, chip in enumerate(chips):
            copy(4 + j, (*chip, 1 - c), me).wait_recv()
        for cp in first + passed:
            cp.wait_send()
        mine.wait()

    return pl.pallas_call(
        body, name=name, out_shape=jax.ShapeDtypeStruct((8 * m_per, n), block.dtype),
        in_specs=[pl.BlockSpec(memory_space=pltpu.VMEM)], out_specs=pl.BlockSpec(memory_space=pltpu.VMEM),
        scratch_shapes=[pltpu.SemaphoreType.DMA((7,)), pltpu.SemaphoreType.DMA((7,)), pltpu.SemaphoreType.DMA],
        compiler_params=pltpu.CompilerParams(vmem_limit_bytes=VMEM_LIMIT_BYTES))(block)


HBM_SPEC = pl.BlockSpec(memory_space=pltpu.HBM)
SEM_SPEC = pl.BlockSpec(memory_space=pltpu.SEMAPHORE)
DATAFLOW = pltpu.SideEffectType.DATAFLOW_SIDE_EFFECTING


def _in_hbm(a):
    return pltpu.with_memory_space_constraint(a, pltpu.HBM)


def _chip_copies(ins, lands, send_sems, recv_sems, src_slot):
    n = len(ins)
    x, y, c, chips = _place()
    me = 2 * x + y
    return [pltpu.make_async_remote_copy(
        src_ref=ins[w].at[2 * chip[0] + chip[1]] if src_slot else ins[w], dst_ref=lands[w].at[me],
        send_sem=send_sems.at[p * n + w], recv_sem=recv_sems.at[p * n + w], device_id=(*chip, c),
        device_id_type=MESH) for w in range(n) for p, chip in enumerate(chips)]


def _exchange_start(srcs, lands, src_slot, name, dep=None):
    n = len(srcs)
    first_out = 2 * n + (dep is not None)

    def body(*refs):
        for cp in _chip_copies(refs[:n], refs[n:2 * n], refs[first_out], refs[first_out + 1], src_slot):
            cp.start()
        token = refs[-1]
        token[...] = jnp.zeros_like(token)

    thru = [pltpu.HBM(a.shape, a.dtype) for a in list(srcs) + list(lands)]
    res = pl.pallas_call(
        body, name=name,
        out_shape=(pltpu.SemaphoreType.DMA((3 * n,)), pltpu.SemaphoreType.DMA((3 * n,)), *thru,
                   jax.ShapeDtypeStruct((8, 128), F32)),
        in_specs=[HBM_SPEC] * (2 * n) + ([pl.BlockSpec(memory_space=pl.ANY)] if dep is not None else []),
        out_specs=(SEM_SPEC, SEM_SPEC, *[HBM_SPEC] * (2 * n), pl.BlockSpec(memory_space=pltpu.VMEM)),
        input_output_aliases={i: 2 + i for i in range(2 * n)},
        compiler_params=pltpu.CompilerParams(has_side_effects=DATAFLOW))(
            *[_in_hbm(a) for a in srcs], *[_in_hbm(a) for a in lands], *([dep] if dep is not None else []))
    return (res[0], res[1], list(res[2:2 + n]), list(res[2 + n:2 + 2 * n])), res[-1]


def _exchange_wait(flight, after, src_slot, name):
    send_sems, recv_sems, srcs, lands = flight
    n = len(srcs)

    def body(*refs):
        for cp in _chip_copies(refs[:n], refs[n:2 * n], refs[2 * n], refs[2 * n + 1], src_slot):
            cp.wait_send()
            cp.wait_recv()

    thru = [pltpu.HBM(a.shape, a.dtype) for a in list(srcs) + list(lands)]
    res = pl.pallas_call(
        body, name=name, out_shape=thru,
        in_specs=[HBM_SPEC] * (2 * n) + [SEM_SPEC, SEM_SPEC, pl.BlockSpec(memory_space=pl.ANY)],
        out_specs=[HBM_SPEC] * (2 * n), input_output_aliases={i: i for i in range(2 * n)},
        compiler_params=pltpu.CompilerParams(has_side_effects=DATAFLOW))(*srcs, *lands, send_sems, recv_sems, after)
    return list(res[n:])


def _landing(own, me):
    return lax.dynamic_update_index_in_dim(lax.empty((4, *own.shape), own.dtype), own, me, 0)


def _swap_with_sibling(arrays, name):
    n = len(arrays)

    def body(*refs):
        ins, outs = refs[:n], refs[n:2 * n]
        send_sems, recv_sems = refs[2 * n:]
        x, y, c, _ = _place()
        copies = [pltpu.make_async_remote_copy(src_ref=ins[w], dst_ref=outs[w], send_sem=send_sems.at[w],
                                               recv_sem=recv_sems.at[w], device_id=(x, y, 1 - c), device_id_type=MESH)
                  for w in range(n)]
        for cp in copies:
            cp.start()
        for cp in copies:
            cp.wait()

    any_spec = pl.BlockSpec(memory_space=pl.ANY)
    return pl.pallas_call(
        body, name=name, out_shape=[jax.ShapeDtypeStruct(a.shape, a.dtype) for a in arrays],
        in_specs=[any_spec] * n, out_specs=[any_spec] * n,
        scratch_shapes=[pltpu.SemaphoreType.DMA((n,)), pltpu.SemaphoreType.DMA((n,))])(*arrays)


def _as_rows(a):
    return a.reshape(-1, a.shape[-1])


def _row_tile(r, c, budget_bytes=1 << 20):
    tr = r
    while tr % 16 == 0 and tr * c * 4 > budget_bytes:
        tr //= 2
    return tr


def _sum_slots(layers, nlayer, name, into=None):
    _, r, c = layers[0][1].shape
    tr = _row_tile(r, c)
    nt = r // tr
    acc = into
    for l, r4 in layers:
        def kern(r_ref, *rest):
            o_ref = rest[-1]
            o_ref[...] = (((r_ref[0].astype(F32) + r_ref[1].astype(F32)) + r_ref[2].astype(F32))
                          + r_ref[3].astype(F32))

        out_spec = pl.BlockSpec((tr, c), lambda i, l=l: (l * nt + i, 0))
        first = acc is None
        acc = pl.pallas_call(
            kern, name=f"{name}_l{l}", grid=(nt,),
            in_specs=[pl.BlockSpec((4, tr, c), lambda i: (0, i, 0))]
            + ([] if first else [pl.BlockSpec(memory_space=pl.ANY)]),
            out_specs=out_spec, out_shape=jax.ShapeDtypeStruct((nlayer * r, c), F32),
            input_output_aliases={} if first else {1: 0},
            compiler_params=_params("parallel"))(*([r4] if first else [r4, acc]))
    return acc


def _adamw(w, m, v, parts, name):
    r, c = w.shape
    tr = _row_tile(r, c, 3 << 19)
    npart = len(parts)
    c1 = 1.0 - ADAM_B1 ** ADAM_STEP
    c2 = 1.0 - ADAM_B2 ** ADAM_STEP

    def kern(*refs):
        w_ref, m_ref, v_ref = refs[:3]
        p_refs = refs[3:3 + npart]
        g_ref, d_ref, mo_ref, vo_ref = refs[3 + npart:]
        g = p_refs[0][...]
        for p in p_refs[1:]:
            g = g + p[...]
        mn = ADAM_B1 * m_ref[...] + (1.0 - ADAM_B1) * g
        vn = ADAM_B2 * v_ref[...] + (1.0 - ADAM_B2) * (g * g)
        g_ref[...] = g
        mo_ref[...] = mn
        vo_ref[...] = vn
        d_ref[...] = -ADAM_LR * ((mn / c1) / (jnp.sqrt(vn / c2) + ADAM_EPS) + ADAM_WD * w_ref[...])

    blk = pl.BlockSpec((tr, c), lambda i: (i, 0))
    shape = jax.ShapeDtypeStruct((r, c), F32)
    return pl.pallas_call(
        kern, name=name, grid=(r // tr,), in_specs=[blk] * (3 + npart), out_specs=[blk] * 4, out_shape=[shape] * 4,
        compiler_params=_params("parallel"))(w, m, v, *parts)


def _sum_devices(g8, name):
    _, r, c = g8.shape

    def kern(g_ref, o_ref):
        tot = g_ref[0]
        for dev in range(1, 8):
            tot = tot + g_ref[dev]
        o_ref[...] = tot

    return pl.pallas_call(
        kern, name=name, grid=(1,), in_specs=[pl.BlockSpec((8, r, c), lambda i: (0, 0, 0))],
        out_specs=pl.BlockSpec((r, c), lambda i: (0, 0)), out_shape=jax.ShapeDtypeStruct((r, c), F32),
        compiler_params=_params("arbitrary"))(g8)


def _pad_lanes(a, width):
    return jnp.pad(a, [(0, 0)] * (a.ndim - 1) + [(0, width - a.shape[-1])])


def kernel(x, positions, ffn_norm1, ffn1_w1, ffn1_w3, ffn1_w2, mix_norm, ffn_norm2, ffn2_w1, ffn2_w3, ffn2_w2, conv_w_pw1, conv_w_dw, conv_norm, conv_w_pw2, mla_w_a, mla_q_norm, mla_kv_norm, mla_w_uq, mla_w_ukv, mla_w_o, final_norm, loss_target, m_ffn_norm1, m_ffn1_w1, m_ffn1_w3, m_ffn1_w2, m_mix_norm, m_ffn_norm2, m_ffn2_w1, m_ffn2_w3, m_ffn2_w2, m_conv_w_pw1, m_conv_w_dw, m_conv_norm, m_conv_w_pw2, m_mla_w_a, m_mla_q_norm, m_mla_kv_norm, m_mla_w_uq, m_mla_w_ukv, m_mla_w_o, m_final_norm, v_ffn_norm1, v_ffn1_w1, v_ffn1_w3, v_ffn1_w2, v_mix_norm, v_ffn_norm2, v_ffn2_w1, v_ffn2_w3, v_ffn2_w2, v_conv_w_pw1, v_conv_w_dw, v_conv_norm, v_conv_w_pw2, v_mla_w_a, v_mla_q_norm, v_mla_kv_norm, v_mla_w_uq, v_mla_w_ukv, v_mla_w_o, v_final_norm):
    given = locals()
    return _step({nm: given[nm] for nm in INPUTS})


def _step(A):
    x = A['x'][0]
    target = A['loss_target'][0]
    t, d = x.shape
    pos = A['positions'].reshape(t, 1)
    me = 2 * lax.axis_index("x") + lax.axis_index("y")

    ffn = [f'ffn{k}_{w}' for k in (1, 2) for w in ('w1', 'w3', 'w2')]
    gather_groups = [[(nm, 0) for nm in ffn[:3]],
                     [('conv_w_pw1', 0), ('conv_w_pw2', 0)] + [(nm, 0) for nm in ffn[3:]],
                     [(nm, 1) for nm in ffn[:3]] + [('mla_w_a', 0), ('mla_w_uq', 0), ('mla_w_ukv', 0), ('mla_w_o', 0)],
                     [(nm, 1) for nm in ffn[3:]]]
    gather_flights = {}
    big = {}

    def gather_start(gi, dep):
        shards = [_bf(A[nm][l]) for nm, l in gather_groups[gi]]
        gather_flights[gi], token = _exchange_start(shards, [_landing(s, me) for s in shards], False,
                                                    f"gather_start_{gi}", dep)
        return token

    def gather_wait(gi, after):
        landed = _exchange_wait(gather_flights[gi], after, False, f"gather_wait_{gi}")
        big.update(zip(gather_groups[gi], landed))
        return landed[0]

    dw_shard = A['conv_w_dw'][0]
    cw = dw_shard.shape[1]
    small = jnp.concatenate([
        jnp.pad(dw_shard, ((0, CONV_HALO - CONV_WIDTH), (0, 0))),
        jnp.pad(_pad_lanes(A['mla_q_norm'], cw), ((0, 7), (0, 0))),
        jnp.pad(_pad_lanes(A['mla_kv_norm'], cw), ((0, 7), (0, 0)))], axis=0)
    small = _all_gather_rows(small, "gather_small_weights").reshape(4, 2, 48, cw)[:, 0]
    w_dw = jnp.concatenate([small[j, :CONV_HALO] for j in range(4)], axis=1)
    gq = jnp.concatenate([small[j, CONV_HALO, :Q_LORA // 4] for j in range(4)])
    gkv = jnp.concatenate([small[j, CONV_HALO + 8, :KV_LORA // 4] for j in range(4)])

    def cols(nm, layer):
        return jnp.concatenate([big[nm, layer][j] for j in range(4)], axis=1)

    def rows(nm, layer):
        g = big[nm, layer]
        return g.reshape(-1, g.shape[-1])

    ffn_w = {}

    def ffn_weights(k, l):
        ffn_w[k, l] = (cols(f'ffn{k}_w1', l), cols(f'ffn{k}_w3', l), rows(f'ffn{k}_w2', l))
        return ffn_w[k, l]

    token = gather_start(0, None)
    cs_c, cs_s = _rope_tables(pos)
    h0 = x
    token = gather_start(1, gather_wait(0, token))
    h1, n01, z01a, z01b = _ffn_fwd(h0, A['ffn_norm1'][0], *ffn_weights(1, 0), token, "ffn1_l0_fwd")
    token = gather_start(3, gather_start(2, gather_wait(1, h1)))
    pw1 = big['conv_w_pw1', 0]
    pw1_a = jnp.concatenate([pw1[0], pw1[1]], axis=1)
    pw1_b = jnp.concatenate([pw1[2], pw1[3]], axis=1)
    pw2 = rows('conv_w_pw2', 0)
    m0 = _norm_fwd(h1, A['mix_norm'][0], token, "mix_norm_l0")
    ca, cb, glu = _glu_fwd(m0, pw1_a, pw1_b)
    cv, cs = _conv_fwd(glu, w_dw, A['conv_norm'][0])
    h2 = _mm([(cs, pw2)], F32, "conv_pw2_fwd", res=h1)
    h3, n02, z02a, z02b = _ffn_fwd(h2, A['ffn_norm2'][0], *ffn_weights(2, 0), token, "ffn2_l0_fwd")
    gather_wait(2, h3)
    w_a = _pad_lanes(rows('mla_w_a', 0), A_PAD)
    wuq = _pad_lanes(big['mla_w_uq', 0].reshape(Q_LORA, HEADS, NOPE + ROPE), HEAD_PAD).reshape(Q_LORA, -1)
    wukv = big['mla_w_ukv', 0].reshape(KV_LORA, HEADS * (NOPE + V_HEAD))
    w_o = rows('mla_w_o', 0)
    h4, n11, z11a, z11b = _ffn_fwd(h3, A['ffn_norm1'][1], *ffn_weights(1, 1), token, "ffn1_l1_fwd")
    m1 = _norm_fwd(h4, A['mix_norm'][1], token, "mix_norm_l1")
    a_lat = _mm([(m1, w_a)], F32, "mla_down_fwd")
    cq, ckv, kr = _mla_prep(a_lat, gq, gkv, cs_c, cs_s)
    q, k, v = _mla_qkv(cq, ckv, kr, cs_c, cs_s, wuq, wukv)
    o, lse = _flash_fwd(q, k, v)
    h5 = _mm([(o, w_o)], F32, "mla_out_fwd", res=h4)
    gather_wait(3, h5)
    h6, n12, z12a, z12b = _ffn_fwd(h5, A['ffn_norm2'][1], *ffn_weights(2, 1), token, "ffn2_l1_fwd")

    def col_slots(g):
        r, c4 = g.shape
        return g.reshape(r, 4, c4 // 4).transpose(1, 0, 2)

    def row_slots(g):
        return g.reshape(4, g.shape[0] // 4, g.shape[1])

    scatter_flights = []

    def scatter_start(named):
        srcs = [g for _, g in named]
        lands = [_landing(lax.dynamic_index_in_dim(g, me, 0, keepdims=False), me) for g in srcs]
        flight, token = _exchange_start(srcs, lands, True, f"scatter_start_{len(scatter_flights)}")
        scatter_flights.append(([key for key, _ in named], flight))
        return token

    def send_ffn(k, l, dw1, dw3, dw2):
        return scatter_start([((f'ffn{k}_w1', l), col_slots(dw1)), ((f'ffn{k}_w3', l), col_slots(dw3)),
                              ((f'ffn{k}_w2', l), row_slots(dw2))])

    dh6, dg_final, loss_part = _loss_bwd(h6, target, A['final_norm'])
    dh5, dg_n2_l1, *dws = _ffn_bwd(dh6, h5, A['ffn_norm2'][1], n12, z12a, z12b, *ffn_w[2, 1], loss_part, "ffn2_l1")
    token = send_ffn(2, 1, *dws)

    do = _mm([(dh5, w_o)], BF16, "mla_out_bwd", trans_b=True, dep=token)
    dw_o = _mm_tn(o, dh5, BF16, "mla_dw_o")
    delta = _attn_delta(do, o)
    dq, dk, dv = _flash_bwd(q, k, v, do, lse, delta)
    dr, dkv, dcq, dckv, dar = _mla_qkv_bwd(dq, dk, dv, cs_c, cs_s, wuq, wukv)
    dwuq = _mm_tn(cq, dr, BF16, "mla_dw_uq", bn=dr.shape[1] // 2)
    dwukv = _mm_tn(ckv, dkv, BF16, "mla_dw_ukv", bn=dkv.shape[1] // 2)
    da_lat, dgq, dgkv = _mla_prep_bwd(a_lat, dcq, dckv, dar, gq, gkv)
    dw_a = _mm_tn(m1, da_lat, BF16, "mla_dw_a")
    token = scatter_start([
        (('mla_w_a', 0), row_slots(dw_a[:, :Q_LORA + KV_LORA + ROPE])),
        (('mla_w_uq', 0), dwuq.reshape(4, Q_LORA // 4, HEADS, HEAD_PAD)[..., :NOPE + ROPE]),
        (('mla_w_ukv', 0), dwukv.reshape(4, KV_LORA // 4, HEADS, NOPE + V_HEAD)),
        (('mla_w_o', 0), row_slots(dw_o))])
    dh4, dg_mix_l1 = _mm_normbwd([(da_lat, w_a)], h4, A['mix_norm'][1], dh5, token, "mla_down_bwd")

    dh3, dg_n1_l1, *dws = _ffn_bwd(dh4, h3, A['ffn_norm1'][1], n11, z11a, z11b, *ffn_w[1, 1], token, "ffn1_l1")
    token = send_ffn(1, 1, *dws)
    dh2, dg_n2_l0, *dws = _ffn_bwd(dh3, h2, A['ffn_norm2'][0], n02, z02a, z02b, *ffn_w[2, 0], token, "ffn2_l0")
    token = send_ffn(2, 0, *dws)

    dcv, dg_conv = _conv_bwd_norm(dh2, cv, pw2, A['conv_norm'][0], token)
    dw_pw2 = _mm_tn(cs, dh2, BF16, "conv_dw_pw2")
    dca, dcb, ddw = _conv_bwd_dw(dcv, glu, ca, cb, w_dw)
    dpw1_a = _mm_tn(m0, dca, BF16, "conv_dw_pw1a")
    dpw1_b = _mm_tn(m0, dcb, BF16, "conv_dw_pw1b")
    half = dpw1_a.shape[1] // 2
    token = scatter_start([
        (('conv_w_pw1', 0), jnp.stack([dpw1_a[:, :half], dpw1_a[:, half:], dpw1_b[:, :half], dpw1_b[:, half:]])),
        (('conv_w_pw2', 0), row_slots(dw_pw2))])
    dh1, dg_mix_l0 = _mm_normbwd([(dca, pw1_a), (dcb, pw1_b)], h1, A['mix_norm'][0], dh2, token, "conv_pw1_bwd")

    dx, dg_n1_l0, *dws = _ffn_bwd(dh1, h0, A['ffn_norm1'][0], n01, z01a, z01b, *ffn_w[1, 0], token, "ffn1_l0")
    last_sent = send_ffn(1, 0, *dws)
    out = {}

    qkv_row = jnp.concatenate([dgq, dgkv, jnp.zeros((8, d - Q_LORA - KV_LORA), F32)], axis=1)
    loss_row = _pad_lanes(loss_part, d)
    small_g = jnp.concatenate([dg_n1_l0, dg_n1_l1, dg_mix_l0, dg_mix_l1, dg_n2_l0, dg_n2_l1, dg_conv, dg_final,
                               qkv_row, loss_row, ddw], axis=0)
    nrow = small_g.shape[0]
    tot = _sum_devices(_all_gather_rows(small_g, "gather_small_grads").reshape(8, nrow, d), "sum_small_grads")
    loss = tot[72, 0]
    q_shard = lax.dynamic_slice_in_dim(tot[64, :Q_LORA], me * (Q_LORA // 4), Q_LORA // 4)
    kv_shard = lax.dynamic_slice_in_dim(tot[64, Q_LORA:Q_LORA + KV_LORA], me * (KV_LORA // 4), KV_LORA // 4)
    dw_shard_g = lax.dynamic_slice_in_dim(tot[80:80 + CONV_WIDTH], me * cw, cw, axis=1)
    small_grads = {
        'ffn_norm1': jnp.stack([tot[0], tot[8]]), 'mix_norm': jnp.stack([tot[16], tot[24]]),
        'ffn_norm2': jnp.stack([tot[32], tot[40]]), 'conv_norm': tot[48][None], 'final_norm': tot[56],
        'mla_q_norm': q_shard[None], 'mla_kv_norm': kv_shard[None], 'conv_w_dw': dw_shard_g[None],
    }
    for nm, g in small_grads.items():
        res = _adamw(_as_rows(A[nm]) if A[nm].ndim > 1 else A[nm].reshape(1, -1),
                     A['m_' + nm].reshape(-1, A[nm].shape[-1]), A['v_' + nm].reshape(-1, A[nm].shape[-1]),
                     [g.reshape(-1, A[nm].shape[-1])], "adamw_" + nm)
        out[nm] = [r.reshape(A[nm].shape) for r in res]

    received = {}
    after = last_sent

    def scatter_wait(si, after):
        keys, flight = scatter_flights[si]
        landed = _exchange_wait(flight, after, True, f"scatter_wait_{si}")
        received.update(zip(keys, landed))
        return landed[0]

    def slots(nm, l):
        return received[nm, l].reshape(4, -1, received[nm, l].shape[-1])

    def finish(names, sums, tag):
        for nm, mine, theirs in zip(names, sums, _swap_with_sibling(sums, "swap_with_sibling_" + tag)):
            res = _adamw(_as_rows(A[nm]), _as_rows(A['m_' + nm]), _as_rows(A['v_' + nm]), [mine, theirs],
                         "adamw_" + nm)
            out[nm] = [r.reshape(A[nm].shape) for r in res]
        return res[1]

    last = len(scatter_flights) - 1
    for si in range(last):
        after = scatter_wait(si, after)
    late = ffn[:3]
    early = [nm for nm in BIG if nm not in late]
    late_l1 = [_sum_slots([(1, slots(nm, 1))], 2, "sum_" + nm) for nm in late]
    after = finish(early, [_sum_slots([(l, slots(nm, l)) for l in range(A[nm].shape[0])], A[nm].shape[0],
                                      "sum_" + nm) for nm in early], "early")
    scatter_wait(last, after)
    finish(late, [_sum_slots([(0, slots(nm, 0))], 2, "sum_" + nm, into=part) for nm, part in zip(late, late_l1)],
           "late")

    return (loss, dx[None], *[out[nm][0] for nm in WEIGHTS], *[out[nm][1] for nm in WEIGHTS],
            *[out[nm][2] for nm in WEIGHTS], *[out[nm][3] for nm in WEIGHTS])
```

```python
import functools

import jax
import jax.numpy as jnp
import numpy as np
from jax import lax
from jax.experimental import pallas as pl
from jax.experimental.pallas import tpu as pltpu

F32 = jnp.float32
BF16 = jnp.bfloat16
MESH = pl.DeviceIdType.MESH

RMS_EPS = 1e-6
HEADS = 8
NOPE = 128
ROPE = 64
HEAD_PAD = 256
V_HEAD = 128
Q_LORA = 512
KV_LORA = 256
A_PAD = 896
CHUNK = 64
CONV_WIDTH = 31
CONV_HALO = 32
CONV_ROWS = 16
ROPE_THETA = 10000.0
ATTN_SCALE = (NOPE + ROPE) ** -0.5
FFN_RES = 0.5

ADAM_LR = 0.001
ADAM_B1 = 0.9
ADAM_B2 = 0.999
ADAM_EPS = 1e-08
ADAM_WD = 0.01
ADAM_STEP = 10

VMEM_LIMIT_BYTES = 56 * 1024 * 1024

WEIGHTS = ['ffn_norm1', 'ffn1_w1', 'ffn1_w3', 'ffn1_w2', 'mix_norm', 'ffn_norm2', 'ffn2_w1', 'ffn2_w3', 'ffn2_w2',
           'conv_w_pw1', 'conv_w_dw', 'conv_norm', 'conv_w_pw2', 'mla_w_a', 'mla_q_norm', 'mla_kv_norm', 'mla_w_uq',
           'mla_w_ukv', 'mla_w_o', 'final_norm']
INPUTS = (['x', 'positions'] + WEIGHTS + ['loss_target'] + ['m_' + w for w in WEIGHTS] + ['v_' + w for w in WEIGHTS])
BIG = ['ffn1_w1', 'ffn1_w3', 'ffn1_w2', 'ffn2_w1', 'ffn2_w3', 'ffn2_w2', 'conv_w_pw1', 'conv_w_pw2', 'mla_w_a',
       'mla_w_uq', 'mla_w_ukv', 'mla_w_o']


def _params(*sem):
    return pltpu.CompilerParams(dimension_semantics=sem, vmem_limit_bytes=VMEM_LIMIT_BYTES)


def _bf(v):
    return v.astype(BF16)


def _rstd(x):
    return lax.rsqrt(jnp.mean(x * x, axis=-1, keepdims=True) + RMS_EPS)


def _sigmoid(x):
    return jax.nn.sigmoid(x)


def _rot(x):
    lane = lax.broadcasted_iota(jnp.int32, x.shape, 1)
    return jnp.where(lane < ROPE // 2, -pltpu.roll(x, 128 - ROPE // 2, 1), pltpu.roll(x, ROPE // 2, 1))


def _rot_t(y):
    lane = lax.broadcasted_iota(jnp.int32, y.shape, 1)
    return jnp.where(lane < ROPE // 2, pltpu.roll(y, 128 - ROPE // 2, 1), -pltpu.roll(y, ROPE // 2, 1))


def _pair_sum(a_refs, b_refs, trans_b):
    tot = None
    for a_r, b_r in zip(a_refs, b_refs):
        a, b = _bf(a_r[...]), _bf(b_r[...])
        if trans_b:
            d = lax.dot_general(a, b, (((1,), (1,)), ((), ())), preferred_element_type=F32)
        else:
            d = jnp.dot(a, b, preferred_element_type=F32)
        tot = d if tot is None else tot + d
    return tot


def _mm(pairs, out_dtype, name, *, trans_b=False, tm=512, tn=None, tk=None, res=None, dep=None):
    m, k = pairs[0][0].shape
    n = pairs[0][1].shape[0] if trans_b else pairs[0][1].shape[1]
    tm, tn, tk = min(tm, m), tn or n, tk or k
    nk, npair = k // tk, len(pairs)

    def kern(*refs):
        a_refs, b_refs = refs[:npair], refs[npair:2 * npair]
        rest = list(refs[2 * npair:])
        res_ref = rest.pop(0) if res is not None else None
        if dep is not None:
            rest.pop(0)
        o_ref = rest.pop(0)

        def finish(acc):
            if res_ref is not None:
                acc = res_ref[...] + acc
            o_ref[...] = acc.astype(o_ref.dtype)

        if nk == 1:
            finish(_pair_sum(a_refs, b_refs, trans_b))
        else:
            acc_ref = rest.pop(0)
            kk = pl.program_id(2)

            @pl.when(kk == 0)
            def _():
                acc_ref[...] = jnp.zeros_like(acc_ref)

            acc_ref[...] += _pair_sum(a_refs, b_refs, trans_b)

            @pl.when(kk == nk - 1)
            def _():
                finish(acc_ref[...])

    a_spec = pl.BlockSpec((tm, tk), lambda i, j, kk: (i, kk))
    b_spec = (pl.BlockSpec((tn, tk), lambda i, j, kk: (j, kk)) if trans_b
              else pl.BlockSpec((tk, tn), lambda i, j, kk: (kk, j)))
    io_spec = pl.BlockSpec((tm, tn), lambda i, j, kk: (i, j))
    in_specs = ([a_spec] * npair + [b_spec] * npair + ([io_spec] if res is not None else [])
                + ([pl.BlockSpec((8, 128), lambda i, j, kk: (0, 0))] if dep is not None else []))
    args = ([p[0] for p in pairs] + [p[1] for p in pairs] + ([res] if res is not None else [])
            + ([dep] if dep is not None else []))
    return pl.pallas_call(
        kern, name=name, grid=(m // tm, n // tn, nk), in_specs=in_specs, out_specs=io_spec,
        out_shape=jax.ShapeDtypeStruct((m, n), out_dtype),
        scratch_shapes=[pltpu.VMEM((tm, tn), F32)] if nk > 1 else [],
        compiler_params=_params("parallel", "parallel", "arbitrary"))(*args)


def _mm_normbwd(pairs, h, g, dres, dep, name, *, tm=512, tk=None):
    m, k = pairs[0][0].shape
    d = pairs[0][1].shape[0]
    tm, tk = min(tm, m), tk or k
    nk, npair = k // tk, len(pairs)

    def kern(*refs):
        a_refs, b_refs = refs[:npair], refs[npair:2 * npair]
        h_ref, g_ref, dres_ref, _, o_ref, dg_ref, acc_ref = refs[2 * npair:]
        i, kk = pl.program_id(0), pl.program_id(1)

        @pl.when(jnp.logical_and(i == 0, kk == 0))
        def _():
            dg_ref[...] = jnp.zeros_like(dg_ref)

        @pl.when(kk == 0)
        def _():
            acc_ref[...] = jnp.zeros_like(acc_ref)

        acc_ref[...] += _pair_sum(a_refs, b_refs, True)

        @pl.when(kk == nk - 1)
        def _():
            dn = acc_ref[...]
            x = h_ref[...]
            rstd = _rstd(x)
            xhat = x * rstd
            dg_ref[...] += jnp.broadcast_to(jnp.sum(dn * xhat, axis=0, keepdims=True), dg_ref.shape)
            dxh = dn * g_ref[...]
            dx = rstd * (dxh - xhat * jnp.mean(dxh * xhat, axis=-1, keepdims=True))
            o_ref[...] = dres_ref[...] + dx

    row = pl.BlockSpec((tm, d), lambda i, kk: (i, 0))
    in_specs = ([pl.BlockSpec((tm, tk), lambda i, kk: (i, kk))] * npair
                + [pl.BlockSpec((d, tk), lambda i, kk: (0, kk))] * npair
                + [row, pl.BlockSpec((1, d), lambda i, kk: (0, 0)), row, pl.BlockSpec((8, 128), lambda i, kk: (0, 0))])
    return pl.pallas_call(
        kern, name=name, grid=(m // tm, nk), in_specs=in_specs,
        out_specs=[row, pl.BlockSpec((8, d), lambda i, kk: (0, 0))],
        out_shape=[jax.ShapeDtypeStruct((m, d), F32), jax.ShapeDtypeStruct((8, d), F32)],
        scratch_shapes=[pltpu.VMEM((tm, d), F32)],
        compiler_params=_params("arbitrary", "arbitrary"))(
            *[p[0] for p in pairs], *[p[1] for p in pairs], h, g.reshape(1, d), dres, dep)


def _mm_tn(a, b, out_dtype, name, *, bm=None, bn=None, tk=1024):
    t, m = a.shape
    batched = b.ndim == 3
    n = b.shape[-1]
    nb = b.shape[0] if batched else 1
    bm, bn, tk = bm or m, bn or n, min(tk, t)
    nk = t // tk

    def kern(a_ref, b_ref, o_ref, acc_ref):
        kk = pl.program_id(3)

        @pl.when(kk == 0)
        def _():
            acc_ref[...] = jnp.zeros_like(acc_ref)

        acc_ref[...] += lax.dot_general(_bf(a_ref[...]), _bf(b_ref[...]), (((0,), (0,)), ((), ())),
                                        preferred_element_type=F32)

        @pl.when(kk == nk - 1)
        def _():
            o_ref[...] = acc_ref[...].astype(o_ref.dtype)

    a_spec = pl.BlockSpec((tk, bm), lambda h, i, j, kk: (kk, i))
    if batched:
        b_spec = pl.BlockSpec((None, tk, bn), lambda h, i, j, kk: (h, kk, j))
        o_spec = pl.BlockSpec((None, bm, bn), lambda h, i, j, kk: (h, i, j))
        out_shape = jax.ShapeDtypeStruct((nb, m, n), out_dtype)
    else:
        b_spec = pl.BlockSpec((tk, bn), lambda h, i, j, kk: (kk, j))
        o_spec = pl.BlockSpec((bm, bn), lambda h, i, j, kk: (i, j))
        out_shape = jax.ShapeDtypeStruct((m, n), out_dtype)
    return pl.pallas_call(
        kern, name=name, grid=(nb, m // bm, n // bn, nk), in_specs=[a_spec, b_spec], out_specs=o_spec,
        out_shape=out_shape, scratch_shapes=[pltpu.VMEM((bm, bn), F32)],
        compiler_params=_params("parallel", "parallel", "parallel", "arbitrary"))(a, b)


def _ffn_tile(f):
    return f // 2 if (f // 2) % 128 == 0 else f


def _ffn_fwd(h, g, w1t, w3t, w2, dep, name):
    t, d = h.shape
    f = w1t.shape[0]
    tm = min(256, t)
    nt = (((1,), (1,)), ((), ()))

    def kern(h_ref, g_ref, w1_hbm, w3_hbm, w2_hbm, dep_ref, ho_ref, n_ref, z1_ref, z3_ref,
             w1_ref, w3_ref, w2_ref, sems):
        @pl.when(pl.program_id(0) == 0)
        def _():
            copies = [pltpu.make_async_copy(src, dst, sems.at[k]) for k, (src, dst) in
                      enumerate(((w1_hbm, w1_ref), (w3_hbm, w3_ref), (w2_hbm, w2_ref)))]
            for cp in copies:
                cp.start()
            for cp in copies:
                cp.wait()

        x = h_ref[...]
        n = _bf(x * _rstd(x) * g_ref[...])
        n_ref[...] = n
        z1 = lax.dot_general(n, w1_ref[...], nt, preferred_element_type=F32)
        z3 = lax.dot_general(n, w3_ref[...], nt, preferred_element_type=F32)
        z1_ref[...] = _bf(z1)
        z3_ref[...] = _bf(z3)
        act = _bf(z1 * _sigmoid(z1) * z3)
        ho_ref[...] = x + FFN_RES * jnp.dot(act, w2_ref[...], preferred_element_type=F32)

    row = pl.BlockSpec((tm, d), lambda i: (i, 0))
    col = pl.BlockSpec((tm, f), lambda i: (i, 0))
    whole = pl.BlockSpec(memory_space=pl.ANY)
    return pl.pallas_call(
        kern, name=name, grid=(t // tm,),
        in_specs=[row, pl.BlockSpec((1, d), lambda i: (0, 0)), whole, whole, whole,
                  pl.BlockSpec((8, 128), lambda i: (0, 0))],
        out_specs=[row, row, col, col],
        out_shape=[jax.ShapeDtypeStruct((t, d), F32), jax.ShapeDtypeStruct((t, d), BF16),
                   jax.ShapeDtypeStruct((t, f), BF16), jax.ShapeDtypeStruct((t, f), BF16)],
        scratch_shapes=[pltpu.VMEM((f, d), BF16), pltpu.VMEM((f, d), BF16), pltpu.VMEM((f, d), BF16),
                        pltpu.SemaphoreType.DMA((3,))],
        compiler_params=_params("arbitrary"))(h, g.reshape(1, d), w1t, w3t, w2, dep)


def _ffn_bwd_x(dh, h_in, g, z1, z3, w1t, w3t, w2, dep, name):
    t, d = dh.shape
    f = z1.shape[1]
    tm = min(256, t)

    def kern(dh_ref, h_ref, g_ref, z1_ref, z3_ref, w2_hbm, w1_hbm, w3_hbm, dep_ref,
             o_ref, dg_ref, dz1_ref, dz3_ref, a_ref, df_ref, w2_ref, w1_ref, w3_ref, sems):
        @pl.when(pl.program_id(0) == 0)
        def _():
            copies = [pltpu.make_async_copy(src, dst, sems.at[k]) for k, (src, dst) in
                      enumerate(((w2_hbm, w2_ref), (w1_hbm, w1_ref), (w3_hbm, w3_ref)))]
            for cp in copies:
                cp.start()
            dg_ref[...] = jnp.zeros_like(dg_ref)
            for cp in copies:
                cp.wait()

        df = _bf(FFN_RES * dh_ref[...])
        df_ref[...] = df
        da = lax.dot_general(df, w2_ref[...], (((1,), (1,)), ((), ())), preferred_element_type=F32)
        z1v, z3v = z1_ref[...].astype(F32), z3_ref[...].astype(F32)
        sig = _sigmoid(z1v)
        silu = z1v * sig
        a_ref[...] = _bf(silu * z3v)
        dz1 = _bf(da * z3v * (sig * (1.0 + z1v * (1.0 - sig))))
        dz3 = _bf(da * silu)
        dz1_ref[...] = dz1
        dz3_ref[...] = dz3
        dn = (jnp.dot(dz1, w1_ref[...], preferred_element_type=F32)
              + jnp.dot(dz3, w3_ref[...], preferred_element_type=F32))
        x = h_ref[...]
        rstd = _rstd(x)
        xhat = x * rstd
        dg_ref[...] += jnp.broadcast_to(jnp.sum(dn * xhat, axis=0, keepdims=True), dg_ref.shape)
        dxh = dn * g_ref[...]
        o_ref[...] = dh_ref[...] + rstd * (dxh - xhat * jnp.mean(dxh * xhat, axis=-1, keepdims=True))

    row = pl.BlockSpec((tm, d), lambda i: (i, 0))
    col = pl.BlockSpec((tm, f), lambda i: (i, 0))
    whole = pl.BlockSpec(memory_space=pl.ANY)
    colshape = jax.ShapeDtypeStruct((t, f), BF16)
    return pl.pallas_call(
        kern, name=name, grid=(t // tm,),
        in_specs=[row, row, pl.BlockSpec((1, d), lambda i: (0, 0)), col, col, whole, whole, whole,
                  pl.BlockSpec((8, 128), lambda i: (0, 0))],
        out_specs=[row, pl.BlockSpec((8, d), lambda i: (0, 0)), col, col, col, row],
        out_shape=[jax.ShapeDtypeStruct((t, d), F32), jax.ShapeDtypeStruct((8, d), F32), colshape, colshape, colshape,
                   jax.ShapeDtypeStruct((t, d), BF16)],
        scratch_shapes=[pltpu.VMEM((f, d), BF16), pltpu.VMEM((f, d), BF16), pltpu.VMEM((f, d), BF16),
                        pltpu.SemaphoreType.DMA((3,))],
        compiler_params=_params("arbitrary"))(dh, h_in, g.reshape(1, d), z1, z3, w2, w1t, w3t, dep)


def _ffn_bwd(dh, h_in, g, n, z1, z3, w1t, w3t, w2, dep, tag):
    f = w2.shape[0]
    dh_in, dg, dz1, dz3, act, df = _ffn_bwd_x(dh, h_in, g, z1, z3, w1t, w3t, w2, dep, tag + "_bwd_x")
    dw1t = _mm_tn(dz1, n, BF16, tag + "_dw1", bm=_ffn_tile(f))
    dw3t = _mm_tn(dz3, n, BF16, tag + "_dw3", bm=_ffn_tile(f))
    dw2 = _mm_tn(act, df, BF16, tag + "_dw2", bm=_ffn_tile(f))
    return dh_in, dg, dw1t, dw3t, dw2


def _norm_fwd(h, g, dep, name):
    t, d = h.shape
    tm = min(512, t)

    def kern(h_ref, g_ref, dep_ref, o_ref):
        x = h_ref[...]
        o_ref[...] = _bf(x * _rstd(x) * g_ref[...])

    row = pl.BlockSpec((tm, d), lambda i: (i, 0))
    return pl.pallas_call(
        kern, name=name, grid=(t // tm,),
        in_specs=[row, pl.BlockSpec((1, d), lambda i: (0, 0)), pl.BlockSpec((8, 128), lambda i: (0, 0))],
        out_specs=row, out_shape=jax.ShapeDtypeStruct((t, d), BF16),
        compiler_params=_params("parallel"))(h, g.reshape(1, d), dep)


def _loss_bwd(h, target, g):
    t, d = h.shape
    tm = min(512, t)

    def kern(h_ref, t_ref, g_ref, dh_ref, dg_ref, loss_ref):
        @pl.when(pl.program_id(0) == 0)
        def _():
            dg_ref[...] = jnp.zeros_like(dg_ref)
            loss_ref[...] = jnp.zeros_like(loss_ref)

        x = h_ref[...]
        rstd = _rstd(x)
        xhat = x * rstd
        err = xhat * g_ref[...] - t_ref[...]
        row_loss = jnp.sum(err * err, axis=-1, keepdims=True) * (0.5 / d)
        loss_ref[...] += jnp.broadcast_to(jnp.sum(row_loss, axis=0, keepdims=True), loss_ref.shape)
        dy = err * (1.0 / d)
        dg_ref[...] += jnp.broadcast_to(jnp.sum(dy * xhat, axis=0, keepdims=True), dg_ref.shape)
        dxh = dy * g_ref[...]
        dh_ref[...] = rstd * (dxh - xhat * jnp.mean(dxh * xhat, axis=-1, keepdims=True))

    row = pl.BlockSpec((tm, d), lambda i: (i, 0))
    return pl.pallas_call(
        kern, name="loss_bwd", grid=(t // tm,),
        in_specs=[row, row, pl.BlockSpec((1, d), lambda i: (0, 0))],
        out_specs=[row, pl.BlockSpec((8, d), lambda i: (0, 0)), pl.BlockSpec((8, 128), lambda i: (0, 0))],
        out_shape=[jax.ShapeDtypeStruct((t, d), F32), jax.ShapeDtypeStruct((8, d), F32),
                   jax.ShapeDtypeStruct((8, 128), F32)],
        compiler_params=_params("arbitrary"))(h, target, g.reshape(1, d))


def _glu_fwd(m, wa, wb):
    t, d = m.shape
    c = wa.shape[1]
    tm, tc = min(512, t), min(512, c)

    def kern(m_ref, wa_ref, wb_ref, a_ref, b_ref, glu_ref):
        mv = m_ref[...]
        a = jnp.dot(mv, wa_ref[...], preferred_element_type=F32)
        b = jnp.dot(mv, wb_ref[...], preferred_element_type=F32)
        a_ref[...] = _bf(a)
        b_ref[...] = _bf(b)
        glu_ref[...] = _bf(a * _sigmoid(b))

    col = pl.BlockSpec((tm, tc), lambda i, j: (i, j))
    wspec = pl.BlockSpec((d, tc), lambda i, j: (0, j))
    shape = jax.ShapeDtypeStruct((t, c), BF16)
    return pl.pallas_call(
        kern, name="conv_glu_fwd", grid=(t // tm, c // tc),
        in_specs=[pl.BlockSpec((tm, d), lambda i, j: (i, 0)), wspec, wspec], out_specs=[col, col, col],
        out_shape=[shape, shape, shape], compiler_params=_params("parallel", "parallel"))(m, wa, wb)


def _conv_tile(t):
    return min(256, t)


def _shift_copies(ext, shifted, rows):
    for s in range(8):
        shifted[s] = ext[pl.ds(s, rows), :]


def _shifted_rows(shifted, start, nrows):
    return shifted[start % 8, pl.ds(start - start % 8, nrows), :]


def _conv_fwd(glu, w_dw, g):
    t, c = glu.shape
    tm = _conv_tile(t)
    hb = tm // CONV_HALO

    def kern(cur_ref, halo_ref, w_ref, g_ref, cv_ref, s_ref, ext, shifted):
        i = pl.program_id(0)
        ext[0:CONV_HALO, :] = jnp.where(i > 0, halo_ref[...].astype(F32), 0.0)
        ext[CONV_HALO:tm + CONV_HALO, :] = cur_ref[...].astype(F32)
        ext[tm + CONV_HALO:, :] = jnp.zeros((8, c), F32)
        _shift_copies(ext, shifted, tm + CONV_HALO)
        gv = g_ref[...]
        for r0 in range(0, tm, CONV_ROWS):
            acc = jnp.zeros((CONV_ROWS, c), F32)
            for k in range(CONV_WIDTH):
                acc = acc + _shifted_rows(shifted, r0 + 2 + k, CONV_ROWS) * w_ref[k:k + 1, :]
            cv_ref[r0:r0 + CONV_ROWS, :] = acc
            rn = acc * _rstd(acc) * gv
            s_ref[r0:r0 + CONV_ROWS, :] = _bf(rn * _sigmoid(rn))

    row = pl.BlockSpec((tm, c), lambda i: (i, 0))
    return pl.pallas_call(
        kern, name="conv_fwd", grid=(t // tm,),
        in_specs=[row, pl.BlockSpec((CONV_HALO, c), lambda i: (jnp.maximum(i * hb - 1, 0), 0)),
                  pl.BlockSpec((CONV_HALO, c), lambda i: (0, 0)), pl.BlockSpec((1, c), lambda i: (0, 0))],
        out_specs=[row, row],
        out_shape=[jax.ShapeDtypeStruct((t, c), F32), jax.ShapeDtypeStruct((t, c), BF16)],
        scratch_shapes=[pltpu.VMEM((tm + CONV_HALO + 8, c), F32), pltpu.VMEM((8, tm + CONV_HALO, c), F32)],
        compiler_params=_params("parallel"))(glu, glu, w_dw, g.reshape(1, c))


def _conv_bwd_norm(dh, cv, w_pw2, g, dep):
    t, c = cv.shape
    tm = min(512, t)

    def kern(dh_ref, cv_ref, w_ref, g_ref, dep_ref, dcv_ref, dg_ref):
        @pl.when(pl.program_id(0) == 0)
        def _():
            dg_ref[...] = jnp.zeros_like(dg_ref)

        ds = lax.dot_general(_bf(dh_ref[...]), w_ref[...], (((1,), (1,)), ((), ())), preferred_element_type=F32)
        x = cv_ref[...]
        rstd = _rstd(x)
        xhat = x * rstd
        rn = xhat * g_ref[...]
        sig = _sigmoid(rn)
        drn = ds * (sig * (1.0 + rn * (1.0 - sig)))
        dg_ref[...] += jnp.broadcast_to(jnp.sum(drn * xhat, axis=0, keepdims=True), dg_ref.shape)
        dxh = drn * g_ref[...]
        dcv_ref[...] = rstd * (dxh - xhat * jnp.mean(dxh * xhat, axis=-1, keepdims=True))

    row = pl.BlockSpec((tm, c), lambda i: (i, 0))
    return pl.pallas_call(
        kern, name="conv_bwd_norm", grid=(t // tm,),
        in_specs=[pl.BlockSpec((tm, dh.shape[1]), lambda i: (i, 0)), row,
                  pl.BlockSpec(w_pw2.shape, lambda i: (0, 0)), pl.BlockSpec((1, c), lambda i: (0, 0)),
                  pl.BlockSpec((8, 128), lambda i: (0, 0))],
        out_specs=[row, pl.BlockSpec((8, c), lambda i: (0, 0))],
        out_shape=[jax.ShapeDtypeStruct((t, c), F32), jax.ShapeDtypeStruct((8, c), F32)],
        compiler_params=_params("arbitrary"))(dh, cv, w_pw2, g.reshape(1, c), dep)


def _conv_bwd_dw(dcv, glu, a, b, w_dw):
    t, c = dcv.shape
    tm = _conv_tile(t)
    hb = tm // CONV_HALO
    last = t // CONV_HALO - 1

    def kern(dcv_ref, dnext_ref, glu_ref, gprev_ref, a_ref, b_ref, w_ref, da_ref, db_ref, dw_ref,
             dext, gext, dshift, gshift):
        i = pl.program_id(0)

        @pl.when(i == 0)
        def _():
            dw_ref[...] = jnp.zeros_like(dw_ref)

        dext[0:tm, :] = dcv_ref[...]
        dext[tm:tm + CONV_HALO, :] = jnp.where(i < t // tm - 1, dnext_ref[...], 0.0)
        dext[tm + CONV_HALO:, :] = jnp.zeros((8, c), F32)
        gext[0:CONV_HALO, :] = jnp.where(i > 0, gprev_ref[...].astype(F32), 0.0)
        gext[CONV_HALO:tm + CONV_HALO, :] = glu_ref[...].astype(F32)
        gext[tm + CONV_HALO:, :] = jnp.zeros((8, c), F32)
        _shift_copies(dext, dshift, tm + CONV_HALO)
        _shift_copies(gext, gshift, tm + CONV_HALO)
        for r0 in range(0, tm, CONV_ROWS):
            acc = jnp.zeros((CONV_ROWS, c), F32)
            for k in range(CONV_WIDTH):
                acc = acc + _shifted_rows(dshift, r0 + CONV_WIDTH - 1 - k, CONV_ROWS) * w_ref[k:k + 1, :]
            av = a_ref[r0:r0 + CONV_ROWS, :].astype(F32)
            sig = _sigmoid(b_ref[r0:r0 + CONV_ROWS, :].astype(F32))
            da_ref[r0:r0 + CONV_ROWS, :] = _bf(acc * sig)
            db_ref[r0:r0 + CONV_ROWS, :] = _bf(acc * av * sig * (1.0 - sig))
        for k in range(CONV_WIDTH):
            acc = jnp.zeros((CONV_ROWS, c), F32)
            for r0 in range(0, tm, CONV_ROWS):
                acc = acc + _shifted_rows(gshift, r0 + 2 + k, CONV_ROWS) * dext[r0:r0 + CONV_ROWS, :]
            dw_ref[k:k + 1, :] += jnp.sum(acc, axis=0, keepdims=True)

    row = pl.BlockSpec((tm, c), lambda i: (i, 0))
    shape = jax.ShapeDtypeStruct((t, c), BF16)
    return pl.pallas_call(
        kern, name="conv_bwd_dw", grid=(t // tm,),
        in_specs=[row, pl.BlockSpec((CONV_HALO, c), lambda i: (jnp.minimum((i + 1) * hb, last), 0)),
                  row, pl.BlockSpec((CONV_HALO, c), lambda i: (jnp.maximum(i * hb - 1, 0), 0)),
                  row, row, pl.BlockSpec((CONV_HALO, c), lambda i: (0, 0))],
        out_specs=[row, row, pl.BlockSpec((CONV_HALO, c), lambda i: (0, 0))],
        out_shape=[shape, shape, jax.ShapeDtypeStruct((CONV_HALO, c), F32)],
        scratch_shapes=[pltpu.VMEM((tm + CONV_HALO + 8, c), F32), pltpu.VMEM((tm + CONV_HALO + 8, c), F32),
                        pltpu.VMEM((8, tm + CONV_HALO, c), F32), pltpu.VMEM((8, tm + CONV_HALO, c), F32)],
        compiler_params=_params("arbitrary"))(dcv, dcv, glu, glu, a, b, w_dw)


def _rope_tables(pos):
    t = pos.shape[0]
    tm = min(512, t)
    freq = (np.float32(ROPE_THETA) ** (np.float32(-2.0) * np.arange(ROPE // 2, dtype=np.float32)
                                       / np.float32(ROPE))).astype(np.float32)
    row = np.zeros((2, 128), np.float32)
    row[0, :ROPE] = np.concatenate([freq, freq])
    row[1, :ROPE] = 1.0

    def kern(pos_ref, f_ref, c_ref, s_ref):
        ang = pos_ref[...].astype(F32) * f_ref[0:1, :]
        mask = f_ref[1:2, :]
        c_ref[...] = jnp.cos(ang) * mask
        s_ref[...] = jnp.sin(ang) * mask

    out = pl.BlockSpec((tm, 128), lambda i: (i, 0))
    shape = jax.ShapeDtypeStruct((t, 128), F32)
    return pl.pallas_call(
        kern, name="rope_tables", grid=(t // tm,),
        in_specs=[pl.BlockSpec((tm, 1), lambda i: (i, 0)), pl.BlockSpec((2, 128), lambda i: (0, 0))],
        out_specs=[out, out], out_shape=[shape, shape], compiler_params=_params("parallel"))(pos, jnp.asarray(row))


def _mla_prep(a, gq, gkv, cs_c, cs_s):
    t = a.shape[0]
    tm = min(512, t)
    kv0, r0 = Q_LORA, Q_LORA + KV_LORA

    def kern(a_ref, gq_ref, gkv_ref, c_ref, s_ref, cq_ref, ckv_ref, kr_ref):
        aq = a_ref[:, 0:kv0]
        akv = a_ref[:, kv0:r0]
        ar = a_ref[:, r0:A_PAD]
        cq_ref[...] = _bf(aq * _rstd(aq) * gq_ref[...])
        ckv_ref[...] = _bf(akv * _rstd(akv) * gkv_ref[...])
        kr_ref[...] = _bf(ar * c_ref[...] + _rot(ar) * s_ref[...])

    def row(w):
        return pl.BlockSpec((tm, w), lambda i: (i, 0))

    def vec(w):
        return pl.BlockSpec((1, w), lambda i: (0, 0))

    return pl.pallas_call(
        kern, name="mla_prep", grid=(t // tm,),
        in_specs=[row(A_PAD), vec(Q_LORA), vec(KV_LORA), row(128), row(128)],
        out_specs=[row(Q_LORA), row(KV_LORA), row(128)],
        out_shape=[jax.ShapeDtypeStruct((t, Q_LORA), BF16), jax.ShapeDtypeStruct((t, KV_LORA), BF16),
                   jax.ShapeDtypeStruct((t, 128), BF16)],
        compiler_params=_params("parallel"))(a, gq.reshape(1, -1), gkv.reshape(1, -1), cs_c, cs_s)


def _mla_prep_bwd(a, dcq, dckv, dar, gq, gkv):
    t = a.shape[0]
    tm = min(512, t)
    kv0, r0 = Q_LORA, Q_LORA + KV_LORA

    def kern(a_ref, dcq_ref, dckv_ref, dar_ref, gq_ref, gkv_ref, da_ref, dgq_ref, dgkv_ref):
        @pl.when(pl.program_id(0) == 0)
        def _():
            dgq_ref[...] = jnp.zeros_like(dgq_ref)
            dgkv_ref[...] = jnp.zeros_like(dgkv_ref)

        def back(x, dy, g_ref, dg_ref):
            rstd = _rstd(x)
            xhat = x * rstd
            dg_ref[...] += jnp.broadcast_to(jnp.sum(dy * xhat, axis=0, keepdims=True), dg_ref.shape)
            dxh = dy * g_ref[...]
            return rstd * (dxh - xhat * jnp.mean(dxh * xhat, axis=-1, keepdims=True))

        da_ref[:, 0:kv0] = _bf(back(a_ref[:, 0:kv0], dcq_ref[...], gq_ref, dgq_ref))
        da_ref[:, kv0:r0] = _bf(back(a_ref[:, kv0:r0], dckv_ref[...], gkv_ref, dgkv_ref))
        da_ref[:, r0:A_PAD] = _bf(dar_ref[...])

    def row(w):
        return pl.BlockSpec((tm, w), lambda i: (i, 0))

    def vec(r, w):
        return pl.BlockSpec((r, w), lambda i: (0, 0))

    return pl.pallas_call(
        kern, name="mla_prep_bwd", grid=(t // tm,),
        in_specs=[row(A_PAD), row(Q_LORA), row(KV_LORA), row(128), vec(1, Q_LORA), vec(1, KV_LORA)],
        out_specs=[row(A_PAD), vec(8, Q_LORA), vec(8, KV_LORA)],
        out_shape=[jax.ShapeDtypeStruct((t, A_PAD), BF16), jax.ShapeDtypeStruct((8, Q_LORA), F32),
                   jax.ShapeDtypeStruct((8, KV_LORA), F32)],
        compiler_params=_params("arbitrary"))(a, dcq, dckv, dar, gq.reshape(1, -1), gkv.reshape(1, -1))


def _mla_qkv(cq, ckv, kr, cs_c, cs_s, wuq, wukv):
    t = cq.shape[0]
    tm = min(512, t)
    kvw = NOPE + V_HEAD

    def kern(cq_ref, ckv_ref, kr_ref, c_ref, s_ref, wq_ref, wkv_ref, q_ref, k_ref, v_ref):
        r = jnp.dot(cq_ref[...], wq_ref[...], preferred_element_type=F32)
        kv = jnp.dot(ckv_ref[...], wkv_ref[...], preferred_element_type=F32)
        cv, sv, krv = c_ref[...], s_ref[...], kr_ref[...]
        for h in range(HEADS):
            xr = r[:, h * HEAD_PAD + NOPE:(h + 1) * HEAD_PAD]
            q_ref[h, :, 0:NOPE] = _bf(r[:, h * HEAD_PAD:h * HEAD_PAD + NOPE] * ATTN_SCALE)
            q_ref[h, :, NOPE:] = _bf((xr * cv + _rot(xr) * sv) * ATTN_SCALE)
            k_ref[h, :, 0:NOPE] = _bf(kv[:, h * kvw:h * kvw + NOPE])
            k_ref[h, :, NOPE:] = krv
            v_ref[h] = _bf(kv[:, h * kvw + NOPE:(h + 1) * kvw])

    def row(w):
        return pl.BlockSpec((tm, w), lambda i: (i, 0))

    def heads(w):
        return pl.BlockSpec((HEADS, tm, w), lambda i: (0, i, 0))

    return pl.pallas_call(
        kern, name="mla_qkv", grid=(t // tm,),
        in_specs=[row(Q_LORA), row(KV_LORA), row(128), row(128), row(128),
                  pl.BlockSpec(wuq.shape, lambda i: (0, 0)), pl.BlockSpec(wukv.shape, lambda i: (0, 0))],
        out_specs=[heads(HEAD_PAD), heads(HEAD_PAD), heads(V_HEAD)],
        out_shape=[jax.ShapeDtypeStruct((HEADS, t, HEAD_PAD), BF16), jax.ShapeDtypeStruct((HEADS, t, HEAD_PAD), BF16),
                   jax.ShapeDtypeStruct((HEADS, t, V_HEAD), BF16)],
        compiler_params=_params("parallel"))(cq, ckv, kr, cs_c, cs_s, wuq, wukv)


def _mla_qkv_bwd(dq, dk, dv, cs_c, cs_s, wuq, wukv):
    t = dq.shape[1]
    tm = min(256, t)
    kvw = NOPE + V_HEAD

    def kern(dq_ref, dk_ref, dv_ref, c_ref, s_ref, wq_ref, wkv_ref, dr_ref, dkv_ref, dcq_ref, dckv_ref, dar_ref):
        cv, sv = c_ref[...], s_ref[...]
        dar = jnp.zeros_like(cv)
        for h in range(HEADS):
            dqx = dq_ref[h, :, NOPE:]
            dr_ref[:, h * HEAD_PAD:h * HEAD_PAD + NOPE] = _bf(dq_ref[h, :, 0:NOPE] * ATTN_SCALE)
            dr_ref[:, h * HEAD_PAD + NOPE:(h + 1) * HEAD_PAD] = _bf((dqx * cv + _rot_t(dqx * sv)) * ATTN_SCALE)
            dkx = dk_ref[h, :, NOPE:]
            dar = dar + (dkx * cv + _rot_t(dkx * sv))
            dkv_ref[:, h * kvw:h * kvw + NOPE] = _bf(dk_ref[h, :, 0:NOPE])
            dkv_ref[:, h * kvw + NOPE:(h + 1) * kvw] = _bf(dv_ref[h])
        dar_ref[...] = dar
        dcq_ref[...] = lax.dot_general(dr_ref[...], wq_ref[...], (((1,), (1,)), ((), ())),
                                       preferred_element_type=F32)
        dckv_ref[...] = lax.dot_general(dkv_ref[...], wkv_ref[...], (((1,), (1,)), ((), ())),
                                        preferred_element_type=F32)

    def row(w):
        return pl.BlockSpec((tm, w), lambda i: (i, 0))

    def heads(w):
        return pl.BlockSpec((HEADS, tm, w), lambda i: (0, i, 0))

    return pl.pallas_call(
        kern, name="mla_qkv_bwd", grid=(t // tm,),
        in_specs=[heads(HEAD_PAD), heads(HEAD_PAD), heads(V_HEAD), row(128), row(128),
                  pl.BlockSpec(wuq.shape, lambda i: (0, 0)), pl.BlockSpec(wukv.shape, lambda i: (0, 0))],
        out_specs=[row(HEADS * HEAD_PAD), row(HEADS * kvw), row(Q_LORA), row(KV_LORA), row(128)],
        out_shape=[jax.ShapeDtypeStruct((t, HEADS * HEAD_PAD), BF16), jax.ShapeDtypeStruct((t, HEADS * kvw), BF16),
                   jax.ShapeDtypeStruct((t, Q_LORA), F32), jax.ShapeDtypeStruct((t, KV_LORA), F32),
                   jax.ShapeDtypeStruct((t, 128), F32)],
        compiler_params=_params("parallel"))(dq, dk, dv, cs_c, cs_s, wuq, wukv)


def _attn_block(t):
    return 512 if t >= 4096 else 128


def _fold_rows(x, op):
    r = x.shape[0]
    while r > 8:
        r //= 2
        x = op(x[:r], x[r:])
    return x


def _chunk_mask(bk, bq):
    kc = lax.broadcasted_iota(jnp.int32, (bk, bq), 0) // CHUNK
    qc = lax.broadcasted_iota(jnp.int32, (bk, bq), 1) // CHUNK
    return qc >= kc


def _flash_fwd(q, k, v):
    t = q.shape[1]
    bq = _attn_block(t)
    nq = t // bq

    def kern(q_ref, k_ref, v_ref, o_ref, lse_ref, s_buf, p_buf, m_ref, l_ref, acc_ref):
        i = pl.program_id(1)
        queries = (q_ref[0:bq, :], q_ref[bq:2 * bq, :])

        def block(j):
            rows = pl.ds(pl.multiple_of(j * bq, bq), bq)
            return k_ref[rows, :], v_ref[rows, :]

        def scores(kj, chain):
            return lax.dot_general(kj, queries[chain], (((1,), (1,)), ((), ())), preferred_element_type=F32)

        def softmax_block(chain, slot, vj):
            for c0 in range(0, bq, 128):
                cols = slice(c0, c0 + 128)
                s = s_buf[slot, chain, :, cols]
                m_old = m_ref[chain, 0:1, cols]
                m_new = jnp.maximum(m_old, jnp.max(s, axis=0, keepdims=True))
                alpha = jnp.exp(m_old - m_new)
                p = jnp.exp(s - m_new)
                l_ref[chain, 0:1, cols] = alpha * l_ref[chain, 0:1, cols] + jnp.sum(p, axis=0, keepdims=True)
                m_ref[chain, 0:1, cols] = m_new
                p_buf[chain, :, cols] = _bf(p)
                acc_ref[chain, :, cols] = acc_ref[chain, :, cols] * alpha
            acc_ref[chain] += lax.dot_general(vj, p_buf[chain], (((0,), (0,)), ((), ())),
                                              preferred_element_type=F32)

        m_ref[...] = jnp.full(m_ref.shape, -1e30, F32)
        l_ref[...] = jnp.zeros_like(l_ref)
        acc_ref[...] = jnp.zeros_like(acc_ref)
        mask = _chunk_mask(bq, bq)
        k0, v0 = block(2 * i)
        k1, v1 = block(2 * i + 1)
        s_buf[0, 0] = jnp.where(mask, scores(k0, 0), -1e30)
        s_buf[0, 1] = scores(k0, 1)
        s_buf[1, 1] = jnp.where(mask, scores(k1, 1), -1e30)
        softmax_block(0, 0, v0)
        softmax_block(1, 0, v0)
        softmax_block(1, 1, v1)
        kf = block(0)[0]
        s_buf[0, 0] = scores(kf, 0)
        s_buf[0, 1] = scores(kf, 1)

        def body(pair, carry):
            for cur in range(2):
                j = 2 * pair + cur
                kn = block(jnp.minimum(j + 1, jnp.maximum(2 * i - 1, 0)))[0]
                s_buf[1 - cur, 0] = scores(kn, 0)
                s_buf[1 - cur, 1] = scores(kn, 1)
                vj = block(j)[1]
                softmax_block(0, cur, vj)
                softmax_block(1, cur, vj)
            return carry

        lax.fori_loop(0, i, body, 0)
        for chain in range(2):
            l = l_ref[chain, 0:1, :]
            o_ref[chain * bq:(chain + 1) * bq, :] = _bf((acc_ref[chain] / l).T)
            lse_ref[chain] = jnp.broadcast_to(m_ref[chain, 0:1, :] + jnp.log(l), (8, bq))

    return pl.pallas_call(
        kern, name="flash_fwd", grid=(HEADS, nq // 2),
        in_specs=[pl.BlockSpec((None, 2 * bq, HEAD_PAD), lambda h, i: (h, i, 0)),
                  pl.BlockSpec((None, t, HEAD_PAD), lambda h, i: (h, 0, 0)),
                  pl.BlockSpec((None, t, V_HEAD), lambda h, i: (h, 0, 0))],
        out_specs=[pl.BlockSpec((2 * bq, V_HEAD), lambda h, i: (i, h)),
                   pl.BlockSpec((None, 2, 8, bq), lambda h, i: (h, i, 0, 0))],
        out_shape=[jax.ShapeDtypeStruct((t, HEADS * V_HEAD), BF16), jax.ShapeDtypeStruct((HEADS, nq, 8, bq), F32)],
        scratch_shapes=[pltpu.VMEM((2, 2, bq, bq), F32), pltpu.VMEM((2, bq, bq), BF16), pltpu.VMEM((2, 8, bq), F32),
                        pltpu.VMEM((2, 8, bq), F32), pltpu.VMEM((2, V_HEAD, bq), F32)],
        compiler_params=_params("parallel", "arbitrary"))(q, k, v)


def _attn_delta(do, o):
    t = do.shape[0]
    bq = _attn_block(t)

    def kern(do_ref, o_ref, d_ref):
        for h in range(HEADS):
            cols = slice(h * V_HEAD, (h + 1) * V_HEAD)
            prod = do_ref[:, cols].astype(F32) * o_ref[:, cols].astype(F32)
            d_ref[h] = jnp.broadcast_to(jnp.sum(prod.T, axis=0, keepdims=True), (8, bq))

    blk = pl.BlockSpec((bq, HEADS * V_HEAD), lambda i: (i, 0))
    return pl.pallas_call(
        kern, name="attn_delta", grid=(t // bq,), in_specs=[blk, blk],
        out_specs=pl.BlockSpec((HEADS, None, 8, bq), lambda i: (0, i, 0, 0)),
        out_shape=jax.ShapeDtypeStruct((HEADS, t // bq, 8, bq), F32),
        compiler_params=_params("parallel"))(do, o)


def _flash_bwd(q, k, v, do, lse, delta):
    t = q.shape[1]
    bq = _attn_block(t)
    nq = t // bq

    def kern(q_ref, k_ref, v_ref, do_ref, lse_ref, del_ref, dq_ref, dk_ref, dv_ref, dvt_ref):
        j = pl.program_id(1)

        @pl.when(j == 0)
        def _():
            dq_ref[...] = jnp.zeros_like(dq_ref)

        dk_ref[...] = jnp.zeros_like(dk_ref)
        dvt_ref[...] = jnp.zeros_like(dvt_ref)
        kj, vj = k_ref[...], v_ref[...]

        def step(i, masked):
            rows = pl.ds(pl.multiple_of(i * bq, bq), bq)
            qi, doi = q_ref[rows, :], do_ref[rows, :]
            st = lax.dot_general(kj, qi, (((1,), (1,)), ((), ())), preferred_element_type=F32)
            pt = jnp.exp(st - lse_ref[i][0:1, :])
            if masked:
                pt = jnp.where(_chunk_mask(bq, bq), pt, 0.0)
            dpt = lax.dot_general(vj, doi, (((1,), (1,)), ((), ())), preferred_element_type=F32)
            dst = _bf(pt * (dpt - del_ref[i][0:1, :]))
            dvt_ref[...] += lax.dot_general(doi, _bf(pt), (((0,), (1,)), ((), ())), preferred_element_type=F32)
            dk_ref[...] += jnp.dot(dst, qi, preferred_element_type=F32)
            dq_ref[rows, :] += lax.dot_general(dst, kj, (((0,), (0,)), ((), ())), preferred_element_type=F32)

        step(j, True)

        def body(pair, carry):
            step(j + 1 + 2 * pair, False)
            step(j + 2 + 2 * pair, False)
            return carry

        rest = nq - 1 - j
        lax.fori_loop(0, rest // 2, body, 0)

        @pl.when(rest % 2 == 1)
        def _():
            step(nq - 1, False)

        dv_ref[...] = dvt_ref[...].T

    stat = pl.BlockSpec((None, nq, 8, bq), lambda h, j: (h, 0, 0, 0))
    return pl.pallas_call(
        kern, name="flash_bwd", grid=(HEADS, nq),
        in_specs=[pl.BlockSpec((None, t, HEAD_PAD), lambda h, j: (h, 0, 0)),
                  pl.BlockSpec((None, bq, HEAD_PAD), lambda h, j: (h, j, 0)),
                  pl.BlockSpec((None, bq, V_HEAD), lambda h, j: (h, j, 0)),
                  pl.BlockSpec((t, V_HEAD), lambda h, j: (0, h)), stat, stat],
        out_specs=[pl.BlockSpec((None, t, HEAD_PAD), lambda h, j: (h, 0, 0)),
                   pl.BlockSpec((None, bq, HEAD_PAD), lambda h, j: (h, j, 0)),
                   pl.BlockSpec((None, bq, V_HEAD), lambda h, j: (h, j, 0))],
        out_shape=[jax.ShapeDtypeStruct((HEADS, t, HEAD_PAD), F32), jax.ShapeDtypeStruct((HEADS, t, HEAD_PAD), F32),
                   jax.ShapeDtypeStruct((HEADS, t, V_HEAD), F32)],
        scratch_shapes=[pltpu.VMEM((V_HEAD, bq), F32)],
        compiler_params=_params("parallel", "arbitrary"))(q, k, v, do, lse, delta)


def _place():
    x, y, c = lax.axis_index("x"), lax.axis_index("y"), lax.axis_index("c")
    return x, y, c, [(1 - x, y), (x, 1 - y), (1 - x, 1 - y)]


def _all_gather_rows(block, name):
    m_per, n = block.shape

    def body(x_ref, out_ref, send_sems, recv_sems, local_sem):
        x, y, c, chips = _place()
        me, sibling = (x, y, c), (x, y, 1 - c)

        def rows(px, py, pc):
            return out_ref.at[pl.ds((4 * px + 2 * py + pc) * m_per, m_per), :]

        def copy(k, blk, to, src=None):
            return pltpu.make_async_remote_copy(
                src_ref=rows(*blk) if src is None else src, dst_ref=rows(*blk), send_sem=send_sems.at[k],
                recv_sem=recv_sems.at[k], device_id=to, device_id_type=MESH)

        mine = pltpu.make_async_copy(x_ref, rows(*me), local_sem)
        mine.start()
        first = [copy(0, me, sibling, src=x_ref)]
        first += [copy(1 + j, me, (*chip, c), src=x_ref) for j, chip in enumerate(chips)]
        for cp in first:
            cp.start()
        passed = [copy(4 + j, (*chip, c), sibling) for j, chip in enumerate(chips)]
        for j, chip in enumerate(chips):
            copy(1 + j, (*chip, c), me).wait_recv()
            passed[j].start()
        copy(0, sibling, me).wait_recv()
        for j, chip in enumerate(chips):
            copy(4 + j, (*chip, 1 - c), me).wait_recv()
        for cp in first + passed:
            cp.wait_send()
        mine.wait()

    return pl.pallas_call(
        body, name=name, out_shape=jax.ShapeDtypeStruct((8 * m_per, n), block.dtype),
        in_specs=[pl.BlockSpec(memory_space=pltpu.VMEM)], out_specs=pl.BlockSpec(memory_space=pltpu.VMEM),
        scratch_shapes=[pltpu.SemaphoreType.DMA((7,)), pltpu.SemaphoreType.DMA((7,)), pltpu.SemaphoreType.DMA],
        compiler_params=pltpu.CompilerParams(vmem_limit_bytes=VMEM_LIMIT_BYTES))(block)


HBM_SPEC = pl.BlockSpec(memory_space=pltpu.HBM)
SEM_SPEC = pl.BlockSpec(memory_space=pltpu.SEMAPHORE)
DATAFLOW = pltpu.SideEffectType.DATAFLOW_SIDE_EFFECTING


def _in_hbm(a):
    return pltpu.with_memory_space_constraint(a, pltpu.HBM)


def _chip_copies(ins, lands, send_sems, recv_sems, src_slot):
    n = len(ins)
    x, y, c, chips = _place()
    me = 2 * x + y
    return [pltpu.make_async_remote_copy(
        src_ref=ins[w].at[2 * chip[0] + chip[1]] if src_slot else ins[w], dst_ref=lands[w].at[me],
        send_sem=send_sems.at[p * n + w], recv_sem=recv_sems.at[p * n + w], device_id=(*chip, c),
        device_id_type=MESH) for w in range(n) for p, chip in enumerate(chips)]


def _exchange_start(srcs, lands, src_slot, name, dep=None):
    n = len(srcs)
    first_out = 2 * n + (dep is not None)

    def body(*refs):
        for cp in _chip_copies(refs[:n], refs[n:2 * n], refs[first_out], refs[first_out + 1], src_slot):
            cp.start()
        token = refs[-1]
        token[...] = jnp.zeros_like(token)

    thru = [pltpu.HBM(a.shape, a.dtype) for a in list(srcs) + list(lands)]
    res = pl.pallas_call(
        body, name=name,
        out_shape=(pltpu.SemaphoreType.DMA((3 * n,)), pltpu.SemaphoreType.DMA((3 * n,)), *thru,
                   jax.ShapeDtypeStruct((8, 128), F32)),
        in_specs=[HBM_SPEC] * (2 * n) + ([pl.BlockSpec(memory_space=pl.ANY)] if dep is not None else []),
        out_specs=(SEM_SPEC, SEM_SPEC, *[HBM_SPEC] * (2 * n), pl.BlockSpec(memory_space=pltpu.VMEM)),
        input_output_aliases={i: 2 + i for i in range(2 * n)},
        compiler_params=pltpu.CompilerParams(has_side_effects=DATAFLOW))(
            *[_in_hbm(a) for a in srcs], *[_in_hbm(a) for a in lands], *([dep] if dep is not None else []))
    return (res[0], res[1], list(res[2:2 + n]), list(res[2 + n:2 + 2 * n])), res[-1]


def _exchange_wait(flight, after, src_slot, name):
    send_sems, recv_sems, srcs, lands = flight
    n = len(srcs)

    def body(*refs):
        for cp in _chip_copies(refs[:n], refs[n:2 * n], refs[2 * n], refs[2 * n + 1], src_slot):
            cp.wait_send()
            cp.wait_recv()

    thru = [pltpu.HBM(a.shape, a.dtype) for a in list(srcs) + list(lands)]
    res = pl.pallas_call(
        body, name=name, out_shape=thru,
        in_specs=[HBM_SPEC] * (2 * n) + [SEM_SPEC, SEM_SPEC, pl.BlockSpec(memory_space=pl.ANY)],
        out_specs=[HBM_SPEC] * (2 * n), input_output_aliases={i: i for i in range(2 * n)},
        compiler_params=pltpu.CompilerParams(has_side_effects=DATAFLOW))(*srcs, *lands, send_sems, recv_sems, after)
    return list(res[n:])


def _landing(own, me):
    return lax.dynamic_update_index_in_dim(lax.empty((4, *own.shape), own.dtype), own, me, 0)


def _swap_with_sibling(arrays, name):
    n = len(arrays)

    def body(*refs):
        ins, outs = refs[:n], refs[n:2 * n]
        send_sems, recv_sems = refs[2 * n:]
        x, y, c, _ = _place()
        copies = [pltpu.make_async_remote_copy(src_ref=ins[w], dst_ref=outs[w], send_sem=send_sems.at[w],
                                               recv_sem=recv_sems.at[w], device_id=(x, y, 1 - c), device_id_type=MESH)
                  for w in range(n)]
        for cp in copies:
            cp.start()
        for cp in copies:
            cp.wait()

    any_spec = pl.BlockSpec(memory_space=pl.ANY)
    return pl.pallas_call(
        body, name=name, out_shape=[jax.ShapeDtypeStruct(a.shape, a.dtype) for a in arrays],
        in_specs=[any_spec] * n, out_specs=[any_spec] * n,
        scratch_shapes=[pltpu.SemaphoreType.DMA((n,)), pltpu.SemaphoreType.DMA((n,))])(*arrays)


def _as_rows(a):
    return a.reshape(-1, a.shape[-1])


def _row_tile(r, c, budget_bytes=1 << 20):
    tr = r
    while tr % 16 == 0 and tr * c * 4 > budget_bytes:
        tr //= 2
    return tr


def _sum_slots(layers, nlayer, name, into=None):
    _, r, c = layers[0][1].shape
    tr = _row_tile(r, c)
    nt = r // tr
    acc = into
    for l, r4 in layers:
        def kern(r_ref, *rest):
            o_ref = rest[-1]
            o_ref[...] = (((r_ref[0].astype(F32) + r_ref[1].astype(F32)) + r_ref[2].astype(F32))
                          + r_ref[3].astype(F32))

        out_spec = pl.BlockSpec((tr, c), lambda i, l=l: (l * nt + i, 0))
        first = acc is None
        acc = pl.pallas_call(
            kern, name=f"{name}_l{l}", grid=(nt,),
            in_specs=[pl.BlockSpec((4, tr, c), lambda i: (0, i, 0))]
            + ([] if first else [pl.BlockSpec(memory_space=pl.ANY)]),
            out_specs=out_spec, out_shape=jax.ShapeDtypeStruct((nlayer * r, c), F32),
            input_output_aliases={} if first else {1: 0},
            compiler_params=_params("parallel"))(*([r4] if first else [r4, acc]))
    return acc


def _adamw(w, m, v, parts, name):
    r, c = w.shape
    tr = _row_tile(r, c, 3 << 19)
    npart = len(parts)
    c1 = 1.0 - ADAM_B1 ** ADAM_STEP
    c2 = 1.0 - ADAM_B2 ** ADAM_STEP

    def kern(*refs):
        w_ref, m_ref, v_ref = refs[:3]
        p_refs = refs[3:3 + npart]
        g_ref, d_ref, mo_ref, vo_ref = refs[3 + npart:]
        g = p_refs[0][...]
        for p in p_refs[1:]:
            g = g + p[...]
        mn = ADAM_B1 * m_ref[...] + (1.0 - ADAM_B1) * g
        vn = ADAM_B2 * v_ref[...] + (1.0 - ADAM_B2) * (g * g)
        g_ref[...] = g
        mo_ref[...] = mn
        vo_ref[...] = vn
        d_ref[...] = -ADAM_LR * ((mn / c1) / (jnp.sqrt(vn / c2) + ADAM_EPS) + ADAM_WD * w_ref[...])

    blk = pl.BlockSpec((tr, c), lambda i: (i, 0))
    shape = jax.ShapeDtypeStruct((r, c), F32)
    return pl.pallas_call(
        kern, name=name, grid=(r // tr,), in_specs=[blk] * (3 + npart), out_specs=[blk] * 4, out_shape=[shape] * 4,
        compiler_params=_params("parallel"))(w, m, v, *parts)


def _sum_devices(g8, name):
    _, r, c = g8.shape

    def kern(g_ref, o_ref):
        tot = g_ref[0]
        for dev in range(1, 8):
            tot = tot + g_ref[dev]
        o_ref[...] = tot

    return pl.pallas_call(
        kern, name=name, grid=(1,), in_specs=[pl.BlockSpec((8, r, c), lambda i: (0, 0, 0))],
        out_specs=pl.BlockSpec((r, c), lambda i: (0, 0)), out_shape=jax.ShapeDtypeStruct((r, c), F32),
        compiler_params=_params("arbitrary"))(g8)


def _pad_lanes(a, width):
    return jnp.pad(a, [(0, 0)] * (a.ndim - 1) + [(0, width - a.shape[-1])])


def kernel(x, positions, ffn_norm1, ffn1_w1, ffn1_w3, ffn1_w2, mix_norm, ffn_norm2, ffn2_w1, ffn2_w3, ffn2_w2, conv_w_pw1, conv_w_dw, conv_norm, conv_w_pw2, mla_w_a, mla_q_norm, mla_kv_norm, mla_w_uq, mla_w_ukv, mla_w_o, final_norm, loss_target, m_ffn_norm1, m_ffn1_w1, m_ffn1_w3, m_ffn1_w2, m_mix_norm, m_ffn_norm2, m_ffn2_w1, m_ffn2_w3, m_ffn2_w2, m_conv_w_pw1, m_conv_w_dw, m_conv_norm, m_conv_w_pw2, m_mla_w_a, m_mla_q_norm, m_mla_kv_norm, m_mla_w_uq, m_mla_w_ukv, m_mla_w_o, m_final_norm, v_ffn_norm1, v_ffn1_w1, v_ffn1_w3, v_ffn1_w2, v_mix_norm, v_ffn_norm2, v_ffn2_w1, v_ffn2_w3, v_ffn2_w2, v_conv_w_pw1, v_conv_w_dw, v_conv_norm, v_conv_w_pw2, v_mla_w_a, v_mla_q_norm, v_mla_kv_norm, v_mla_w_uq, v_mla_w_ukv, v_mla_w_o, v_final_norm):
    given = locals()
    return _step({nm: given[nm] for nm in INPUTS})


def _step(A):
    x = A['x'][0]
    target = A['loss_target'][0]
    t, d = x.shape
    pos = A['positions'].reshape(t, 1)
    me = 2 * lax.axis_index("x") + lax.axis_index("y")

    flipped = {f'ffn{k}_{w}' for k in (1, 2) for w in ('w1', 'w3')}
    P = {}
    for nm in BIG:
        for key in (nm, 'm_' + nm, 'v_' + nm):
            P[key] = jnp.swapaxes(A[key], 1, 2) if nm in flipped else A[key]

    def unflip(nm, a):
        return jnp.swapaxes(a, 1, 2) if nm in flipped else a

    ffn = [f'ffn{k}_{w}' for k in (1, 2) for w in ('w1', 'w3', 'w2')]
    gather_groups = [[(nm, 0) for nm in ffn[:3]],
                     [('conv_w_pw1', 0), ('conv_w_pw2', 0)] + [(nm, 0) for nm in ffn[3:]],
                     [(nm, 1) for nm in ffn[:3]] + [('mla_w_a', 0), ('mla_w_uq', 0), ('mla_w_ukv', 0), ('mla_w_o', 0)],
                     [(nm, 1) for nm in ffn[3:]]]
    gather_flights = {}
    big = {}

    def gather_start(gi, dep):
        shards = [_bf(P[nm][l]) for nm, l in gather_groups[gi]]
        gather_flights[gi], token = _exchange_start(shards, [_landing(s, me) for s in shards], False,
                                                    f"gather_start_{gi}", dep)
        return token

    def gather_wait(gi, after):
        landed = _exchange_wait(gather_flights[gi], after, False, f"gather_wait_{gi}")
        big.update(zip(gather_groups[gi], landed))
        return landed[0]

    dw_shard = A['conv_w_dw'][0]
    cw = dw_shard.shape[1]
    small = jnp.concatenate([
        jnp.pad(dw_shard, ((0, CONV_HALO - CONV_WIDTH), (0, 0))),
        jnp.pad(_pad_lanes(A['mla_q_norm'], cw), ((0, 7), (0, 0))),
        jnp.pad(_pad_lanes(A['mla_kv_norm'], cw), ((0, 7), (0, 0)))], axis=0)
    small = _all_gather_rows(small, "gather_small_weights").reshape(4, 2, 48, cw)[:, 0]
    w_dw = jnp.concatenate([small[j, :CONV_HALO] for j in range(4)], axis=1)
    gq = jnp.concatenate([small[j, CONV_HALO, :Q_LORA // 4] for j in range(4)])
    gkv = jnp.concatenate([small[j, CONV_HALO + 8, :KV_LORA // 4] for j in range(4)])

    def rows(nm, layer):
        g = big[nm, layer]
        return g.reshape(-1, g.shape[-1])

    ffn_w = {}

    def ffn_weights(k, l):
        ffn_w[k, l] = (rows(f'ffn{k}_w1', l), rows(f'ffn{k}_w3', l), rows(f'ffn{k}_w2', l))
        return ffn_w[k, l]

    token = gather_start(0, None)
    cs_c, cs_s = _rope_tables(pos)
    h0 = x
    token = gather_start(1, gather_wait(0, token))
    h1, n01, z01a, z01b = _ffn_fwd(h0, A['ffn_norm1'][0], *ffn_weights(1, 0), token, "ffn1_l0_fwd")
    token = gather_start(3, gather_start(2, gather_wait(1, h1)))
    pw1 = big['conv_w_pw1', 0]
    pw1_a = jnp.concatenate([pw1[0], pw1[1]], axis=1)
    pw1_b = jnp.concatenate([pw1[2], pw1[3]], axis=1)
    pw2 = rows('conv_w_pw2', 0)
    m0 = _norm_fwd(h1, A['mix_norm'][0], token, "mix_norm_l0")
    ca, cb, glu = _glu_fwd(m0, pw1_a, pw1_b)
    cv, cs = _conv_fwd(glu, w_dw, A['conv_norm'][0])
    h2 = _mm([(cs, pw2)], F32, "conv_pw2_fwd", res=h1)
    h3, n02, z02a, z02b = _ffn_fwd(h2, A['ffn_norm2'][0], *ffn_weights(2, 0), token, "ffn2_l0_fwd")
    gather_wait(2, h3)
    w_a = _pad_lanes(rows('mla_w_a', 0), A_PAD)
    wuq = _pad_lanes(big['mla_w_uq', 0].reshape(Q_LORA, HEADS, NOPE + ROPE), HEAD_PAD).reshape(Q_LORA, -1)
    wukv = big['mla_w_ukv', 0].reshape(KV_LORA, HEADS * (NOPE + V_HEAD))
    w_o = rows('mla_w_o', 0)
    h4, n11, z11a, z11b = _ffn_fwd(h3, A['ffn_norm1'][1], *ffn_weights(1, 1), token, "ffn1_l1_fwd")
    m1 = _norm_fwd(h4, A['mix_norm'][1], token, "mix_norm_l1")
    a_lat = _mm([(m1, w_a)], F32, "mla_down_fwd")
    cq, ckv, kr = _mla_prep(a_lat, gq, gkv, cs_c, cs_s)
    q, k, v = _mla_qkv(cq, ckv, kr, cs_c, cs_s, wuq, wukv)
    o, lse = _flash_fwd(q, k, v)
    h5 = _mm([(o, w_o)], F32, "mla_out_fwd", res=h4)
    gather_wait(3, h5)
    h6, n12, z12a, z12b = _ffn_fwd(h5, A['ffn_norm2'][1], *ffn_weights(2, 1), token, "ffn2_l1_fwd")

    def row_slots(g):
        return g.reshape(4, g.shape[0] // 4, g.shape[1])

    scatter_flights = []

    def scatter_start(named):
        srcs = [g for _, g in named]
        lands = [_landing(lax.dynamic_index_in_dim(g, me, 0, keepdims=False), me) for g in srcs]
        flight, token = _exchange_start(srcs, lands, True, f"scatter_start_{len(scatter_flights)}")
        scatter_flights.append(([key for key, _ in named], flight))
        return token

    def send_ffn(k, l, dw1t, dw3t, dw2):
        return scatter_start([((f'ffn{k}_w1', l), row_slots(dw1t)), ((f'ffn{k}_w3', l), row_slots(dw3t)),
                              ((f'ffn{k}_w2', l), row_slots(dw2))])

    dh6, dg_final, loss_part = _loss_bwd(h6, target, A['final_norm'])
    dh5, dg_n2_l1, *dws = _ffn_bwd(dh6, h5, A['ffn_norm2'][1], n12, z12a, z12b, *ffn_w[2, 1], loss_part, "ffn2_l1")
    token = send_ffn(2, 1, *dws)

    do = _mm([(dh5, w_o)], BF16, "mla_out_bwd", trans_b=True, dep=token)
    dw_o = _mm_tn(o, dh5, BF16, "mla_dw_o")
    delta = _attn_delta(do, o)
    dq, dk, dv = _flash_bwd(q, k, v, do, lse, delta)
    dr, dkv, dcq, dckv, dar = _mla_qkv_bwd(dq, dk, dv, cs_c, cs_s, wuq, wukv)
    dwuq = _mm_tn(cq, dr, BF16, "mla_dw_uq", bn=dr.shape[1] // 2)
    dwukv = _mm_tn(ckv, dkv, BF16, "mla_dw_ukv", bn=dkv.shape[1] // 2)
    da_lat, dgq, dgkv = _mla_prep_bwd(a_lat, dcq, dckv, dar, gq, gkv)
    dw_a = _mm_tn(m1, da_lat, BF16, "mla_dw_a")
    token = scatter_start([
        (('mla_w_a', 0), row_slots(dw_a[:, :Q_LORA + KV_LORA + ROPE])),
        (('mla_w_uq', 0), dwuq.reshape(4, Q_LORA // 4, HEADS, HEAD_PAD)[..., :NOPE + ROPE]),
        (('mla_w_ukv', 0), dwukv.reshape(4, KV_LORA // 4, HEADS, NOPE + V_HEAD)),
        (('mla_w_o', 0), row_slots(dw_o))])
    dh4, dg_mix_l1 = _mm_normbwd([(da_lat, w_a)], h4, A['mix_norm'][1], dh5, token, "mla_down_bwd")

    dh3, dg_n1_l1, *dws = _ffn_bwd(dh4, h3, A['ffn_norm1'][1], n11, z11a, z11b, *ffn_w[1, 1], token, "ffn1_l1")
    token = send_ffn(1, 1, *dws)
    dh2, dg_n2_l0, *dws = _ffn_bwd(dh3, h2, A['ffn_norm2'][0], n02, z02a, z02b, *ffn_w[2, 0], token, "ffn2_l0")
    token = send_ffn(2, 0, *dws)

    dcv, dg_conv = _conv_bwd_norm(dh2, cv, pw2, A['conv_norm'][0], token)
    dw_pw2 = _mm_tn(cs, dh2, BF16, "conv_dw_pw2")
    dca, dcb, ddw = _conv_bwd_dw(dcv, glu, ca, cb, w_dw)
    dpw1_a = _mm_tn(m0, dca, BF16, "conv_dw_pw1a")
    dpw1_b = _mm_tn(m0, dcb, BF16, "conv_dw_pw1b")
    half = dpw1_a.shape[1] // 2
    token = scatter_start([
        (('conv_w_pw1', 0), jnp.stack([dpw1_a[:, :half], dpw1_a[:, half:], dpw1_b[:, :half], dpw1_b[:, half:]])),
        (('conv_w_pw2', 0), row_slots(dw_pw2))])
    dh1, dg_mix_l0 = _mm_normbwd([(dca, pw1_a), (dcb, pw1_b)], h1, A['mix_norm'][0], dh2, token, "conv_pw1_bwd")

    dx, dg_n1_l0, *dws = _ffn_bwd(dh1, h0, A['ffn_norm1'][0], n01, z01a, z01b, *ffn_w[1, 0], token, "ffn1_l0")
    last_sent = send_ffn(1, 0, *dws)
    out = {}

    qkv_row = jnp.concatenate([dgq, dgkv, jnp.zeros((8, d - Q_LORA - KV_LORA), F32)], axis=1)
    loss_row = _pad_lanes(loss_part, d)
    small_g = jnp.concatenate([dg_n1_l0, dg_n1_l1, dg_mix_l0, dg_mix_l1, dg_n2_l0, dg_n2_l1, dg_conv, dg_final,
                               qkv_row, loss_row, ddw], axis=0)
    nrow = small_g.shape[0]
    tot = _sum_devices(_all_gather_rows(small_g, "gather_small_grads").reshape(8, nrow, d), "sum_small_grads")
    loss = tot[72, 0]
    q_shard = lax.dynamic_slice_in_dim(tot[64, :Q_LORA], me * (Q_LORA // 4), Q_LORA // 4)
    kv_shard = lax.dynamic_slice_in_dim(tot[64, Q_LORA:Q_LORA + KV_LORA], me * (KV_LORA // 4), KV_LORA // 4)
    dw_shard_g = lax.dynamic_slice_in_dim(tot[80:80 + CONV_WIDTH], me * cw, cw, axis=1)
    small_grads = {
        'ffn_norm1': jnp.stack([tot[0], tot[8]]), 'mix_norm': jnp.stack([tot[16], tot[24]]),
        'ffn_norm2': jnp.stack([tot[32], tot[40]]), 'conv_norm': tot[48][None], 'final_norm': tot[56],
        'mla_q_norm': q_shard[None], 'mla_kv_norm': kv_shard[None], 'conv_w_dw': dw_shard_g[None],
    }
    for nm, g in small_grads.items():
        res = _adamw(_as_rows(A[nm]) if A[nm].ndim > 1 else A[nm].reshape(1, -1),
                     A['m_' + nm].reshape(-1, A[nm].shape[-1]), A['v_' + nm].reshape(-1, A[nm].shape[-1]),
                     [g.reshape(-1, A[nm].shape[-1])], "adamw_" + nm)
        out[nm] = [r.reshape(A[nm].shape) for r in res]

    received = {}
    after = last_sent

    def scatter_wait(si, after):
        keys, flight = scatter_flights[si]
        landed = _exchange_wait(flight, after, True, f"scatter_wait_{si}")
        received.update(zip(keys, landed))
        return landed[0]

    def slots(nm, l):
        return received[nm, l].reshape(4, -1, received[nm, l].shape[-1])

    def finish(names, sums, tag):
        for nm, mine, theirs in zip(names, sums, _swap_with_sibling(sums, "swap_with_sibling_" + tag)):
            res = _adamw(_as_rows(P[nm]), _as_rows(P['m_' + nm]), _as_rows(P['v_' + nm]), [mine, theirs],
                         "adamw_" + nm)
            out[nm] = [unflip(nm, r.reshape(P[nm].shape)) for r in res]
        return res[1]

    last = len(scatter_flights) - 1
    for si in range(last):
        after = scatter_wait(si, after)
    late = ffn[:3]
    early = [nm for nm in BIG if nm not in late]
    late_l1 = [_sum_slots([(1, slots(nm, 1))], 2, "sum_" + nm) for nm in late]
    after = finish(early, [_sum_slots([(l, slots(nm, l)) for l in range(A[nm].shape[0])], A[nm].shape[0],
                                      "sum_" + nm) for nm in early], "early")
    scatter_wait(last, after)
    finish(late, [_sum_slots([(0, slots(nm, 0))], 2, "sum_" + nm, into=part) for nm, part in zip(late, late_l1)],
           "late")

    return (loss, dx[None], *[out[nm][0] for nm in WEIGHTS], *[out[nm][1] for nm in WEIGHTS],
            *[out[nm][2] for nm in WEIGHTS], *[out[nm][3] for nm in WEIGHTS])
```

```python
import functools

import jax
import jax.numpy as jnp
import numpy as np
from jax import lax
from jax.experimental import pallas as pl
from jax.experimental.pallas import tpu as pltpu

F32 = jnp.float32
BF16 = jnp.bfloat16
MESH = pl.DeviceIdType.MESH

RMS_EPS = 1e-6
HEADS = 8
NOPE = 128
ROPE = 64
HEAD_PAD = 256
V_HEAD = 128
Q_LORA = 512
KV_LORA = 256
A_PAD = 896
CHUNK = 64
CONV_WIDTH = 31
CONV_HALO = 32
CONV_ROWS = 16
ROPE_THETA = 10000.0
ATTN_SCALE = (NOPE + ROPE) ** -0.5
FFN_RES = 0.5

ADAM_LR = 0.001
ADAM_B1 = 0.9
ADAM_B2 = 0.999
ADAM_EPS = 1e-08
ADAM_WD = 0.01
ADAM_STEP = 10

VMEM_LIMIT_BYTES = 56 * 1024 * 1024

WEIGHTS = ['ffn_norm1', 'ffn1_w1', 'ffn1_w3', 'ffn1_w2', 'mix_norm', 'ffn_norm2', 'ffn2_w1', 'ffn2_w3', 'ffn2_w2',
           'conv_w_pw1', 'conv_w_dw', 'conv_norm', 'conv_w_pw2', 'mla_w_a', 'mla_q_norm', 'mla_kv_norm', 'mla_w_uq',
           'mla_w_ukv', 'mla_w_o', 'final_norm']
INPUTS = (['x', 'positions'] + WEIGHTS + ['loss_target'] + ['m_' + w for w in WEIGHTS] + ['v_' + w for w in WEIGHTS])
BIG = ['ffn1_w1', 'ffn1_w3', 'ffn1_w2', 'ffn2_w1', 'ffn2_w3', 'ffn2_w2', 'conv_w_pw1', 'conv_w_pw2', 'mla_w_a',
       'mla_w_uq', 'mla_w_ukv', 'mla_w_o']


def _params(*sem):
    return pltpu.CompilerParams(dimension_semantics=sem, vmem_limit_bytes=VMEM_LIMIT_BYTES)


def _bf(v):
    return v.astype(BF16)


def _rstd(x):
    return lax.rsqrt(jnp.mean(x * x, axis=-1, keepdims=True) + RMS_EPS)


def _sigmoid(x):
    return jax.nn.sigmoid(x)


def _rot(x):
    lane = lax.broadcasted_iota(jnp.int32, x.shape, 1)
    return jnp.where(lane < ROPE // 2, -pltpu.roll(x, 128 - ROPE // 2, 1), pltpu.roll(x, ROPE // 2, 1))


def _rot_t(y):
    lane = lax.broadcasted_iota(jnp.int32, y.shape, 1)
    return jnp.where(lane < ROPE // 2, pltpu.roll(y, 128 - ROPE // 2, 1), -pltpu.roll(y, ROPE // 2, 1))


def _pair_sum(a_refs, b_refs, trans_b):
    tot = None
    for a_r, b_r in zip(a_refs, b_refs):
        a, b = _bf(a_r[...]), _bf(b_r[...])
        if trans_b:
            d = lax.dot_general(a, b, (((1,), (1,)), ((), ())), preferred_element_type=F32)
        else:
            d = jnp.dot(a, b, preferred_element_type=F32)
        tot = d if tot is None else tot + d
    return tot


def _mm(pairs, out_dtype, name, *, trans_b=False, tm=512, tn=None, tk=None, res=None, dep=None):
    m, k = pairs[0][0].shape
    n = pairs[0][1].shape[0] if trans_b else pairs[0][1].shape[1]
    tm, tn, tk = min(tm, m), tn or n, tk or k
    nk, npair = k // tk, len(pairs)

    def kern(*refs):
        a_refs, b_refs = refs[:npair], refs[npair:2 * npair]
        rest = list(refs[2 * npair:])
        res_ref = rest.pop(0) if res is not None else None
        if dep is not None:
            rest.pop(0)
        o_ref = rest.pop(0)

        def finish(acc):
            if res_ref is not None:
                acc = res_ref[...] + acc
            o_ref[...] = acc.astype(o_ref.dtype)

        if nk == 1:
            finish(_pair_sum(a_refs, b_refs, trans_b))
        else:
            acc_ref = rest.pop(0)
            kk = pl.program_id(2)

            @pl.when(kk == 0)
            def _():
                acc_ref[...] = jnp.zeros_like(acc_ref)

            acc_ref[...] += _pair_sum(a_refs, b_refs, trans_b)

            @pl.when(kk == nk - 1)
            def _():
                finish(acc_ref[...])

    a_spec = pl.BlockSpec((tm, tk), lambda i, j, kk: (i, kk))
    b_spec = (pl.BlockSpec((tn, tk), lambda i, j, kk: (j, kk)) if trans_b
              else pl.BlockSpec((tk, tn), lambda i, j, kk: (kk, j)))
    io_spec = pl.BlockSpec((tm, tn), lambda i, j, kk: (i, j))
    in_specs = ([a_spec] * npair + [b_spec] * npair + ([io_spec] if res is not None else [])
                + ([pl.BlockSpec((8, 128), lambda i, j, kk: (0, 0))] if dep is not None else []))
    args = ([p[0] for p in pairs] + [p[1] for p in pairs] + ([res] if res is not None else [])
            + ([dep] if dep is not None else []))
    return pl.pallas_call(
        kern, name=name, grid=(m // tm, n // tn, nk), in_specs=in_specs, out_specs=io_spec,
        out_shape=jax.ShapeDtypeStruct((m, n), out_dtype),
        scratch_shapes=[pltpu.VMEM((tm, tn), F32)] if nk > 1 else [],
        compiler_params=_params("parallel", "parallel", "arbitrary"))(*args)


def _mm_normbwd(pairs, h, g, dres, dep, name, *, tm=512, tk=None):
    m, k = pairs[0][0].shape
    d = pairs[0][1].shape[0]
    tm, tk = min(tm, m), tk or k
    nk, npair = k // tk, len(pairs)

    def kern(*refs):
        a_refs, b_refs = refs[:npair], refs[npair:2 * npair]
        h_ref, g_ref, dres_ref, _, o_ref, dg_ref, acc_ref = refs[2 * npair:]
        i, kk = pl.program_id(0), pl.program_id(1)

        @pl.when(jnp.logical_and(i == 0, kk == 0))
        def _():
            dg_ref[...] = jnp.zeros_like(dg_ref)

        @pl.when(kk == 0)
        def _():
            acc_ref[...] = jnp.zeros_like(acc_ref)

        acc_ref[...] += _pair_sum(a_refs, b_refs, True)

        @pl.when(kk == nk - 1)
        def _():
            dn = acc_ref[...]
            x = h_ref[...]
            rstd = _rstd(x)
            xhat = x * rstd
            dg_ref[...] += jnp.broadcast_to(jnp.sum(dn * xhat, axis=0, keepdims=True), dg_ref.shape)
            dxh = dn * g_ref[...]
            dx = rstd * (dxh - xhat * jnp.mean(dxh * xhat, axis=-1, keepdims=True))
            o_ref[...] = dres_ref[...] + dx

    row = pl.BlockSpec((tm, d), lambda i, kk: (i, 0))
    in_specs = ([pl.BlockSpec((tm, tk), lambda i, kk: (i, kk))] * npair
                + [pl.BlockSpec((d, tk), lambda i, kk: (0, kk))] * npair
                + [row, pl.BlockSpec((1, d), lambda i, kk: (0, 0)), row, pl.BlockSpec((8, 128), lambda i, kk: (0, 0))])
    return pl.pallas_call(
        kern, name=name, grid=(m // tm, nk), in_specs=in_specs,
        out_specs=[row, pl.BlockSpec((8, d), lambda i, kk: (0, 0))],
        out_shape=[jax.ShapeDtypeStruct((m, d), F32), jax.ShapeDtypeStruct((8, d), F32)],
        scratch_shapes=[pltpu.VMEM((tm, d), F32)],
        compiler_params=_params("arbitrary", "arbitrary"))(
            *[p[0] for p in pairs], *[p[1] for p in pairs], h, g.reshape(1, d), dres, dep)


def _mm_tn(a, b, out_dtype, name, *, bm=None, bn=None, tk=1024):
    t, m = a.shape
    batched = b.ndim == 3
    n = b.shape[-1]
    nb = b.shape[0] if batched else 1
    bm, bn, tk = bm or m, bn or n, min(tk, t)
    nk = t // tk

    def kern(a_ref, b_ref, o_ref, acc_ref):
        kk = pl.program_id(3)

        @pl.when(kk == 0)
        def _():
            acc_ref[...] = jnp.zeros_like(acc_ref)

        acc_ref[...] += lax.dot_general(_bf(a_ref[...]), _bf(b_ref[...]), (((0,), (0,)), ((), ())),
                                        preferred_element_type=F32)

        @pl.when(kk == nk - 1)
        def _():
            o_ref[...] = acc_ref[...].astype(o_ref.dtype)

    a_spec = pl.BlockSpec((tk, bm), lambda h, i, j, kk: (kk, i))
    if batched:
        b_spec = pl.BlockSpec((None, tk, bn), lambda h, i, j, kk: (h, kk, j))
        o_spec = pl.BlockSpec((None, bm, bn), lambda h, i, j, kk: (h, i, j))
        out_shape = jax.ShapeDtypeStruct((nb, m, n), out_dtype)
    else:
        b_spec = pl.BlockSpec((tk, bn), lambda h, i, j, kk: (kk, j))
        o_spec = pl.BlockSpec((bm, bn), lambda h, i, j, kk: (i, j))
        out_shape = jax.ShapeDtypeStruct((m, n), out_dtype)
    return pl.pallas_call(
        kern, name=name, grid=(nb, m // bm, n // bn, nk), in_specs=[a_spec, b_spec], out_specs=o_spec,
        out_shape=out_shape, scratch_shapes=[pltpu.VMEM((bm, bn), F32)],
        compiler_params=_params("parallel", "parallel", "parallel", "arbitrary"))(a, b)


def _ffn_tile(f):
    return f // 2 if (f // 2) % 128 == 0 else f


def _ffn_fwd(h, g, w1t, w3t, w2, dep, name):
    t, d = h.shape
    f = w1t.shape[0]
    tm = min(256, t)
    nt = (((1,), (1,)), ((), ()))

    def kern(h_ref, g_ref, w1_hbm, w3_hbm, w2_hbm, dep_ref, ho_ref, n_ref, z1_ref, z3_ref,
             w1_ref, w3_ref, w2_ref, sems):
        @pl.when(pl.program_id(0) == 0)
        def _():
            copies = [pltpu.make_async_copy(src, dst, sems.at[k]) for k, (src, dst) in
                      enumerate(((w1_hbm, w1_ref), (w3_hbm, w3_ref), (w2_hbm, w2_ref)))]
            for cp in copies:
                cp.start()
            for cp in copies:
                cp.wait()

        x = h_ref[...]
        n = _bf(x * _rstd(x) * g_ref[...])
        n_ref[...] = n
        z1 = lax.dot_general(n, w1_ref[...], nt, preferred_element_type=F32)
        z3 = lax.dot_general(n, w3_ref[...], nt, preferred_element_type=F32)
        z1_ref[...] = _bf(z1)
        z3_ref[...] = _bf(z3)
        act = _bf(z1 * _sigmoid(z1) * z3)
        ho_ref[...] = x + FFN_RES * jnp.dot(act, w2_ref[...], preferred_element_type=F32)

    row = pl.BlockSpec((tm, d), lambda i: (i, 0))
    col = pl.BlockSpec((tm, f), lambda i: (i, 0))
    whole = pl.BlockSpec(memory_space=pl.ANY)
    return pl.pallas_call(
        kern, name=name, grid=(t // tm,),
        in_specs=[row, pl.BlockSpec((1, d), lambda i: (0, 0)), whole, whole, whole,
                  pl.BlockSpec((8, 128), lambda i: (0, 0))],
        out_specs=[row, row, col, col],
        out_shape=[jax.ShapeDtypeStruct((t, d), F32), jax.ShapeDtypeStruct((t, d), BF16),
                   jax.ShapeDtypeStruct((t, f), BF16), jax.ShapeDtypeStruct((t, f), BF16)],
        scratch_shapes=[pltpu.VMEM((f, d), BF16), pltpu.VMEM((f, d), BF16), pltpu.VMEM((f, d), BF16),
                        pltpu.SemaphoreType.DMA((3,))],
        compiler_params=_params("arbitrary"))(h, g.reshape(1, d), w1t, w3t, w2, dep)


def _ffn_bwd_x(dh, h_in, g, z1, z3, w1t, w3t, w2, dep, name):
    t, d = dh.shape
    f = z1.shape[1]
    tm = min(256, t)

    def kern(dh_ref, h_ref, g_ref, z1_ref, z3_ref, w2_hbm, w1_hbm, w3_hbm, dep_ref,
             o_ref, dg_ref, dz1_ref, dz3_ref, a_ref, df_ref, w2_ref, w1_ref, w3_ref, sems):
        @pl.when(pl.program_id(0) == 0)
        def _():
            copies = [pltpu.make_async_copy(src, dst, sems.at[k]) for k, (src, dst) in
                      enumerate(((w2_hbm, w2_ref), (w1_hbm, w1_ref), (w3_hbm, w3_ref)))]
            for cp in copies:
                cp.start()
            dg_ref[...] = jnp.zeros_like(dg_ref)
            for cp in copies:
                cp.wait()

        df = _bf(FFN_RES * dh_ref[...])
        df_ref[...] = df
        da = lax.dot_general(df, w2_ref[...], (((1,), (1,)), ((), ())), preferred_element_type=F32)
        z1v, z3v = z1_ref[...].astype(F32), z3_ref[...].astype(F32)
        sig = _sigmoid(z1v)
        silu = z1v * sig
        a_ref[...] = _bf(silu * z3v)
        dz1 = _bf(da * z3v * (sig * (1.0 + z1v * (1.0 - sig))))
        dz3 = _bf(da * silu)
        dz1_ref[...] = dz1
        dz3_ref[...] = dz3
        dn = (jnp.dot(dz1, w1_ref[...], preferred_element_type=F32)
              + jnp.dot(dz3, w3_ref[...], preferred_element_type=F32))
        x = h_ref[...]
        rstd = _rstd(x)
        xhat = x * rstd
        dg_ref[...] += jnp.broadcast_to(jnp.sum(dn * xhat, axis=0, keepdims=True), dg_ref.shape)
        dxh = dn * g_ref[...]
        o_ref[...] = dh_ref[...] + rstd * (dxh - xhat * jnp.mean(dxh * xhat, axis=-1, keepdims=True))

    row = pl.BlockSpec((tm, d), lambda i: (i, 0))
    col = pl.BlockSpec((tm, f), lambda i: (i, 0))
    whole = pl.BlockSpec(memory_space=pl.ANY)
    colshape = jax.ShapeDtypeStruct((t, f), BF16)
    return pl.pallas_call(
        kern, name=name, grid=(t // tm,),
        in_specs=[row, row, pl.BlockSpec((1, d), lambda i: (0, 0)), col, col, whole, whole, whole,
                  pl.BlockSpec((8, 128), lambda i: (0, 0))],
        out_specs=[row, pl.BlockSpec((8, d), lambda i: (0, 0)), col, col, col, row],
        out_shape=[jax.ShapeDtypeStruct((t, d), F32), jax.ShapeDtypeStruct((8, d), F32), colshape, colshape, colshape,
                   jax.ShapeDtypeStruct((t, d), BF16)],
        scratch_shapes=[pltpu.VMEM((f, d), BF16), pltpu.VMEM((f, d), BF16), pltpu.VMEM((f, d), BF16),
                        pltpu.SemaphoreType.DMA((3,))],
        compiler_params=_params("arbitrary"))(dh, h_in, g.reshape(1, d), z1, z3, w2, w1t, w3t, dep)


def _ffn_bwd(dh, h_in, g, n, z1, z3, w1t, w3t, w2, dep, tag):
    f = w2.shape[0]
    dh_in, dg, dz1, dz3, act, df = _ffn_bwd_x(dh, h_in, g, z1, z3, w1t, w3t, w2, dep, tag + "_bwd_x")
    dw1t = _mm_tn(dz1, n, BF16, tag + "_dw1", bm=_ffn_tile(f))
    dw3t = _mm_tn(dz3, n, BF16, tag + "_dw3", bm=_ffn_tile(f))
    dw2 = _mm_tn(act, df, BF16, tag + "_dw2", bm=_ffn_tile(f))
    return dh_in, dg, dw1t, dw3t, dw2


def _norm_fwd(h, g, dep, name):
    t, d = h.shape
    tm = min(512, t)

    def kern(h_ref, g_ref, dep_ref, o_ref):
        x = h_ref[...]
        o_ref[...] = _bf(x * _rstd(x) * g_ref[...])

    row = pl.BlockSpec((tm, d), lambda i: (i, 0))
    return pl.pallas_call(
        kern, name=name, grid=(t // tm,),
        in_specs=[row, pl.BlockSpec((1, d), lambda i: (0, 0)), pl.BlockSpec((8, 128), lambda i: (0, 0))],
        out_specs=row, out_shape=jax.ShapeDtypeStruct((t, d), BF16),
        compiler_params=_params("parallel"))(h, g.reshape(1, d), dep)


def _loss_bwd(h, target, g):
    t, d = h.shape
    tm = min(512, t)

    def kern(h_ref, t_ref, g_ref, dh_ref, dg_ref, loss_ref):
        @pl.when(pl.program_id(0) == 0)
        def _():
            dg_ref[...] = jnp.zeros_like(dg_ref)
            loss_ref[...] = jnp.zeros_like(loss_ref)

        x = h_ref[...]
        rstd = _rstd(x)
        xhat = x * rstd
        err = xhat * g_ref[...] - t_ref[...]
        row_loss = jnp.sum(err * err, axis=-1, keepdims=True) * (0.5 / d)
        loss_ref[...] += jnp.broadcast_to(jnp.sum(row_loss, axis=0, keepdims=True), loss_ref.shape)
        dy = err * (1.0 / d)
        dg_ref[...] += jnp.broadcast_to(jnp.sum(dy * xhat, axis=0, keepdims=True), dg_ref.shape)
        dxh = dy * g_ref[...]
        dh_ref[...] = rstd * (dxh - xhat * jnp.mean(dxh * xhat, axis=-1, keepdims=True))

    row = pl.BlockSpec((tm, d), lambda i: (i, 0))
    return pl.pallas_call(
        kern, name="loss_bwd", grid=(t // tm,),
        in_specs=[row, row, pl.BlockSpec((1, d), lambda i: (0, 0))],
        out_specs=[row, pl.BlockSpec((8, d), lambda i: (0, 0)), pl.BlockSpec((8, 128), lambda i: (0, 0))],
        out_shape=[jax.ShapeDtypeStruct((t, d), F32), jax.ShapeDtypeStruct((8, d), F32),
                   jax.ShapeDtypeStruct((8, 128), F32)],
        compiler_params=_params("arbitrary"))(h, target, g.reshape(1, d))


def _glu_fwd(m, wa, wb):
    t, d = m.shape
    c = wa.shape[1]
    tm, tc = min(512, t), min(512, c)

    def kern(m_ref, wa_ref, wb_ref, a_ref, b_ref, glu_ref):
        mv = m_ref[...]
        a = jnp.dot(mv, wa_ref[...], preferred_element_type=F32)
        b = jnp.dot(mv, wb_ref[...], preferred_element_type=F32)
        a_ref[...] = _bf(a)
        b_ref[...] = _bf(b)
        glu_ref[...] = _bf(a * _sigmoid(b))

    col = pl.BlockSpec((tm, tc), lambda i, j: (i, j))
    wspec = pl.BlockSpec((d, tc), lambda i, j: (0, j))
    shape = jax.ShapeDtypeStruct((t, c), BF16)
    return pl.pallas_call(
        kern, name="conv_glu_fwd", grid=(t // tm, c // tc),
        in_specs=[pl.BlockSpec((tm, d), lambda i, j: (i, 0)), wspec, wspec], out_specs=[col, col, col],
        out_shape=[shape, shape, shape], compiler_params=_params("parallel", "parallel"))(m, wa, wb)


def _conv_tile(t):
    return min(256, t)


def _shift_copies(ext, shifted, rows):
    for s in range(8):
        shifted[s] = ext[pl.ds(s, rows), :]


def _shifted_rows(shifted, start, nrows):
    return shifted[start % 8, pl.ds(start - start % 8, nrows), :]


def _conv_fwd(glu, w_dw, g):
    t, c = glu.shape
    tm = _conv_tile(t)
    hb = tm // CONV_HALO

    def kern(cur_ref, halo_ref, w_ref, g_ref, cv_ref, s_ref, ext, shifted):
        i = pl.program_id(0)
        ext[0:CONV_HALO, :] = jnp.where(i > 0, halo_ref[...].astype(F32), 0.0)
        ext[CONV_HALO:tm + CONV_HALO, :] = cur_ref[...].astype(F32)
        ext[tm + CONV_HALO:, :] = jnp.zeros((8, c), F32)
        _shift_copies(ext, shifted, tm + CONV_HALO)
        gv = g_ref[...]
        for r0 in range(0, tm, CONV_ROWS):
            acc = jnp.zeros((CONV_ROWS, c), F32)
            for k in range(CONV_WIDTH):
                acc = acc + _shifted_rows(shifted, r0 + 2 + k, CONV_ROWS) * w_ref[k:k + 1, :]
            cv_ref[r0:r0 + CONV_ROWS, :] = acc
            rn = acc * _rstd(acc) * gv
            s_ref[r0:r0 + CONV_ROWS, :] = _bf(rn * _sigmoid(rn))

    row = pl.BlockSpec((tm, c), lambda i: (i, 0))
    return pl.pallas_call(
        kern, name="conv_fwd", grid=(t // tm,),
        in_specs=[row, pl.BlockSpec((CONV_HALO, c), lambda i: (jnp.maximum(i * hb - 1, 0), 0)),
                  pl.BlockSpec((CONV_HALO, c), lambda i: (0, 0)), pl.BlockSpec((1, c), lambda i: (0, 0))],
        out_specs=[row, row],
        out_shape=[jax.ShapeDtypeStruct((t, c), F32), jax.ShapeDtypeStruct((t, c), BF16)],
        scratch_shapes=[pltpu.VMEM((tm + CONV_HALO + 8, c), F32), pltpu.VMEM((8, tm + CONV_HALO, c), F32)],
        compiler_params=_params("parallel"))(glu, glu, w_dw, g.reshape(1, c))


def _conv_bwd_norm(dh, cv, w_pw2, g, dep):
    t, c = cv.shape
    tm = min(512, t)

    def kern(dh_ref, cv_ref, w_ref, g_ref, dep_ref, dcv_ref, dg_ref):
        @pl.when(pl.program_id(0) == 0)
        def _():
            dg_ref[...] = jnp.zeros_like(dg_ref)

        ds = lax.dot_general(_bf(dh_ref[...]), w_ref[...], (((1,), (1,)), ((), ())), preferred_element_type=F32)
        x = cv_ref[...]
        rstd = _rstd(x)
        xhat = x * rstd
        rn = xhat * g_ref[...]
        sig = _sigmoid(rn)
        drn = ds * (sig * (1.0 + rn * (1.0 - sig)))
        dg_ref[...] += jnp.broadcast_to(jnp.sum(drn * xhat, axis=0, keepdims=True), dg_ref.shape)
        dxh = drn * g_ref[...]
        dcv_ref[...] = rstd * (dxh - xhat * jnp.mean(dxh * xhat, axis=-1, keepdims=True))

    row = pl.BlockSpec((tm, c), lambda i: (i, 0))
    return pl.pallas_call(
        kern, name="conv_bwd_norm", grid=(t // tm,),
        in_specs=[pl.BlockSpec((tm, dh.shape[1]), lambda i: (i, 0)), row,
                  pl.BlockSpec(w_pw2.shape, lambda i: (0, 0)), pl.BlockSpec((1, c), lambda i: (0, 0)),
                  pl.BlockSpec((8, 128), lambda i: (0, 0))],
        out_specs=[row, pl.BlockSpec((8, c), lambda i: (0, 0))],
        out_shape=[jax.ShapeDtypeStruct((t, c), F32), jax.ShapeDtypeStruct((8, c), F32)],
        compiler_params=_params("arbitrary"))(dh, cv, w_pw2, g.reshape(1, c), dep)


def _conv_bwd_dw(dcv, glu, a, b, w_dw):
    t, c = dcv.shape
    tm = _conv_tile(t)
    hb = tm // CONV_HALO
    last = t // CONV_HALO - 1

    def kern(dcv_ref, dnext_ref, glu_ref, gprev_ref, a_ref, b_ref, w_ref, da_ref, db_ref, dw_ref,
             dext, gext, dshift, gshift):
        i = pl.program_id(0)

        @pl.when(i == 0)
        def _():
            dw_ref[...] = jnp.zeros_like(dw_ref)

        dext[0:tm, :] = dcv_ref[...]
        dext[tm:tm + CONV_HALO, :] = jnp.where(i < t // tm - 1, dnext_ref[...], 0.0)
        dext[tm + CONV_HALO:, :] = jnp.zeros((8, c), F32)
        gext[0:CONV_HALO, :] = jnp.where(i > 0, gprev_ref[...].astype(F32), 0.0)
        gext[CONV_HALO:tm + CONV_HALO, :] = glu_ref[...].astype(F32)
        gext[tm + CONV_HALO:, :] = jnp.zeros((8, c), F32)
        _shift_copies(dext, dshift, tm + CONV_HALO)
        _shift_copies(gext, gshift, tm + CONV_HALO)
        for r0 in range(0, tm, CONV_ROWS):
            acc = jnp.zeros((CONV_ROWS, c), F32)
            for k in range(CONV_WIDTH):
                acc = acc + _shifted_rows(dshift, r0 + CONV_WIDTH - 1 - k, CONV_ROWS) * w_ref[k:k + 1, :]
            av = a_ref[r0:r0 + CONV_ROWS, :].astype(F32)
            sig = _sigmoid(b_ref[r0:r0 + CONV_ROWS, :].astype(F32))
            da_ref[r0:r0 + CONV_ROWS, :] = _bf(acc * sig)
            db_ref[r0:r0 + CONV_ROWS, :] = _bf(acc * av * sig * (1.0 - sig))
        for k in range(CONV_WIDTH):
            acc = jnp.zeros((CONV_ROWS, c), F32)
            for r0 in range(0, tm, CONV_ROWS):
                acc = acc + _shifted_rows(gshift, r0 + 2 + k, CONV_ROWS) * dext[r0:r0 + CONV_ROWS, :]
            dw_ref[k:k + 1, :] += jnp.sum(acc, axis=0, keepdims=True)

    row = pl.BlockSpec((tm, c), lambda i: (i, 0))
    shape = jax.ShapeDtypeStruct((t, c), BF16)
    return pl.pallas_call(
        kern, name="conv_bwd_dw", grid=(t // tm,),
        in_specs=[row, pl.BlockSpec((CONV_HALO, c), lambda i: (jnp.minimum((i + 1) * hb, last), 0)),
                  row, pl.BlockSpec((CONV_HALO, c), lambda i: (jnp.maximum(i * hb - 1, 0), 0)),
                  row, row, pl.BlockSpec((CONV_HALO, c), lambda i: (0, 0))],
        out_specs=[row, row, pl.BlockSpec((CONV_HALO, c), lambda i: (0, 0))],
        out_shape=[shape, shape, jax.ShapeDtypeStruct((CONV_HALO, c), F32)],
        scratch_shapes=[pltpu.VMEM((tm + CONV_HALO + 8, c), F32), pltpu.VMEM((tm + CONV_HALO + 8, c), F32),
                        pltpu.VMEM((8, tm + CONV_HALO, c), F32), pltpu.VMEM((8, tm + CONV_HALO, c), F32)],
        compiler_params=_params("arbitrary"))(dcv, dcv, glu, glu, a, b, w_dw)


def _rope_tables(pos):
    t = pos.shape[0]
    tm = min(512, t)
    freq = (np.float32(ROPE_THETA) ** (np.float32(-2.0) * np.arange(ROPE // 2, dtype=np.float32)
                                       / np.float32(ROPE))).astype(np.float32)
    row = np.zeros((2, 128), np.float32)
    row[0, :ROPE] = np.concatenate([freq, freq])
    row[1, :ROPE] = 1.0

    def kern(pos_ref, f_ref, c_ref, s_ref):
        ang = pos_ref[...].astype(F32) * f_ref[0:1, :]
        mask = f_ref[1:2, :]
        c_ref[...] = jnp.cos(ang) * mask
        s_ref[...] = jnp.sin(ang) * mask

    out = pl.BlockSpec((tm, 128), lambda i: (i, 0))
    shape = jax.ShapeDtypeStruct((t, 128), F32)
    return pl.pallas_call(
        kern, name="rope_tables", grid=(t // tm,),
        in_specs=[pl.BlockSpec((tm, 1), lambda i: (i, 0)), pl.BlockSpec((2, 128), lambda i: (0, 0))],
        out_specs=[out, out], out_shape=[shape, shape], compiler_params=_params("parallel"))(pos, jnp.asarray(row))


def _mla_prep(a, gq, gkv, cs_c, cs_s):
    t = a.shape[0]
    tm = min(512, t)
    kv0, r0 = Q_LORA, Q_LORA + KV_LORA

    def kern(a_ref, gq_ref, gkv_ref, c_ref, s_ref, cq_ref, ckv_ref, kr_ref):
        aq = a_ref[:, 0:kv0]
        akv = a_ref[:, kv0:r0]
        ar = a_ref[:, r0:A_PAD]
        cq_ref[...] = _bf(aq * _rstd(aq) * gq_ref[...])
        ckv_ref[...] = _bf(akv * _rstd(akv) * gkv_ref[...])
        kr_ref[...] = _bf(ar * c_ref[...] + _rot(ar) * s_ref[...])

    def row(w):
        return pl.BlockSpec((tm, w), lambda i: (i, 0))

    def vec(w):
        return pl.BlockSpec((1, w), lambda i: (0, 0))

    return pl.pallas_call(
        kern, name="mla_prep", grid=(t // tm,),
        in_specs=[row(A_PAD), vec(Q_LORA), vec(KV_LORA), row(128), row(128)],
        out_specs=[row(Q_LORA), row(KV_LORA), row(128)],
        out_shape=[jax.ShapeDtypeStruct((t, Q_LORA), BF16), jax.ShapeDtypeStruct((t, KV_LORA), BF16),
                   jax.ShapeDtypeStruct((t, 128), BF16)],
        compiler_params=_params("parallel"))(a, gq.reshape(1, -1), gkv.reshape(1, -1), cs_c, cs_s)


def _mla_prep_bwd(a, dcq, dckv, dar, gq, gkv):
    t = a.shape[0]
    tm = min(512, t)
    kv0, r0 = Q_LORA, Q_LORA + KV_LORA

    def kern(a_ref, dcq_ref, dckv_ref, dar_ref, gq_ref, gkv_ref, da_ref, dgq_ref, dgkv_ref):
        @pl.when(pl.program_id(0) == 0)
        def _():
            dgq_ref[...] = jnp.zeros_like(dgq_ref)
            dgkv_ref[...] = jnp.zeros_like(dgkv_ref)

        def back(x, dy, g_ref, dg_ref):
            rstd = _rstd(x)
            xhat = x * rstd
            dg_ref[...] += jnp.broadcast_to(jnp.sum(dy * xhat, axis=0, keepdims=True), dg_ref.shape)
            dxh = dy * g_ref[...]
            return rstd * (dxh - xhat * jnp.mean(dxh * xhat, axis=-1, keepdims=True))

        da_ref[:, 0:kv0] = _bf(back(a_ref[:, 0:kv0], dcq_ref[...], gq_ref, dgq_ref))
        da_ref[:, kv0:r0] = _bf(back(a_ref[:, kv0:r0], dckv_ref[...], gkv_ref, dgkv_ref))
        da_ref[:, r0:A_PAD] = _bf(dar_ref[...])

    def row(w):
        return pl.BlockSpec((tm, w), lambda i: (i, 0))

    def vec(r, w):
        return pl.BlockSpec((r, w), lambda i: (0, 0))

    return pl.pallas_call(
        kern, name="mla_prep_bwd", grid=(t // tm,),
        in_specs=[row(A_PAD), row(Q_LORA), row(KV_LORA), row(128), vec(1, Q_LORA), vec(1, KV_LORA)],
        out_specs=[row(A_PAD), vec(8, Q_LORA), vec(8, KV_LORA)],
        out_shape=[jax.ShapeDtypeStruct((t, A_PAD), BF16), jax.ShapeDtypeStruct((8, Q_LORA), F32),
                   jax.ShapeDtypeStruct((8, KV_LORA), F32)],
        compiler_params=_params("arbitrary"))(a, dcq, dckv, dar, gq.reshape(1, -1), gkv.reshape(1, -1))


def _mla_qkv(cq, ckv, kr, cs_c, cs_s, wuq, wukv):
    t = cq.shape[0]
    tm = min(512, t)
    kvw = NOPE + V_HEAD

    def kern(cq_ref, ckv_ref, kr_ref, c_ref, s_ref, wq_ref, wkv_ref, q_ref, k_ref, v_ref):
        r = jnp.dot(cq_ref[...], wq_ref[...], preferred_element_type=F32)
        kv = jnp.dot(ckv_ref[...], wkv_ref[...], preferred_element_type=F32)
        cv, sv, krv = c_ref[...], s_ref[...], kr_ref[...]
        for h in range(HEADS):
            xr = r[:, h * HEAD_PAD + NOPE:(h + 1) * HEAD_PAD]
            q_ref[h, :, 0:NOPE] = _bf(r[:, h * HEAD_PAD:h * HEAD_PAD + NOPE] * ATTN_SCALE)
            q_ref[h, :, NOPE:] = _bf((xr * cv + _rot(xr) * sv) * ATTN_SCALE)
            k_ref[h, :, 0:NOPE] = _bf(kv[:, h * kvw:h * kvw + NOPE])
            k_ref[h, :, NOPE:] = krv
            v_ref[h] = _bf(kv[:, h * kvw + NOPE:(h + 1) * kvw])

    def row(w):
        return pl.BlockSpec((tm, w), lambda i: (i, 0))

    def heads(w):
        return pl.BlockSpec((HEADS, tm, w), lambda i: (0, i, 0))

    return pl.pallas_call(
        kern, name="mla_qkv", grid=(t // tm,),
        in_specs=[row(Q_LORA), row(KV_LORA), row(128), row(128), row(128),
                  pl.BlockSpec(wuq.shape, lambda i: (0, 0)), pl.BlockSpec(wukv.shape, lambda i: (0, 0))],
        out_specs=[heads(HEAD_PAD), heads(HEAD_PAD), heads(V_HEAD)],
        out_shape=[jax.ShapeDtypeStruct((HEADS, t, HEAD_PAD), BF16), jax.ShapeDtypeStruct((HEADS, t, HEAD_PAD), BF16),
                   jax.ShapeDtypeStruct((HEADS, t, V_HEAD), BF16)],
        compiler_params=_params("parallel"))(cq, ckv, kr, cs_c, cs_s, wuq, wukv)


def _mla_qkv_bwd(dq, dk, dv, cs_c, cs_s, wuq, wukv):
    t = dq.shape[1]
    tm = min(256, t)
    kvw = NOPE + V_HEAD

    def kern(dq_ref, dk_ref, dv_ref, c_ref, s_ref, wq_ref, wkv_ref, dr_ref, dkv_ref, dcq_ref, dckv_ref, dar_ref):
        cv, sv = c_ref[...], s_ref[...]
        dar = jnp.zeros_like(cv)
        for h in range(HEADS):
            dqx = dq_ref[h, :, NOPE:]
            dr_ref[:, h * HEAD_PAD:h * HEAD_PAD + NOPE] = _bf(dq_ref[h, :, 0:NOPE] * ATTN_SCALE)
            dr_ref[:, h * HEAD_PAD + NOPE:(h + 1) * HEAD_PAD] = _bf((dqx * cv + _rot_t(dqx * sv)) * ATTN_SCALE)
            dkx = dk_ref[h, :, NOPE:]
            dar = dar + (dkx * cv + _rot_t(dkx * sv))
            dkv_ref[:, h * kvw:h * kvw + NOPE] = _bf(dk_ref[h, :, 0:NOPE])
            dkv_ref[:, h * kvw + NOPE:(h + 1) * kvw] = _bf(dv_ref[h])
        dar_ref[...] = dar
        dcq_ref[...] = lax.dot_general(dr_ref[...], wq_ref[...], (((1,), (1,)), ((), ())),
                                       preferred_element_type=F32)
        dckv_ref[...] = lax.dot_general(dkv_ref[...], wkv_ref[...], (((1,), (1,)), ((), ())),
                                        preferred_element_type=F32)

    def row(w):
        return pl.BlockSpec((tm, w), lambda i: (i, 0))

    def heads(w):
        return pl.BlockSpec((HEADS, tm, w), lambda i: (0, i, 0))

    return pl.pallas_call(
        kern, name="mla_qkv_bwd", grid=(t // tm,),
        in_specs=[heads(HEAD_PAD), heads(HEAD_PAD), heads(V_HEAD), row(128), row(128),
                  pl.BlockSpec(wuq.shape, lambda i: (0, 0)), pl.BlockSpec(wukv.shape, lambda i: (0, 0))],
        out_specs=[row(HEADS * HEAD_PAD), row(HEADS * kvw), row(Q_LORA), row(KV_LORA), row(128)],
        out_shape=[jax.ShapeDtypeStruct((t, HEADS * HEAD_PAD), BF16), jax.ShapeDtypeStruct((t, HEADS * kvw), BF16),
                   jax.ShapeDtypeStruct((t, Q_LORA), F32), jax.ShapeDtypeStruct((t, KV_LORA), F32),
                   jax.ShapeDtypeStruct((t, 128), F32)],
        compiler_params=_params("parallel"))(dq, dk, dv, cs_c, cs_s, wuq, wukv)


def _attn_block(t):
    return 512 if t >= 4096 else 128


def _fold_rows(x, op):
    r = x.shape[0]
    while r > 8:
        r //= 2
        x = op(x[:r], x[r:])
    return x


def _chunk_mask(bk, bq):
    kc = lax.broadcasted_iota(jnp.int32, (bk, bq), 0) // CHUNK
    qc = lax.broadcasted_iota(jnp.int32, (bk, bq), 1) // CHUNK
    return qc >= kc


def _flash_fwd(q, k, v):
    t = q.shape[1]
    bq = _attn_block(t)
    nq = t // bq

    def kern(q_ref, k_ref, v_ref, o_ref, lse_ref, s_buf, p_buf, m_ref, l_ref, acc_ref):
        i = pl.program_id(1)
        queries = (q_ref[0:bq, :], q_ref[bq:2 * bq, :])

        def block(j):
            rows = pl.ds(pl.multiple_of(j * bq, bq), bq)
            return k_ref[rows, :], v_ref[rows, :]

        def scores(kj, chain):
            return lax.dot_general(kj, queries[chain], (((1,), (1,)), ((), ())), preferred_element_type=F32)

        def softmax_block(chain, slot, vj):
            for c0 in range(0, bq, 128):
                cols = slice(c0, c0 + 128)
                s = s_buf[slot, chain, :, cols]
                m_old = m_ref[chain, 0:1, cols]
                m_new = jnp.maximum(m_old, jnp.max(s, axis=0, keepdims=True))
                alpha = jnp.exp(m_old - m_new)
                p = jnp.exp(s - m_new)
                l_ref[chain, 0:1, cols] = alpha * l_ref[chain, 0:1, cols] + jnp.sum(p, axis=0, keepdims=True)
                m_ref[chain, 0:1, cols] = m_new
                p_buf[chain, :, cols] = _bf(p)
                acc_ref[chain, :, cols] = acc_ref[chain, :, cols] * alpha
            acc_ref[chain] += lax.dot_general(vj, p_buf[chain], (((0,), (0,)), ((), ())),
                                              preferred_element_type=F32)

        m_ref[...] = jnp.full(m_ref.shape, -1e30, F32)
        l_ref[...] = jnp.zeros_like(l_ref)
        acc_ref[...] = jnp.zeros_like(acc_ref)
        mask = _chunk_mask(bq, bq)
        k0, v0 = block(2 * i)
        k1, v1 = block(2 * i + 1)
        s_buf[0, 0] = jnp.where(mask, scores(k0, 0), -1e30)
        s_buf[0, 1] = scores(k0, 1)
        s_buf[1, 1] = jnp.where(mask, scores(k1, 1), -1e30)
        softmax_block(0, 0, v0)
        softmax_block(1, 0, v0)
        softmax_block(1, 1, v1)
        kf = block(0)[0]
        s_buf[0, 0] = scores(kf, 0)
        s_buf[0, 1] = scores(kf, 1)

        def body(pair, carry):
            for cur in range(2):
                j = 2 * pair + cur
                kn = block(jnp.minimum(j + 1, jnp.maximum(2 * i - 1, 0)))[0]
                s_buf[1 - cur, 0] = scores(kn, 0)
                s_buf[1 - cur, 1] = scores(kn, 1)
                vj = block(j)[1]
                softmax_block(0, cur, vj)
                softmax_block(1, cur, vj)
            return carry

        lax.fori_loop(0, i, body, 0)
        for chain in range(2):
            l = l_ref[chain, 0:1, :]
            o_ref[chain * bq:(chain + 1) * bq, :] = _bf((acc_ref[chain] / l).T)
            lse_ref[chain] = jnp.broadcast_to(m_ref[chain, 0:1, :] + jnp.log(l), (8, bq))

    return pl.pallas_call(
        kern, name="flash_fwd", grid=(HEADS, nq // 2),
        in_specs=[pl.BlockSpec((None, 2 * bq, HEAD_PAD), lambda h, i: (h, i, 0)),
                  pl.BlockSpec((None, t, HEAD_PAD), lambda h, i: (h, 0, 0)),
                  pl.BlockSpec((None, t, V_HEAD), lambda h, i: (h, 0, 0))],
        out_specs=[pl.BlockSpec((2 * bq, V_HEAD), lambda h, i: (i, h)),
                   pl.BlockSpec((None, 2, 8, bq), lambda h, i: (h, i, 0, 0))],
        out_shape=[jax.ShapeDtypeStruct((t, HEADS * V_HEAD), BF16), jax.ShapeDtypeStruct((HEADS, nq, 8, bq), F32)],
        scratch_shapes=[pltpu.VMEM((2, 2, bq, bq), F32), pltpu.VMEM((2, bq, bq), BF16), pltpu.VMEM((2, 8, bq), F32),
                        pltpu.VMEM((2, 8, bq), F32), pltpu.VMEM((2, V_HEAD, bq), F32)],
        compiler_params=_params("parallel", "arbitrary"))(q, k, v)


def _attn_delta(do, o):
    t = do.shape[0]
    bq = _attn_block(t)

    def kern(do_ref, o_ref, d_ref):
        for h in range(HEADS):
            cols = slice(h * V_HEAD, (h + 1) * V_HEAD)
            prod = do_ref[:, cols].astype(F32) * o_ref[:, cols].astype(F32)
            d_ref[h] = jnp.broadcast_to(jnp.sum(prod.T, axis=0, keepdims=True), (8, bq))

    blk = pl.BlockSpec((bq, HEADS * V_HEAD), lambda i: (i, 0))
    return pl.pallas_call(
        kern, name="attn_delta", grid=(t // bq,), in_specs=[blk, blk],
        out_specs=pl.BlockSpec((HEADS, None, 8, bq), lambda i: (0, i, 0, 0)),
        out_shape=jax.ShapeDtypeStruct((HEADS, t // bq, 8, bq), F32),
        compiler_params=_params("parallel"))(do, o)


def _flash_bwd(q, k, v, do, lse, delta):
    t = q.shape[1]
    bq = _attn_block(t)
    nq = t // bq

    def kern(q_ref, k_ref, v_ref, do_ref, lse_ref, del_ref, dq_ref, dk_ref, dv_ref, dvt_ref):
        j = pl.program_id(1)

        @pl.when(j == 0)
        def _():
            dq_ref[...] = jnp.zeros_like(dq_ref)

        dk_ref[...] = jnp.zeros_like(dk_ref)
        dvt_ref[...] = jnp.zeros_like(dvt_ref)
        kj, vj = k_ref[...], v_ref[...]

        def step(i, masked):
            rows = pl.ds(pl.multiple_of(i * bq, bq), bq)
            qi, doi = q_ref[rows, :], do_ref[rows, :]
            st = lax.dot_general(kj, qi, (((1,), (1,)), ((), ())), preferred_element_type=F32)
            pt = jnp.exp(st - lse_ref[i][0:1, :])
            if masked:
                pt = jnp.where(_chunk_mask(bq, bq), pt, 0.0)
            dpt = lax.dot_general(vj, doi, (((1,), (1,)), ((), ())), preferred_element_type=F32)
            dst = _bf(pt * (dpt - del_ref[i][0:1, :]))
            dvt_ref[...] += lax.dot_general(doi, _bf(pt), (((0,), (1,)), ((), ())), preferred_element_type=F32)
            dk_ref[...] += jnp.dot(dst, qi, preferred_element_type=F32)
            dq_ref[rows, :] += lax.dot_general(dst, kj, (((0,), (0,)), ((), ())), preferred_element_type=F32)

        step(j, True)

        def body(pair, carry):
            step(j + 1 + 2 * pair, False)
            step(j + 2 + 2 * pair, False)
            return carry

        rest = nq - 1 - j
        lax.fori_loop(0, rest // 2, body, 0)

        @pl.when(rest % 2 == 1)
        def _():
            step(nq - 1, False)

        dv_ref[...] = dvt_ref[...].T

    stat = pl.BlockSpec((None, nq, 8, bq), lambda h, j: (h, 0, 0, 0))
    return pl.pallas_call(
        kern, name="flash_bwd", grid=(HEADS, nq),
        in_specs=[pl.BlockSpec((None, t, HEAD_PAD), lambda h, j: (h, 0, 0)),
                  pl.BlockSpec((None, bq, HEAD_PAD), lambda h, j: (h, j, 0)),
                  pl.BlockSpec((None, bq, V_HEAD), lambda h, j: (h, j, 0)),
                  pl.BlockSpec((t, V_HEAD), lambda h, j: (0, h)), stat, stat],
        out_specs=[pl.BlockSpec((None, t, HEAD_PAD), lambda h, j: (h, 0, 0)),
                   pl.BlockSpec((None, bq, HEAD_PAD), lambda h, j: (h, j, 0)),
                   pl.BlockSpec((None, bq, V_HEAD), lambda h, j: (h, j, 0))],
        out_shape=[jax.ShapeDtypeStruct((HEADS, t, HEAD_PAD), F32), jax.ShapeDtypeStruct((HEADS, t, HEAD_PAD), F32),
                   jax.ShapeDtypeStruct((HEADS, t, V_HEAD), F32)],
        scratch_shapes=[pltpu.VMEM((V_HEAD, bq), F32)],
        compiler_params=_params("parallel", "arbitrary"))(q, k, v, do, lse, delta)


def _place():
    x, y, c = lax.axis_index("x"), lax.axis_index("y"), lax.axis_index("c")
    return x, y, c, [(1 - x, y), (x, 1 - y), (1 - x, 1 - y)]


def _all_gather_rows(block, name):
    m_per, n = block.shape

    def body(x_ref, out_ref, send_sems, recv_sems, local_sem):
        x, y, c, chips = _place()
        me, sibling = (x, y, c), (x, y, 1 - c)

        def rows(px, py, pc):
            return out_ref.at[pl.ds((4 * px + 2 * py + pc) * m_per, m_per), :]

        def copy(k, blk, to, src=None):
            return pltpu.make_async_remote_copy(
                src_ref=rows(*blk) if src is None else src, dst_ref=rows(*blk), send_sem=send_sems.at[k],
                recv_sem=recv_sems.at[k], device_id=to, device_id_type=MESH)

        mine = pltpu.make_async_copy(x_ref, rows(*me), local_sem)
        mine.start()
        first = [copy(0, me, sibling, src=x_ref)]
        first += [copy(1 + j, me, (*chip, c), src=x_ref) for j, chip in enumerate(chips)]
        for cp in first:
            cp.start()
        passed = [copy(4 + j, (*chip, c), sibling) for j, chip in enumerate(chips)]
        for j, chip in enumerate(chips):
            copy(1 + j, (*chip, c), me).wait_recv()
            passed[j].start()
        copy(0, sibling, me).wait_recv()
        for j, chip in enumerate(chips):
            copy(4 + j, (*chip, 1 - c), me).wait_recv()
        for cp in first + passed:
            cp.wait_send()
        mine.wait()

    return pl.pallas_call(
        body, name=name, out_shape=jax.ShapeDtypeStruct((8 * m_per, n), block.dtype),
        in_specs=[pl.BlockSpec(memory_space=pltpu.VMEM)], out_specs=pl.BlockSpec(memory_space=pltpu.VMEM),
        scratch_shapes=[pltpu.SemaphoreType.DMA((7,)), pltpu.SemaphoreType.DMA((7,)), pltpu.SemaphoreType.DMA],
        compiler_params=pltpu.CompilerParams(vmem_limit_bytes=VMEM_LIMIT_BYTES))(block)


HBM_SPEC = pl.BlockSpec(memory_space=pltpu.HBM)
SEM_SPEC = pl.BlockSpec(memory_space=pltpu.SEMAPHORE)
DATAFLOW = pltpu.SideEffectType.DATAFLOW_SIDE_EFFECTING


def _in_hbm(a):
    return pltpu.with_memory_space_constraint(a, pltpu.HBM)


def _chip_copies(ins, lands, send_sems, recv_sems, src_slot, half=False):
    n = len(ins)
    x, y, c, chips = _place()
    me = 2 * x + y

    def ends(w, chip):
        src = ins[w].at[2 * chip[0] + chip[1]] if src_slot else ins[w]
        if not half:
            return src, lands[w].at[me]
        rows = pl.ds(pl.multiple_of(c * (src.shape[0] // 2), 16), src.shape[0] // 2)
        return src.at[rows], lands[w].at[me, rows]

    copies = []
    for w in range(n):
        for p, chip in enumerate(chips):
            src, dst = ends(w, chip)
            copies.append(pltpu.make_async_remote_copy(
                src_ref=src, dst_ref=dst, send_sem=send_sems.at[p * n + w], recv_sem=recv_sems.at[p * n + w],
                device_id=(*chip, c), device_id_type=MESH))
    return copies


def _fill_halves(lands, name):
    n = len(lands)

    def body(*refs):
        bufs = refs[n:2 * n]
        send_sems, recv_sems = refs[2 * n:]
        x, y, c, chips = _place()
        copies = []
        for w in range(n):
            hr = bufs[w].shape[1] // 2
            for p, chip in enumerate(chips):
                part = bufs[w].at[2 * chip[0] + chip[1], pl.ds(pl.multiple_of(c * hr, 16), hr)]
                copies.append(pltpu.make_async_remote_copy(
                    src_ref=part, dst_ref=part, send_sem=send_sems.at[p * n + w], recv_sem=recv_sems.at[p * n + w],
                    device_id=(x, y, 1 - c), device_id_type=MESH))
        for cp in copies:
            cp.start()
        for cp in copies:
            cp.wait_send()
        for w in range(n):
            hr = bufs[w].shape[1] // 2
            for p, chip in enumerate(chips):
                part = bufs[w].at[2 * chip[0] + chip[1], pl.ds(pl.multiple_of((1 - c) * hr, 16), hr)]
                pltpu.make_async_remote_copy(
                    src_ref=part, dst_ref=part, send_sem=send_sems.at[p * n + w], recv_sem=recv_sems.at[p * n + w],
                    device_id=(x, y, 1 - c), device_id_type=MESH).wait_recv()

    any_spec = pl.BlockSpec(memory_space=pl.ANY)
    return list(pl.pallas_call(
        body, name=name, out_shape=[jax.ShapeDtypeStruct(a.shape, a.dtype) for a in lands],
        in_specs=[any_spec] * n, out_specs=[any_spec] * n, input_output_aliases={i: i for i in range(n)},
        scratch_shapes=[pltpu.SemaphoreType.DMA((3 * n,)), pltpu.SemaphoreType.DMA((3 * n,))])(*lands))


def _exchange_start(srcs, lands, src_slot, name, dep=None, half=False):
    n = len(srcs)
    first_out = 2 * n + (dep is not None)

    def body(*refs):
        for cp in _chip_copies(refs[:n], refs[n:2 * n], refs[first_out], refs[first_out + 1], src_slot, half):
            cp.start()
        token = refs[-1]
        token[...] = jnp.zeros_like(token)

    thru = [pltpu.HBM(a.shape, a.dtype) for a in list(srcs) + list(lands)]
    res = pl.pallas_call(
        body, name=name,
        out_shape=(pltpu.SemaphoreType.DMA((3 * n,)), pltpu.SemaphoreType.DMA((3 * n,)), *thru,
                   jax.ShapeDtypeStruct((8, 128), F32)),
        in_specs=[HBM_SPEC] * (2 * n) + ([pl.BlockSpec(memory_space=pl.ANY)] if dep is not None else []),
        out_specs=(SEM_SPEC, SEM_SPEC, *[HBM_SPEC] * (2 * n), pl.BlockSpec(memory_space=pltpu.VMEM)),
        input_output_aliases={i: 2 + i for i in range(2 * n)},
        compiler_params=pltpu.CompilerParams(has_side_effects=DATAFLOW))(
            *[_in_hbm(a) for a in srcs], *[_in_hbm(a) for a in lands], *([dep] if dep is not None else []))
    return (res[0], res[1], list(res[2:2 + n]), list(res[2 + n:2 + 2 * n])), res[-1]


def _exchange_wait(flight, after, src_slot, name, half=False):
    send_sems, recv_sems, srcs, lands = flight
    n = len(srcs)

    def body(*refs):
        for cp in _chip_copies(refs[:n], refs[n:2 * n], refs[2 * n], refs[2 * n + 1], src_slot, half):
            cp.wait_send()
            cp.wait_recv()

    thru = [pltpu.HBM(a.shape, a.dtype) for a in list(srcs) + list(lands)]
    res = pl.pallas_call(
        body, name=name, out_shape=thru,
        in_specs=[HBM_SPEC] * (2 * n) + [SEM_SPEC, SEM_SPEC, pl.BlockSpec(memory_space=pl.ANY)],
        out_specs=[HBM_SPEC] * (2 * n), input_output_aliases={i: i for i in range(2 * n)},
        compiler_params=pltpu.CompilerParams(has_side_effects=DATAFLOW))(*srcs, *lands, send_sems, recv_sems, after)
    return list(res[n:])


def _landing(own, me):
    return lax.dynamic_update_index_in_dim(lax.empty((4, *own.shape), own.dtype), own, me, 0)


def _swap_with_sibling(arrays, name):
    n = len(arrays)

    def body(*refs):
        ins, outs = refs[:n], refs[n:2 * n]
        send_sems, recv_sems = refs[2 * n:]
        x, y, c, _ = _place()
        copies = [pltpu.make_async_remote_copy(src_ref=ins[w], dst_ref=outs[w], send_sem=send_sems.at[w],
                                               recv_sem=recv_sems.at[w], device_id=(x, y, 1 - c), device_id_type=MESH)
                  for w in range(n)]
        for cp in copies:
            cp.start()
        for cp in copies:
            cp.wait()

    any_spec = pl.BlockSpec(memory_space=pl.ANY)
    return pl.pallas_call(
        body, name=name, out_shape=[jax.ShapeDtypeStruct(a.shape, a.dtype) for a in arrays],
        in_specs=[any_spec] * n, out_specs=[any_spec] * n,
        scratch_shapes=[pltpu.SemaphoreType.DMA((n,)), pltpu.SemaphoreType.DMA((n,))])(*arrays)


def _as_rows(a):
    return a.reshape(-1, a.shape[-1])


def _row_tile(r, c, budget_bytes=1 << 20):
    tr = r
    while tr % 16 == 0 and tr * c * 4 > budget_bytes:
        tr //= 2
    return tr


def _sum_slots(layers, nlayer, name, into=None):
    _, r, c = layers[0][1].shape
    tr = _row_tile(r, c)
    nt = r // tr
    acc = into
    for l, r4 in layers:
        def kern(r_ref, *rest):
            o_ref = rest[-1]
            o_ref[...] = (((r_ref[0].astype(F32) + r_ref[1].astype(F32)) + r_ref[2].astype(F32))
                          + r_ref[3].astype(F32))

        out_spec = pl.BlockSpec((tr, c), lambda i, l=l: (l * nt + i, 0))
        first = acc is None
        acc = pl.pallas_call(
            kern, name=f"{name}_l{l}", grid=(nt,),
            in_specs=[pl.BlockSpec((4, tr, c), lambda i: (0, i, 0))]
            + ([] if first else [pl.BlockSpec(memory_space=pl.ANY)]),
            out_specs=out_spec, out_shape=jax.ShapeDtypeStruct((nlayer * r, c), F32),
            input_output_aliases={} if first else {1: 0},
            compiler_params=_params("parallel"))(*([r4] if first else [r4, acc]))
    return acc


def _adamw(w, m, v, parts, name):
    r, c = w.shape
    tr = _row_tile(r, c, 3 << 19)
    npart = len(parts)
    c1 = 1.0 - ADAM_B1 ** ADAM_STEP
    c2 = 1.0 - ADAM_B2 ** ADAM_STEP

    def kern(*refs):
        w_ref, m_ref, v_ref = refs[:3]
        p_refs = refs[3:3 + npart]
        g_ref, d_ref, mo_ref, vo_ref = refs[3 + npart:]
        g = p_refs[0][...]
        for p in p_refs[1:]:
            g = g + p[...]
        mn = ADAM_B1 * m_ref[...] + (1.0 - ADAM_B1) * g
        vn = ADAM_B2 * v_ref[...] + (1.0 - ADAM_B2) * (g * g)
        g_ref[...] = g
        mo_ref[...] = mn
        vo_ref[...] = vn
        d_ref[...] = -ADAM_LR * ((mn / c1) / (jnp.sqrt(vn / c2) + ADAM_EPS) + ADAM_WD * w_ref[...])

    blk = pl.BlockSpec((tr, c), lambda i: (i, 0))
    shape = jax.ShapeDtypeStruct((r, c), F32)
    return pl.pallas_call(
        kern, name=name, grid=(r // tr,), in_specs=[blk] * (3 + npart), out_specs=[blk] * 4, out_shape=[shape] * 4,
        compiler_params=_params("parallel"))(w, m, v, *parts)


def _sum_devices(g8, name):
    _, r, c = g8.shape

    def kern(g_ref, o_ref):
        tot = g_ref[0]
        for dev in range(1, 8):
            tot = tot + g_ref[dev]
        o_ref[...] = tot

    return pl.pallas_call(
        kern, name=name, grid=(1,), in_specs=[pl.BlockSpec((8, r, c), lambda i: (0, 0, 0))],
        out_specs=pl.BlockSpec((r, c), lambda i: (0, 0)), out_shape=jax.ShapeDtypeStruct((r, c), F32),
        compiler_params=_params("arbitrary"))(g8)


def _pad_lanes(a, width):
    return jnp.pad(a, [(0, 0)] * (a.ndim - 1) + [(0, width - a.shape[-1])])


def kernel(x, positions, ffn_norm1, ffn1_w1, ffn1_w3, ffn1_w2, mix_norm, ffn_norm2, ffn2_w1, ffn2_w3, ffn2_w2, conv_w_pw1, conv_w_dw, conv_norm, conv_w_pw2, mla_w_a, mla_q_norm, mla_kv_norm, mla_w_uq, mla_w_ukv, mla_w_o, final_norm, loss_target, m_ffn_norm1, m_ffn1_w1, m_ffn1_w3, m_ffn1_w2, m_mix_norm, m_ffn_norm2, m_ffn2_w1, m_ffn2_w3, m_ffn2_w2, m_conv_w_pw1, m_conv_w_dw, m_conv_norm, m_conv_w_pw2, m_mla_w_a, m_mla_q_norm, m_mla_kv_norm, m_mla_w_uq, m_mla_w_ukv, m_mla_w_o, m_final_norm, v_ffn_norm1, v_ffn1_w1, v_ffn1_w3, v_ffn1_w2, v_mix_norm, v_ffn_norm2, v_ffn2_w1, v_ffn2_w3, v_ffn2_w2, v_conv_w_pw1, v_conv_w_dw, v_conv_norm, v_conv_w_pw2, v_mla_w_a, v_mla_q_norm, v_mla_kv_norm, v_mla_w_uq, v_mla_w_ukv, v_mla_w_o, v_final_norm):
    given = locals()
    return _step({nm: given[nm] for nm in INPUTS})


def _step(A):
    x = A['x'][0]
    target = A['loss_target'][0]
    t, d = x.shape
    pos = A['positions'].reshape(t, 1)
    me = 2 * lax.axis_index("x") + lax.axis_index("y")

    flipped = {f'ffn{k}_{w}' for k in (1, 2) for w in ('w1', 'w3')}
    P = {}
    for nm in BIG:
        for key in (nm, 'm_' + nm, 'v_' + nm):
            P[key] = jnp.swapaxes(A[key], 1, 2) if nm in flipped else A[key]

    def unflip(nm, a):
        return jnp.swapaxes(a, 1, 2) if nm in flipped else a

    ffn = [f'ffn{k}_{w}' for k in (1, 2) for w in ('w1', 'w3', 'w2')]
    gather_groups = [[(nm, 0) for nm in ffn[:3]],
                     [('conv_w_pw1', 0), ('conv_w_pw2', 0)] + [(nm, 0) for nm in ffn[3:]],
                     [(nm, 1) for nm in ffn[:3]] + [('mla_w_a', 0), ('mla_w_uq', 0), ('mla_w_ukv', 0), ('mla_w_o', 0)],
                     [(nm, 1) for nm in ffn[3:]]]
    halved = (0, 1)
    gather_flights = {}
    big = {}

    def gather_start(gi, dep):
        shards = [_bf(P[nm][l]) for nm, l in gather_groups[gi]]
        gather_flights[gi], token = _exchange_start(shards, [_landing(s, me) for s in shards], False,
                                                    f"gather_start_{gi}", dep, half=gi in halved)
        return token

    def gather_wait(gi, after):
        landed = _exchange_wait(gather_flights[gi], after, False, f"gather_wait_{gi}", half=gi in halved)
        if gi in halved:
            landed = _fill_halves(landed, f"gather_fill_{gi}")
        big.update(zip(gather_groups[gi], landed))
        return landed[0]

    dw_shard = A['conv_w_dw'][0]
    cw = dw_shard.shape[1]
    small = jnp.concatenate([
        jnp.pad(dw_shard, ((0, CONV_HALO - CONV_WIDTH), (0, 0))),
        jnp.pad(_pad_lanes(A['mla_q_norm'], cw), ((0, 7), (0, 0))),
        jnp.pad(_pad_lanes(A['mla_kv_norm'], cw), ((0, 7), (0, 0)))], axis=0)
    small = _all_gather_rows(small, "gather_small_weights").reshape(4, 2, 48, cw)[:, 0]
    w_dw = jnp.concatenate([small[j, :CONV_HALO] for j in range(4)], axis=1)
    gq = jnp.concatenate([small[j, CONV_HALO, :Q_LORA // 4] for j in range(4)])
    gkv = jnp.concatenate([small[j, CONV_HALO + 8, :KV_LORA // 4] for j in range(4)])

    def rows(nm, layer):
        g = big[nm, layer]
        return g.reshape(-1, g.shape[-1])

    ffn_w = {}

    def ffn_weights(k, l):
        ffn_w[k, l] = (rows(f'ffn{k}_w1', l), rows(f'ffn{k}_w3', l), rows(f'ffn{k}_w2', l))
        return ffn_w[k, l]

    token = gather_start(0, small)
    cs_c, cs_s = _rope_tables(pos)
    h0 = x
    token = gather_start(1, gather_wait(0, token))
    h1, n01, z01a, z01b = _ffn_fwd(h0, A['ffn_norm1'][0], *ffn_weights(1, 0), token, "ffn1_l0_fwd")
    token = gather_start(3, gather_start(2, gather_wait(1, h1)))
    pw1 = big['conv_w_pw1', 0]
    pw1_a = jnp.concatenate([pw1[0], pw1[1]], axis=1)
    pw1_b = jnp.concatenate([pw1[2], pw1[3]], axis=1)
    pw2 = rows('conv_w_pw2', 0)
    m0 = _norm_fwd(h1, A['mix_norm'][0], token, "mix_norm_l0")
    ca, cb, glu = _glu_fwd(m0, pw1_a, pw1_b)
    cv, cs = _conv_fwd(glu, w_dw, A['conv_norm'][0])
    h2 = _mm([(cs, pw2)], F32, "conv_pw2_fwd", res=h1)
    h3, n02, z02a, z02b = _ffn_fwd(h2, A['ffn_norm2'][0], *ffn_weights(2, 0), token, "ffn2_l0_fwd")
    gather_wait(2, h3)
    w_a = _pad_lanes(rows('mla_w_a', 0), A_PAD)
    wuq = _pad_lanes(big['mla_w_uq', 0].reshape(Q_LORA, HEADS, NOPE + ROPE), HEAD_PAD).reshape(Q_LORA, -1)
    wukv = big['mla_w_ukv', 0].reshape(KV_LORA, HEADS * (NOPE + V_HEAD))
    w_o = rows('mla_w_o', 0)
    h4, n11, z11a, z11b = _ffn_fwd(h3, A['ffn_norm1'][1], *ffn_weights(1, 1), token, "ffn1_l1_fwd")
    m1 = _norm_fwd(h4, A['mix_norm'][1], token, "mix_norm_l1")
    a_lat = _mm([(m1, w_a)], F32, "mla_down_fwd")
    cq, ckv, kr = _mla_prep(a_lat, gq, gkv, cs_c, cs_s)
    q, k, v = _mla_qkv(cq, ckv, kr, cs_c, cs_s, wuq, wukv)
    o, lse = _flash_fwd(q, k, v)
    h5 = _mm([(o, w_o)], F32, "mla_out_fwd", res=h4)
    gather_wait(3, h5)
    h6, n12, z12a, z12b = _ffn_fwd(h5, A['ffn_norm2'][1], *ffn_weights(2, 1), token, "ffn2_l1_fwd")

    def row_slots(g):
        return g.reshape(4, g.shape[0] // 4, g.shape[1])

    scatter_flights = []

    def scatter_start(named):
        srcs = [g for _, g in named]
        lands = [_landing(lax.dynamic_index_in_dim(g, me, 0, keepdims=False), me) for g in srcs]
        flight, token = _exchange_start(srcs, lands, True, f"scatter_start_{len(scatter_flights)}")
        scatter_flights.append(([key for key, _ in named], flight))
        return token

    def send_ffn(k, l, dw1t, dw3t, dw2):
        return scatter_start([((f'ffn{k}_w1', l), row_slots(dw1t)), ((f'ffn{k}_w3', l), row_slots(dw3t)),
                              ((f'ffn{k}_w2', l), row_slots(dw2))])

    dh6, dg_final, loss_part = _loss_bwd(h6, target, A['final_norm'])
    dh5, dg_n2_l1, *dws = _ffn_bwd(dh6, h5, A['ffn_norm2'][1], n12, z12a, z12b, *ffn_w[2, 1], loss_part, "ffn2_l1")
    token = send_ffn(2, 1, *dws)

    do = _mm([(dh5, w_o)], BF16, "mla_out_bwd", trans_b=True, dep=token)
    dw_o = _mm_tn(o, dh5, BF16, "mla_dw_o")
    delta = _attn_delta(do, o)
    dq, dk, dv = _flash_bwd(q, k, v, do, lse, delta)
    dr, dkv, dcq, dckv, dar = _mla_qkv_bwd(dq, dk, dv, cs_c, cs_s, wuq, wukv)
    dwuq = _mm_tn(cq, dr, BF16, "mla_dw_uq", bn=dr.shape[1] // 2)
    dwukv = _mm_tn(ckv, dkv, BF16, "mla_dw_ukv", bn=dkv.shape[1] // 2)
    da_lat, dgq, dgkv = _mla_prep_bwd(a_lat, dcq, dckv, dar, gq, gkv)
    dw_a = _mm_tn(m1, da_lat, BF16, "mla_dw_a")
    token = scatter_start([
        (('mla_w_a', 0), row_slots(dw_a[:, :Q_LORA + KV_LORA + ROPE])),
        (('mla_w_uq', 0), dwuq.reshape(4, Q_LORA // 4, HEADS, HEAD_PAD)[..., :NOPE + ROPE]),
        (('mla_w_ukv', 0), dwukv.reshape(4, KV_LORA // 4, HEADS, NOPE + V_HEAD)),
        (('mla_w_o', 0), row_slots(dw_o))])
    dh4, dg_mix_l1 = _mm_normbwd([(da_lat, w_a)], h4, A['mix_norm'][1], dh5, token, "mla_down_bwd")

    dh3, dg_n1_l1, *dws = _ffn_bwd(dh4, h3, A['ffn_norm1'][1], n11, z11a, z11b, *ffn_w[1, 1], token, "ffn1_l1")
    token = send_ffn(1, 1, *dws)
    dh2, dg_n2_l0, *dws = _ffn_bwd(dh3, h2, A['ffn_norm2'][0], n02, z02a, z02b, *ffn_w[2, 0], token, "ffn2_l0")
    token = send_ffn(2, 0, *dws)

    dcv, dg_conv = _conv_bwd_norm(dh2, cv, pw2, A['conv_norm'][0], token)
    dw_pw2 = _mm_tn(cs, dh2, BF16, "conv_dw_pw2")
    dca, dcb, ddw = _conv_bwd_dw(dcv, glu, ca, cb, w_dw)
    dpw1_a = _mm_tn(m0, dca, BF16, "conv_dw_pw1a")
    dpw1_b = _mm_tn(m0, dcb, BF16, "conv_dw_pw1b")
    half = dpw1_a.shape[1] // 2
    token = scatter_start([
        (('conv_w_pw1', 0), jnp.stack([dpw1_a[:, :half], dpw1_a[:, half:], dpw1_b[:, :half], dpw1_b[:, half:]])),
        (('conv_w_pw2', 0), row_slots(dw_pw2))])
    dh1, dg_mix_l0 = _mm_normbwd([(dca, pw1_a), (dcb, pw1_b)], h1, A['mix_norm'][0], dh2, token, "conv_pw1_bwd")

    dx, dg_n1_l0, *dws = _ffn_bwd(dh1, h0, A['ffn_norm1'][0], n01, z01a, z01b, *ffn_w[1, 0], token, "ffn1_l0")
    last_sent = send_ffn(1, 0, *dws)
    out = {}

    qkv_row = jnp.concatenate([dgq, dgkv, jnp.zeros((8, d - Q_LORA - KV_LORA), F32)], axis=1)
    loss_row = _pad_lanes(loss_part, d)
    small_g = jnp.concatenate([dg_n1_l0, dg_n1_l1, dg_mix_l0, dg_mix_l1, dg_n2_l0, dg_n2_l1, dg_conv, dg_final,
                               qkv_row, loss_row, ddw], axis=0)
    nrow = small_g.shape[0]
    tot = _sum_devices(_all_gather_rows(small_g, "gather_small_grads").reshape(8, nrow, d), "sum_small_grads")
    loss = tot[72, 0]
    q_shard = lax.dynamic_slice_in_dim(tot[64, :Q_LORA], me * (Q_LORA // 4), Q_LORA // 4)
    kv_shard = lax.dynamic_slice_in_dim(tot[64, Q_LORA:Q_LORA + KV_LORA], me * (KV_LORA // 4), KV_LORA // 4)
    dw_shard_g = lax.dynamic_slice_in_dim(tot[80:80 + CONV_WIDTH], me * cw, cw, axis=1)
    small_grads = {
        'ffn_norm1': jnp.stack([tot[0], tot[8]]), 'mix_norm': jnp.stack([tot[16], tot[24]]),
        'ffn_norm2': jnp.stack([tot[32], tot[40]]), 'conv_norm': tot[48][None], 'final_norm': tot[56],
        'mla_q_norm': q_shard[None], 'mla_kv_norm': kv_shard[None], 'conv_w_dw': dw_shard_g[None],
    }
    for nm, g in small_grads.items():
        res = _adamw(_as_rows(A[nm]) if A[nm].ndim > 1 else A[nm].reshape(1, -1),
                     A['m_' + nm].reshape(-1, A[nm].shape[-1]), A['v_' + nm].reshape(-1, A[nm].shape[-1]),
                     [g.reshape(-1, A[nm].shape[-1])], "adamw_" + nm)
        out[nm] = [r.reshape(A[nm].shape) for r in res]

    received = {}
    after = last_sent

    def scatter_wait(si, after):
        keys, flight = scatter_flights[si]
        landed = _exchange_wait(flight, after, True, f"scatter_wait_{si}")
        received.update(zip(keys, landed))
        return landed[0]

    def slots(nm, l):
        return received[nm, l].reshape(4, -1, received[nm, l].shape[-1])

    def finish(names, sums, tag):
        for nm, mine, theirs in zip(names, sums, _swap_with_sibling(sums, "swap_with_sibling_" + tag)):
            res = _adamw(_as_rows(P[nm]), _as_rows(P['m_' + nm]), _as_rows(P['v_' + nm]), [mine, theirs],
                         "adamw_" + nm)
            out[nm] = [unflip(nm, r.reshape(P[nm].shape)) for r in res]
        return res[1]

    last = len(scatter_flights) - 1
    for si in range(last):
        after = scatter_wait(si, after)
    late = ffn[:3]
    early = [nm for nm in BIG if nm not in late]
    late_l1 = [_sum_slots([(1, slots(nm, 1))], 2, "sum_" + nm) for nm in late]
    after = finish(early, [_sum_slots([(l, slots(nm, l)) for l in range(A[nm].shape[0])], A[nm].shape[0],
                                      "sum_" + nm) for nm in early], "early")
    scatter_wait(last, after)
    finish(late, [_sum_slots([(0, slots(nm, 0))], 2, "sum_" + nm, into=part) for nm, part in zip(late, late_l1)],
           "late")

    return (loss, dx[None], *[out[nm][0] for nm in WEIGHTS], *[out[nm][1] for nm in WEIGHTS],
            *[out[nm][2] for nm in WEIGHTS], *[out[nm][3] for nm in WEIGHTS])
```

```python
import functools

import jax
import jax.numpy as jnp
import numpy as np
from jax import lax
from jax.experimental import pallas as pl
from jax.experimental.pallas import tpu as pltpu

F32 = jnp.float32
BF16 = jnp.bfloat16
MESH = pl.DeviceIdType.MESH

RMS_EPS = 1e-6
HEADS = 8
NOPE = 128
ROPE = 64
HEAD_PAD = 256
V_HEAD = 128
Q_LORA = 512
KV_LORA = 256
A_PAD = 896
CHUNK = 64
CONV_WIDTH = 31
CONV_HALO = 32
CONV_ROWS = 16
ROPE_THETA = 10000.0
ATTN_SCALE = (NOPE + ROPE) ** -0.5
FFN_RES = 0.5

ADAM_LR = 0.001
ADAM_B1 = 0.9
ADAM_B2 = 0.999
ADAM_EPS = 1e-08
ADAM_WD = 0.01
ADAM_STEP = 10

VMEM_LIMIT_BYTES = 56 * 1024 * 1024

WEIGHTS = ['ffn_norm1', 'ffn1_w1', 'ffn1_w3', 'ffn1_w2', 'mix_norm', 'ffn_norm2', 'ffn2_w1', 'ffn2_w3', 'ffn2_w2',
           'conv_w_pw1', 'conv_w_dw', 'conv_norm', 'conv_w_pw2', 'mla_w_a', 'mla_q_norm', 'mla_kv_norm', 'mla_w_uq',
           'mla_w_ukv', 'mla_w_o', 'final_norm']
INPUTS = (['x', 'positions'] + WEIGHTS + ['loss_target'] + ['m_' + w for w in WEIGHTS] + ['v_' + w for w in WEIGHTS])
BIG = ['ffn1_w1', 'ffn1_w3', 'ffn1_w2', 'ffn2_w1', 'ffn2_w3', 'ffn2_w2', 'conv_w_pw1', 'conv_w_pw2', 'mla_w_a',
       'mla_w_uq', 'mla_w_ukv', 'mla_w_o']


def _params(*sem):
    return pltpu.CompilerParams(dimension_semantics=sem, vmem_limit_bytes=VMEM_LIMIT_BYTES)


def _bf(v):
    return v.astype(BF16)


def _rstd(x):
    return lax.rsqrt(jnp.mean(x * x, axis=-1, keepdims=True) + RMS_EPS)


def _sigmoid(x):
    return jax.nn.sigmoid(x)


def _rot(x):
    lane = lax.broadcasted_iota(jnp.int32, x.shape, 1)
    return jnp.where(lane < ROPE // 2, -pltpu.roll(x, 128 - ROPE // 2, 1), pltpu.roll(x, ROPE // 2, 1))


def _rot_t(y):
    lane = lax.broadcasted_iota(jnp.int32, y.shape, 1)
    return jnp.where(lane < ROPE // 2, pltpu.roll(y, 128 - ROPE // 2, 1), -pltpu.roll(y, ROPE // 2, 1))


def _pair_sum(a_refs, b_refs, trans_b):
    tot = None
    for a_r, b_r in zip(a_refs, b_refs):
        a, b = _bf(a_r[...]), _bf(b_r[...])
        if trans_b:
            d = lax.dot_general(a, b, (((1,), (1,)), ((), ())), preferred_element_type=F32)
        else:
            d = jnp.dot(a, b, preferred_element_type=F32)
        tot = d if tot is None else tot + d
    return tot


def _mm(pairs, out_dtype, name, *, trans_b=False, tm=512, tn=None, tk=None, res=None, dep=None):
    m, k = pairs[0][0].shape
    n = pairs[0][1].shape[0] if trans_b else pairs[0][1].shape[1]
    tm, tn, tk = min(tm, m), tn or n, tk or k
    nk, npair = k // tk, len(pairs)

    def kern(*refs):
        a_refs, b_refs = refs[:npair], refs[npair:2 * npair]
        rest = list(refs[2 * npair:])
        res_ref = rest.pop(0) if res is not None else None
        if dep is not None:
            rest.pop(0)
        o_ref = rest.pop(0)

        def finish(acc):
            if res_ref is not None:
                acc = res_ref[...] + acc
            o_ref[...] = acc.astype(o_ref.dtype)

        if nk == 1:
            finish(_pair_sum(a_refs, b_refs, trans_b))
        else:
            acc_ref = rest.pop(0)
            kk = pl.program_id(2)

            @pl.when(kk == 0)
            def _():
                acc_ref[...] = jnp.zeros_like(acc_ref)

            acc_ref[...] += _pair_sum(a_refs, b_refs, trans_b)

            @pl.when(kk == nk - 1)
            def _():
                finish(acc_ref[...])

    a_spec = pl.BlockSpec((tm, tk), lambda i, j, kk: (i, kk))
    b_spec = (pl.BlockSpec((tn, tk), lambda i, j, kk: (j, kk)) if trans_b
              else pl.BlockSpec((tk, tn), lambda i, j, kk: (kk, j)))
    io_spec = pl.BlockSpec((tm, tn), lambda i, j, kk: (i, j))
    in_specs = ([a_spec] * npair + [b_spec] * npair + ([io_spec] if res is not None else [])
                + ([pl.BlockSpec((8, 128), lambda i, j, kk: (0, 0))] if dep is not None else []))
    args = ([p[0] for p in pairs] + [p[1] for p in pairs] + ([res] if res is not None else [])
            + ([dep] if dep is not None else []))
    return pl.pallas_call(
        kern, name=name, grid=(m // tm, n // tn, nk), in_specs=in_specs, out_specs=io_spec,
        out_shape=jax.ShapeDtypeStruct((m, n), out_dtype),
        scratch_shapes=[pltpu.VMEM((tm, tn), F32)] if nk > 1 else [],
        compiler_params=_params("parallel", "parallel", "arbitrary"))(*args)


def _mm_normbwd(pairs, h, g, dres, dep, name, *, tm=512, tk=None):
    m, k = pairs[0][0].shape
    d = pairs[0][1].shape[0]
    tm, tk = min(tm, m), tk or k
    nk, npair = k // tk, len(pairs)

    def kern(*refs):
        a_refs, b_refs = refs[:npair], refs[npair:2 * npair]
        h_ref, g_ref, dres_ref, _, o_ref, dg_ref, acc_ref = refs[2 * npair:]
        i, kk = pl.program_id(0), pl.program_id(1)

        @pl.when(jnp.logical_and(i == 0, kk == 0))
        def _():
            dg_ref[...] = jnp.zeros_like(dg_ref)

        @pl.when(kk == 0)
        def _():
            acc_ref[...] = jnp.zeros_like(acc_ref)

        acc_ref[...] += _pair_sum(a_refs, b_refs, True)

        @pl.when(kk == nk - 1)
        def _():
            dn = acc_ref[...]
            x = h_ref[...]
            rstd = _rstd(x)
            xhat = x * rstd
            dg_ref[...] += jnp.broadcast_to(jnp.sum(dn * xhat, axis=0, keepdims=True), dg_ref.shape)
            dxh = dn * g_ref[...]
            dx = rstd * (dxh - xhat * jnp.mean(dxh * xhat, axis=-1, keepdims=True))
            o_ref[...] = dres_ref[...] + dx

    row = pl.BlockSpec((tm, d), lambda i, kk: (i, 0))
    in_specs = ([pl.BlockSpec((tm, tk), lambda i, kk: (i, kk))] * npair
                + [pl.BlockSpec((d, tk), lambda i, kk: (0, kk))] * npair
                + [row, pl.BlockSpec((1, d), lambda i, kk: (0, 0)), row, pl.BlockSpec((8, 128), lambda i, kk: (0, 0))])
    return pl.pallas_call(
        kern, name=name, grid=(m // tm, nk), in_specs=in_specs,
        out_specs=[row, pl.BlockSpec((8, d), lambda i, kk: (0, 0))],
        out_shape=[jax.ShapeDtypeStruct((m, d), F32), jax.ShapeDtypeStruct((8, d), F32)],
        scratch_shapes=[pltpu.VMEM((tm, d), F32)],
        compiler_params=_params("arbitrary", "arbitrary"))(
            *[p[0] for p in pairs], *[p[1] for p in pairs], h, g.reshape(1, d), dres, dep)


def _mm_tn(a, b, out_dtype, name, *, bm=None, bn=None, tk=1024):
    t, m = a.shape
    batched = b.ndim == 3
    n = b.shape[-1]
    nb = b.shape[0] if batched else 1
    bm, bn, tk = bm or m, bn or n, min(tk, t)
    nk = t // tk

    def kern(a_ref, b_ref, o_ref, acc_ref):
        kk = pl.program_id(3)

        @pl.when(kk == 0)
        def _():
            acc_ref[...] = jnp.zeros_like(acc_ref)

        acc_ref[...] += lax.dot_general(_bf(a_ref[...]), _bf(b_ref[...]), (((0,), (0,)), ((), ())),
                                        preferred_element_type=F32)

        @pl.when(kk == nk - 1)
        def _():
            o_ref[...] = acc_ref[...].astype(o_ref.dtype)

    a_spec = pl.BlockSpec((tk, bm), lambda h, i, j, kk: (kk, i))
    if batched:
        b_spec = pl.BlockSpec((None, tk, bn), lambda h, i, j, kk: (h, kk, j))
        o_spec = pl.BlockSpec((None, bm, bn), lambda h, i, j, kk: (h, i, j))
        out_shape = jax.ShapeDtypeStruct((nb, m, n), out_dtype)
    else:
        b_spec = pl.BlockSpec((tk, bn), lambda h, i, j, kk: (kk, j))
        o_spec = pl.BlockSpec((bm, bn), lambda h, i, j, kk: (i, j))
        out_shape = jax.ShapeDtypeStruct((m, n), out_dtype)
    return pl.pallas_call(
        kern, name=name, grid=(nb, m // bm, n // bn, nk), in_specs=[a_spec, b_spec], out_specs=o_spec,
        out_shape=out_shape, scratch_shapes=[pltpu.VMEM((bm, bn), F32)],
        compiler_params=_params("parallel", "parallel", "parallel", "arbitrary"))(a, b)


def _ffn_tile(f):
    return f // 2 if (f // 2) % 128 == 0 else f


def _ffn_fwd(h, g, w1t, w3t, w2, dep, name):
    t, d = h.shape
    f = w1t.shape[0]
    tm = min(256, t)
    nt = (((1,), (1,)), ((), ()))

    def kern(h_ref, g_ref, w1_hbm, w3_hbm, w2_hbm, dep_ref, ho_ref, n_ref, z1_ref, z3_ref,
             w1_ref, w3_ref, w2_ref, sems):
        @pl.when(pl.program_id(0) == 0)
        def _():
            copies = [pltpu.make_async_copy(src, dst, sems.at[k]) for k, (src, dst) in
                      enumerate(((w1_hbm, w1_ref), (w3_hbm, w3_ref), (w2_hbm, w2_ref)))]
            for cp in copies:
                cp.start()
            for cp in copies:
                cp.wait()

        x = h_ref[...]
        n = _bf(x * _rstd(x) * g_ref[...])
        n_ref[...] = n
        z1 = lax.dot_general(n, w1_ref[...], nt, preferred_element_type=F32)
        z3 = lax.dot_general(n, w3_ref[...], nt, preferred_element_type=F32)
        z1_ref[...] = _bf(z1)
        z3_ref[...] = _bf(z3)
        act = _bf(z1 * _sigmoid(z1) * z3)
        ho_ref[...] = x + FFN_RES * jnp.dot(act, w2_ref[...], preferred_element_type=F32)

    row = pl.BlockSpec((tm, d), lambda i: (i, 0))
    col = pl.BlockSpec((tm, f), lambda i: (i, 0))
    whole = pl.BlockSpec(memory_space=pl.ANY)
    return pl.pallas_call(
        kern, name=name, grid=(t // tm,),
        in_specs=[row, pl.BlockSpec((1, d), lambda i: (0, 0)), whole, whole, whole,
                  pl.BlockSpec((8, 128), lambda i: (0, 0))],
        out_specs=[row, row, col, col],
        out_shape=[jax.ShapeDtypeStruct((t, d), F32), jax.ShapeDtypeStruct((t, d), BF16),
                   jax.ShapeDtypeStruct((t, f), BF16), jax.ShapeDtypeStruct((t, f), BF16)],
        scratch_shapes=[pltpu.VMEM((f, d), BF16), pltpu.VMEM((f, d), BF16), pltpu.VMEM((f, d), BF16),
                        pltpu.SemaphoreType.DMA((3,))],
        compiler_params=_params("arbitrary"))(h, g.reshape(1, d), w1t, w3t, w2, dep)


def _ffn_bwd_x(dh, h_in, g, z1, z3, w1t, w3t, w2, dep, name):
    t, d = dh.shape
    f = z1.shape[1]
    tm = min(256, t)

    def kern(dh_ref, h_ref, g_ref, z1_ref, z3_ref, w2_hbm, w1_hbm, w3_hbm, dep_ref,
             o_ref, dg_ref, dz1_ref, dz3_ref, a_ref, df_ref, w2_ref, w1_ref, w3_ref, sems):
        @pl.when(pl.program_id(0) == 0)
        def _():
            copies = [pltpu.make_async_copy(src, dst, sems.at[k]) for k, (src, dst) in
                      enumerate(((w2_hbm, w2_ref), (w1_hbm, w1_ref), (w3_hbm, w3_ref)))]
            for cp in copies:
                cp.start()
            dg_ref[...] = jnp.zeros_like(dg_ref)
            for cp in copies:
                cp.wait()

        df = _bf(FFN_RES * dh_ref[...])
        df_ref[...] = df
        da = lax.dot_general(df, w2_ref[...], (((1,), (1,)), ((), ())), preferred_element_type=F32)
        z1v, z3v = z1_ref[...].astype(F32), z3_ref[...].astype(F32)
        sig = _sigmoid(z1v)
        silu = z1v * sig
        a_ref[...] = _bf(silu * z3v)
        dz1 = _bf(da * z3v * (sig * (1.0 + z1v * (1.0 - sig))))
        dz3 = _bf(da * silu)
        dz1_ref[...] = dz1
        dz3_ref[...] = dz3
        dn = (jnp.dot(dz1, w1_ref[...], preferred_element_type=F32)
              + jnp.dot(dz3, w3_ref[...], preferred_element_type=F32))
        x = h_ref[...]
        rstd = _rstd(x)
        xhat = x * rstd
        dg_ref[...] += jnp.broadcast_to(jnp.sum(dn * xhat, axis=0, keepdims=True), dg_ref.shape)
        dxh = dn * g_ref[...]
        o_ref[...] = dh_ref[...] + rstd * (dxh - xhat * jnp.mean(dxh * xhat, axis=-1, keepdims=True))

    row = pl.BlockSpec((tm, d), lambda i: (i, 0))
    col = pl.BlockSpec((tm, f), lambda i: (i, 0))
    whole = pl.BlockSpec(memory_space=pl.ANY)
    colshape = jax.ShapeDtypeStruct((t, f), BF16)
    return pl.pallas_call(
        kern, name=name, grid=(t // tm,),
        in_specs=[row, row, pl.BlockSpec((1, d), lambda i: (0, 0)), col, col, whole, whole, whole,
                  pl.BlockSpec((8, 128), lambda i: (0, 0))],
        out_specs=[row, pl.BlockSpec((8, d), lambda i: (0, 0)), col, col, col, row],
        out_shape=[jax.ShapeDtypeStruct((t, d), F32), jax.ShapeDtypeStruct((8, d), F32), colshape, colshape, colshape,
                   jax.ShapeDtypeStruct((t, d), BF16)],
        scratch_shapes=[pltpu.VMEM((f, d), BF16), pltpu.VMEM((f, d), BF16), pltpu.VMEM((f, d), BF16),
                        pltpu.SemaphoreType.DMA((3,))],
        compiler_params=_params("arbitrary"))(dh, h_in, g.reshape(1, d), z1, z3, w2, w1t, w3t, dep)


def _ffn_bwd(dh, h_in, g, n, z1, z3, w1t, w3t, w2, dep, tag):
    f = w2.shape[0]
    dh_in, dg, dz1, dz3, act, df = _ffn_bwd_x(dh, h_in, g, z1, z3, w1t, w3t, w2, dep, tag + "_bwd_x")
    dw1t = _mm_tn(dz1, n, BF16, tag + "_dw1", bm=_ffn_tile(f), tk=2048)
    dw3t = _mm_tn(dz3, n, BF16, tag + "_dw3", bm=_ffn_tile(f), tk=2048)
    dw2 = _mm_tn(act, df, BF16, tag + "_dw2", bm=_ffn_tile(f), tk=2048)
    return dh_in, dg, dw1t, dw3t, dw2


def _norm_fwd(h, g, dep, name):
    t, d = h.shape
    tm = min(512, t)

    def kern(h_ref, g_ref, dep_ref, o_ref):
        x = h_ref[...]
        o_ref[...] = _bf(x * _rstd(x) * g_ref[...])

    row = pl.BlockSpec((tm, d), lambda i: (i, 0))
    return pl.pallas_call(
        kern, name=name, grid=(t // tm,),
        in_specs=[row, pl.BlockSpec((1, d), lambda i: (0, 0)), pl.BlockSpec((8, 128), lambda i: (0, 0))],
        out_specs=row, out_shape=jax.ShapeDtypeStruct((t, d), BF16),
        compiler_params=_params("parallel"))(h, g.reshape(1, d), dep)


def _loss_bwd(h, target, g):
    t, d = h.shape
    tm = min(512, t)

    def kern(h_ref, t_ref, g_ref, dh_ref, dg_ref, loss_ref):
        @pl.when(pl.program_id(0) == 0)
        def _():
            dg_ref[...] = jnp.zeros_like(dg_ref)
            loss_ref[...] = jnp.zeros_like(loss_ref)

        x = h_ref[...]
        rstd = _rstd(x)
        xhat = x * rstd
        err = xhat * g_ref[...] - t_ref[...]
        row_loss = jnp.sum(err * err, axis=-1, keepdims=True) * (0.5 / d)
        loss_ref[...] += jnp.broadcast_to(jnp.sum(row_loss, axis=0, keepdims=True), loss_ref.shape)
        dy = err * (1.0 / d)
        dg_ref[...] += jnp.broadcast_to(jnp.sum(dy * xhat, axis=0, keepdims=True), dg_ref.shape)
        dxh = dy * g_ref[...]
        dh_ref[...] = rstd * (dxh - xhat * jnp.mean(dxh * xhat, axis=-1, keepdims=True))

    row = pl.BlockSpec((tm, d), lambda i: (i, 0))
    return pl.pallas_call(
        kern, name="loss_bwd", grid=(t // tm,),
        in_specs=[row, row, pl.BlockSpec((1, d), lambda i: (0, 0))],
        out_specs=[row, pl.BlockSpec((8, d), lambda i: (0, 0)), pl.BlockSpec((8, 128), lambda i: (0, 0))],
        out_shape=[jax.ShapeDtypeStruct((t, d), F32), jax.ShapeDtypeStruct((8, d), F32),
                   jax.ShapeDtypeStruct((8, 128), F32)],
        compiler_params=_params("arbitrary"))(h, target, g.reshape(1, d))


def _glu_fwd(m, wa, wb):
    t, d = m.shape
    c = wa.shape[1]
    tm, tc = min(512, t), min(512, c)

    def kern(m_ref, wa_ref, wb_ref, a_ref, b_ref, glu_ref):
        mv = m_ref[...]
        a = jnp.dot(mv, wa_ref[...], preferred_element_type=F32)
        b = jnp.dot(mv, wb_ref[...], preferred_element_type=F32)
        a_ref[...] = _bf(a)
        b_ref[...] = _bf(b)
        glu_ref[...] = _bf(a * _sigmoid(b))

    col = pl.BlockSpec((tm, tc), lambda i, j: (i, j))
    wspec = pl.BlockSpec((d, tc), lambda i, j: (0, j))
    shape = jax.ShapeDtypeStruct((t, c), BF16)
    return pl.pallas_call(
        kern, name="conv_glu_fwd", grid=(t // tm, c // tc),
        in_specs=[pl.BlockSpec((tm, d), lambda i, j: (i, 0)), wspec, wspec], out_specs=[col, col, col],
        out_shape=[shape, shape, shape], compiler_params=_params("parallel", "parallel"))(m, wa, wb)


def _conv_tile(t):
    return min(256, t)


def _shift_copies(ext, shifted, rows):
    for s in range(8):
        shifted[s] = ext[pl.ds(s, rows), :]


def _shifted_rows(shifted, start, nrows):
    return shifted[start % 8, pl.ds(start - start % 8, nrows), :]


def _conv_fwd(glu, w_dw, g):
    t, c = glu.shape
    tm = _conv_tile(t)
    hb = tm // CONV_HALO

    def kern(cur_ref, halo_ref, w_ref, g_ref, cv_ref, s_ref, ext, shifted):
        i = pl.program_id(0)
        ext[0:CONV_HALO, :] = jnp.where(i > 0, halo_ref[...].astype(F32), 0.0)
        ext[CONV_HALO:tm + CONV_HALO, :] = cur_ref[...].astype(F32)
        ext[tm + CONV_HALO:, :] = jnp.zeros((8, c), F32)
        _shift_copies(ext, shifted, tm + CONV_HALO)
        gv = g_ref[...]
        for r0 in range(0, tm, CONV_ROWS):
            acc = jnp.zeros((CONV_ROWS, c), F32)
            for k in range(CONV_WIDTH):
                acc = acc + _shifted_rows(shifted, r0 + 2 + k, CONV_ROWS) * w_ref[k:k + 1, :]
            cv_ref[r0:r0 + CONV_ROWS, :] = acc
            rn = acc * _rstd(acc) * gv
            s_ref[r0:r0 + CONV_ROWS, :] = _bf(rn * _sigmoid(rn))

    row = pl.BlockSpec((tm, c), lambda i: (i, 0))
    return pl.pallas_call(
        kern, name="conv_fwd", grid=(t // tm,),
        in_specs=[row, pl.BlockSpec((CONV_HALO, c), lambda i: (jnp.maximum(i * hb - 1, 0), 0)),
                  pl.BlockSpec((CONV_HALO, c), lambda i: (0, 0)), pl.BlockSpec((1, c), lambda i: (0, 0))],
        out_specs=[row, row],
        out_shape=[jax.ShapeDtypeStruct((t, c), F32), jax.ShapeDtypeStruct((t, c), BF16)],
        scratch_shapes=[pltpu.VMEM((tm + CONV_HALO + 8, c), F32), pltpu.VMEM((8, tm + CONV_HALO, c), F32)],
        compiler_params=_params("parallel"))(glu, glu, w_dw, g.reshape(1, c))


def _conv_bwd_norm(dh, cv, w_pw2, g, dep):
    t, c = cv.shape
    tm = min(512, t)

    def kern(dh_ref, cv_ref, w_ref, g_ref, dep_ref, dcv_ref, dg_ref):
        @pl.when(pl.program_id(0) == 0)
        def _():
            dg_ref[...] = jnp.zeros_like(dg_ref)

        ds = lax.dot_general(_bf(dh_ref[...]), w_ref[...], (((1,), (1,)), ((), ())), preferred_element_type=F32)
        x = cv_ref[...]
        rstd = _rstd(x)
        xhat = x * rstd
        rn = xhat * g_ref[...]
        sig = _sigmoid(rn)
        drn = ds * (sig * (1.0 + rn * (1.0 - sig)))
        dg_ref[...] += jnp.broadcast_to(jnp.sum(drn * xhat, axis=0, keepdims=True), dg_ref.shape)
        dxh = drn * g_ref[...]
        dcv_ref[...] = rstd * (dxh - xhat * jnp.mean(dxh * xhat, axis=-1, keepdims=True))

    row = pl.BlockSpec((tm, c), lambda i: (i, 0))
    return pl.pallas_call(
        kern, name="conv_bwd_norm", grid=(t // tm,),
        in_specs=[pl.BlockSpec((tm, dh.shape[1]), lambda i: (i, 0)), row,
                  pl.BlockSpec(w_pw2.shape, lambda i: (0, 0)), pl.BlockSpec((1, c), lambda i: (0, 0)),
                  pl.BlockSpec((8, 128), lambda i: (0, 0))],
        out_specs=[row, pl.BlockSpec((8, c), lambda i: (0, 0))],
        out_shape=[jax.ShapeDtypeStruct((t, c), F32), jax.ShapeDtypeStruct((8, c), F32)],
        compiler_params=_params("arbitrary"))(dh, cv, w_pw2, g.reshape(1, c), dep)


def _conv_bwd_dw(dcv, glu, a, b, w_dw):
    t, c = dcv.shape
    tm = _conv_tile(t)
    hb = tm // CONV_HALO
    last = t // CONV_HALO - 1

    def kern(dcv_ref, dnext_ref, glu_ref, gprev_ref, a_ref, b_ref, w_ref, da_ref, db_ref, dw_ref,
             dext, gext, dshift, gshift):
        i = pl.program_id(0)

        @pl.when(i == 0)
        def _():
            dw_ref[...] = jnp.zeros_like(dw_ref)

        dext[0:tm, :] = dcv_ref[...]
        dext[tm:tm + CONV_HALO, :] = jnp.where(i < t // tm - 1, dnext_ref[...], 0.0)
        dext[tm + CONV_HALO:, :] = jnp.zeros((8, c), F32)
        gext[0:CONV_HALO, :] = jnp.where(i > 0, gprev_ref[...].astype(F32), 0.0)
        gext[CONV_HALO:tm + CONV_HALO, :] = glu_ref[...].astype(F32)
        gext[tm + CONV_HALO:, :] = jnp.zeros((8, c), F32)
        _shift_copies(dext, dshift, tm + CONV_HALO)
        _shift_copies(gext, gshift, tm + CONV_HALO)
        for r0 in range(0, tm, CONV_ROWS):
            acc = jnp.zeros((CONV_ROWS, c), F32)
            for k in range(CONV_WIDTH):
                acc = acc + _shifted_rows(dshift, r0 + CONV_WIDTH - 1 - k, CONV_ROWS) * w_ref[k:k + 1, :]
            av = a_ref[r0:r0 + CONV_ROWS, :].astype(F32)
            sig = _sigmoid(b_ref[r0:r0 + CONV_ROWS, :].astype(F32))
            da_ref[r0:r0 + CONV_ROWS, :] = _bf(acc * sig)
            db_ref[r0:r0 + CONV_ROWS, :] = _bf(acc * av * sig * (1.0 - sig))
        for k in range(CONV_WIDTH):
            acc = jnp.zeros((CONV_ROWS, c), F32)
            for r0 in range(0, tm, CONV_ROWS):
                acc = acc + _shifted_rows(gshift, r0 + 2 + k, CONV_ROWS) * dext[r0:r0 + CONV_ROWS, :]
            dw_ref[k:k + 1, :] += jnp.sum(acc, axis=0, keepdims=True)

    row = pl.BlockSpec((tm, c), lambda i: (i, 0))
    shape = jax.ShapeDtypeStruct((t, c), BF16)
    return pl.pallas_call(
        kern, name="conv_bwd_dw", grid=(t // tm,),
        in_specs=[row, pl.BlockSpec((CONV_HALO, c), lambda i: (jnp.minimum((i + 1) * hb, last), 0)),
                  row, pl.BlockSpec((CONV_HALO, c), lambda i: (jnp.maximum(i * hb - 1, 0), 0)),
                  row, row, pl.BlockSpec((CONV_HALO, c), lambda i: (0, 0))],
        out_specs=[row, row, pl.BlockSpec((CONV_HALO, c), lambda i: (0, 0))],
        out_shape=[shape, shape, jax.ShapeDtypeStruct((CONV_HALO, c), F32)],
        scratch_shapes=[pltpu.VMEM((tm + CONV_HALO + 8, c), F32), pltpu.VMEM((tm + CONV_HALO + 8, c), F32),
                        pltpu.VMEM((8, tm + CONV_HALO, c), F32), pltpu.VMEM((8, tm + CONV_HALO, c), F32)],
        compiler_params=_params("arbitrary"))(dcv, dcv, glu, glu, a, b, w_dw)


def _rope_tables(pos):
    t = pos.shape[0]
    tm = min(512, t)
    freq = (np.float32(ROPE_THETA) ** (np.float32(-2.0) * np.arange(ROPE // 2, dtype=np.float32)
                                       / np.float32(ROPE))).astype(np.float32)
    row = np.zeros((2, 128), np.float32)
    row[0, :ROPE] = np.concatenate([freq, freq])
    row[1, :ROPE] = 1.0

    def kern(pos_ref, f_ref, c_ref, s_ref):
        ang = pos_ref[...].astype(F32) * f_ref[0:1, :]
        mask = f_ref[1:2, :]
        c_ref[...] = jnp.cos(ang) * mask
        s_ref[...] = jnp.sin(ang) * mask

    out = pl.BlockSpec((tm, 128), lambda i: (i, 0))
    shape = jax.ShapeDtypeStruct((t, 128), F32)
    return pl.pallas_call(
        kern, name="rope_tables", grid=(t // tm,),
        in_specs=[pl.BlockSpec((tm, 1), lambda i: (i, 0)), pl.BlockSpec((2, 128), lambda i: (0, 0))],
        out_specs=[out, out], out_shape=[shape, shape], compiler_params=_params("parallel"))(pos, jnp.asarray(row))


def _mla_prep(a, gq, gkv, cs_c, cs_s):
    t = a.shape[0]
    tm = min(512, t)
    kv0, r0 = Q_LORA, Q_LORA + KV_LORA

    def kern(a_ref, gq_ref, gkv_ref, c_ref, s_ref, cq_ref, ckv_ref, kr_ref):
        aq = a_ref[:, 0:kv0]
        akv = a_ref[:, kv0:r0]
        ar = a_ref[:, r0:A_PAD]
        cq_ref[...] = _bf(aq * _rstd(aq) * gq_ref[...])
        ckv_ref[...] = _bf(akv * _rstd(akv) * gkv_ref[...])
        kr_ref[...] = _bf(ar * c_ref[...] + _rot(ar) * s_ref[...])

    def row(w):
        return pl.BlockSpec((tm, w), lambda i: (i, 0))

    def vec(w):
        return pl.BlockSpec((1, w), lambda i: (0, 0))

    return pl.pallas_call(
        kern, name="mla_prep", grid=(t // tm,),
        in_specs=[row(A_PAD), vec(Q_LORA), vec(KV_LORA), row(128), row(128)],
        out_specs=[row(Q_LORA), row(KV_LORA), row(128)],
        out_shape=[jax.ShapeDtypeStruct((t, Q_LORA), BF16), jax.ShapeDtypeStruct((t, KV_LORA), BF16),
                   jax.ShapeDtypeStruct((t, 128), BF16)],
        compiler_params=_params("parallel"))(a, gq.reshape(1, -1), gkv.reshape(1, -1), cs_c, cs_s)


def _mla_prep_bwd(a, dcq, dckv, dar, gq, gkv):
    t = a.shape[0]
    tm = min(512, t)
    kv0, r0 = Q_LORA, Q_LORA + KV_LORA

    def kern(a_ref, dcq_ref, dckv_ref, dar_ref, gq_ref, gkv_ref, da_ref, dgq_ref, dgkv_ref):
        @pl.when(pl.program_id(0) == 0)
        def _():
            dgq_ref[...] = jnp.zeros_like(dgq_ref)
            dgkv_ref[...] = jnp.zeros_like(dgkv_ref)

        def back(x, dy, g_ref, dg_ref):
            rstd = _rstd(x)
            xhat = x * rstd
            dg_ref[...] += jnp.broadcast_to(jnp.sum(dy * xhat, axis=0, keepdims=True), dg_ref.shape)
            dxh = dy * g_ref[...]
            return rstd * (dxh - xhat * jnp.mean(dxh * xhat, axis=-1, keepdims=True))

        da_ref[:, 0:kv0] = _bf(back(a_ref[:, 0:kv0], dcq_ref[...], gq_ref, dgq_ref))
        da_ref[:, kv0:r0] = _bf(back(a_ref[:, kv0:r0], dckv_ref[...], gkv_ref, dgkv_ref))
        da_ref[:, r0:A_PAD] = _bf(dar_ref[...])

    def row(w):
        return pl.BlockSpec((tm, w), lambda i: (i, 0))

    def vec(r, w):
        return pl.BlockSpec((r, w), lambda i: (0, 0))

    return pl.pallas_call(
        kern, name="mla_prep_bwd", grid=(t // tm,),
        in_specs=[row(A_PAD), row(Q_LORA), row(KV_LORA), row(128), vec(1, Q_LORA), vec(1, KV_LORA)],
        out_specs=[row(A_PAD), vec(8, Q_LORA), vec(8, KV_LORA)],
        out_shape=[jax.ShapeDtypeStruct((t, A_PAD), BF16), jax.ShapeDtypeStruct((8, Q_LORA), F32),
                   jax.ShapeDtypeStruct((8, KV_LORA), F32)],
        compiler_params=_params("arbitrary"))(a, dcq, dckv, dar, gq.reshape(1, -1), gkv.reshape(1, -1))


def _mla_qkv(cq, ckv, kr, cs_c, cs_s, wuq, wukv):
    t = cq.shape[0]
    tm = min(512, t)
    kvw = NOPE + V_HEAD

    def kern(cq_ref, ckv_ref, kr_ref, c_ref, s_ref, wq_ref, wkv_ref, q_ref, k_ref, v_ref):
        r = jnp.dot(cq_ref[...], wq_ref[...], preferred_element_type=F32)
        kv = jnp.dot(ckv_ref[...], wkv_ref[...], preferred_element_type=F32)
        cv, sv, krv = c_ref[...], s_ref[...], kr_ref[...]
        for h in range(HEADS):
            xr = r[:, h * HEAD_PAD + NOPE:(h + 1) * HEAD_PAD]
            q_ref[h, :, 0:NOPE] = _bf(r[:, h * HEAD_PAD:h * HEAD_PAD + NOPE] * ATTN_SCALE)
            q_ref[h, :, NOPE:] = _bf((xr * cv + _rot(xr) * sv) * ATTN_SCALE)
            k_ref[h, :, 0:NOPE] = _bf(kv[:, h * kvw:h * kvw + NOPE])
            k_ref[h, :, NOPE:] = krv
            v_ref[h] = _bf(kv[:, h * kvw + NOPE:(h + 1) * kvw])

    def row(w):
        return pl.BlockSpec((tm, w), lambda i: (i, 0))

    def heads(w):
        return pl.BlockSpec((HEADS, tm, w), lambda i: (0, i, 0))

    return pl.pallas_call(
        kern, name="mla_qkv", grid=(t // tm,),
        in_specs=[row(Q_LORA), row(KV_LORA), row(128), row(128), row(128),
                  pl.BlockSpec(wuq.shape, lambda i: (0, 0)), pl.BlockSpec(wukv.shape, lambda i: (0, 0))],
        out_specs=[heads(HEAD_PAD), heads(HEAD_PAD), heads(V_HEAD)],
        out_shape=[jax.ShapeDtypeStruct((HEADS, t, HEAD_PAD), BF16), jax.ShapeDtypeStruct((HEADS, t, HEAD_PAD), BF16),
                   jax.ShapeDtypeStruct((HEADS, t, V_HEAD), BF16)],
        compiler_params=_params("parallel"))(cq, ckv, kr, cs_c, cs_s, wuq, wukv)


def _mla_qkv_bwd(dq, dk, dv, cs_c, cs_s, wuq, wukv):
    t = dq.shape[1]
    tm = min(256, t)
    kvw = NOPE + V_HEAD

    def kern(dq_ref, dk_ref, dv_ref, c_ref, s_ref, wq_ref, wkv_ref, dr_ref, dkv_ref, dcq_ref, dckv_ref, dar_ref):
        cv, sv = c_ref[...], s_ref[...]
        dar = jnp.zeros_like(cv)
        for h in range(HEADS):
            dqx = dq_ref[h, :, NOPE:]
            dr_ref[:, h * HEAD_PAD:h * HEAD_PAD + NOPE] = _bf(dq_ref[h, :, 0:NOPE] * ATTN_SCALE)
            dr_ref[:, h * HEAD_PAD + NOPE:(h + 1) * HEAD_PAD] = _bf((dqx * cv + _rot_t(dqx * sv)) * ATTN_SCALE)
            dkx = dk_ref[h, :, NOPE:]
            dar = dar + (dkx * cv + _rot_t(dkx * sv))
            dkv_ref[:, h * kvw:h * kvw + NOPE] = _bf(dk_ref[h, :, 0:NOPE])
            dkv_ref[:, h * kvw + NOPE:(h + 1) * kvw] = _bf(dv_ref[h])
        dar_ref[...] = dar
        dcq_ref[...] = lax.dot_general(dr_ref[...], wq_ref[...], (((1,), (1,)), ((), ())),
                                       preferred_element_type=F32)
        dckv_ref[...] = lax.dot_general(dkv_ref[...], wkv_ref[...], (((1,), (1,)), ((), ())),
                                        preferred_element_type=F32)

    def row(w):
        return pl.BlockSpec((tm, w), lambda i: (i, 0))

    def heads(w):
        return pl.BlockSpec((HEADS, tm, w), lambda i: (0, i, 0))

    return pl.pallas_call(
        kern, name="mla_qkv_bwd", grid=(t // tm,),
        in_specs=[heads(HEAD_PAD), heads(HEAD_PAD), heads(V_HEAD), row(128), row(128),
                  pl.BlockSpec(wuq.shape, lambda i: (0, 0)), pl.BlockSpec(wukv.shape, lambda i: (0, 0))],
        out_specs=[row(HEADS * HEAD_PAD), row(HEADS * kvw), row(Q_LORA), row(KV_LORA), row(128)],
        out_shape=[jax.ShapeDtypeStruct((t, HEADS * HEAD_PAD), BF16), jax.ShapeDtypeStruct((t, HEADS * kvw), BF16),
                   jax.ShapeDtypeStruct((t, Q_LORA), F32), jax.ShapeDtypeStruct((t, KV_LORA), F32),
                   jax.ShapeDtypeStruct((t, 128), F32)],
        compiler_params=_params("parallel"))(dq, dk, dv, cs_c, cs_s, wuq, wukv)


def _attn_block(t):
    return 512 if t >= 4096 else 128


def _fold_rows(x, op):
    r = x.shape[0]
    while r > 8:
        r //= 2
        x = op(x[:r], x[r:])
    return x


def _chunk_mask(bk, bq):
    kc = lax.broadcasted_iota(jnp.int32, (bk, bq), 0) // CHUNK
    qc = lax.broadcasted_iota(jnp.int32, (bk, bq), 1) // CHUNK
    return qc >= kc


def _flash_fwd(q, k, v):
    t = q.shape[1]
    bq = _attn_block(t)
    nq = t // bq

    def kern(q_ref, k_ref, v_ref, o_ref, lse_ref, s_buf, p_buf, m_ref, l_ref, acc_ref):
        i = pl.program_id(1)
        queries = (q_ref[0:bq, :], q_ref[bq:2 * bq, :])

        def block(j):
            rows = pl.ds(pl.multiple_of(j * bq, bq), bq)
            return k_ref[rows, :], v_ref[rows, :]

        def scores(kj, chain):
            return lax.dot_general(kj, queries[chain], (((1,), (1,)), ((), ())), preferred_element_type=F32)

        def softmax_block(chain, slot, vj):
            for c0 in range(0, bq, 128):
                cols = slice(c0, c0 + 128)
                s = s_buf[slot, chain, :, cols]
                m_old = m_ref[chain, 0:1, cols]
                m_new = jnp.maximum(m_old, jnp.max(s, axis=0, keepdims=True))
                alpha = jnp.exp(m_old - m_new)
                p = jnp.exp(s - m_new)
                l_ref[chain, 0:1, cols] = alpha * l_ref[chain, 0:1, cols] + jnp.sum(p, axis=0, keepdims=True)
                m_ref[chain, 0:1, cols] = m_new
                p_buf[chain, :, cols] = _bf(p)
                acc_ref[chain, :, cols] = acc_ref[chain, :, cols] * alpha
            acc_ref[chain] += lax.dot_general(vj, p_buf[chain], (((0,), (0,)), ((), ())),
                                              preferred_element_type=F32)

        m_ref[...] = jnp.full(m_ref.shape, -1e30, F32)
        l_ref[...] = jnp.zeros_like(l_ref)
        acc_ref[...] = jnp.zeros_like(acc_ref)
        mask = _chunk_mask(bq, bq)
        k0, v0 = block(2 * i)
        k1, v1 = block(2 * i + 1)
        s_buf[0, 0] = jnp.where(mask, scores(k0, 0), -1e30)
        s_buf[0, 1] = scores(k0, 1)
        s_buf[1, 1] = jnp.where(mask, scores(k1, 1), -1e30)
        softmax_block(0, 0, v0)
        softmax_block(1, 0, v0)
        softmax_block(1, 1, v1)
        kf = block(0)[0]
        s_buf[0, 0] = scores(kf, 0)
        s_buf[0, 1] = scores(kf, 1)

        def body(pair, carry):
            for cur in range(2):
                j = 2 * pair + cur
                kn = block(jnp.minimum(j + 1, jnp.maximum(2 * i - 1, 0)))[0]
                s_buf[1 - cur, 0] = scores(kn, 0)
                s_buf[1 - cur, 1] = scores(kn, 1)
                vj = block(j)[1]
                softmax_block(0, cur, vj)
                softmax_block(1, cur, vj)
            return carry

        lax.fori_loop(0, i, body, 0)
        for chain in range(2):
            l = l_ref[chain, 0:1, :]
            o_ref[chain * bq:(chain + 1) * bq, :] = _bf((acc_ref[chain] / l).T)
            lse_ref[chain] = jnp.broadcast_to(m_ref[chain, 0:1, :] + jnp.log(l), (8, bq))

    return pl.pallas_call(
        kern, name="flash_fwd", grid=(HEADS, nq // 2),
        in_specs=[pl.BlockSpec((None, 2 * bq, HEAD_PAD), lambda h, i: (h, i, 0)),
                  pl.BlockSpec((None, t, HEAD_PAD), lambda h, i: (h, 0, 0)),
                  pl.BlockSpec((None, t, V_HEAD), lambda h, i: (h, 0, 0))],
        out_specs=[pl.BlockSpec((2 * bq, V_HEAD), lambda h, i: (i, h)),
                   pl.BlockSpec((None, 2, 8, bq), lambda h, i: (h, i, 0, 0))],
        out_shape=[jax.ShapeDtypeStruct((t, HEADS * V_HEAD), BF16), jax.ShapeDtypeStruct((HEADS, nq, 8, bq), F32)],
        scratch_shapes=[pltpu.VMEM((2, 2, bq, bq), F32), pltpu.VMEM((2, bq, bq), BF16), pltpu.VMEM((2, 8, bq), F32),
                        pltpu.VMEM((2, 8, bq), F32), pltpu.VMEM((2, V_HEAD, bq), F32)],
        compiler_params=_params("parallel", "arbitrary"))(q, k, v)


def _attn_delta(do, o):
    t = do.shape[0]
    bq = _attn_block(t)

    def kern(do_ref, o_ref, d_ref):
        for h in range(HEADS):
            cols = slice(h * V_HEAD, (h + 1) * V_HEAD)
            prod = do_ref[:, cols].astype(F32) * o_ref[:, cols].astype(F32)
            d_ref[h] = jnp.broadcast_to(jnp.sum(prod.T, axis=0, keepdims=True), (8, bq))

    blk = pl.BlockSpec((bq, HEADS * V_HEAD), lambda i: (i, 0))
    return pl.pallas_call(
        kern, name="attn_delta", grid=(t // bq,), in_specs=[blk, blk],
        out_specs=pl.BlockSpec((HEADS, None, 8, bq), lambda i: (0, i, 0, 0)),
        out_shape=jax.ShapeDtypeStruct((HEADS, t // bq, 8, bq), F32),
        compiler_params=_params("parallel"))(do, o)


def _flash_bwd(q, k, v, do, lse, delta):
    t = q.shape[1]
    bq = _attn_block(t)
    nq = t // bq

    def kern(q_ref, k_ref, v_ref, do_ref, lse_ref, del_ref, dq_ref, dk_ref, dv_ref, dvt_ref):
        j = pl.program_id(1)

        @pl.when(j == 0)
        def _():
            dq_ref[...] = jnp.zeros_like(dq_ref)

        dk_ref[...] = jnp.zeros_like(dk_ref)
        dvt_ref[...] = jnp.zeros_like(dvt_ref)
        kj, vj = k_ref[...], v_ref[...]

        def step(i, masked):
            rows = pl.ds(pl.multiple_of(i * bq, bq), bq)
            qi, doi = q_ref[rows, :], do_ref[rows, :]
            st = lax.dot_general(kj, qi, (((1,), (1,)), ((), ())), preferred_element_type=F32)
            pt = jnp.exp(st - lse_ref[i][0:1, :])
            if masked:
                pt = jnp.where(_chunk_mask(bq, bq), pt, 0.0)
            dpt = lax.dot_general(vj, doi, (((1,), (1,)), ((), ())), preferred_element_type=F32)
            dst = _bf(pt * (dpt - del_ref[i][0:1, :]))
            dvt_ref[...] += lax.dot_general(doi, _bf(pt), (((0,), (1,)), ((), ())), preferred_element_type=F32)
            dk_ref[...] += jnp.dot(dst, qi, preferred_element_type=F32)
            dq_ref[rows, :] += lax.dot_general(dst, kj, (((0,), (0,)), ((), ())), preferred_element_type=F32)

        step(j, True)

        def body(pair, carry):
            step(j + 1 + 2 * pair, False)
            step(j + 2 + 2 * pair, False)
            return carry

        rest = nq - 1 - j
        lax.fori_loop(0, rest // 2, body, 0)

        @pl.when(rest % 2 == 1)
        def _():
            step(nq - 1, False)

        dv_ref[...] = dvt_ref[...].T

    stat = pl.BlockSpec((None, nq, 8, bq), lambda h, j: (h, 0, 0, 0))
    return pl.pallas_call(
        kern, name="flash_bwd", grid=(HEADS, nq),
        in_specs=[pl.BlockSpec((None, t, HEAD_PAD), lambda h, j: (h, 0, 0)),
                  pl.BlockSpec((None, bq, HEAD_PAD), lambda h, j: (h, j, 0)),
                  pl.BlockSpec((None, bq, V_HEAD), lambda h, j: (h, j, 0)),
                  pl.BlockSpec((t, V_HEAD), lambda h, j: (0, h)), stat, stat],
        out_specs=[pl.BlockSpec((None, t, HEAD_PAD), lambda h, j: (h, 0, 0)),
                   pl.BlockSpec((None, bq, HEAD_PAD), lambda h, j: (h, j, 0)),
                   pl.BlockSpec((None, bq, V_HEAD), lambda h, j: (h, j, 0))],
        out_shape=[jax.ShapeDtypeStruct((HEADS, t, HEAD_PAD), F32), jax.ShapeDtypeStruct((HEADS, t, HEAD_PAD), F32),
                   jax.ShapeDtypeStruct((HEADS, t, V_HEAD), F32)],
        scratch_shapes=[pltpu.VMEM((V_HEAD, bq), F32)],
        compiler_params=_params("parallel", "arbitrary"))(q, k, v, do, lse, delta)


def _place():
    x, y, c = lax.axis_index("x"), lax.axis_index("y"), lax.axis_index("c")
    return x, y, c, [(1 - x, y), (x, 1 - y), (1 - x, 1 - y)]


def _all_gather_rows(block, name):
    m_per, n = block.shape

    def body(x_ref, out_ref, send_sems, recv_sems, local_sem):
        x, y, c, chips = _place()
        me, sibling = (x, y, c), (x, y, 1 - c)

        def rows(px, py, pc):
            return out_ref.at[pl.ds((4 * px + 2 * py + pc) * m_per, m_per), :]

        def copy(k, blk, to, src=None):
            return pltpu.make_async_remote_copy(
                src_ref=rows(*blk) if src is None else src, dst_ref=rows(*blk), send_sem=send_sems.at[k],
                recv_sem=recv_sems.at[k], device_id=to, device_id_type=MESH)

        mine = pltpu.make_async_copy(x_ref, rows(*me), local_sem)
        mine.start()
        first = [copy(0, me, sibling, src=x_ref)]
        first += [copy(1 + j, me, (*chip, c), src=x_ref) for j, chip in enumerate(chips)]
        for cp in first:
            cp.start()
        passed = [copy(4 + j, (*chip, c), sibling) for j, chip in enumerate(chips)]
        for j, chip in enumerate(chips):
            copy(1 + j, (*chip, c), me).wait_recv()
            passed[j].start()
        copy(0, sibling, me).wait_recv()
        for j, chip in enumerate(chips):
            copy(4 + j, (*chip, 1 - c), me).wait_recv()
        for cp in first + passed:
            cp.wait_send()
        mine.wait()

    return pl.pallas_call(
        body, name=name, out_shape=jax.ShapeDtypeStruct((8 * m_per, n), block.dtype),
        in_specs=[pl.BlockSpec(memory_space=pltpu.VMEM)], out_specs=pl.BlockSpec(memory_space=pltpu.VMEM),
        scratch_shapes=[pltpu.SemaphoreType.DMA((7,)), pltpu.SemaphoreType.DMA((7,)), pltpu.SemaphoreType.DMA],
        compiler_params=pltpu.CompilerParams(vmem_limit_bytes=VMEM_LIMIT_BYTES))(block)


HBM_SPEC = pl.BlockSpec(memory_space=pltpu.HBM)
SEM_SPEC = pl.BlockSpec(memory_space=pltpu.SEMAPHORE)
DATAFLOW = pltpu.SideEffectType.DATAFLOW_SIDE_EFFECTING


def _in_hbm(a):
    return pltpu.with_memory_space_constraint(a, pltpu.HBM)


def _chip_copies(ins, lands, send_sems, recv_sems, src_slot, half=False):
    n = len(ins)
    x, y, c, chips = _place()
    me = 2 * x + y

    def ends(w, chip):
        src = ins[w].at[2 * chip[0] + chip[1]] if src_slot else ins[w]
        if not half:
            return src, lands[w].at[me]
        rows = pl.ds(pl.multiple_of(c * (src.shape[0] // 2), 16), src.shape[0] // 2)
        return src.at[rows], lands[w].at[me, rows]

    copies = []
    for w in range(n):
        for p, chip in enumerate(chips):
            src, dst = ends(w, chip)
            copies.append(pltpu.make_async_remote_copy(
                src_ref=src, dst_ref=dst, send_sem=send_sems.at[p * n + w], recv_sem=recv_sems.at[p * n + w],
                device_id=(*chip, c), device_id_type=MESH))
    return copies


def _fill_halves(lands, name):
    n = len(lands)

    def body(*refs):
        bufs = refs[n:2 * n]
        send_sems, recv_sems = refs[2 * n:]
        x, y, c, chips = _place()
        copies = []
        for w in range(n):
            hr = bufs[w].shape[1] // 2
            for p, chip in enumerate(chips):
                part = bufs[w].at[2 * chip[0] + chip[1], pl.ds(pl.multiple_of(c * hr, 16), hr)]
                copies.append(pltpu.make_async_remote_copy(
                    src_ref=part, dst_ref=part, send_sem=send_sems.at[p * n + w], recv_sem=recv_sems.at[p * n + w],
                    device_id=(x, y, 1 - c), device_id_type=MESH))
        for cp in copies:
            cp.start()
        for cp in copies:
            cp.wait_send()
        for w in range(n):
            hr = bufs[w].shape[1] // 2
            for p, chip in enumerate(chips):
                part = bufs[w].at[2 * chip[0] + chip[1], pl.ds(pl.multiple_of((1 - c) * hr, 16), hr)]
                pltpu.make_async_remote_copy(
                    src_ref=part, dst_ref=part, send_sem=send_sems.at[p * n + w], recv_sem=recv_sems.at[p * n + w],
                    device_id=(x, y, 1 - c), device_id_type=MESH).wait_recv()

    any_spec = pl.BlockSpec(memory_space=pl.ANY)
    return list(pl.pallas_call(
        body, name=name, out_shape=[jax.ShapeDtypeStruct(a.shape, a.dtype) for a in lands],
        in_specs=[any_spec] * n, out_specs=[any_spec] * n, input_output_aliases={i: i for i in range(n)},
        scratch_shapes=[pltpu.SemaphoreType.DMA((3 * n,)), pltpu.SemaphoreType.DMA((3 * n,))])(*lands))


def _exchange_start(srcs, lands, src_slot, name, dep=None, half=False):
    n = len(srcs)
    first_out = 2 * n + (dep is not None)

    def body(*refs):
        for cp in _chip_copies(refs[:n], refs[n:2 * n], refs[first_out], refs[first_out + 1], src_slot, half):
            cp.start()
        token = refs[-1]
        token[...] = jnp.zeros_like(token)

    thru = [pltpu.HBM(a.shape, a.dtype) for a in list(srcs) + list(lands)]
    res = pl.pallas_call(
        body, name=name,
        out_shape=(pltpu.SemaphoreType.DMA((3 * n,)), pltpu.SemaphoreType.DMA((3 * n,)), *thru,
                   jax.ShapeDtypeStruct((8, 128), F32)),
        in_specs=[HBM_SPEC] * (2 * n) + ([pl.BlockSpec(memory_space=pl.ANY)] if dep is not None else []),
        out_specs=(SEM_SPEC, SEM_SPEC, *[HBM_SPEC] * (2 * n), pl.BlockSpec(memory_space=pltpu.VMEM)),
        input_output_aliases={i: 2 + i for i in range(2 * n)},
        compiler_params=pltpu.CompilerParams(has_side_effects=DATAFLOW))(
            *[_in_hbm(a) for a in srcs], *[_in_hbm(a) for a in lands], *([dep] if dep is not None else []))
    return (res[0], res[1], list(res[2:2 + n]), list(res[2 + n:2 + 2 * n])), res[-1]


def _exchange_wait(flight, after, src_slot, name, half=False):
    send_sems, recv_sems, srcs, lands = flight
    n = len(srcs)

    def body(*refs):
        for cp in _chip_copies(refs[:n], refs[n:2 * n], refs[2 * n], refs[2 * n + 1], src_slot, half):
            cp.wait_send()
            cp.wait_recv()

    thru = [pltpu.HBM(a.shape, a.dtype) for a in list(srcs) + list(lands)]
    res = pl.pallas_call(
        body, name=name, out_shape=thru,
        in_specs=[HBM_SPEC] * (2 * n) + [SEM_SPEC, SEM_SPEC, pl.BlockSpec(memory_space=pl.ANY)],
        out_specs=[HBM_SPEC] * (2 * n), input_output_aliases={i: i for i in range(2 * n)},
        compiler_params=pltpu.CompilerParams(has_side_effects=DATAFLOW))(*srcs, *lands, send_sems, recv_sems, after)
    return list(res[n:])


def _landing(own, me):
    return lax.dynamic_update_index_in_dim(lax.empty((4, *own.shape), own.dtype), own, me, 0)


def _swap_with_sibling(arrays, name):
    n = len(arrays)

    def body(*refs):
        ins, outs = refs[:n], refs[n:2 * n]
        send_sems, recv_sems = refs[2 * n:]
        x, y, c, _ = _place()
        copies = [pltpu.make_async_remote_copy(src_ref=ins[w], dst_ref=outs[w], send_sem=send_sems.at[w],
                                               recv_sem=recv_sems.at[w], device_id=(x, y, 1 - c), device_id_type=MESH)
                  for w in range(n)]
        for cp in copies:
            cp.start()
        for cp in copies:
            cp.wait()

    any_spec = pl.BlockSpec(memory_space=pl.ANY)
    return pl.pallas_call(
        body, name=name, out_shape=[jax.ShapeDtypeStruct(a.shape, a.dtype) for a in arrays],
        in_specs=[any_spec] * n, out_specs=[any_spec] * n,
        scratch_shapes=[pltpu.SemaphoreType.DMA((n,)), pltpu.SemaphoreType.DMA((n,))])(*arrays)


def _as_rows(a):
    return a.reshape(-1, a.shape[-1])


def _row_tile(r, c, budget_bytes=1 << 20):
    tr = r
    while tr % 16 == 0 and tr * c * 4 > budget_bytes:
        tr //= 2
    return tr


def _sum_slots(layers, nlayer, name, into=None):
    _, r, c = layers[0][1].shape
    tr = _row_tile(r, c)
    nt = r // tr
    acc = into
    for l, r4 in layers:
        def kern(r_ref, *rest):
            o_ref = rest[-1]
            o_ref[...] = (((r_ref[0].astype(F32) + r_ref[1].astype(F32)) + r_ref[2].astype(F32))
                          + r_ref[3].astype(F32))

        out_spec = pl.BlockSpec((tr, c), lambda i, l=l: (l * nt + i, 0))
        first = acc is None
        acc = pl.pallas_call(
            kern, name=f"{name}_l{l}", grid=(nt,),
            in_specs=[pl.BlockSpec((4, tr, c), lambda i: (0, i, 0))]
            + ([] if first else [pl.BlockSpec(memory_space=pl.ANY)]),
            out_specs=out_spec, out_shape=jax.ShapeDtypeStruct((nlayer * r, c), F32),
            input_output_aliases={} if first else {1: 0},
            compiler_params=_params("parallel"))(*([r4] if first else [r4, acc]))
    return acc


def _adamw(w, m, v, parts, name):
    r, c = w.shape
    tr = _row_tile(r, c, 3 << 19)
    npart = len(parts)
    c1 = 1.0 - ADAM_B1 ** ADAM_STEP
    c2 = 1.0 - ADAM_B2 ** ADAM_STEP

    def kern(*refs):
        w_ref, m_ref, v_ref = refs[:3]
        p_refs = refs[3:3 + npart]
        g_ref, d_ref, mo_ref, vo_ref = refs[3 + npart:]
        g = p_refs[0][...]
        for p in p_refs[1:]:
            g = g + p[...]
        mn = ADAM_B1 * m_ref[...] + (1.0 - ADAM_B1) * g
        vn = ADAM_B2 * v_ref[...] + (1.0 - ADAM_B2) * (g * g)
        g_ref[...] = g
        mo_ref[...] = mn
        vo_ref[...] = vn
        d_ref[...] = -ADAM_LR * ((mn / c1) / (jnp.sqrt(vn / c2) + ADAM_EPS) + ADAM_WD * w_ref[...])

    blk = pl.BlockSpec((tr, c), lambda i: (i, 0))
    shape = jax.ShapeDtypeStruct((r, c), F32)
    return pl.pallas_call(
        kern, name=name, grid=(r // tr,), in_specs=[blk] * (3 + npart), out_specs=[blk] * 4, out_shape=[shape] * 4,
        compiler_params=_params("parallel"))(w, m, v, *parts)


def _sum_devices(g8, name):
    _, r, c = g8.shape

    def kern(g_ref, o_ref):
        tot = g_ref[0]
        for dev in range(1, 8):
            tot = tot + g_ref[dev]
        o_ref[...] = tot

    return pl.pallas_call(
        kern, name=name, grid=(1,), in_specs=[pl.BlockSpec((8, r, c), lambda i: (0, 0, 0))],
        out_specs=pl.BlockSpec((r, c), lambda i: (0, 0)), out_shape=jax.ShapeDtypeStruct((r, c), F32),
        compiler_params=_params("arbitrary"))(g8)


def _pad_lanes(a, width):
    return jnp.pad(a, [(0, 0)] * (a.ndim - 1) + [(0, width - a.shape[-1])])


def kernel(x, positions, ffn_norm1, ffn1_w1, ffn1_w3, ffn1_w2, mix_norm, ffn_norm2, ffn2_w1, ffn2_w3, ffn2_w2, conv_w_pw1, conv_w_dw, conv_norm, conv_w_pw2, mla_w_a, mla_q_norm, mla_kv_norm, mla_w_uq, mla_w_ukv, mla_w_o, final_norm, loss_target, m_ffn_norm1, m_ffn1_w1, m_ffn1_w3, m_ffn1_w2, m_mix_norm, m_ffn_norm2, m_ffn2_w1, m_ffn2_w3, m_ffn2_w2, m_conv_w_pw1, m_conv_w_dw, m_conv_norm, m_conv_w_pw2, m_mla_w_a, m_mla_q_norm, m_mla_kv_norm, m_mla_w_uq, m_mla_w_ukv, m_mla_w_o, m_final_norm, v_ffn_norm1, v_ffn1_w1, v_ffn1_w3, v_ffn1_w2, v_mix_norm, v_ffn_norm2, v_ffn2_w1, v_ffn2_w3, v_ffn2_w2, v_conv_w_pw1, v_conv_w_dw, v_conv_norm, v_conv_w_pw2, v_mla_w_a, v_mla_q_norm, v_mla_kv_norm, v_mla_w_uq, v_mla_w_ukv, v_mla_w_o, v_final_norm):
    given = locals()
    return _step({nm: given[nm] for nm in INPUTS})


def _step(A):
    x = A['x'][0]
    target = A['loss_target'][0]
    t, d = x.shape
    pos = A['positions'].reshape(t, 1)
    me = 2 * lax.axis_index("x") + lax.axis_index("y")

    flipped = {f'ffn{k}_{w}' for k in (1, 2) for w in ('w1', 'w3')}
    P = {}
    for nm in BIG:
        for key in (nm, 'm_' + nm, 'v_' + nm):
            P[key] = jnp.swapaxes(A[key], 1, 2) if nm in flipped else A[key]

    def unflip(nm, a):
        return jnp.swapaxes(a, 1, 2) if nm in flipped else a

    ffn = [f'ffn{k}_{w}' for k in (1, 2) for w in ('w1', 'w3', 'w2')]
    gather_groups = [[(nm, 0) for nm in ffn[:3]],
                     [('conv_w_pw1', 0), ('conv_w_pw2', 0)] + [(nm, 0) for nm in ffn[3:]],
                     [(nm, 1) for nm in ffn[:3]] + [('mla_w_a', 0), ('mla_w_uq', 0), ('mla_w_ukv', 0), ('mla_w_o', 0)],
                     [(nm, 1) for nm in ffn[3:]]]
    halved = (0, 1)
    gather_flights = {}
    big = {}

    def gather_start(gi, dep):
        shards = [_bf(P[nm][l]) for nm, l in gather_groups[gi]]
        gather_flights[gi], token = _exchange_start(shards, [_landing(s, me) for s in shards], False,
                                                    f"gather_start_{gi}", dep, half=gi in halved)
        return token

    def gather_wait(gi, after):
        landed = _exchange_wait(gather_flights[gi], after, False, f"gather_wait_{gi}", half=gi in halved)
        if gi in halved:
            landed = _fill_halves(landed, f"gather_fill_{gi}")
        big.update(zip(gather_groups[gi], landed))
        return landed[0]

    dw_shard = A['conv_w_dw'][0]
    cw = dw_shard.shape[1]
    small = jnp.concatenate([
        jnp.pad(dw_shard, ((0, CONV_HALO - CONV_WIDTH), (0, 0))),
        jnp.pad(_pad_lanes(A['mla_q_norm'], cw), ((0, 7), (0, 0))),
        jnp.pad(_pad_lanes(A['mla_kv_norm'], cw), ((0, 7), (0, 0)))], axis=0)
    small = _all_gather_rows(small, "gather_small_weights").reshape(4, 2, 48, cw)[:, 0]
    w_dw = jnp.concatenate([small[j, :CONV_HALO] for j in range(4)], axis=1)
    gq = jnp.concatenate([small[j, CONV_HALO, :Q_LORA // 4] for j in range(4)])
    gkv = jnp.concatenate([small[j, CONV_HALO + 8, :KV_LORA // 4] for j in range(4)])

    def rows(nm, layer):
        g = big[nm, layer]
        return g.reshape(-1, g.shape[-1])

    ffn_w = {}

    def ffn_weights(k, l):
        ffn_w[k, l] = (rows(f'ffn{k}_w1', l), rows(f'ffn{k}_w3', l), rows(f'ffn{k}_w2', l))
        return ffn_w[k, l]

    token = gather_start(0, small)
    cs_c, cs_s = _rope_tables(pos)
    h0 = x
    token = gather_start(1, gather_wait(0, token))
    h1, n01, z01a, z01b = _ffn_fwd(h0, A['ffn_norm1'][0], *ffn_weights(1, 0), token, "ffn1_l0_fwd")
    token = gather_start(3, gather_start(2, gather_wait(1, h1)))
    pw1 = big['conv_w_pw1', 0]
    pw1_a = jnp.concatenate([pw1[0], pw1[1]], axis=1)
    pw1_b = jnp.concatenate([pw1[2], pw1[3]], axis=1)
    pw2 = rows('conv_w_pw2', 0)
    m0 = _norm_fwd(h1, A['mix_norm'][0], token, "mix_norm_l0")
    ca, cb, glu = _glu_fwd(m0, pw1_a, pw1_b)
    cv, cs = _conv_fwd(glu, w_dw, A['conv_norm'][0])
    h2 = _mm([(cs, pw2)], F32, "conv_pw2_fwd", res=h1)
    h3, n02, z02a, z02b = _ffn_fwd(h2, A['ffn_norm2'][0], *ffn_weights(2, 0), token, "ffn2_l0_fwd")
    gather_wait(2, h3)
    w_a = _pad_lanes(rows('mla_w_a', 0), A_PAD)
    wuq = _pad_lanes(big['mla_w_uq', 0].reshape(Q_LORA, HEADS, NOPE + ROPE), HEAD_PAD).reshape(Q_LORA, -1)
    wukv = big['mla_w_ukv', 0].reshape(KV_LORA, HEADS * (NOPE + V_HEAD))
    w_o = rows('mla_w_o', 0)
    h4, n11, z11a, z11b = _ffn_fwd(h3, A['ffn_norm1'][1], *ffn_weights(1, 1), token, "ffn1_l1_fwd")
    m1 = _norm_fwd(h4, A['mix_norm'][1], token, "mix_norm_l1")
    a_lat = _mm([(m1, w_a)], F32, "mla_down_fwd")
    cq, ckv, kr = _mla_prep(a_lat, gq, gkv, cs_c, cs_s)
    q, k, v = _mla_qkv(cq, ckv, kr, cs_c, cs_s, wuq, wukv)
    o, lse = _flash_fwd(q, k, v)
    h5 = _mm([(o, w_o)], F32, "mla_out_fwd", res=h4)
    gather_wait(3, h5)
    h6, n12, z12a, z12b = _ffn_fwd(h5, A['ffn_norm2'][1], *ffn_weights(2, 1), token, "ffn2_l1_fwd")

    def row_slots(g):
        return g.reshape(4, g.shape[0] // 4, g.shape[1])

    scatter_flights = []

    def scatter_start(named):
        srcs = [g for _, g in named]
        lands = [_landing(lax.dynamic_index_in_dim(g, me, 0, keepdims=False), me) for g in srcs]
        flight, token = _exchange_start(srcs, lands, True, f"scatter_start_{len(scatter_flights)}")
        scatter_flights.append(([key for key, _ in named], flight))
        return token

    def send_ffn(k, l, dw1t, dw3t, dw2):
        return scatter_start([((f'ffn{k}_w1', l), row_slots(dw1t)), ((f'ffn{k}_w3', l), row_slots(dw3t)),
                              ((f'ffn{k}_w2', l), row_slots(dw2))])

    dh6, dg_final, loss_part = _loss_bwd(h6, target, A['final_norm'])
    dh5, dg_n2_l1, *dws = _ffn_bwd(dh6, h5, A['ffn_norm2'][1], n12, z12a, z12b, *ffn_w[2, 1], loss_part, "ffn2_l1")
    token = send_ffn(2, 1, *dws)

    do = _mm([(dh5, w_o)], BF16, "mla_out_bwd", trans_b=True, dep=token)
    dw_o = _mm_tn(o, dh5, BF16, "mla_dw_o")
    delta = _attn_delta(do, o)
    dq, dk, dv = _flash_bwd(q, k, v, do, lse, delta)
    dr, dkv, dcq, dckv, dar = _mla_qkv_bwd(dq, dk, dv, cs_c, cs_s, wuq, wukv)
    dwuq = _mm_tn(cq, dr, BF16, "mla_dw_uq", bn=dr.shape[1] // 2)
    dwukv = _mm_tn(ckv, dkv, BF16, "mla_dw_ukv", bn=dkv.shape[1] // 2)
    da_lat, dgq, dgkv = _mla_prep_bwd(a_lat, dcq, dckv, dar, gq, gkv)
    dw_a = _mm_tn(m1, da_lat, BF16, "mla_dw_a")
    token = scatter_start([
        (('mla_w_a', 0), row_slots(dw_a[:, :Q_LORA + KV_LORA + ROPE])),
        (('mla_w_uq', 0), dwuq.reshape(4, Q_LORA // 4, HEADS, HEAD_PAD)[..., :NOPE + ROPE]),
        (('mla_w_ukv', 0), dwukv.reshape(4, KV_LORA // 4, HEADS, NOPE + V_HEAD)),
        (('mla_w_o', 0), row_slots(dw_o))])
    dh4, dg_mix_l1 = _mm_normbwd([(da_lat, w_a)], h4, A['mix_norm'][1], dh5, token, "mla_down_bwd")

    dh3, dg_n1_l1, *dws = _ffn_bwd(dh4, h3, A['ffn_norm1'][1], n11, z11a, z11b, *ffn_w[1, 1], token, "ffn1_l1")
    token = send_ffn(1, 1, *dws)
    dh2, dg_n2_l0, *dws = _ffn_bwd(dh3, h2, A['ffn_norm2'][0], n02, z02a, z02b, *ffn_w[2, 0], token, "ffn2_l0")
    token = send_ffn(2, 0, *dws)

    dcv, dg_conv = _conv_bwd_norm(dh2, cv, pw2, A['conv_norm'][0], token)
    dw_pw2 = _mm_tn(cs, dh2, BF16, "conv_dw_pw2")
    dca, dcb, ddw = _conv_bwd_dw(dcv, glu, ca, cb, w_dw)
    dpw1_a = _mm_tn(m0, dca, BF16, "conv_dw_pw1a")
    dpw1_b = _mm_tn(m0, dcb, BF16, "conv_dw_pw1b")
    half = dpw1_a.shape[1] // 2
    token = scatter_start([
        (('conv_w_pw1', 0), jnp.stack([dpw1_a[:, :half], dpw1_a[:, half:], dpw1_b[:, :half], dpw1_b[:, half:]])),
        (('conv_w_pw2', 0), row_slots(dw_pw2))])
    dh1, dg_mix_l0 = _mm_normbwd([(dca, pw1_a), (dcb, pw1_b)], h1, A['mix_norm'][0], dh2, token, "conv_pw1_bwd")

    dx, dg_n1_l0, *dws = _ffn_bwd(dh1, h0, A['ffn_norm1'][0], n01, z01a, z01b, *ffn_w[1, 0], token, "ffn1_l0")
    last_sent = send_ffn(1, 0, *dws)
    out = {}

    qkv_row = jnp.concatenate([dgq, dgkv, jnp.zeros((8, d - Q_LORA - KV_LORA), F32)], axis=1)
    loss_row = _pad_lanes(loss_part, d)
    small_g = jnp.concatenate([dg_n1_l0, dg_n1_l1, dg_mix_l0, dg_mix_l1, dg_n2_l0, dg_n2_l1, dg_conv, dg_final,
                               qkv_row, loss_row, ddw], axis=0)
    nrow = small_g.shape[0]
    tot = _sum_devices(_all_gather_rows(small_g, "gather_small_grads").reshape(8, nrow, d), "sum_small_grads")
    loss = tot[72, 0]
    q_shard = lax.dynamic_slice_in_dim(tot[64, :Q_LORA], me * (Q_LORA // 4), Q_LORA // 4)
    kv_shard = lax.dynamic_slice_in_dim(tot[64, Q_LORA:Q_LORA + KV_LORA], me * (KV_LORA // 4), KV_LORA // 4)
    dw_shard_g = lax.dynamic_slice_in_dim(tot[80:80 + CONV_WIDTH], me * cw, cw, axis=1)
    small_grads = {
        'ffn_norm1': jnp.stack([tot[0], tot[8]]), 'mix_norm': jnp.stack([tot[16], tot[24]]),
        'ffn_norm2': jnp.stack([tot[32], tot[40]]), 'conv_norm': tot[48][None], 'final_norm': tot[56],
        'mla_q_norm': q_shard[None], 'mla_kv_norm': kv_shard[None], 'conv_w_dw': dw_shard_g[None],
    }
    for nm, g in small_grads.items():
        res = _adamw(_as_rows(A[nm]) if A[nm].ndim > 1 else A[nm].reshape(1, -1),
                     A['m_' + nm].reshape(-1, A[nm].shape[-1]), A['v_' + nm].reshape(-1, A[nm].shape[-1]),
                     [g.reshape(-1, A[nm].shape[-1])], "adamw_" + nm)
        out[nm] = [r.reshape(A[nm].shape) for r in res]

    received = {}
    after = last_sent

    def scatter_wait(si, after):
        keys, flight = scatter_flights[si]
        landed = _exchange_wait(flight, after, True, f"scatter_wait_{si}")
        received.update(zip(keys, landed))
        return landed[0]

    def slots(nm, l):
        return received[nm, l].reshape(4, -1, received[nm, l].shape[-1])

    def finish(names, sums, tag):
        for nm, mine, theirs in zip(names, sums, _swap_with_sibling(sums, "swap_with_sibling_" + tag)):
            res = _adamw(_as_rows(P[nm]), _as_rows(P['m_' + nm]), _as_rows(P['v_' + nm]), [mine, theirs],
                         "adamw_" + nm)
            out[nm] = [unflip(nm, r.reshape(P[nm].shape)) for r in res]
        return res[1]

    last = len(scatter_flights) - 1
    for si in range(last):
        after = scatter_wait(si, after)
    late = ffn[:3]
    early = [nm for nm in BIG if nm not in late]
    late_l1 = [_sum_slots([(1, slots(nm, 1))], 2, "sum_" + nm) for nm in late]
    after = finish(early, [_sum_slots([(l, slots(nm, l)) for l in range(A[nm].shape[0])], A[nm].shape[0],
                                      "sum_" + nm) for nm in early], "early")
    scatter_wait(last, after)
    finish(late, [_sum_slots([(0, slots(nm, 0))], 2, "sum_" + nm, into=part) for nm, part in zip(late, late_l1)],
           "late")

    return (loss, dx[None], *[out[nm][0] for nm in WEIGHTS], *[out[nm][1] for nm in WEIGHTS],
            *[out[nm][2] for nm in WEIGHTS], *[out[nm][3] for nm in WEIGHTS])
```

```python
import functools

import jax
import jax.numpy as jnp
import numpy as np
from jax import lax
from jax.experimental import pallas as pl
from jax.experimental.pallas import tpu as pltpu

F32 = jnp.float32
BF16 = jnp.bfloat16
MESH = pl.DeviceIdType.MESH

RMS_EPS = 1e-6
HEADS = 8
NOPE = 128
ROPE = 64
HEAD_PAD = 256
V_HEAD = 128
Q_LORA = 512
KV_LORA = 256
A_PAD = 896
CHUNK = 64
CONV_WIDTH = 31
CONV_HALO = 32
CONV_ROWS = 16
ROPE_THETA = 10000.0
ATTN_SCALE = (NOPE + ROPE) ** -0.5
FFN_RES = 0.5

ADAM_LR = 0.001
ADAM_B1 = 0.9
ADAM_B2 = 0.999
ADAM_EPS = 1e-08
ADAM_WD = 0.01
ADAM_STEP = 10

VMEM_LIMIT_BYTES = 56 * 1024 * 1024

WEIGHTS = ['ffn_norm1', 'ffn1_w1', 'ffn1_w3', 'ffn1_w2', 'mix_norm', 'ffn_norm2', 'ffn2_w1', 'ffn2_w3', 'ffn2_w2',
           'conv_w_pw1', 'conv_w_dw', 'conv_norm', 'conv_w_pw2', 'mla_w_a', 'mla_q_norm', 'mla_kv_norm', 'mla_w_uq',
           'mla_w_ukv', 'mla_w_o', 'final_norm']
INPUTS = (['x', 'positions'] + WEIGHTS + ['loss_target'] + ['m_' + w for w in WEIGHTS] + ['v_' + w for w in WEIGHTS])
BIG = ['ffn1_w1', 'ffn1_w3', 'ffn1_w2', 'ffn2_w1', 'ffn2_w3', 'ffn2_w2', 'conv_w_pw1', 'conv_w_pw2', 'mla_w_a',
       'mla_w_uq', 'mla_w_ukv', 'mla_w_o']


def _params(*sem):
    return pltpu.CompilerParams(dimension_semantics=sem, vmem_limit_bytes=VMEM_LIMIT_BYTES)


def _bf(v):
    return v.astype(BF16)


def _rstd(x):
    return lax.rsqrt(jnp.mean(x * x, axis=-1, keepdims=True) + RMS_EPS)


def _sigmoid(x):
    return jax.nn.sigmoid(x)


def _rot(x):
    lane = lax.broadcasted_iota(jnp.int32, x.shape, 1)
    return jnp.where(lane < ROPE // 2, -pltpu.roll(x, 128 - ROPE // 2, 1), pltpu.roll(x, ROPE // 2, 1))


def _rot_t(y):
    lane = lax.broadcasted_iota(jnp.int32, y.shape, 1)
    return jnp.where(lane < ROPE // 2, pltpu.roll(y, 128 - ROPE // 2, 1), -pltpu.roll(y, ROPE // 2, 1))


def _pair_sum(a_refs, b_refs, trans_b):
    tot = None
    for a_r, b_r in zip(a_refs, b_refs):
        a, b = _bf(a_r[...]), _bf(b_r[...])
        if trans_b:
            d = lax.dot_general(a, b, (((1,), (1,)), ((), ())), preferred_element_type=F32)
        else:
            d = jnp.dot(a, b, preferred_element_type=F32)
        tot = d if tot is None else tot + d
    return tot


def _mm(pairs, out_dtype, name, *, trans_b=False, tm=512, tn=None, tk=None, res=None, dep=None):
    m, k = pairs[0][0].shape
    n = pairs[0][1].shape[0] if trans_b else pairs[0][1].shape[1]
    tm, tn, tk = min(tm, m), tn or n, tk or k
    nk, npair = k // tk, len(pairs)

    def kern(*refs):
        a_refs, b_refs = refs[:npair], refs[npair:2 * npair]
        rest = list(refs[2 * npair:])
        res_ref = rest.pop(0) if res is not None else None
        if dep is not None:
            rest.pop(0)
        o_ref = rest.pop(0)

        def finish(acc):
            if res_ref is not None:
                acc = res_ref[...] + acc
            o_ref[...] = acc.astype(o_ref.dtype)

        if nk == 1:
            finish(_pair_sum(a_refs, b_refs, trans_b))
        else:
            acc_ref = rest.pop(0)
            kk = pl.program_id(2)

            @pl.when(kk == 0)
            def _():
                acc_ref[...] = jnp.zeros_like(acc_ref)

            acc_ref[...] += _pair_sum(a_refs, b_refs, trans_b)

            @pl.when(kk == nk - 1)
            def _():
                finish(acc_ref[...])

    a_spec = pl.BlockSpec((tm, tk), lambda i, j, kk: (i, kk))
    b_spec = (pl.BlockSpec((tn, tk), lambda i, j, kk: (j, kk)) if trans_b
              else pl.BlockSpec((tk, tn), lambda i, j, kk: (kk, j)))
    io_spec = pl.BlockSpec((tm, tn), lambda i, j, kk: (i, j))
    in_specs = ([a_spec] * npair + [b_spec] * npair + ([io_spec] if res is not None else [])
                + ([pl.BlockSpec((8, 128), lambda i, j, kk: (0, 0))] if dep is not None else []))
    args = ([p[0] for p in pairs] + [p[1] for p in pairs] + ([res] if res is not None else [])
            + ([dep] if dep is not None else []))
    return pl.pallas_call(
        kern, name=name, grid=(m // tm, n // tn, nk), in_specs=in_specs, out_specs=io_spec,
        out_shape=jax.ShapeDtypeStruct((m, n), out_dtype),
        scratch_shapes=[pltpu.VMEM((tm, tn), F32)] if nk > 1 else [],
        compiler_params=_params("parallel", "parallel", "arbitrary"))(*args)


def _mm_normbwd(pairs, h, g, dres, dep, name, *, tm=512, tk=None):
    m, k = pairs[0][0].shape
    d = pairs[0][1].shape[0]
    tm, tk = min(tm, m), tk or k
    nk, npair = k // tk, len(pairs)

    def kern(*refs):
        a_refs, b_refs = refs[:npair], refs[npair:2 * npair]
        h_ref, g_ref, dres_ref, _, o_ref, dg_ref, acc_ref = refs[2 * npair:]
        i, kk = pl.program_id(0), pl.program_id(1)

        @pl.when(jnp.logical_and(i == 0, kk == 0))
        def _():
            dg_ref[...] = jnp.zeros_like(dg_ref)

        @pl.when(kk == 0)
        def _():
            acc_ref[...] = jnp.zeros_like(acc_ref)

        acc_ref[...] += _pair_sum(a_refs, b_refs, True)

        @pl.when(kk == nk - 1)
        def _():
            dn = acc_ref[...]
            x = h_ref[...]
            rstd = _rstd(x)
            xhat = x * rstd
            dg_ref[...] += jnp.broadcast_to(jnp.sum(dn * xhat, axis=0, keepdims=True), dg_ref.shape)
            dxh = dn * g_ref[...]
            dx = rstd * (dxh - xhat * jnp.mean(dxh * xhat, axis=-1, keepdims=True))
            o_ref[...] = dres_ref[...] + dx

    row = pl.BlockSpec((tm, d), lambda i, kk: (i, 0))
    in_specs = ([pl.BlockSpec((tm, tk), lambda i, kk: (i, kk))] * npair
                + [pl.BlockSpec((d, tk), lambda i, kk: (0, kk))] * npair
                + [row, pl.BlockSpec((1, d), lambda i, kk: (0, 0)), row, pl.BlockSpec((8, 128), lambda i, kk: (0, 0))])
    return pl.pallas_call(
        kern, name=name, grid=(m // tm, nk), in_specs=in_specs,
        out_specs=[row, pl.BlockSpec((8, d), lambda i, kk: (0, 0))],
        out_shape=[jax.ShapeDtypeStruct((m, d), F32), jax.ShapeDtypeStruct((8, d), F32)],
        scratch_shapes=[pltpu.VMEM((tm, d), F32)],
        compiler_params=_params("arbitrary", "arbitrary"))(
            *[p[0] for p in pairs], *[p[1] for p in pairs], h, g.reshape(1, d), dres, dep)


def _mm_tn(a, b, out_dtype, name, *, bm=None, bn=None, tk=1024):
    t, m = a.shape
    batched = b.ndim == 3
    n = b.shape[-1]
    nb = b.shape[0] if batched else 1
    bm, bn, tk = bm or m, bn or n, min(tk, t)
    nk = t // tk

    def kern(a_ref, b_ref, o_ref, acc_ref):
        kk = pl.program_id(3)

        @pl.when(kk == 0)
        def _():
            acc_ref[...] = jnp.zeros_like(acc_ref)

        acc_ref[...] += lax.dot_general(_bf(a_ref[...]), _bf(b_ref[...]), (((0,), (0,)), ((), ())),
                                        preferred_element_type=F32)

        @pl.when(kk == nk - 1)
        def _():
            o_ref[...] = acc_ref[...].astype(o_ref.dtype)

    a_spec = pl.BlockSpec((tk, bm), lambda h, i, j, kk: (kk, i))
    if batched:
        b_spec = pl.BlockSpec((None, tk, bn), lambda h, i, j, kk: (h, kk, j))
        o_spec = pl.BlockSpec((None, bm, bn), lambda h, i, j, kk: (h, i, j))
        out_shape = jax.ShapeDtypeStruct((nb, m, n), out_dtype)
    else:
        b_spec = pl.BlockSpec((tk, bn), lambda h, i, j, kk: (kk, j))
        o_spec = pl.BlockSpec((bm, bn), lambda h, i, j, kk: (i, j))
        out_shape = jax.ShapeDtypeStruct((m, n), out_dtype)
    return pl.pallas_call(
        kern, name=name, grid=(nb, m // bm, n // bn, nk), in_specs=[a_spec, b_spec], out_specs=o_spec,
        out_shape=out_shape, scratch_shapes=[pltpu.VMEM((bm, bn), F32)],
        compiler_params=_params("parallel", "parallel", "parallel", "arbitrary"))(a, b)


def _ffn_tile(f):
    return f // 2 if (f // 2) % 128 == 0 else f


def _ffn_fwd(h, g, w1t, w3t, w2, dep, name):
    t, d = h.shape
    f = w1t.shape[0]
    tm = min(256, t)
    nt = (((1,), (1,)), ((), ()))

    def kern(h_ref, g_ref, w1_hbm, w3_hbm, w2_hbm, dep_ref, ho_ref, n_ref, z1_ref, z3_ref,
             w1_ref, w3_ref, w2_ref, sems):
        @pl.when(pl.program_id(0) == 0)
        def _():
            copies = [pltpu.make_async_copy(src, dst, sems.at[k]) for k, (src, dst) in
                      enumerate(((w1_hbm, w1_ref), (w3_hbm, w3_ref), (w2_hbm, w2_ref)))]
            for cp in copies:
                cp.start()
            for cp in copies:
                cp.wait()

        x = h_ref[...]
        n = _bf(x * _rstd(x) * g_ref[...])
        n_ref[...] = n
        z1 = lax.dot_general(n, w1_ref[...], nt, preferred_element_type=F32)
        z3 = lax.dot_general(n, w3_ref[...], nt, preferred_element_type=F32)
        z1_ref[...] = _bf(z1)
        z3_ref[...] = _bf(z3)
        act = _bf(z1 * _sigmoid(z1) * z3)
        ho_ref[...] = x + FFN_RES * jnp.dot(act, w2_ref[...], preferred_element_type=F32)

    row = pl.BlockSpec((tm, d), lambda i: (i, 0))
    col = pl.BlockSpec((tm, f), lambda i: (i, 0))
    whole = pl.BlockSpec(memory_space=pl.ANY)
    return pl.pallas_call(
        kern, name=name, grid=(t // tm,),
        in_specs=[row, pl.BlockSpec((1, d), lambda i: (0, 0)), whole, whole, whole,
                  pl.BlockSpec((8, 128), lambda i: (0, 0))],
        out_specs=[row, row, col, col],
        out_shape=[jax.ShapeDtypeStruct((t, d), F32), jax.ShapeDtypeStruct((t, d), BF16),
                   jax.ShapeDtypeStruct((t, f), BF16), jax.ShapeDtypeStruct((t, f), BF16)],
        scratch_shapes=[pltpu.VMEM((f, d), BF16), pltpu.VMEM((f, d), BF16), pltpu.VMEM((f, d), BF16),
                        pltpu.SemaphoreType.DMA((3,))],
        compiler_params=_params("arbitrary"))(h, g.reshape(1, d), w1t, w3t, w2, dep)


def _ffn_bwd_x(dh, h_in, g, z1, z3, w1t, w3t, w2, dep, name):
    t, d = dh.shape
    f = z1.shape[1]
    tm = min(256, t)

    def kern(dh_ref, h_ref, g_ref, z1_ref, z3_ref, w2_hbm, w1_hbm, w3_hbm, dep_ref,
             o_ref, dg_ref, dz1_ref, dz3_ref, a_ref, df_ref, w2_ref, w1_ref, w3_ref, sems):
        @pl.when(pl.program_id(0) == 0)
        def _():
            copies = [pltpu.make_async_copy(src, dst, sems.at[k]) for k, (src, dst) in
                      enumerate(((w2_hbm, w2_ref), (w1_hbm, w1_ref), (w3_hbm, w3_ref)))]
            for cp in copies:
                cp.start()
            dg_ref[...] = jnp.zeros_like(dg_ref)
            for cp in copies:
                cp.wait()

        df = _bf(FFN_RES * dh_ref[...])
        df_ref[...] = df
        da = lax.dot_general(df, w2_ref[...], (((1,), (1,)), ((), ())), preferred_element_type=F32)
        z1v, z3v = z1_ref[...].astype(F32), z3_ref[...].astype(F32)
        sig = _sigmoid(z1v)
        silu = z1v * sig
        a_ref[...] = _bf(silu * z3v)
        dz1 = _bf(da * z3v * (sig * (1.0 + z1v * (1.0 - sig))))
        dz3 = _bf(da * silu)
        dz1_ref[...] = dz1
        dz3_ref[...] = dz3
        dn = (jnp.dot(dz1, w1_ref[...], preferred_element_type=F32)
              + jnp.dot(dz3, w3_ref[...], preferred_element_type=F32))
        x = h_ref[...]
        rstd = _rstd(x)
        xhat = x * rstd
        dg_ref[...] += jnp.broadcast_to(jnp.sum(dn * xhat, axis=0, keepdims=True), dg_ref.shape)
        dxh = dn * g_ref[...]
        o_ref[...] = dh_ref[...] + rstd * (dxh - xhat * jnp.mean(dxh * xhat, axis=-1, keepdims=True))

    row = pl.BlockSpec((tm, d), lambda i: (i, 0))
    col = pl.BlockSpec((tm, f), lambda i: (i, 0))
    whole = pl.BlockSpec(memory_space=pl.ANY)
    colshape = jax.ShapeDtypeStruct((t, f), BF16)
    return pl.pallas_call(
        kern, name=name, grid=(t // tm,),
        in_specs=[row, row, pl.BlockSpec((1, d), lambda i: (0, 0)), col, col, whole, whole, whole,
                  pl.BlockSpec((8, 128), lambda i: (0, 0))],
        out_specs=[row, pl.BlockSpec((8, d), lambda i: (0, 0)), col, col, col, row],
        out_shape=[jax.ShapeDtypeStruct((t, d), F32), jax.ShapeDtypeStruct((8, d), F32), colshape, colshape, colshape,
                   jax.ShapeDtypeStruct((t, d), BF16)],
        scratch_shapes=[pltpu.VMEM((f, d), BF16), pltpu.VMEM((f, d), BF16), pltpu.VMEM((f, d), BF16),
                        pltpu.SemaphoreType.DMA((3,))],
        compiler_params=_params("arbitrary"))(dh, h_in, g.reshape(1, d), z1, z3, w2, w1t, w3t, dep)


def _ffn_bwd(dh, h_in, g, n, z1, z3, w1t, w3t, w2, dep, tag):
    f = w2.shape[0]
    dh_in, dg, dz1, dz3, act, df = _ffn_bwd_x(dh, h_in, g, z1, z3, w1t, w3t, w2, dep, tag + "_bwd_x")
    dw1t = _mm_tn(dz1, n, BF16, tag + "_dw1", bm=_ffn_tile(f), tk=2048)
    dw3t = _mm_tn(dz3, n, BF16, tag + "_dw3", bm=_ffn_tile(f), tk=2048)
    dw2 = _mm_tn(act, df, BF16, tag + "_dw2", bm=_ffn_tile(f), tk=2048)
    return dh_in, dg, dw1t, dw3t, dw2


def _norm_fwd(h, g, dep, name):
    t, d = h.shape
    tm = min(512, t)

    def kern(h_ref, g_ref, dep_ref, o_ref):
        x = h_ref[...]
        o_ref[...] = _bf(x * _rstd(x) * g_ref[...])

    row = pl.BlockSpec((tm, d), lambda i: (i, 0))
    return pl.pallas_call(
        kern, name=name, grid=(t // tm,),
        in_specs=[row, pl.BlockSpec((1, d), lambda i: (0, 0)), pl.BlockSpec((8, 128), lambda i: (0, 0))],
        out_specs=row, out_shape=jax.ShapeDtypeStruct((t, d), BF16),
        compiler_params=_params("parallel"))(h, g.reshape(1, d), dep)


def _loss_bwd(h, target, g):
    t, d = h.shape
    tm = min(512, t)

    def kern(h_ref, t_ref, g_ref, dh_ref, dg_ref, loss_ref):
        @pl.when(pl.program_id(0) == 0)
        def _():
            dg_ref[...] = jnp.zeros_like(dg_ref)
            loss_ref[...] = jnp.zeros_like(loss_ref)

        x = h_ref[...]
        rstd = _rstd(x)
        xhat = x * rstd
        err = xhat * g_ref[...] - t_ref[...]
        row_loss = jnp.sum(err * err, axis=-1, keepdims=True) * (0.5 / d)
        loss_ref[...] += jnp.broadcast_to(jnp.sum(row_loss, axis=0, keepdims=True), loss_ref.shape)
        dy = err * (1.0 / d)
        dg_ref[...] += jnp.broadcast_to(jnp.sum(dy * xhat, axis=0, keepdims=True), dg_ref.shape)
        dxh = dy * g_ref[...]
        dh_ref[...] = rstd * (dxh - xhat * jnp.mean(dxh * xhat, axis=-1, keepdims=True))

    row = pl.BlockSpec((tm, d), lambda i: (i, 0))
    return pl.pallas_call(
        kern, name="loss_bwd", grid=(t // tm,),
        in_specs=[row, row, pl.BlockSpec((1, d), lambda i: (0, 0))],
        out_specs=[row, pl.BlockSpec((8, d), lambda i: (0, 0)), pl.BlockSpec((8, 128), lambda i: (0, 0))],
        out_shape=[jax.ShapeDtypeStruct((t, d), F32), jax.ShapeDtypeStruct((8, d), F32),
                   jax.ShapeDtypeStruct((8, 128), F32)],
        compiler_params=_params("arbitrary"))(h, target, g.reshape(1, d))


def _glu_fwd(m, wa, wb):
    t, d = m.shape
    c = wa.shape[1]
    tm, tc = min(512, t), min(512, c)

    def kern(m_ref, wa_ref, wb_ref, a_ref, b_ref, glu_ref):
        mv = m_ref[...]
        a = jnp.dot(mv, wa_ref[...], preferred_element_type=F32)
        b = jnp.dot(mv, wb_ref[...], preferred_element_type=F32)
        a_ref[...] = _bf(a)
        b_ref[...] = _bf(b)
        glu_ref[...] = _bf(a * _sigmoid(b))

    col = pl.BlockSpec((tm, tc), lambda i, j: (i, j))
    wspec = pl.BlockSpec((d, tc), lambda i, j: (0, j))
    shape = jax.ShapeDtypeStruct((t, c), BF16)
    return pl.pallas_call(
        kern, name="conv_glu_fwd", grid=(t // tm, c // tc),
        in_specs=[pl.BlockSpec((tm, d), lambda i, j: (i, 0)), wspec, wspec], out_specs=[col, col, col],
        out_shape=[shape, shape, shape], compiler_params=_params("parallel", "parallel"))(m, wa, wb)


def _conv_tile(t):
    return min(256, t)


def _shift_copies(ext, shifted, rows):
    for s in range(8):
        shifted[s] = ext[pl.ds(s, rows), :]


def _shifted_rows(shifted, start, nrows):
    return shifted[start % 8, pl.ds(start - start % 8, nrows), :]


def _conv_fwd(glu, w_dw, g):
    t, c = glu.shape
    tm = _conv_tile(t)
    hb = tm // CONV_HALO

    def kern(cur_ref, halo_ref, w_ref, g_ref, cv_ref, s_ref, ext, shifted):
        i = pl.program_id(0)
        ext[0:CONV_HALO, :] = jnp.where(i > 0, halo_ref[...].astype(F32), 0.0)
        ext[CONV_HALO:tm + CONV_HALO, :] = cur_ref[...].astype(F32)
        ext[tm + CONV_HALO:, :] = jnp.zeros((8, c), F32)
        _shift_copies(ext, shifted, tm + CONV_HALO)
        gv = g_ref[...]
        for r0 in range(0, tm, CONV_ROWS):
            acc = jnp.zeros((CONV_ROWS, c), F32)
            for k in range(CONV_WIDTH):
                acc = acc + _shifted_rows(shifted, r0 + 2 + k, CONV_ROWS) * w_ref[k:k + 1, :]
            cv_ref[r0:r0 + CONV_ROWS, :] = acc
            rn = acc * _rstd(acc) * gv
            s_ref[r0:r0 + CONV_ROWS, :] = _bf(rn * _sigmoid(rn))

    row = pl.BlockSpec((tm, c), lambda i: (i, 0))
    return pl.pallas_call(
        kern, name="conv_fwd", grid=(t // tm,),
        in_specs=[row, pl.BlockSpec((CONV_HALO, c), lambda i: (jnp.maximum(i * hb - 1, 0), 0)),
                  pl.BlockSpec((CONV_HALO, c), lambda i: (0, 0)), pl.BlockSpec((1, c), lambda i: (0, 0))],
        out_specs=[row, row],
        out_shape=[jax.ShapeDtypeStruct((t, c), F32), jax.ShapeDtypeStruct((t, c), BF16)],
        scratch_shapes=[pltpu.VMEM((tm + CONV_HALO + 8, c), F32), pltpu.VMEM((8, tm + CONV_HALO, c), F32)],
        compiler_params=_params("parallel"))(glu, glu, w_dw, g.reshape(1, c))


def _conv_bwd_norm(dh, cv, w_pw2, g, dep):
    t, c = cv.shape
    tm = min(512, t)

    def kern(dh_ref, cv_ref, w_ref, g_ref, dep_ref, dcv_ref, dg_ref):
        @pl.when(pl.program_id(0) == 0)
        def _():
            dg_ref[...] = jnp.zeros_like(dg_ref)

        ds = lax.dot_general(_bf(dh_ref[...]), w_ref[...], (((1,), (1,)), ((), ())), preferred_element_type=F32)
        x = cv_ref[...]
        rstd = _rstd(x)
        xhat = x * rstd
        rn = xhat * g_ref[...]
        sig = _sigmoid(rn)
        drn = ds * (sig * (1.0 + rn * (1.0 - sig)))
        dg_ref[...] += jnp.broadcast_to(jnp.sum(drn * xhat, axis=0, keepdims=True), dg_ref.shape)
        dxh = drn * g_ref[...]
        dcv_ref[...] = rstd * (dxh - xhat * jnp.mean(dxh * xhat, axis=-1, keepdims=True))

    row = pl.BlockSpec((tm, c), lambda i: (i, 0))
    return pl.pallas_call(
        kern, name="conv_bwd_norm", grid=(t // tm,),
        in_specs=[pl.BlockSpec((tm, dh.shape[1]), lambda i: (i, 0)), row,
                  pl.BlockSpec(w_pw2.shape, lambda i: (0, 0)), pl.BlockSpec((1, c), lambda i: (0, 0)),
                  pl.BlockSpec((8, 128), lambda i: (0, 0))],
        out_specs=[row, pl.BlockSpec((8, c), lambda i: (0, 0))],
        out_shape=[jax.ShapeDtypeStruct((t, c), F32), jax.ShapeDtypeStruct((8, c), F32)],
        compiler_params=_params("arbitrary"))(dh, cv, w_pw2, g.reshape(1, c), dep)


def _conv_bwd_dw(dcv, glu, a, b, w_dw):
    t, c = dcv.shape
    tm = _conv_tile(t)
    hb = tm // CONV_HALO
    last = t // CONV_HALO - 1

    def kern(dcv_ref, dnext_ref, glu_ref, gprev_ref, a_ref, b_ref, w_ref, da_ref, db_ref, dw_ref,
             dext, gext, dshift, gshift):
        i = pl.program_id(0)

        @pl.when(i == 0)
        def _():
            dw_ref[...] = jnp.zeros_like(dw_ref)

        dext[0:tm, :] = dcv_ref[...]
        dext[tm:tm + CONV_HALO, :] = jnp.where(i < t // tm - 1, dnext_ref[...], 0.0)
        dext[tm + CONV_HALO:, :] = jnp.zeros((8, c), F32)
        gext[0:CONV_HALO, :] = jnp.where(i > 0, gprev_ref[...].astype(F32), 0.0)
        gext[CONV_HALO:tm + CONV_HALO, :] = glu_ref[...].astype(F32)
        gext[tm + CONV_HALO:, :] = jnp.zeros((8, c), F32)
        _shift_copies(dext, dshift, tm + CONV_HALO)
        _shift_copies(gext, gshift, tm + CONV_HALO)
        for r0 in range(0, tm, CONV_ROWS):
            acc = jnp.zeros((CONV_ROWS, c), F32)
            for k in range(CONV_WIDTH):
                acc = acc + _shifted_rows(dshift, r0 + CONV_WIDTH - 1 - k, CONV_ROWS) * w_ref[k:k + 1, :]
            av = a_ref[r0:r0 + CONV_ROWS, :].astype(F32)
            sig = _sigmoid(b_ref[r0:r0 + CONV_ROWS, :].astype(F32))
            da_ref[r0:r0 + CONV_ROWS, :] = _bf(acc * sig)
            db_ref[r0:r0 + CONV_ROWS, :] = _bf(acc * av * sig * (1.0 - sig))
        for k in range(CONV_WIDTH):
            acc = jnp.zeros((CONV_ROWS, c), F32)
            for r0 in range(0, tm, CONV_ROWS):
                acc = acc + _shifted_rows(gshift, r0 + 2 + k, CONV_ROWS) * dext[r0:r0 + CONV_ROWS, :]
            dw_ref[k:k + 1, :] += jnp.sum(acc, axis=0, keepdims=True)

    row = pl.BlockSpec((tm, c), lambda i: (i, 0))
    shape = jax.ShapeDtypeStruct((t, c), BF16)
    return pl.pallas_call(
        kern, name="conv_bwd_dw", grid=(t // tm,),
        in_specs=[row, pl.BlockSpec((CONV_HALO, c), lambda i: (jnp.minimum((i + 1) * hb, last), 0)),
                  row, pl.BlockSpec((CONV_HALO, c), lambda i: (jnp.maximum(i * hb - 1, 0), 0)),
                  row, row, pl.BlockSpec((CONV_HALO, c), lambda i: (0, 0))],
        out_specs=[row, row, pl.BlockSpec((CONV_HALO, c), lambda i: (0, 0))],
        out_shape=[shape, shape, jax.ShapeDtypeStruct((CONV_HALO, c), F32)],
        scratch_shapes=[pltpu.VMEM((tm + CONV_HALO + 8, c), F32), pltpu.VMEM((tm + CONV_HALO + 8, c), F32),
                        pltpu.VMEM((8, tm + CONV_HALO, c), F32), pltpu.VMEM((8, tm + CONV_HALO, c), F32)],
        compiler_params=_params("arbitrary"))(dcv, dcv, glu, glu, a, b, w_dw)


def _rope_tables(pos):
    t = pos.shape[0]
    tm = min(512, t)
    freq = (np.float32(ROPE_THETA) ** (np.float32(-2.0) * np.arange(ROPE // 2, dtype=np.float32)
                                       / np.float32(ROPE))).astype(np.float32)
    row = np.zeros((2, 128), np.float32)
    row[0, :ROPE] = np.concatenate([freq, freq])
    row[1, :ROPE] = 1.0

    def kern(pos_ref, f_ref, c_ref, s_ref):
        ang = pos_ref[...].astype(F32) * f_ref[0:1, :]
        mask = f_ref[1:2, :]
        c_ref[...] = jnp.cos(ang) * mask
        s_ref[...] = jnp.sin(ang) * mask

    out = pl.BlockSpec((tm, 128), lambda i: (i, 0))
    shape = jax.ShapeDtypeStruct((t, 128), F32)
    return pl.pallas_call(
        kern, name="rope_tables", grid=(t // tm,),
        in_specs=[pl.BlockSpec((tm, 1), lambda i: (i, 0)), pl.BlockSpec((2, 128), lambda i: (0, 0))],
        out_specs=[out, out], out_shape=[shape, shape], compiler_params=_params("parallel"))(pos, jnp.asarray(row))


def _mla_prep(a, gq, gkv, cs_c, cs_s):
    t = a.shape[0]
    tm = min(512, t)
    kv0, r0 = Q_LORA, Q_LORA + KV_LORA

    def kern(a_ref, gq_ref, gkv_ref, c_ref, s_ref, cq_ref, ckv_ref, kr_ref):
        aq = a_ref[:, 0:kv0]
        akv = a_ref[:, kv0:r0]
        ar = a_ref[:, r0:A_PAD]
        cq_ref[...] = _bf(aq * _rstd(aq) * gq_ref[...])
        ckv_ref[...] = _bf(akv * _rstd(akv) * gkv_ref[...])
        kr_ref[...] = _bf(ar * c_ref[...] + _rot(ar) * s_ref[...])

    def row(w):
        return pl.BlockSpec((tm, w), lambda i: (i, 0))

    def vec(w):
        return pl.BlockSpec((1, w), lambda i: (0, 0))

    return pl.pallas_call(
        kern, name="mla_prep", grid=(t // tm,),
        in_specs=[row(A_PAD), vec(Q_LORA), vec(KV_LORA), row(128), row(128)],
        out_specs=[row(Q_LORA), row(KV_LORA), row(128)],
        out_shape=[jax.ShapeDtypeStruct((t, Q_LORA), BF16), jax.ShapeDtypeStruct((t, KV_LORA), BF16),
                   jax.ShapeDtypeStruct((t, 128), BF16)],
        compiler_params=_params("parallel"))(a, gq.reshape(1, -1), gkv.reshape(1, -1), cs_c, cs_s)


def _mla_prep_bwd(a, dcq, dckv, dar, gq, gkv):
    t = a.shape[0]
    tm = min(512, t)
    kv0, r0 = Q_LORA, Q_LORA + KV_LORA

    def kern(a_ref, dcq_ref, dckv_ref, dar_ref, gq_ref, gkv_ref, da_ref, dgq_ref, dgkv_ref):
        @pl.when(pl.program_id(0) == 0)
        def _():
            dgq_ref[...] = jnp.zeros_like(dgq_ref)
            dgkv_ref[...] = jnp.zeros_like(dgkv_ref)

        def back(x, dy, g_ref, dg_ref):
            rstd = _rstd(x)
            xhat = x * rstd
            dg_ref[...] += jnp.broadcast_to(jnp.sum(dy * xhat, axis=0, keepdims=True), dg_ref.shape)
            dxh = dy * g_ref[...]
            return rstd * (dxh - xhat * jnp.mean(dxh * xhat, axis=-1, keepdims=True))

        da_ref[:, 0:kv0] = _bf(back(a_ref[:, 0:kv0], dcq_ref[...], gq_ref, dgq_ref))
        da_ref[:, kv0:r0] = _bf(back(a_ref[:, kv0:r0], dckv_ref[...], gkv_ref, dgkv_ref))
        da_ref[:, r0:A_PAD] = _bf(dar_ref[...])

    def row(w):
        return pl.BlockSpec((tm, w), lambda i: (i, 0))

    def vec(r, w):
        return pl.BlockSpec((r, w), lambda i: (0, 0))

    return pl.pallas_call(
        kern, name="mla_prep_bwd", grid=(t // tm,),
        in_specs=[row(A_PAD), row(Q_LORA), row(KV_LORA), row(128), vec(1, Q_LORA), vec(1, KV_LORA)],
        out_specs=[row(A_PAD), vec(8, Q_LORA), vec(8, KV_LORA)],
        out_shape=[jax.ShapeDtypeStruct((t, A_PAD), BF16), jax.ShapeDtypeStruct((8, Q_LORA), F32),
                   jax.ShapeDtypeStruct((8, KV_LORA), F32)],
        compiler_params=_params("arbitrary"))(a, dcq, dckv, dar, gq.reshape(1, -1), gkv.reshape(1, -1))


def _mla_qkv(cq, ckv, kr, cs_c, cs_s, wuq, wukv):
    t = cq.shape[0]
    tm = min(512, t)
    kvw = NOPE + V_HEAD

    def kern(cq_ref, ckv_ref, kr_ref, c_ref, s_ref, wq_ref, wkv_ref, q_ref, k_ref, v_ref):
        r = jnp.dot(cq_ref[...], wq_ref[...], preferred_element_type=F32)
        kv = jnp.dot(ckv_ref[...], wkv_ref[...], preferred_element_type=F32)
        cv, sv, krv = c_ref[...], s_ref[...], kr_ref[...]
        for h in range(HEADS):
            xr = r[:, h * HEAD_PAD + NOPE:(h + 1) * HEAD_PAD]
            q_ref[h, :, 0:NOPE] = _bf(r[:, h * HEAD_PAD:h * HEAD_PAD + NOPE] * ATTN_SCALE)
            q_ref[h, :, NOPE:] = _bf((xr * cv + _rot(xr) * sv) * ATTN_SCALE)
            k_ref[h, :, 0:NOPE] = _bf(kv[:, h * kvw:h * kvw + NOPE])
            k_ref[h, :, NOPE:] = krv
            v_ref[h] = _bf(kv[:, h * kvw + NOPE:(h + 1) * kvw])

    def row(w):
        return pl.BlockSpec((tm, w), lambda i: (i, 0))

    def heads(w):
        return pl.BlockSpec((HEADS, tm, w), lambda i: (0, i, 0))

    return pl.pallas_call(
        kern, name="mla_qkv", grid=(t // tm,),
        in_specs=[row(Q_LORA), row(KV_LORA), row(128), row(128), row(128),
                  pl.BlockSpec(wuq.shape, lambda i: (0, 0)), pl.BlockSpec(wukv.shape, lambda i: (0, 0))],
        out_specs=[heads(HEAD_PAD), heads(HEAD_PAD), heads(V_HEAD)],
        out_shape=[jax.ShapeDtypeStruct((HEADS, t, HEAD_PAD), BF16), jax.ShapeDtypeStruct((HEADS, t, HEAD_PAD), BF16),
                   jax.ShapeDtypeStruct((HEADS, t, V_HEAD), BF16)],
        compiler_params=_params("parallel"))(cq, ckv, kr, cs_c, cs_s, wuq, wukv)


def _mla_qkv_bwd(dq, dk, dv, cs_c, cs_s, wuq, wukv):
    t = dq.shape[1]
    tm = min(256, t)
    kvw = NOPE + V_HEAD

    def kern(dq_ref, dk_ref, dv_ref, c_ref, s_ref, wq_ref, wkv_ref, dr_ref, dkv_ref, dcq_ref, dckv_ref, dar_ref):
        cv, sv = c_ref[...], s_ref[...]
        dar = jnp.zeros_like(cv)
        for h in range(HEADS):
            dqx = dq_ref[h, :, NOPE:]
            dr_ref[:, h * HEAD_PAD:h * HEAD_PAD + NOPE] = _bf(dq_ref[h, :, 0:NOPE] * ATTN_SCALE)
            dr_ref[:, h * HEAD_PAD + NOPE:(h + 1) * HEAD_PAD] = _bf((dqx * cv + _rot_t(dqx * sv)) * ATTN_SCALE)
            dkx = dk_ref[h, :, NOPE:]
            dar = dar + (dkx * cv + _rot_t(dkx * sv))
            dkv_ref[:, h * kvw:h * kvw + NOPE] = _bf(dk_ref[h, :, 0:NOPE])
            dkv_ref[:, h * kvw + NOPE:(h + 1) * kvw] = _bf(dv_ref[h])
        dar_ref[...] = dar
        dcq_ref[...] = lax.dot_general(dr_ref[...], wq_ref[...], (((1,), (1,)), ((), ())),
                                       preferred_element_type=F32)
        dckv_ref[...] = lax.dot_general(dkv_ref[...], wkv_ref[...], (((1,), (1,)), ((), ())),
                                        preferred_element_type=F32)

    def row(w):
        return pl.BlockSpec((tm, w), lambda i: (i, 0))

    def heads(w):
        return pl.BlockSpec((HEADS, tm, w), lambda i: (0, i, 0))

    return pl.pallas_call(
        kern, name="mla_qkv_bwd", grid=(t // tm,),
        in_specs=[heads(HEAD_PAD), heads(HEAD_PAD), heads(V_HEAD), row(128), row(128),
                  pl.BlockSpec(wuq.shape, lambda i: (0, 0)), pl.BlockSpec(wukv.shape, lambda i: (0, 0))],
        out_specs=[row(HEADS * HEAD_PAD), row(HEADS * kvw), row(Q_LORA), row(KV_LORA), row(128)],
        out_shape=[jax.ShapeDtypeStruct((t, HEADS * HEAD_PAD), BF16), jax.ShapeDtypeStruct((t, HEADS * kvw), BF16),
                   jax.ShapeDtypeStruct((t, Q_LORA), F32), jax.ShapeDtypeStruct((t, KV_LORA), F32),
                   jax.ShapeDtypeStruct((t, 128), F32)],
        compiler_params=_params("parallel"))(dq, dk, dv, cs_c, cs_s, wuq, wukv)


def _attn_block(t):
    return 512 if t >= 4096 else 128


def _chunk_mask(bk, bq):
    kc = lax.broadcasted_iota(jnp.int32, (bk, bq), 0) // CHUNK
    qc = lax.broadcasted_iota(jnp.int32, (bk, bq), 1) // CHUNK
    return qc >= kc


def _flash_fwd(q, k, v):
    t = q.shape[1]
    bq = _attn_block(t)
    nq = t // bq
    nch = 2

    def kern(q_ref, k_ref, v_ref, o_ref, lse_ref, s_buf, p_buf, m_ref, l_ref, acc_ref):
        i = pl.program_id(1)
        queries = [q_ref[c * bq:(c + 1) * bq, :] for c in range(nch)]

        def block(j):
            rows = pl.ds(pl.multiple_of(j * bq, bq), bq)
            return k_ref[rows, :], v_ref[rows, :]

        def scores(kj, chain):
            return lax.dot_general(kj, queries[chain], (((1,), (1,)), ((), ())), preferred_element_type=F32)

        def softmax_block(chain, slot, vj):
            for c0 in range(0, bq, 128):
                cols = slice(c0, c0 + 128)
                s = s_buf[slot, chain, :, cols]
                m_old = m_ref[chain, 0:1, cols]
                m_new = jnp.maximum(m_old, jnp.max(s, axis=0, keepdims=True))
                alpha = jnp.exp(m_old - m_new)
                p = jnp.exp(s - m_new)
                l_ref[chain, 0:1, cols] = alpha * l_ref[chain, 0:1, cols] + jnp.sum(p, axis=0, keepdims=True)
                m_ref[chain, 0:1, cols] = m_new
                p_buf[chain, :, cols] = _bf(p)
                acc_ref[chain, :, cols] = acc_ref[chain, :, cols] * alpha
            acc_ref[chain] += lax.dot_general(vj, p_buf[chain], (((0,), (0,)), ((), ())),
                                              preferred_element_type=F32)

        m_ref[...] = jnp.full(m_ref.shape, -1e30, F32)
        l_ref[...] = jnp.zeros_like(l_ref)
        acc_ref[...] = jnp.zeros_like(acc_ref)
        mask = _chunk_mask(bq, bq)
        for b in range(nch):
            kb, vb = block(nch * i + b)
            for c in range(b, nch):
                s_buf[b % 2, c] = jnp.where(mask, scores(kb, c), -1e30) if c == b else scores(kb, c)
                softmax_block(c, b % 2, vb)
        kf = block(0)[0]
        for c in range(nch):
            s_buf[0, c] = scores(kf, c)

        def body(pair, carry):
            for cur in range(2):
                j = 2 * pair + cur
                kn = block(jnp.minimum(j + 1, jnp.maximum(nch * i - 1, 0)))[0]
                for c in range(nch):
                    s_buf[1 - cur, c] = scores(kn, c)
                vj = block(j)[1]
                for c in range(nch):
                    softmax_block(c, cur, vj)
            return carry

        lax.fori_loop(0, (nch // 2) * i, body, 0)
        for chain in range(nch):
            l = l_ref[chain, 0:1, :]
            o_ref[chain * bq:(chain + 1) * bq, :] = _bf((acc_ref[chain] / l).T)
            lse_ref[chain] = jnp.broadcast_to(m_ref[chain, 0:1, :] + jnp.log(l), (8, bq))

    return pl.pallas_call(
        kern, name="flash_fwd", grid=(HEADS, nq // nch),
        in_specs=[pl.BlockSpec((None, nch * bq, HEAD_PAD), lambda h, i: (h, i, 0)),
                  pl.BlockSpec((None, t, HEAD_PAD), lambda h, i: (h, 0, 0)),
                  pl.BlockSpec((None, t, V_HEAD), lambda h, i: (h, 0, 0))],
        out_specs=[pl.BlockSpec((nch * bq, V_HEAD), lambda h, i: (i, h)),
                   pl.BlockSpec((None, nch, 8, bq), lambda h, i: (h, i, 0, 0))],
        out_shape=[jax.ShapeDtypeStruct((t, HEADS * V_HEAD), BF16), jax.ShapeDtypeStruct((HEADS, nq, 8, bq), F32)],
        scratch_shapes=[pltpu.VMEM((2, nch, bq, bq), F32), pltpu.VMEM((nch, bq, bq), BF16),
                        pltpu.VMEM((nch, 8, bq), F32), pltpu.VMEM((nch, 8, bq), F32),
                        pltpu.VMEM((nch, V_HEAD, bq), F32)],
        compiler_params=_params("parallel", "arbitrary"))(q, k, v)


def _attn_delta(do, o):
    t = do.shape[0]
    bq = _attn_block(t)

    def kern(do_ref, o_ref, d_ref):
        for h in range(HEADS):
            cols = slice(h * V_HEAD, (h + 1) * V_HEAD)
            prod = do_ref[:, cols].astype(F32) * o_ref[:, cols].astype(F32)
            d_ref[h] = jnp.broadcast_to(jnp.sum(prod.T, axis=0, keepdims=True), (8, bq))

    blk = pl.BlockSpec((bq, HEADS * V_HEAD), lambda i: (i, 0))
    return pl.pallas_call(
        kern, name="attn_delta", grid=(t // bq,), in_specs=[blk, blk],
        out_specs=pl.BlockSpec((HEADS, None, 8, bq), lambda i: (0, i, 0, 0)),
        out_shape=jax.ShapeDtypeStruct((HEADS, t // bq, 8, bq), F32),
        compiler_params=_params("parallel"))(do, o)


def _flash_bwd(q, k, v, do, lse, delta):
    t = q.shape[1]
    bq = _attn_block(t)
    nq = t // bq

    def kern(q_ref, k_ref, v_ref, do_ref, lse_ref, del_ref, dq_ref, dk_ref, dv_ref, dvt_ref):
        j = pl.program_id(1)

        @pl.when(j == 0)
        def _():
            dq_ref[...] = jnp.zeros_like(dq_ref)

        dk_ref[...] = jnp.zeros_like(dk_ref)
        dvt_ref[...] = jnp.zeros_like(dvt_ref)
        kj, vj = k_ref[...], v_ref[...]

        def step(i, masked):
            rows = pl.ds(pl.multiple_of(i * bq, bq), bq)
            qi, doi = q_ref[rows, :], do_ref[rows, :]
            st = lax.dot_general(kj, qi, (((1,), (1,)), ((), ())), preferred_element_type=F32)
            pt = jnp.exp(st - lse_ref[i][0:1, :])
            if masked:
                pt = jnp.where(_chunk_mask(bq, bq), pt, 0.0)
            dpt = lax.dot_general(vj, doi, (((1,), (1,)), ((), ())), preferred_element_type=F32)
            dst = _bf(pt * (dpt - del_ref[i][0:1, :]))
            dvt_ref[...] += lax.dot_general(doi, _bf(pt), (((0,), (1,)), ((), ())), preferred_element_type=F32)
            dk_ref[...] += jnp.dot(dst, qi, preferred_element_type=F32)
            dq_ref[rows, :] += lax.dot_general(dst, kj, (((0,), (0,)), ((), ())), preferred_element_type=F32)

        step(j, True)

        def body(pair, carry):
            step(j + 1 + 2 * pair, False)
            step(j + 2 + 2 * pair, False)
            return carry

        rest = nq - 1 - j
        lax.fori_loop(0, rest // 2, body, 0)

        @pl.when(rest % 2 == 1)
        def _():
            step(nq - 1, False)

        dv_ref[...] = dvt_ref[...].T

    stat = pl.BlockSpec((None, nq, 8, bq), lambda h, j: (h, 0, 0, 0))
    return pl.pallas_call(
        kern, name="flash_bwd", grid=(HEADS, nq),
        in_specs=[pl.BlockSpec((None, t, HEAD_PAD), lambda h, j: (h, 0, 0)),
                  pl.BlockSpec((None, bq, HEAD_PAD), lambda h, j: (h, j, 0)),
                  pl.BlockSpec((None, bq, V_HEAD), lambda h, j: (h, j, 0)),
                  pl.BlockSpec((t, V_HEAD), lambda h, j: (0, h)), stat, stat],
        out_specs=[pl.BlockSpec((None, t, HEAD_PAD), lambda h, j: (h, 0, 0)),
                   pl.BlockSpec((None, bq, HEAD_PAD), lambda h, j: (h, j, 0)),
                   pl.BlockSpec((None, bq, V_HEAD), lambda h, j: (h, j, 0))],
        out_shape=[jax.ShapeDtypeStruct((HEADS, t, HEAD_PAD), F32), jax.ShapeDtypeStruct((HEADS, t, HEAD_PAD), F32),
                   jax.ShapeDtypeStruct((HEADS, t, V_HEAD), F32)],
        scratch_shapes=[pltpu.VMEM((V_HEAD, bq), F32)],
        compiler_params=_params("parallel", "arbitrary"))(q, k, v, do, lse, delta)


def _place():
    x, y, c = lax.axis_index("x"), lax.axis_index("y"), lax.axis_index("c")
    return x, y, c, [(1 - x, y), (x, 1 - y), (1 - x, 1 - y)]


def _all_gather_rows(block, name, dep=None):
    m_per, n = block.shape

    def body(x_ref, *rest):
        out_ref, send_sems, recv_sems, local_sem = rest[-4:]
        x, y, c, chips = _place()
        me, sibling = (x, y, c), (x, y, 1 - c)

        def rows(px, py, pc):
            return out_ref.at[pl.ds((4 * px + 2 * py + pc) * m_per, m_per), :]

        def copy(k, blk, to, src=None):
            return pltpu.make_async_remote_copy(
                src_ref=rows(*blk) if src is None else src, dst_ref=rows(*blk), send_sem=send_sems.at[k],
                recv_sem=recv_sems.at[k], device_id=to, device_id_type=MESH)

        mine = pltpu.make_async_copy(x_ref, rows(*me), local_sem)
        mine.start()
        first = [copy(0, me, sibling, src=x_ref)]
        first += [copy(1 + j, me, (*chip, c), src=x_ref) for j, chip in enumerate(chips)]
        for cp in first:
            cp.start()
        passed = [copy(4 + j, (*chip, c), sibling) for j, chip in enumerate(chips)]
        for j, chip in enumerate(chips):
            copy(1 + j, (*chip, c), me).wait_recv()
            passed[j].start()
        copy(0, sibling, me).wait_recv()
        for j, chip in enumerate(chips):
            copy(4 + j, (*chip, 1 - c), me).wait_recv()
        for cp in first + passed:
            cp.wait_send()
        mine.wait()

    return pl.pallas_call(
        body, name=name, out_shape=jax.ShapeDtypeStruct((8 * m_per, n), block.dtype),
        in_specs=[pl.BlockSpec(memory_space=pltpu.VMEM)] + ([pl.BlockSpec(memory_space=pl.ANY)] if dep is not None else []),
        out_specs=pl.BlockSpec(memory_space=pltpu.VMEM),
        scratch_shapes=[pltpu.SemaphoreType.DMA((7,)), pltpu.SemaphoreType.DMA((7,)), pltpu.SemaphoreType.DMA],
        compiler_params=pltpu.CompilerParams(vmem_limit_bytes=VMEM_LIMIT_BYTES))(
            *([block] if dep is None else [block, dep]))


HBM_SPEC = pl.BlockSpec(memory_space=pltpu.HBM)
SEM_SPEC = pl.BlockSpec(memory_space=pltpu.SEMAPHORE)
DATAFLOW = pltpu.SideEffectType.DATAFLOW_SIDE_EFFECTING


def _in_hbm(a):
    return pltpu.with_memory_space_constraint(a, pltpu.HBM)


def _chip_copies(ins, lands, send_sems, recv_sems, src_slot, half=False):
    n = len(ins)
    x, y, c, chips = _place()
    me = 2 * x + y
    if src_slot == "sibling":
        return [pltpu.make_async_remote_copy(src_ref=ins[w], dst_ref=lands[w], send_sem=send_sems.at[w],
                                             recv_sem=recv_sems.at[w], device_id=(x, y, 1 - c), device_id_type=MESH)
                for w in range(n)]

    def ends(w, chip):
        src = ins[w].at[2 * chip[0] + chip[1]] if src_slot else ins[w]
        if not half:
            return src, lands[w].at[me]
        rows = pl.ds(pl.multiple_of(c * (src.shape[0] // 2), 16), src.shape[0] // 2)
        return src.at[rows], lands[w].at[me, rows]

    copies = []
    for w in range(n):
        for p, chip in enumerate(chips):
            src, dst = ends(w, chip)
            copies.append(pltpu.make_async_remote_copy(
                src_ref=src, dst_ref=dst, send_sem=send_sems.at[p * n + w], recv_sem=recv_sems.at[p * n + w],
                device_id=(*chip, c), device_id_type=MESH))
    return copies


def _fill_halves(lands, name):
    n = len(lands)

    def body(*refs):
        bufs = refs[n:2 * n]
        send_sems, recv_sems = refs[2 * n:]
        x, y, c, chips = _place()
        copies = []
        for w in range(n):
            hr = bufs[w].shape[1] // 2
            for p, chip in enumerate(chips):
                part = bufs[w].at[2 * chip[0] + chip[1], pl.ds(pl.multiple_of(c * hr, 16), hr)]
                copies.append(pltpu.make_async_remote_copy(
                    src_ref=part, dst_ref=part, send_sem=send_sems.at[p * n + w], recv_sem=recv_sems.at[p * n + w],
                    device_id=(x, y, 1 - c), device_id_type=MESH))
        for cp in copies:
            cp.start()
        for cp in copies:
            cp.wait_send()
        for w in range(n):
            hr = bufs[w].shape[1] // 2
            for p, chip in enumerate(chips):
                part = bufs[w].at[2 * chip[0] + chip[1], pl.ds(pl.multiple_of((1 - c) * hr, 16), hr)]
                pltpu.make_async_remote_copy(
                    src_ref=part, dst_ref=part, send_sem=send_sems.at[p * n + w], recv_sem=recv_sems.at[p * n + w],
                    device_id=(x, y, 1 - c), device_id_type=MESH).wait_recv()

    any_spec = pl.BlockSpec(memory_space=pl.ANY)
    return list(pl.pallas_call(
        body, name=name, out_shape=[jax.ShapeDtypeStruct(a.shape, a.dtype) for a in lands],
        in_specs=[any_spec] * n, out_specs=[any_spec] * n, input_output_aliases={i: i for i in range(n)},
        scratch_shapes=[pltpu.SemaphoreType.DMA((3 * n,)), pltpu.SemaphoreType.DMA((3 * n,))])(*lands))


def _exchange_start(srcs, lands, src_slot, name, dep=None, half=False):
    n = len(srcs)
    first_out = 2 * n + (dep is not None)

    def body(*refs):
        for cp in _chip_copies(refs[:n], refs[n:2 * n], refs[first_out], refs[first_out + 1], src_slot, half):
            cp.start()
        token = refs[-1]
        token[...] = jnp.zeros_like(token)

    thru = [pltpu.HBM(a.shape, a.dtype) for a in list(srcs) + list(lands)]
    res = pl.pallas_call(
        body, name=name,
        out_shape=(pltpu.SemaphoreType.DMA((3 * n,)), pltpu.SemaphoreType.DMA((3 * n,)), *thru,
                   jax.ShapeDtypeStruct((8, 128), F32)),
        in_specs=[HBM_SPEC] * (2 * n) + ([pl.BlockSpec(memory_space=pl.ANY)] if dep is not None else []),
        out_specs=(SEM_SPEC, SEM_SPEC, *[HBM_SPEC] * (2 * n), pl.BlockSpec(memory_space=pltpu.VMEM)),
        input_output_aliases={i: 2 + i for i in range(2 * n)},
        compiler_params=pltpu.CompilerParams(has_side_effects=DATAFLOW))(
            *[_in_hbm(a) for a in srcs], *[_in_hbm(a) for a in lands], *([dep] if dep is not None else []))
    return (res[0], res[1], list(res[2:2 + n]), list(res[2 + n:2 + 2 * n])), res[-1]


def _exchange_wait(flight, after, src_slot, name, half=False):
    send_sems, recv_sems, srcs, lands = flight
    n = len(srcs)

    def body(*refs):
        for cp in _chip_copies(refs[:n], refs[n:2 * n], refs[2 * n], refs[2 * n + 1], src_slot, half):
            cp.wait_send()
            cp.wait_recv()

    thru = [pltpu.HBM(a.shape, a.dtype) for a in list(srcs) + list(lands)]
    res = pl.pallas_call(
        body, name=name, out_shape=thru,
        in_specs=[HBM_SPEC] * (2 * n) + [SEM_SPEC, SEM_SPEC, pl.BlockSpec(memory_space=pl.ANY)],
        out_specs=[HBM_SPEC] * (2 * n), input_output_aliases={i: i for i in range(2 * n)},
        compiler_params=pltpu.CompilerParams(has_side_effects=DATAFLOW))(*srcs, *lands, send_sems, recv_sems, after)
    return list(res[n:])


def _landing(own, me):
    return lax.dynamic_update_index_in_dim(lax.empty((4, *own.shape), own.dtype), own, me, 0)


def _as_rows(a):
    return a.reshape(-1, a.shape[-1])


def _row_tile(r, c, budget_bytes=1 << 20):
    tr = r
    while tr % 16 == 0 and tr * c * 4 > budget_bytes:
        tr //= 2
    return tr


def _sum_slots(layers, nlayer, name, into=None):
    _, r, c = layers[0][1].shape
    tr = _row_tile(r, c)
    nt = r // tr
    acc = into
    for l, r4 in layers:
        def kern(r_ref, *rest):
            o_ref = rest[-1]
            o_ref[...] = (((r_ref[0].astype(F32) + r_ref[1].astype(F32)) + r_ref[2].astype(F32))
                          + r_ref[3].astype(F32))

        out_spec = pl.BlockSpec((tr, c), lambda i, l=l: (l * nt + i, 0))
        first = acc is None
        acc = pl.pallas_call(
            kern, name=f"{name}_l{l}", grid=(nt,),
            in_specs=[pl.BlockSpec((4, tr, c), lambda i: (0, i, 0))]
            + ([] if first else [pl.BlockSpec(memory_space=pl.ANY)]),
            out_specs=out_spec, out_shape=jax.ShapeDtypeStruct((nlayer * r, c), F32),
            input_output_aliases={} if first else {1: 0},
            compiler_params=_params("parallel"))(*([r4] if first else [r4, acc]))
    return acc


def _adamw(w, m, v, parts, name):
    r, c = w.shape
    tr = _row_tile(r, c, 3 << 19)
    npart = len(parts)
    c1 = 1.0 - ADAM_B1 ** ADAM_STEP
    c2 = 1.0 - ADAM_B2 ** ADAM_STEP

    def kern(*refs):
        w_ref, m_ref, v_ref = refs[:3]
        p_refs = refs[3:3 + npart]
        g_ref, d_ref, mo_ref, vo_ref = refs[3 + npart:]
        g = p_refs[0][...]
        for p in p_refs[1:]:
            g = g + p[...]
        mn = ADAM_B1 * m_ref[...] + (1.0 - ADAM_B1) * g
        vn = ADAM_B2 * v_ref[...] + (1.0 - ADAM_B2) * (g * g)
        g_ref[...] = g
        mo_ref[...] = mn
        vo_ref[...] = vn
        d_ref[...] = -ADAM_LR * ((mn / c1) / (jnp.sqrt(vn / c2) + ADAM_EPS) + ADAM_WD * w_ref[...])

    blk = pl.BlockSpec((tr, c), lambda i: (i, 0))
    shape = jax.ShapeDtypeStruct((r, c), F32)
    return pl.pallas_call(
        kern, name=name, grid=(r // tr,), in_specs=[blk] * (3 + npart), out_specs=[blk] * 4, out_shape=[shape] * 4,
        compiler_params=_params("parallel"))(w, m, v, *parts)


def _sum_devices(g8, name):
    _, r, c = g8.shape

    def kern(g_ref, o_ref):
        tot = g_ref[0]
        for dev in range(1, 8):
            tot = tot + g_ref[dev]
        o_ref[...] = tot

    return pl.pallas_call(
        kern, name=name, grid=(1,), in_specs=[pl.BlockSpec((8, r, c), lambda i: (0, 0, 0))],
        out_specs=pl.BlockSpec((r, c), lambda i: (0, 0)), out_shape=jax.ShapeDtypeStruct((r, c), F32),
        compiler_params=_params("arbitrary"))(g8)


def _pad_lanes(a, width):
    return jnp.pad(a, [(0, 0)] * (a.ndim - 1) + [(0, width - a.shape[-1])])


def kernel(x, positions, ffn_norm1, ffn1_w1, ffn1_w3, ffn1_w2, mix_norm, ffn_norm2, ffn2_w1, ffn2_w3, ffn2_w2, conv_w_pw1, conv_w_dw, conv_norm, conv_w_pw2, mla_w_a, mla_q_norm, mla_kv_norm, mla_w_uq, mla_w_ukv, mla_w_o, final_norm, loss_target, m_ffn_norm1, m_ffn1_w1, m_ffn1_w3, m_ffn1_w2, m_mix_norm, m_ffn_norm2, m_ffn2_w1, m_ffn2_w3, m_ffn2_w2, m_conv_w_pw1, m_conv_w_dw, m_conv_norm, m_conv_w_pw2, m_mla_w_a, m_mla_q_norm, m_mla_kv_norm, m_mla_w_uq, m_mla_w_ukv, m_mla_w_o, m_final_norm, v_ffn_norm1, v_ffn1_w1, v_ffn1_w3, v_ffn1_w2, v_mix_norm, v_ffn_norm2, v_ffn2_w1, v_ffn2_w3, v_ffn2_w2, v_conv_w_pw1, v_conv_w_dw, v_conv_norm, v_conv_w_pw2, v_mla_w_a, v_mla_q_norm, v_mla_kv_norm, v_mla_w_uq, v_mla_w_ukv, v_mla_w_o, v_final_norm):
    given = locals()
    return _step({nm: given[nm] for nm in INPUTS})


def _step(A):
    x = A['x'][0]
    target = A['loss_target'][0]
    t, d = x.shape
    pos = A['positions'].reshape(t, 1)
    me = 2 * lax.axis_index("x") + lax.axis_index("y")

    flipped = {f'ffn{k}_{w}' for k in (1, 2) for w in ('w1', 'w3')}
    P = {}
    for nm in BIG:
        for key in (nm, 'm_' + nm, 'v_' + nm):
            P[key] = jnp.swapaxes(A[key], 1, 2) if nm in flipped else A[key]

    def unflip(nm, a):
        return jnp.swapaxes(a, 1, 2) if nm in flipped else a

    ffn = [f'ffn{k}_{w}' for k in (1, 2) for w in ('w1', 'w3', 'w2')]
    gather_groups = [[(nm, 0) for nm in ffn[:3]],
                     [('conv_w_pw1', 0), ('conv_w_pw2', 0)] + [(nm, 0) for nm in ffn[3:]],
                     [(nm, 1) for nm in ffn[:3]] + [('mla_w_a', 0), ('mla_w_uq', 0), ('mla_w_ukv', 0), ('mla_w_o', 0)],
                     [(nm, 1) for nm in ffn[3:]]]
    halved = (0, 1)
    gather_flights = {}
    big = {}

    def gather_start(gi, dep):
        shards = [_bf(P[nm][l]) for nm, l in gather_groups[gi]]
        gather_flights[gi], token = _exchange_start(shards, [_landing(s, me) for s in shards], False,
                                                    f"gather_start_{gi}", dep, half=gi in halved)
        return token

    def gather_wait(gi, after):
        landed = _exchange_wait(gather_flights[gi], after, False, f"gather_wait_{gi}", half=gi in halved)
        if gi in halved:
            landed = _fill_halves(landed, f"gather_fill_{gi}")
        big.update(zip(gather_groups[gi], landed))
        return landed[0]

    dw_shard = A['conv_w_dw'][0]
    cw = dw_shard.shape[1]
    small = jnp.concatenate([
        jnp.pad(dw_shard, ((0, CONV_HALO - CONV_WIDTH), (0, 0))),
        jnp.pad(_pad_lanes(A['mla_q_norm'], cw), ((0, 7), (0, 0))),
        jnp.pad(_pad_lanes(A['mla_kv_norm'], cw), ((0, 7), (0, 0)))], axis=0)
    small = _all_gather_rows(small, "gather_small_weights").reshape(4, 2, 48, cw)[:, 0]
    w_dw = jnp.concatenate([small[j, :CONV_HALO] for j in range(4)], axis=1)
    gq = jnp.concatenate([small[j, CONV_HALO, :Q_LORA // 4] for j in range(4)])
    gkv = jnp.concatenate([small[j, CONV_HALO + 8, :KV_LORA // 4] for j in range(4)])

    def rows(nm, layer):
        g = big[nm, layer]
        return g.reshape(-1, g.shape[-1])

    ffn_w = {}

    def ffn_weights(k, l):
        ffn_w[k, l] = (rows(f'ffn{k}_w1', l), rows(f'ffn{k}_w3', l), rows(f'ffn{k}_w2', l))
        return ffn_w[k, l]

    token = gather_start(0, small)
    cs_c, cs_s = _rope_tables(pos)
    h0 = x
    token = gather_start(1, gather_wait(0, token))
    h1, n01, z01a, z01b = _ffn_fwd(h0, A['ffn_norm1'][0], *ffn_weights(1, 0), token, "ffn1_l0_fwd")
    token = gather_start(3, gather_start(2, gather_wait(1, h1)))
    pw1 = big['conv_w_pw1', 0]
    pw1_a = jnp.concatenate([pw1[0], pw1[1]], axis=1)
    pw1_b = jnp.concatenate([pw1[2], pw1[3]], axis=1)
    pw2 = rows('conv_w_pw2', 0)
    m0 = _norm_fwd(h1, A['mix_norm'][0], token, "mix_norm_l0")
    ca, cb, glu = _glu_fwd(m0, pw1_a, pw1_b)
    cv, cs = _conv_fwd(glu, w_dw, A['conv_norm'][0])
    h2 = _mm([(cs, pw2)], F32, "conv_pw2_fwd", res=h1)
    h3, n02, z02a, z02b = _ffn_fwd(h2, A['ffn_norm2'][0], *ffn_weights(2, 0), token, "ffn2_l0_fwd")
    gather_wait(2, h3)
    w_a = _pad_lanes(rows('mla_w_a', 0), A_PAD)
    wuq = _pad_lanes(big['mla_w_uq', 0].reshape(Q_LORA, HEADS, NOPE + ROPE), HEAD_PAD).reshape(Q_LORA, -1)
    wukv = big['mla_w_ukv', 0].reshape(KV_LORA, HEADS * (NOPE + V_HEAD))
    w_o = rows('mla_w_o', 0)
    h4, n11, z11a, z11b = _ffn_fwd(h3, A['ffn_norm1'][1], *ffn_weights(1, 1), token, "ffn1_l1_fwd")
    m1 = _norm_fwd(h4, A['mix_norm'][1], token, "mix_norm_l1")
    a_lat = _mm([(m1, w_a)], F32, "mla_down_fwd")
    cq, ckv, kr = _mla_prep(a_lat, gq, gkv, cs_c, cs_s)
    q, k, v = _mla_qkv(cq, ckv, kr, cs_c, cs_s, wuq, wukv)
    o, lse = _flash_fwd(q, k, v)
    h5 = _mm([(o, w_o)], F32, "mla_out_fwd", res=h4)
    gather_wait(3, h5)
    h6, n12, z12a, z12b = _ffn_fwd(h5, A['ffn_norm2'][1], *ffn_weights(2, 1), token, "ffn2_l1_fwd")

    def row_slots(g):
        return g.reshape(4, g.shape[0] // 4, g.shape[1])

    scatter_flights = []

    def scatter_start(named):
        srcs = [g for _, g in named]
        lands = [_landing(lax.dynamic_index_in_dim(g, me, 0, keepdims=False), me) for g in srcs]
        flight, token = _exchange_start(srcs, lands, True, f"scatter_start_{len(scatter_flights)}")
        scatter_flights.append(([key for key, _ in named], flight))
        return token

    def send_ffn(k, l, dw1t, dw3t, dw2):
        return scatter_start([((f'ffn{k}_w1', l), row_slots(dw1t)), ((f'ffn{k}_w3', l), row_slots(dw3t)),
                              ((f'ffn{k}_w2', l), row_slots(dw2))])

    dh6, dg_final, loss_part = _loss_bwd(h6, target, A['final_norm'])
    dh5, dg_n2_l1, *dws = _ffn_bwd(dh6, h5, A['ffn_norm2'][1], n12, z12a, z12b, *ffn_w[2, 1], loss_part, "ffn2_l1")
    token = send_ffn(2, 1, *dws)

    do = _mm([(dh5, w_o)], BF16, "mla_out_bwd", trans_b=True, dep=token)
    dw_o = _mm_tn(o, dh5, BF16, "mla_dw_o")
    delta = _attn_delta(do, o)
    dq, dk, dv = _flash_bwd(q, k, v, do, lse, delta)
    dr, dkv, dcq, dckv, dar = _mla_qkv_bwd(dq, dk, dv, cs_c, cs_s, wuq, wukv)
    dwuq = _mm_tn(cq, dr, BF16, "mla_dw_uq", bn=dr.shape[1] // 2)
    dwukv = _mm_tn(ckv, dkv, BF16, "mla_dw_ukv", bn=dkv.shape[1] // 2)
    da_lat, dgq, dgkv = _mla_prep_bwd(a_lat, dcq, dckv, dar, gq, gkv)
    dw_a = _mm_tn(m1, da_lat, BF16, "mla_dw_a")
    token = scatter_start([
        (('mla_w_a', 0), row_slots(dw_a[:, :Q_LORA + KV_LORA + ROPE])),
        (('mla_w_uq', 0), dwuq.reshape(4, Q_LORA // 4, HEADS, HEAD_PAD)[..., :NOPE + ROPE]),
        (('mla_w_ukv', 0), dwukv.reshape(4, KV_LORA // 4, HEADS, NOPE + V_HEAD)),
        (('mla_w_o', 0), row_slots(dw_o))])
    dh4, dg_mix_l1 = _mm_normbwd([(da_lat, w_a)], h4, A['mix_norm'][1], dh5, token, "mla_down_bwd")

    dh3, dg_n1_l1, *dws = _ffn_bwd(dh4, h3, A['ffn_norm1'][1], n11, z11a, z11b, *ffn_w[1, 1], token, "ffn1_l1")
    token = send_ffn(1, 1, *dws)
    dh2, dg_n2_l0, *dws = _ffn_bwd(dh3, h2, A['ffn_norm2'][0], n02, z02a, z02b, *ffn_w[2, 0], token, "ffn2_l0")
    token = send_ffn(2, 0, *dws)

    dcv, dg_conv = _conv_bwd_norm(dh2, cv, pw2, A['conv_norm'][0], token)
    dw_pw2 = _mm_tn(cs, dh2, BF16, "conv_dw_pw2")
    dca, dcb, ddw = _conv_bwd_dw(dcv, glu, ca, cb, w_dw)
    dpw1_a = _mm_tn(m0, dca, BF16, "conv_dw_pw1a")
    dpw1_b = _mm_tn(m0, dcb, BF16, "conv_dw_pw1b")
    half = dpw1_a.shape[1] // 2
    token = scatter_start([
        (('conv_w_pw1', 0), jnp.stack([dpw1_a[:, :half], dpw1_a[:, half:], dpw1_b[:, :half], dpw1_b[:, half:]])),
        (('conv_w_pw2', 0), row_slots(dw_pw2))])
    dh1, dg_mix_l0 = _mm_normbwd([(dca, pw1_a), (dcb, pw1_b)], h1, A['mix_norm'][0], dh2, token, "conv_pw1_bwd")

    dx, dg_n1_l0, *dws = _ffn_bwd(dh1, h0, A['ffn_norm1'][0], n01, z01a, z01b, *ffn_w[1, 0], token, "ffn1_l0")
    last_sent = send_ffn(1, 0, *dws)
    out = {}

    received = {}

    def scatter_wait(si, after):
        keys, flight = scatter_flights[si]
        landed = _exchange_wait(flight, after, True, f"scatter_wait_{si}")
        received.update(zip(keys, landed))
        return landed[0]

    def slots(nm, l):
        return received[nm, l].reshape(4, -1, received[nm, l].shape[-1])

    def sibling_start(sums, tag):
        return _exchange_start(sums, [lax.empty(s.shape, s.dtype) for s in sums], "sibling", "swap_start_" + tag)

    def finish(names, sums, flight, after, tag):
        for nm, mine, theirs in zip(names, sums, _exchange_wait(flight, after, "sibling", "swap_wait_" + tag)):
            res = _adamw(_as_rows(P[nm]), _as_rows(P['m_' + nm]), _as_rows(P['v_' + nm]), [mine, theirs],
                         "adamw_" + nm)
            out[nm] = [unflip(nm, r.reshape(P[nm].shape)) for r in res]
        return res[1]

    after = last_sent
    last = len(scatter_flights) - 1
    for si in range(last):
        after = scatter_wait(si, after)
    late = ffn[:3]
    early = [nm for nm in BIG if nm not in late]
    early_sums = [_sum_slots([(l, slots(nm, l)) for l in range(A[nm].shape[0])], A[nm].shape[0], "sum_" + nm)
                  for nm in early]
    early_flight, token = sibling_start(early_sums, "early")
    late_l1 = [_sum_slots([(1, slots(nm, 1))], 2, "sum_" + nm) for nm in late]

    qkv_row = jnp.concatenate([dgq, dgkv, jnp.zeros((8, d - Q_LORA - KV_LORA), F32)], axis=1)
    loss_row = _pad_lanes(loss_part, d)
    small_g = jnp.concatenate([dg_n1_l0, dg_n1_l1, dg_mix_l0, dg_mix_l1, dg_n2_l0, dg_n2_l1, dg_conv, dg_final,
                               qkv_row, loss_row, ddw], axis=0)
    nrow = small_g.shape[0]
    tot = _sum_devices(_all_gather_rows(small_g, "gather_small_grads", token).reshape(8, nrow, d), "sum_small_grads")
    loss = tot[72, 0]
    q_shard = lax.dynamic_slice_in_dim(tot[64, :Q_LORA], me * (Q_LORA // 4), Q_LORA // 4)
    kv_shard = lax.dynamic_slice_in_dim(tot[64, Q_LORA:Q_LORA + KV_LORA], me * (KV_LORA // 4), KV_LORA // 4)
    dw_shard_g = lax.dynamic_slice_in_dim(tot[80:80 + CONV_WIDTH], me * cw, cw, axis=1)
    small_grads = {
        'ffn_norm1': jnp.stack([tot[0], tot[8]]), 'mix_norm': jnp.stack([tot[16], tot[24]]),
        'ffn_norm2': jnp.stack([tot[32], tot[40]]), 'conv_norm': tot[48][None], 'final_norm': tot[56],
        'mla_q_norm': q_shard[None], 'mla_kv_norm': kv_shard[None], 'conv_w_dw': dw_shard_g[None],
    }
    for nm, g in small_grads.items():
        res = _adamw(_as_rows(A[nm]) if A[nm].ndim > 1 else A[nm].reshape(1, -1),
                     A['m_' + nm].reshape(-1, A[nm].shape[-1]), A['v_' + nm].reshape(-1, A[nm].shape[-1]),
                     [g.reshape(-1, A[nm].shape[-1])], "adamw_" + nm)
        out[nm] = [r.reshape(A[nm].shape) for r in res]

    scatter_wait(last, tot)
    late_sums = [_sum_slots([(0, slots(nm, 0))], 2, "sum_" + nm, into=part) for nm, part in zip(late, late_l1)]
    late_flight, token = sibling_start(late_sums, "late")
    after = finish(early, early_sums, early_flight, token, "early")
    finish(late, late_sums, late_flight, after, "late")

    return (loss, dx[None], *[out[nm][0] for nm in WEIGHTS], *[out[nm][1] for nm in WEIGHTS],
            *[out[nm][2] for nm in WEIGHTS], *[out[nm][3] for nm in WEIGHTS])
```

```python
import functools

import jax
import jax.numpy as jnp
import numpy as np
from jax import lax
from jax.experimental import pallas as pl
from jax.experimental.pallas import tpu as pltpu

F32 = jnp.float32
BF16 = jnp.bfloat16
MESH = pl.DeviceIdType.MESH

RMS_EPS = 1e-6
HEADS = 8
NOPE = 128
ROPE = 64
HEAD_PAD = 256
V_HEAD = 128
Q_LORA = 512
KV_LORA = 256
A_PAD = 896
CHUNK = 64
CONV_WIDTH = 31
CONV_HALO = 32
CONV_ROWS = 16
ROPE_THETA = 10000.0
ATTN_SCALE = (NOPE + ROPE) ** -0.5
FFN_RES = 0.5

ADAM_LR = 0.001
ADAM_B1 = 0.9
ADAM_B2 = 0.999
ADAM_EPS = 1e-08
ADAM_WD = 0.01
ADAM_STEP = 10

VMEM_LIMIT_BYTES = 56 * 1024 * 1024

WEIGHTS = ['ffn_norm1', 'ffn1_w1', 'ffn1_w3', 'ffn1_w2', 'mix_norm', 'ffn_norm2', 'ffn2_w1', 'ffn2_w3', 'ffn2_w2',
           'conv_w_pw1', 'conv_w_dw', 'conv_norm', 'conv_w_pw2', 'mla_w_a', 'mla_q_norm', 'mla_kv_norm', 'mla_w_uq',
           'mla_w_ukv', 'mla_w_o', 'final_norm']
INPUTS = (['x', 'positions'] + WEIGHTS + ['loss_target'] + ['m_' + w for w in WEIGHTS] + ['v_' + w for w in WEIGHTS])
BIG = ['ffn1_w1', 'ffn1_w3', 'ffn1_w2', 'ffn2_w1', 'ffn2_w3', 'ffn2_w2', 'conv_w_pw1', 'conv_w_pw2', 'mla_w_a',
       'mla_w_uq', 'mla_w_ukv', 'mla_w_o']


def _params(*sem):
    return pltpu.CompilerParams(dimension_semantics=sem, vmem_limit_bytes=VMEM_LIMIT_BYTES)


def _bf(v):
    return v.astype(BF16)


def _rstd(x):
    return lax.rsqrt(jnp.mean(x * x, axis=-1, keepdims=True) + RMS_EPS)


def _sigmoid(x):
    return jax.nn.sigmoid(x)


def _rot(x):
    lane = lax.broadcasted_iota(jnp.int32, x.shape, 1)
    return jnp.where(lane < ROPE // 2, -pltpu.roll(x, 128 - ROPE // 2, 1), pltpu.roll(x, ROPE // 2, 1))


def _rot_t(y):
    lane = lax.broadcasted_iota(jnp.int32, y.shape, 1)
    return jnp.where(lane < ROPE // 2, pltpu.roll(y, 128 - ROPE // 2, 1), -pltpu.roll(y, ROPE // 2, 1))


def _pair_sum(a_refs, b_refs, trans_b):
    tot = None
    for a_r, b_r in zip(a_refs, b_refs):
        a, b = _bf(a_r[...]), _bf(b_r[...])
        if trans_b:
            d = lax.dot_general(a, b, (((1,), (1,)), ((), ())), preferred_element_type=F32)
        else:
            d = jnp.dot(a, b, preferred_element_type=F32)
        tot = d if tot is None else tot + d
    return tot


def _mm(pairs, out_dtype, name, *, trans_b=False, tm=512, tn=None, tk=None, res=None, dep=None):
    m, k = pairs[0][0].shape
    n = pairs[0][1].shape[0] if trans_b else pairs[0][1].shape[1]
    tm, tn, tk = min(tm, m), tn or n, tk or k
    nk, npair = k // tk, len(pairs)

    def kern(*refs):
        a_refs, b_refs = refs[:npair], refs[npair:2 * npair]
        rest = list(refs[2 * npair:])
        res_ref = rest.pop(0) if res is not None else None
        if dep is not None:
            rest.pop(0)
        o_ref = rest.pop(0)

        def finish(acc):
            if res_ref is not None:
                acc = res_ref[...] + acc
            o_ref[...] = acc.astype(o_ref.dtype)

        if nk == 1:
            finish(_pair_sum(a_refs, b_refs, trans_b))
        else:
            acc_ref = rest.pop(0)
            kk = pl.program_id(2)

            @pl.when(kk == 0)
            def _():
                acc_ref[...] = jnp.zeros_like(acc_ref)

            acc_ref[...] += _pair_sum(a_refs, b_refs, trans_b)

            @pl.when(kk == nk - 1)
            def _():
                finish(acc_ref[...])

    a_spec = pl.BlockSpec((tm, tk), lambda i, j, kk: (i, kk))
    b_spec = (pl.BlockSpec((tn, tk), lambda i, j, kk: (j, kk)) if trans_b
              else pl.BlockSpec((tk, tn), lambda i, j, kk: (kk, j)))
    io_spec = pl.BlockSpec((tm, tn), lambda i, j, kk: (i, j))
    in_specs = ([a_spec] * npair + [b_spec] * npair + ([io_spec] if res is not None else [])
                + ([pl.BlockSpec((8, 128), lambda i, j, kk: (0, 0))] if dep is not None else []))
    args = ([p[0] for p in pairs] + [p[1] for p in pairs] + ([res] if res is not None else [])
            + ([dep] if dep is not None else []))
    return pl.pallas_call(
        kern, name=name, grid=(m // tm, n // tn, nk), in_specs=in_specs, out_specs=io_spec,
        out_shape=jax.ShapeDtypeStruct((m, n), out_dtype),
        scratch_shapes=[pltpu.VMEM((tm, tn), F32)] if nk > 1 else [],
        compiler_params=_params("parallel", "parallel", "arbitrary"))(*args)


def _mm_normbwd(pairs, h, g, dres, dep, name, *, tm=512, tk=None):
    m, k = pairs[0][0].shape
    d = pairs[0][1].shape[0]
    tm, tk = min(tm, m), tk or k
    nk, npair = k // tk, len(pairs)

    def kern(*refs):
        a_refs, b_refs = refs[:npair], refs[npair:2 * npair]
        h_ref, g_ref, dres_ref, _, o_ref, dg_ref, acc_ref = refs[2 * npair:]
        i, kk = pl.program_id(0), pl.program_id(1)

        @pl.when(jnp.logical_and(i == 0, kk == 0))
        def _():
            dg_ref[...] = jnp.zeros_like(dg_ref)

        @pl.when(kk == 0)
        def _():
            acc_ref[...] = jnp.zeros_like(acc_ref)

        acc_ref[...] += _pair_sum(a_refs, b_refs, True)

        @pl.when(kk == nk - 1)
        def _():
            dn = acc_ref[...]
            x = h_ref[...]
            rstd = _rstd(x)
            xhat = x * rstd
            dg_ref[...] += jnp.broadcast_to(jnp.sum(dn * xhat, axis=0, keepdims=True), dg_ref.shape)
            dxh = dn * g_ref[...]
            dx = rstd * (dxh - xhat * jnp.mean(dxh * xhat, axis=-1, keepdims=True))
            o_ref[...] = dres_ref[...] + dx

    row = pl.BlockSpec((tm, d), lambda i, kk: (i, 0))
    in_specs = ([pl.BlockSpec((tm, tk), lambda i, kk: (i, kk))] * npair
                + [pl.BlockSpec((d, tk), lambda i, kk: (0, kk))] * npair
                + [row, pl.BlockSpec((1, d), lambda i, kk: (0, 0)), row, pl.BlockSpec((8, 128), lambda i, kk: (0, 0))])
    return pl.pallas_call(
        kern, name=name, grid=(m // tm, nk), in_specs=in_specs,
        out_specs=[row, pl.BlockSpec((8, d), lambda i, kk: (0, 0))],
        out_shape=[jax.ShapeDtypeStruct((m, d), F32), jax.ShapeDtypeStruct((8, d), F32)],
        scratch_shapes=[pltpu.VMEM((tm, d), F32)],
        compiler_params=_params("arbitrary", "arbitrary"))(
            *[p[0] for p in pairs], *[p[1] for p in pairs], h, g.reshape(1, d), dres, dep)


def _mm_tn(a, b, out_dtype, name, *, bm=None, bn=None, tk=1024):
    t, m = a.shape
    batched = b.ndim == 3
    n = b.shape[-1]
    nb = b.shape[0] if batched else 1
    bm, bn, tk = bm or m, bn or n, min(tk, t)
    nk = t // tk

    def kern(a_ref, b_ref, o_ref, acc_ref):
        kk = pl.program_id(3)

        @pl.when(kk == 0)
        def _():
            acc_ref[...] = jnp.zeros_like(acc_ref)

        acc_ref[...] += lax.dot_general(_bf(a_ref[...]), _bf(b_ref[...]), (((0,), (0,)), ((), ())),
                                        preferred_element_type=F32)

        @pl.when(kk == nk - 1)
        def _():
            o_ref[...] = acc_ref[...].astype(o_ref.dtype)

    a_spec = pl.BlockSpec((tk, bm), lambda h, i, j, kk: (kk, i))
    if batched:
        b_spec = pl.BlockSpec((None, tk, bn), lambda h, i, j, kk: (h, kk, j))
        o_spec = pl.BlockSpec((None, bm, bn), lambda h, i, j, kk: (h, i, j))
        out_shape = jax.ShapeDtypeStruct((nb, m, n), out_dtype)
    else:
        b_spec = pl.BlockSpec((tk, bn), lambda h, i, j, kk: (kk, j))
        o_spec = pl.BlockSpec((bm, bn), lambda h, i, j, kk: (i, j))
        out_shape = jax.ShapeDtypeStruct((m, n), out_dtype)
    return pl.pallas_call(
        kern, name=name, grid=(nb, m // bm, n // bn, nk), in_specs=[a_spec, b_spec], out_specs=o_spec,
        out_shape=out_shape, scratch_shapes=[pltpu.VMEM((bm, bn), F32)],
        compiler_params=_params("parallel", "parallel", "parallel", "arbitrary"))(a, b)


def _ffn_tile(f):
    return f // 2 if (f // 2) % 128 == 0 else f


def _ffn_fwd(h, g, w1t, w3t, w2, dep, name):
    t, d = h.shape
    f = w1t.shape[0]
    tm = min(256, t)
    nt = (((1,), (1,)), ((), ()))

    def kern(h_ref, g_ref, w1_hbm, w3_hbm, w2_hbm, dep_ref, ho_ref, n_ref, z1_ref, z3_ref,
             w1_ref, w3_ref, w2_ref, sems):
        @pl.when(pl.program_id(0) == 0)
        def _():
            copies = [pltpu.make_async_copy(src, dst, sems.at[k]) for k, (src, dst) in
                      enumerate(((w1_hbm, w1_ref), (w3_hbm, w3_ref), (w2_hbm, w2_ref)))]
            for cp in copies:
                cp.start()
            for cp in copies:
                cp.wait()

        x = h_ref[...]
        n = _bf(x * _rstd(x) * g_ref[...])
        n_ref[...] = n
        z1 = lax.dot_general(n, w1_ref[...], nt, preferred_element_type=F32)
        z3 = lax.dot_general(n, w3_ref[...], nt, preferred_element_type=F32)
        z1_ref[...] = _bf(z1)
        z3_ref[...] = _bf(z3)
        act = _bf(z1 * _sigmoid(z1) * z3)
        ho_ref[...] = x + FFN_RES * jnp.dot(act, w2_ref[...], preferred_element_type=F32)

    row = pl.BlockSpec((tm, d), lambda i: (i, 0))
    col = pl.BlockSpec((tm, f), lambda i: (i, 0))
    whole = pl.BlockSpec(memory_space=pl.ANY)
    return pl.pallas_call(
        kern, name=name, grid=(t // tm,),
        in_specs=[row, pl.BlockSpec((1, d), lambda i: (0, 0)), whole, whole, whole,
                  pl.BlockSpec((8, 128), lambda i: (0, 0))],
        out_specs=[row, row, col, col],
        out_shape=[jax.ShapeDtypeStruct((t, d), F32), jax.ShapeDtypeStruct((t, d), BF16),
                   jax.ShapeDtypeStruct((t, f), BF16), jax.ShapeDtypeStruct((t, f), BF16)],
        scratch_shapes=[pltpu.VMEM((f, d), BF16), pltpu.VMEM((f, d), BF16), pltpu.VMEM((f, d), BF16),
                        pltpu.SemaphoreType.DMA((3,))],
        compiler_params=_params("arbitrary"))(h, g.reshape(1, d), w1t, w3t, w2, dep)


def _ffn_bwd_x(dh, h_in, g, z1, z3, w1t, w3t, w2, dep, name):
    t, d = dh.shape
    f = z1.shape[1]
    tm = min(256, t)

    def kern(dh_ref, h_ref, g_ref, z1_ref, z3_ref, w2_hbm, w1_hbm, w3_hbm, dep_ref,
             o_ref, dg_ref, dz1_ref, dz3_ref, a_ref, df_ref, w2_ref, w1_ref, w3_ref, sems):
        @pl.when(pl.program_id(0) == 0)
        def _():
            copies = [pltpu.make_async_copy(src, dst, sems.at[k]) for k, (src, dst) in
                      enumerate(((w2_hbm, w2_ref), (w1_hbm, w1_ref), (w3_hbm, w3_ref)))]
            for cp in copies:
                cp.start()
            dg_ref[...] = jnp.zeros_like(dg_ref)
            for cp in copies:
                cp.wait()

        df = _bf(FFN_RES * dh_ref[...])
        df_ref[...] = df
        da = lax.dot_general(df, w2_ref[...], (((1,), (1,)), ((), ())), preferred_element_type=F32)
        z1v, z3v = z1_ref[...].astype(F32), z3_ref[...].astype(F32)
        sig = _sigmoid(z1v)
        silu = z1v * sig
        a_ref[...] = _bf(silu * z3v)
        dz1 = _bf(da * z3v * (sig * (1.0 + z1v * (1.0 - sig))))
        dz3 = _bf(da * silu)
        dz1_ref[...] = dz1
        dz3_ref[...] = dz3
        dn = (jnp.dot(dz1, w1_ref[...], preferred_element_type=F32)
              + jnp.dot(dz3, w3_ref[...], preferred_element_type=F32))
        x = h_ref[...]
        rstd = _rstd(x)
        xhat = x * rstd
        dg_ref[...] += jnp.broadcast_to(jnp.sum(dn * xhat, axis=0, keepdims=True), dg_ref.shape)
        dxh = dn * g_ref[...]
        o_ref[...] = dh_ref[...] + rstd * (dxh - xhat * jnp.mean(dxh * xhat, axis=-1, keepdims=True))

    row = pl.BlockSpec((tm, d), lambda i: (i, 0))
    col = pl.BlockSpec((tm, f), lambda i: (i, 0))
    whole = pl.BlockSpec(memory_space=pl.ANY)
    colshape = jax.ShapeDtypeStruct((t, f), BF16)
    return pl.pallas_call(
        kern, name=name, grid=(t // tm,),
        in_specs=[row, row, pl.BlockSpec((1, d), lambda i: (0, 0)), col, col, whole, whole, whole,
                  pl.BlockSpec((8, 128), lambda i: (0, 0))],
        out_specs=[row, pl.BlockSpec((8, d), lambda i: (0, 0)), col, col, col, row],
        out_shape=[jax.ShapeDtypeStruct((t, d), F32), jax.ShapeDtypeStruct((8, d), F32), colshape, colshape, colshape,
                   jax.ShapeDtypeStruct((t, d), BF16)],
        scratch_shapes=[pltpu.VMEM((f, d), BF16), pltpu.VMEM((f, d), BF16), pltpu.VMEM((f, d), BF16),
                        pltpu.SemaphoreType.DMA((3,))],
        compiler_params=_params("arbitrary"))(dh, h_in, g.reshape(1, d), z1, z3, w2, w1t, w3t, dep)


def _ffn_bwd(dh, h_in, g, n, z1, z3, w1t, w3t, w2, dep, tag):
    f = w2.shape[0]
    dh_in, dg, dz1, dz3, act, df = _ffn_bwd_x(dh, h_in, g, z1, z3, w1t, w3t, w2, dep, tag + "_bwd_x")
    dw1t = _mm_tn(dz1, n, BF16, tag + "_dw1", bm=_ffn_tile(f), tk=2048)
    dw3t = _mm_tn(dz3, n, BF16, tag + "_dw3", bm=_ffn_tile(f), tk=2048)
    dw2 = _mm_tn(act, df, BF16, tag + "_dw2", bm=_ffn_tile(f), tk=2048)
    return dh_in, dg, dw1t, dw3t, dw2


def _norm_fwd(h, g, dep, name):
    t, d = h.shape
    tm = min(512, t)

    def kern(h_ref, g_ref, dep_ref, o_ref):
        x = h_ref[...]
        o_ref[...] = _bf(x * _rstd(x) * g_ref[...])

    row = pl.BlockSpec((tm, d), lambda i: (i, 0))
    return pl.pallas_call(
        kern, name=name, grid=(t // tm,),
        in_specs=[row, pl.BlockSpec((1, d), lambda i: (0, 0)), pl.BlockSpec((8, 128), lambda i: (0, 0))],
        out_specs=row, out_shape=jax.ShapeDtypeStruct((t, d), BF16),
        compiler_params=_params("parallel"))(h, g.reshape(1, d), dep)


def _loss_bwd(h, target, g):
    t, d = h.shape
    tm = min(512, t)

    def kern(h_ref, t_ref, g_ref, dh_ref, dg_ref, loss_ref):
        @pl.when(pl.program_id(0) == 0)
        def _():
            dg_ref[...] = jnp.zeros_like(dg_ref)
            loss_ref[...] = jnp.zeros_like(loss_ref)

        x = h_ref[...]
        rstd = _rstd(x)
        xhat = x * rstd
        err = xhat * g_ref[...] - t_ref[...]
        row_loss = jnp.sum(err * err, axis=-1, keepdims=True) * (0.5 / d)
        loss_ref[...] += jnp.broadcast_to(jnp.sum(row_loss, axis=0, keepdims=True), loss_ref.shape)
        dy = err * (1.0 / d)
        dg_ref[...] += jnp.broadcast_to(jnp.sum(dy * xhat, axis=0, keepdims=True), dg_ref.shape)
        dxh = dy * g_ref[...]
        dh_ref[...] = rstd * (dxh - xhat * jnp.mean(dxh * xhat, axis=-1, keepdims=True))

    row = pl.BlockSpec((tm, d), lambda i: (i, 0))
    return pl.pallas_call(
        kern, name="loss_bwd", grid=(t // tm,),
        in_specs=[row, row, pl.BlockSpec((1, d), lambda i: (0, 0))],
        out_specs=[row, pl.BlockSpec((8, d), lambda i: (0, 0)), pl.BlockSpec((8, 128), lambda i: (0, 0))],
        out_shape=[jax.ShapeDtypeStruct((t, d), F32), jax.ShapeDtypeStruct((8, d), F32),
                   jax.ShapeDtypeStruct((8, 128), F32)],
        compiler_params=_params("arbitrary"))(h, target, g.reshape(1, d))


def _glu_fwd(m, wa, wb):
    t, d = m.shape
    c = wa.shape[1]
    tm, tc = min(512, t), min(512, c)

    def kern(m_ref, wa_ref, wb_ref, a_ref, b_ref, glu_ref):
        mv = m_ref[...]
        a = jnp.dot(mv, wa_ref[...], preferred_element_type=F32)
        b = jnp.dot(mv, wb_ref[...], preferred_element_type=F32)
        a_ref[...] = _bf(a)
        b_ref[...] = _bf(b)
        glu_ref[...] = _bf(a * _sigmoid(b))

    col = pl.BlockSpec((tm, tc), lambda i, j: (i, j))
    wspec = pl.BlockSpec((d, tc), lambda i, j: (0, j))
    shape = jax.ShapeDtypeStruct((t, c), BF16)
    return pl.pallas_call(
        kern, name="conv_glu_fwd", grid=(t // tm, c // tc),
        in_specs=[pl.BlockSpec((tm, d), lambda i, j: (i, 0)), wspec, wspec], out_specs=[col, col, col],
        out_shape=[shape, shape, shape], compiler_params=_params("parallel", "parallel"))(m, wa, wb)


def _conv_tile(t):
    return min(256, t)


def _shift_copies(ext, shifted, rows):
    for s in range(8):
        shifted[s] = ext[pl.ds(s, rows), :]


def _shifted_rows(shifted, start, nrows):
    return shifted[start % 8, pl.ds(start - start % 8, nrows), :]


def _conv_fwd(glu, w_dw, g):
    t, c = glu.shape
    tm = _conv_tile(t)
    hb = tm // CONV_HALO

    def kern(cur_ref, halo_ref, w_ref, g_ref, cv_ref, s_ref, ext, shifted):
        i = pl.program_id(0)
        ext[0:CONV_HALO, :] = jnp.where(i > 0, halo_ref[...].astype(F32), 0.0)
        ext[CONV_HALO:tm + CONV_HALO, :] = cur_ref[...].astype(F32)
        ext[tm + CONV_HALO:, :] = jnp.zeros((8, c), F32)
        _shift_copies(ext, shifted, tm + CONV_HALO)
        gv = g_ref[...]
        for r0 in range(0, tm, CONV_ROWS):
            acc = jnp.zeros((CONV_ROWS, c), F32)
            for k in range(CONV_WIDTH):
                acc = acc + _shifted_rows(shifted, r0 + 2 + k, CONV_ROWS) * w_ref[k:k + 1, :]
            cv_ref[r0:r0 + CONV_ROWS, :] = acc
            rn = acc * _rstd(acc) * gv
            s_ref[r0:r0 + CONV_ROWS, :] = _bf(rn * _sigmoid(rn))

    row = pl.BlockSpec((tm, c), lambda i: (i, 0))
    return pl.pallas_call(
        kern, name="conv_fwd", grid=(t // tm,),
        in_specs=[row, pl.BlockSpec((CONV_HALO, c), lambda i: (jnp.maximum(i * hb - 1, 0), 0)),
                  pl.BlockSpec((CONV_HALO, c), lambda i: (0, 0)), pl.BlockSpec((1, c), lambda i: (0, 0))],
        out_specs=[row, row],
        out_shape=[jax.ShapeDtypeStruct((t, c), F32), jax.ShapeDtypeStruct((t, c), BF16)],
        scratch_shapes=[pltpu.VMEM((tm + CONV_HALO + 8, c), F32), pltpu.VMEM((8, tm + CONV_HALO, c), F32)],
        compiler_params=_params("parallel"))(glu, glu, w_dw, g.reshape(1, c))


def _conv_bwd_norm(dh, cv, w_pw2, g, dep):
    t, c = cv.shape
    tm = min(512, t)

    def kern(dh_ref, cv_ref, w_ref, g_ref, dep_ref, dcv_ref, dg_ref):
        @pl.when(pl.program_id(0) == 0)
        def _():
            dg_ref[...] = jnp.zeros_like(dg_ref)

        ds = lax.dot_general(_bf(dh_ref[...]), w_ref[...], (((1,), (1,)), ((), ())), preferred_element_type=F32)
        x = cv_ref[...]
        rstd = _rstd(x)
        xhat = x * rstd
        rn = xhat * g_ref[...]
        sig = _sigmoid(rn)
        drn = ds * (sig * (1.0 + rn * (1.0 - sig)))
        dg_ref[...] += jnp.broadcast_to(jnp.sum(drn * xhat, axis=0, keepdims=True), dg_ref.shape)
        dxh = drn * g_ref[...]
        dcv_ref[...] = rstd * (dxh - xhat * jnp.mean(dxh * xhat, axis=-1, keepdims=True))

    row = pl.BlockSpec((tm, c), lambda i: (i, 0))
    return pl.pallas_call(
        kern, name="conv_bwd_norm", grid=(t // tm,),
        in_specs=[pl.BlockSpec((tm, dh.shape[1]), lambda i: (i, 0)), row,
                  pl.BlockSpec(w_pw2.shape, lambda i: (0, 0)), pl.BlockSpec((1, c), lambda i: (0, 0)),
                  pl.BlockSpec((8, 128), lambda i: (0, 0))],
        out_specs=[row, pl.BlockSpec((8, c), lambda i: (0, 0))],
        out_shape=[jax.ShapeDtypeStruct((t, c), F32), jax.ShapeDtypeStruct((8, c), F32)],
        compiler_params=_params("arbitrary"))(dh, cv, w_pw2, g.reshape(1, c), dep)


def _conv_bwd_dw(dcv, glu, a, b, w_dw):
    t, c = dcv.shape
    tm = _conv_tile(t)
    hb = tm // CONV_HALO
    last = t // CONV_HALO - 1

    def kern(dcv_ref, dnext_ref, glu_ref, gprev_ref, a_ref, b_ref, w_ref, da_ref, db_ref, dw_ref,
             dext, gext, dshift, gshift):
        i = pl.program_id(0)

        @pl.when(i == 0)
        def _():
            dw_ref[...] = jnp.zeros_like(dw_ref)

        dext[0:tm, :] = dcv_ref[...]
        dext[tm:tm + CONV_HALO, :] = jnp.where(i < t // tm - 1, dnext_ref[...], 0.0)
        dext[tm + CONV_HALO:, :] = jnp.zeros((8, c), F32)
        gext[0:CONV_HALO, :] = jnp.where(i > 0, gprev_ref[...].astype(F32), 0.0)
        gext[CONV_HALO:tm + CONV_HALO, :] = glu_ref[...].astype(F32)
        gext[tm + CONV_HALO:, :] = jnp.zeros((8, c), F32)
        _shift_copies(dext, dshift, tm + CONV_HALO)
        _shift_copies(gext, gshift, tm + CONV_HALO)
        for r0 in range(0, tm, CONV_ROWS):
            acc = jnp.zeros((CONV_ROWS, c), F32)
            for k in range(CONV_WIDTH):
                acc = acc + _shifted_rows(dshift, r0 + CONV_WIDTH - 1 - k, CONV_ROWS) * w_ref[k:k + 1, :]
            av = a_ref[r0:r0 + CONV_ROWS, :].astype(F32)
            sig = _sigmoid(b_ref[r0:r0 + CONV_ROWS, :].astype(F32))
            da_ref[r0:r0 + CONV_ROWS, :] = _bf(acc * sig)
            db_ref[r0:r0 + CONV_ROWS, :] = _bf(acc * av * sig * (1.0 - sig))
        for k in range(CONV_WIDTH):
            acc = jnp.zeros((CONV_ROWS, c), F32)
            for r0 in range(0, tm, CONV_ROWS):
                acc = acc + _shifted_rows(gshift, r0 + 2 + k, CONV_ROWS) * dext[r0:r0 + CONV_ROWS, :]
            dw_ref[k:k + 1, :] += jnp.sum(acc, axis=0, keepdims=True)

    row = pl.BlockSpec((tm, c), lambda i: (i, 0))
    shape = jax.ShapeDtypeStruct((t, c), BF16)
    return pl.pallas_call(
        kern, name="conv_bwd_dw", grid=(t // tm,),
        in_specs=[row, pl.BlockSpec((CONV_HALO, c), lambda i: (jnp.minimum((i + 1) * hb, last), 0)),
                  row, pl.BlockSpec((CONV_HALO, c), lambda i: (jnp.maximum(i * hb - 1, 0), 0)),
                  row, row, pl.BlockSpec((CONV_HALO, c), lambda i: (0, 0))],
        out_specs=[row, row, pl.BlockSpec((CONV_HALO, c), lambda i: (0, 0))],
        out_shape=[shape, shape, jax.ShapeDtypeStruct((CONV_HALO, c), F32)],
        scratch_shapes=[pltpu.VMEM((tm + CONV_HALO + 8, c), F32), pltpu.VMEM((tm + CONV_HALO + 8, c), F32),
                        pltpu.VMEM((8, tm + CONV_HALO, c), F32), pltpu.VMEM((8, tm + CONV_HALO, c), F32)],
        compiler_params=_params("arbitrary"))(dcv, dcv, glu, glu, a, b, w_dw)


def _rope_tables(pos):
    t = pos.shape[0]
    tm = min(512, t)
    freq = (np.float32(ROPE_THETA) ** (np.float32(-2.0) * np.arange(ROPE // 2, dtype=np.float32)
                                       / np.float32(ROPE))).astype(np.float32)
    row = np.zeros((2, 128), np.float32)
    row[0, :ROPE] = np.concatenate([freq, freq])
    row[1, :ROPE] = 1.0

    def kern(pos_ref, f_ref, c_ref, s_ref):
        ang = pos_ref[...].astype(F32) * f_ref[0:1, :]
        mask = f_ref[1:2, :]
        c_ref[...] = jnp.cos(ang) * mask
        s_ref[...] = jnp.sin(ang) * mask

    out = pl.BlockSpec((tm, 128), lambda i: (i, 0))
    shape = jax.ShapeDtypeStruct((t, 128), F32)
    return pl.pallas_call(
        kern, name="rope_tables", grid=(t // tm,),
        in_specs=[pl.BlockSpec((tm, 1), lambda i: (i, 0)), pl.BlockSpec((2, 128), lambda i: (0, 0))],
        out_specs=[out, out], out_shape=[shape, shape], compiler_params=_params("parallel"))(pos, jnp.asarray(row))


def _mla_prep(a, gq, gkv, cs_c, cs_s):
    t = a.shape[0]
    tm = min(512, t)
    kv0, r0 = Q_LORA, Q_LORA + KV_LORA

    def kern(a_ref, gq_ref, gkv_ref, c_ref, s_ref, cq_ref, ckv_ref, kr_ref):
        aq = a_ref[:, 0:kv0]
        akv = a_ref[:, kv0:r0]
        ar = a_ref[:, r0:A_PAD]
        cq_ref[...] = _bf(aq * _rstd(aq) * gq_ref[...])
        ckv_ref[...] = _bf(akv * _rstd(akv) * gkv_ref[...])
        kr_ref[...] = _bf(ar * c_ref[...] + _rot(ar) * s_ref[...])

    def row(w):
        return pl.BlockSpec((tm, w), lambda i: (i, 0))

    def vec(w):
        return pl.BlockSpec((1, w), lambda i: (0, 0))

    return pl.pallas_call(
        kern, name="mla_prep", grid=(t // tm,),
        in_specs=[row(A_PAD), vec(Q_LORA), vec(KV_LORA), row(128), row(128)],
        out_specs=[row(Q_LORA), row(KV_LORA), row(128)],
        out_shape=[jax.ShapeDtypeStruct((t, Q_LORA), BF16), jax.ShapeDtypeStruct((t, KV_LORA), BF16),
                   jax.ShapeDtypeStruct((t, 128), BF16)],
        compiler_params=_params("parallel"))(a, gq.reshape(1, -1), gkv.reshape(1, -1), cs_c, cs_s)


def _mla_prep_bwd(a, dcq, dckv, dar, gq, gkv):
    t = a.shape[0]
    tm = min(512, t)
    kv0, r0 = Q_LORA, Q_LORA + KV_LORA

    def kern(a_ref, dcq_ref, dckv_ref, dar_ref, gq_ref, gkv_ref, da_ref, dgq_ref, dgkv_ref):
        @pl.when(pl.program_id(0) == 0)
        def _():
            dgq_ref[...] = jnp.zeros_like(dgq_ref)
            dgkv_ref[...] = jnp.zeros_like(dgkv_ref)

        def back(x, dy, g_ref, dg_ref):
            rstd = _rstd(x)
            xhat = x * rstd
            dg_ref[...] += jnp.broadcast_to(jnp.sum(dy * xhat, axis=0, keepdims=True), dg_ref.shape)
            dxh = dy * g_ref[...]
            return rstd * (dxh - xhat * jnp.mean(dxh * xhat, axis=-1, keepdims=True))

        da_ref[:, 0:kv0] = _bf(back(a_ref[:, 0:kv0], dcq_ref[...], gq_ref, dgq_ref))
        da_ref[:, kv0:r0] = _bf(back(a_ref[:, kv0:r0], dckv_ref[...], gkv_ref, dgkv_ref))
        da_ref[:, r0:A_PAD] = _bf(dar_ref[...])

    def row(w):
        return pl.BlockSpec((tm, w), lambda i: (i, 0))

    def vec(r, w):
        return pl.BlockSpec((r, w), lambda i: (0, 0))

    return pl.pallas_call(
        kern, name="mla_prep_bwd", grid=(t // tm,),
        in_specs=[row(A_PAD), row(Q_LORA), row(KV_LORA), row(128), vec(1, Q_LORA), vec(1, KV_LORA)],
        out_specs=[row(A_PAD), vec(8, Q_LORA), vec(8, KV_LORA)],
        out_shape=[jax.ShapeDtypeStruct((t, A_PAD), BF16), jax.ShapeDtypeStruct((8, Q_LORA), F32),
                   jax.ShapeDtypeStruct((8, KV_LORA), F32)],
        compiler_params=_params("arbitrary"))(a, dcq, dckv, dar, gq.reshape(1, -1), gkv.reshape(1, -1))


def _mla_qkv(cq, ckv, kr, cs_c, cs_s, wuq, wukv):
    t = cq.shape[0]
    tm = min(512, t)
    kvw = NOPE + V_HEAD

    def kern(cq_ref, ckv_ref, kr_ref, c_ref, s_ref, wq_ref, wkv_ref, q_ref, k_ref, v_ref):
        r = jnp.dot(cq_ref[...], wq_ref[...], preferred_element_type=F32)
        kv = jnp.dot(ckv_ref[...], wkv_ref[...], preferred_element_type=F32)
        cv, sv, krv = c_ref[...], s_ref[...], kr_ref[...]
        for h in range(HEADS):
            xr = r[:, h * HEAD_PAD + NOPE:(h + 1) * HEAD_PAD]
            q_ref[h, :, 0:NOPE] = _bf(r[:, h * HEAD_PAD:h * HEAD_PAD + NOPE] * ATTN_SCALE)
            q_ref[h, :, NOPE:] = _bf((xr * cv + _rot(xr) * sv) * ATTN_SCALE)
            k_ref[h, :, 0:NOPE] = _bf(kv[:, h * kvw:h * kvw + NOPE])
            k_ref[h, :, NOPE:] = krv
            v_ref[h] = _bf(kv[:, h * kvw + NOPE:(h + 1) * kvw])

    def row(w):
        return pl.BlockSpec((tm, w), lambda i: (i, 0))

    def heads(w):
        return pl.BlockSpec((HEADS, tm, w), lambda i: (0, i, 0))

    return pl.pallas_call(
        kern, name="mla_qkv", grid=(t // tm,),
        in_specs=[row(Q_LORA), row(KV_LORA), row(128), row(128), row(128),
                  pl.BlockSpec(wuq.shape, lambda i: (0, 0)), pl.BlockSpec(wukv.shape, lambda i: (0, 0))],
        out_specs=[heads(HEAD_PAD), heads(HEAD_PAD), heads(V_HEAD)],
        out_shape=[jax.ShapeDtypeStruct((HEADS, t, HEAD_PAD), BF16), jax.ShapeDtypeStruct((HEADS, t, HEAD_PAD), BF16),
                   jax.ShapeDtypeStruct((HEADS, t, V_HEAD), BF16)],
        compiler_params=_params("parallel"))(cq, ckv, kr, cs_c, cs_s, wuq, wukv)


def _mla_qkv_bwd(dq, dk, dv, cs_c, cs_s, wuq, wukv):
    t = dq.shape[1]
    tm = min(256, t)
    kvw = NOPE + V_HEAD

    def kern(dq_ref, dk_ref, dv_ref, c_ref, s_ref, wq_ref, wkv_ref, dr_ref, dkv_ref, dcq_ref, dckv_ref, dar_ref):
        cv, sv = c_ref[...], s_ref[...]
        dar = jnp.zeros_like(cv)
        for h in range(HEADS):
            dqx = dq_ref[h, :, NOPE:]
            dr_ref[:, h * HEAD_PAD:h * HEAD_PAD + NOPE] = _bf(dq_ref[h, :, 0:NOPE] * ATTN_SCALE)
            dr_ref[:, h * HEAD_PAD + NOPE:(h + 1) * HEAD_PAD] = _bf((dqx * cv + _rot_t(dqx * sv)) * ATTN_SCALE)
            dkx = dk_ref[h, :, NOPE:]
            dar = dar + (dkx * cv + _rot_t(dkx * sv))
            dkv_ref[:, h * kvw:h * kvw + NOPE] = _bf(dk_ref[h, :, 0:NOPE])
            dkv_ref[:, h * kvw + NOPE:(h + 1) * kvw] = _bf(dv_ref[h])
        dar_ref[...] = dar
        dcq_ref[...] = lax.dot_general(dr_ref[...], wq_ref[...], (((1,), (1,)), ((), ())),
                                       preferred_element_type=F32)
        dckv_ref[...] = lax.dot_general(dkv_ref[...], wkv_ref[...], (((1,), (1,)), ((), ())),
                                        preferred_element_type=F32)

    def row(w):
        return pl.BlockSpec((tm, w), lambda i: (i, 0))

    def heads(w):
        return pl.BlockSpec((HEADS, tm, w), lambda i: (0, i, 0))

    return pl.pallas_call(
        kern, name="mla_qkv_bwd", grid=(t // tm,),
        in_specs=[heads(HEAD_PAD), heads(HEAD_PAD), heads(V_HEAD), row(128), row(128),
                  pl.BlockSpec(wuq.shape, lambda i: (0, 0)), pl.BlockSpec(wukv.shape, lambda i: (0, 0))],
        out_specs=[row(HEADS * HEAD_PAD), row(HEADS * kvw), row(Q_LORA), row(KV_LORA), row(128)],
        out_shape=[jax.ShapeDtypeStruct((t, HEADS * HEAD_PAD), BF16), jax.ShapeDtypeStruct((t, HEADS * kvw), BF16),
                   jax.ShapeDtypeStruct((t, Q_LORA), F32), jax.ShapeDtypeStruct((t, KV_LORA), F32),
                   jax.ShapeDtypeStruct((t, 128), F32)],
        compiler_params=_params("parallel"))(dq, dk, dv, cs_c, cs_s, wuq, wukv)


def _attn_block(t):
    return 512 if t >= 4096 else 128


def _chunk_mask(bk, bq):
    kc = lax.broadcasted_iota(jnp.int32, (bk, bq), 0) // CHUNK
    qc = lax.broadcasted_iota(jnp.int32, (bk, bq), 1) // CHUNK
    return qc >= kc


def _flash_fwd(q, k, v):
    t = q.shape[1]
    bq = _attn_block(t)
    nq = t // bq
    nch = 2

    def kern(q_ref, k_ref, v_ref, o_ref, lse_ref, s_buf, p_buf, m_ref, l_ref, acc_ref):
        i = pl.program_id(1)
        queries = [q_ref[c * bq:(c + 1) * bq, :] for c in range(nch)]

        def block(j):
            rows = pl.ds(pl.multiple_of(j * bq, bq), bq)
            return k_ref[rows, :], v_ref[rows, :]

        def scores(kj, chain):
            return lax.dot_general(kj, queries[chain], (((1,), (1,)), ((), ())), preferred_element_type=F32)

        def softmax_block(chain, slot, vj):
            for c0 in range(0, bq, 128):
                cols = slice(c0, c0 + 128)
                s = s_buf[slot, chain, :, cols]
                m_old = m_ref[chain, 0:1, cols]
                m_new = jnp.maximum(m_old, jnp.max(s, axis=0, keepdims=True))
                alpha = jnp.exp(m_old - m_new)
                p = jnp.exp(s - m_new)
                l_ref[chain, 0:1, cols] = alpha * l_ref[chain, 0:1, cols] + jnp.sum(p, axis=0, keepdims=True)
                m_ref[chain, 0:1, cols] = m_new
                p_buf[chain, :, cols] = _bf(p)
                acc_ref[chain, :, cols] = acc_ref[chain, :, cols] * alpha
            acc_ref[chain] += lax.dot_general(vj, p_buf[chain], (((0,), (0,)), ((), ())),
                                              preferred_element_type=F32)

        m_ref[...] = jnp.full(m_ref.shape, -1e30, F32)
        l_ref[...] = jnp.zeros_like(l_ref)
        acc_ref[...] = jnp.zeros_like(acc_ref)
        mask = _chunk_mask(bq, bq)
        for b in range(nch):
            kb, vb = block(nch * i + b)
            for c in range(b, nch):
                s_buf[b % 2, c] = jnp.where(mask, scores(kb, c), -1e30) if c == b else scores(kb, c)
                softmax_block(c, b % 2, vb)
        kf = block(0)[0]
        for c in range(nch):
            s_buf[0, c] = scores(kf, c)

        def body(pair, carry):
            for cur in range(2):
                j = 2 * pair + cur
                kn = block(jnp.minimum(j + 1, jnp.maximum(nch * i - 1, 0)))[0]
                for c in range(nch):
                    s_buf[1 - cur, c] = scores(kn, c)
                vj = block(j)[1]
                for c in range(nch):
                    softmax_block(c, cur, vj)
            return carry

        lax.fori_loop(0, (nch // 2) * i, body, 0)
        for chain in range(nch):
            l = l_ref[chain, 0:1, :]
            o_ref[chain * bq:(chain + 1) * bq, :] = _bf((acc_ref[chain] / l).T)
            lse_ref[chain] = jnp.broadcast_to(m_ref[chain, 0:1, :] + jnp.log(l), (8, bq))

    return pl.pallas_call(
        kern, name="flash_fwd", grid=(HEADS, nq // nch),
        in_specs=[pl.BlockSpec((None, nch * bq, HEAD_PAD), lambda h, i: (h, i, 0)),
                  pl.BlockSpec((None, t, HEAD_PAD), lambda h, i: (h, 0, 0)),
                  pl.BlockSpec((None, t, V_HEAD), lambda h, i: (h, 0, 0))],
        out_specs=[pl.BlockSpec((nch * bq, V_HEAD), lambda h, i: (i, h)),
                   pl.BlockSpec((None, nch, 8, bq), lambda h, i: (h, i, 0, 0))],
        out_shape=[jax.ShapeDtypeStruct((t, HEADS * V_HEAD), BF16), jax.ShapeDtypeStruct((HEADS, nq, 8, bq), F32)],
        scratch_shapes=[pltpu.VMEM((2, nch, bq, bq), F32), pltpu.VMEM((nch, bq, bq), BF16),
                        pltpu.VMEM((nch, 8, bq), F32), pltpu.VMEM((nch, 8, bq), F32),
                        pltpu.VMEM((nch, V_HEAD, bq), F32)],
        compiler_params=_params("parallel", "arbitrary"))(q, k, v)


def _attn_delta(do, o):
    t = do.shape[0]
    bq = _attn_block(t)

    def kern(do_ref, o_ref, d_ref):
        for h in range(HEADS):
            cols = slice(h * V_HEAD, (h + 1) * V_HEAD)
            prod = do_ref[:, cols].astype(F32) * o_ref[:, cols].astype(F32)
            d_ref[h] = jnp.broadcast_to(jnp.sum(prod.T, axis=0, keepdims=True), (8, bq))

    blk = pl.BlockSpec((bq, HEADS * V_HEAD), lambda i: (i, 0))
    return pl.pallas_call(
        kern, name="attn_delta", grid=(t // bq,), in_specs=[blk, blk],
        out_specs=pl.BlockSpec((HEADS, None, 8, bq), lambda i: (0, i, 0, 0)),
        out_shape=jax.ShapeDtypeStruct((HEADS, t // bq, 8, bq), F32),
        compiler_params=_params("parallel"))(do, o)


def _flash_bwd(q, k, v, do, lse, delta):
    t = q.shape[1]
    bq = _attn_block(t)
    nq = t // bq

    def kern(q_ref, k_ref, v_ref, do_ref, lse_ref, del_ref, dq_ref, dk_ref, dv_ref, dvt_ref):
        j = pl.program_id(1)

        @pl.when(j == 0)
        def _():
            dq_ref[...] = jnp.zeros_like(dq_ref)

        dk_ref[...] = jnp.zeros_like(dk_ref)
        dvt_ref[...] = jnp.zeros_like(dvt_ref)
        kj, vj = k_ref[...], v_ref[...]

        def step(i, masked):
            rows = pl.ds(pl.multiple_of(i * bq, bq), bq)
            qi, doi = q_ref[rows, :], do_ref[rows, :]
            st = lax.dot_general(kj, qi, (((1,), (1,)), ((), ())), preferred_element_type=F32)
            pt = jnp.exp(st - lse_ref[i][0:1, :])
            if masked:
                pt = jnp.where(_chunk_mask(bq, bq), pt, 0.0)
            dpt = lax.dot_general(vj, doi, (((1,), (1,)), ((), ())), preferred_element_type=F32)
            dst = _bf(pt * (dpt - del_ref[i][0:1, :]))
            dvt_ref[...] += lax.dot_general(doi, _bf(pt), (((0,), (1,)), ((), ())), preferred_element_type=F32)
            dk_ref[...] += jnp.dot(dst, qi, preferred_element_type=F32)
            dq_ref[rows, :] += lax.dot_general(dst, kj, (((0,), (0,)), ((), ())), preferred_element_type=F32)

        step(j, True)

        def body(pair, carry):
            step(j + 1 + 2 * pair, False)
            step(j + 2 + 2 * pair, False)
            return carry

        rest = nq - 1 - j
        lax.fori_loop(0, rest // 2, body, 0)

        @pl.when(rest % 2 == 1)
        def _():
            step(nq - 1, False)

        dv_ref[...] = dvt_ref[...].T

    stat = pl.BlockSpec((None, nq, 8, bq), lambda h, j: (h, 0, 0, 0))
    return pl.pallas_call(
        kern, name="flash_bwd", grid=(HEADS, nq),
        in_specs=[pl.BlockSpec((None, t, HEAD_PAD), lambda h, j: (h, 0, 0)),
                  pl.BlockSpec((None, bq, HEAD_PAD), lambda h, j: (h, j, 0)),
                  pl.BlockSpec((None, bq, V_HEAD), lambda h, j: (h, j, 0)),
                  pl.BlockSpec((t, V_HEAD), lambda h, j: (0, h)), stat, stat],
        out_specs=[pl.BlockSpec((None, t, HEAD_PAD), lambda h, j: (h, 0, 0)),
                   pl.BlockSpec((None, bq, HEAD_PAD), lambda h, j: (h, j, 0)),
                   pl.BlockSpec((None, bq, V_HEAD), lambda h, j: (h, j, 0))],
        out_shape=[jax.ShapeDtypeStruct((HEADS, t, HEAD_PAD), F32), jax.ShapeDtypeStruct((HEADS, t, HEAD_PAD), F32),
                   jax.ShapeDtypeStruct((HEADS, t, V_HEAD), F32)],
        scratch_shapes=[pltpu.VMEM((V_HEAD, bq), F32)],
        compiler_params=_params("parallel", "arbitrary"))(q, k, v, do, lse, delta)


def _place():
    x, y, c = lax.axis_index("x"), lax.axis_index("y"), lax.axis_index("c")
    return x, y, c, [(1 - x, y), (x, 1 - y), (1 - x, 1 - y)]


def _all_gather_rows(block, name, dep=None):
    m_per, n = block.shape

    def body(x_ref, *rest):
        out_ref, send_sems, recv_sems, local_sem = rest[-4:]
        x, y, c, chips = _place()
        me, sibling = (x, y, c), (x, y, 1 - c)

        def rows(px, py, pc):
            return out_ref.at[pl.ds((4 * px + 2 * py + pc) * m_per, m_per), :]

        def copy(k, blk, to, src=None):
            return pltpu.make_async_remote_copy(
                src_ref=rows(*blk) if src is None else src, dst_ref=rows(*blk), send_sem=send_sems.at[k],
                recv_sem=recv_sems.at[k], device_id=to, device_id_type=MESH)

        mine = pltpu.make_async_copy(x_ref, rows(*me), local_sem)
        mine.start()
        first = [copy(0, me, sibling, src=x_ref)]
        first += [copy(1 + j, me, (*chip, c), src=x_ref) for j, chip in enumerate(chips)]
        for cp in first:
            cp.start()
        passed = [copy(4 + j, (*chip, c), sibling) for j, chip in enumerate(chips)]
        for j, chip in enumerate(chips):
            copy(1 + j, (*chip, c), me).wait_recv()
            passed[j].start()
        copy(0, sibling, me).wait_recv()
        for j, chip in enumerate(chips):
            copy(4 + j, (*chip, 1 - c), me).wait_recv()
        for cp in first + passed:
            cp.wait_send()
        mine.wait()

    return pl.pallas_call(
        body, name=name, out_shape=jax.ShapeDtypeStruct((8 * m_per, n), block.dtype),
        in_specs=[pl.BlockSpec(memory_space=pltpu.VMEM)] + ([pl.BlockSpec(memory_space=pl.ANY)] if dep is not None else []),
        out_specs=pl.BlockSpec(memory_space=pltpu.VMEM),
        scratch_shapes=[pltpu.SemaphoreType.DMA((7,)), pltpu.SemaphoreType.DMA((7,)), pltpu.SemaphoreType.DMA],
        compiler_params=pltpu.CompilerParams(vmem_limit_bytes=VMEM_LIMIT_BYTES))(
            *([block] if dep is None else [block, dep]))


HBM_SPEC = pl.BlockSpec(memory_space=pltpu.HBM)
SEM_SPEC = pl.BlockSpec(memory_space=pltpu.SEMAPHORE)
DATAFLOW = pltpu.SideEffectType.DATAFLOW_SIDE_EFFECTING


def _in_hbm(a):
    return pltpu.with_memory_space_constraint(a, pltpu.HBM)


def _chip_copies(ins, lands, send_sems, recv_sems, src_slot, half=False):
    n = len(ins)
    x, y, c, chips = _place()
    me = 2 * x + y
    if src_slot == "sibling":
        return [pltpu.make_async_remote_copy(src_ref=ins[w], dst_ref=lands[w], send_sem=send_sems.at[w],
                                             recv_sem=recv_sems.at[w], device_id=(x, y, 1 - c), device_id_type=MESH)
                for w in range(n)]

    def ends(w, chip):
        src = ins[w].at[2 * chip[0] + chip[1]] if src_slot else ins[w]
        if not half:
            return src, lands[w].at[me]
        rows = pl.ds(pl.multiple_of(c * (src.shape[0] // 2), 16), src.shape[0] // 2)
        return src.at[rows], lands[w].at[me, rows]

    copies = []
    for w in range(n):
        for p, chip in enumerate(chips):
            src, dst = ends(w, chip)
            copies.append(pltpu.make_async_remote_copy(
                src_ref=src, dst_ref=dst, send_sem=send_sems.at[p * n + w], recv_sem=recv_sems.at[p * n + w],
                device_id=(*chip, c), device_id_type=MESH))
    return copies


def _fill_halves(lands, name):
    n = len(lands)

    def body(*refs):
        bufs = refs[n:2 * n]
        send_sems, recv_sems = refs[2 * n:]
        x, y, c, chips = _place()
        copies = []
        for w in range(n):
            hr = bufs[w].shape[1] // 2
            for p, chip in enumerate(chips):
                part = bufs[w].at[2 * chip[0] + chip[1], pl.ds(pl.multiple_of(c * hr, 16), hr)]
                copies.append(pltpu.make_async_remote_copy(
                    src_ref=part, dst_ref=part, send_sem=send_sems.at[p * n + w], recv_sem=recv_sems.at[p * n + w],
                    device_id=(x, y, 1 - c), device_id_type=MESH))
        for cp in copies:
            cp.start()
        for cp in copies:
            cp.wait_send()
        for w in range(n):
            hr = bufs[w].shape[1] // 2
            for p, chip in enumerate(chips):
                part = bufs[w].at[2 * chip[0] + chip[1], pl.ds(pl.multiple_of((1 - c) * hr, 16), hr)]
                pltpu.make_async_remote_copy(
                    src_ref=part, dst_ref=part, send_sem=send_sems.at[p * n + w], recv_sem=recv_sems.at[p * n + w],
                    device_id=(x, y, 1 - c), device_id_type=MESH).wait_recv()

    any_spec = pl.BlockSpec(memory_space=pl.ANY)
    return list(pl.pallas_call(
        body, name=name, out_shape=[jax.ShapeDtypeStruct(a.shape, a.dtype) for a in lands],
        in_specs=[any_spec] * n, out_specs=[any_spec] * n, input_output_aliases={i: i for i in range(n)},
        scratch_shapes=[pltpu.SemaphoreType.DMA((3 * n,)), pltpu.SemaphoreType.DMA((3 * n,))])(*lands))


def _exchange_start(srcs, lands, src_slot, name, dep=None, half=False):
    n = len(srcs)
    first_out = 2 * n + (dep is not None)

    def body(*refs):
        for cp in _chip_copies(refs[:n], refs[n:2 * n], refs[first_out], refs[first_out + 1], src_slot, half):
            cp.start()
        token = refs[-1]
        token[...] = jnp.zeros_like(token)

    thru = [pltpu.HBM(a.shape, a.dtype) for a in list(srcs) + list(lands)]
    res = pl.pallas_call(
        body, name=name,
        out_shape=(pltpu.SemaphoreType.DMA((3 * n,)), pltpu.SemaphoreType.DMA((3 * n,)), *thru,
                   jax.ShapeDtypeStruct((8, 128), F32)),
        in_specs=[HBM_SPEC] * (2 * n) + ([pl.BlockSpec(memory_space=pl.ANY)] if dep is not None else []),
        out_specs=(SEM_SPEC, SEM_SPEC, *[HBM_SPEC] * (2 * n), pl.BlockSpec(memory_space=pltpu.VMEM)),
        input_output_aliases={i: 2 + i for i in range(2 * n)},
        compiler_params=pltpu.CompilerParams(has_side_effects=DATAFLOW))(
            *[_in_hbm(a) for a in srcs], *[_in_hbm(a) for a in lands], *([dep] if dep is not None else []))
    return (res[0], res[1], list(res[2:2 + n]), list(res[2 + n:2 + 2 * n])), res[-1]


def _exchange_wait(flight, after, src_slot, name, half=False):
    send_sems, recv_sems, srcs, lands = flight
    n = len(srcs)

    def body(*refs):
        for cp in _chip_copies(refs[:n], refs[n:2 * n], refs[2 * n], refs[2 * n + 1], src_slot, half):
            cp.wait_send()
            cp.wait_recv()

    thru = [pltpu.HBM(a.shape, a.dtype) for a in list(srcs) + list(lands)]
    res = pl.pallas_call(
        body, name=name, out_shape=thru,
        in_specs=[HBM_SPEC] * (2 * n) + [SEM_SPEC, SEM_SPEC, pl.BlockSpec(memory_space=pl.ANY)],
        out_specs=[HBM_SPEC] * (2 * n), input_output_aliases={i: i for i in range(2 * n)},
        compiler_params=pltpu.CompilerParams(has_side_effects=DATAFLOW))(*srcs, *lands, send_sems, recv_sems, after)
    return list(res[n:])


def _landing(own, me):
    return lax.dynamic_update_index_in_dim(lax.empty((4, *own.shape), own.dtype), own, me, 0)


def _as_rows(a):
    return a.reshape(-1, a.shape[-1])


def _row_tile(r, c, budget_bytes=1 << 20):
    tr = r
    while tr % 16 == 0 and tr * c * 4 > budget_bytes:
        tr //= 2
    return tr


def _sum_slots(layers, nlayer, name, into=None):
    _, r, c = layers[0][1].shape
    tr = _row_tile(r, c)
    nt = r // tr
    acc = into
    for l, r4 in layers:
        def kern(r_ref, *rest):
            o_ref = rest[-1]
            o_ref[...] = (((r_ref[0].astype(F32) + r_ref[1].astype(F32)) + r_ref[2].astype(F32))
                          + r_ref[3].astype(F32))

        out_spec = pl.BlockSpec((tr, c), lambda i, l=l: (l * nt + i, 0))
        first = acc is None
        acc = pl.pallas_call(
            kern, name=f"{name}_l{l}", grid=(nt,),
            in_specs=[pl.BlockSpec((4, tr, c), lambda i: (0, i, 0))]
            + ([] if first else [pl.BlockSpec(memory_space=pl.ANY)]),
            out_specs=out_spec, out_shape=jax.ShapeDtypeStruct((nlayer * r, c), F32),
            input_output_aliases={} if first else {1: 0},
            compiler_params=_params("parallel"))(*([r4] if first else [r4, acc]))
    return acc


def _adamw(w, m, v, parts, name):
    r, c = w.shape
    tr = _row_tile(r, c, 3 << 19)
    npart = len(parts)
    c1 = 1.0 - ADAM_B1 ** ADAM_STEP
    c2 = 1.0 - ADAM_B2 ** ADAM_STEP

    def kern(*refs):
        w_ref, m_ref, v_ref = refs[:3]
        p_refs = refs[3:3 + npart]
        g_ref, d_ref, mo_ref, vo_ref = refs[3 + npart:]
        g = p_refs[0][...]
        for p in p_refs[1:]:
            g = g + p[...]
        mn = ADAM_B1 * m_ref[...] + (1.0 - ADAM_B1) * g
        vn = ADAM_B2 * v_ref[...] + (1.0 - ADAM_B2) * (g * g)
        g_ref[...] = g
        mo_ref[...] = mn
        vo_ref[...] = vn
        d_ref[...] = -ADAM_LR * ((mn / c1) / (jnp.sqrt(vn / c2) + ADAM_EPS) + ADAM_WD * w_ref[...])

    blk = pl.BlockSpec((tr, c), lambda i: (i, 0))
    shape = jax.ShapeDtypeStruct((r, c), F32)
    return pl.pallas_call(
        kern, name=name, grid=(r // tr,), in_specs=[blk] * (3 + npart), out_specs=[blk] * 4, out_shape=[shape] * 4,
        compiler_params=_params("parallel"))(w, m, v, *parts)


def _sum_devices(g8, name):
    _, r, c = g8.shape

    def kern(g_ref, o_ref):
        tot = g_ref[0]
        for dev in range(1, 8):
            tot = tot + g_ref[dev]
        o_ref[...] = tot

    return pl.pallas_call(
        kern, name=name, grid=(1,), in_specs=[pl.BlockSpec((8, r, c), lambda i: (0, 0, 0))],
        out_specs=pl.BlockSpec((r, c), lambda i: (0, 0)), out_shape=jax.ShapeDtypeStruct((r, c), F32),
        compiler_params=_params("arbitrary"))(g8)


def _pad_lanes(a, width):
    return jnp.pad(a, [(0, 0)] * (a.ndim - 1) + [(0, width - a.shape[-1])])


def kernel(x, positions, ffn_norm1, ffn1_w1, ffn1_w3, ffn1_w2, mix_norm, ffn_norm2, ffn2_w1, ffn2_w3, ffn2_w2, conv_w_pw1, conv_w_dw, conv_norm, conv_w_pw2, mla_w_a, mla_q_norm, mla_kv_norm, mla_w_uq, mla_w_ukv, mla_w_o, final_norm, loss_target, m_ffn_norm1, m_ffn1_w1, m_ffn1_w3, m_ffn1_w2, m_mix_norm, m_ffn_norm2, m_ffn2_w1, m_ffn2_w3, m_ffn2_w2, m_conv_w_pw1, m_conv_w_dw, m_conv_norm, m_conv_w_pw2, m_mla_w_a, m_mla_q_norm, m_mla_kv_norm, m_mla_w_uq, m_mla_w_ukv, m_mla_w_o, m_final_norm, v_ffn_norm1, v_ffn1_w1, v_ffn1_w3, v_ffn1_w2, v_mix_norm, v_ffn_norm2, v_ffn2_w1, v_ffn2_w3, v_ffn2_w2, v_conv_w_pw1, v_conv_w_dw, v_conv_norm, v_conv_w_pw2, v_mla_w_a, v_mla_q_norm, v_mla_kv_norm, v_mla_w_uq, v_mla_w_ukv, v_mla_w_o, v_final_norm):
    given = locals()
    return _step({nm: given[nm] for nm in INPUTS})


def _step(A):
    x = A['x'][0]
    target = A['loss_target'][0]
    t, d = x.shape
    pos = A['positions'].reshape(t, 1)
    me = 2 * lax.axis_index("x") + lax.axis_index("y")

    flipped = {f'ffn{k}_{w}' for k in (1, 2) for w in ('w1', 'w3')}
    P = {}
    for nm in BIG:
        for key in (nm, 'm_' + nm, 'v_' + nm):
            P[key] = jnp.swapaxes(A[key], 1, 2) if nm in flipped else A[key]

    def unflip(nm, a):
        return jnp.swapaxes(a, 1, 2) if nm in flipped else a

    ffn = [f'ffn{k}_{w}' for k in (1, 2) for w in ('w1', 'w3', 'w2')]
    gather_groups = [[(nm, 0) for nm in ffn[:3]],
                     [('conv_w_pw1', 0), ('conv_w_pw2', 0)] + [(nm, 0) for nm in ffn[3:]],
                     [(nm, 1) for nm in ffn[:3]] + [('mla_w_a', 0), ('mla_w_uq', 0), ('mla_w_ukv', 0), ('mla_w_o', 0)],
                     [(nm, 1) for nm in ffn[3:]]]
    halved = (0, 1)
    gather_flights = {}
    big = {}

    def gather_start(gi, dep):
        shards = [_bf(P[nm][l]) for nm, l in gather_groups[gi]]
        gather_flights[gi], token = _exchange_start(shards, [_landing(s, me) for s in shards], False,
                                                    f"gather_start_{gi}", dep, half=gi in halved)
        return token

    def gather_wait(gi, after):
        landed = _exchange_wait(gather_flights[gi], after, False, f"gather_wait_{gi}", half=gi in halved)
        if gi in halved:
            landed = _fill_halves(landed, f"gather_fill_{gi}")
        big.update(zip(gather_groups[gi], landed))
        return landed[0]

    dw_shard = A['conv_w_dw'][0]
    cw = dw_shard.shape[1]
    small = jnp.concatenate([
        jnp.pad(dw_shard, ((0, CONV_HALO - CONV_WIDTH), (0, 0))),
        jnp.pad(_pad_lanes(A['mla_q_norm'], cw), ((0, 7), (0, 0))),
        jnp.pad(_pad_lanes(A['mla_kv_norm'], cw), ((0, 7), (0, 0)))], axis=0)
    small = _all_gather_rows(small, "gather_small_weights").reshape(4, 2, 48, cw)[:, 0]
    w_dw = jnp.concatenate([small[j, :CONV_HALO] for j in range(4)], axis=1)
    gq = jnp.concatenate([small[j, CONV_HALO, :Q_LORA // 4] for j in range(4)])
    gkv = jnp.concatenate([small[j, CONV_HALO + 8, :KV_LORA // 4] for j in range(4)])

    def rows(nm, layer):
        g = big[nm, layer]
        return g.reshape(-1, g.shape[-1])

    ffn_w = {}

    def ffn_weights(k, l):
        ffn_w[k, l] = (rows(f'ffn{k}_w1', l), rows(f'ffn{k}_w3', l), rows(f'ffn{k}_w2', l))
        return ffn_w[k, l]

    token = gather_start(0, small)
    cs_c, cs_s = _rope_tables(pos)
    h0 = x
    token = gather_start(1, gather_wait(0, token))
    h1, n01, z01a, z01b = _ffn_fwd(h0, A['ffn_norm1'][0], *ffn_weights(1, 0), token, "ffn1_l0_fwd")
    token = gather_start(3, gather_start(2, gather_wait(1, h1)))
    pw1 = big['conv_w_pw1', 0]
    pw1_a = jnp.concatenate([pw1[0], pw1[1]], axis=1)
    pw1_b = jnp.concatenate([pw1[2], pw1[3]], axis=1)
    pw2 = rows('conv_w_pw2', 0)
    m0 = _norm_fwd(h1, A['mix_norm'][0], token, "mix_norm_l0")
    ca, cb, glu = _glu_fwd(m0, pw1_a, pw1_b)
    cv, cs = _conv_fwd(glu, w_dw, A['conv_norm'][0])
    h2 = _mm([(cs, pw2)], F32, "conv_pw2_fwd", res=h1)
    h3, n02, z02a, z02b = _ffn_fwd(h2, A['ffn_norm2'][0], *ffn_weights(2, 0), token, "ffn2_l0_fwd")
    gather_wait(2, h3)
    w_a = _pad_lanes(rows('mla_w_a', 0), A_PAD)
    wuq = _pad_lanes(big['mla_w_uq', 0].reshape(Q_LORA, HEADS, NOPE + ROPE), HEAD_PAD).reshape(Q_LORA, -1)
    wukv = big['mla_w_ukv', 0].reshape(KV_LORA, HEADS * (NOPE + V_HEAD))
    w_o = rows('mla_w_o', 0)
    h4, n11, z11a, z11b = _ffn_fwd(h3, A['ffn_norm1'][1], *ffn_weights(1, 1), token, "ffn1_l1_fwd")
    m1 = _norm_fwd(h4, A['mix_norm'][1], token, "mix_norm_l1")
    a_lat = _mm([(m1, w_a)], F32, "mla_down_fwd")
    cq, ckv, kr = _mla_prep(a_lat, gq, gkv, cs_c, cs_s)
    q, k, v = _mla_qkv(cq, ckv, kr, cs_c, cs_s, wuq, wukv)
    o, lse = _flash_fwd(q, k, v)
    h5 = _mm([(o, w_o)], F32, "mla_out_fwd", res=h4)
    gather_wait(3, h5)
    h6, n12, z12a, z12b = _ffn_fwd(h5, A['ffn_norm2'][1], *ffn_weights(2, 1), token, "ffn2_l1_fwd")

    def row_slots(g):
        return g.reshape(4, g.shape[0] // 4, g.shape[1])

    scatter_flights = []

    def scatter_start(named):
        srcs = [g for _, g in named]
        lands = [_landing(lax.dynamic_index_in_dim(g, me, 0, keepdims=False), me) for g in srcs]
        flight, token = _exchange_start(srcs, lands, True, f"scatter_start_{len(scatter_flights)}")
        scatter_flights.append(([key for key, _ in named], flight))
        return token

    def send_ffn(k, l, dw1t, dw3t, dw2):
        return scatter_start([((f'ffn{k}_w1', l), row_slots(dw1t)), ((f'ffn{k}_w3', l), row_slots(dw3t)),
                              ((f'ffn{k}_w2', l), row_slots(dw2))])

    dh6, dg_final, loss_part = _loss_bwd(h6, target, A['final_norm'])
    dh5, dg_n2_l1, *dws = _ffn_bwd(dh6, h5, A['ffn_norm2'][1], n12, z12a, z12b, *ffn_w[2, 1], loss_part, "ffn2_l1")
    token = send_ffn(2, 1, *dws)

    do = _mm([(dh5, w_o)], BF16, "mla_out_bwd", trans_b=True, dep=token)
    dw_o = _mm_tn(o, dh5, BF16, "mla_dw_o")
    delta = _attn_delta(do, o)
    dq, dk, dv = _flash_bwd(q, k, v, do, lse, delta)
    dr, dkv, dcq, dckv, dar = _mla_qkv_bwd(dq, dk, dv, cs_c, cs_s, wuq, wukv)
    dwuq = _mm_tn(cq, dr, BF16, "mla_dw_uq", bn=dr.shape[1] // 2)
    dwukv = _mm_tn(ckv, dkv, BF16, "mla_dw_ukv", bn=dkv.shape[1] // 2)
    da_lat, dgq, dgkv = _mla_prep_bwd(a_lat, dcq, dckv, dar, gq, gkv)
    dw_a = _mm_tn(m1, da_lat, BF16, "mla_dw_a")
    token = scatter_start([
        (('mla_w_a', 0), row_slots(dw_a[:, :Q_LORA + KV_LORA + ROPE])),
        (('mla_w_uq', 0), dwuq.reshape(4, Q_LORA // 4, HEADS, HEAD_PAD)[..., :NOPE + ROPE]),
        (('mla_w_ukv', 0), dwukv.reshape(4, KV_LORA // 4, HEADS, NOPE + V_HEAD)),
        (('mla_w_o', 0), row_slots(dw_o))])
    dh4, dg_mix_l1 = _mm_normbwd([(da_lat, w_a)], h4, A['mix_norm'][1], dh5, token, "mla_down_bwd")

    dh3, dg_n1_l1, *dws = _ffn_bwd(dh4, h3, A['ffn_norm1'][1], n11, z11a, z11b, *ffn_w[1, 1], token, "ffn1_l1")
    token = send_ffn(1, 1, *dws)
    dh2, dg_n2_l0, *dws = _ffn_bwd(dh3, h2, A['ffn_norm2'][0], n02, z02a, z02b, *ffn_w[2, 0], token, "ffn2_l0")
    token = send_ffn(2, 0, *dws)

    dcv, dg_conv = _conv_bwd_norm(dh2, cv, pw2, A['conv_norm'][0], token)
    dw_pw2 = _mm_tn(cs, dh2, BF16, "conv_dw_pw2")
    dca, dcb, ddw = _conv_bwd_dw(dcv, glu, ca, cb, w_dw)
    dpw1_a = _mm_tn(m0, dca, BF16, "conv_dw_pw1a")
    dpw1_b = _mm_tn(m0, dcb, BF16, "conv_dw_pw1b")
    half = dpw1_a.shape[1] // 2
    token = scatter_start([
        (('conv_w_pw1', 0), jnp.stack([dpw1_a[:, :half], dpw1_a[:, half:], dpw1_b[:, :half], dpw1_b[:, half:]])),
        (('conv_w_pw2', 0), row_slots(dw_pw2))])
    dh1, dg_mix_l0 = _mm_normbwd([(dca, pw1_a), (dcb, pw1_b)], h1, A['mix_norm'][0], dh2, token, "conv_pw1_bwd")

    dx, dg_n1_l0, *dws = _ffn_bwd(dh1, h0, A['ffn_norm1'][0], n01, z01a, z01b, *ffn_w[1, 0], token, "ffn1_l0")
    last_sent = send_ffn(1, 0, *dws)
    out = {}

    received = {}

    def scatter_wait(si, after):
        keys, flight = scatter_flights[si]
        landed = _exchange_wait(flight, after, True, f"scatter_wait_{si}")
        received.update(zip(keys, landed))
        return landed[0]

    def slots(nm, l):
        return received[nm, l].reshape(4, -1, received[nm, l].shape[-1])

    def sibling_start(sums, tag):
        return _exchange_start(sums, [lax.empty(s.shape, s.dtype) for s in sums], "sibling", "swap_start_" + tag)

    def finish(names, sums, flight, after, tag):
        for nm, mine, theirs in zip(names, sums, _exchange_wait(flight, after, "sibling", "swap_wait_" + tag)):
            res = _adamw(_as_rows(P[nm]), _as_rows(P['m_' + nm]), _as_rows(P['v_' + nm]), [mine, theirs],
                         "adamw_" + nm)
            out[nm] = [unflip(nm, r.reshape(P[nm].shape)) for r in res]
        return res[1]

    qkv_row = jnp.concatenate([dgq, dgkv, jnp.zeros((8, d - Q_LORA - KV_LORA), F32)], axis=1)
    loss_row = _pad_lanes(loss_part, d)
    small_g = jnp.concatenate([dg_n1_l0, dg_n1_l1, dg_mix_l0, dg_mix_l1, dg_n2_l0, dg_n2_l1, dg_conv, dg_final,
                               qkv_row, loss_row, ddw], axis=0)
    nrow = small_g.shape[0]
    tot = _sum_devices(_all_gather_rows(small_g, "gather_small_grads", last_sent).reshape(8, nrow, d),
                       "sum_small_grads")
    loss = tot[72, 0]
    q_shard = lax.dynamic_slice_in_dim(tot[64, :Q_LORA], me * (Q_LORA // 4), Q_LORA // 4)
    kv_shard = lax.dynamic_slice_in_dim(tot[64, Q_LORA:Q_LORA + KV_LORA], me * (KV_LORA // 4), KV_LORA // 4)
    dw_shard_g = lax.dynamic_slice_in_dim(tot[80:80 + CONV_WIDTH], me * cw, cw, axis=1)
    small_grads = {
        'ffn_norm1': jnp.stack([tot[0], tot[8]]), 'mix_norm': jnp.stack([tot[16], tot[24]]),
        'ffn_norm2': jnp.stack([tot[32], tot[40]]), 'conv_norm': tot[48][None], 'final_norm': tot[56],
        'mla_q_norm': q_shard[None], 'mla_kv_norm': kv_shard[None], 'conv_w_dw': dw_shard_g[None],
    }
    for nm, g in small_grads.items():
        res = _adamw(_as_rows(A[nm]) if A[nm].ndim > 1 else A[nm].reshape(1, -1),
                     A['m_' + nm].reshape(-1, A[nm].shape[-1]), A['v_' + nm].reshape(-1, A[nm].shape[-1]),
                     [g.reshape(-1, A[nm].shape[-1])], "adamw_" + nm)
        out[nm] = [r.reshape(A[nm].shape) for r in res]

    after = tot
    last = len(scatter_flights) - 1
    for si in range(last):
        after = scatter_wait(si, after)
    late = ffn[:3]
    early = [nm for nm in BIG if nm not in late]
    early_sums = [_sum_slots([(l, slots(nm, l)) for l in range(A[nm].shape[0])], A[nm].shape[0], "sum_" + nm)
                  for nm in early]
    early_flight, token = sibling_start(early_sums, "early")
    late_l1 = [_sum_slots([(1, slots(nm, 1))], 2, "sum_" + nm) for nm in late]
    scatter_wait(last, token)
    late_sums = [_sum_slots([(0, slots(nm, 0))], 2, "sum_" + nm, into=part) for nm, part in zip(late, late_l1)]
    late_flight, token = sibling_start(late_sums, "late")
    after = finish(early, early_sums, early_flight, token, "early")
    finish(late, late_sums, late_flight, after, "late")

    return (loss, dx[None], *[out[nm][0] for nm in WEIGHTS], *[out[nm][1] for nm in WEIGHTS],
            *[out[nm][2] for nm in WEIGHTS], *[out[nm][3] for nm in WEIGHTS])
```

```python
import functools

import jax
import jax.numpy as jnp
import numpy as np
from jax import lax
from jax.experimental import pallas as pl
from jax.experimental.pallas import tpu as pltpu

F32 = jnp.float32
BF16 = jnp.bfloat16
MESH = pl.DeviceIdType.MESH

RMS_EPS = 1e-6
HEADS = 8
NOPE = 128
ROPE = 64
HEAD_PAD = 256
V_HEAD = 128
Q_LORA = 512
KV_LORA = 256
A_PAD = 896
CHUNK = 64
CONV_WIDTH = 31
CONV_HALO = 32
CONV_ROWS = 16
ROPE_THETA = 10000.0
ATTN_SCALE = (NOPE + ROPE) ** -0.5
FFN_RES = 0.5

ADAM_LR = 0.001
ADAM_B1 = 0.9
ADAM_B2 = 0.999
ADAM_EPS = 1e-08
ADAM_WD = 0.01
ADAM_STEP = 10

VMEM_LIMIT_BYTES = 56 * 1024 * 1024

WEIGHTS = ['ffn_norm1', 'ffn1_w1', 'ffn1_w3', 'ffn1_w2', 'mix_norm', 'ffn_norm2', 'ffn2_w1', 'ffn2_w3', 'ffn2_w2',
           'conv_w_pw1', 'conv_w_dw', 'conv_norm', 'conv_w_pw2', 'mla_w_a', 'mla_q_norm', 'mla_kv_norm', 'mla_w_uq',
           'mla_w_ukv', 'mla_w_o', 'final_norm']
INPUTS = (['x', 'positions'] + WEIGHTS + ['loss_target'] + ['m_' + w for w in WEIGHTS] + ['v_' + w for w in WEIGHTS])
BIG = ['ffn1_w1', 'ffn1_w3', 'ffn1_w2', 'ffn2_w1', 'ffn2_w3', 'ffn2_w2', 'conv_w_pw1', 'conv_w_pw2', 'mla_w_a',
       'mla_w_uq', 'mla_w_ukv', 'mla_w_o']


def _params(*sem):
    return pltpu.CompilerParams(dimension_semantics=sem, vmem_limit_bytes=VMEM_LIMIT_BYTES)


def _bf(v):
    return v.astype(BF16)


def _rstd(x):
    return lax.rsqrt(jnp.mean(x * x, axis=-1, keepdims=True) + RMS_EPS)


def _sigmoid(x):
    return jax.nn.sigmoid(x)


def _rot(x):
    lane = lax.broadcasted_iota(jnp.int32, x.shape, 1)
    return jnp.where(lane < ROPE // 2, -pltpu.roll(x, 128 - ROPE // 2, 1), pltpu.roll(x, ROPE // 2, 1))


def _rot_t(y):
    lane = lax.broadcasted_iota(jnp.int32, y.shape, 1)
    return jnp.where(lane < ROPE // 2, pltpu.roll(y, 128 - ROPE // 2, 1), -pltpu.roll(y, ROPE // 2, 1))


def _pair_sum(a_refs, b_refs, trans_b):
    tot = None
    for a_r, b_r in zip(a_refs, b_refs):
        a, b = _bf(a_r[...]), _bf(b_r[...])
        if trans_b:
            d = lax.dot_general(a, b, (((1,), (1,)), ((), ())), preferred_element_type=F32)
        else:
            d = jnp.dot(a, b, preferred_element_type=F32)
        tot = d if tot is None else tot + d
    return tot


def _mm(pairs, out_dtype, name, *, trans_b=False, tm=512, tn=None, tk=None, res=None, dep=None):
    m, k = pairs[0][0].shape
    n = pairs[0][1].shape[0] if trans_b else pairs[0][1].shape[1]
    tm, tn, tk = min(tm, m), tn or n, tk or k
    nk, npair = k // tk, len(pairs)

    def kern(*refs):
        a_refs, b_refs = refs[:npair], refs[npair:2 * npair]
        rest = list(refs[2 * npair:])
        res_ref = rest.pop(0) if res is not None else None
        if dep is not None:
            rest.pop(0)
        o_ref = rest.pop(0)

        def finish(acc):
            if res_ref is not None:
                acc = res_ref[...] + acc
            o_ref[...] = acc.astype(o_ref.dtype)

        if nk == 1:
            finish(_pair_sum(a_refs, b_refs, trans_b))
        else:
            acc_ref = rest.pop(0)
            kk = pl.program_id(2)

            @pl.when(kk == 0)
            def _():
                acc_ref[...] = jnp.zeros_like(acc_ref)

            acc_ref[...] += _pair_sum(a_refs, b_refs, trans_b)

            @pl.when(kk == nk - 1)
            def _():
                finish(acc_ref[...])

    a_spec = pl.BlockSpec((tm, tk), lambda i, j, kk: (i, kk))
    b_spec = (pl.BlockSpec((tn, tk), lambda i, j, kk: (j, kk)) if trans_b
              else pl.BlockSpec((tk, tn), lambda i, j, kk: (kk, j)))
    io_spec = pl.BlockSpec((tm, tn), lambda i, j, kk: (i, j))
    in_specs = ([a_spec] * npair + [b_spec] * npair + ([io_spec] if res is not None else [])
                + ([pl.BlockSpec((8, 128), lambda i, j, kk: (0, 0))] if dep is not None else []))
    args = ([p[0] for p in pairs] + [p[1] for p in pairs] + ([res] if res is not None else [])
            + ([dep] if dep is not None else []))
    return pl.pallas_call(
        kern, name=name, grid=(m // tm, n // tn, nk), in_specs=in_specs, out_specs=io_spec,
        out_shape=jax.ShapeDtypeStruct((m, n), out_dtype),
        scratch_shapes=[pltpu.VMEM((tm, tn), F32)] if nk > 1 else [],
        compiler_params=_params("parallel", "parallel", "arbitrary"))(*args)


def _mm_normbwd(pairs, h, g, dres, dep, name, *, tm=512, tk=None):
    m, k = pairs[0][0].shape
    d = pairs[0][1].shape[0]
    tm, tk = min(tm, m), tk or k
    nk, npair = k // tk, len(pairs)

    def kern(*refs):
        a_refs, b_refs = refs[:npair], refs[npair:2 * npair]
        h_ref, g_ref, dres_ref, _, o_ref, dg_ref, acc_ref = refs[2 * npair:]
        i, kk = pl.program_id(0), pl.program_id(1)

        @pl.when(jnp.logical_and(i == 0, kk == 0))
        def _():
            dg_ref[...] = jnp.zeros_like(dg_ref)

        @pl.when(kk == 0)
        def _():
            acc_ref[...] = jnp.zeros_like(acc_ref)

        acc_ref[...] += _pair_sum(a_refs, b_refs, True)

        @pl.when(kk == nk - 1)
        def _():
            dn = acc_ref[...]
            x = h_ref[...]
            rstd = _rstd(x)
            xhat = x * rstd
            dg_ref[...] += jnp.broadcast_to(jnp.sum(dn * xhat, axis=0, keepdims=True), dg_ref.shape)
            dxh = dn * g_ref[...]
            dx = rstd * (dxh - xhat * jnp.mean(dxh * xhat, axis=-1, keepdims=True))
            o_ref[...] = dres_ref[...] + dx

    row = pl.BlockSpec((tm, d), lambda i, kk: (i, 0))
    in_specs = ([pl.BlockSpec((tm, tk), lambda i, kk: (i, kk))] * npair
                + [pl.BlockSpec((d, tk), lambda i, kk: (0, kk))] * npair
                + [row, pl.BlockSpec((1, d), lambda i, kk: (0, 0)), row, pl.BlockSpec((8, 128), lambda i, kk: (0, 0))])
    return pl.pallas_call(
        kern, name=name, grid=(m // tm, nk), in_specs=in_specs,
        out_specs=[row, pl.BlockSpec((8, d), lambda i, kk: (0, 0))],
        out_shape=[jax.ShapeDtypeStruct((m, d), F32), jax.ShapeDtypeStruct((8, d), F32)],
        scratch_shapes=[pltpu.VMEM((tm, d), F32)],
        compiler_params=_params("arbitrary", "arbitrary"))(
            *[p[0] for p in pairs], *[p[1] for p in pairs], h, g.reshape(1, d), dres, dep)


def _mm_tn(a, b, out_dtype, name, *, bm=None, bn=None, tk=1024):
    t, m = a.shape
    batched = b.ndim == 3
    n = b.shape[-1]
    nb = b.shape[0] if batched else 1
    bm, bn, tk = bm or m, bn or n, min(tk, t)
    nk = t // tk

    def kern(a_ref, b_ref, o_ref, acc_ref):
        kk = pl.program_id(3)

        @pl.when(kk == 0)
        def _():
            acc_ref[...] = jnp.zeros_like(acc_ref)

        acc_ref[...] += lax.dot_general(_bf(a_ref[...]), _bf(b_ref[...]), (((0,), (0,)), ((), ())),
                                        preferred_element_type=F32)

        @pl.when(kk == nk - 1)
        def _():
            o_ref[...] = acc_ref[...].astype(o_ref.dtype)

    a_spec = pl.BlockSpec((tk, bm), lambda h, i, j, kk: (kk, i))
    if batched:
        b_spec = pl.BlockSpec((None, tk, bn), lambda h, i, j, kk: (h, kk, j))
        o_spec = pl.BlockSpec((None, bm, bn), lambda h, i, j, kk: (h, i, j))
        out_shape = jax.ShapeDtypeStruct((nb, m, n), out_dtype)
    else:
        b_spec = pl.BlockSpec((tk, bn), lambda h, i, j, kk: (kk, j))
        o_spec = pl.BlockSpec((bm, bn), lambda h, i, j, kk: (i, j))
        out_shape = jax.ShapeDtypeStruct((m, n), out_dtype)
    return pl.pallas_call(
        kern, name=name, grid=(nb, m // bm, n // bn, nk), in_specs=[a_spec, b_spec], out_specs=o_spec,
        out_shape=out_shape, scratch_shapes=[pltpu.VMEM((bm, bn), F32)],
        compiler_params=_params("parallel", "parallel", "parallel", "arbitrary"))(a, b)


def _ffn_tile(f):
    return f // 2 if (f // 2) % 128 == 0 else f


def _ffn_fwd(h, g, w1t, w3t, w2, dep, name):
    t, d = h.shape
    f = w1t.shape[0]
    tm = min(256, t)
    nt = (((1,), (1,)), ((), ()))

    def kern(h_ref, g_ref, w1_hbm, w3_hbm, w2_hbm, dep_ref, ho_ref, n_ref, z1_ref, z3_ref,
             w1_ref, w3_ref, w2_ref, sems):
        @pl.when(pl.program_id(0) == 0)
        def _():
            copies = [pltpu.make_async_copy(src, dst, sems.at[k]) for k, (src, dst) in
                      enumerate(((w1_hbm, w1_ref), (w3_hbm, w3_ref), (w2_hbm, w2_ref)))]
            for cp in copies:
                cp.start()
            for cp in copies:
                cp.wait()

        x = h_ref[...]
        n = _bf(x * _rstd(x) * g_ref[...])
        n_ref[...] = n
        z1 = lax.dot_general(n, w1_ref[...], nt, preferred_element_type=F32)
        z3 = lax.dot_general(n, w3_ref[...], nt, preferred_element_type=F32)
        z1_ref[...] = _bf(z1)
        z3_ref[...] = _bf(z3)
        act = _bf(z1 * _sigmoid(z1) * z3)
        ho_ref[...] = x + FFN_RES * jnp.dot(act, w2_ref[...], preferred_element_type=F32)

    row = pl.BlockSpec((tm, d), lambda i: (i, 0))
    col = pl.BlockSpec((tm, f), lambda i: (i, 0))
    whole = pl.BlockSpec(memory_space=pl.ANY)
    return pl.pallas_call(
        kern, name=name, grid=(t // tm,),
        in_specs=[row, pl.BlockSpec((1, d), lambda i: (0, 0)), whole, whole, whole,
                  pl.BlockSpec((8, 128), lambda i: (0, 0))],
        out_specs=[row, row, col, col],
        out_shape=[jax.ShapeDtypeStruct((t, d), F32), jax.ShapeDtypeStruct((t, d), BF16),
                   jax.ShapeDtypeStruct((t, f), BF16), jax.ShapeDtypeStruct((t, f), BF16)],
        scratch_shapes=[pltpu.VMEM((f, d), BF16), pltpu.VMEM((f, d), BF16), pltpu.VMEM((f, d), BF16),
                        pltpu.SemaphoreType.DMA((3,))],
        compiler_params=_params("arbitrary"))(h, g.reshape(1, d), w1t, w3t, w2, dep)


def _ffn_bwd_x(dh, h_in, g, z1, z3, w1t, w3t, w2, dep, name):
    t, d = dh.shape
    f = z1.shape[1]
    tm = min(256, t)

    def kern(dh_ref, h_ref, g_ref, z1_ref, z3_ref, w2_hbm, w1_hbm, w3_hbm, dep_ref,
             o_ref, dg_ref, dz1_ref, dz3_ref, a_ref, df_ref, w2_ref, w1_ref, w3_ref, sems):
        @pl.when(pl.program_id(0) == 0)
        def _():
            copies = [pltpu.make_async_copy(src, dst, sems.at[k]) for k, (src, dst) in
                      enumerate(((w2_hbm, w2_ref), (w1_hbm, w1_ref), (w3_hbm, w3_ref)))]
            for cp in copies:
                cp.start()
            dg_ref[...] = jnp.zeros_like(dg_ref)
            for cp in copies:
                cp.wait()

        df = _bf(FFN_RES * dh_ref[...])
        df_ref[...] = df
        da = lax.dot_general(df, w2_ref[...], (((1,), (1,)), ((), ())), preferred_element_type=F32)
        z1v, z3v = z1_ref[...].astype(F32), z3_ref[...].astype(F32)
        sig = _sigmoid(z1v)
        silu = z1v * sig
        a_ref[...] = _bf(silu * z3v)
        dz1 = _bf(da * z3v * (sig * (1.0 + z1v * (1.0 - sig))))
        dz3 = _bf(da * silu)
        dz1_ref[...] = dz1
        dz3_ref[...] = dz3
        dn = (jnp.dot(dz1, w1_ref[...], preferred_element_type=F32)
              + jnp.dot(dz3, w3_ref[...], preferred_element_type=F32))
        x = h_ref[...]
        rstd = _rstd(x)
        xhat = x * rstd
        dg_ref[...] += jnp.broadcast_to(jnp.sum(dn * xhat, axis=0, keepdims=True), dg_ref.shape)
        dxh = dn * g_ref[...]
        o_ref[...] = dh_ref[...] + rstd * (dxh - xhat * jnp.mean(dxh * xhat, axis=-1, keepdims=True))

    row = pl.BlockSpec((tm, d), lambda i: (i, 0))
    col = pl.BlockSpec((tm, f), lambda i: (i, 0))
    whole = pl.BlockSpec(memory_space=pl.ANY)
    colshape = jax.ShapeDtypeStruct((t, f), BF16)
    return pl.pallas_call(
        kern, name=name, grid=(t // tm,),
        in_specs=[row, row, pl.BlockSpec((1, d), lambda i: (0, 0)), col, col, whole, whole, whole,
                  pl.BlockSpec((8, 128), lambda i: (0, 0))],
        out_specs=[row, pl.BlockSpec((8, d), lambda i: (0, 0)), col, col, col, row],
        out_shape=[jax.ShapeDtypeStruct((t, d), F32), jax.ShapeDtypeStruct((8, d), F32), colshape, colshape, colshape,
                   jax.ShapeDtypeStruct((t, d), BF16)],
        scratch_shapes=[pltpu.VMEM((f, d), BF16), pltpu.VMEM((f, d), BF16), pltpu.VMEM((f, d), BF16),
                        pltpu.SemaphoreType.DMA((3,))],
        compiler_params=_params("arbitrary"))(dh, h_in, g.reshape(1, d), z1, z3, w2, w1t, w3t, dep)


def _ffn_bwd(dh, h_in, g, n, z1, z3, w1t, w3t, w2, dep, tag):
    f = w2.shape[0]
    dh_in, dg, dz1, dz3, act, df = _ffn_bwd_x(dh, h_in, g, z1, z3, w1t, w3t, w2, dep, tag + "_bwd_x")
    dw1t = _mm_tn(dz1, n, BF16, tag + "_dw1", bm=_ffn_tile(f), tk=2048)
    dw3t = _mm_tn(dz3, n, BF16, tag + "_dw3", bm=_ffn_tile(f), tk=2048)
    dw2 = _mm_tn(act, df, BF16, tag + "_dw2", bm=_ffn_tile(f), tk=2048)
    return dh_in, dg, dw1t, dw3t, dw2


def _norm_fwd(h, g, dep, name):
    t, d = h.shape
    tm = min(512, t)

    def kern(h_ref, g_ref, dep_ref, o_ref):
        x = h_ref[...]
        o_ref[...] = _bf(x * _rstd(x) * g_ref[...])

    row = pl.BlockSpec((tm, d), lambda i: (i, 0))
    return pl.pallas_call(
        kern, name=name, grid=(t // tm,),
        in_specs=[row, pl.BlockSpec((1, d), lambda i: (0, 0)), pl.BlockSpec((8, 128), lambda i: (0, 0))],
        out_specs=row, out_shape=jax.ShapeDtypeStruct((t, d), BF16),
        compiler_params=_params("parallel"))(h, g.reshape(1, d), dep)


def _loss_bwd(h, target, g):
    t, d = h.shape
    tm = min(512, t)

    def kern(h_ref, t_ref, g_ref, dh_ref, dg_ref, loss_ref):
        @pl.when(pl.program_id(0) == 0)
        def _():
            dg_ref[...] = jnp.zeros_like(dg_ref)
            loss_ref[...] = jnp.zeros_like(loss_ref)

        x = h_ref[...]
        rstd = _rstd(x)
        xhat = x * rstd
        err = xhat * g_ref[...] - t_ref[...]
        row_loss = jnp.sum(err * err, axis=-1, keepdims=True) * (0.5 / d)
        loss_ref[...] += jnp.broadcast_to(jnp.sum(row_loss, axis=0, keepdims=True), loss_ref.shape)
        dy = err * (1.0 / d)
        dg_ref[...] += jnp.broadcast_to(jnp.sum(dy * xhat, axis=0, keepdims=True), dg_ref.shape)
        dxh = dy * g_ref[...]
        dh_ref[...] = rstd * (dxh - xhat * jnp.mean(dxh * xhat, axis=-1, keepdims=True))

    row = pl.BlockSpec((tm, d), lambda i: (i, 0))
    return pl.pallas_call(
        kern, name="loss_bwd", grid=(t // tm,),
        in_specs=[row, row, pl.BlockSpec((1, d), lambda i: (0, 0))],
        out_specs=[row, pl.BlockSpec((8, d), lambda i: (0, 0)), pl.BlockSpec((8, 128), lambda i: (0, 0))],
        out_shape=[jax.ShapeDtypeStruct((t, d), F32), jax.ShapeDtypeStruct((8, d), F32),
                   jax.ShapeDtypeStruct((8, 128), F32)],
        compiler_params=_params("arbitrary"))(h, target, g.reshape(1, d))


def _glu_fwd(m, wa, wb):
    t, d = m.shape
    c = wa.shape[1]
    tm, tc = min(512, t), min(512, c)

    def kern(m_ref, wa_ref, wb_ref, a_ref, b_ref, glu_ref):
        mv = m_ref[...]
        a = jnp.dot(mv, wa_ref[...], preferred_element_type=F32)
        b = jnp.dot(mv, wb_ref[...], preferred_element_type=F32)
        a_ref[...] = _bf(a)
        b_ref[...] = _bf(b)
        glu_ref[...] = _bf(a * _sigmoid(b))

    col = pl.BlockSpec((tm, tc), lambda i, j: (i, j))
    wspec = pl.BlockSpec((d, tc), lambda i, j: (0, j))
    shape = jax.ShapeDtypeStruct((t, c), BF16)
    return pl.pallas_call(
        kern, name="conv_glu_fwd", grid=(t // tm, c // tc),
        in_specs=[pl.BlockSpec((tm, d), lambda i, j: (i, 0)), wspec, wspec], out_specs=[col, col, col],
        out_shape=[shape, shape, shape], compiler_params=_params("parallel", "parallel"))(m, wa, wb)


def _conv_tile(t):
    return min(256, t)


def _shift_copies(ext, shifted, rows):
    for s in range(8):
        shifted[s] = ext[pl.ds(s, rows), :]


def _shifted_rows(shifted, start, nrows):
    return shifted[start % 8, pl.ds(start - start % 8, nrows), :]


def _conv_fwd(glu, w_dw, g):
    t, c = glu.shape
    tm = _conv_tile(t)
    hb = tm // CONV_HALO

    def kern(cur_ref, halo_ref, w_ref, g_ref, cv_ref, s_ref, ext, shifted):
        i = pl.program_id(0)
        ext[0:CONV_HALO, :] = jnp.where(i > 0, halo_ref[...].astype(F32), 0.0)
        ext[CONV_HALO:tm + CONV_HALO, :] = cur_ref[...].astype(F32)
        ext[tm + CONV_HALO:, :] = jnp.zeros((8, c), F32)
        _shift_copies(ext, shifted, tm + CONV_HALO)
        gv = g_ref[...]
        for r0 in range(0, tm, CONV_ROWS):
            acc = jnp.zeros((CONV_ROWS, c), F32)
            for k in range(CONV_WIDTH):
                acc = acc + _shifted_rows(shifted, r0 + 2 + k, CONV_ROWS) * w_ref[k:k + 1, :]
            cv_ref[r0:r0 + CONV_ROWS, :] = acc
            rn = acc * _rstd(acc) * gv
            s_ref[r0:r0 + CONV_ROWS, :] = _bf(rn * _sigmoid(rn))

    row = pl.BlockSpec((tm, c), lambda i: (i, 0))
    return pl.pallas_call(
        kern, name="conv_fwd", grid=(t // tm,),
        in_specs=[row, pl.BlockSpec((CONV_HALO, c), lambda i: (jnp.maximum(i * hb - 1, 0), 0)),
                  pl.BlockSpec((CONV_HALO, c), lambda i: (0, 0)), pl.BlockSpec((1, c), lambda i: (0, 0))],
        out_specs=[row, row],
        out_shape=[jax.ShapeDtypeStruct((t, c), F32), jax.ShapeDtypeStruct((t, c), BF16)],
        scratch_shapes=[pltpu.VMEM((tm + CONV_HALO + 8, c), F32), pltpu.VMEM((8, tm + CONV_HALO, c), F32)],
        compiler_params=_params("parallel"))(glu, glu, w_dw, g.reshape(1, c))


def _conv_bwd_norm(dh, cv, w_pw2, g, dep):
    t, c = cv.shape
    tm = min(512, t)

    def kern(dh_ref, cv_ref, w_ref, g_ref, dep_ref, dcv_ref, dg_ref):
        @pl.when(pl.program_id(0) == 0)
        def _():
            dg_ref[...] = jnp.zeros_like(dg_ref)

        ds = lax.dot_general(_bf(dh_ref[...]), w_ref[...], (((1,), (1,)), ((), ())), preferred_element_type=F32)
        x = cv_ref[...]
        rstd = _rstd(x)
        xhat = x * rstd
        rn = xhat * g_ref[...]
        sig = _sigmoid(rn)
        drn = ds * (sig * (1.0 + rn * (1.0 - sig)))
        dg_ref[...] += jnp.broadcast_to(jnp.sum(drn * xhat, axis=0, keepdims=True), dg_ref.shape)
        dxh = drn * g_ref[...]
        dcv_ref[...] = rstd * (dxh - xhat * jnp.mean(dxh * xhat, axis=-1, keepdims=True))

    row = pl.BlockSpec((tm, c), lambda i: (i, 0))
    return pl.pallas_call(
        kern, name="conv_bwd_norm", grid=(t // tm,),
        in_specs=[pl.BlockSpec((tm, dh.shape[1]), lambda i: (i, 0)), row,
                  pl.BlockSpec(w_pw2.shape, lambda i: (0, 0)), pl.BlockSpec((1, c), lambda i: (0, 0)),
                  pl.BlockSpec((8, 128), lambda i: (0, 0))],
        out_specs=[row, pl.BlockSpec((8, c), lambda i: (0, 0))],
        out_shape=[jax.ShapeDtypeStruct((t, c), F32), jax.ShapeDtypeStruct((8, c), F32)],
        compiler_params=_params("arbitrary"))(dh, cv, w_pw2, g.reshape(1, c), dep)


def _conv_bwd_dw(dcv, glu, a, b, w_dw):
    t, c = dcv.shape
    tm = _conv_tile(t)
    hb = tm // CONV_HALO
    last = t // CONV_HALO - 1

    def kern(dcv_ref, dnext_ref, glu_ref, gprev_ref, a_ref, b_ref, w_ref, da_ref, db_ref, dw_ref,
             dext, gext, dshift, gshift):
        i = pl.program_id(0)

        @pl.when(i == 0)
        def _():
            dw_ref[...] = jnp.zeros_like(dw_ref)

        dext[0:tm, :] = dcv_ref[...]
        dext[tm:tm + CONV_HALO, :] = jnp.where(i < t // tm - 1, dnext_ref[...], 0.0)
        dext[tm + CONV_HALO:, :] = jnp.zeros((8, c), F32)
        gext[0:CONV_HALO, :] = jnp.where(i > 0, gprev_ref[...].astype(F32), 0.0)
        gext[CONV_HALO:tm + CONV_HALO, :] = glu_ref[...].astype(F32)
        gext[tm + CONV_HALO:, :] = jnp.zeros((8, c), F32)
        _shift_copies(dext, dshift, tm + CONV_HALO)
        _shift_copies(gext, gshift, tm + CONV_HALO)
        for r0 in range(0, tm, CONV_ROWS):
            acc = jnp.zeros((CONV_ROWS, c), F32)
            for k in range(CONV_WIDTH):
                acc = acc + _shifted_rows(dshift, r0 + CONV_WIDTH - 1 - k, CONV_ROWS) * w_ref[k:k + 1, :]
            av = a_ref[r0:r0 + CONV_ROWS, :].astype(F32)
            sig = _sigmoid(b_ref[r0:r0 + CONV_ROWS, :].astype(F32))
            da_ref[r0:r0 + CONV_ROWS, :] = _bf(acc * sig)
            db_ref[r0:r0 + CONV_ROWS, :] = _bf(acc * av * sig * (1.0 - sig))
        for k in range(CONV_WIDTH):
            acc = jnp.zeros((CONV_ROWS, c), F32)
            for r0 in range(0, tm, CONV_ROWS):
                acc = acc + _shifted_rows(gshift, r0 + 2 + k, CONV_ROWS) * dext[r0:r0 + CONV_ROWS, :]
            dw_ref[k:k + 1, :] += jnp.sum(acc, axis=0, keepdims=True)

    row = pl.BlockSpec((tm, c), lambda i: (i, 0))
    shape = jax.ShapeDtypeStruct((t, c), BF16)
    return pl.pallas_call(
        kern, name="conv_bwd_dw", grid=(t // tm,),
        in_specs=[row, pl.BlockSpec((CONV_HALO, c), lambda i: (jnp.minimum((i + 1) * hb, last), 0)),
                  row, pl.BlockSpec((CONV_HALO, c), lambda i: (jnp.maximum(i * hb - 1, 0), 0)),
                  row, row, pl.BlockSpec((CONV_HALO, c), lambda i: (0, 0))],
        out_specs=[row, row, pl.BlockSpec((CONV_HALO, c), lambda i: (0, 0))],
        out_shape=[shape, shape, jax.ShapeDtypeStruct((CONV_HALO, c), F32)],
        scratch_shapes=[pltpu.VMEM((tm + CONV_HALO + 8, c), F32), pltpu.VMEM((tm + CONV_HALO + 8, c), F32),
                        pltpu.VMEM((8, tm + CONV_HALO, c), F32), pltpu.VMEM((8, tm + CONV_HALO, c), F32)],
        compiler_params=_params("arbitrary"))(dcv, dcv, glu, glu, a, b, w_dw)


def _rope_tables(pos):
    t = pos.shape[0]
    tm = min(512, t)
    freq = (np.float32(ROPE_THETA) ** (np.float32(-2.0) * np.arange(ROPE // 2, dtype=np.float32)
                                       / np.float32(ROPE))).astype(np.float32)
    row = np.zeros((2, 128), np.float32)
    row[0, :ROPE] = np.concatenate([freq, freq])
    row[1, :ROPE] = 1.0

    def kern(pos_ref, f_ref, c_ref, s_ref):
        ang = pos_ref[...].astype(F32) * f_ref[0:1, :]
        mask = f_ref[1:2, :]
        c_ref[...] = jnp.cos(ang) * mask
        s_ref[...] = jnp.sin(ang) * mask

    out = pl.BlockSpec((tm, 128), lambda i: (i, 0))
    shape = jax.ShapeDtypeStruct((t, 128), F32)
    return pl.pallas_call(
        kern, name="rope_tables", grid=(t // tm,),
        in_specs=[pl.BlockSpec((tm, 1), lambda i: (i, 0)), pl.BlockSpec((2, 128), lambda i: (0, 0))],
        out_specs=[out, out], out_shape=[shape, shape], compiler_params=_params("parallel"))(pos, jnp.asarray(row))


def _mla_prep(a, gq, gkv, cs_c, cs_s):
    t = a.shape[0]
    tm = min(512, t)
    kv0, r0 = Q_LORA, Q_LORA + KV_LORA

    def kern(a_ref, gq_ref, gkv_ref, c_ref, s_ref, cq_ref, ckv_ref, kr_ref):
        aq = a_ref[:, 0:kv0]
        akv = a_ref[:, kv0:r0]
        ar = a_ref[:, r0:A_PAD]
        cq_ref[...] = _bf(aq * _rstd(aq) * gq_ref[...])
        ckv_ref[...] = _bf(akv * _rstd(akv) * gkv_ref[...])
        kr_ref[...] = _bf(ar * c_ref[...] + _rot(ar) * s_ref[...])

    def row(w):
        return pl.BlockSpec((tm, w), lambda i: (i, 0))

    def vec(w):
        return pl.BlockSpec((1, w), lambda i: (0, 0))

    return pl.pallas_call(
        kern, name="mla_prep", grid=(t // tm,),
        in_specs=[row(A_PAD), vec(Q_LORA), vec(KV_LORA), row(128), row(128)],
        out_specs=[row(Q_LORA), row(KV_LORA), row(128)],
        out_shape=[jax.ShapeDtypeStruct((t, Q_LORA), BF16), jax.ShapeDtypeStruct((t, KV_LORA), BF16),
                   jax.ShapeDtypeStruct((t, 128), BF16)],
        compiler_params=_params("parallel"))(a, gq.reshape(1, -1), gkv.reshape(1, -1), cs_c, cs_s)


def _mla_prep_bwd(a, dcq, dckv, dar, gq, gkv):
    t = a.shape[0]
    tm = min(512, t)
    kv0, r0 = Q_LORA, Q_LORA + KV_LORA

    def kern(a_ref, dcq_ref, dckv_ref, dar_ref, gq_ref, gkv_ref, da_ref, dgq_ref, dgkv_ref):
        @pl.when(pl.program_id(0) == 0)
        def _():
            dgq_ref[...] = jnp.zeros_like(dgq_ref)
            dgkv_ref[...] = jnp.zeros_like(dgkv_ref)

        def back(x, dy, g_ref, dg_ref):
            rstd = _rstd(x)
            xhat = x * rstd
            dg_ref[...] += jnp.broadcast_to(jnp.sum(dy * xhat, axis=0, keepdims=True), dg_ref.shape)
            dxh = dy * g_ref[...]
            return rstd * (dxh - xhat * jnp.mean(dxh * xhat, axis=-1, keepdims=True))

        da_ref[:, 0:kv0] = _bf(back(a_ref[:, 0:kv0], dcq_ref[...], gq_ref, dgq_ref))
        da_ref[:, kv0:r0] = _bf(back(a_ref[:, kv0:r0], dckv_ref[...], gkv_ref, dgkv_ref))
        da_ref[:, r0:A_PAD] = _bf(dar_ref[...])

    def row(w):
        return pl.BlockSpec((tm, w), lambda i: (i, 0))

    def vec(r, w):
        return pl.BlockSpec((r, w), lambda i: (0, 0))

    return pl.pallas_call(
        kern, name="mla_prep_bwd", grid=(t // tm,),
        in_specs=[row(A_PAD), row(Q_LORA), row(KV_LORA), row(128), vec(1, Q_LORA), vec(1, KV_LORA)],
        out_specs=[row(A_PAD), vec(8, Q_LORA), vec(8, KV_LORA)],
        out_shape=[jax.ShapeDtypeStruct((t, A_PAD), BF16), jax.ShapeDtypeStruct((8, Q_LORA), F32),
                   jax.ShapeDtypeStruct((8, KV_LORA), F32)],
        compiler_params=_params("arbitrary"))(a, dcq, dckv, dar, gq.reshape(1, -1), gkv.reshape(1, -1))


def _mla_qkv(cq, ckv, kr, cs_c, cs_s, wuq, wukv):
    t = cq.shape[0]
    tm = min(512, t)
    kvw = NOPE + V_HEAD

    def kern(cq_ref, ckv_ref, kr_ref, c_ref, s_ref, wq_ref, wkv_ref, q_ref, k_ref, v_ref):
        r = jnp.dot(cq_ref[...], wq_ref[...], preferred_element_type=F32)
        kv = jnp.dot(ckv_ref[...], wkv_ref[...], preferred_element_type=F32)
        cv, sv, krv = c_ref[...], s_ref[...], kr_ref[...]
        for h in range(HEADS):
            xr = r[:, h * HEAD_PAD + NOPE:(h + 1) * HEAD_PAD]
            q_ref[h, :, 0:NOPE] = _bf(r[:, h * HEAD_PAD:h * HEAD_PAD + NOPE] * ATTN_SCALE)
            q_ref[h, :, NOPE:] = _bf((xr * cv + _rot(xr) * sv) * ATTN_SCALE)
            k_ref[h, :, 0:NOPE] = _bf(kv[:, h * kvw:h * kvw + NOPE])
            k_ref[h, :, NOPE:] = krv
            v_ref[h] = _bf(kv[:, h * kvw + NOPE:(h + 1) * kvw])

    def row(w):
        return pl.BlockSpec((tm, w), lambda i: (i, 0))

    def heads(w):
        return pl.BlockSpec((HEADS, tm, w), lambda i: (0, i, 0))

    return pl.pallas_call(
        kern, name="mla_qkv", grid=(t // tm,),
        in_specs=[row(Q_LORA), row(KV_LORA), row(128), row(128), row(128),
                  pl.BlockSpec(wuq.shape, lambda i: (0, 0)), pl.BlockSpec(wukv.shape, lambda i: (0, 0))],
        out_specs=[heads(HEAD_PAD), heads(HEAD_PAD), heads(V_HEAD)],
        out_shape=[jax.ShapeDtypeStruct((HEADS, t, HEAD_PAD), BF16), jax.ShapeDtypeStruct((HEADS, t, HEAD_PAD), BF16),
                   jax.ShapeDtypeStruct((HEADS, t, V_HEAD), BF16)],
        compiler_params=_params("parallel"))(cq, ckv, kr, cs_c, cs_s, wuq, wukv)


def _mla_qkv_bwd(dq, dk, dv, cs_c, cs_s, wuq, wukv):
    t = dq.shape[1]
    tm = min(256, t)
    kvw = NOPE + V_HEAD

    def kern(dq_ref, dk_ref, dv_ref, c_ref, s_ref, wq_ref, wkv_ref, dr_ref, dkv_ref, dcq_ref, dckv_ref, dar_ref):
        cv, sv = c_ref[...], s_ref[...]
        dar = jnp.zeros_like(cv)
        for h in range(HEADS):
            dqx = dq_ref[h, :, NOPE:]
            dr_ref[:, h * HEAD_PAD:h * HEAD_PAD + NOPE] = _bf(dq_ref[h, :, 0:NOPE] * ATTN_SCALE)
            dr_ref[:, h * HEAD_PAD + NOPE:(h + 1) * HEAD_PAD] = _bf((dqx * cv + _rot_t(dqx * sv)) * ATTN_SCALE)
            dkx = dk_ref[h, :, NOPE:]
            dar = dar + (dkx * cv + _rot_t(dkx * sv))
            dkv_ref[:, h * kvw:h * kvw + NOPE] = _bf(dk_ref[h, :, 0:NOPE])
            dkv_ref[:, h * kvw + NOPE:(h + 1) * kvw] = _bf(dv_ref[h])
        dar_ref[...] = dar
        dcq_ref[...] = lax.dot_general(dr_ref[...], wq_ref[...], (((1,), (1,)), ((), ())),
                                       preferred_element_type=F32)
        dckv_ref[...] = lax.dot_general(dkv_ref[...], wkv_ref[...], (((1,), (1,)), ((), ())),
                                        preferred_element_type=F32)

    def row(w):
        return pl.BlockSpec((tm, w), lambda i: (i, 0))

    def heads(w):
        return pl.BlockSpec((HEADS, tm, w), lambda i: (0, i, 0))

    return pl.pallas_call(
        kern, name="mla_qkv_bwd", grid=(t // tm,),
        in_specs=[heads(HEAD_PAD), heads(HEAD_PAD), heads(V_HEAD), row(128), row(128),
                  pl.BlockSpec(wuq.shape, lambda i: (0, 0)), pl.BlockSpec(wukv.shape, lambda i: (0, 0))],
        out_specs=[row(HEADS * HEAD_PAD), row(HEADS * kvw), row(Q_LORA), row(KV_LORA), row(128)],
        out_shape=[jax.ShapeDtypeStruct((t, HEADS * HEAD_PAD), BF16), jax.ShapeDtypeStruct((t, HEADS * kvw), BF16),
                   jax.ShapeDtypeStruct((t, Q_LORA), F32), jax.ShapeDtypeStruct((t, KV_LORA), F32),
                   jax.ShapeDtypeStruct((t, 128), F32)],
        compiler_params=_params("parallel"))(dq, dk, dv, cs_c, cs_s, wuq, wukv)


def _attn_block(t):
    return 512 if t >= 4096 else 128


def _chunk_mask(bk, bq):
    kc = lax.broadcasted_iota(jnp.int32, (bk, bq), 0) // CHUNK
    qc = lax.broadcasted_iota(jnp.int32, (bk, bq), 1) // CHUNK
    return qc >= kc


def _flash_fwd(q, k, v):
    t = q.shape[1]
    bq = _attn_block(t)
    nq = t // bq
    nch = 2

    def kern(q_ref, k_ref, v_ref, o_ref, lse_ref, s_buf, p_buf, m_ref, l_ref, acc_ref):
        i = pl.program_id(1)
        queries = [q_ref[c * bq:(c + 1) * bq, :] for c in range(nch)]

        def block(j):
            rows = pl.ds(pl.multiple_of(j * bq, bq), bq)
            return k_ref[rows, :], v_ref[rows, :]

        def scores(kj, chain):
            return lax.dot_general(kj, queries[chain], (((1,), (1,)), ((), ())), preferred_element_type=F32)

        def softmax_block(chain, slot, vj):
            for c0 in range(0, bq, 128):
                cols = slice(c0, c0 + 128)
                s = s_buf[slot, chain, :, cols]
                m_old = m_ref[chain, 0:1, cols]
                m_new = jnp.maximum(m_old, jnp.max(s, axis=0, keepdims=True))
                alpha = jnp.exp(m_old - m_new)
                p = jnp.exp(s - m_new)
                l_ref[chain, 0:1, cols] = alpha * l_ref[chain, 0:1, cols] + jnp.sum(p, axis=0, keepdims=True)
                m_ref[chain, 0:1, cols] = m_new
                p_buf[chain, :, cols] = _bf(p)
                acc_ref[chain, :, cols] = acc_ref[chain, :, cols] * alpha
            acc_ref[chain] += lax.dot_general(vj, p_buf[chain], (((0,), (0,)), ((), ())),
                                              preferred_element_type=F32)

        m_ref[...] = jnp.full(m_ref.shape, -1e30, F32)
        l_ref[...] = jnp.zeros_like(l_ref)
        acc_ref[...] = jnp.zeros_like(acc_ref)
        mask = _chunk_mask(bq, bq)
        for b in range(nch):
            kb, vb = block(nch * i + b)
            for c in range(b, nch):
                s_buf[b % 2, c] = jnp.where(mask, scores(kb, c), -1e30) if c == b else scores(kb, c)
                softmax_block(c, b % 2, vb)
        kf = block(0)[0]
        for c in range(nch):
            s_buf[0, c] = scores(kf, c)

        def body(pair, carry):
            for cur in range(2):
                j = 2 * pair + cur
                kn = block(jnp.minimum(j + 1, jnp.maximum(nch * i - 1, 0)))[0]
                for c in range(nch):
                    s_buf[1 - cur, c] = scores(kn, c)
                vj = block(j)[1]
                for c in range(nch):
                    softmax_block(c, cur, vj)
            return carry

        lax.fori_loop(0, (nch // 2) * i, body, 0)
        for chain in range(nch):
            l = l_ref[chain, 0:1, :]
            o_ref[chain * bq:(chain + 1) * bq, :] = _bf((acc_ref[chain] / l).T)
            lse_ref[chain] = jnp.broadcast_to(m_ref[chain, 0:1, :] + jnp.log(l), (8, bq))

    return pl.pallas_call(
        kern, name="flash_fwd", grid=(HEADS, nq // nch),
        in_specs=[pl.BlockSpec((None, nch * bq, HEAD_PAD), lambda h, i: (h, i, 0)),
                  pl.BlockSpec((None, t, HEAD_PAD), lambda h, i: (h, 0, 0)),
                  pl.BlockSpec((None, t, V_HEAD), lambda h, i: (h, 0, 0))],
        out_specs=[pl.BlockSpec((nch * bq, V_HEAD), lambda h, i: (i, h)),
                   pl.BlockSpec((None, nch, 8, bq), lambda h, i: (h, i, 0, 0))],
        out_shape=[jax.ShapeDtypeStruct((t, HEADS * V_HEAD), BF16), jax.ShapeDtypeStruct((HEADS, nq, 8, bq), F32)],
        scratch_shapes=[pltpu.VMEM((2, nch, bq, bq), F32), pltpu.VMEM((nch, bq, bq), BF16),
                        pltpu.VMEM((nch, 8, bq), F32), pltpu.VMEM((nch, 8, bq), F32),
                        pltpu.VMEM((nch, V_HEAD, bq), F32)],
        compiler_params=_params("parallel", "arbitrary"))(q, k, v)


def _attn_delta(do, o):
    t = do.shape[0]
    bq = _attn_block(t)

    def kern(do_ref, o_ref, d_ref):
        for h in range(HEADS):
            cols = slice(h * V_HEAD, (h + 1) * V_HEAD)
            prod = do_ref[:, cols].astype(F32) * o_ref[:, cols].astype(F32)
            d_ref[h] = jnp.broadcast_to(jnp.sum(prod.T, axis=0, keepdims=True), (8, bq))

    blk = pl.BlockSpec((bq, HEADS * V_HEAD), lambda i: (i, 0))
    return pl.pallas_call(
        kern, name="attn_delta", grid=(t // bq,), in_specs=[blk, blk],
        out_specs=pl.BlockSpec((HEADS, None, 8, bq), lambda i: (0, i, 0, 0)),
        out_shape=jax.ShapeDtypeStruct((HEADS, t // bq, 8, bq), F32),
        compiler_params=_params("parallel"))(do, o)


def _flash_bwd(q, k, v, do, lse, delta):
    t = q.shape[1]
    bq = _attn_block(t)
    nq = t // bq

    def kern(q_ref, k_ref, v_ref, do_ref, lse_ref, del_ref, dq_ref, dk_ref, dv_ref, dvt_ref):
        j = pl.program_id(1)

        @pl.when(j == 0)
        def _():
            dq_ref[...] = jnp.zeros_like(dq_ref)

        dk_ref[...] = jnp.zeros_like(dk_ref)
        dvt_ref[...] = jnp.zeros_like(dvt_ref)
        kj, vj = k_ref[...], v_ref[...]

        def step(i, masked):
            rows = pl.ds(pl.multiple_of(i * bq, bq), bq)
            qi, doi = q_ref[rows, :], do_ref[rows, :]
            st = lax.dot_general(kj, qi, (((1,), (1,)), ((), ())), preferred_element_type=F32)
            pt = jnp.exp(st - lse_ref[i][0:1, :])
            if masked:
                pt = jnp.where(_chunk_mask(bq, bq), pt, 0.0)
            dpt = lax.dot_general(vj, doi, (((1,), (1,)), ((), ())), preferred_element_type=F32)
            dst = _bf(pt * (dpt - del_ref[i][0:1, :]))
            dvt_ref[...] += lax.dot_general(doi, _bf(pt), (((0,), (1,)), ((), ())), preferred_element_type=F32)
            dk_ref[...] += jnp.dot(dst, qi, preferred_element_type=F32)
            dq_ref[rows, :] += lax.dot_general(dst, kj, (((0,), (0,)), ((), ())), preferred_element_type=F32)

        step(j, True)

        def body(pair, carry):
            step(j + 1 + 2 * pair, False)
            step(j + 2 + 2 * pair, False)
            return carry

        rest = nq - 1 - j
        lax.fori_loop(0, rest // 2, body, 0)

        @pl.when(rest % 2 == 1)
        def _():
            step(nq - 1, False)

        dv_ref[...] = dvt_ref[...].T

    stat = pl.BlockSpec((None, nq, 8, bq), lambda h, j: (h, 0, 0, 0))
    return pl.pallas_call(
        kern, name="flash_bwd", grid=(HEADS, nq),
        in_specs=[pl.BlockSpec((None, t, HEAD_PAD), lambda h, j: (h, 0, 0)),
                  pl.BlockSpec((None, bq, HEAD_PAD), lambda h, j: (h, j, 0)),
                  pl.BlockSpec((None, bq, V_HEAD), lambda h, j: (h, j, 0)),
                  pl.BlockSpec((t, V_HEAD), lambda h, j: (0, h)), stat, stat],
        out_specs=[pl.BlockSpec((None, t, HEAD_PAD), lambda h, j: (h, 0, 0)),
                   pl.BlockSpec((None, bq, HEAD_PAD), lambda h, j: (h, j, 0)),
                   pl.BlockSpec((None, bq, V_HEAD), lambda h, j: (h, j, 0))],
        out_shape=[jax.ShapeDtypeStruct((HEADS, t, HEAD_PAD), F32), jax.ShapeDtypeStruct((HEADS, t, HEAD_PAD), F32),
                   jax.ShapeDtypeStruct((HEADS, t, V_HEAD), F32)],
        scratch_shapes=[pltpu.VMEM((V_HEAD, bq), F32)],
        compiler_params=_params("parallel", "arbitrary"))(q, k, v, do, lse, delta)


def _place():
    x, y, c = lax.axis_index("x"), lax.axis_index("y"), lax.axis_index("c")
    return x, y, c, [(1 - x, y), (x, 1 - y), (1 - x, 1 - y)]


def _all_gather_rows(block, name):
    m_per, n = block.shape

    def body(x_ref, out_ref, send_sems, recv_sems, local_sem):
        x, y, c, chips = _place()
        me, sibling = (x, y, c), (x, y, 1 - c)

        def rows(px, py, pc):
            return out_ref.at[pl.ds((4 * px + 2 * py + pc) * m_per, m_per), :]

        def copy(k, blk, to, src=None):
            return pltpu.make_async_remote_copy(
                src_ref=rows(*blk) if src is None else src, dst_ref=rows(*blk), send_sem=send_sems.at[k],
                recv_sem=recv_sems.at[k], device_id=to, device_id_type=MESH)

        mine = pltpu.make_async_copy(x_ref, rows(*me), local_sem)
        mine.start()
        first = [copy(0, me, sibling, src=x_ref)]
        first += [copy(1 + j, me, (*chip, c), src=x_ref) for j, chip in enumerate(chips)]
        for cp in first:
            cp.start()
        passed = [copy(4 + j, (*chip, c), sibling) for j, chip in enumerate(chips)]
        for j, chip in enumerate(chips):
            copy(1 + j, (*chip, c), me).wait_recv()
            passed[j].start()
        copy(0, sibling, me).wait_recv()
        for j, chip in enumerate(chips):
            copy(4 + j, (*chip, 1 - c), me).wait_recv()
        for cp in first + passed:
            cp.wait_send()
        mine.wait()

    return pl.pallas_call(
        body, name=name, out_shape=jax.ShapeDtypeStruct((8 * m_per, n), block.dtype),
        in_specs=[pl.BlockSpec(memory_space=pltpu.VMEM)], out_specs=pl.BlockSpec(memory_space=pltpu.VMEM),
        scratch_shapes=[pltpu.SemaphoreType.DMA((7,)), pltpu.SemaphoreType.DMA((7,)), pltpu.SemaphoreType.DMA],
        compiler_params=pltpu.CompilerParams(vmem_limit_bytes=VMEM_LIMIT_BYTES))(block)


HBM_SPEC = pl.BlockSpec(memory_space=pltpu.HBM)
SEM_SPEC = pl.BlockSpec(memory_space=pltpu.SEMAPHORE)
DATAFLOW = pltpu.SideEffectType.DATAFLOW_SIDE_EFFECTING


def _in_hbm(a):
    return pltpu.with_memory_space_constraint(a, pltpu.HBM)


def _chip_copies(ins, lands, send_sems, recv_sems, src_slot, half=False):
    n = len(ins)
    x, y, c, chips = _place()
    me = 2 * x + y
    if src_slot == "sibling":
        return [pltpu.make_async_remote_copy(src_ref=ins[w], dst_ref=lands[w], send_sem=send_sems.at[w],
                                             recv_sem=recv_sems.at[w], device_id=(x, y, 1 - c), device_id_type=MESH)
                for w in range(n)]

    def ends(w, chip):
        src = ins[w].at[2 * chip[0] + chip[1]] if src_slot else ins[w]
        if not half:
            return src, lands[w].at[me]
        rows = pl.ds(pl.multiple_of(c * (src.shape[0] // 2), 16), src.shape[0] // 2)
        return src.at[rows], lands[w].at[me, rows]

    copies = []
    for w in range(n):
        for p, chip in enumerate(chips):
            src, dst = ends(w, chip)
            copies.append(pltpu.make_async_remote_copy(
                src_ref=src, dst_ref=dst, send_sem=send_sems.at[p * n + w], recv_sem=recv_sems.at[p * n + w],
                device_id=(*chip, c), device_id_type=MESH))
    return copies


def _fill_halves(lands, name):
    n = len(lands)

    def body(*refs):
        bufs = refs[n:2 * n]
        send_sems, recv_sems = refs[2 * n:]
        x, y, c, chips = _place()
        copies = []
        for w in range(n):
            hr = bufs[w].shape[1] // 2
            for p, chip in enumerate(chips):
                part = bufs[w].at[2 * chip[0] + chip[1], pl.ds(pl.multiple_of(c * hr, 16), hr)]
                copies.append(pltpu.make_async_remote_copy(
                    src_ref=part, dst_ref=part, send_sem=send_sems.at[p * n + w], recv_sem=recv_sems.at[p * n + w],
                    device_id=(x, y, 1 - c), device_id_type=MESH))
        for cp in copies:
            cp.start()
        for cp in copies:
            cp.wait_send()
        for w in range(n):
            hr = bufs[w].shape[1] // 2
            for p, chip in enumerate(chips):
                part = bufs[w].at[2 * chip[0] + chip[1], pl.ds(pl.multiple_of((1 - c) * hr, 16), hr)]
                pltpu.make_async_remote_copy(
                    src_ref=part, dst_ref=part, send_sem=send_sems.at[p * n + w], recv_sem=recv_sems.at[p * n + w],
                    device_id=(x, y, 1 - c), device_id_type=MESH).wait_recv()

    any_spec = pl.BlockSpec(memory_space=pl.ANY)
    return list(pl.pallas_call(
        body, name=name, out_shape=[jax.ShapeDtypeStruct(a.shape, a.dtype) for a in lands],
        in_specs=[any_spec] * n, out_specs=[any_spec] * n, input_output_aliases={i: i for i in range(n)},
        scratch_shapes=[pltpu.SemaphoreType.DMA((3 * n,)), pltpu.SemaphoreType.DMA((3 * n,))])(*lands))


def _exchange_start(srcs, lands, src_slot, name, dep=None, half=False):
    n = len(srcs)
    first_out = 2 * n + (dep is not None)

    def body(*refs):
        for cp in _chip_copies(refs[:n], refs[n:2 * n], refs[first_out], refs[first_out + 1], src_slot, half):
            cp.start()
        token = refs[-1]
        token[...] = jnp.zeros_like(token)

    thru = [pltpu.HBM(a.shape, a.dtype) for a in list(srcs) + list(lands)]
    res = pl.pallas_call(
        body, name=name,
        out_shape=(pltpu.SemaphoreType.DMA((3 * n,)), pltpu.SemaphoreType.DMA((3 * n,)), *thru,
                   jax.ShapeDtypeStruct((8, 128), F32)),
        in_specs=[HBM_SPEC] * (2 * n) + ([pl.BlockSpec(memory_space=pl.ANY)] if dep is not None else []),
        out_specs=(SEM_SPEC, SEM_SPEC, *[HBM_SPEC] * (2 * n), pl.BlockSpec(memory_space=pltpu.VMEM)),
        input_output_aliases={i: 2 + i for i in range(2 * n)},
        compiler_params=pltpu.CompilerParams(has_side_effects=DATAFLOW))(
            *[_in_hbm(a) for a in srcs], *[_in_hbm(a) for a in lands], *([dep] if dep is not None else []))
    return (res[0], res[1], list(res[2:2 + n]), list(res[2 + n:2 + 2 * n])), res[-1]


def _exchange_wait(flight, after, src_slot, name, half=False):
    send_sems, recv_sems, srcs, lands = flight
    n = len(srcs)

    def body(*refs):
        for cp in _chip_copies(refs[:n], refs[n:2 * n], refs[2 * n], refs[2 * n + 1], src_slot, half):
            cp.wait_send()
            cp.wait_recv()

    thru = [pltpu.HBM(a.shape, a.dtype) for a in list(srcs) + list(lands)]
    res = pl.pallas_call(
        body, name=name, out_shape=thru,
        in_specs=[HBM_SPEC] * (2 * n) + [SEM_SPEC, SEM_SPEC, pl.BlockSpec(memory_space=pl.ANY)],
        out_specs=[HBM_SPEC] * (2 * n), input_output_aliases={i: i for i in range(2 * n)},
        compiler_params=pltpu.CompilerParams(has_side_effects=DATAFLOW))(*srcs, *lands, send_sems, recv_sems, after)
    return list(res[n:])


def _landing(own, me):
    return lax.dynamic_update_index_in_dim(lax.empty((4, *own.shape), own.dtype), own, me, 0)


def _as_rows(a):
    return a.reshape(-1, a.shape[-1])


def _row_tile(r, c, budget_bytes=1 << 20):
    tr = r
    while tr % 16 == 0 and tr * c * 4 > budget_bytes:
        tr //= 2
    return tr


def _sum_slots(layers, nlayer, name, into=None):
    _, r, c = layers[0][1].shape
    tr = _row_tile(r, c)
    nt = r // tr
    acc = into
    for l, r4 in layers:
        def kern(r_ref, *rest):
            o_ref = rest[-1]
            o_ref[...] = (((r_ref[0].astype(F32) + r_ref[1].astype(F32)) + r_ref[2].astype(F32))
                          + r_ref[3].astype(F32))

        out_spec = pl.BlockSpec((tr, c), lambda i, l=l: (l * nt + i, 0))
        first = acc is None
        acc = pl.pallas_call(
            kern, name=f"{name}_l{l}", grid=(nt,),
            in_specs=[pl.BlockSpec((4, tr, c), lambda i: (0, i, 0))]
            + ([] if first else [pl.BlockSpec(memory_space=pl.ANY)]),
            out_specs=out_spec, out_shape=jax.ShapeDtypeStruct((nlayer * r, c), F32),
            input_output_aliases={} if first else {1: 0},
            compiler_params=_params("parallel"))(*([r4] if first else [r4, acc]))
    return acc


def _adamw(w, m, v, parts, name):
    r, c = w.shape
    tr = _row_tile(r, c, 3 << 19)
    npart = len(parts)
    c1 = 1.0 - ADAM_B1 ** ADAM_STEP
    c2 = 1.0 - ADAM_B2 ** ADAM_STEP

    def kern(*refs):
        w_ref, m_ref, v_ref = refs[:3]
        p_refs = refs[3:3 + npart]
        g_ref, d_ref, mo_ref, vo_ref = refs[3 + npart:]
        g = p_refs[0][...]
        for p in p_refs[1:]:
            g = g + p[...]
        mn = ADAM_B1 * m_ref[...] + (1.0 - ADAM_B1) * g
        vn = ADAM_B2 * v_ref[...] + (1.0 - ADAM_B2) * (g * g)
        g_ref[...] = g
        mo_ref[...] = mn
        vo_ref[...] = vn
        d_ref[...] = -ADAM_LR * ((mn / c1) / (jnp.sqrt(vn / c2) + ADAM_EPS) + ADAM_WD * w_ref[...])

    blk = pl.BlockSpec((tr, c), lambda i: (i, 0))
    shape = jax.ShapeDtypeStruct((r, c), F32)
    return pl.pallas_call(
        kern, name=name, grid=(r // tr,), in_specs=[blk] * (3 + npart), out_specs=[blk] * 4, out_shape=[shape] * 4,
        compiler_params=_params("parallel"))(w, m, v, *parts)


def _sum_devices(g8, name):
    _, r, c = g8.shape

    def kern(g_ref, o_ref):
        tot = g_ref[0]
        for dev in range(1, 8):
            tot = tot + g_ref[dev]
        o_ref[...] = tot

    return pl.pallas_call(
        kern, name=name, grid=(1,), in_specs=[pl.BlockSpec((8, r, c), lambda i: (0, 0, 0))],
        out_specs=pl.BlockSpec((r, c), lambda i: (0, 0)), out_shape=jax.ShapeDtypeStruct((r, c), F32),
        compiler_params=_params("arbitrary"))(g8)


def _pad_lanes(a, width):
    return jnp.pad(a, [(0, 0)] * (a.ndim - 1) + [(0, width - a.shape[-1])])


def kernel(x, positions, ffn_norm1, ffn1_w1, ffn1_w3, ffn1_w2, mix_norm, ffn_norm2, ffn2_w1, ffn2_w3, ffn2_w2, conv_w_pw1, conv_w_dw, conv_norm, conv_w_pw2, mla_w_a, mla_q_norm, mla_kv_norm, mla_w_uq, mla_w_ukv, mla_w_o, final_norm, loss_target, m_ffn_norm1, m_ffn1_w1, m_ffn1_w3, m_ffn1_w2, m_mix_norm, m_ffn_norm2, m_ffn2_w1, m_ffn2_w3, m_ffn2_w2, m_conv_w_pw1, m_conv_w_dw, m_conv_norm, m_conv_w_pw2, m_mla_w_a, m_mla_q_norm, m_mla_kv_norm, m_mla_w_uq, m_mla_w_ukv, m_mla_w_o, m_final_norm, v_ffn_norm1, v_ffn1_w1, v_ffn1_w3, v_ffn1_w2, v_mix_norm, v_ffn_norm2, v_ffn2_w1, v_ffn2_w3, v_ffn2_w2, v_conv_w_pw1, v_conv_w_dw, v_conv_norm, v_conv_w_pw2, v_mla_w_a, v_mla_q_norm, v_mla_kv_norm, v_mla_w_uq, v_mla_w_ukv, v_mla_w_o, v_final_norm):
    given = locals()
    return _step({nm: given[nm] for nm in INPUTS})


def _step(A):
    x = A['x'][0]
    target = A['loss_target'][0]
    t, d = x.shape
    pos = A['positions'].reshape(t, 1)
    me = 2 * lax.axis_index("x") + lax.axis_index("y")

    flipped = {f'ffn{k}_{w}' for k in (1, 2) for w in ('w1', 'w3')}
    P = {}
    for nm in BIG:
        for key in (nm, 'm_' + nm, 'v_' + nm):
            P[key] = jnp.swapaxes(A[key], 1, 2) if nm in flipped else A[key]

    def unflip(nm, a):
        return jnp.swapaxes(a, 1, 2) if nm in flipped else a

    ffn = [f'ffn{k}_{w}' for k in (1, 2) for w in ('w1', 'w3', 'w2')]
    gather_groups = [[(nm, 0) for nm in ffn[:3]],
                     [('conv_w_pw1', 0), ('conv_w_pw2', 0)] + [(nm, 0) for nm in ffn[3:]],
                     [(nm, 1) for nm in ffn[:3]] + [('mla_w_a', 0), ('mla_w_uq', 0), ('mla_w_ukv', 0), ('mla_w_o', 0)],
                     [(nm, 1) for nm in ffn[3:]]]
    halved = (0, 1)
    gather_flights = {}
    big = {}

    def gather_start(gi, dep):
        shards = [_bf(P[nm][l]) for nm, l in gather_groups[gi]]
        gather_flights[gi], token = _exchange_start(shards, [_landing(s, me) for s in shards], False,
                                                    f"gather_start_{gi}", dep, half=gi in halved)
        return token

    def gather_wait(gi, after):
        landed = _exchange_wait(gather_flights[gi], after, False, f"gather_wait_{gi}", half=gi in halved)
        if gi in halved:
            landed = _fill_halves(landed, f"gather_fill_{gi}")
        big.update(zip(gather_groups[gi], landed))
        return landed[0]

    dw_shard = A['conv_w_dw'][0]
    cw = dw_shard.shape[1]
    small = jnp.concatenate([
        jnp.pad(dw_shard, ((0, CONV_HALO - CONV_WIDTH), (0, 0))),
        jnp.pad(_pad_lanes(A['mla_q_norm'], cw), ((0, 7), (0, 0))),
        jnp.pad(_pad_lanes(A['mla_kv_norm'], cw), ((0, 7), (0, 0)))], axis=0)
    small = _all_gather_rows(small, "gather_small_weights").reshape(4, 2, 48, cw)[:, 0]
    w_dw = jnp.concatenate([small[j, :CONV_HALO] for j in range(4)], axis=1)
    gq = jnp.concatenate([small[j, CONV_HALO, :Q_LORA // 4] for j in range(4)])
    gkv = jnp.concatenate([small[j, CONV_HALO + 8, :KV_LORA // 4] for j in range(4)])

    def rows(nm, layer):
        g = big[nm, layer]
        return g.reshape(-1, g.shape[-1])

    ffn_w = {}

    def ffn_weights(k, l):
        ffn_w[k, l] = (rows(f'ffn{k}_w1', l), rows(f'ffn{k}_w3', l), rows(f'ffn{k}_w2', l))
        return ffn_w[k, l]

    token = gather_start(0, small)
    cs_c, cs_s = _rope_tables(pos)
    h0 = x
    token = gather_start(1, gather_wait(0, token))
    h1, n01, z01a, z01b = _ffn_fwd(h0, A['ffn_norm1'][0], *ffn_weights(1, 0), token, "ffn1_l0_fwd")
    token = gather_start(3, gather_start(2, gather_wait(1, h1)))
    pw1 = big['conv_w_pw1', 0]
    pw1_a = jnp.concatenate([pw1[0], pw1[1]], axis=1)
    pw1_b = jnp.concatenate([pw1[2], pw1[3]], axis=1)
    pw2 = rows('conv_w_pw2', 0)
    m0 = _norm_fwd(h1, A['mix_norm'][0], token, "mix_norm_l0")
    ca, cb, glu = _glu_fwd(m0, pw1_a, pw1_b)
    cv, cs = _conv_fwd(glu, w_dw, A['conv_norm'][0])
    h2 = _mm([(cs, pw2)], F32, "conv_pw2_fwd", res=h1)
    h3, n02, z02a, z02b = _ffn_fwd(h2, A['ffn_norm2'][0], *ffn_weights(2, 0), token, "ffn2_l0_fwd")
    gather_wait(2, h3)
    w_a = _pad_lanes(rows('mla_w_a', 0), A_PAD)
    wuq = _pad_lanes(big['mla_w_uq', 0].reshape(Q_LORA, HEADS, NOPE + ROPE), HEAD_PAD).reshape(Q_LORA, -1)
    wukv = big['mla_w_ukv', 0].reshape(KV_LORA, HEADS * (NOPE + V_HEAD))
    w_o = rows('mla_w_o', 0)
    h4, n11, z11a, z11b = _ffn_fwd(h3, A['ffn_norm1'][1], *ffn_weights(1, 1), token, "ffn1_l1_fwd")
    m1 = _norm_fwd(h4, A['mix_norm'][1], token, "mix_norm_l1")
    a_lat = _mm([(m1, w_a)], F32, "mla_down_fwd")
    cq, ckv, kr = _mla_prep(a_lat, gq, gkv, cs_c, cs_s)
    q, k, v = _mla_qkv(cq, ckv, kr, cs_c, cs_s, wuq, wukv)
    o, lse = _flash_fwd(q, k, v)
    h5 = _mm([(o, w_o)], F32, "mla_out_fwd", res=h4)
    gather_wait(3, h5)
    h6, n12, z12a, z12b = _ffn_fwd(h5, A['ffn_norm2'][1], *ffn_weights(2, 1), token, "ffn2_l1_fwd")

    def row_slots(g):
        return g.reshape(4, g.shape[0] // 4, g.shape[1])

    scatter_flights = []

    def scatter_start(named, dep=None):
        srcs = [g for _, g in named]
        lands = [_landing(lax.dynamic_index_in_dim(g, me, 0, keepdims=False), me) for g in srcs]
        flight, token = _exchange_start(srcs, lands, True, f"scatter_start_{len(scatter_flights)}", dep)
        scatter_flights.append(([key for key, _ in named], flight))
        return token

    def send_ffn(k, l, dw1t, dw3t, dw2, dep=None):
        return scatter_start([((f'ffn{k}_w1', l), row_slots(dw1t)), ((f'ffn{k}_w3', l), row_slots(dw3t)),
                              ((f'ffn{k}_w2', l), row_slots(dw2))], dep)

    dh6, dg_final, loss_part = _loss_bwd(h6, target, A['final_norm'])
    dh5, dg_n2_l1, *dws = _ffn_bwd(dh6, h5, A['ffn_norm2'][1], n12, z12a, z12b, *ffn_w[2, 1], loss_part, "ffn2_l1")
    token = send_ffn(2, 1, *dws)

    do = _mm([(dh5, w_o)], BF16, "mla_out_bwd", trans_b=True, dep=token)
    dw_o = _mm_tn(o, dh5, BF16, "mla_dw_o")
    delta = _attn_delta(do, o)
    dq, dk, dv = _flash_bwd(q, k, v, do, lse, delta)
    dr, dkv, dcq, dckv, dar = _mla_qkv_bwd(dq, dk, dv, cs_c, cs_s, wuq, wukv)
    dwuq = _mm_tn(cq, dr, BF16, "mla_dw_uq", bn=dr.shape[1] // 2)
    dwukv = _mm_tn(ckv, dkv, BF16, "mla_dw_ukv", bn=dkv.shape[1] // 2)
    da_lat, dgq, dgkv = _mla_prep_bwd(a_lat, dcq, dckv, dar, gq, gkv)
    dw_a = _mm_tn(m1, da_lat, BF16, "mla_dw_a")
    token = scatter_start([
        (('mla_w_a', 0), row_slots(dw_a[:, :Q_LORA + KV_LORA + ROPE])),
        (('mla_w_uq', 0), dwuq.reshape(4, Q_LORA // 4, HEADS, HEAD_PAD)[..., :NOPE + ROPE]),
        (('mla_w_ukv', 0), dwukv.reshape(4, KV_LORA // 4, HEADS, NOPE + V_HEAD)),
        (('mla_w_o', 0), row_slots(dw_o))])
    dh4, dg_mix_l1 = _mm_normbwd([(da_lat, w_a)], h4, A['mix_norm'][1], dh5, token, "mla_down_bwd")

    dh3, dg_n1_l1, *dws = _ffn_bwd(dh4, h3, A['ffn_norm1'][1], n11, z11a, z11b, *ffn_w[1, 1], token, "ffn1_l1")
    token = send_ffn(1, 1, *dws)
    dh2, dg_n2_l0, *dws = _ffn_bwd(dh3, h2, A['ffn_norm2'][0], n02, z02a, z02b, *ffn_w[2, 0], token, "ffn2_l0")
    token = send_ffn(2, 0, *dws)

    dcv, dg_conv = _conv_bwd_norm(dh2, cv, pw2, A['conv_norm'][0], token)
    dw_pw2 = _mm_tn(cs, dh2, BF16, "conv_dw_pw2")
    dca, dcb, ddw = _conv_bwd_dw(dcv, glu, ca, cb, w_dw)
    dpw1_a = _mm_tn(m0, dca, BF16, "conv_dw_pw1a")
    dpw1_b = _mm_tn(m0, dcb, BF16, "conv_dw_pw1b")
    half = dpw1_a.shape[1] // 2
    token = scatter_start([
        (('conv_w_pw1', 0), jnp.stack([dpw1_a[:, :half], dpw1_a[:, half:], dpw1_b[:, :half], dpw1_b[:, half:]])),
        (('conv_w_pw2', 0), row_slots(dw_pw2))])
    dh1, dg_mix_l0 = _mm_normbwd([(dca, pw1_a), (dcb, pw1_b)], h1, A['mix_norm'][0], dh2, token, "conv_pw1_bwd")

    dx, dg_n1_l0, *dws = _ffn_bwd(dh1, h0, A['ffn_norm1'][0], n01, z01a, z01b, *ffn_w[1, 0], token, "ffn1_l0")
    out = {}

    received = {}

    def scatter_wait(si, after):
        keys, flight = scatter_flights[si]
        landed = _exchange_wait(flight, after, True, f"scatter_wait_{si}")
        received.update(zip(keys, landed))
        return landed[0]

    def slots(nm, l):
        return received[nm, l].reshape(4, -1, received[nm, l].shape[-1])

    def sibling_start(sums, tag):
        return _exchange_start(sums, [lax.empty(s.shape, s.dtype) for s in sums], "sibling", "swap_start_" + tag)

    def finish(names, sums, flight, after, tag):
        for nm, mine, theirs in zip(names, sums, _exchange_wait(flight, after, "sibling", "swap_wait_" + tag)):
            res = _adamw(_as_rows(P[nm]), _as_rows(P['m_' + nm]), _as_rows(P['v_' + nm]), [mine, theirs],
                         "adamw_" + nm)
            out[nm] = [unflip(nm, r.reshape(P[nm].shape)) for r in res]
        return res[1]

    qkv_row = jnp.concatenate([dgq, dgkv, jnp.zeros((8, d - Q_LORA - KV_LORA), F32)], axis=1)
    loss_row = _pad_lanes(loss_part, d)
    small_g = jnp.concatenate([dg_n1_l0, dg_n1_l1, dg_mix_l0, dg_mix_l1, dg_n2_l0, dg_n2_l1, dg_conv, dg_final,
                               qkv_row, loss_row, ddw], axis=0)
    nrow = small_g.shape[0]
    tot = _sum_devices(_all_gather_rows(small_g, "gather_small_grads").reshape(8, nrow, d), "sum_small_grads")
    loss = tot[72, 0]
    q_shard = lax.dynamic_slice_in_dim(tot[64, :Q_LORA], me * (Q_LORA // 4), Q_LORA // 4)
    kv_shard = lax.dynamic_slice_in_dim(tot[64, Q_LORA:Q_LORA + KV_LORA], me * (KV_LORA // 4), KV_LORA // 4)
    dw_shard_g = lax.dynamic_slice_in_dim(tot[80:80 + CONV_WIDTH], me * cw, cw, axis=1)
    small_grads = {
        'ffn_norm1': jnp.stack([tot[0], tot[8]]), 'mix_norm': jnp.stack([tot[16], tot[24]]),
        'ffn_norm2': jnp.stack([tot[32], tot[40]]), 'conv_norm': tot[48][None], 'final_norm': tot[56],
        'mla_q_norm': q_shard[None], 'mla_kv_norm': kv_shard[None], 'conv_w_dw': dw_shard_g[None],
    }
    for nm, g in small_grads.items():
        res = _adamw(_as_rows(A[nm]) if A[nm].ndim > 1 else A[nm].reshape(1, -1),
                     A['m_' + nm].reshape(-1, A[nm].shape[-1]), A['v_' + nm].reshape(-1, A[nm].shape[-1]),
                     [g.reshape(-1, A[nm].shape[-1])], "adamw_" + nm)
        out[nm] = [r.reshape(A[nm].shape) for r in res]

    after = send_ffn(1, 0, *dws, dep=tot)
    last = len(scatter_flights) - 1
    for si in range(last):
        after = scatter_wait(si, after)
    late = ffn[:3]
    early = [nm for nm in BIG if nm not in late]
    early_sums = [_sum_slots([(l, slots(nm, l)) for l in range(A[nm].shape[0])], A[nm].shape[0], "sum_" + nm)
                  for nm in early]
    early_flight, token = sibling_start(early_sums, "early")
    late_l1 = [_sum_slots([(1, slots(nm, 1))], 2, "sum_" + nm) for nm in late]
    scatter_wait(last, token)
    late_sums = [_sum_slots([(0, slots(nm, 0))], 2, "sum_" + nm, into=part) for nm, part in zip(late, late_l1)]
    late_flight, token = sibling_start(late_sums, "late")
    after = finish(early, early_sums, early_flight, token, "early")
    finish(late, late_sums, late_flight, after, "late")

    return (loss, dx[None], *[out[nm][0] for nm in WEIGHTS], *[out[nm][1] for nm in WEIGHTS],
            *[out[nm][2] for nm in WEIGHTS], *[out[nm][3] for nm in WEIGHTS])
```

```python
import functools

import jax
import jax.numpy as jnp
import numpy as np
from jax import lax
from jax.experimental import pallas as pl
from jax.experimental.pallas import tpu as pltpu

F32 = jnp.float32
BF16 = jnp.bfloat16
MESH = pl.DeviceIdType.MESH

RMS_EPS = 1e-6
HEADS = 8
NOPE = 128
ROPE = 64
HEAD_PAD = 256
V_HEAD = 128
Q_LORA = 512
KV_LORA = 256
A_PAD = 896
CHUNK = 64
CONV_WIDTH = 31
CONV_HALO = 32
CONV_ROWS = 16
ROPE_THETA = 10000.0
ATTN_SCALE = (NOPE + ROPE) ** -0.5
FFN_RES = 0.5

ADAM_LR = 0.001
ADAM_B1 = 0.9
ADAM_B2 = 0.999
ADAM_EPS = 1e-08
ADAM_WD = 0.01
ADAM_STEP = 10

VMEM_LIMIT_BYTES = 56 * 1024 * 1024

WEIGHTS = ['ffn_norm1', 'ffn1_w1', 'ffn1_w3', 'ffn1_w2', 'mix_norm', 'ffn_norm2', 'ffn2_w1', 'ffn2_w3', 'ffn2_w2',
           'conv_w_pw1', 'conv_w_dw', 'conv_norm', 'conv_w_pw2', 'mla_w_a', 'mla_q_norm', 'mla_kv_norm', 'mla_w_uq',
           'mla_w_ukv', 'mla_w_o', 'final_norm']
INPUTS = (['x', 'positions'] + WEIGHTS + ['loss_target'] + ['m_' + w for w in WEIGHTS] + ['v_' + w for w in WEIGHTS])
BIG = ['ffn1_w1', 'ffn1_w3', 'ffn1_w2', 'ffn2_w1', 'ffn2_w3', 'ffn2_w2', 'conv_w_pw1', 'conv_w_pw2', 'mla_w_a',
       'mla_w_uq', 'mla_w_ukv', 'mla_w_o']


def _params(*sem):
    return pltpu.CompilerParams(dimension_semantics=sem, vmem_limit_bytes=VMEM_LIMIT_BYTES)


def _bf(v):
    return v.astype(BF16)


def _rstd(x):
    return lax.rsqrt(jnp.mean(x * x, axis=-1, keepdims=True) + RMS_EPS)


def _sigmoid(x):
    return jax.nn.sigmoid(x)


def _rot(x):
    lane = lax.broadcasted_iota(jnp.int32, x.shape, 1)
    return jnp.where(lane < ROPE // 2, -pltpu.roll(x, 128 - ROPE // 2, 1), pltpu.roll(x, ROPE // 2, 1))


def _rot_t(y):
    lane = lax.broadcasted_iota(jnp.int32, y.shape, 1)
    return jnp.where(lane < ROPE // 2, pltpu.roll(y, 128 - ROPE // 2, 1), -pltpu.roll(y, ROPE // 2, 1))


def _pair_sum(a_refs, b_refs, trans_b):
    tot = None
    for a_r, b_r in zip(a_refs, b_refs):
        a, b = _bf(a_r[...]), _bf(b_r[...])
        if trans_b:
            d = lax.dot_general(a, b, (((1,), (1,)), ((), ())), preferred_element_type=F32)
        else:
            d = jnp.dot(a, b, preferred_element_type=F32)
        tot = d if tot is None else tot + d
    return tot


def _mm(pairs, out_dtype, name, *, trans_b=False, tm=512, tn=None, tk=None, res=None, dep=None):
    m, k = pairs[0][0].shape
    n = pairs[0][1].shape[0] if trans_b else pairs[0][1].shape[1]
    tm, tn, tk = min(tm, m), tn or n, tk or k
    nk, npair = k // tk, len(pairs)

    def kern(*refs):
        a_refs, b_refs = refs[:npair], refs[npair:2 * npair]
        rest = list(refs[2 * npair:])
        res_ref = rest.pop(0) if res is not None else None
        if dep is not None:
            rest.pop(0)
        o_ref = rest.pop(0)

        def finish(acc):
            if res_ref is not None:
                acc = res_ref[...] + acc
            o_ref[...] = acc.astype(o_ref.dtype)

        if nk == 1:
            finish(_pair_sum(a_refs, b_refs, trans_b))
        else:
            acc_ref = rest.pop(0)
            kk = pl.program_id(2)

            @pl.when(kk == 0)
            def _():
                acc_ref[...] = jnp.zeros_like(acc_ref)

            acc_ref[...] += _pair_sum(a_refs, b_refs, trans_b)

            @pl.when(kk == nk - 1)
            def _():
                finish(acc_ref[...])

    a_spec = pl.BlockSpec((tm, tk), lambda i, j, kk: (i, kk))
    b_spec = (pl.BlockSpec((tn, tk), lambda i, j, kk: (j, kk)) if trans_b
              else pl.BlockSpec((tk, tn), lambda i, j, kk: (kk, j)))
    io_spec = pl.BlockSpec((tm, tn), lambda i, j, kk: (i, j))
    in_specs = ([a_spec] * npair + [b_spec] * npair + ([io_spec] if res is not None else [])
                + ([pl.BlockSpec((8, 128), lambda i, j, kk: (0, 0))] if dep is not None else []))
    args = ([p[0] for p in pairs] + [p[1] for p in pairs] + ([res] if res is not None else [])
            + ([dep] if dep is not None else []))
    return pl.pallas_call(
        kern, name=name, grid=(m // tm, n // tn, nk), in_specs=in_specs, out_specs=io_spec,
        out_shape=jax.ShapeDtypeStruct((m, n), out_dtype),
        scratch_shapes=[pltpu.VMEM((tm, tn), F32)] if nk > 1 else [],
        compiler_params=_params("parallel", "parallel", "arbitrary"))(*args)


def _mm_normbwd(pairs, h, g, dres, dep, name, *, tm=512, tk=None):
    m, k = pairs[0][0].shape
    d = pairs[0][1].shape[0]
    tm, tk = min(tm, m), tk or k
    nk, npair = k // tk, len(pairs)

    def kern(*refs):
        a_refs, b_refs = refs[:npair], refs[npair:2 * npair]
        h_ref, g_ref, dres_ref, _, o_ref, dg_ref, acc_ref = refs[2 * npair:]
        i, kk = pl.program_id(0), pl.program_id(1)

        @pl.when(jnp.logical_and(i == 0, kk == 0))
        def _():
            dg_ref[...] = jnp.zeros_like(dg_ref)

        @pl.when(kk == 0)
        def _():
            acc_ref[...] = jnp.zeros_like(acc_ref)

        acc_ref[...] += _pair_sum(a_refs, b_refs, True)

        @pl.when(kk == nk - 1)
        def _():
            dn = acc_ref[...]
            x = h_ref[...]
            rstd = _rstd(x)
            xhat = x * rstd
            dg_ref[...] += jnp.broadcast_to(jnp.sum(dn * xhat, axis=0, keepdims=True), dg_ref.shape)
            dxh = dn * g_ref[...]
            dx = rstd * (dxh - xhat * jnp.mean(dxh * xhat, axis=-1, keepdims=True))
            o_ref[...] = dres_ref[...] + dx

    row = pl.BlockSpec((tm, d), lambda i, kk: (i, 0))
    in_specs = ([pl.BlockSpec((tm, tk), lambda i, kk: (i, kk))] * npair
                + [pl.BlockSpec((d, tk), lambda i, kk: (0, kk))] * npair
                + [row, pl.BlockSpec((1, d), lambda i, kk: (0, 0)), row, pl.BlockSpec((8, 128), lambda i, kk: (0, 0))])
    return pl.pallas_call(
        kern, name=name, grid=(m // tm, nk), in_specs=in_specs,
        out_specs=[row, pl.BlockSpec((8, d), lambda i, kk: (0, 0))],
        out_shape=[jax.ShapeDtypeStruct((m, d), F32), jax.ShapeDtypeStruct((8, d), F32)],
        scratch_shapes=[pltpu.VMEM((tm, d), F32)],
        compiler_params=_params("arbitrary", "arbitrary"))(
            *[p[0] for p in pairs], *[p[1] for p in pairs], h, g.reshape(1, d), dres, dep)


def _mm_tn(a, b, out_dtype, name, *, bm=None, bn=None, tk=1024):
    t, m = a.shape
    batched = b.ndim == 3
    n = b.shape[-1]
    nb = b.shape[0] if batched else 1
    bm, bn, tk = bm or m, bn or n, min(tk, t)
    nk = t // tk

    def kern(a_ref, b_ref, o_ref, acc_ref):
        kk = pl.program_id(3)

        @pl.when(kk == 0)
        def _():
            acc_ref[...] = jnp.zeros_like(acc_ref)

        acc_ref[...] += lax.dot_general(_bf(a_ref[...]), _bf(b_ref[...]), (((0,), (0,)), ((), ())),
                                        preferred_element_type=F32)

        @pl.when(kk == nk - 1)
        def _():
            o_ref[...] = acc_ref[...].astype(o_ref.dtype)

    a_spec = pl.BlockSpec((tk, bm), lambda h, i, j, kk: (kk, i))
    if batched:
        b_spec = pl.BlockSpec((None, tk, bn), lambda h, i, j, kk: (h, kk, j))
        o_spec = pl.BlockSpec((None, bm, bn), lambda h, i, j, kk: (h, i, j))
        out_shape = jax.ShapeDtypeStruct((nb, m, n), out_dtype)
    else:
        b_spec = pl.BlockSpec((tk, bn), lambda h, i, j, kk: (kk, j))
        o_spec = pl.BlockSpec((bm, bn), lambda h, i, j, kk: (i, j))
        out_shape = jax.ShapeDtypeStruct((m, n), out_dtype)
    return pl.pallas_call(
        kern, name=name, grid=(nb, m // bm, n // bn, nk), in_specs=[a_spec, b_spec], out_specs=o_spec,
        out_shape=out_shape, scratch_shapes=[pltpu.VMEM((bm, bn), F32)],
        compiler_params=_params("parallel", "parallel", "parallel", "arbitrary"))(a, b)


def _ffn_tile(f):
    return f // 2 if (f // 2) % 128 == 0 else f


def _ffn_fwd(h, g, w1t, w3t, w2, dep, name):
    t, d = h.shape
    f = w1t.shape[0]
    tm = min(256, t)
    nt = (((1,), (1,)), ((), ()))

    def kern(h_ref, g_ref, w1_hbm, w3_hbm, w2_hbm, dep_ref, ho_ref, n_ref, z1_ref, z3_ref,
             w1_ref, w3_ref, w2_ref, sems):
        @pl.when(pl.program_id(0) == 0)
        def _():
            copies = [pltpu.make_async_copy(src, dst, sems.at[k]) for k, (src, dst) in
                      enumerate(((w1_hbm, w1_ref), (w3_hbm, w3_ref), (w2_hbm, w2_ref)))]
            for cp in copies:
                cp.start()
            for cp in copies:
                cp.wait()

        x = h_ref[...]
        n = _bf(x * _rstd(x) * g_ref[...])
        n_ref[...] = n
        z1 = lax.dot_general(n, w1_ref[...], nt, preferred_element_type=F32)
        z3 = lax.dot_general(n, w3_ref[...], nt, preferred_element_type=F32)
        z1_ref[...] = _bf(z1)
        z3_ref[...] = _bf(z3)
        act = _bf(z1 * _sigmoid(z1) * z3)
        ho_ref[...] = x + FFN_RES * jnp.dot(act, w2_ref[...], preferred_element_type=F32)

    row = pl.BlockSpec((tm, d), lambda i: (i, 0))
    col = pl.BlockSpec((tm, f), lambda i: (i, 0))
    whole = pl.BlockSpec(memory_space=pl.ANY)
    return pl.pallas_call(
        kern, name=name, grid=(t // tm,),
        in_specs=[row, pl.BlockSpec((1, d), lambda i: (0, 0)), whole, whole, whole,
                  pl.BlockSpec((8, 128), lambda i: (0, 0))],
        out_specs=[row, row, col, col],
        out_shape=[jax.ShapeDtypeStruct((t, d), F32), jax.ShapeDtypeStruct((t, d), BF16),
                   jax.ShapeDtypeStruct((t, f), BF16), jax.ShapeDtypeStruct((t, f), BF16)],
        scratch_shapes=[pltpu.VMEM((f, d), BF16), pltpu.VMEM((f, d), BF16), pltpu.VMEM((f, d), BF16),
                        pltpu.SemaphoreType.DMA((3,))],
        compiler_params=_params("arbitrary"))(h, g.reshape(1, d), w1t, w3t, w2, dep)


def _ffn_bwd_x(dh, h_in, g, z1, z3, w1t, w3t, w2, dep, name):
    t, d = dh.shape
    f = z1.shape[1]
    tm = min(256, t)

    def kern(dh_ref, h_ref, g_ref, z1_ref, z3_ref, w2_hbm, w1_hbm, w3_hbm, dep_ref,
             o_ref, dg_ref, dz1_ref, dz3_ref, a_ref, df_ref, w2_ref, w1_ref, w3_ref, sems):
        @pl.when(pl.program_id(0) == 0)
        def _():
            copies = [pltpu.make_async_copy(src, dst, sems.at[k]) for k, (src, dst) in
                      enumerate(((w2_hbm, w2_ref), (w1_hbm, w1_ref), (w3_hbm, w3_ref)))]
            for cp in copies:
                cp.start()
            dg_ref[...] = jnp.zeros_like(dg_ref)
            for cp in copies:
                cp.wait()

        df = _bf(FFN_RES * dh_ref[...])
        df_ref[...] = df
        da = lax.dot_general(df, w2_ref[...], (((1,), (1,)), ((), ())), preferred_element_type=F32)
        z1v, z3v = z1_ref[...].astype(F32), z3_ref[...].astype(F32)
        sig = _sigmoid(z1v)
        silu = z1v * sig
        a_ref[...] = _bf(silu * z3v)
        dz1 = _bf(da * z3v * (sig * (1.0 + z1v * (1.0 - sig))))
        dz3 = _bf(da * silu)
        dz1_ref[...] = dz1
        dz3_ref[...] = dz3
        dn = (jnp.dot(dz1, w1_ref[...], preferred_element_type=F32)
              + jnp.dot(dz3, w3_ref[...], preferred_element_type=F32))
        x = h_ref[...]
        rstd = _rstd(x)
        xhat = x * rstd
        dg_ref[...] += jnp.broadcast_to(jnp.sum(dn * xhat, axis=0, keepdims=True), dg_ref.shape)
        dxh = dn * g_ref[...]
        o_ref[...] = dh_ref[...] + rstd * (dxh - xhat * jnp.mean(dxh * xhat, axis=-1, keepdims=True))

    row = pl.BlockSpec((tm, d), lambda i: (i, 0))
    col = pl.BlockSpec((tm, f), lambda i: (i, 0))
    whole = pl.BlockSpec(memory_space=pl.ANY)
    colshape = jax.ShapeDtypeStruct((t, f), BF16)
    return pl.pallas_call(
        kern, name=name, grid=(t // tm,),
        in_specs=[row, row, pl.BlockSpec((1, d), lambda i: (0, 0)), col, col, whole, whole, whole,
                  pl.BlockSpec((8, 128), lambda i: (0, 0))],
        out_specs=[row, pl.BlockSpec((8, d), lambda i: (0, 0)), col, col, col, row],
        out_shape=[jax.ShapeDtypeStruct((t, d), F32), jax.ShapeDtypeStruct((8, d), F32), colshape, colshape, colshape,
                   jax.ShapeDtypeStruct((t, d), BF16)],
        scratch_shapes=[pltpu.VMEM((f, d), BF16), pltpu.VMEM((f, d), BF16), pltpu.VMEM((f, d), BF16),
                        pltpu.SemaphoreType.DMA((3,))],
        compiler_params=_params("arbitrary"))(dh, h_in, g.reshape(1, d), z1, z3, w2, w1t, w3t, dep)


def _ffn_bwd(dh, h_in, g, n, z1, z3, w1t, w3t, w2, dep, tag):
    f = w2.shape[0]
    dh_in, dg, dz1, dz3, act, df = _ffn_bwd_x(dh, h_in, g, z1, z3, w1t, w3t, w2, dep, tag + "_bwd_x")
    dw1t = _mm_tn(dz1, n, BF16, tag + "_dw1", bm=_ffn_tile(f), tk=2048)
    dw3t = _mm_tn(dz3, n, BF16, tag + "_dw3", bm=_ffn_tile(f), tk=2048)
    dw2 = _mm_tn(act, df, BF16, tag + "_dw2", bm=_ffn_tile(f), tk=2048)
    return dh_in, dg, dw1t, dw3t, dw2


def _norm_fwd(h, g, dep, name):
    t, d = h.shape
    tm = min(512, t)

    def kern(h_ref, g_ref, dep_ref, o_ref):
        x = h_ref[...]
        o_ref[...] = _bf(x * _rstd(x) * g_ref[...])

    row = pl.BlockSpec((tm, d), lambda i: (i, 0))
    return pl.pallas_call(
        kern, name=name, grid=(t // tm,),
        in_specs=[row, pl.BlockSpec((1, d), lambda i: (0, 0)), pl.BlockSpec((8, 128), lambda i: (0, 0))],
        out_specs=row, out_shape=jax.ShapeDtypeStruct((t, d), BF16),
        compiler_params=_params("parallel"))(h, g.reshape(1, d), dep)


def _loss_bwd(h, target, g):
    t, d = h.shape
    tm = min(512, t)

    def kern(h_ref, t_ref, g_ref, dh_ref, dg_ref, loss_ref):
        @pl.when(pl.program_id(0) == 0)
        def _():
            dg_ref[...] = jnp.zeros_like(dg_ref)
            loss_ref[...] = jnp.zeros_like(loss_ref)

        x = h_ref[...]
        rstd = _rstd(x)
        xhat = x * rstd
        err = xhat * g_ref[...] - t_ref[...]
        row_loss = jnp.sum(err * err, axis=-1, keepdims=True) * (0.5 / d)
        loss_ref[...] += jnp.broadcast_to(jnp.sum(row_loss, axis=0, keepdims=True), loss_ref.shape)
        dy = err * (1.0 / d)
        dg_ref[...] += jnp.broadcast_to(jnp.sum(dy * xhat, axis=0, keepdims=True), dg_ref.shape)
        dxh = dy * g_ref[...]
        dh_ref[...] = rstd * (dxh - xhat * jnp.mean(dxh * xhat, axis=-1, keepdims=True))

    row = pl.BlockSpec((tm, d), lambda i: (i, 0))
    return pl.pallas_call(
        kern, name="loss_bwd", grid=(t // tm,),
        in_specs=[row, row, pl.BlockSpec((1, d), lambda i: (0, 0))],
        out_specs=[row, pl.BlockSpec((8, d), lambda i: (0, 0)), pl.BlockSpec((8, 128), lambda i: (0, 0))],
        out_shape=[jax.ShapeDtypeStruct((t, d), F32), jax.ShapeDtypeStruct((8, d), F32),
                   jax.ShapeDtypeStruct((8, 128), F32)],
        compiler_params=_params("arbitrary"))(h, target, g.reshape(1, d))


def _glu_fwd(m, wa, wb):
    t, d = m.shape
    c = wa.shape[1]
    tm, tc = min(512, t), min(512, c)

    def kern(m_ref, wa_ref, wb_ref, a_ref, b_ref, glu_ref):
        mv = m_ref[...]
        a = jnp.dot(mv, wa_ref[...], preferred_element_type=F32)
        b = jnp.dot(mv, wb_ref[...], preferred_element_type=F32)
        a_ref[...] = _bf(a)
        b_ref[...] = _bf(b)
        glu_ref[...] = _bf(a * _sigmoid(b))

    col = pl.BlockSpec((tm, tc), lambda i, j: (i, j))
    wspec = pl.BlockSpec((d, tc), lambda i, j: (0, j))
    shape = jax.ShapeDtypeStruct((t, c), BF16)
    return pl.pallas_call(
        kern, name="conv_glu_fwd", grid=(t // tm, c // tc),
        in_specs=[pl.BlockSpec((tm, d), lambda i, j: (i, 0)), wspec, wspec], out_specs=[col, col, col],
        out_shape=[shape, shape, shape], compiler_params=_params("parallel", "parallel"))(m, wa, wb)


def _conv_tile(t):
    return min(256, t)


def _shift_copies(ext, shifted, rows):
    for s in range(8):
        shifted[s] = ext[pl.ds(s, rows), :]


def _shifted_rows(shifted, start, nrows):
    return shifted[start % 8, pl.ds(start - start % 8, nrows), :]


def _conv_fwd(glu, w_dw, g):
    t, c = glu.shape
    tm = _conv_tile(t)
    hb = tm // CONV_HALO

    def kern(cur_ref, halo_ref, w_ref, g_ref, cv_ref, s_ref, ext, shifted):
        i = pl.program_id(0)
        ext[0:CONV_HALO, :] = jnp.where(i > 0, halo_ref[...].astype(F32), 0.0)
        ext[CONV_HALO:tm + CONV_HALO, :] = cur_ref[...].astype(F32)
        ext[tm + CONV_HALO:, :] = jnp.zeros((8, c), F32)
        _shift_copies(ext, shifted, tm + CONV_HALO)
        gv = g_ref[...]
        for r0 in range(0, tm, CONV_ROWS):
            acc = jnp.zeros((CONV_ROWS, c), F32)
            for k in range(CONV_WIDTH):
                acc = acc + _shifted_rows(shifted, r0 + 2 + k, CONV_ROWS) * w_ref[k:k + 1, :]
            cv_ref[r0:r0 + CONV_ROWS, :] = acc
            rn = acc * _rstd(acc) * gv
            s_ref[r0:r0 + CONV_ROWS, :] = _bf(rn * _sigmoid(rn))

    row = pl.BlockSpec((tm, c), lambda i: (i, 0))
    return pl.pallas_call(
        kern, name="conv_fwd", grid=(t // tm,),
        in_specs=[row, pl.BlockSpec((CONV_HALO, c), lambda i: (jnp.maximum(i * hb - 1, 0), 0)),
                  pl.BlockSpec((CONV_HALO, c), lambda i: (0, 0)), pl.BlockSpec((1, c), lambda i: (0, 0))],
        out_specs=[row, row],
        out_shape=[jax.ShapeDtypeStruct((t, c), F32), jax.ShapeDtypeStruct((t, c), BF16)],
        scratch_shapes=[pltpu.VMEM((tm + CONV_HALO + 8, c), F32), pltpu.VMEM((8, tm + CONV_HALO, c), F32)],
        compiler_params=_params("parallel"))(glu, glu, w_dw, g.reshape(1, c))


def _conv_bwd_norm(dh, cv, w_pw2, g, dep):
    t, c = cv.shape
    tm = min(512, t)

    def kern(dh_ref, cv_ref, w_ref, g_ref, dep_ref, dcv_ref, dg_ref):
        @pl.when(pl.program_id(0) == 0)
        def _():
            dg_ref[...] = jnp.zeros_like(dg_ref)

        ds = lax.dot_general(_bf(dh_ref[...]), w_ref[...], (((1,), (1,)), ((), ())), preferred_element_type=F32)
        x = cv_ref[...]
        rstd = _rstd(x)
        xhat = x * rstd
        rn = xhat * g_ref[...]
        sig = _sigmoid(rn)
        drn = ds * (sig * (1.0 + rn * (1.0 - sig)))
        dg_ref[...] += jnp.broadcast_to(jnp.sum(drn * xhat, axis=0, keepdims=True), dg_ref.shape)
        dxh = drn * g_ref[...]
        dcv_ref[...] = rstd * (dxh - xhat * jnp.mean(dxh * xhat, axis=-1, keepdims=True))

    row = pl.BlockSpec((tm, c), lambda i: (i, 0))
    return pl.pallas_call(
        kern, name="conv_bwd_norm", grid=(t // tm,),
        in_specs=[pl.BlockSpec((tm, dh.shape[1]), lambda i: (i, 0)), row,
                  pl.BlockSpec(w_pw2.shape, lambda i: (0, 0)), pl.BlockSpec((1, c), lambda i: (0, 0)),
                  pl.BlockSpec((8, 128), lambda i: (0, 0))],
        out_specs=[row, pl.BlockSpec((8, c), lambda i: (0, 0))],
        out_shape=[jax.ShapeDtypeStruct((t, c), F32), jax.ShapeDtypeStruct((8, c), F32)],
        compiler_params=_params("arbitrary"))(dh, cv, w_pw2, g.reshape(1, c), dep)


def _conv_bwd_dw(dcv, glu, a, b, w_dw):
    t, c = dcv.shape
    tm = _conv_tile(t)
    hb = tm // CONV_HALO
    last = t // CONV_HALO - 1

    def kern(dcv_ref, dnext_ref, glu_ref, gprev_ref, a_ref, b_ref, w_ref, da_ref, db_ref, dw_ref,
             dext, gext, dshift, gshift):
        i = pl.program_id(0)

        @pl.when(i == 0)
        def _():
            dw_ref[...] = jnp.zeros_like(dw_ref)

        dext[0:tm, :] = dcv_ref[...]
        dext[tm:tm + CONV_HALO, :] = jnp.where(i < t // tm - 1, dnext_ref[...], 0.0)
        dext[tm + CONV_HALO:, :] = jnp.zeros((8, c), F32)
        gext[0:CONV_HALO, :] = jnp.where(i > 0, gprev_ref[...].astype(F32), 0.0)
        gext[CONV_HALO:tm + CONV_HALO, :] = glu_ref[...].astype(F32)
        gext[tm + CONV_HALO:, :] = jnp.zeros((8, c), F32)
        _shift_copies(dext, dshift, tm + CONV_HALO)
        _shift_copies(gext, gshift, tm + CONV_HALO)
        for r0 in range(0, tm, CONV_ROWS):
            acc = jnp.zeros((CONV_ROWS, c), F32)
            for k in range(CONV_WIDTH):
                acc = acc + _shifted_rows(dshift, r0 + CONV_WIDTH - 1 - k, CONV_ROWS) * w_ref[k:k + 1, :]
            av = a_ref[r0:r0 + CONV_ROWS, :].astype(F32)
            sig = _sigmoid(b_ref[r0:r0 + CONV_ROWS, :].astype(F32))
            da_ref[r0:r0 + CONV_ROWS, :] = _bf(acc * sig)
            db_ref[r0:r0 + CONV_ROWS, :] = _bf(acc * av * sig * (1.0 - sig))
        for k in range(CONV_WIDTH):
            acc = jnp.zeros((CONV_ROWS, c), F32)
            for r0 in range(0, tm, CONV_ROWS):
                acc = acc + _shifted_rows(gshift, r0 + 2 + k, CONV_ROWS) * dext[r0:r0 + CONV_ROWS, :]
            dw_ref[k:k + 1, :] += jnp.sum(acc, axis=0, keepdims=True)

    row = pl.BlockSpec((tm, c), lambda i: (i, 0))
    shape = jax.ShapeDtypeStruct((t, c), BF16)
    return pl.pallas_call(
        kern, name="conv_bwd_dw", grid=(t // tm,),
        in_specs=[row, pl.BlockSpec((CONV_HALO, c), lambda i: (jnp.minimum((i + 1) * hb, last), 0)),
                  row, pl.BlockSpec((CONV_HALO, c), lambda i: (jnp.maximum(i * hb - 1, 0), 0)),
                  row, row, pl.BlockSpec((CONV_HALO, c), lambda i: (0, 0))],
        out_specs=[row, row, pl.BlockSpec((CONV_HALO, c), lambda i: (0, 0))],
        out_shape=[shape, shape, jax.ShapeDtypeStruct((CONV_HALO, c), F32)],
        scratch_shapes=[pltpu.VMEM((tm + CONV_HALO + 8, c), F32), pltpu.VMEM((tm + CONV_HALO + 8, c), F32),
                        pltpu.VMEM((8, tm + CONV_HALO, c), F32), pltpu.VMEM((8, tm + CONV_HALO, c), F32)],
        compiler_params=_params("arbitrary"))(dcv, dcv, glu, glu, a, b, w_dw)


def _rope_tables(pos):
    t = pos.shape[0]
    tm = min(512, t)
    freq = (np.float32(ROPE_THETA) ** (np.float32(-2.0) * np.arange(ROPE // 2, dtype=np.float32)
                                       / np.float32(ROPE))).astype(np.float32)
    row = np.zeros((2, 128), np.float32)
    row[0, :ROPE] = np.concatenate([freq, freq])
    row[1, :ROPE] = 1.0

    def kern(pos_ref, f_ref, c_ref, s_ref):
        ang = pos_ref[...].astype(F32) * f_ref[0:1, :]
        mask = f_ref[1:2, :]
        c_ref[...] = jnp.cos(ang) * mask
        s_ref[...] = jnp.sin(ang) * mask

    out = pl.BlockSpec((tm, 128), lambda i: (i, 0))
    shape = jax.ShapeDtypeStruct((t, 128), F32)
    return pl.pallas_call(
        kern, name="rope_tables", grid=(t // tm,),
        in_specs=[pl.BlockSpec((tm, 1), lambda i: (i, 0)), pl.BlockSpec((2, 128), lambda i: (0, 0))],
        out_specs=[out, out], out_shape=[shape, shape], compiler_params=_params("parallel"))(pos, jnp.asarray(row))


def _mla_prep(a, gq, gkv, cs_c, cs_s):
    t = a.shape[0]
    tm = min(512, t)
    kv0, r0 = Q_LORA, Q_LORA + KV_LORA

    def kern(a_ref, gq_ref, gkv_ref, c_ref, s_ref, cq_ref, ckv_ref, kr_ref):
        aq = a_ref[:, 0:kv0]
        akv = a_ref[:, kv0:r0]
        ar = a_ref[:, r0:A_PAD]
        cq_ref[...] = _bf(aq * _rstd(aq) * gq_ref[...])
        ckv_ref[...] = _bf(akv * _rstd(akv) * gkv_ref[...])
        kr_ref[...] = _bf(ar * c_ref[...] + _rot(ar) * s_ref[...])

    def row(w):
        return pl.BlockSpec((tm, w), lambda i: (i, 0))

    def vec(w):
        return pl.BlockSpec((1, w), lambda i: (0, 0))

    return pl.pallas_call(
        kern, name="mla_prep", grid=(t // tm,),
        in_specs=[row(A_PAD), vec(Q_LORA), vec(KV_LORA), row(128), row(128)],
        out_specs=[row(Q_LORA), row(KV_LORA), row(128)],
        out_shape=[jax.ShapeDtypeStruct((t, Q_LORA), BF16), jax.ShapeDtypeStruct((t, KV_LORA), BF16),
                   jax.ShapeDtypeStruct((t, 128), BF16)],
        compiler_params=_params("parallel"))(a, gq.reshape(1, -1), gkv.reshape(1, -1), cs_c, cs_s)


def _mla_prep_bwd(a, dcq, dckv, dar, gq, gkv):
    t = a.shape[0]
    tm = min(512, t)
    kv0, r0 = Q_LORA, Q_LORA + KV_LORA

    def kern(a_ref, dcq_ref, dckv_ref, dar_ref, gq_ref, gkv_ref, da_ref, dgq_ref, dgkv_ref):
        @pl.when(pl.program_id(0) == 0)
        def _():
            dgq_ref[...] = jnp.zeros_like(dgq_ref)
            dgkv_ref[...] = jnp.zeros_like(dgkv_ref)

        def back(x, dy, g_ref, dg_ref):
            rstd = _rstd(x)
            xhat = x * rstd
            dg_ref[...] += jnp.broadcast_to(jnp.sum(dy * xhat, axis=0, keepdims=True), dg_ref.shape)
            dxh = dy * g_ref[...]
            return rstd * (dxh - xhat * jnp.mean(dxh * xhat, axis=-1, keepdims=True))

        da_ref[:, 0:kv0] = _bf(back(a_ref[:, 0:kv0], dcq_ref[...], gq_ref, dgq_ref))
        da_ref[:, kv0:r0] = _bf(back(a_ref[:, kv0:r0], dckv_ref[...], gkv_ref, dgkv_ref))
        da_ref[:, r0:A_PAD] = _bf(dar_ref[...])

    def row(w):
        return pl.BlockSpec((tm, w), lambda i: (i, 0))

    def vec(r, w):
        return pl.BlockSpec((r, w), lambda i: (0, 0))

    return pl.pallas_call(
        kern, name="mla_prep_bwd", grid=(t // tm,),
        in_specs=[row(A_PAD), row(Q_LORA), row(KV_LORA), row(128), vec(1, Q_LORA), vec(1, KV_LORA)],
        out_specs=[row(A_PAD), vec(8, Q_LORA), vec(8, KV_LORA)],
        out_shape=[jax.ShapeDtypeStruct((t, A_PAD), BF16), jax.ShapeDtypeStruct((8, Q_LORA), F32),
                   jax.ShapeDtypeStruct((8, KV_LORA), F32)],
        compiler_params=_params("arbitrary"))(a, dcq, dckv, dar, gq.reshape(1, -1), gkv.reshape(1, -1))


def _mla_qkv(cq, ckv, kr, cs_c, cs_s, wuq, wukv):
    t = cq.shape[0]
    tm = min(512, t)
    kvw = NOPE + V_HEAD

    def kern(cq_ref, ckv_ref, kr_ref, c_ref, s_ref, wq_ref, wkv_ref, q_ref, k_ref, v_ref):
        r = jnp.dot(cq_ref[...], wq_ref[...], preferred_element_type=F32)
        kv = jnp.dot(ckv_ref[...], wkv_ref[...], preferred_element_type=F32)
        cv, sv, krv = c_ref[...], s_ref[...], kr_ref[...]
        for h in range(HEADS):
            xr = r[:, h * HEAD_PAD + NOPE:(h + 1) * HEAD_PAD]
            q_ref[h, :, 0:NOPE] = _bf(r[:, h * HEAD_PAD:h * HEAD_PAD + NOPE] * ATTN_SCALE)
            q_ref[h, :, NOPE:] = _bf((xr * cv + _rot(xr) * sv) * ATTN_SCALE)
            k_ref[h, :, 0:NOPE] = _bf(kv[:, h * kvw:h * kvw + NOPE])
            k_ref[h, :, NOPE:] = krv
            v_ref[h] = _bf(kv[:, h * kvw + NOPE:(h + 1) * kvw])

    def row(w):
        return pl.BlockSpec((tm, w), lambda i: (i, 0))

    def heads(w):
        return pl.BlockSpec((HEADS, tm, w), lambda i: (0, i, 0))

    return pl.pallas_call(
        kern, name="mla_qkv", grid=(t // tm,),
        in_specs=[row(Q_LORA), row(KV_LORA), row(128), row(128), row(128),
                  pl.BlockSpec(wuq.shape, lambda i: (0, 0)), pl.BlockSpec(wukv.shape, lambda i: (0, 0))],
        out_specs=[heads(HEAD_PAD), heads(HEAD_PAD), heads(V_HEAD)],
        out_shape=[jax.ShapeDtypeStruct((HEADS, t, HEAD_PAD), BF16), jax.ShapeDtypeStruct((HEADS, t, HEAD_PAD), BF16),
                   jax.ShapeDtypeStruct((HEADS, t, V_HEAD), BF16)],
        compiler_params=_params("parallel"))(cq, ckv, kr, cs_c, cs_s, wuq, wukv)


def _mla_qkv_bwd(dq, dk, dv, cs_c, cs_s, wuq, wukv):
    t = dq.shape[1]
    tm = min(256, t)
    kvw = NOPE + V_HEAD

    def kern(dq_ref, dk_ref, dv_ref, c_ref, s_ref, wq_ref, wkv_ref, dr_ref, dkv_ref, dcq_ref, dckv_ref, dar_ref):
        cv, sv = c_ref[...], s_ref[...]
        dar = jnp.zeros_like(cv)
        for h in range(HEADS):
            dqx = dq_ref[h, :, NOPE:]
            dr_ref[:, h * HEAD_PAD:h * HEAD_PAD + NOPE] = _bf(dq_ref[h, :, 0:NOPE] * ATTN_SCALE)
            dr_ref[:, h * HEAD_PAD + NOPE:(h + 1) * HEAD_PAD] = _bf((dqx * cv + _rot_t(dqx * sv)) * ATTN_SCALE)
            dkx = dk_ref[h, :, NOPE:]
            dar = dar + (dkx * cv + _rot_t(dkx * sv))
            dkv_ref[:, h * kvw:h * kvw + NOPE] = _bf(dk_ref[h, :, 0:NOPE])
            dkv_ref[:, h * kvw + NOPE:(h + 1) * kvw] = _bf(dv_ref[h])
        dar_ref[...] = dar
        dcq_ref[...] = lax.dot_general(dr_ref[...], wq_ref[...], (((1,), (1,)), ((), ())),
                                       preferred_element_type=F32)
        dckv_ref[...] = lax.dot_general(dkv_ref[...], wkv_ref[...], (((1,), (1,)), ((), ())),
                                        preferred_element_type=F32)

    def row(w):
        return pl.BlockSpec((tm, w), lambda i: (i, 0))

    def heads(w):
        return pl.BlockSpec((HEADS, tm, w), lambda i: (0, i, 0))

    return pl.pallas_call(
        kern, name="mla_qkv_bwd", grid=(t // tm,),
        in_specs=[heads(HEAD_PAD), heads(HEAD_PAD), heads(V_HEAD), row(128), row(128),
                  pl.BlockSpec(wuq.shape, lambda i: (0, 0)), pl.BlockSpec(wukv.shape, lambda i: (0, 0))],
        out_specs=[row(HEADS * HEAD_PAD), row(HEADS * kvw), row(Q_LORA), row(KV_LORA), row(128)],
        out_shape=[jax.ShapeDtypeStruct((t, HEADS * HEAD_PAD), BF16), jax.ShapeDtypeStruct((t, HEADS * kvw), BF16),
                   jax.ShapeDtypeStruct((t, Q_LORA), F32), jax.ShapeDtypeStruct((t, KV_LORA), F32),
                   jax.ShapeDtypeStruct((t, 128), F32)],
        compiler_params=_params("parallel"))(dq, dk, dv, cs_c, cs_s, wuq, wukv)


def _attn_block(t):
    return 512 if t >= 4096 else 128


def _chunk_mask(bk, bq):
    kc = lax.broadcasted_iota(jnp.int32, (bk, bq), 0) // CHUNK
    qc = lax.broadcasted_iota(jnp.int32, (bk, bq), 1) // CHUNK
    return qc >= kc


def _flash_fwd(q, k, v):
    t = q.shape[1]
    bq = _attn_block(t)
    nq = t // bq
    nch = 2

    def kern(q_ref, k_ref, v_ref, o_ref, lse_ref, s_buf, p_buf, m_ref, l_ref, acc_ref):
        i = pl.program_id(1)
        queries = [q_ref[c * bq:(c + 1) * bq, :] for c in range(nch)]

        def block(j):
            rows = pl.ds(pl.multiple_of(j * bq, bq), bq)
            return k_ref[rows, :], v_ref[rows, :]

        def scores(kj, chain):
            return lax.dot_general(kj, queries[chain], (((1,), (1,)), ((), ())), preferred_element_type=F32)

        def softmax_block(chain, slot, vj):
            for c0 in range(0, bq, 128):
                cols = slice(c0, c0 + 128)
                s = s_buf[slot, chain, :, cols]
                m_old = m_ref[chain, 0:1, cols]
                m_new = jnp.maximum(m_old, jnp.max(s, axis=0, keepdims=True))
                alpha = jnp.exp(m_old - m_new)
                p = jnp.exp(s - m_new)
                l_ref[chain, 0:1, cols] = alpha * l_ref[chain, 0:1, cols] + jnp.sum(p, axis=0, keepdims=True)
                m_ref[chain, 0:1, cols] = m_new
                p_buf[chain, :, cols] = _bf(p)
                acc_ref[chain, :, cols] = acc_ref[chain, :, cols] * alpha
            acc_ref[chain] += lax.dot_general(vj, p_buf[chain], (((0,), (0,)), ((), ())),
                                              preferred_element_type=F32)

        m_ref[...] = jnp.full(m_ref.shape, -1e30, F32)
        l_ref[...] = jnp.zeros_like(l_ref)
        acc_ref[...] = jnp.zeros_like(acc_ref)
        mask = _chunk_mask(bq, bq)
        for b in range(nch):
            kb, vb = block(nch * i + b)
            for c in range(b, nch):
                s_buf[b % 2, c] = jnp.where(mask, scores(kb, c), -1e30) if c == b else scores(kb, c)
                softmax_block(c, b % 2, vb)
        kf = block(0)[0]
        for c in range(nch):
            s_buf[0, c] = scores(kf, c)

        def body(pair, carry):
            for cur in range(2):
                j = 2 * pair + cur
                kn = block(jnp.minimum(j + 1, jnp.maximum(nch * i - 1, 0)))[0]
                for c in range(nch):
                    s_buf[1 - cur, c] = scores(kn, c)
                vj = block(j)[1]
                for c in range(nch):
                    softmax_block(c, cur, vj)
            return carry

        lax.fori_loop(0, (nch // 2) * i, body, 0)
        for chain in range(nch):
            l = l_ref[chain, 0:1, :]
            o_ref[chain * bq:(chain + 1) * bq, :] = _bf((acc_ref[chain] / l).T)
            lse_ref[chain] = jnp.broadcast_to(m_ref[chain, 0:1, :] + jnp.log(l), (8, bq))

    return pl.pallas_call(
        kern, name="flash_fwd", grid=(HEADS, nq // nch),
        in_specs=[pl.BlockSpec((None, nch * bq, HEAD_PAD), lambda h, i: (h, i, 0)),
                  pl.BlockSpec((None, t, HEAD_PAD), lambda h, i: (h, 0, 0)),
                  pl.BlockSpec((None, t, V_HEAD), lambda h, i: (h, 0, 0))],
        out_specs=[pl.BlockSpec((nch * bq, V_HEAD), lambda h, i: (i, h)),
                   pl.BlockSpec((None, nch, 8, bq), lambda h, i: (h, i, 0, 0))],
        out_shape=[jax.ShapeDtypeStruct((t, HEADS * V_HEAD), BF16), jax.ShapeDtypeStruct((HEADS, nq, 8, bq), F32)],
        scratch_shapes=[pltpu.VMEM((2, nch, bq, bq), F32), pltpu.VMEM((nch, bq, bq), BF16),
                        pltpu.VMEM((nch, 8, bq), F32), pltpu.VMEM((nch, 8, bq), F32),
                        pltpu.VMEM((nch, V_HEAD, bq), F32)],
        compiler_params=_params("parallel", "arbitrary"))(q, k, v)


def _attn_delta(do, o):
    t = do.shape[0]
    bq = _attn_block(t)

    def kern(do_ref, o_ref, d_ref):
        for h in range(HEADS):
            cols = slice(h * V_HEAD, (h + 1) * V_HEAD)
            prod = do_ref[:, cols].astype(F32) * o_ref[:, cols].astype(F32)
            d_ref[h] = jnp.broadcast_to(jnp.sum(prod.T, axis=0, keepdims=True), (8, bq))

    blk = pl.BlockSpec((bq, HEADS * V_HEAD), lambda i: (i, 0))
    return pl.pallas_call(
        kern, name="attn_delta", grid=(t // bq,), in_specs=[blk, blk],
        out_specs=pl.BlockSpec((HEADS, None, 8, bq), lambda i: (0, i, 0, 0)),
        out_shape=jax.ShapeDtypeStruct((HEADS, t // bq, 8, bq), F32),
        compiler_params=_params("parallel"))(do, o)


def _flash_bwd(q, k, v, do, lse, delta):
    t = q.shape[1]
    bq = _attn_block(t)
    nq = t // bq

    def kern(q_ref, k_ref, v_ref, do_ref, lse_ref, del_ref, dq_ref, dk_ref, dv_ref, dvt_ref):
        j = pl.program_id(1)

        @pl.when(j == 0)
        def _():
            dq_ref[...] = jnp.zeros_like(dq_ref)

        dk_ref[...] = jnp.zeros_like(dk_ref)
        dvt_ref[...] = jnp.zeros_like(dvt_ref)
        kj, vj = k_ref[...], v_ref[...]

        def step(i, masked):
            rows = pl.ds(pl.multiple_of(i * bq, bq), bq)
            qi, doi = q_ref[rows, :], do_ref[rows, :]
            st = lax.dot_general(kj, qi, (((1,), (1,)), ((), ())), preferred_element_type=F32)
            pt = jnp.exp(st - lse_ref[i][0:1, :])
            if masked:
                pt = jnp.where(_chunk_mask(bq, bq), pt, 0.0)
            dpt = lax.dot_general(vj, doi, (((1,), (1,)), ((), ())), preferred_element_type=F32)
            dst = _bf(pt * (dpt - del_ref[i][0:1, :]))
            dvt_ref[...] += lax.dot_general(doi, _bf(pt), (((0,), (1,)), ((), ())), preferred_element_type=F32)
            dk_ref[...] += jnp.dot(dst, qi, preferred_element_type=F32)
            dq_ref[rows, :] += lax.dot_general(dst, kj, (((0,), (0,)), ((), ())), preferred_element_type=F32)

        step(j, True)

        def body(pair, carry):
            step(j + 1 + 2 * pair, False)
            step(j + 2 + 2 * pair, False)
            return carry

        rest = nq - 1 - j
        lax.fori_loop(0, rest // 2, body, 0)

        @pl.when(rest % 2 == 1)
        def _():
            step(nq - 1, False)

        dv_ref[...] = dvt_ref[...].T

    stat = pl.BlockSpec((None, nq, 8, bq), lambda h, j: (h, 0, 0, 0))
    return pl.pallas_call(
        kern, name="flash_bwd", grid=(HEADS, nq),
        in_specs=[pl.BlockSpec((None, t, HEAD_PAD), lambda h, j: (h, 0, 0)),
                  pl.BlockSpec((None, bq, HEAD_PAD), lambda h, j: (h, j, 0)),
                  pl.BlockSpec((None, bq, V_HEAD), lambda h, j: (h, j, 0)),
                  pl.BlockSpec((t, V_HEAD), lambda h, j: (0, h)), stat, stat],
        out_specs=[pl.BlockSpec((None, t, HEAD_PAD), lambda h, j: (h, 0, 0)),
                   pl.BlockSpec((None, bq, HEAD_PAD), lambda h, j: (h, j, 0)),
                   pl.BlockSpec((None, bq, V_HEAD), lambda h, j: (h, j, 0))],
        out_shape=[jax.ShapeDtypeStruct((HEADS, t, HEAD_PAD), F32), jax.ShapeDtypeStruct((HEADS, t, HEAD_PAD), F32),
                   jax.ShapeDtypeStruct((HEADS, t, V_HEAD), F32)],
        scratch_shapes=[pltpu.VMEM((V_HEAD, bq), F32)],
        compiler_params=_params("parallel", "arbitrary"))(q, k, v, do, lse, delta)


def _place():
    x, y, c = lax.axis_index("x"), lax.axis_index("y"), lax.axis_index("c")
    return x, y, c, [(1 - x, y), (x, 1 - y), (1 - x, 1 - y)]


def _all_gather_rows(block, name):
    m_per, n = block.shape

    def body(x_ref, out_ref, send_sems, recv_sems, local_sem):
        x, y, c, chips = _place()
        me, sibling = (x, y, c), (x, y, 1 - c)

        def rows(px, py, pc):
            return out_ref.at[pl.ds((4 * px + 2 * py + pc) * m_per, m_per), :]

        def copy(k, blk, to, src=None):
            return pltpu.make_async_remote_copy(
                src_ref=rows(*blk) if src is None else src, dst_ref=rows(*blk), send_sem=send_sems.at[k],
                recv_sem=recv_sems.at[k], device_id=to, device_id_type=MESH)

        mine = pltpu.make_async_copy(x_ref, rows(*me), local_sem)
        mine.start()
        first = [copy(0, me, sibling, src=x_ref)]
        first += [copy(1 + j, me, (*chip, c), src=x_ref) for j, chip in enumerate(chips)]
        for cp in first:
            cp.start()
        passed = [copy(4 + j, (*chip, c), sibling) for j, chip in enumerate(chips)]
        for j, chip in enumerate(chips):
            copy(1 + j, (*chip, c), me).wait_recv()
            passed[j].start()
        copy(0, sibling, me).wait_recv()
        for j, chip in enumerate(chips):
            copy(4 + j, (*chip, 1 - c), me).wait_recv()
        for cp in first + passed:
            cp.wait_send()
        mine.wait()

    return pl.pallas_call(
        body, name=name, out_shape=jax.ShapeDtypeStruct((8 * m_per, n), block.dtype),
        in_specs=[pl.BlockSpec(memory_space=pltpu.VMEM)], out_specs=pl.BlockSpec(memory_space=pltpu.VMEM),
        scratch_shapes=[pltpu.SemaphoreType.DMA((7,)), pltpu.SemaphoreType.DMA((7,)), pltpu.SemaphoreType.DMA],
        compiler_params=pltpu.CompilerParams(vmem_limit_bytes=VMEM_LIMIT_BYTES))(block)


HBM_SPEC = pl.BlockSpec(memory_space=pltpu.HBM)
SEM_SPEC = pl.BlockSpec(memory_space=pltpu.SEMAPHORE)
DATAFLOW = pltpu.SideEffectType.DATAFLOW_SIDE_EFFECTING


def _in_hbm(a):
    return pltpu.with_memory_space_constraint(a, pltpu.HBM)


def _chip_copies(ins, lands, send_sems, recv_sems, src_slot, half=False):
    n = len(ins)
    x, y, c, chips = _place()
    me = 2 * x + y

    def ends(w, chip):
        src = ins[w].at[2 * chip[0] + chip[1]] if src_slot else ins[w]
        if not half:
            return src, lands[w].at[me]
        rows = pl.ds(pl.multiple_of(c * (src.shape[0] // 2), 16), src.shape[0] // 2)
        return src.at[rows], lands[w].at[me, rows]

    copies = []
    for w in range(n):
        for p, chip in enumerate(chips):
            src, dst = ends(w, chip)
            copies.append(pltpu.make_async_remote_copy(
                src_ref=src, dst_ref=dst, send_sem=send_sems.at[p * n + w], recv_sem=recv_sems.at[p * n + w],
                device_id=(*chip, c), device_id_type=MESH))
    return copies


def _fill_halves(lands, name):
    n = len(lands)

    def body(*refs):
        bufs = refs[n:2 * n]
        send_sems, recv_sems = refs[2 * n:]
        x, y, c, chips = _place()
        copies = []
        for w in range(n):
            hr = bufs[w].shape[1] // 2
            for p, chip in enumerate(chips):
                part = bufs[w].at[2 * chip[0] + chip[1], pl.ds(pl.multiple_of(c * hr, 16), hr)]
                copies.append(pltpu.make_async_remote_copy(
                    src_ref=part, dst_ref=part, send_sem=send_sems.at[p * n + w], recv_sem=recv_sems.at[p * n + w],
                    device_id=(x, y, 1 - c), device_id_type=MESH))
        for cp in copies:
            cp.start()
        for cp in copies:
            cp.wait_send()
        for w in range(n):
            hr = bufs[w].shape[1] // 2
            for p, chip in enumerate(chips):
                part = bufs[w].at[2 * chip[0] + chip[1], pl.ds(pl.multiple_of((1 - c) * hr, 16), hr)]
                pltpu.make_async_remote_copy(
                    src_ref=part, dst_ref=part, send_sem=send_sems.at[p * n + w], recv_sem=recv_sems.at[p * n + w],
                    device_id=(x, y, 1 - c), device_id_type=MESH).wait_recv()

    any_spec = pl.BlockSpec(memory_space=pl.ANY)
    return list(pl.pallas_call(
        body, name=name, out_shape=[jax.ShapeDtypeStruct(a.shape, a.dtype) for a in lands],
        in_specs=[any_spec] * n, out_specs=[any_spec] * n, input_output_aliases={i: i for i in range(n)},
        scratch_shapes=[pltpu.SemaphoreType.DMA((3 * n,)), pltpu.SemaphoreType.DMA((3 * n,))])(*lands))


def _exchange_start(srcs, lands, src_slot, name, dep=None, half=False):
    n = len(srcs)
    first_out = 2 * n + (dep is not None)

    def body(*refs):
        for cp in _chip_copies(refs[:n], refs[n:2 * n], refs[first_out], refs[first_out + 1], src_slot, half):
            cp.start()
        token = refs[-1]
        token[...] = jnp.zeros_like(token)

    thru = [pltpu.HBM(a.shape, a.dtype) for a in list(srcs) + list(lands)]
    res = pl.pallas_call(
        body, name=name,
        out_shape=(pltpu.SemaphoreType.DMA((3 * n,)), pltpu.SemaphoreType.DMA((3 * n,)), *thru,
                   jax.ShapeDtypeStruct((8, 128), F32)),
        in_specs=[HBM_SPEC] * (2 * n) + ([pl.BlockSpec(memory_space=pl.ANY)] if dep is not None else []),
        out_specs=(SEM_SPEC, SEM_SPEC, *[HBM_SPEC] * (2 * n), pl.BlockSpec(memory_space=pltpu.VMEM)),
        input_output_aliases={i: 2 + i for i in range(2 * n)},
        compiler_params=pltpu.CompilerParams(has_side_effects=DATAFLOW))(
            *[_in_hbm(a) for a in srcs], *[_in_hbm(a) for a in lands], *([dep] if dep is not None else []))
    return (res[0], res[1], list(res[2:2 + n]), list(res[2 + n:2 + 2 * n])), res[-1]


def _exchange_wait(flight, after, src_slot, name, half=False):
    send_sems, recv_sems, srcs, lands = flight
    n = len(srcs)

    def body(*refs):
        for cp in _chip_copies(refs[:n], refs[n:2 * n], refs[2 * n], refs[2 * n + 1], src_slot, half):
            cp.wait_send()
            cp.wait_recv()

    thru = [pltpu.HBM(a.shape, a.dtype) for a in list(srcs) + list(lands)]
    res = pl.pallas_call(
        body, name=name, out_shape=thru,
        in_specs=[HBM_SPEC] * (2 * n) + [SEM_SPEC, SEM_SPEC, pl.BlockSpec(memory_space=pl.ANY)],
        out_specs=[HBM_SPEC] * (2 * n), input_output_aliases={i: i for i in range(2 * n)},
        compiler_params=pltpu.CompilerParams(has_side_effects=DATAFLOW))(*srcs, *lands, send_sems, recv_sems, after)
    return list(res[n:])


def _landing(own, me):
    return lax.dynamic_update_index_in_dim(lax.empty((4, *own.shape), own.dtype), own, me, 0)


def _swap_with_sibling(arrays, name):
    n = len(arrays)

    def body(*refs):
        ins, outs = refs[:n], refs[n:2 * n]
        send_sems, recv_sems = refs[2 * n:]
        x, y, c, _ = _place()
        copies = [pltpu.make_async_remote_copy(src_ref=ins[w], dst_ref=outs[w], send_sem=send_sems.at[w],
                                               recv_sem=recv_sems.at[w], device_id=(x, y, 1 - c), device_id_type=MESH)
                  for w in range(n)]
        for cp in copies:
            cp.start()
        for cp in copies:
            cp.wait()

    any_spec = pl.BlockSpec(memory_space=pl.ANY)
    return pl.pallas_call(
        body, name=name, out_shape=[jax.ShapeDtypeStruct(a.shape, a.dtype) for a in arrays],
        in_specs=[any_spec] * n, out_specs=[any_spec] * n,
        scratch_shapes=[pltpu.SemaphoreType.DMA((n,)), pltpu.SemaphoreType.DMA((n,))])(*arrays)


def _as_rows(a):
    return a.reshape(-1, a.shape[-1])


def _row_tile(r, c, budget_bytes=1 << 20):
    tr = r
    while tr % 16 == 0 and tr * c * 4 > budget_bytes:
        tr //= 2
    return tr


def _sum_slots(layers, nlayer, name, into=None):
    _, r, c = layers[0][1].shape
    tr = _row_tile(r, c)
    nt = r // tr
    acc = into
    for l, r4 in layers:
        def kern(r_ref, *rest):
            o_ref = rest[-1]
            o_ref[...] = (((r_ref[0].astype(F32) + r_ref[1].astype(F32)) + r_ref[2].astype(F32))
                          + r_ref[3].astype(F32))

        out_spec = pl.BlockSpec((tr, c), lambda i, l=l: (l * nt + i, 0))
        first = acc is None
        acc = pl.pallas_call(
            kern, name=f"{name}_l{l}", grid=(nt,),
            in_specs=[pl.BlockSpec((4, tr, c), lambda i: (0, i, 0))]
            + ([] if first else [pl.BlockSpec(memory_space=pl.ANY)]),
            out_specs=out_spec, out_shape=jax.ShapeDtypeStruct((nlayer * r, c), F32),
            input_output_aliases={} if first else {1: 0},
            compiler_params=_params("parallel"))(*([r4] if first else [r4, acc]))
    return acc


def _adamw(w, m, v, parts, name):
    r, c = w.shape
    tr = _row_tile(r, c, 3 << 19)
    npart = len(parts)
    c1 = 1.0 - ADAM_B1 ** ADAM_STEP
    c2 = 1.0 - ADAM_B2 ** ADAM_STEP

    def kern(*refs):
        w_ref, m_ref, v_ref = refs[:3]
        p_refs = refs[3:3 + npart]
        g_ref, d_ref, mo_ref, vo_ref = refs[3 + npart:]
        g = p_refs[0][...]
        for p in p_refs[1:]:
            g = g + p[...]
        mn = ADAM_B1 * m_ref[...] + (1.0 - ADAM_B1) * g
        vn = ADAM_B2 * v_ref[...] + (1.0 - ADAM_B2) * (g * g)
        g_ref[...] = g
        mo_ref[...] = mn
        vo_ref[...] = vn
        d_ref[...] = -ADAM_LR * ((mn / c1) / (jnp.sqrt(vn / c2) + ADAM_EPS) + ADAM_WD * w_ref[...])

    blk = pl.BlockSpec((tr, c), lambda i: (i, 0))
    shape = jax.ShapeDtypeStruct((r, c), F32)
    return pl.pallas_call(
        kern, name=name, grid=(r // tr,), in_specs=[blk] * (3 + npart), out_specs=[blk] * 4, out_shape=[shape] * 4,
        compiler_params=_params("parallel"))(w, m, v, *parts)


def _sum_devices(g8, name):
    _, r, c = g8.shape

    def kern(g_ref, o_ref):
        tot = g_ref[0]
        for dev in range(1, 8):
            tot = tot + g_ref[dev]
        o_ref[...] = tot

    return pl.pallas_call(
        kern, name=name, grid=(1,), in_specs=[pl.BlockSpec((8, r, c), lambda i: (0, 0, 0))],
        out_specs=pl.BlockSpec((r, c), lambda i: (0, 0)), out_shape=jax.ShapeDtypeStruct((r, c), F32),
        compiler_params=_params("arbitrary"))(g8)


def _pad_lanes(a, width):
    return jnp.pad(a, [(0, 0)] * (a.ndim - 1) + [(0, width - a.shape[-1])])


def kernel(x, positions, ffn_norm1, ffn1_w1, ffn1_w3, ffn1_w2, mix_norm, ffn_norm2, ffn2_w1, ffn2_w3, ffn2_w2, conv_w_pw1, conv_w_dw, conv_norm, conv_w_pw2, mla_w_a, mla_q_norm, mla_kv_norm, mla_w_uq, mla_w_ukv, mla_w_o, final_norm, loss_target, m_ffn_norm1, m_ffn1_w1, m_ffn1_w3, m_ffn1_w2, m_mix_norm, m_ffn_norm2, m_ffn2_w1, m_ffn2_w3, m_ffn2_w2, m_conv_w_pw1, m_conv_w_dw, m_conv_norm, m_conv_w_pw2, m_mla_w_a, m_mla_q_norm, m_mla_kv_norm, m_mla_w_uq, m_mla_w_ukv, m_mla_w_o, m_final_norm, v_ffn_norm1, v_ffn1_w1, v_ffn1_w3, v_ffn1_w2, v_mix_norm, v_ffn_norm2, v_ffn2_w1, v_ffn2_w3, v_ffn2_w2, v_conv_w_pw1, v_conv_w_dw, v_conv_norm, v_conv_w_pw2, v_mla_w_a, v_mla_q_norm, v_mla_kv_norm, v_mla_w_uq, v_mla_w_ukv, v_mla_w_o, v_final_norm):
    given = locals()
    return _step({nm: given[nm] for nm in INPUTS})


def _step(A):
    x = A['x'][0]
    target = A['loss_target'][0]
    t, d = x.shape
    pos = A['positions'].reshape(t, 1)
    me = 2 * lax.axis_index("x") + lax.axis_index("y")

    flipped = {f'ffn{k}_{w}' for k in (1, 2) for w in ('w1', 'w3')}
    P = {}
    for nm in BIG:
        for key in (nm, 'm_' + nm, 'v_' + nm):
            P[key] = jnp.swapaxes(A[key], 1, 2) if nm in flipped else A[key]

    def unflip(nm, a):
        return jnp.swapaxes(a, 1, 2) if nm in flipped else a

    ffn = [f'ffn{k}_{w}' for k in (1, 2) for w in ('w1', 'w3', 'w2')]
    gather_groups = [[(nm, 0) for nm in ffn[:3]],
                     [('conv_w_pw1', 0), ('conv_w_pw2', 0)] + [(nm, 0) for nm in ffn[3:]],
                     [(nm, 1) for nm in ffn[:3]] + [('mla_w_a', 0), ('mla_w_uq', 0), ('mla_w_ukv', 0), ('mla_w_o', 0)],
                     [(nm, 1) for nm in ffn[3:]]]
    halved = (0, 1)
    gather_flights = {}
    big = {}

    def gather_start(gi, dep):
        shards = [_bf(P[nm][l]) for nm, l in gather_groups[gi]]
        gather_flights[gi], token = _exchange_start(shards, [_landing(s, me) for s in shards], False,
                                                    f"gather_start_{gi}", dep, half=gi in halved)
        return token

    def gather_wait(gi, after):
        landed = _exchange_wait(gather_flights[gi], after, False, f"gather_wait_{gi}", half=gi in halved)
        if gi in halved:
            landed = _fill_halves(landed, f"gather_fill_{gi}")
        big.update(zip(gather_groups[gi], landed))
        return landed[0]

    dw_shard = A['conv_w_dw'][0]
    cw = dw_shard.shape[1]
    small = jnp.concatenate([
        jnp.pad(dw_shard, ((0, CONV_HALO - CONV_WIDTH), (0, 0))),
        jnp.pad(_pad_lanes(A['mla_q_norm'], cw), ((0, 7), (0, 0))),
        jnp.pad(_pad_lanes(A['mla_kv_norm'], cw), ((0, 7), (0, 0)))], axis=0)
    small = _all_gather_rows(small, "gather_small_weights").reshape(4, 2, 48, cw)[:, 0]
    w_dw = jnp.concatenate([small[j, :CONV_HALO] for j in range(4)], axis=1)
    gq = jnp.concatenate([small[j, CONV_HALO, :Q_LORA // 4] for j in range(4)])
    gkv = jnp.concatenate([small[j, CONV_HALO + 8, :KV_LORA // 4] for j in range(4)])

    def rows(nm, layer):
        g = big[nm, layer]
        return g.reshape(-1, g.shape[-1])

    ffn_w = {}

    def ffn_weights(k, l):
        ffn_w[k, l] = (rows(f'ffn{k}_w1', l), rows(f'ffn{k}_w3', l), rows(f'ffn{k}_w2', l))
        return ffn_w[k, l]

    token = gather_start(0, small)
    cs_c, cs_s = _rope_tables(pos)
    h0 = x
    token = gather_start(1, gather_wait(0, token))
    h1, n01, z01a, z01b = _ffn_fwd(h0, A['ffn_norm1'][0], *ffn_weights(1, 0), token, "ffn1_l0_fwd")
    token = gather_start(3, gather_start(2, gather_wait(1, h1)))
    pw1 = big['conv_w_pw1', 0]
    pw1_a = jnp.concatenate([pw1[0], pw1[1]], axis=1)
    pw1_b = jnp.concatenate([pw1[2], pw1[3]], axis=1)
    pw2 = rows('conv_w_pw2', 0)
    m0 = _norm_fwd(h1, A['mix_norm'][0], token, "mix_norm_l0")
    ca, cb, glu = _glu_fwd(m0, pw1_a, pw1_b)
    cv, cs = _conv_fwd(glu, w_dw, A['conv_norm'][0])
    h2 = _mm([(cs, pw2)], F32, "conv_pw2_fwd", res=h1)
    h3, n02, z02a, z02b = _ffn_fwd(h2, A['ffn_norm2'][0], *ffn_weights(2, 0), token, "ffn2_l0_fwd")
    gather_wait(2, h3)
    w_a = _pad_lanes(rows('mla_w_a', 0), A_PAD)
    wuq = _pad_lanes(big['mla_w_uq', 0].reshape(Q_LORA, HEADS, NOPE + ROPE), HEAD_PAD).reshape(Q_LORA, -1)
    wukv = big['mla_w_ukv', 0].reshape(KV_LORA, HEADS * (NOPE + V_HEAD))
    w_o = rows('mla_w_o', 0)
    h4, n11, z11a, z11b = _ffn_fwd(h3, A['ffn_norm1'][1], *ffn_weights(1, 1), token, "ffn1_l1_fwd")
    m1 = _norm_fwd(h4, A['mix_norm'][1], token, "mix_norm_l1")
    a_lat = _mm([(m1, w_a)], F32, "mla_down_fwd")
    cq, ckv, kr = _mla_prep(a_lat, gq, gkv, cs_c, cs_s)
    q, k, v = _mla_qkv(cq, ckv, kr, cs_c, cs_s, wuq, wukv)
    o, lse = _flash_fwd(q, k, v)
    h5 = _mm([(o, w_o)], F32, "mla_out_fwd", res=h4)
    gather_wait(3, h5)
    h6, n12, z12a, z12b = _ffn_fwd(h5, A['ffn_norm2'][1], *ffn_weights(2, 1), token, "ffn2_l1_fwd")

    def row_slots(g):
        return g.reshape(4, g.shape[0] // 4, g.shape[1])

    scatter_flights = []

    def scatter_start(named):
        srcs = [g for _, g in named]
        lands = [_landing(lax.dynamic_index_in_dim(g, me, 0, keepdims=False), me) for g in srcs]
        flight, token = _exchange_start(srcs, lands, True, f"scatter_start_{len(scatter_flights)}")
        scatter_flights.append(([key for key, _ in named], flight))
        return token

    def send_ffn(k, l, dw1t, dw3t, dw2):
        return scatter_start([((f'ffn{k}_w1', l), row_slots(dw1t)), ((f'ffn{k}_w3', l), row_slots(dw3t)),
                              ((f'ffn{k}_w2', l), row_slots(dw2))])

    dh6, dg_final, loss_part = _loss_bwd(h6, target, A['final_norm'])
    dh5, dg_n2_l1, *dws = _ffn_bwd(dh6, h5, A['ffn_norm2'][1], n12, z12a, z12b, *ffn_w[2, 1], loss_part, "ffn2_l1")
    token = send_ffn(2, 1, *dws)

    do = _mm([(dh5, w_o)], BF16, "mla_out_bwd", trans_b=True, dep=token)
    dw_o = _mm_tn(o, dh5, BF16, "mla_dw_o")
    delta = _attn_delta(do, o)
    dq, dk, dv = _flash_bwd(q, k, v, do, lse, delta)
    dr, dkv, dcq, dckv, dar = _mla_qkv_bwd(dq, dk, dv, cs_c, cs_s, wuq, wukv)
    dwuq = _mm_tn(cq, dr, BF16, "mla_dw_uq", bn=dr.shape[1] // 2)
    dwukv = _mm_tn(ckv, dkv, BF16, "mla_dw_ukv", bn=dkv.shape[1] // 2)
    da_lat, dgq, dgkv = _mla_prep_bwd(a_lat, dcq, dckv, dar, gq, gkv)
    dw_a = _mm_tn(m1, da_lat, BF16, "mla_dw_a")
    token = scatter_start([
        (('mla_w_a', 0), row_slots(dw_a[:, :Q_LORA + KV_LORA + ROPE])),
        (('mla_w_uq', 0), dwuq.reshape(4, Q_LORA // 4, HEADS, HEAD_PAD)[..., :NOPE + ROPE]),
        (('mla_w_ukv', 0), dwukv.reshape(4, KV_LORA // 4, HEADS, NOPE + V_HEAD)),
        (('mla_w_o', 0), row_slots(dw_o))])
    dh4, dg_mix_l1 = _mm_normbwd([(da_lat, w_a)], h4, A['mix_norm'][1], dh5, token, "mla_down_bwd")

    dh3, dg_n1_l1, *dws = _ffn_bwd(dh4, h3, A['ffn_norm1'][1], n11, z11a, z11b, *ffn_w[1, 1], token, "ffn1_l1")
    token = send_ffn(1, 1, *dws)
    dh2, dg_n2_l0, *dws = _ffn_bwd(dh3, h2, A['ffn_norm2'][0], n02, z02a, z02b, *ffn_w[2, 0], token, "ffn2_l0")
    token = send_ffn(2, 0, *dws)

    dcv, dg_conv = _conv_bwd_norm(dh2, cv, pw2, A['conv_norm'][0], token)
    dw_pw2 = _mm_tn(cs, dh2, BF16, "conv_dw_pw2")
    dca, dcb, ddw = _conv_bwd_dw(dcv, glu, ca, cb, w_dw)
    dpw1_a = _mm_tn(m0, dca, BF16, "conv_dw_pw1a")
    dpw1_b = _mm_tn(m0, dcb, BF16, "conv_dw_pw1b")
    half = dpw1_a.shape[1] // 2
    token = scatter_start([
        (('conv_w_pw1', 0), jnp.stack([dpw1_a[:, :half], dpw1_a[:, half:], dpw1_b[:, :half], dpw1_b[:, half:]])),
        (('conv_w_pw2', 0), row_slots(dw_pw2))])
    dh1, dg_mix_l0 = _mm_normbwd([(dca, pw1_a), (dcb, pw1_b)], h1, A['mix_norm'][0], dh2, token, "conv_pw1_bwd")

    dx, dg_n1_l0, *dws = _ffn_bwd(dh1, h0, A['ffn_norm1'][0], n01, z01a, z01b, *ffn_w[1, 0], token, "ffn1_l0")
    last_sent = send_ffn(1, 0, *dws)
    out = {}

    qkv_row = jnp.concatenate([dgq, dgkv, jnp.zeros((8, d - Q_LORA - KV_LORA), F32)], axis=1)
    loss_row = _pad_lanes(loss_part, d)
    small_g = jnp.concatenate([dg_n1_l0, dg_n1_l1, dg_mix_l0, dg_mix_l1, dg_n2_l0, dg_n2_l1, dg_conv, dg_final,
                               qkv_row, loss_row, ddw], axis=0)
    nrow = small_g.shape[0]
    tot = _sum_devices(_all_gather_rows(small_g, "gather_small_grads").reshape(8, nrow, d), "sum_small_grads")
    loss = tot[72, 0]
    q_shard = lax.dynamic_slice_in_dim(tot[64, :Q_LORA], me * (Q_LORA // 4), Q_LORA // 4)
    kv_shard = lax.dynamic_slice_in_dim(tot[64, Q_LORA:Q_LORA + KV_LORA], me * (KV_LORA // 4), KV_LORA // 4)
    dw_shard_g = lax.dynamic_slice_in_dim(tot[80:80 + CONV_WIDTH], me * cw, cw, axis=1)
    small_grads = {
        'ffn_norm1': jnp.stack([tot[0], tot[8]]), 'mix_norm': jnp.stack([tot[16], tot[24]]),
        'ffn_norm2': jnp.stack([tot[32], tot[40]]), 'conv_norm': tot[48][None], 'final_norm': tot[56],
        'mla_q_norm': q_shard[None], 'mla_kv_norm': kv_shard[None], 'conv_w_dw': dw_shard_g[None],
    }
    for nm, g in small_grads.items():
        res = _adamw(_as_rows(A[nm]) if A[nm].ndim > 1 else A[nm].reshape(1, -1),
                     A['m_' + nm].reshape(-1, A[nm].shape[-1]), A['v_' + nm].reshape(-1, A[nm].shape[-1]),
                     [g.reshape(-1, A[nm].shape[-1])], "adamw_" + nm)
        out[nm] = [r.reshape(A[nm].shape) for r in res]

    received = {}
    after = last_sent

    def scatter_wait(si, after):
        keys, flight = scatter_flights[si]
        landed = _exchange_wait(flight, after, True, f"scatter_wait_{si}")
        received.update(zip(keys, landed))
        return landed[0]

    def slots(nm, l):
        return received[nm, l].reshape(4, -1, received[nm, l].shape[-1])

    def finish(names, sums, tag):
        for nm, mine, theirs in zip(names, sums, _swap_with_sibling(sums, "swap_with_sibling_" + tag)):
            res = _adamw(_as_rows(P[nm]), _as_rows(P['m_' + nm]), _as_rows(P['v_' + nm]), [mine, theirs],
                         "adamw_" + nm)
            out[nm] = [unflip(nm, r.reshape(P[nm].shape)) for r in res]
        return res[1]

    last = len(scatter_flights) - 1
    for si in range(last):
        after = scatter_wait(si, after)
    late = ffn[:3]
    early = [nm for nm in BIG if nm not in late]
    late_l1 = [_sum_slots([(1, slots(nm, 1))], 2, "sum_" + nm) for nm in late]
    after = finish(early, [_sum_slots([(l, slots(nm, l)) for l in range(A[nm].shape[0])], A[nm].shape[0],
                                      "sum_" + nm) for nm in early], "early")
    scatter_wait(last, after)
    finish(late, [_sum_slots([(0, slots(nm, 0))], 2, "sum_" + nm, into=part) for nm, part in zip(late, late_l1)],
           "late")

    return (loss, dx[None], *[out[nm][0] for nm in WEIGHTS], *[out[nm][1] for nm in WEIGHTS],
            *[out[nm][2] for nm in WEIGHTS], *[out[nm][3] for nm in WEIGHTS])
```

```python
import functools

import jax
import jax.numpy as jnp
import numpy as np
from jax import lax
from jax.experimental import pallas as pl
from jax.experimental.pallas import tpu as pltpu

F32 = jnp.float32
BF16 = jnp.bfloat16
MESH = pl.DeviceIdType.MESH

RMS_EPS = 1e-6
HEADS = 8
NOPE = 128
ROPE = 64
HEAD_PAD = 256
V_HEAD = 128
Q_LORA = 512
KV_LORA = 256
A_PAD = 896
CHUNK = 64
CONV_WIDTH = 31
CONV_HALO = 32
CONV_ROWS = 16
ROPE_THETA = 10000.0
ATTN_SCALE = (NOPE + ROPE) ** -0.5
FFN_RES = 0.5

ADAM_LR = 0.001
ADAM_B1 = 0.9
ADAM_B2 = 0.999
ADAM_EPS = 1e-08
ADAM_WD = 0.01
ADAM_STEP = 10

VMEM_LIMIT_BYTES = 56 * 1024 * 1024

WEIGHTS = ['ffn_norm1', 'ffn1_w1', 'ffn1_w3', 'ffn1_w2', 'mix_norm', 'ffn_norm2', 'ffn2_w1', 'ffn2_w3', 'ffn2_w2',
           'conv_w_pw1', 'conv_w_dw', 'conv_norm', 'conv_w_pw2', 'mla_w_a', 'mla_q_norm', 'mla_kv_norm', 'mla_w_uq',
           'mla_w_ukv', 'mla_w_o', 'final_norm']
INPUTS = (['x', 'positions'] + WEIGHTS + ['loss_target'] + ['m_' + w for w in WEIGHTS] + ['v_' + w for w in WEIGHTS])
BIG = ['ffn1_w1', 'ffn1_w3', 'ffn1_w2', 'ffn2_w1', 'ffn2_w3', 'ffn2_w2', 'conv_w_pw1', 'conv_w_pw2', 'mla_w_a',
       'mla_w_uq', 'mla_w_ukv', 'mla_w_o']


def _params(*sem):
    return pltpu.CompilerParams(dimension_semantics=sem, vmem_limit_bytes=VMEM_LIMIT_BYTES)


def _bf(v):
    return v.astype(BF16)


def _rstd(x):
    return lax.rsqrt(jnp.mean(x * x, axis=-1, keepdims=True) + RMS_EPS)


def _sigmoid(x):
    return jax.nn.sigmoid(x)


def _rot(x):
    lane = lax.broadcasted_iota(jnp.int32, x.shape, 1)
    return jnp.where(lane < ROPE // 2, -pltpu.roll(x, 128 - ROPE // 2, 1), pltpu.roll(x, ROPE // 2, 1))


def _rot_t(y):
    lane = lax.broadcasted_iota(jnp.int32, y.shape, 1)
    return jnp.where(lane < ROPE // 2, pltpu.roll(y, 128 - ROPE // 2, 1), -pltpu.roll(y, ROPE // 2, 1))


def _pair_sum(a_refs, b_refs, trans_b):
    tot = None
    for a_r, b_r in zip(a_refs, b_refs):
        a, b = _bf(a_r[...]), _bf(b_r[...])
        if trans_b:
            d = lax.dot_general(a, b, (((1,), (1,)), ((), ())), preferred_element_type=F32)
        else:
            d = jnp.dot(a, b, preferred_element_type=F32)
        tot = d if tot is None else tot + d
    return tot


def _mm(pairs, out_dtype, name, *, trans_b=False, tm=512, tn=None, tk=None, res=None, dep=None):
    m, k = pairs[0][0].shape
    n = pairs[0][1].shape[0] if trans_b else pairs[0][1].shape[1]
    tm, tn, tk = min(tm, m), tn or n, tk or k
    nk, npair = k // tk, len(pairs)

    def kern(*refs):
        a_refs, b_refs = refs[:npair], refs[npair:2 * npair]
        rest = list(refs[2 * npair:])
        res_ref = rest.pop(0) if res is not None else None
        if dep is not None:
            rest.pop(0)
        o_ref = rest.pop(0)

        def finish(acc):
            if res_ref is not None:
                acc = res_ref[...] + acc
            o_ref[...] = acc.astype(o_ref.dtype)

        if nk == 1:
            finish(_pair_sum(a_refs, b_refs, trans_b))
        else:
            acc_ref = rest.pop(0)
            kk = pl.program_id(2)

            @pl.when(kk == 0)
            def _():
                acc_ref[...] = jnp.zeros_like(acc_ref)

            acc_ref[...] += _pair_sum(a_refs, b_refs, trans_b)

            @pl.when(kk == nk - 1)
            def _():
                finish(acc_ref[...])

    a_spec = pl.BlockSpec((tm, tk), lambda i, j, kk: (i, kk))
    b_spec = (pl.BlockSpec((tn, tk), lambda i, j, kk: (j, kk)) if trans_b
              else pl.BlockSpec((tk, tn), lambda i, j, kk: (kk, j)))
    io_spec = pl.BlockSpec((tm, tn), lambda i, j, kk: (i, j))
    in_specs = ([a_spec] * npair + [b_spec] * npair + ([io_spec] if res is not None else [])
                + ([pl.BlockSpec((8, 128), lambda i, j, kk: (0, 0))] if dep is not None else []))
    args = ([p[0] for p in pairs] + [p[1] for p in pairs] + ([res] if res is not None else [])
            + ([dep] if dep is not None else []))
    return pl.pallas_call(
        kern, name=name, grid=(m // tm, n // tn, nk), in_specs=in_specs, out_specs=io_spec,
        out_shape=jax.ShapeDtypeStruct((m, n), out_dtype),
        scratch_shapes=[pltpu.VMEM((tm, tn), F32)] if nk > 1 else [],
        compiler_params=_params("parallel", "parallel", "arbitrary"))(*args)


def _mm_normbwd(pairs, h, g, dres, dep, name, *, tm=512, tk=None):
    m, k = pairs[0][0].shape
    d = pairs[0][1].shape[0]
    tm, tk = min(tm, m), tk or k
    nk, npair = k // tk, len(pairs)

    def kern(*refs):
        a_refs, b_refs = refs[:npair], refs[npair:2 * npair]
        h_ref, g_ref, dres_ref, _, o_ref, dg_ref, acc_ref = refs[2 * npair:]
        i, kk = pl.program_id(0), pl.program_id(1)

        @pl.when(jnp.logical_and(i == 0, kk == 0))
        def _():
            dg_ref[...] = jnp.zeros_like(dg_ref)

        @pl.when(kk == 0)
        def _():
            acc_ref[...] = jnp.zeros_like(acc_ref)

        acc_ref[...] += _pair_sum(a_refs, b_refs, True)

        @pl.when(kk == nk - 1)
        def _():
            dn = acc_ref[...]
            x = h_ref[...]
            rstd = _rstd(x)
            xhat = x * rstd
            dg_ref[...] += jnp.broadcast_to(jnp.sum(dn * xhat, axis=0, keepdims=True), dg_ref.shape)
            dxh = dn * g_ref[...]
            dx = rstd * (dxh - xhat * jnp.mean(dxh * xhat, axis=-1, keepdims=True))
            o_ref[...] = dres_ref[...] + dx

    row = pl.BlockSpec((tm, d), lambda i, kk: (i, 0))
    in_specs = ([pl.BlockSpec((tm, tk), lambda i, kk: (i, kk))] * npair
                + [pl.BlockSpec((d, tk), lambda i, kk: (0, kk))] * npair
                + [row, pl.BlockSpec((1, d), lambda i, kk: (0, 0)), row, pl.BlockSpec((8, 128), lambda i, kk: (0, 0))])
    return pl.pallas_call(
        kern, name=name, grid=(m // tm, nk), in_specs=in_specs,
        out_specs=[row, pl.BlockSpec((8, d), lambda i, kk: (0, 0))],
        out_shape=[jax.ShapeDtypeStruct((m, d), F32), jax.ShapeDtypeStruct((8, d), F32)],
        scratch_shapes=[pltpu.VMEM((tm, d), F32)],
        compiler_params=_params("arbitrary", "arbitrary"))(
            *[p[0] for p in pairs], *[p[1] for p in pairs], h, g.reshape(1, d), dres, dep)


def _mm_tn(a, b, out_dtype, name, *, bm=None, bn=None, tk=1024):
    t, m = a.shape
    batched = b.ndim == 3
    n = b.shape[-1]
    nb = b.shape[0] if batched else 1
    bm, bn, tk = bm or m, bn or n, min(tk, t)
    nk = t // tk

    def kern(a_ref, b_ref, o_ref, acc_ref):
        kk = pl.program_id(3)

        @pl.when(kk == 0)
        def _():
            acc_ref[...] = jnp.zeros_like(acc_ref)

        acc_ref[...] += lax.dot_general(_bf(a_ref[...]), _bf(b_ref[...]), (((0,), (0,)), ((), ())),
                                        preferred_element_type=F32)

        @pl.when(kk == nk - 1)
        def _():
            o_ref[...] = acc_ref[...].astype(o_ref.dtype)

    a_spec = pl.BlockSpec((tk, bm), lambda h, i, j, kk: (kk, i))
    if batched:
        b_spec = pl.BlockSpec((None, tk, bn), lambda h, i, j, kk: (h, kk, j))
        o_spec = pl.BlockSpec((None, bm, bn), lambda h, i, j, kk: (h, i, j))
        out_shape = jax.ShapeDtypeStruct((nb, m, n), out_dtype)
    else:
        b_spec = pl.BlockSpec((tk, bn), lambda h, i, j, kk: (kk, j))
        o_spec = pl.BlockSpec((bm, bn), lambda h, i, j, kk: (i, j))
        out_shape = jax.ShapeDtypeStruct((m, n), out_dtype)
    return pl.pallas_call(
        kern, name=name, grid=(nb, m // bm, n // bn, nk), in_specs=[a_spec, b_spec], out_specs=o_spec,
        out_shape=out_shape, scratch_shapes=[pltpu.VMEM((bm, bn), F32)],
        compiler_params=_params("parallel", "parallel", "parallel", "arbitrary"))(a, b)


def _ffn_tile(f):
    return f // 2 if (f // 2) % 128 == 0 else f


def _ffn_fwd(h, g, w1t, w3t, w2, dep, name):
    t, d = h.shape
    f = w1t.shape[0]
    tm = min(256, t)
    nt = (((1,), (1,)), ((), ()))

    def kern(h_ref, g_ref, w1_hbm, w3_hbm, w2_hbm, dep_ref, ho_ref, n_ref, z1_ref, z3_ref,
             w1_ref, w3_ref, w2_ref, sems):
        @pl.when(pl.program_id(0) == 0)
        def _():
            copies = [pltpu.make_async_copy(src, dst, sems.at[k]) for k, (src, dst) in
                      enumerate(((w1_hbm, w1_ref), (w3_hbm, w3_ref), (w2_hbm, w2_ref)))]
            for cp in copies:
                cp.start()
            for cp in copies:
                cp.wait()

        x = h_ref[...]
        n = _bf(x * _rstd(x) * g_ref[...])
        n_ref[...] = n
        z1 = lax.dot_general(n, w1_ref[...], nt, preferred_element_type=F32)
        z3 = lax.dot_general(n, w3_ref[...], nt, preferred_element_type=F32)
        z1_ref[...] = _bf(z1)
        z3_ref[...] = _bf(z3)
        act = _bf(z1 * _sigmoid(z1) * z3)
        ho_ref[...] = x + FFN_RES * jnp.dot(act, w2_ref[...], preferred_element_type=F32)

    row = pl.BlockSpec((tm, d), lambda i: (i, 0))
    col = pl.BlockSpec((tm, f), lambda i: (i, 0))
    whole = pl.BlockSpec(memory_space=pl.ANY)
    return pl.pallas_call(
        kern, name=name, grid=(t // tm,),
        in_specs=[row, pl.BlockSpec((1, d), lambda i: (0, 0)), whole, whole, whole,
                  pl.BlockSpec((8, 128), lambda i: (0, 0))],
        out_specs=[row, row, col, col],
        out_shape=[jax.ShapeDtypeStruct((t, d), F32), jax.ShapeDtypeStruct((t, d), BF16),
                   jax.ShapeDtypeStruct((t, f), BF16), jax.ShapeDtypeStruct((t, f), BF16)],
        scratch_shapes=[pltpu.VMEM((f, d), BF16), pltpu.VMEM((f, d), BF16), pltpu.VMEM((f, d), BF16),
                        pltpu.SemaphoreType.DMA((3,))],
        compiler_params=_params("arbitrary"))(h, g.reshape(1, d), w1t, w3t, w2, dep)


def _ffn_bwd_x(dh, h_in, g, z1, z3, w1t, w3t, w2, dep, name):
    t, d = dh.shape
    f = z1.shape[1]
    tm = min(256, t)

    def kern(dh_ref, h_ref, g_ref, z1_ref, z3_ref, w2_hbm, w1_hbm, w3_hbm, dep_ref,
             o_ref, dg_ref, dz1_ref, dz3_ref, a_ref, df_ref, w2_ref, w1_ref, w3_ref, sems):
        @pl.when(pl.program_id(0) == 0)
        def _():
            copies = [pltpu.make_async_copy(src, dst, sems.at[k]) for k, (src, dst) in
                      enumerate(((w2_hbm, w2_ref), (w1_hbm, w1_ref), (w3_hbm, w3_ref)))]
            for cp in copies:
                cp.start()
            dg_ref[...] = jnp.zeros_like(dg_ref)
            for cp in copies:
                cp.wait()

        df = _bf(FFN_RES * dh_ref[...])
        df_ref[...] = df
        da = lax.dot_general(df, w2_ref[...], (((1,), (1,)), ((), ())), preferred_element_type=F32)
        z1v, z3v = z1_ref[...].astype(F32), z3_ref[...].astype(F32)
        sig = _sigmoid(z1v)
        silu = z1v * sig
        a_ref[...] = _bf(silu * z3v)
        dz1 = _bf(da * z3v * (sig * (1.0 + z1v * (1.0 - sig))))
        dz3 = _bf(da * silu)
        dz1_ref[...] = dz1
        dz3_ref[...] = dz3
        dn = (jnp.dot(dz1, w1_ref[...], preferred_element_type=F32)
              + jnp.dot(dz3, w3_ref[...], preferred_element_type=F32))
        x = h_ref[...]
        rstd = _rstd(x)
        xhat = x * rstd
        dg_ref[...] += jnp.broadcast_to(jnp.sum(dn * xhat, axis=0, keepdims=True), dg_ref.shape)
        dxh = dn * g_ref[...]
        o_ref[...] = dh_ref[...] + rstd * (dxh - xhat * jnp.mean(dxh * xhat, axis=-1, keepdims=True))

    row = pl.BlockSpec((tm, d), lambda i: (i, 0))
    col = pl.BlockSpec((tm, f), lambda i: (i, 0))
    whole = pl.BlockSpec(memory_space=pl.ANY)
    colshape = jax.ShapeDtypeStruct((t, f), BF16)
    return pl.pallas_call(
        kern, name=name, grid=(t // tm,),
        in_specs=[row, row, pl.BlockSpec((1, d), lambda i: (0, 0)), col, col, whole, whole, whole,
                  pl.BlockSpec((8, 128), lambda i: (0, 0))],
        out_specs=[row, pl.BlockSpec((8, d), lambda i: (0, 0)), col, col, col, row],
        out_shape=[jax.ShapeDtypeStruct((t, d), F32), jax.ShapeDtypeStruct((8, d), F32), colshape, colshape, colshape,
                   jax.ShapeDtypeStruct((t, d), BF16)],
        scratch_shapes=[pltpu.VMEM((f, d), BF16), pltpu.VMEM((f, d), BF16), pltpu.VMEM((f, d), BF16),
                        pltpu.SemaphoreType.DMA((3,))],
        compiler_params=_params("arbitrary"))(dh, h_in, g.reshape(1, d), z1, z3, w2, w1t, w3t, dep)


def _ffn_bwd(dh, h_in, g, n, z1, z3, w1t, w3t, w2, dep, tag):
    f = w2.shape[0]
    dh_in, dg, dz1, dz3, act, df = _ffn_bwd_x(dh, h_in, g, z1, z3, w1t, w3t, w2, dep, tag + "_bwd_x")
    dw1t = _mm_tn(dz1, n, BF16, tag + "_dw1", bm=_ffn_tile(f), tk=2048)
    dw3t = _mm_tn(dz3, n, BF16, tag + "_dw3", bm=_ffn_tile(f), tk=2048)
    dw2 = _mm_tn(act, df, BF16, tag + "_dw2", bm=_ffn_tile(f), tk=2048)
    return dh_in, dg, dw1t, dw3t, dw2


def _norm_fwd(h, g, dep, name):
    t, d = h.shape
    tm = min(512, t)

    def kern(h_ref, g_ref, dep_ref, o_ref):
        x = h_ref[...]
        o_ref[...] = _bf(x * _rstd(x) * g_ref[...])

    row = pl.BlockSpec((tm, d), lambda i: (i, 0))
    return pl.pallas_call(
        kern, name=name, grid=(t // tm,),
        in_specs=[row, pl.BlockSpec((1, d), lambda i: (0, 0)), pl.BlockSpec((8, 128), lambda i: (0, 0))],
        out_specs=row, out_shape=jax.ShapeDtypeStruct((t, d), BF16),
        compiler_params=_params("parallel"))(h, g.reshape(1, d), dep)


def _loss_bwd(h, target, g):
    t, d = h.shape
    tm = min(512, t)

    def kern(h_ref, t_ref, g_ref, dh_ref, dg_ref, loss_ref):
        @pl.when(pl.program_id(0) == 0)
        def _():
            dg_ref[...] = jnp.zeros_like(dg_ref)
            loss_ref[...] = jnp.zeros_like(loss_ref)

        x = h_ref[...]
        rstd = _rstd(x)
        xhat = x * rstd
        err = xhat * g_ref[...] - t_ref[...]
        row_loss = jnp.sum(err * err, axis=-1, keepdims=True) * (0.5 / d)
        loss_ref[...] += jnp.broadcast_to(jnp.sum(row_loss, axis=0, keepdims=True), loss_ref.shape)
        dy = err * (1.0 / d)
        dg_ref[...] += jnp.broadcast_to(jnp.sum(dy * xhat, axis=0, keepdims=True), dg_ref.shape)
        dxh = dy * g_ref[...]
        dh_ref[...] = rstd * (dxh - xhat * jnp.mean(dxh * xhat, axis=-1, keepdims=True))

    row = pl.BlockSpec((tm, d), lambda i: (i, 0))
    return pl.pallas_call(
        kern, name="loss_bwd", grid=(t // tm,),
        in_specs=[row, row, pl.BlockSpec((1, d), lambda i: (0, 0))],
        out_specs=[row, pl.BlockSpec((8, d), lambda i: (0, 0)), pl.BlockSpec((8, 128), lambda i: (0, 0))],
        out_shape=[jax.ShapeDtypeStruct((t, d), F32), jax.ShapeDtypeStruct((8, d), F32),
                   jax.ShapeDtypeStruct((8, 128), F32)],
        compiler_params=_params("arbitrary"))(h, target, g.reshape(1, d))


def _glu_fwd(m, wa, wb):
    t, d = m.shape
    c = wa.shape[1]
    tm, tc = min(512, t), c

    def kern(m_ref, wa_ref, wb_ref, a_ref, b_ref, glu_ref):
        mv = m_ref[...]
        a = jnp.dot(mv, wa_ref[...], preferred_element_type=F32)
        b = jnp.dot(mv, wb_ref[...], preferred_element_type=F32)
        a_ref[...] = _bf(a)
        b_ref[...] = _bf(b)
        glu_ref[...] = _bf(a * _sigmoid(b))

    col = pl.BlockSpec((tm, tc), lambda i, j: (i, j))
    wspec = pl.BlockSpec((d, tc), lambda i, j: (0, j))
    shape = jax.ShapeDtypeStruct((t, c), BF16)
    return pl.pallas_call(
        kern, name="conv_glu_fwd", grid=(t // tm, c // tc),
        in_specs=[pl.BlockSpec((tm, d), lambda i, j: (i, 0)), wspec, wspec], out_specs=[col, col, col],
        out_shape=[shape, shape, shape], compiler_params=_params("parallel", "parallel"))(m, wa, wb)


def _conv_tile(t):
    return min(256, t)


def _shift_copies(ext, shifted, rows):
    for s in range(8):
        shifted[s] = ext[pl.ds(s, rows), :]


def _shifted_rows(shifted, start, nrows):
    return shifted[start % 8, pl.ds(start - start % 8, nrows), :]


def _conv_fwd(glu, w_dw, g):
    t, c = glu.shape
    tm = _conv_tile(t)
    hb = tm // CONV_HALO

    def kern(cur_ref, halo_ref, w_ref, g_ref, cv_ref, s_ref, ext, shifted):
        i = pl.program_id(0)
        ext[0:CONV_HALO, :] = jnp.where(i > 0, halo_ref[...].astype(F32), 0.0)
        ext[CONV_HALO:tm + CONV_HALO, :] = cur_ref[...].astype(F32)
        ext[tm + CONV_HALO:, :] = jnp.zeros((8, c), F32)
        _shift_copies(ext, shifted, tm + CONV_HALO)
        gv = g_ref[...]
        for r0 in range(0, tm, CONV_ROWS):
            acc = jnp.zeros((CONV_ROWS, c), F32)
            for k in range(CONV_WIDTH):
                acc = acc + _shifted_rows(shifted, r0 + 2 + k, CONV_ROWS) * w_ref[k:k + 1, :]
            cv_ref[r0:r0 + CONV_ROWS, :] = acc
            rn = acc * _rstd(acc) * gv
            s_ref[r0:r0 + CONV_ROWS, :] = _bf(rn * _sigmoid(rn))

    row = pl.BlockSpec((tm, c), lambda i: (i, 0))
    return pl.pallas_call(
        kern, name="conv_fwd", grid=(t // tm,),
        in_specs=[row, pl.BlockSpec((CONV_HALO, c), lambda i: (jnp.maximum(i * hb - 1, 0), 0)),
                  pl.BlockSpec((CONV_HALO, c), lambda i: (0, 0)), pl.BlockSpec((1, c), lambda i: (0, 0))],
        out_specs=[row, row],
        out_shape=[jax.ShapeDtypeStruct((t, c), F32), jax.ShapeDtypeStruct((t, c), BF16)],
        scratch_shapes=[pltpu.VMEM((tm + CONV_HALO + 8, c), F32), pltpu.VMEM((8, tm + CONV_HALO, c), F32)],
        compiler_params=_params("parallel"))(glu, glu, w_dw, g.reshape(1, c))


def _conv_bwd_norm(dh, cv, w_pw2, g, dep):
    t, c = cv.shape
    tm = min(512, t)

    def kern(dh_ref, cv_ref, w_ref, g_ref, dep_ref, dcv_ref, dg_ref):
        @pl.when(pl.program_id(0) == 0)
        def _():
            dg_ref[...] = jnp.zeros_like(dg_ref)

        ds = lax.dot_general(_bf(dh_ref[...]), w_ref[...], (((1,), (1,)), ((), ())), preferred_element_type=F32)
        x = cv_ref[...]
        rstd = _rstd(x)
        xhat = x * rstd
        rn = xhat * g_ref[...]
        sig = _sigmoid(rn)
        drn = ds * (sig * (1.0 + rn * (1.0 - sig)))
        dg_ref[...] += jnp.broadcast_to(jnp.sum(drn * xhat, axis=0, keepdims=True), dg_ref.shape)
        dxh = drn * g_ref[...]
        dcv_ref[...] = rstd * (dxh - xhat * jnp.mean(dxh * xhat, axis=-1, keepdims=True))

    row = pl.BlockSpec((tm, c), lambda i: (i, 0))
    return pl.pallas_call(
        kern, name="conv_bwd_norm", grid=(t // tm,),
        in_specs=[pl.BlockSpec((tm, dh.shape[1]), lambda i: (i, 0)), row,
                  pl.BlockSpec(w_pw2.shape, lambda i: (0, 0)), pl.BlockSpec((1, c), lambda i: (0, 0)),
                  pl.BlockSpec((8, 128), lambda i: (0, 0))],
        out_specs=[row, pl.BlockSpec((8, c), lambda i: (0, 0))],
        out_shape=[jax.ShapeDtypeStruct((t, c), F32), jax.ShapeDtypeStruct((8, c), F32)],
        compiler_params=_params("arbitrary"))(dh, cv, w_pw2, g.reshape(1, c), dep)


def _conv_bwd_dw(dcv, glu, a, b, w_dw):
    t, c = dcv.shape
    tm = _conv_tile(t)
    hb = tm // CONV_HALO
    last = t // CONV_HALO - 1

    def kern(dcv_ref, dnext_ref, glu_ref, gprev_ref, a_ref, b_ref, w_ref, da_ref, db_ref, dw_ref,
             dext, gext, dshift, gshift):
        i = pl.program_id(0)

        @pl.when(i == 0)
        def _():
            dw_ref[...] = jnp.zeros_like(dw_ref)

        dext[0:tm, :] = dcv_ref[...]
        dext[tm:tm + CONV_HALO, :] = jnp.where(i < t // tm - 1, dnext_ref[...], 0.0)
        dext[tm + CONV_HALO:, :] = jnp.zeros((8, c), F32)
        gext[0:CONV_HALO, :] = jnp.where(i > 0, gprev_ref[...].astype(F32), 0.0)
        gext[CONV_HALO:tm + CONV_HALO, :] = glu_ref[...].astype(F32)
        gext[tm + CONV_HALO:, :] = jnp.zeros((8, c), F32)
        _shift_copies(dext, dshift, tm + CONV_HALO)
        _shift_copies(gext, gshift, tm + CONV_HALO)
        for r0 in range(0, tm, CONV_ROWS):
            acc = jnp.zeros((CONV_ROWS, c), F32)
            for k in range(CONV_WIDTH):
                acc = acc + _shifted_rows(dshift, r0 + CONV_WIDTH - 1 - k, CONV_ROWS) * w_ref[k:k + 1, :]
            av = a_ref[r0:r0 + CONV_ROWS, :].astype(F32)
            sig = _sigmoid(b_ref[r0:r0 + CONV_ROWS, :].astype(F32))
            da_ref[r0:r0 + CONV_ROWS, :] = _bf(acc * sig)
            db_ref[r0:r0 + CONV_ROWS, :] = _bf(acc * av * sig * (1.0 - sig))
        for k in range(CONV_WIDTH):
            acc = jnp.zeros((CONV_ROWS, c), F32)
            for r0 in range(0, tm, CONV_ROWS):
                acc = acc + _shifted_rows(gshift, r0 + 2 + k, CONV_ROWS) * dext[r0:r0 + CONV_ROWS, :]
            dw_ref[k:k + 1, :] += jnp.sum(acc, axis=0, keepdims=True)

    row = pl.BlockSpec((tm, c), lambda i: (i, 0))
    shape = jax.ShapeDtypeStruct((t, c), BF16)
    return pl.pallas_call(
        kern, name="conv_bwd_dw", grid=(t // tm,),
        in_specs=[row, pl.BlockSpec((CONV_HALO, c), lambda i: (jnp.minimum((i + 1) * hb, last), 0)),
                  row, pl.BlockSpec((CONV_HALO, c), lambda i: (jnp.maximum(i * hb - 1, 0), 0)),
                  row, row, pl.BlockSpec((CONV_HALO, c), lambda i: (0, 0))],
        out_specs=[row, row, pl.BlockSpec((CONV_HALO, c), lambda i: (0, 0))],
        out_shape=[shape, shape, jax.ShapeDtypeStruct((CONV_HALO, c), F32)],
        scratch_shapes=[pltpu.VMEM((tm + CONV_HALO + 8, c), F32), pltpu.VMEM((tm + CONV_HALO + 8, c), F32),
                        pltpu.VMEM((8, tm + CONV_HALO, c), F32), pltpu.VMEM((8, tm + CONV_HALO, c), F32)],
        compiler_params=_params("arbitrary"))(dcv, dcv, glu, glu, a, b, w_dw)


def _rope_tables(pos):
    t = pos.shape[0]
    tm = min(512, t)
    freq = (np.float32(ROPE_THETA) ** (np.float32(-2.0) * np.arange(ROPE // 2, dtype=np.float32)
                                       / np.float32(ROPE))).astype(np.float32)
    row = np.zeros((2, 128), np.float32)
    row[0, :ROPE] = np.concatenate([freq, freq])
    row[1, :ROPE] = 1.0

    def kern(pos_ref, f_ref, c_ref, s_ref):
        ang = pos_ref[...].astype(F32) * f_ref[0:1, :]
        mask = f_ref[1:2, :]
        c_ref[...] = jnp.cos(ang) * mask
        s_ref[...] = jnp.sin(ang) * mask

    out = pl.BlockSpec((tm, 128), lambda i: (i, 0))
    shape = jax.ShapeDtypeStruct((t, 128), F32)
    return pl.pallas_call(
        kern, name="rope_tables", grid=(t // tm,),
        in_specs=[pl.BlockSpec((tm, 1), lambda i: (i, 0)), pl.BlockSpec((2, 128), lambda i: (0, 0))],
        out_specs=[out, out], out_shape=[shape, shape], compiler_params=_params("parallel"))(pos, jnp.asarray(row))


def _mla_prep(a, gq, gkv, cs_c, cs_s):
    t = a.shape[0]
    tm = min(512, t)
    kv0, r0 = Q_LORA, Q_LORA + KV_LORA

    def kern(a_ref, gq_ref, gkv_ref, c_ref, s_ref, cq_ref, ckv_ref, kr_ref):
        aq = a_ref[:, 0:kv0]
        akv = a_ref[:, kv0:r0]
        ar = a_ref[:, r0:A_PAD]
        cq_ref[...] = _bf(aq * _rstd(aq) * gq_ref[...])
        ckv_ref[...] = _bf(akv * _rstd(akv) * gkv_ref[...])
        kr_ref[...] = _bf(ar * c_ref[...] + _rot(ar) * s_ref[...])

    def row(w):
        return pl.BlockSpec((tm, w), lambda i: (i, 0))

    def vec(w):
        return pl.BlockSpec((1, w), lambda i: (0, 0))

    return pl.pallas_call(
        kern, name="mla_prep", grid=(t // tm,),
        in_specs=[row(A_PAD), vec(Q_LORA), vec(KV_LORA), row(128), row(128)],
        out_specs=[row(Q_LORA), row(KV_LORA), row(128)],
        out_shape=[jax.ShapeDtypeStruct((t, Q_LORA), BF16), jax.ShapeDtypeStruct((t, KV_LORA), BF16),
                   jax.ShapeDtypeStruct((t, 128), BF16)],
        compiler_params=_params("parallel"))(a, gq.reshape(1, -1), gkv.reshape(1, -1), cs_c, cs_s)


def _mla_prep_bwd(a, dcq, dckv, dar, gq, gkv):
    t = a.shape[0]
    tm = min(512, t)
    kv0, r0 = Q_LORA, Q_LORA + KV_LORA

    def kern(a_ref, dcq_ref, dckv_ref, dar_ref, gq_ref, gkv_ref, da_ref, dgq_ref, dgkv_ref):
        @pl.when(pl.program_id(0) == 0)
        def _():
            dgq_ref[...] = jnp.zeros_like(dgq_ref)
            dgkv_ref[...] = jnp.zeros_like(dgkv_ref)

        def back(x, dy, g_ref, dg_ref):
            rstd = _rstd(x)
            xhat = x * rstd
            dg_ref[...] += jnp.broadcast_to(jnp.sum(dy * xhat, axis=0, keepdims=True), dg_ref.shape)
            dxh = dy * g_ref[...]
            return rstd * (dxh - xhat * jnp.mean(dxh * xhat, axis=-1, keepdims=True))

        da_ref[:, 0:kv0] = _bf(back(a_ref[:, 0:kv0], dcq_ref[...], gq_ref, dgq_ref))
        da_ref[:, kv0:r0] = _bf(back(a_ref[:, kv0:r0], dckv_ref[...], gkv_ref, dgkv_ref))
        da_ref[:, r0:A_PAD] = _bf(dar_ref[...])

    def row(w):
        return pl.BlockSpec((tm, w), lambda i: (i, 0))

    def vec(r, w):
        return pl.BlockSpec((r, w), lambda i: (0, 0))

    return pl.pallas_call(
        kern, name="mla_prep_bwd", grid=(t // tm,),
        in_specs=[row(A_PAD), row(Q_LORA), row(KV_LORA), row(128), vec(1, Q_LORA), vec(1, KV_LORA)],
        out_specs=[row(A_PAD), vec(8, Q_LORA), vec(8, KV_LORA)],
        out_shape=[jax.ShapeDtypeStruct((t, A_PAD), BF16), jax.ShapeDtypeStruct((8, Q_LORA), F32),
                   jax.ShapeDtypeStruct((8, KV_LORA), F32)],
        compiler_params=_params("arbitrary"))(a, dcq, dckv, dar, gq.reshape(1, -1), gkv.reshape(1, -1))


def _mla_qkv(cq, ckv, kr, cs_c, cs_s, wuq, wukv):
    t = cq.shape[0]
    tm = min(512, t)
    kvw = NOPE + V_HEAD

    def kern(cq_ref, ckv_ref, kr_ref, c_ref, s_ref, wq_ref, wkv_ref, q_ref, k_ref, v_ref):
        r = jnp.dot(cq_ref[...], wq_ref[...], preferred_element_type=F32)
        kv = jnp.dot(ckv_ref[...], wkv_ref[...], preferred_element_type=F32)
        cv, sv, krv = c_ref[...], s_ref[...], kr_ref[...]
        for h in range(HEADS):
            xr = r[:, h * HEAD_PAD + NOPE:(h + 1) * HEAD_PAD]
            q_ref[h, :, 0:NOPE] = _bf(r[:, h * HEAD_PAD:h * HEAD_PAD + NOPE] * ATTN_SCALE)
            q_ref[h, :, NOPE:] = _bf((xr * cv + _rot(xr) * sv) * ATTN_SCALE)
            k_ref[h, :, 0:NOPE] = _bf(kv[:, h * kvw:h * kvw + NOPE])
            k_ref[h, :, NOPE:] = krv
            v_ref[h] = _bf(kv[:, h * kvw + NOPE:(h + 1) * kvw])

    def row(w):
        return pl.BlockSpec((tm, w), lambda i: (i, 0))

    def heads(w):
        return pl.BlockSpec((HEADS, tm, w), lambda i: (0, i, 0))

    return pl.pallas_call(
        kern, name="mla_qkv", grid=(t // tm,),
        in_specs=[row(Q_LORA), row(KV_LORA), row(128), row(128), row(128),
                  pl.BlockSpec(wuq.shape, lambda i: (0, 0)), pl.BlockSpec(wukv.shape, lambda i: (0, 0))],
        out_specs=[heads(HEAD_PAD), heads(HEAD_PAD), heads(V_HEAD)],
        out_shape=[jax.ShapeDtypeStruct((HEADS, t, HEAD_PAD), BF16), jax.ShapeDtypeStruct((HEADS, t, HEAD_PAD), BF16),
                   jax.ShapeDtypeStruct((HEADS, t, V_HEAD), BF16)],
        compiler_params=_params("parallel"))(cq, ckv, kr, cs_c, cs_s, wuq, wukv)


def _mla_qkv_bwd(dq, dk, dv, cs_c, cs_s, wuq, wukv):
    t = dq.shape[1]
    tm = min(256, t)
    kvw = NOPE + V_HEAD

    def kern(dq_ref, dk_ref, dv_ref, c_ref, s_ref, wq_ref, wkv_ref, dr_ref, dkv_ref, dcq_ref, dckv_ref, dar_ref):
        cv, sv = c_ref[...], s_ref[...]
        dar = jnp.zeros_like(cv)
        for h in range(HEADS):
            dqx = dq_ref[h, :, NOPE:].astype(F32)
            dr_ref[:, h * HEAD_PAD:h * HEAD_PAD + NOPE] = _bf(dq_ref[h, :, 0:NOPE].astype(F32) * ATTN_SCALE)
            dr_ref[:, h * HEAD_PAD + NOPE:(h + 1) * HEAD_PAD] = _bf((dqx * cv + _rot_t(dqx * sv)) * ATTN_SCALE)
            dkx = dk_ref[h, :, NOPE:].astype(F32)
            dar = dar + (dkx * cv + _rot_t(dkx * sv))
            dkv_ref[:, h * kvw:h * kvw + NOPE] = dk_ref[h, :, 0:NOPE]
            dkv_ref[:, h * kvw + NOPE:(h + 1) * kvw] = dv_ref[h]
        dar_ref[...] = dar
        dcq_ref[...] = lax.dot_general(dr_ref[...], wq_ref[...], (((1,), (1,)), ((), ())),
                                       preferred_element_type=F32)
        dckv_ref[...] = lax.dot_general(dkv_ref[...], wkv_ref[...], (((1,), (1,)), ((), ())),
                                        preferred_element_type=F32)

    def row(w):
        return pl.BlockSpec((tm, w), lambda i: (i, 0))

    def heads(w):
        return pl.BlockSpec((HEADS, tm, w), lambda i: (0, i, 0))

    return pl.pallas_call(
        kern, name="mla_qkv_bwd", grid=(t // tm,),
        in_specs=[heads(HEAD_PAD), heads(HEAD_PAD), heads(V_HEAD), row(128), row(128),
                  pl.BlockSpec(wuq.shape, lambda i: (0, 0)), pl.BlockSpec(wukv.shape, lambda i: (0, 0))],
        out_specs=[row(HEADS * HEAD_PAD), row(HEADS * kvw), row(Q_LORA), row(KV_LORA), row(128)],
        out_shape=[jax.ShapeDtypeStruct((t, HEADS * HEAD_PAD), BF16), jax.ShapeDtypeStruct((t, HEADS * kvw), BF16),
                   jax.ShapeDtypeStruct((t, Q_LORA), F32), jax.ShapeDtypeStruct((t, KV_LORA), F32),
                   jax.ShapeDtypeStruct((t, 128), F32)],
        compiler_params=_params("parallel"))(dq, dk, dv, cs_c, cs_s, wuq, wukv)


def _attn_block(t):
    return 512 if t >= 4096 else 128


def _chunk_mask(bk, bq):
    kc = lax.broadcasted_iota(jnp.int32, (bk, bq), 0) // CHUNK
    qc = lax.broadcasted_iota(jnp.int32, (bk, bq), 1) // CHUNK
    return qc >= kc


def _flash_fwd(q, k, v):
    t = q.shape[1]
    bq = _attn_block(t)
    nq = t // bq
    nch = 2

    def kern(q_ref, k_ref, v_ref, o_ref, lse_ref, s_buf, p_buf, m_ref, l_ref, acc_ref):
        i = pl.program_id(1)
        queries = [q_ref[c * bq:(c + 1) * bq, :] for c in range(nch)]

        def block(j):
            rows = pl.ds(pl.multiple_of(j * bq, bq), bq)
            return k_ref[rows, :], v_ref[rows, :]

        def scores(kj, chain):
            return lax.dot_general(kj, queries[chain], (((1,), (1,)), ((), ())), preferred_element_type=F32)

        def softmax_block(chain, slot, vj):
            for c0 in range(0, bq, 128):
                cols = slice(c0, c0 + 128)
                s = s_buf[slot, chain, :, cols]
                m_old = m_ref[chain, 0:1, cols]
                m_new = jnp.maximum(m_old, jnp.max(s, axis=0, keepdims=True))
                alpha = jnp.exp(m_old - m_new)
                p = jnp.exp(s - m_new)
                l_ref[chain, 0:1, cols] = alpha * l_ref[chain, 0:1, cols] + jnp.sum(p, axis=0, keepdims=True)
                m_ref[chain, 0:1, cols] = m_new
                p_buf[chain, :, cols] = _bf(p)
                acc_ref[chain, :, cols] = acc_ref[chain, :, cols] * alpha
            acc_ref[chain] += lax.dot_general(vj, p_buf[chain], (((0,), (0,)), ((), ())),
                                              preferred_element_type=F32)

        m_ref[...] = jnp.full(m_ref.shape, -1e30, F32)
        l_ref[...] = jnp.zeros_like(l_ref)
        acc_ref[...] = jnp.zeros_like(acc_ref)
        mask = _chunk_mask(bq, bq)
        for b in range(nch):
            kb, vb = block(nch * i + b)
            for c in range(b, nch):
                s_buf[b % 2, c] = jnp.where(mask, scores(kb, c), -1e30) if c == b else scores(kb, c)
                softmax_block(c, b % 2, vb)
        kf = block(0)[0]
        for c in range(nch):
            s_buf[0, c] = scores(kf, c)

        def body(pair, carry):
            for cur in range(2):
                j = 2 * pair + cur
                kn = block(jnp.minimum(j + 1, jnp.maximum(nch * i - 1, 0)))[0]
                for c in range(nch):
                    s_buf[1 - cur, c] = scores(kn, c)
                vj = block(j)[1]
                for c in range(nch):
                    softmax_block(c, cur, vj)
            return carry

        lax.fori_loop(0, (nch // 2) * i, body, 0)
        for chain in range(nch):
            l = l_ref[chain, 0:1, :]
            o_ref[chain * bq:(chain + 1) * bq, :] = _bf((acc_ref[chain] / l).T)
            lse_ref[chain] = jnp.broadcast_to(m_ref[chain, 0:1, :] + jnp.log(l), (8, bq))

    return pl.pallas_call(
        kern, name="flash_fwd", grid=(HEADS, nq // nch),
        in_specs=[pl.BlockSpec((None, nch * bq, HEAD_PAD), lambda h, i: (h, i, 0)),
                  pl.BlockSpec((None, t, HEAD_PAD), lambda h, i: (h, 0, 0)),
                  pl.BlockSpec((None, t, V_HEAD), lambda h, i: (h, 0, 0))],
        out_specs=[pl.BlockSpec((nch * bq, V_HEAD), lambda h, i: (i, h)),
                   pl.BlockSpec((None, nch, 8, bq), lambda h, i: (h, i, 0, 0))],
        out_shape=[jax.ShapeDtypeStruct((t, HEADS * V_HEAD), BF16), jax.ShapeDtypeStruct((HEADS, nq, 8, bq), F32)],
        scratch_shapes=[pltpu.VMEM((2, nch, bq, bq), F32), pltpu.VMEM((nch, bq, bq), BF16),
                        pltpu.VMEM((nch, 8, bq), F32), pltpu.VMEM((nch, 8, bq), F32),
                        pltpu.VMEM((nch, V_HEAD, bq), F32)],
        compiler_params=_params("parallel", "arbitrary"))(q, k, v)


def _attn_delta(do, o):
    t = do.shape[0]
    bq = _attn_block(t)

    def kern(do_ref, o_ref, d_ref):
        for h in range(HEADS):
            cols = slice(h * V_HEAD, (h + 1) * V_HEAD)
            prod = do_ref[:, cols].astype(F32) * o_ref[:, cols].astype(F32)
            d_ref[h] = jnp.broadcast_to(jnp.sum(prod.T, axis=0, keepdims=True), (8, bq))

    blk = pl.BlockSpec((bq, HEADS * V_HEAD), lambda i: (i, 0))
    return pl.pallas_call(
        kern, name="attn_delta", grid=(t // bq,), in_specs=[blk, blk],
        out_specs=pl.BlockSpec((HEADS, None, 8, bq), lambda i: (0, i, 0, 0)),
        out_shape=jax.ShapeDtypeStruct((HEADS, t // bq, 8, bq), F32),
        compiler_params=_params("parallel"))(do, o)


def _flash_bwd(q, k, v, do, lse, delta):
    t = q.shape[1]
    bq = _attn_block(t)
    nq = t // bq

    def kern(q_ref, k_ref, v_ref, do_ref, lse_ref, del_ref, dq_out, dk_out, dv_out, dq_ref, dk_ref, dvt_ref):
        j = pl.program_id(1)

        @pl.when(j == 0)
        def _():
            dq_ref[...] = jnp.zeros_like(dq_ref)

        dk_ref[...] = jnp.zeros_like(dk_ref)
        dvt_ref[...] = jnp.zeros_like(dvt_ref)
        kj, vj = k_ref[...], v_ref[...]

        def step(i, masked):
            rows = pl.ds(pl.multiple_of(i * bq, bq), bq)
            qi, doi = q_ref[rows, :], do_ref[rows, :]
            st = lax.dot_general(kj, qi, (((1,), (1,)), ((), ())), preferred_element_type=F32)
            pt = jnp.exp(st - lse_ref[i][0:1, :])
            if masked:
                pt = jnp.where(_chunk_mask(bq, bq), pt, 0.0)
            dpt = lax.dot_general(vj, doi, (((1,), (1,)), ((), ())), preferred_element_type=F32)
            dst = _bf(pt * (dpt - del_ref[i][0:1, :]))
            dvt_ref[...] += lax.dot_general(doi, _bf(pt), (((0,), (1,)), ((), ())), preferred_element_type=F32)
            dk_ref[...] += jnp.dot(dst, qi, preferred_element_type=F32)
            dq_ref[rows, :] += lax.dot_general(dst, kj, (((0,), (0,)), ((), ())), preferred_element_type=F32)

        step(j, True)

        def body(pair, carry):
            step(j + 1 + 2 * pair, False)
            step(j + 2 + 2 * pair, False)
            return carry

        rest = nq - 1 - j
        lax.fori_loop(0, rest // 2, body, 0)

        @pl.when(rest % 2 == 1)
        def _():
            step(nq - 1, False)

        dk_out[...] = _bf(dk_ref[...])
        dv_out[...] = _bf(dvt_ref[...].T)

        @pl.when(j == nq - 1)
        def _():
            dq_out[...] = _bf(dq_ref[...])

    stat = pl.BlockSpec((None, nq, 8, bq), lambda h, j: (h, 0, 0, 0))
    return pl.pallas_call(
        kern, name="flash_bwd", grid=(HEADS, nq),
        in_specs=[pl.BlockSpec((None, t, HEAD_PAD), lambda h, j: (h, 0, 0)),
                  pl.BlockSpec((None, bq, HEAD_PAD), lambda h, j: (h, j, 0)),
                  pl.BlockSpec((None, bq, V_HEAD), lambda h, j: (h, j, 0)),
                  pl.BlockSpec((t, V_HEAD), lambda h, j: (0, h)), stat, stat],
        out_specs=[pl.BlockSpec((None, t, HEAD_PAD), lambda h, j: (h, 0, 0)),
                   pl.BlockSpec((None, bq, HEAD_PAD), lambda h, j: (h, j, 0)),
                   pl.BlockSpec((None, bq, V_HEAD), lambda h, j: (h, j, 0))],
        out_shape=[jax.ShapeDtypeStruct((HEADS, t, HEAD_PAD), BF16), jax.ShapeDtypeStruct((HEADS, t, HEAD_PAD), BF16),
                   jax.ShapeDtypeStruct((HEADS, t, V_HEAD), BF16)],
        scratch_shapes=[pltpu.VMEM((t, HEAD_PAD), F32), pltpu.VMEM((bq, HEAD_PAD), F32), pltpu.VMEM((V_HEAD, bq), F32)],
        compiler_params=_params("parallel", "arbitrary"))(q, k, v, do, lse, delta)


def _place():
    x, y, c = lax.axis_index("x"), lax.axis_index("y"), lax.axis_index("c")
    return x, y, c, [(1 - x, y), (x, 1 - y), (1 - x, 1 - y)]


def _all_gather_rows(block, name):
    m_per, n = block.shape

    def body(x_ref, out_ref, send_sems, recv_sems, local_sem):
        x, y, c, chips = _place()
        me, sibling = (x, y, c), (x, y, 1 - c)

        def rows(px, py, pc):
            return out_ref.at[pl.ds((4 * px + 2 * py + pc) * m_per, m_per), :]

        def copy(k, blk, to, src=None):
            return pltpu.make_async_remote_copy(
                src_ref=rows(*blk) if src is None else src, dst_ref=rows(*blk), send_sem=send_sems.at[k],
                recv_sem=recv_sems.at[k], device_id=to, device_id_type=MESH)

        mine = pltpu.make_async_copy(x_ref, rows(*me), local_sem)
        mine.start()
        first = [copy(0, me, sibling, src=x_ref)]
        first += [copy(1 + j, me, (*chip, c), src=x_ref) for j, chip in enumerate(chips)]
        for cp in first:
            cp.start()
        passed = [copy(4 + j, (*chip, c), sibling) for j, chip in enumerate(chips)]
        for j, chip in enumerate(chips):
            copy(1 + j, (*chip, c), me).wait_recv()
            passed[j].start()
        copy(0, sibling, me).wait_recv()
        for j, chip in enumerate(chips):
            copy(4 + j, (*chip, 1 - c), me).wait_recv()
        for cp in first + passed:
            cp.wait_send()
        mine.wait()

    return pl.pallas_call(
        body, name=name, out_shape=jax.ShapeDtypeStruct((8 * m_per, n), block.dtype),
        in_specs=[pl.BlockSpec(memory_space=pltpu.VMEM)], out_specs=pl.BlockSpec(memory_space=pltpu.VMEM),
        scratch_shapes=[pltpu.SemaphoreType.DMA((7,)), pltpu.SemaphoreType.DMA((7,)), pltpu.SemaphoreType.DMA],
        compiler_params=pltpu.CompilerParams(vmem_limit_bytes=VMEM_LIMIT_BYTES))(block)


HBM_SPEC = pl.BlockSpec(memory_space=pltpu.HBM)
SEM_SPEC = pl.BlockSpec(memory_space=pltpu.SEMAPHORE)
DATAFLOW = pltpu.SideEffectType.DATAFLOW_SIDE_EFFECTING


def _in_hbm(a):
    return pltpu.with_memory_space_constraint(a, pltpu.HBM)


def _chip_copies(ins, lands, send_sems, recv_sems, src_slot, half=False):
    n = len(ins)
    x, y, c, chips = _place()
    me = 2 * x + y

    def ends(w, chip):
        src = ins[w].at[2 * chip[0] + chip[1]] if src_slot else ins[w]
        if not half:
            return src, lands[w].at[me]
        rows = pl.ds(pl.multiple_of(c * (src.shape[0] // 2), 16), src.shape[0] // 2)
        return src.at[rows], lands[w].at[me, rows]

    copies = []
    for w in range(n):
        for p, chip in enumerate(chips):
            src, dst = ends(w, chip)
            copies.append(pltpu.make_async_remote_copy(
                src_ref=src, dst_ref=dst, send_sem=send_sems.at[p * n + w], recv_sem=recv_sems.at[p * n + w],
                device_id=(*chip, c), device_id_type=MESH))
    return copies


def _fill_halves(lands, name):
    n = len(lands)

    def body(*refs):
        bufs = refs[n:2 * n]
        send_sems, recv_sems = refs[2 * n:]
        x, y, c, chips = _place()
        copies = []
        for w in range(n):
            hr = bufs[w].shape[1] // 2
            for p, chip in enumerate(chips):
                part = bufs[w].at[2 * chip[0] + chip[1], pl.ds(pl.multiple_of(c * hr, 16), hr)]
                copies.append(pltpu.make_async_remote_copy(
                    src_ref=part, dst_ref=part, send_sem=send_sems.at[p * n + w], recv_sem=recv_sems.at[p * n + w],
                    device_id=(x, y, 1 - c), device_id_type=MESH))
        for cp in copies:
            cp.start()
        for cp in copies:
            cp.wait_send()
        for w in range(n):
            hr = bufs[w].shape[1] // 2
            for p, chip in enumerate(chips):
                part = bufs[w].at[2 * chip[0] + chip[1], pl.ds(pl.multiple_of((1 - c) * hr, 16), hr)]
                pltpu.make_async_remote_copy(
                    src_ref=part, dst_ref=part, send_sem=send_sems.at[p * n + w], recv_sem=recv_sems.at[p * n + w],
                    device_id=(x, y, 1 - c), device_id_type=MESH).wait_recv()

    any_spec = pl.BlockSpec(memory_space=pl.ANY)
    return list(pl.pallas_call(
        body, name=name, out_shape=[jax.ShapeDtypeStruct(a.shape, a.dtype) for a in lands],
        in_specs=[any_spec] * n, out_specs=[any_spec] * n, input_output_aliases={i: i for i in range(n)},
        scratch_shapes=[pltpu.SemaphoreType.DMA((3 * n,)), pltpu.SemaphoreType.DMA((3 * n,))])(*lands))


def _exchange_start(srcs, lands, src_slot, name, dep=None, half=False):
    n = len(srcs)
    first_out = 2 * n + (dep is not None)

    def body(*refs):
        for cp in _chip_copies(refs[:n], refs[n:2 * n], refs[first_out], refs[first_out + 1], src_slot, half):
            cp.start()
        token = refs[-1]
        token[...] = jnp.zeros_like(token)

    thru = [pltpu.HBM(a.shape, a.dtype) for a in list(srcs) + list(lands)]
    res = pl.pallas_call(
        body, name=name,
        out_shape=(pltpu.SemaphoreType.DMA((3 * n,)), pltpu.SemaphoreType.DMA((3 * n,)), *thru,
                   jax.ShapeDtypeStruct((8, 128), F32)),
        in_specs=[HBM_SPEC] * (2 * n) + ([pl.BlockSpec(memory_space=pl.ANY)] if dep is not None else []),
        out_specs=(SEM_SPEC, SEM_SPEC, *[HBM_SPEC] * (2 * n), pl.BlockSpec(memory_space=pltpu.VMEM)),
        input_output_aliases={i: 2 + i for i in range(2 * n)},
        compiler_params=pltpu.CompilerParams(has_side_effects=DATAFLOW))(
            *[_in_hbm(a) for a in srcs], *[_in_hbm(a) for a in lands], *([dep] if dep is not None else []))
    return (res[0], res[1], list(res[2:2 + n]), list(res[2 + n:2 + 2 * n])), res[-1]


def _exchange_wait(flight, after, src_slot, name, half=False):
    send_sems, recv_sems, srcs, lands = flight
    n = len(srcs)

    def body(*refs):
        for cp in _chip_copies(refs[:n], refs[n:2 * n], refs[2 * n], refs[2 * n + 1], src_slot, half):
            cp.wait_send()
            cp.wait_recv()

    thru = [pltpu.HBM(a.shape, a.dtype) for a in list(srcs) + list(lands)]
    res = pl.pallas_call(
        body, name=name, out_shape=thru,
        in_specs=[HBM_SPEC] * (2 * n) + [SEM_SPEC, SEM_SPEC, pl.BlockSpec(memory_space=pl.ANY)],
        out_specs=[HBM_SPEC] * (2 * n), input_output_aliases={i: i for i in range(2 * n)},
        compiler_params=pltpu.CompilerParams(has_side_effects=DATAFLOW))(*srcs, *lands, send_sems, recv_sems, after)
    return list(res[n:])


def _landing(own, me):
    return lax.dynamic_update_index_in_dim(lax.empty((4, *own.shape), own.dtype), own, me, 0)


def _swap_with_sibling(arrays, name):
    n = len(arrays)

    def body(*refs):
        ins, outs = refs[:n], refs[n:2 * n]
        send_sems, recv_sems = refs[2 * n:]
        x, y, c, _ = _place()
        copies = [pltpu.make_async_remote_copy(src_ref=ins[w], dst_ref=outs[w], send_sem=send_sems.at[w],
                                               recv_sem=recv_sems.at[w], device_id=(x, y, 1 - c), device_id_type=MESH)
                  for w in range(n)]
        for cp in copies:
            cp.start()
        for cp in copies:
            cp.wait()

    any_spec = pl.BlockSpec(memory_space=pl.ANY)
    return pl.pallas_call(
        body, name=name, out_shape=[jax.ShapeDtypeStruct(a.shape, a.dtype) for a in arrays],
        in_specs=[any_spec] * n, out_specs=[any_spec] * n,
        scratch_shapes=[pltpu.SemaphoreType.DMA((n,)), pltpu.SemaphoreType.DMA((n,))])(*arrays)


def _as_rows(a):
    return a.reshape(-1, a.shape[-1])


def _row_tile(r, c, budget_bytes=1 << 20):
    tr = r
    while tr % 16 == 0 and tr * c * 4 > budget_bytes:
        tr //= 2
    return tr


def _sum_slots(layers, nlayer, name, into=None):
    _, r, c = layers[0][1].shape
    tr = _row_tile(r, c)
    nt = r // tr
    acc = into
    for l, r4 in layers:
        def kern(r_ref, *rest):
            o_ref = rest[-1]
            o_ref[...] = (((r_ref[0].astype(F32) + r_ref[1].astype(F32)) + r_ref[2].astype(F32))
                          + r_ref[3].astype(F32))

        out_spec = pl.BlockSpec((tr, c), lambda i, l=l: (l * nt + i, 0))
        first = acc is None
        acc = pl.pallas_call(
            kern, name=f"{name}_l{l}", grid=(nt,),
            in_specs=[pl.BlockSpec((4, tr, c), lambda i: (0, i, 0))]
            + ([] if first else [pl.BlockSpec(memory_space=pl.ANY)]),
            out_specs=out_spec, out_shape=jax.ShapeDtypeStruct((nlayer * r, c), F32),
            input_output_aliases={} if first else {1: 0},
            compiler_params=_params("parallel"))(*([r4] if first else [r4, acc]))
    return acc


def _adamw(w, m, v, parts, name):
    r, c = w.shape
    tr = _row_tile(r, c, 3 << 19)
    npart = len(parts)
    c1 = 1.0 - ADAM_B1 ** ADAM_STEP
    c2 = 1.0 - ADAM_B2 ** ADAM_STEP

    def kern(*refs):
        w_ref, m_ref, v_ref = refs[:3]
        p_refs = refs[3:3 + npart]
        g_ref, d_ref, mo_ref, vo_ref = refs[3 + npart:]
        g = p_refs[0][...]
        for p in p_refs[1:]:
            g = g + p[...]
        mn = ADAM_B1 * m_ref[...] + (1.0 - ADAM_B1) * g
        vn = ADAM_B2 * v_ref[...] + (1.0 - ADAM_B2) * (g * g)
        g_ref[...] = g
        mo_ref[...] = mn
        vo_ref[...] = vn
        d_ref[...] = -ADAM_LR * ((mn / c1) / (jnp.sqrt(vn / c2) + ADAM_EPS) + ADAM_WD * w_ref[...])

    blk = pl.BlockSpec((tr, c), lambda i: (i, 0))
    shape = jax.ShapeDtypeStruct((r, c), F32)
    return pl.pallas_call(
        kern, name=name, grid=(r // tr,), in_specs=[blk] * (3 + npart), out_specs=[blk] * 4, out_shape=[shape] * 4,
        compiler_params=_params("parallel"))(w, m, v, *parts)


def _sum_devices(g8, name):
    _, r, c = g8.shape

    def kern(g_ref, o_ref):
        tot = g_ref[0]
        for dev in range(1, 8):
            tot = tot + g_ref[dev]
        o_ref[...] = tot

    return pl.pallas_call(
        kern, name=name, grid=(1,), in_specs=[pl.BlockSpec((8, r, c), lambda i: (0, 0, 0))],
        out_specs=pl.BlockSpec((r, c), lambda i: (0, 0)), out_shape=jax.ShapeDtypeStruct((r, c), F32),
        compiler_params=_params("arbitrary"))(g8)


def _pad_lanes(a, width):
    return jnp.pad(a, [(0, 0)] * (a.ndim - 1) + [(0, width - a.shape[-1])])


def kernel(x, positions, ffn_norm1, ffn1_w1, ffn1_w3, ffn1_w2, mix_norm, ffn_norm2, ffn2_w1, ffn2_w3, ffn2_w2, conv_w_pw1, conv_w_dw, conv_norm, conv_w_pw2, mla_w_a, mla_q_norm, mla_kv_norm, mla_w_uq, mla_w_ukv, mla_w_o, final_norm, loss_target, m_ffn_norm1, m_ffn1_w1, m_ffn1_w3, m_ffn1_w2, m_mix_norm, m_ffn_norm2, m_ffn2_w1, m_ffn2_w3, m_ffn2_w2, m_conv_w_pw1, m_conv_w_dw, m_conv_norm, m_conv_w_pw2, m_mla_w_a, m_mla_q_norm, m_mla_kv_norm, m_mla_w_uq, m_mla_w_ukv, m_mla_w_o, m_final_norm, v_ffn_norm1, v_ffn1_w1, v_ffn1_w3, v_ffn1_w2, v_mix_norm, v_ffn_norm2, v_ffn2_w1, v_ffn2_w3, v_ffn2_w2, v_conv_w_pw1, v_conv_w_dw, v_conv_norm, v_conv_w_pw2, v_mla_w_a, v_mla_q_norm, v_mla_kv_norm, v_mla_w_uq, v_mla_w_ukv, v_mla_w_o, v_final_norm):
    given = locals()
    return _step({nm: given[nm] for nm in INPUTS})


def _step(A):
    x = A['x'][0]
    target = A['loss_target'][0]
    t, d = x.shape
    pos = A['positions'].reshape(t, 1)
    me = 2 * lax.axis_index("x") + lax.axis_index("y")

    flipped = {f'ffn{k}_{w}' for k in (1, 2) for w in ('w1', 'w3')}
    P = {}
    for nm in BIG:
        for key in (nm, 'm_' + nm, 'v_' + nm):
            P[key] = jnp.swapaxes(A[key], 1, 2) if nm in flipped else A[key]

    def unflip(nm, a):
        return jnp.swapaxes(a, 1, 2) if nm in flipped else a

    ffn = [f'ffn{k}_{w}' for k in (1, 2) for w in ('w1', 'w3', 'w2')]
    gather_groups = [[(nm, 0) for nm in ffn[:3]],
                     [('conv_w_pw1', 0), ('conv_w_pw2', 0)] + [(nm, 0) for nm in ffn[3:]],
                     [(nm, 1) for nm in ffn[:3]] + [('mla_w_a', 0), ('mla_w_uq', 0), ('mla_w_ukv', 0), ('mla_w_o', 0)],
                     [(nm, 1) for nm in ffn[3:]]]
    halved = (0, 1)
    gather_flights = {}
    big = {}

    def gather_start(gi, dep):
        shards = [_bf(P[nm][l]) for nm, l in gather_groups[gi]]
        gather_flights[gi], token = _exchange_start(shards, [_landing(s, me) for s in shards], False,
                                                    f"gather_start_{gi}", dep, half=gi in halved)
        return token

    def gather_wait(gi, after):
        landed = _exchange_wait(gather_flights[gi], after, False, f"gather_wait_{gi}", half=gi in halved)
        if gi in halved:
            landed = _fill_halves(landed, f"gather_fill_{gi}")
        big.update(zip(gather_groups[gi], landed))
        return landed[0]

    dw_shard = A['conv_w_dw'][0]
    cw = dw_shard.shape[1]
    small = jnp.concatenate([
        jnp.pad(dw_shard, ((0, CONV_HALO - CONV_WIDTH), (0, 0))),
        jnp.pad(_pad_lanes(A['mla_q_norm'], cw), ((0, 7), (0, 0))),
        jnp.pad(_pad_lanes(A['mla_kv_norm'], cw), ((0, 7), (0, 0)))], axis=0)
    small = _all_gather_rows(small, "gather_small_weights").reshape(4, 2, 48, cw)[:, 0]
    w_dw = jnp.concatenate([small[j, :CONV_HALO] for j in range(4)], axis=1)
    gq = jnp.concatenate([small[j, CONV_HALO, :Q_LORA // 4] for j in range(4)])
    gkv = jnp.concatenate([small[j, CONV_HALO + 8, :KV_LORA // 4] for j in range(4)])

    def rows(nm, layer):
        g = big[nm, layer]
        return g.reshape(-1, g.shape[-1])

    ffn_w = {}

    def ffn_weights(k, l):
        ffn_w[k, l] = (rows(f'ffn{k}_w1', l), rows(f'ffn{k}_w3', l), rows(f'ffn{k}_w2', l))
        return ffn_w[k, l]

    token = gather_start(0, small)
    cs_c, cs_s = _rope_tables(pos)
    h0 = x
    token = gather_start(1, gather_wait(0, token))
    h1, n01, z01a, z01b = _ffn_fwd(h0, A['ffn_norm1'][0], *ffn_weights(1, 0), token, "ffn1_l0_fwd")
    token = gather_start(3, gather_start(2, gather_wait(1, h1)))
    pw1 = big['conv_w_pw1', 0]
    pw1_a = jnp.concatenate([pw1[0], pw1[1]], axis=1)
    pw1_b = jnp.concatenate([pw1[2], pw1[3]], axis=1)
    pw2 = rows('conv_w_pw2', 0)
    m0 = _norm_fwd(h1, A['mix_norm'][0], token, "mix_norm_l0")
    ca, cb, glu = _glu_fwd(m0, pw1_a, pw1_b)
    cv, cs = _conv_fwd(glu, w_dw, A['conv_norm'][0])
    h2 = _mm([(cs, pw2)], F32, "conv_pw2_fwd", res=h1)
    h3, n02, z02a, z02b = _ffn_fwd(h2, A['ffn_norm2'][0], *ffn_weights(2, 0), token, "ffn2_l0_fwd")
    gather_wait(2, h3)
    w_a = _pad_lanes(rows('mla_w_a', 0), A_PAD)
    wuq = _pad_lanes(big['mla_w_uq', 0].reshape(Q_LORA, HEADS, NOPE + ROPE), HEAD_PAD).reshape(Q_LORA, -1)
    wukv = big['mla_w_ukv', 0].reshape(KV_LORA, HEADS * (NOPE + V_HEAD))
    w_o = rows('mla_w_o', 0)
    h4, n11, z11a, z11b = _ffn_fwd(h3, A['ffn_norm1'][1], *ffn_weights(1, 1), token, "ffn1_l1_fwd")
    m1 = _norm_fwd(h4, A['mix_norm'][1], token, "mix_norm_l1")
    a_lat = _mm([(m1, w_a)], F32, "mla_down_fwd")
    cq, ckv, kr = _mla_prep(a_lat, gq, gkv, cs_c, cs_s)
    q, k, v = _mla_qkv(cq, ckv, kr, cs_c, cs_s, wuq, wukv)
    o, lse = _flash_fwd(q, k, v)
    h5 = _mm([(o, w_o)], F32, "mla_out_fwd", res=h4)
    gather_wait(3, h5)
    h6, n12, z12a, z12b = _ffn_fwd(h5, A['ffn_norm2'][1], *ffn_weights(2, 1), token, "ffn2_l1_fwd")

    def row_slots(g):
        return g.reshape(4, g.shape[0] // 4, g.shape[1])

    scatter_flights = []

    def scatter_start(named):
        srcs = [g for _, g in named]
        lands = [_landing(lax.dynamic_index_in_dim(g, me, 0, keepdims=False), me) for g in srcs]
        flight, token = _exchange_start(srcs, lands, True, f"scatter_start_{len(scatter_flights)}")
        scatter_flights.append(([key for key, _ in named], flight))
        return token

    def send_ffn(k, l, dw1t, dw3t, dw2):
        return scatter_start([((f'ffn{k}_w1', l), row_slots(dw1t)), ((f'ffn{k}_w3', l), row_slots(dw3t)),
                              ((f'ffn{k}_w2', l), row_slots(dw2))])

    dh6, dg_final, loss_part = _loss_bwd(h6, target, A['final_norm'])
    dh5, dg_n2_l1, *dws = _ffn_bwd(dh6, h5, A['ffn_norm2'][1], n12, z12a, z12b, *ffn_w[2, 1], loss_part, "ffn2_l1")
    token = send_ffn(2, 1, *dws)

    do = _mm([(dh5, w_o)], BF16, "mla_out_bwd", trans_b=True, dep=token)
    dw_o = _mm_tn(o, dh5, BF16, "mla_dw_o")
    delta = _attn_delta(do, o)
    dq, dk, dv = _flash_bwd(q, k, v, do, lse, delta)
    dr, dkv, dcq, dckv, dar = _mla_qkv_bwd(dq, dk, dv, cs_c, cs_s, wuq, wukv)
    dwuq = _mm_tn(cq, dr, BF16, "mla_dw_uq", bn=dr.shape[1] // 2)
    dwukv = _mm_tn(ckv, dkv, BF16, "mla_dw_ukv", bn=dkv.shape[1] // 2)
    da_lat, dgq, dgkv = _mla_prep_bwd(a_lat, dcq, dckv, dar, gq, gkv)
    dw_a = _mm_tn(m1, da_lat, BF16, "mla_dw_a")
    token = scatter_start([
        (('mla_w_a', 0), row_slots(dw_a[:, :Q_LORA + KV_LORA + ROPE])),
        (('mla_w_uq', 0), dwuq.reshape(4, Q_LORA // 4, HEADS, HEAD_PAD)[..., :NOPE + ROPE]),
        (('mla_w_ukv', 0), dwukv.reshape(4, KV_LORA // 4, HEADS, NOPE + V_HEAD)),
        (('mla_w_o', 0), row_slots(dw_o))])
    dh4, dg_mix_l1 = _mm_normbwd([(da_lat, w_a)], h4, A['mix_norm'][1], dh5, token, "mla_down_bwd")

    dh3, dg_n1_l1, *dws = _ffn_bwd(dh4, h3, A['ffn_norm1'][1], n11, z11a, z11b, *ffn_w[1, 1], token, "ffn1_l1")
    token = send_ffn(1, 1, *dws)
    dh2, dg_n2_l0, *dws = _ffn_bwd(dh3, h2, A['ffn_norm2'][0], n02, z02a, z02b, *ffn_w[2, 0], token, "ffn2_l0")
    token = send_ffn(2, 0, *dws)

    dcv, dg_conv = _conv_bwd_norm(dh2, cv, pw2, A['conv_norm'][0], token)
    dw_pw2 = _mm_tn(cs, dh2, BF16, "conv_dw_pw2")
    dca, dcb, ddw = _conv_bwd_dw(dcv, glu, ca, cb, w_dw)
    dpw1_a = _mm_tn(m0, dca, BF16, "conv_dw_pw1a")
    dpw1_b = _mm_tn(m0, dcb, BF16, "conv_dw_pw1b")
    half = dpw1_a.shape[1] // 2
    token = scatter_start([
        (('conv_w_pw1', 0), jnp.stack([dpw1_a[:, :half], dpw1_a[:, half:], dpw1_b[:, :half], dpw1_b[:, half:]])),
        (('conv_w_pw2', 0), row_slots(dw_pw2))])
    dh1, dg_mix_l0 = _mm_normbwd([(dca, pw1_a), (dcb, pw1_b)], h1, A['mix_norm'][0], dh2, token, "conv_pw1_bwd")

    dx, dg_n1_l0, *dws = _ffn_bwd(dh1, h0, A['ffn_norm1'][0], n01, z01a, z01b, *ffn_w[1, 0], token, "ffn1_l0")
    last_sent = send_ffn(1, 0, *dws)
    out = {}

    qkv_row = jnp.concatenate([dgq, dgkv, jnp.zeros((8, d - Q_LORA - KV_LORA), F32)], axis=1)
    loss_row = _pad_lanes(loss_part, d)
    small_g = jnp.concatenate([dg_n1_l0, dg_n1_l1, dg_mix_l0, dg_mix_l1, dg_n2_l0, dg_n2_l1, dg_conv, dg_final,
                               qkv_row, loss_row, ddw], axis=0)
    nrow = small_g.shape[0]
    tot = _sum_devices(_all_gather_rows(small_g, "gather_small_grads").reshape(8, nrow, d), "sum_small_grads")
    loss = tot[72, 0]
    q_shard = lax.dynamic_slice_in_dim(tot[64, :Q_LORA], me * (Q_LORA // 4), Q_LORA // 4)
    kv_shard = lax.dynamic_slice_in_dim(tot[64, Q_LORA:Q_LORA + KV_LORA], me * (KV_LORA // 4), KV_LORA // 4)
    dw_shard_g = lax.dynamic_slice_in_dim(tot[80:80 + CONV_WIDTH], me * cw, cw, axis=1)
    small_grads = {
        'ffn_norm1': jnp.stack([tot[0], tot[8]]), 'mix_norm': jnp.stack([tot[16], tot[24]]),
        'ffn_norm2': jnp.stack([tot[32], tot[40]]), 'conv_norm': tot[48][None], 'final_norm': tot[56],
        'mla_q_norm': q_shard[None], 'mla_kv_norm': kv_shard[None], 'conv_w_dw': dw_shard_g[None],
    }
    for nm, g in small_grads.items():
        res = _adamw(_as_rows(A[nm]) if A[nm].ndim > 1 else A[nm].reshape(1, -1),
                     A['m_' + nm].reshape(-1, A[nm].shape[-1]), A['v_' + nm].reshape(-1, A[nm].shape[-1]),
                     [g.reshape(-1, A[nm].shape[-1])], "adamw_" + nm)
        out[nm] = [r.reshape(A[nm].shape) for r in res]

    received = {}
    after = last_sent

    def scatter_wait(si, after):
        keys, flight = scatter_flights[si]
        landed = _exchange_wait(flight, after, True, f"scatter_wait_{si}")
        received.update(zip(keys, landed))
        return landed[0]

    def slots(nm, l):
        return received[nm, l].reshape(4, -1, received[nm, l].shape[-1])

    def finish(names, sums, tag):
        for nm, mine, theirs in zip(names, sums, _swap_with_sibling(sums, "swap_with_sibling_" + tag)):
            res = _adamw(_as_rows(P[nm]), _as_rows(P['m_' + nm]), _as_rows(P['v_' + nm]), [mine, theirs],
                         "adamw_" + nm)
            out[nm] = [unflip(nm, r.reshape(P[nm].shape)) for r in res]
        return res[1]

    last = len(scatter_flights) - 1
    for si in range(last):
        after = scatter_wait(si, after)
    late = ffn[:3]
    early = [nm for nm in BIG if nm not in late]
    late_l1 = [_sum_slots([(1, slots(nm, 1))], 2, "sum_" + nm) for nm in late]
    after = finish(early, [_sum_slots([(l, slots(nm, l)) for l in range(A[nm].shape[0])], A[nm].shape[0],
                                      "sum_" + nm) for nm in early], "early")
    scatter_wait(last, after)
    finish(late, [_sum_slots([(0, slots(nm, 0))], 2, "sum_" + nm, into=part) for nm, part in zip(late, late_l1)],
           "late")

    return (loss, dx[None], *[out[nm][0] for nm in WEIGHTS], *[out[nm][1] for nm in WEIGHTS],
            *[out[nm][2] for nm in WEIGHTS], *[out[nm][3] for nm in WEIGHTS])
```

```python
import functools

import jax
import jax.numpy as jnp
import numpy as np
from jax import lax
from jax.experimental import pallas as pl
from jax.experimental.pallas import tpu as pltpu

F32 = jnp.float32
BF16 = jnp.bfloat16
MESH = pl.DeviceIdType.MESH

RMS_EPS = 1e-6
HEADS = 8
NOPE = 128
ROPE = 64
HEAD_PAD = 256
V_HEAD = 128
Q_LORA = 512
KV_LORA = 256
A_PAD = 896
CHUNK = 64
CONV_WIDTH = 31
CONV_HALO = 32
CONV_ROWS = 16
ROPE_THETA = 10000.0
ATTN_SCALE = (NOPE + ROPE) ** -0.5
FFN_RES = 0.5

ADAM_LR = 0.001
ADAM_B1 = 0.9
ADAM_B2 = 0.999
ADAM_EPS = 1e-08
ADAM_WD = 0.01
ADAM_STEP = 10

VMEM_LIMIT_BYTES = 56 * 1024 * 1024

WEIGHTS = ['ffn_norm1', 'ffn1_w1', 'ffn1_w3', 'ffn1_w2', 'mix_norm', 'ffn_norm2', 'ffn2_w1', 'ffn2_w3', 'ffn2_w2',
           'conv_w_pw1', 'conv_w_dw', 'conv_norm', 'conv_w_pw2', 'mla_w_a', 'mla_q_norm', 'mla_kv_norm', 'mla_w_uq',
           'mla_w_ukv', 'mla_w_o', 'final_norm']
INPUTS = (['x', 'positions'] + WEIGHTS + ['loss_target'] + ['m_' + w for w in WEIGHTS] + ['v_' + w for w in WEIGHTS])
BIG = ['ffn1_w1', 'ffn1_w3', 'ffn1_w2', 'ffn2_w1', 'ffn2_w3', 'ffn2_w2', 'conv_w_pw1', 'conv_w_pw2', 'mla_w_a',
       'mla_w_uq', 'mla_w_ukv', 'mla_w_o']


def _params(*sem):
    return pltpu.CompilerParams(dimension_semantics=sem, vmem_limit_bytes=VMEM_LIMIT_BYTES)


def _bf(v):
    return v.astype(BF16)


def _rstd(x):
    return lax.rsqrt(jnp.mean(x * x, axis=-1, keepdims=True) + RMS_EPS)


def _sigmoid(x):
    return jax.nn.sigmoid(x)


def _rot(x):
    lane = lax.broadcasted_iota(jnp.int32, x.shape, 1)
    return jnp.where(lane < ROPE // 2, -pltpu.roll(x, 128 - ROPE // 2, 1), pltpu.roll(x, ROPE // 2, 1))


def _rot_t(y):
    lane = lax.broadcasted_iota(jnp.int32, y.shape, 1)
    return jnp.where(lane < ROPE // 2, pltpu.roll(y, 128 - ROPE // 2, 1), -pltpu.roll(y, ROPE // 2, 1))


def _pair_sum(a_refs, b_refs, trans_b):
    tot = None
    for a_r, b_r in zip(a_refs, b_refs):
        a, b = _bf(a_r[...]), _bf(b_r[...])
        if trans_b:
            d = lax.dot_general(a, b, (((1,), (1,)), ((), ())), preferred_element_type=F32)
        else:
            d = jnp.dot(a, b, preferred_element_type=F32)
        tot = d if tot is None else tot + d
    return tot


def _mm(pairs, out_dtype, name, *, trans_b=False, tm=512, tn=None, tk=None, res=None, dep=None):
    m, k = pairs[0][0].shape
    n = pairs[0][1].shape[0] if trans_b else pairs[0][1].shape[1]
    tm, tn, tk = min(tm, m), tn or n, tk or k
    nk, npair = k // tk, len(pairs)

    def kern(*refs):
        a_refs, b_refs = refs[:npair], refs[npair:2 * npair]
        rest = list(refs[2 * npair:])
        res_ref = rest.pop(0) if res is not None else None
        if dep is not None:
            rest.pop(0)
        o_ref = rest.pop(0)

        def finish(acc):
            if res_ref is not None:
                acc = res_ref[...] + acc
            o_ref[...] = acc.astype(o_ref.dtype)

        if nk == 1:
            finish(_pair_sum(a_refs, b_refs, trans_b))
        else:
            acc_ref = rest.pop(0)
            kk = pl.program_id(2)

            @pl.when(kk == 0)
            def _():
                acc_ref[...] = jnp.zeros_like(acc_ref)

            acc_ref[...] += _pair_sum(a_refs, b_refs, trans_b)

            @pl.when(kk == nk - 1)
            def _():
                finish(acc_ref[...])

    a_spec = pl.BlockSpec((tm, tk), lambda i, j, kk: (i, kk))
    b_spec = (pl.BlockSpec((tn, tk), lambda i, j, kk: (j, kk)) if trans_b
              else pl.BlockSpec((tk, tn), lambda i, j, kk: (kk, j)))
    io_spec = pl.BlockSpec((tm, tn), lambda i, j, kk: (i, j))
    in_specs = ([a_spec] * npair + [b_spec] * npair + ([io_spec] if res is not None else [])
                + ([pl.BlockSpec((8, 128), lambda i, j, kk: (0, 0))] if dep is not None else []))
    args = ([p[0] for p in pairs] + [p[1] for p in pairs] + ([res] if res is not None else [])
            + ([dep] if dep is not None else []))
    return pl.pallas_call(
        kern, name=name, grid=(m // tm, n // tn, nk), in_specs=in_specs, out_specs=io_spec,
        out_shape=jax.ShapeDtypeStruct((m, n), out_dtype),
        scratch_shapes=[pltpu.VMEM((tm, tn), F32)] if nk > 1 else [],
        compiler_params=_params("parallel", "parallel", "arbitrary"))(*args)


def _mm_normbwd(pairs, h, g, dres, dep, name, *, tm=512, tk=None):
    m, k = pairs[0][0].shape
    d = pairs[0][1].shape[0]
    tm, tk = min(tm, m), tk or k
    nk, npair = k // tk, len(pairs)

    def kern(*refs):
        a_refs, b_refs = refs[:npair], refs[npair:2 * npair]
        h_ref, g_ref, dres_ref, _, o_ref, dg_ref, acc_ref = refs[2 * npair:]
        i, kk = pl.program_id(0), pl.program_id(1)

        @pl.when(jnp.logical_and(i == 0, kk == 0))
        def _():
            dg_ref[...] = jnp.zeros_like(dg_ref)

        @pl.when(kk == 0)
        def _():
            acc_ref[...] = jnp.zeros_like(acc_ref)

        acc_ref[...] += _pair_sum(a_refs, b_refs, True)

        @pl.when(kk == nk - 1)
        def _():
            dn = acc_ref[...]
            x = h_ref[...]
            rstd = _rstd(x)
            xhat = x * rstd
            dg_ref[...] += jnp.broadcast_to(jnp.sum(dn * xhat, axis=0, keepdims=True), dg_ref.shape)
            dxh = dn * g_ref[...]
            dx = rstd * (dxh - xhat * jnp.mean(dxh * xhat, axis=-1, keepdims=True))
            o_ref[...] = dres_ref[...] + dx

    row = pl.BlockSpec((tm, d), lambda i, kk: (i, 0))
    in_specs = ([pl.BlockSpec((tm, tk), lambda i, kk: (i, kk))] * npair
                + [pl.BlockSpec((d, tk), lambda i, kk: (0, kk))] * npair
                + [row, pl.BlockSpec((1, d), lambda i, kk: (0, 0)), row, pl.BlockSpec((8, 128), lambda i, kk: (0, 0))])
    return pl.pallas_call(
        kern, name=name, grid=(m // tm, nk), in_specs=in_specs,
        out_specs=[row, pl.BlockSpec((8, d), lambda i, kk: (0, 0))],
        out_shape=[jax.ShapeDtypeStruct((m, d), F32), jax.ShapeDtypeStruct((8, d), F32)],
        scratch_shapes=[pltpu.VMEM((tm, d), F32)],
        compiler_params=_params("arbitrary", "arbitrary"))(
            *[p[0] for p in pairs], *[p[1] for p in pairs], h, g.reshape(1, d), dres, dep)


def _mm_tn(a, b, out_dtype, name, *, bm=None, bn=None, tk=1024):
    t, m = a.shape
    batched = b.ndim == 3
    n = b.shape[-1]
    nb = b.shape[0] if batched else 1
    bm, bn, tk = bm or m, bn or n, min(tk, t)
    nk = t // tk

    def kern(a_ref, b_ref, o_ref, acc_ref):
        kk = pl.program_id(3)

        @pl.when(kk == 0)
        def _():
            acc_ref[...] = jnp.zeros_like(acc_ref)

        acc_ref[...] += lax.dot_general(_bf(a_ref[...]), _bf(b_ref[...]), (((0,), (0,)), ((), ())),
                                        preferred_element_type=F32)

        @pl.when(kk == nk - 1)
        def _():
            o_ref[...] = acc_ref[...].astype(o_ref.dtype)

    a_spec = pl.BlockSpec((tk, bm), lambda h, i, j, kk: (kk, i))
    if batched:
        b_spec = pl.BlockSpec((None, tk, bn), lambda h, i, j, kk: (h, kk, j))
        o_spec = pl.BlockSpec((None, bm, bn), lambda h, i, j, kk: (h, i, j))
        out_shape = jax.ShapeDtypeStruct((nb, m, n), out_dtype)
    else:
        b_spec = pl.BlockSpec((tk, bn), lambda h, i, j, kk: (kk, j))
        o_spec = pl.BlockSpec((bm, bn), lambda h, i, j, kk: (i, j))
        out_shape = jax.ShapeDtypeStruct((m, n), out_dtype)
    return pl.pallas_call(
        kern, name=name, grid=(nb, m // bm, n // bn, nk), in_specs=[a_spec, b_spec], out_specs=o_spec,
        out_shape=out_shape, scratch_shapes=[pltpu.VMEM((bm, bn), F32)],
        compiler_params=_params("parallel", "parallel", "parallel", "arbitrary"))(a, b)


def _ffn_tile(f):
    return f // 2 if (f // 2) % 128 == 0 else f


def _ffn_fwd(h, g, w1t, w3t, w2, dep, name):
    t, d = h.shape
    f = w1t.shape[0]
    tm = min(256, t)
    nt = (((1,), (1,)), ((), ()))

    def kern(h_ref, g_ref, w1_hbm, w3_hbm, w2_hbm, dep_ref, ho_ref, n_ref, z1_ref, z3_ref,
             w1_ref, w3_ref, w2_ref, sems):
        @pl.when(pl.program_id(0) == 0)
        def _():
            copies = [pltpu.make_async_copy(src, dst, sems.at[k]) for k, (src, dst) in
                      enumerate(((w1_hbm, w1_ref), (w3_hbm, w3_ref), (w2_hbm, w2_ref)))]
            for cp in copies:
                cp.start()
            for cp in copies:
                cp.wait()

        x = h_ref[...]
        n = _bf(x * _rstd(x) * g_ref[...])
        n_ref[...] = n
        z1 = lax.dot_general(n, w1_ref[...], nt, preferred_element_type=F32)
        z3 = lax.dot_general(n, w3_ref[...], nt, preferred_element_type=F32)
        z1_ref[...] = _bf(z1)
        z3_ref[...] = _bf(z3)
        act = _bf(z1 * _sigmoid(z1) * z3)
        ho_ref[...] = x + FFN_RES * jnp.dot(act, w2_ref[...], preferred_element_type=F32)

    row = pl.BlockSpec((tm, d), lambda i: (i, 0))
    col = pl.BlockSpec((tm, f), lambda i: (i, 0))
    whole = pl.BlockSpec(memory_space=pl.ANY)
    return pl.pallas_call(
        kern, name=name, grid=(t // tm,),
        in_specs=[row, pl.BlockSpec((1, d), lambda i: (0, 0)), whole, whole, whole,
                  pl.BlockSpec((8, 128), lambda i: (0, 0))],
        out_specs=[row, row, col, col],
        out_shape=[jax.ShapeDtypeStruct((t, d), F32), jax.ShapeDtypeStruct((t, d), BF16),
                   jax.ShapeDtypeStruct((t, f), BF16), jax.ShapeDtypeStruct((t, f), BF16)],
        scratch_shapes=[pltpu.VMEM((f, d), BF16), pltpu.VMEM((f, d), BF16), pltpu.VMEM((f, d), BF16),
                        pltpu.SemaphoreType.DMA((3,))],
        compiler_params=_params("arbitrary"))(h, g.reshape(1, d), w1t, w3t, w2, dep)


def _ffn_bwd_x(dh, h_in, g, z1, z3, w1t, w3t, w2, dep, name):
    t, d = dh.shape
    f = z1.shape[1]
    tm = min(256, t)

    def kern(dh_ref, h_ref, g_ref, z1_ref, z3_ref, w2_hbm, w1_hbm, w3_hbm, dep_ref,
             o_ref, dg_ref, dz1_ref, dz3_ref, a_ref, df_ref, w2_ref, w1_ref, w3_ref, sems):
        @pl.when(pl.program_id(0) == 0)
        def _():
            copies = [pltpu.make_async_copy(src, dst, sems.at[k]) for k, (src, dst) in
                      enumerate(((w2_hbm, w2_ref), (w1_hbm, w1_ref), (w3_hbm, w3_ref)))]
            for cp in copies:
                cp.start()
            dg_ref[...] = jnp.zeros_like(dg_ref)
            for cp in copies:
                cp.wait()

        df = _bf(FFN_RES * dh_ref[...])
        df_ref[...] = df
        da = lax.dot_general(df, w2_ref[...], (((1,), (1,)), ((), ())), preferred_element_type=F32)
        z1v, z3v = z1_ref[...].astype(F32), z3_ref[...].astype(F32)
        sig = _sigmoid(z1v)
        silu = z1v * sig
        a_ref[...] = _bf(silu * z3v)
        dz1 = _bf(da * z3v * (sig * (1.0 + z1v * (1.0 - sig))))
        dz3 = _bf(da * silu)
        dz1_ref[...] = dz1
        dz3_ref[...] = dz3
        dn = (jnp.dot(dz1, w1_ref[...], preferred_element_type=F32)
              + jnp.dot(dz3, w3_ref[...], preferred_element_type=F32))
        x = h_ref[...]
        rstd = _rstd(x)
        xhat = x * rstd
        dg_ref[...] += jnp.broadcast_to(jnp.sum(dn * xhat, axis=0, keepdims=True), dg_ref.shape)
        dxh = dn * g_ref[...]
        o_ref[...] = dh_ref[...] + rstd * (dxh - xhat * jnp.mean(dxh * xhat, axis=-1, keepdims=True))

    row = pl.BlockSpec((tm, d), lambda i: (i, 0))
    col = pl.BlockSpec((tm, f), lambda i: (i, 0))
    whole = pl.BlockSpec(memory_space=pl.ANY)
    colshape = jax.ShapeDtypeStruct((t, f), BF16)
    return pl.pallas_call(
        kern, name=name, grid=(t // tm,),
        in_specs=[row, row, pl.BlockSpec((1, d), lambda i: (0, 0)), col, col, whole, whole, whole,
                  pl.BlockSpec((8, 128), lambda i: (0, 0))],
        out_specs=[row, pl.BlockSpec((8, d), lambda i: (0, 0)), col, col, col, row],
        out_shape=[jax.ShapeDtypeStruct((t, d), F32), jax.ShapeDtypeStruct((8, d), F32), colshape, colshape, colshape,
                   jax.ShapeDtypeStruct((t, d), BF16)],
        scratch_shapes=[pltpu.VMEM((f, d), BF16), pltpu.VMEM((f, d), BF16), pltpu.VMEM((f, d), BF16),
                        pltpu.SemaphoreType.DMA((3,))],
        compiler_params=_params("arbitrary"))(dh, h_in, g.reshape(1, d), z1, z3, w2, w1t, w3t, dep)


def _ffn_bwd(dh, h_in, g, n, z1, z3, w1t, w3t, w2, dep, tag):
    f = w2.shape[0]
    dh_in, dg, dz1, dz3, act, df = _ffn_bwd_x(dh, h_in, g, z1, z3, w1t, w3t, w2, dep, tag + "_bwd_x")
    dw1t = _mm_tn(dz1, n, BF16, tag + "_dw1", bm=_ffn_tile(f), tk=2048)
    dw3t = _mm_tn(dz3, n, BF16, tag + "_dw3", bm=_ffn_tile(f), tk=2048)
    dw2 = _mm_tn(act, df, BF16, tag + "_dw2", bm=_ffn_tile(f), tk=2048)
    return dh_in, dg, dw1t, dw3t, dw2


def _norm_fwd(h, g, dep, name):
    t, d = h.shape
    tm = min(512, t)

    def kern(h_ref, g_ref, dep_ref, o_ref):
        x = h_ref[...]
        o_ref[...] = _bf(x * _rstd(x) * g_ref[...])

    row = pl.BlockSpec((tm, d), lambda i: (i, 0))
    return pl.pallas_call(
        kern, name=name, grid=(t // tm,),
        in_specs=[row, pl.BlockSpec((1, d), lambda i: (0, 0)), pl.BlockSpec((8, 128), lambda i: (0, 0))],
        out_specs=row, out_shape=jax.ShapeDtypeStruct((t, d), BF16),
        compiler_params=_params("parallel"))(h, g.reshape(1, d), dep)


def _loss_bwd(h, target, g):
    t, d = h.shape
    tm = min(512, t)

    def kern(h_ref, t_ref, g_ref, dh_ref, dg_ref, loss_ref):
        @pl.when(pl.program_id(0) == 0)
        def _():
            dg_ref[...] = jnp.zeros_like(dg_ref)
            loss_ref[...] = jnp.zeros_like(loss_ref)

        x = h_ref[...]
        rstd = _rstd(x)
        xhat = x * rstd
        err = xhat * g_ref[...] - t_ref[...]
        row_loss = jnp.sum(err * err, axis=-1, keepdims=True) * (0.5 / d)
        loss_ref[...] += jnp.broadcast_to(jnp.sum(row_loss, axis=0, keepdims=True), loss_ref.shape)
        dy = err * (1.0 / d)
        dg_ref[...] += jnp.broadcast_to(jnp.sum(dy * xhat, axis=0, keepdims=True), dg_ref.shape)
        dxh = dy * g_ref[...]
        dh_ref[...] = rstd * (dxh - xhat * jnp.mean(dxh * xhat, axis=-1, keepdims=True))

    row = pl.BlockSpec((tm, d), lambda i: (i, 0))
    return pl.pallas_call(
        kern, name="loss_bwd", grid=(t // tm,),
        in_specs=[row, row, pl.BlockSpec((1, d), lambda i: (0, 0))],
        out_specs=[row, pl.BlockSpec((8, d), lambda i: (0, 0)), pl.BlockSpec((8, 128), lambda i: (0, 0))],
        out_shape=[jax.ShapeDtypeStruct((t, d), F32), jax.ShapeDtypeStruct((8, d), F32),
                   jax.ShapeDtypeStruct((8, 128), F32)],
        compiler_params=_params("arbitrary"))(h, target, g.reshape(1, d))


def _glu_fwd(m, wa, wb):
    t, d = m.shape
    c = wa.shape[1]
    tm, tc = min(512, t), c

    def kern(m_ref, wa_ref, wb_ref, a_ref, b_ref, glu_ref):
        mv = m_ref[...]
        a = jnp.dot(mv, wa_ref[...], preferred_element_type=F32)
        b = jnp.dot(mv, wb_ref[...], preferred_element_type=F32)
        a_ref[...] = _bf(a)
        b_ref[...] = _bf(b)
        glu_ref[...] = _bf(a * _sigmoid(b))

    col = pl.BlockSpec((tm, tc), lambda i, j: (i, j))
    wspec = pl.BlockSpec((d, tc), lambda i, j: (0, j))
    shape = jax.ShapeDtypeStruct((t, c), BF16)
    return pl.pallas_call(
        kern, name="conv_glu_fwd", grid=(t // tm, c // tc),
        in_specs=[pl.BlockSpec((tm, d), lambda i, j: (i, 0)), wspec, wspec], out_specs=[col, col, col],
        out_shape=[shape, shape, shape], compiler_params=_params("parallel", "parallel"))(m, wa, wb)


def _conv_tile(t):
    return min(256, t)


def _shift_copies(ext, shifted, rows):
    for s in range(8):
        shifted[s] = ext[pl.ds(s, rows), :]


def _shifted_rows(shifted, start, nrows):
    return shifted[start % 8, pl.ds(start - start % 8, nrows), :]


def _conv_fwd(glu, w_dw, g):
    t, c = glu.shape
    tm = _conv_tile(t)
    hb = tm // CONV_HALO

    def kern(cur_ref, halo_ref, w_ref, g_ref, cv_ref, s_ref, ext, shifted):
        i = pl.program_id(0)
        ext[0:CONV_HALO, :] = jnp.where(i > 0, halo_ref[...].astype(F32), 0.0)
        ext[CONV_HALO:tm + CONV_HALO, :] = cur_ref[...].astype(F32)
        ext[tm + CONV_HALO:, :] = jnp.zeros((8, c), F32)
        _shift_copies(ext, shifted, tm + CONV_HALO)
        gv = g_ref[...]
        for r0 in range(0, tm, CONV_ROWS):
            acc = jnp.zeros((CONV_ROWS, c), F32)
            for k in range(CONV_WIDTH):
                acc = acc + _shifted_rows(shifted, r0 + 2 + k, CONV_ROWS) * w_ref[k:k + 1, :]
            cv_ref[r0:r0 + CONV_ROWS, :] = acc
            rn = acc * _rstd(acc) * gv
            s_ref[r0:r0 + CONV_ROWS, :] = _bf(rn * _sigmoid(rn))

    row = pl.BlockSpec((tm, c), lambda i: (i, 0))
    return pl.pallas_call(
        kern, name="conv_fwd", grid=(t // tm,),
        in_specs=[row, pl.BlockSpec((CONV_HALO, c), lambda i: (jnp.maximum(i * hb - 1, 0), 0)),
                  pl.BlockSpec((CONV_HALO, c), lambda i: (0, 0)), pl.BlockSpec((1, c), lambda i: (0, 0))],
        out_specs=[row, row],
        out_shape=[jax.ShapeDtypeStruct((t, c), F32), jax.ShapeDtypeStruct((t, c), BF16)],
        scratch_shapes=[pltpu.VMEM((tm + CONV_HALO + 8, c), F32), pltpu.VMEM((8, tm + CONV_HALO, c), F32)],
        compiler_params=_params("parallel"))(glu, glu, w_dw, g.reshape(1, c))


def _conv_bwd_norm(dh, cv, w_pw2, g, dep):
    t, c = cv.shape
    tm = min(512, t)

    def kern(dh_ref, cv_ref, w_ref, g_ref, dep_ref, dcv_ref, dg_ref):
        @pl.when(pl.program_id(0) == 0)
        def _():
            dg_ref[...] = jnp.zeros_like(dg_ref)

        ds = lax.dot_general(_bf(dh_ref[...]), w_ref[...], (((1,), (1,)), ((), ())), preferred_element_type=F32)
        x = cv_ref[...]
        rstd = _rstd(x)
        xhat = x * rstd
        rn = xhat * g_ref[...]
        sig = _sigmoid(rn)
        drn = ds * (sig * (1.0 + rn * (1.0 - sig)))
        dg_ref[...] += jnp.broadcast_to(jnp.sum(drn * xhat, axis=0, keepdims=True), dg_ref.shape)
        dxh = drn * g_ref[...]
        dcv_ref[...] = rstd * (dxh - xhat * jnp.mean(dxh * xhat, axis=-1, keepdims=True))

    row = pl.BlockSpec((tm, c), lambda i: (i, 0))
    return pl.pallas_call(
        kern, name="conv_bwd_norm", grid=(t // tm,),
        in_specs=[pl.BlockSpec((tm, dh.shape[1]), lambda i: (i, 0)), row,
                  pl.BlockSpec(w_pw2.shape, lambda i: (0, 0)), pl.BlockSpec((1, c), lambda i: (0, 0)),
                  pl.BlockSpec((8, 128), lambda i: (0, 0))],
        out_specs=[row, pl.BlockSpec((8, c), lambda i: (0, 0))],
        out_shape=[jax.ShapeDtypeStruct((t, c), F32), jax.ShapeDtypeStruct((8, c), F32)],
        compiler_params=_params("arbitrary"))(dh, cv, w_pw2, g.reshape(1, c), dep)


def _conv_bwd_dw(dcv, glu, a, b, w_dw):
    t, c = dcv.shape
    tm = _conv_tile(t)
    hb = tm // CONV_HALO
    last = t // CONV_HALO - 1

    def kern(dcv_ref, dnext_ref, glu_ref, gprev_ref, a_ref, b_ref, w_ref, da_ref, db_ref, dw_ref,
             dext, gext, dshift, gshift):
        i = pl.program_id(0)

        @pl.when(i == 0)
        def _():
            dw_ref[...] = jnp.zeros_like(dw_ref)

        dext[0:tm, :] = dcv_ref[...]
        dext[tm:tm + CONV_HALO, :] = jnp.where(i < t // tm - 1, dnext_ref[...], 0.0)
        dext[tm + CONV_HALO:, :] = jnp.zeros((8, c), F32)
        gext[0:CONV_HALO, :] = jnp.where(i > 0, gprev_ref[...].astype(F32), 0.0)
        gext[CONV_HALO:tm + CONV_HALO, :] = glu_ref[...].astype(F32)
        gext[tm + CONV_HALO:, :] = jnp.zeros((8, c), F32)
        _shift_copies(dext, dshift, tm + CONV_HALO)
        _shift_copies(gext, gshift, tm + CONV_HALO)
        for r0 in range(0, tm, CONV_ROWS):
            acc = jnp.zeros((CONV_ROWS, c), F32)
            for k in range(CONV_WIDTH):
                acc = acc + _shifted_rows(dshift, r0 + CONV_WIDTH - 1 - k, CONV_ROWS) * w_ref[k:k + 1, :]
            av = a_ref[r0:r0 + CONV_ROWS, :].astype(F32)
            sig = _sigmoid(b_ref[r0:r0 + CONV_ROWS, :].astype(F32))
            da_ref[r0:r0 + CONV_ROWS, :] = _bf(acc * sig)
            db_ref[r0:r0 + CONV_ROWS, :] = _bf(acc * av * sig * (1.0 - sig))
        for k in range(CONV_WIDTH):
            acc = jnp.zeros((CONV_ROWS, c), F32)
            for r0 in range(0, tm, CONV_ROWS):
                acc = acc + _shifted_rows(gshift, r0 + 2 + k, CONV_ROWS) * dext[r0:r0 + CONV_ROWS, :]
            dw_ref[k:k + 1, :] += jnp.sum(acc, axis=0, keepdims=True)

    row = pl.BlockSpec((tm, c), lambda i: (i, 0))
    shape = jax.ShapeDtypeStruct((t, c), BF16)
    return pl.pallas_call(
        kern, name="conv_bwd_dw", grid=(t // tm,),
        in_specs=[row, pl.BlockSpec((CONV_HALO, c), lambda i: (jnp.minimum((i + 1) * hb, last), 0)),
                  row, pl.BlockSpec((CONV_HALO, c), lambda i: (jnp.maximum(i * hb - 1, 0), 0)),
                  row, row, pl.BlockSpec((CONV_HALO, c), lambda i: (0, 0))],
        out_specs=[row, row, pl.BlockSpec((CONV_HALO, c), lambda i: (0, 0))],
        out_shape=[shape, shape, jax.ShapeDtypeStruct((CONV_HALO, c), F32)],
        scratch_shapes=[pltpu.VMEM((tm + CONV_HALO + 8, c), F32), pltpu.VMEM((tm + CONV_HALO + 8, c), F32),
                        pltpu.VMEM((8, tm + CONV_HALO, c), F32), pltpu.VMEM((8, tm + CONV_HALO, c), F32)],
        compiler_params=_params("arbitrary"))(dcv, dcv, glu, glu, a, b, w_dw)


def _rope_tables(pos):
    t = pos.shape[0]
    tm = min(512, t)
    freq = (np.float32(ROPE_THETA) ** (np.float32(-2.0) * np.arange(ROPE // 2, dtype=np.float32)
                                       / np.float32(ROPE))).astype(np.float32)
    row = np.zeros((2, 128), np.float32)
    row[0, :ROPE] = np.concatenate([freq, freq])
    row[1, :ROPE] = 1.0

    def kern(pos_ref, f_ref, c_ref, s_ref):
        ang = pos_ref[...].astype(F32) * f_ref[0:1, :]
        mask = f_ref[1:2, :]
        c_ref[...] = jnp.cos(ang) * mask
        s_ref[...] = jnp.sin(ang) * mask

    out = pl.BlockSpec((tm, 128), lambda i: (i, 0))
    shape = jax.ShapeDtypeStruct((t, 128), F32)
    return pl.pallas_call(
        kern, name="rope_tables", grid=(t // tm,),
        in_specs=[pl.BlockSpec((tm, 1), lambda i: (i, 0)), pl.BlockSpec((2, 128), lambda i: (0, 0))],
        out_specs=[out, out], out_shape=[shape, shape], compiler_params=_params("parallel"))(pos, jnp.asarray(row))


def _mla_prep(a, gq, gkv, cs_c, cs_s):
    t = a.shape[0]
    tm = min(512, t)
    kv0, r0 = Q_LORA, Q_LORA + KV_LORA

    def kern(a_ref, gq_ref, gkv_ref, c_ref, s_ref, cq_ref, ckv_ref, kr_ref):
        aq = a_ref[:, 0:kv0]
        akv = a_ref[:, kv0:r0]
        ar = a_ref[:, r0:A_PAD]
        cq_ref[...] = _bf(aq * _rstd(aq) * gq_ref[...])
        ckv_ref[...] = _bf(akv * _rstd(akv) * gkv_ref[...])
        kr_ref[...] = _bf(ar * c_ref[...] + _rot(ar) * s_ref[...])

    def row(w):
        return pl.BlockSpec((tm, w), lambda i: (i, 0))

    def vec(w):
        return pl.BlockSpec((1, w), lambda i: (0, 0))

    return pl.pallas_call(
        kern, name="mla_prep", grid=(t // tm,),
        in_specs=[row(A_PAD), vec(Q_LORA), vec(KV_LORA), row(128), row(128)],
        out_specs=[row(Q_LORA), row(KV_LORA), row(128)],
        out_shape=[jax.ShapeDtypeStruct((t, Q_LORA), BF16), jax.ShapeDtypeStruct((t, KV_LORA), BF16),
                   jax.ShapeDtypeStruct((t, 128), BF16)],
        compiler_params=_params("parallel"))(a, gq.reshape(1, -1), gkv.reshape(1, -1), cs_c, cs_s)


def _mla_prep_bwd(a, dcq, dckv, dar, gq, gkv):
    t = a.shape[0]
    tm = min(512, t)
    kv0, r0 = Q_LORA, Q_LORA + KV_LORA

    def kern(a_ref, dcq_ref, dckv_ref, dar_ref, gq_ref, gkv_ref, da_ref, dgq_ref, dgkv_ref):
        @pl.when(pl.program_id(0) == 0)
        def _():
            dgq_ref[...] = jnp.zeros_like(dgq_ref)
            dgkv_ref[...] = jnp.zeros_like(dgkv_ref)

        def back(x, dy, g_ref, dg_ref):
            rstd = _rstd(x)
            xhat = x * rstd
            dg_ref[...] += jnp.broadcast_to(jnp.sum(dy * xhat, axis=0, keepdims=True), dg_ref.shape)
            dxh = dy * g_ref[...]
            return rstd * (dxh - xhat * jnp.mean(dxh * xhat, axis=-1, keepdims=True))

        da_ref[:, 0:kv0] = _bf(back(a_ref[:, 0:kv0], dcq_ref[...], gq_ref, dgq_ref))
        da_ref[:, kv0:r0] = _bf(back(a_ref[:, kv0:r0], dckv_ref[...], gkv_ref, dgkv_ref))
        da_ref[:, r0:A_PAD] = _bf(dar_ref[...])

    def row(w):
        return pl.BlockSpec((tm, w), lambda i: (i, 0))

    def vec(r, w):
        return pl.BlockSpec((r, w), lambda i: (0, 0))

    return pl.pallas_call(
        kern, name="mla_prep_bwd", grid=(t // tm,),
        in_specs=[row(A_PAD), row(Q_LORA), row(KV_LORA), row(128), vec(1, Q_LORA), vec(1, KV_LORA)],
        out_specs=[row(A_PAD), vec(8, Q_LORA), vec(8, KV_LORA)],
        out_shape=[jax.ShapeDtypeStruct((t, A_PAD), BF16), jax.ShapeDtypeStruct((8, Q_LORA), F32),
                   jax.ShapeDtypeStruct((8, KV_LORA), F32)],
        compiler_params=_params("arbitrary"))(a, dcq, dckv, dar, gq.reshape(1, -1), gkv.reshape(1, -1))


def _mla_qkv(cq, ckv, kr, cs_c, cs_s, wuq, wukv):
    t = cq.shape[0]
    tm = min(512, t)
    kvw = NOPE + V_HEAD

    def kern(cq_ref, ckv_ref, kr_ref, c_ref, s_ref, wq_ref, wkv_ref, q_ref, k_ref, v_ref):
        r = jnp.dot(cq_ref[...], wq_ref[...], preferred_element_type=F32)
        kv = jnp.dot(ckv_ref[...], wkv_ref[...], preferred_element_type=F32)
        cv, sv, krv = c_ref[...], s_ref[...], kr_ref[...]
        for h in range(HEADS):
            xr = r[:, h * HEAD_PAD + NOPE:(h + 1) * HEAD_PAD]
            q_ref[h, :, 0:NOPE] = _bf(r[:, h * HEAD_PAD:h * HEAD_PAD + NOPE] * ATTN_SCALE)
            q_ref[h, :, NOPE:] = _bf((xr * cv + _rot(xr) * sv) * ATTN_SCALE)
            k_ref[h, :, 0:NOPE] = _bf(kv[:, h * kvw:h * kvw + NOPE])
            k_ref[h, :, NOPE:] = krv
            v_ref[h] = _bf(kv[:, h * kvw + NOPE:(h + 1) * kvw])

    def row(w):
        return pl.BlockSpec((tm, w), lambda i: (i, 0))

    def heads(w):
        return pl.BlockSpec((HEADS, tm, w), lambda i: (0, i, 0))

    return pl.pallas_call(
        kern, name="mla_qkv", grid=(t // tm,),
        in_specs=[row(Q_LORA), row(KV_LORA), row(128), row(128), row(128),
                  pl.BlockSpec(wuq.shape, lambda i: (0, 0)), pl.BlockSpec(wukv.shape, lambda i: (0, 0))],
        out_specs=[heads(HEAD_PAD), heads(HEAD_PAD), heads(V_HEAD)],
        out_shape=[jax.ShapeDtypeStruct((HEADS, t, HEAD_PAD), BF16), jax.ShapeDtypeStruct((HEADS, t, HEAD_PAD), BF16),
                   jax.ShapeDtypeStruct((HEADS, t, V_HEAD), BF16)],
        compiler_params=_params("parallel"))(cq, ckv, kr, cs_c, cs_s, wuq, wukv)


def _mla_qkv_bwd(dq, dk, dv, cs_c, cs_s, wuq, wukv):
    t = dq.shape[1]
    tm = min(256, t)
    kvw = NOPE + V_HEAD

    def kern(dq_ref, dk_ref, dv_ref, c_ref, s_ref, wq_ref, wkv_ref, dr_ref, dkv_ref, dcq_ref, dckv_ref, dar_ref):
        cv, sv = c_ref[...], s_ref[...]
        dar = jnp.zeros_like(cv)
        for h in range(HEADS):
            dqx = dq_ref[h, :, NOPE:].astype(F32)
            dr_ref[:, h * HEAD_PAD:h * HEAD_PAD + NOPE] = _bf(dq_ref[h, :, 0:NOPE].astype(F32) * ATTN_SCALE)
            dr_ref[:, h * HEAD_PAD + NOPE:(h + 1) * HEAD_PAD] = _bf((dqx * cv + _rot_t(dqx * sv)) * ATTN_SCALE)
            dkx = dk_ref[h, :, NOPE:].astype(F32)
            dar = dar + (dkx * cv + _rot_t(dkx * sv))
            dkv_ref[:, h * kvw:h * kvw + NOPE] = dk_ref[h, :, 0:NOPE]
            dkv_ref[:, h * kvw + NOPE:(h + 1) * kvw] = dv_ref[h]
        dar_ref[...] = dar
        dcq_ref[...] = lax.dot_general(dr_ref[...], wq_ref[...], (((1,), (1,)), ((), ())),
                                       preferred_element_type=F32)
        dckv_ref[...] = lax.dot_general(dkv_ref[...], wkv_ref[...], (((1,), (1,)), ((), ())),
                                        preferred_element_type=F32)

    def row(w):
        return pl.BlockSpec((tm, w), lambda i: (i, 0))

    def heads(w):
        return pl.BlockSpec((HEADS, tm, w), lambda i: (0, i, 0))

    return pl.pallas_call(
        kern, name="mla_qkv_bwd", grid=(t // tm,),
        in_specs=[heads(HEAD_PAD), heads(HEAD_PAD), heads(V_HEAD), row(128), row(128),
                  pl.BlockSpec(wuq.shape, lambda i: (0, 0)), pl.BlockSpec(wukv.shape, lambda i: (0, 0))],
        out_specs=[row(HEADS * HEAD_PAD), row(HEADS * kvw), row(Q_LORA), row(KV_LORA), row(128)],
        out_shape=[jax.ShapeDtypeStruct((t, HEADS * HEAD_PAD), BF16), jax.ShapeDtypeStruct((t, HEADS * kvw), BF16),
                   jax.ShapeDtypeStruct((t, Q_LORA), F32), jax.ShapeDtypeStruct((t, KV_LORA), F32),
                   jax.ShapeDtypeStruct((t, 128), F32)],
        compiler_params=_params("parallel"))(dq, dk, dv, cs_c, cs_s, wuq, wukv)


def _attn_block(t):
    return 512 if t >= 4096 else 128


def _chunk_mask(bk, bq):
    kc = lax.broadcasted_iota(jnp.int32, (bk, bq), 0) // CHUNK
    qc = lax.broadcasted_iota(jnp.int32, (bk, bq), 1) // CHUNK
    return qc >= kc


def _flash_fwd(q, k, v):
    t = q.shape[1]
    bq = _attn_block(t)
    nq = t // bq
    nch = 2

    def kern(q_ref, k_ref, v_ref, o_ref, lse_ref, s_buf, p_buf, m_ref, l_ref, acc_ref):
        i = pl.program_id(1)
        queries = [q_ref[c * bq:(c + 1) * bq, :] for c in range(nch)]

        def block(j):
            rows = pl.ds(pl.multiple_of(j * bq, bq), bq)
            return k_ref[rows, :], v_ref[rows, :]

        def scores(kj, chain):
            return lax.dot_general(kj, queries[chain], (((1,), (1,)), ((), ())), preferred_element_type=F32)

        def softmax_block(chain, slot, vj):
            for c0 in range(0, bq, 128):
                cols = slice(c0, c0 + 128)
                s = s_buf[slot, chain, :, cols]
                m_old = m_ref[chain, 0:1, cols]
                m_new = jnp.maximum(m_old, jnp.max(s, axis=0, keepdims=True))
                alpha = jnp.exp(m_old - m_new)
                p = jnp.exp(s - m_new)
                l_ref[chain, 0:1, cols] = alpha * l_ref[chain, 0:1, cols] + jnp.sum(p, axis=0, keepdims=True)
                m_ref[chain, 0:1, cols] = m_new
                p_buf[chain, :, cols] = _bf(p)
                acc_ref[chain, :, cols] = acc_ref[chain, :, cols] * alpha
            acc_ref[chain] += lax.dot_general(vj, p_buf[chain], (((0,), (0,)), ((), ())),
                                              preferred_element_type=F32)

        m_ref[...] = jnp.full(m_ref.shape, -1e30, F32)
        l_ref[...] = jnp.zeros_like(l_ref)
        acc_ref[...] = jnp.zeros_like(acc_ref)
        mask = _chunk_mask(bq, bq)
        k0, v0 = block(nch * i)
        k1, v1 = block(nch * i + 1)
        s_buf[0, 0] = jnp.where(mask, scores(k0, 0), -1e30)
        s_buf[0, 1] = scores(k0, 1)
        s_buf[1, 1] = jnp.where(mask, scores(k1, 1), -1e30)
        softmax_block(0, 0, v0)
        softmax_block(1, 0, v0)
        softmax_block(1, 1, v1)
        kf = block(0)[0]
        for c in range(nch):
            s_buf[0, c] = scores(kf, c)

        def body(pair, carry):
            for cur in range(2):
                j = 2 * pair + cur
                kn = block(jnp.minimum(j + 1, jnp.maximum(nch * i - 1, 0)))[0]
                for c in range(nch):
                    s_buf[1 - cur, c] = scores(kn, c)
                vj = block(j)[1]
                for c in range(nch):
                    softmax_block(c, cur, vj)
            return carry

        lax.fori_loop(0, (nch // 2) * i, body, 0)
        for chain in range(nch):
            l = l_ref[chain, 0:1, :]
            o_ref[chain * bq:(chain + 1) * bq, :] = _bf((acc_ref[chain] / l).T)
            lse_ref[chain] = jnp.broadcast_to(m_ref[chain, 0:1, :] + jnp.log(l), (8, bq))

    return pl.pallas_call(
        kern, name="flash_fwd", grid=(HEADS, nq // nch),
        in_specs=[pl.BlockSpec((None, nch * bq, HEAD_PAD), lambda h, i: (h, i, 0)),
                  pl.BlockSpec((None, t, HEAD_PAD), lambda h, i: (h, 0, 0)),
                  pl.BlockSpec((None, t, V_HEAD), lambda h, i: (h, 0, 0))],
        out_specs=[pl.BlockSpec((nch * bq, V_HEAD), lambda h, i: (i, h)),
                   pl.BlockSpec((None, nch, 8, bq), lambda h, i: (h, i, 0, 0))],
        out_shape=[jax.ShapeDtypeStruct((t, HEADS * V_HEAD), BF16), jax.ShapeDtypeStruct((HEADS, nq, 8, bq), F32)],
        scratch_shapes=[pltpu.VMEM((2, nch, bq, bq), F32), pltpu.VMEM((nch, bq, bq), BF16),
                        pltpu.VMEM((nch, 8, bq), F32), pltpu.VMEM((nch, 8, bq), F32),
                        pltpu.VMEM((nch, V_HEAD, bq), F32)],
        compiler_params=_params("parallel", "arbitrary"))(q, k, v)


def _attn_delta(do, o):
    t = do.shape[0]
    bq = _attn_block(t)

    def kern(do_ref, o_ref, d_ref):
        for h in range(HEADS):
            cols = slice(h * V_HEAD, (h + 1) * V_HEAD)
            prod = do_ref[:, cols].astype(F32) * o_ref[:, cols].astype(F32)
            d_ref[h] = jnp.broadcast_to(jnp.sum(prod.T, axis=0, keepdims=True), (8, bq))

    blk = pl.BlockSpec((bq, HEADS * V_HEAD), lambda i: (i, 0))
    return pl.pallas_call(
        kern, name="attn_delta", grid=(t // bq,), in_specs=[blk, blk],
        out_specs=pl.BlockSpec((HEADS, None, 8, bq), lambda i: (0, i, 0, 0)),
        out_shape=jax.ShapeDtypeStruct((HEADS, t // bq, 8, bq), F32),
        compiler_params=_params("parallel"))(do, o)


def _flash_bwd(q, k, v, do, lse, delta):
    t = q.shape[1]
    bq = _attn_block(t)
    nq = t // bq

    def kern(q_ref, k_ref, v_ref, do_ref, lse_ref, del_ref, dq_out, dk_out, dv_out, dq_ref, dk_ref, dvt_ref):
        j = pl.program_id(1)

        @pl.when(j == 0)
        def _():
            dq_ref[...] = jnp.zeros_like(dq_ref)

        dk_ref[...] = jnp.zeros_like(dk_ref)
        dvt_ref[...] = jnp.zeros_like(dvt_ref)
        kj, vj = k_ref[...], v_ref[...]

        def step(i, masked):
            rows = pl.ds(pl.multiple_of(i * bq, bq), bq)
            qi, doi = q_ref[rows, :], do_ref[rows, :]
            st = lax.dot_general(kj, qi, (((1,), (1,)), ((), ())), preferred_element_type=F32)
            pt = jnp.exp(st - lse_ref[i][0:1, :])
            if masked:
                pt = jnp.where(_chunk_mask(bq, bq), pt, 0.0)
            dpt = lax.dot_general(vj, doi, (((1,), (1,)), ((), ())), preferred_element_type=F32)
            dst = _bf(pt * (dpt - del_ref[i][0:1, :]))
            dvt_ref[...] += lax.dot_general(doi, _bf(pt), (((0,), (1,)), ((), ())), preferred_element_type=F32)
            dk_ref[...] += jnp.dot(dst, qi, preferred_element_type=F32)
            dq_ref[rows, :] += lax.dot_general(dst, kj, (((0,), (0,)), ((), ())), preferred_element_type=F32)

        step(j, True)

        def body(pair, carry):
            step(j + 1 + 2 * pair, False)
            step(j + 2 + 2 * pair, False)
            return carry

        rest = nq - 1 - j
        lax.fori_loop(0, rest // 2, body, 0)

        @pl.when(rest % 2 == 1)
        def _():
            step(nq - 1, False)

        dk_out[...] = _bf(dk_ref[...])
        dv_out[...] = _bf(dvt_ref[...].T)

        @pl.when(j == nq - 1)
        def _():
            dq_out[...] = _bf(dq_ref[...])

    stat = pl.BlockSpec((None, nq, 8, bq), lambda h, j: (h, 0, 0, 0))
    return pl.pallas_call(
        kern, name="flash_bwd", grid=(HEADS, nq),
        in_specs=[pl.BlockSpec((None, t, HEAD_PAD), lambda h, j: (h, 0, 0)),
                  pl.BlockSpec((None, bq, HEAD_PAD), lambda h, j: (h, j, 0)),
                  pl.BlockSpec((None, bq, V_HEAD), lambda h, j: (h, j, 0)),
                  pl.BlockSpec((t, V_HEAD), lambda h, j: (0, h)), stat, stat],
        out_specs=[pl.BlockSpec((None, t, HEAD_PAD), lambda h, j: (h, 0, 0)),
                   pl.BlockSpec((None, bq, HEAD_PAD), lambda h, j: (h, j, 0)),
                   pl.BlockSpec((None, bq, V_HEAD), lambda h, j: (h, j, 0))],
        out_shape=[jax.ShapeDtypeStruct((HEADS, t, HEAD_PAD), BF16), jax.ShapeDtypeStruct((HEADS, t, HEAD_PAD), BF16),
                   jax.ShapeDtypeStruct((HEADS, t, V_HEAD), BF16)],
        scratch_shapes=[pltpu.VMEM((t, HEAD_PAD), F32), pltpu.VMEM((bq, HEAD_PAD), F32), pltpu.VMEM((V_HEAD, bq), F32)],
        compiler_params=_params("parallel", "arbitrary"))(q, k, v, do, lse, delta)


def _place():
    x, y, c = lax.axis_index("x"), lax.axis_index("y"), lax.axis_index("c")
    return x, y, c, [(1 - x, y), (x, 1 - y), (1 - x, 1 - y)]


def _all_gather_rows(block, name):
    m_per, n = block.shape

    def body(x_ref, out_ref, send_sems, recv_sems, local_sem):
        x, y, c, chips = _place()
        me, sibling = (x, y, c), (x, y, 1 - c)

        def rows(px, py, pc):
            return out_ref.at[pl.ds((4 * px + 2 * py + pc) * m_per, m_per), :]

        def copy(k, blk, to, src=None):
            return pltpu.make_async_remote_copy(
                src_ref=rows(*blk) if src is None else src, dst_ref=rows(*blk), send_sem=send_sems.at[k],
                recv_sem=recv_sems.at[k], device_id=to, device_id_type=MESH)

        mine = pltpu.make_async_copy(x_ref, rows(*me), local_sem)
        mine.start()
        first = [copy(0, me, sibling, src=x_ref)]
        first += [copy(1 + j, me, (*chip, c), src=x_ref) for j, chip in enumerate(chips)]
        for cp in first:
            cp.start()
        passed = [copy(4 + j, (*chip, c), sibling) for j, chip in enumerate(chips)]
        for j, chip in enumerate(chips):
            copy(1 + j, (*chip, c), me).wait_recv()
            passed[j].start()
        copy(0, sibling, me).wait_recv()
        for j, chip in enumerate(chips):
            copy(4 + j, (*chip, 1 - c), me).wait_recv()
        for cp in first + passed:
            cp.wait_send()
        mine.wait()

    return pl.pallas_call(
        body, name=name, out_shape=jax.ShapeDtypeStruct((8 * m_per, n), block.dtype),
        in_specs=[pl.BlockSpec(memory_space=pltpu.VMEM)], out_specs=pl.BlockSpec(memory_space=pltpu.VMEM),
        scratch_shapes=[pltpu.SemaphoreType.DMA((7,)), pltpu.SemaphoreType.DMA((7,)), pltpu.SemaphoreType.DMA],
        compiler_params=pltpu.CompilerParams(vmem_limit_bytes=VMEM_LIMIT_BYTES))(block)


HBM_SPEC = pl.BlockSpec(memory_space=pltpu.HBM)
SEM_SPEC = pl.BlockSpec(memory_space=pltpu.SEMAPHORE)
DATAFLOW = pltpu.SideEffectType.DATAFLOW_SIDE_EFFECTING


def _in_hbm(a):
    return pltpu.with_memory_space_constraint(a, pltpu.HBM)


def _chip_copies(ins, lands, send_sems, recv_sems, src_slot, half=False):
    n = len(ins)
    x, y, c, chips = _place()
    me = 2 * x + y

    def ends(w, chip):
        src = ins[w].at[2 * chip[0] + chip[1]] if src_slot else ins[w]
        if not half:
            return src, lands[w].at[me]
        rows = pl.ds(pl.multiple_of(c * (src.shape[0] // 2), 16), src.shape[0] // 2)
        return src.at[rows], lands[w].at[me, rows]

    copies = []
    for w in range(n):
        for p, chip in enumerate(chips):
            src, dst = ends(w, chip)
            copies.append(pltpu.make_async_remote_copy(
                src_ref=src, dst_ref=dst, send_sem=send_sems.at[p * n + w], recv_sem=recv_sems.at[p * n + w],
                device_id=(*chip, c), device_id_type=MESH))
    return copies


def _fill_halves(lands, name):
    n = len(lands)

    def body(*refs):
        bufs = refs[n:2 * n]
        send_sems, recv_sems = refs[2 * n:]
        x, y, c, chips = _place()
        copies = []
        for w in range(n):
            hr = bufs[w].shape[1] // 2
            for p, chip in enumerate(chips):
                part = bufs[w].at[2 * chip[0] + chip[1], pl.ds(pl.multiple_of(c * hr, 16), hr)]
                copies.append(pltpu.make_async_remote_copy(
                    src_ref=part, dst_ref=part, send_sem=send_sems.at[p * n + w], recv_sem=recv_sems.at[p * n + w],
                    device_id=(x, y, 1 - c), device_id_type=MESH))
        for cp in copies:
            cp.start()
        for cp in copies:
            cp.wait_send()
        for w in range(n):
            hr = bufs[w].shape[1] // 2
            for p, chip in enumerate(chips):
                part = bufs[w].at[2 * chip[0] + chip[1], pl.ds(pl.multiple_of((1 - c) * hr, 16), hr)]
                pltpu.make_async_remote_copy(
                    src_ref=part, dst_ref=part, send_sem=send_sems.at[p * n + w], recv_sem=recv_sems.at[p * n + w],
                    device_id=(x, y, 1 - c), device_id_type=MESH).wait_recv()

    any_spec = pl.BlockSpec(memory_space=pl.ANY)
    return list(pl.pallas_call(
        body, name=name, out_shape=[jax.ShapeDtypeStruct(a.shape, a.dtype) for a in lands],
        in_specs=[any_spec] * n, out_specs=[any_spec] * n, input_output_aliases={i: i for i in range(n)},
        scratch_shapes=[pltpu.SemaphoreType.DMA((3 * n,)), pltpu.SemaphoreType.DMA((3 * n,))])(*lands))


def _exchange_start(srcs, lands, src_slot, name, dep=None, half=False):
    n = len(srcs)
    first_out = 2 * n + (dep is not None)

    def body(*refs):
        for cp in _chip_copies(refs[:n], refs[n:2 * n], refs[first_out], refs[first_out + 1], src_slot, half):
            cp.start()
        token = refs[-1]
        token[...] = jnp.zeros_like(token)

    thru = [pltpu.HBM(a.shape, a.dtype) for a in list(srcs) + list(lands)]
    res = pl.pallas_call(
        body, name=name,
        out_shape=(pltpu.SemaphoreType.DMA((3 * n,)), pltpu.SemaphoreType.DMA((3 * n,)), *thru,
                   jax.ShapeDtypeStruct((8, 128), F32)),
        in_specs=[HBM_SPEC] * (2 * n) + ([pl.BlockSpec(memory_space=pl.ANY)] if dep is not None else []),
        out_specs=(SEM_SPEC, SEM_SPEC, *[HBM_SPEC] * (2 * n), pl.BlockSpec(memory_space=pltpu.VMEM)),
        input_output_aliases={i: 2 + i for i in range(2 * n)},
        compiler_params=pltpu.CompilerParams(has_side_effects=DATAFLOW))(
            *[_in_hbm(a) for a in srcs], *[_in_hbm(a) for a in lands], *([dep] if dep is not None else []))
    return (res[0], res[1], list(res[2:2 + n]), list(res[2 + n:2 + 2 * n])), res[-1]


def _exchange_wait(flight, after, src_slot, name, half=False):
    send_sems, recv_sems, srcs, lands = flight
    n = len(srcs)

    def body(*refs):
        for cp in _chip_copies(refs[:n], refs[n:2 * n], refs[2 * n], refs[2 * n + 1], src_slot, half):
            cp.wait_send()
            cp.wait_recv()

    thru = [pltpu.HBM(a.shape, a.dtype) for a in list(srcs) + list(lands)]
    res = pl.pallas_call(
        body, name=name, out_shape=thru,
        in_specs=[HBM_SPEC] * (2 * n) + [SEM_SPEC, SEM_SPEC, pl.BlockSpec(memory_space=pl.ANY)],
        out_specs=[HBM_SPEC] * (2 * n), input_output_aliases={i: i for i in range(2 * n)},
        compiler_params=pltpu.CompilerParams(has_side_effects=DATAFLOW))(*srcs, *lands, send_sems, recv_sems, after)
    return list(res[n:])


def _landing(own, me):
    return lax.dynamic_update_index_in_dim(lax.empty((4, *own.shape), own.dtype), own, me, 0)


def _swap_with_sibling(arrays, name):
    n = len(arrays)

    def body(*refs):
        ins, outs = refs[:n], refs[n:2 * n]
        send_sems, recv_sems = refs[2 * n:]
        x, y, c, _ = _place()
        copies = [pltpu.make_async_remote_copy(src_ref=ins[w], dst_ref=outs[w], send_sem=send_sems.at[w],
                                               recv_sem=recv_sems.at[w], device_id=(x, y, 1 - c), device_id_type=MESH)
                  for w in range(n)]
        for cp in copies:
            cp.start()
        for cp in copies:
            cp.wait()

    any_spec = pl.BlockSpec(memory_space=pl.ANY)
    return pl.pallas_call(
        body, name=name, out_shape=[jax.ShapeDtypeStruct(a.shape, a.dtype) for a in arrays],
        in_specs=[any_spec] * n, out_specs=[any_spec] * n,
        scratch_shapes=[pltpu.SemaphoreType.DMA((n,)), pltpu.SemaphoreType.DMA((n,))])(*arrays)


def _as_rows(a):
    return a.reshape(-1, a.shape[-1])


def _row_tile(r, c, budget_bytes=1 << 20):
    tr = r
    while tr % 16 == 0 and tr * c * 4 > budget_bytes:
        tr //= 2
    return tr


def _sum_slots(layers, nlayer, name, into=None):
    _, r, c = layers[0][1].shape
    tr = _row_tile(r, c)
    nt = r // tr
    acc = into
    for l, r4 in layers:
        def kern(r_ref, *rest):
            o_ref = rest[-1]
            o_ref[...] = (((r_ref[0].astype(F32) + r_ref[1].astype(F32)) + r_ref[2].astype(F32))
                          + r_ref[3].astype(F32))

        out_spec = pl.BlockSpec((tr, c), lambda i, l=l: (l * nt + i, 0))
        first = acc is None
        acc = pl.pallas_call(
            kern, name=f"{name}_l{l}", grid=(nt,),
            in_specs=[pl.BlockSpec((4, tr, c), lambda i: (0, i, 0))]
            + ([] if first else [pl.BlockSpec(memory_space=pl.ANY)]),
            out_specs=out_spec, out_shape=jax.ShapeDtypeStruct((nlayer * r, c), F32),
            input_output_aliases={} if first else {1: 0},
            compiler_params=_params("parallel"))(*([r4] if first else [r4, acc]))
    return acc


def _adamw(w, m, v, parts, name):
    r, c = w.shape
    tr = _row_tile(r, c, 3 << 19)
    npart = len(parts)
    c1 = 1.0 - ADAM_B1 ** ADAM_STEP
    c2 = 1.0 - ADAM_B2 ** ADAM_STEP

    def kern(*refs):
        w_ref, m_ref, v_ref = refs[:3]
        p_refs = refs[3:3 + npart]
        g_ref, d_ref, mo_ref, vo_ref = refs[3 + npart:]
        g = p_refs[0][...]
        for p in p_refs[1:]:
            g = g + p[...]
        mn = ADAM_B1 * m_ref[...] + (1.0 - ADAM_B1) * g
        vn = ADAM_B2 * v_ref[...] + (1.0 - ADAM_B2) * (g * g)
        g_ref[...] = g
        mo_ref[...] = mn
        vo_ref[...] = vn
        d_ref[...] = -ADAM_LR * ((mn / c1) / (jnp.sqrt(vn / c2) + ADAM_EPS) + ADAM_WD * w_ref[...])

    blk = pl.BlockSpec((tr, c), lambda i: (i, 0))
    shape = jax.ShapeDtypeStruct((r, c), F32)
    return pl.pallas_call(
        kern, name=name, grid=(r // tr,), in_specs=[blk] * (3 + npart), out_specs=[blk] * 4, out_shape=[shape] * 4,
        compiler_params=_params("parallel"))(w, m, v, *parts)


def _sum_devices(g8, name):
    _, r, c = g8.shape

    def kern(g_ref, o_ref):
        tot = g_ref[0]
        for dev in range(1, 8):
            tot = tot + g_ref[dev]
        o_ref[...] = tot

    return pl.pallas_call(
        kern, name=name, grid=(1,), in_specs=[pl.BlockSpec((8, r, c), lambda i: (0, 0, 0))],
        out_specs=pl.BlockSpec((r, c), lambda i: (0, 0)), out_shape=jax.ShapeDtypeStruct((r, c), F32),
        compiler_params=_params("arbitrary"))(g8)


def _pad_lanes(a, width):
    return jnp.pad(a, [(0, 0)] * (a.ndim - 1) + [(0, width - a.shape[-1])])


def kernel(x, positions, ffn_norm1, ffn1_w1, ffn1_w3, ffn1_w2, mix_norm, ffn_norm2, ffn2_w1, ffn2_w3, ffn2_w2, conv_w_pw1, conv_w_dw, conv_norm, conv_w_pw2, mla_w_a, mla_q_norm, mla_kv_norm, mla_w_uq, mla_w_ukv, mla_w_o, final_norm, loss_target, m_ffn_norm1, m_ffn1_w1, m_ffn1_w3, m_ffn1_w2, m_mix_norm, m_ffn_norm2, m_ffn2_w1, m_ffn2_w3, m_ffn2_w2, m_conv_w_pw1, m_conv_w_dw, m_conv_norm, m_conv_w_pw2, m_mla_w_a, m_mla_q_norm, m_mla_kv_norm, m_mla_w_uq, m_mla_w_ukv, m_mla_w_o, m_final_norm, v_ffn_norm1, v_ffn1_w1, v_ffn1_w3, v_ffn1_w2, v_mix_norm, v_ffn_norm2, v_ffn2_w1, v_ffn2_w3, v_ffn2_w2, v_conv_w_pw1, v_conv_w_dw, v_conv_norm, v_conv_w_pw2, v_mla_w_a, v_mla_q_norm, v_mla_kv_norm, v_mla_w_uq, v_mla_w_ukv, v_mla_w_o, v_final_norm):
    given = locals()
    return _step({nm: given[nm] for nm in INPUTS})


def _step(A):
    x = A['x'][0]
    target = A['loss_target'][0]
    t, d = x.shape
    pos = A['positions'].reshape(t, 1)
    me = 2 * lax.axis_index("x") + lax.axis_index("y")

    flipped = {f'ffn{k}_{w}' for k in (1, 2) for w in ('w1', 'w3')}
    P = {}
    for nm in BIG:
        for key in (nm, 'm_' + nm, 'v_' + nm):
            P[key] = jnp.swapaxes(A[key], 1, 2) if nm in flipped else A[key]

    def unflip(nm, a):
        return jnp.swapaxes(a, 1, 2) if nm in flipped else a

    ffn = [f'ffn{k}_{w}' for k in (1, 2) for w in ('w1', 'w3', 'w2')]
    gather_groups = [[(nm, 0) for nm in ffn[:3]],
                     [('conv_w_pw1', 0), ('conv_w_pw2', 0)] + [(nm, 0) for nm in ffn[3:]],
                     [(nm, 1) for nm in ffn[:3]] + [('mla_w_a', 0), ('mla_w_uq', 0), ('mla_w_ukv', 0), ('mla_w_o', 0)],
                     [(nm, 1) for nm in ffn[3:]]]
    halved = (0, 1)
    gather_flights = {}
    big = {}

    def gather_start(gi, dep):
        shards = [_bf(P[nm][l]) for nm, l in gather_groups[gi]]
        gather_flights[gi], token = _exchange_start(shards, [_landing(s, me) for s in shards], False,
                                                    f"gather_start_{gi}", dep, half=gi in halved)
        return token

    def gather_wait(gi, after):
        landed = _exchange_wait(gather_flights[gi], after, False, f"gather_wait_{gi}", half=gi in halved)
        if gi in halved:
            landed = _fill_halves(landed, f"gather_fill_{gi}")
        big.update(zip(gather_groups[gi], landed))
        return landed[0]

    dw_shard = A['conv_w_dw'][0]
    cw = dw_shard.shape[1]
    small = jnp.concatenate([
        jnp.pad(dw_shard, ((0, CONV_HALO - CONV_WIDTH), (0, 0))),
        jnp.pad(_pad_lanes(A['mla_q_norm'], cw), ((0, 7), (0, 0))),
        jnp.pad(_pad_lanes(A['mla_kv_norm'], cw), ((0, 7), (0, 0)))], axis=0)
    small = _all_gather_rows(small, "gather_small_weights").reshape(4, 2, 48, cw)[:, 0]
    w_dw = jnp.concatenate([small[j, :CONV_HALO] for j in range(4)], axis=1)
    gq = jnp.concatenate([small[j, CONV_HALO, :Q_LORA // 4] for j in range(4)])
    gkv = jnp.concatenate([small[j, CONV_HALO + 8, :KV_LORA // 4] for j in range(4)])

    def rows(nm, layer):
        g = big[nm, layer]
        return g.reshape(-1, g.shape[-1])

    ffn_w = {}

    def ffn_weights(k, l):
        ffn_w[k, l] = (rows(f'ffn{k}_w1', l), rows(f'ffn{k}_w3', l), rows(f'ffn{k}_w2', l))
        return ffn_w[k, l]

    token = gather_start(0, small)
    cs_c, cs_s = _rope_tables(pos)
    h0 = x
    token = gather_start(1, gather_wait(0, token))
    h1, n01, z01a, z01b = _ffn_fwd(h0, A['ffn_norm1'][0], *ffn_weights(1, 0), token, "ffn1_l0_fwd")
    token = gather_start(3, gather_start(2, gather_wait(1, h1)))
    pw1 = big['conv_w_pw1', 0]
    pw1_a = jnp.concatenate([pw1[0], pw1[1]], axis=1)
    pw1_b = jnp.concatenate([pw1[2], pw1[3]], axis=1)
    pw2 = rows('conv_w_pw2', 0)
    m0 = _norm_fwd(h1, A['mix_norm'][0], token, "mix_norm_l0")
    ca, cb, glu = _glu_fwd(m0, pw1_a, pw1_b)
    cv, cs = _conv_fwd(glu, w_dw, A['conv_norm'][0])
    h2 = _mm([(cs, pw2)], F32, "conv_pw2_fwd", res=h1)
    h3, n02, z02a, z02b = _ffn_fwd(h2, A['ffn_norm2'][0], *ffn_weights(2, 0), token, "ffn2_l0_fwd")
    gather_wait(2, h3)
    w_a = _pad_lanes(rows('mla_w_a', 0), A_PAD)
    wuq = _pad_lanes(big['mla_w_uq', 0].reshape(Q_LORA, HEADS, NOPE + ROPE), HEAD_PAD).reshape(Q_LORA, -1)
    wukv = big['mla_w_ukv', 0].reshape(KV_LORA, HEADS * (NOPE + V_HEAD))
    w_o = rows('mla_w_o', 0)
    h4, n11, z11a, z11b = _ffn_fwd(h3, A['ffn_norm1'][1], *ffn_weights(1, 1), token, "ffn1_l1_fwd")
    m1 = _norm_fwd(h4, A['mix_norm'][1], token, "mix_norm_l1")
    a_lat = _mm([(m1, w_a)], F32, "mla_down_fwd")
    cq, ckv, kr = _mla_prep(a_lat, gq, gkv, cs_c, cs_s)
    q, k, v = _mla_qkv(cq, ckv, kr, cs_c, cs_s, wuq, wukv)
    o, lse = _flash_fwd(q, k, v)
    h5 = _mm([(o, w_o)], F32, "mla_out_fwd", res=h4)
    gather_wait(3, h5)
    h6, n12, z12a, z12b = _ffn_fwd(h5, A['ffn_norm2'][1], *ffn_weights(2, 1), token, "ffn2_l1_fwd")

    def row_slots(g):
        return g.reshape(4, g.shape[0] // 4, g.shape[1])

    scatter_flights = []

    def scatter_start(named):
        srcs = [g for _, g in named]
        lands = [_landing(lax.dynamic_index_in_dim(g, me, 0, keepdims=False), me) for g in srcs]
        flight, token = _exchange_start(srcs, lands, True, f"scatter_start_{len(scatter_flights)}")
        scatter_flights.append(([key for key, _ in named], flight))
        return token

    def send_ffn(k, l, dw1t, dw3t, dw2):
        return scatter_start([((f'ffn{k}_w1', l), row_slots(dw1t)), ((f'ffn{k}_w3', l), row_slots(dw3t)),
                              ((f'ffn{k}_w2', l), row_slots(dw2))])

    dh6, dg_final, loss_part = _loss_bwd(h6, target, A['final_norm'])
    dh5, dg_n2_l1, *dws = _ffn_bwd(dh6, h5, A['ffn_norm2'][1], n12, z12a, z12b, *ffn_w[2, 1], loss_part, "ffn2_l1")
    token = send_ffn(2, 1, *dws)

    do = _mm([(dh5, w_o)], BF16, "mla_out_bwd", trans_b=True, dep=token)
    dw_o = _mm_tn(o, dh5, BF16, "mla_dw_o")
    delta = _attn_delta(do, o)
    dq, dk, dv = _flash_bwd(q, k, v, do, lse, delta)
    dr, dkv, dcq, dckv, dar = _mla_qkv_bwd(dq, dk, dv, cs_c, cs_s, wuq, wukv)
    dwuq = _mm_tn(cq, dr, BF16, "mla_dw_uq", bn=dr.shape[1] // 2)
    dwukv = _mm_tn(ckv, dkv, BF16, "mla_dw_ukv", bn=dkv.shape[1] // 2)
    da_lat, dgq, dgkv = _mla_prep_bwd(a_lat, dcq, dckv, dar, gq, gkv)
    dw_a = _mm_tn(m1, da_lat, BF16, "mla_dw_a")
    token = scatter_start([
        (('mla_w_a', 0), row_slots(dw_a[:, :Q_LORA + KV_LORA + ROPE])),
        (('mla_w_uq', 0), dwuq.reshape(4, Q_LORA // 4, HEADS, HEAD_PAD)[..., :NOPE + ROPE]),
        (('mla_w_ukv', 0), dwukv.reshape(4, KV_LORA // 4, HEADS, NOPE + V_HEAD)),
        (('mla_w_o', 0), row_slots(dw_o))])
    dh4, dg_mix_l1 = _mm_normbwd([(da_lat, w_a)], h4, A['mix_norm'][1], dh5, token, "mla_down_bwd")

    dh3, dg_n1_l1, *dws = _ffn_bwd(dh4, h3, A['ffn_norm1'][1], n11, z11a, z11b, *ffn_w[1, 1], token, "ffn1_l1")
    token = send_ffn(1, 1, *dws)
    dh2, dg_n2_l0, *dws = _ffn_bwd(dh3, h2, A['ffn_norm2'][0], n02, z02a, z02b, *ffn_w[2, 0], token, "ffn2_l0")
    token = send_ffn(2, 0, *dws)

    dcv, dg_conv = _conv_bwd_norm(dh2, cv, pw2, A['conv_norm'][0], token)
    dw_pw2 = _mm_tn(cs, dh2, BF16, "conv_dw_pw2")
    dca, dcb, ddw = _conv_bwd_dw(dcv, glu, ca, cb, w_dw)
    dpw1_a = _mm_tn(m0, dca, BF16, "conv_dw_pw1a")
    dpw1_b = _mm_tn(m0, dcb, BF16, "conv_dw_pw1b")
    half = dpw1_a.shape[1] // 2
    token = scatter_start([
        (('conv_w_pw1', 0), jnp.stack([dpw1_a[:, :half], dpw1_a[:, half:], dpw1_b[:, :half], dpw1_b[:, half:]])),
        (('conv_w_pw2', 0), row_slots(dw_pw2))])
    dh1, dg_mix_l0 = _mm_normbwd([(dca, pw1_a), (dcb, pw1_b)], h1, A['mix_norm'][0], dh2, token, "conv_pw1_bwd")

    dx, dg_n1_l0, *dws = _ffn_bwd(dh1, h0, A['ffn_norm1'][0], n01, z01a, z01b, *ffn_w[1, 0], token, "ffn1_l0")
    last_sent = send_ffn(1, 0, *dws)
    out = {}

    qkv_row = jnp.concatenate([dgq, dgkv, jnp.zeros((8, d - Q_LORA - KV_LORA), F32)], axis=1)
    loss_row = _pad_lanes(loss_part, d)
    small_g = jnp.concatenate([dg_n1_l0, dg_n1_l1, dg_mix_l0, dg_mix_l1, dg_n2_l0, dg_n2_l1, dg_conv, dg_final,
                               qkv_row, loss_row, ddw], axis=0)
    nrow = small_g.shape[0]
    tot = _sum_devices(_all_gather_rows(small_g, "gather_small_grads").reshape(8, nrow, d), "sum_small_grads")
    loss = tot[72, 0]
    q_shard = lax.dynamic_slice_in_dim(tot[64, :Q_LORA], me * (Q_LORA // 4), Q_LORA // 4)
    kv_shard = lax.dynamic_slice_in_dim(tot[64, Q_LORA:Q_LORA + KV_LORA], me * (KV_LORA // 4), KV_LORA // 4)
    dw_shard_g = lax.dynamic_slice_in_dim(tot[80:80 + CONV_WIDTH], me * cw, cw, axis=1)
    small_grads = {
        'ffn_norm1': jnp.stack([tot[0], tot[8]]), 'mix_norm': jnp.stack([tot[16], tot[24]]),
        'ffn_norm2': jnp.stack([tot[32], tot[40]]), 'conv_norm': tot[48][None], 'final_norm': tot[56],
        'mla_q_norm': q_shard[None], 'mla_kv_norm': kv_shard[None], 'conv_w_dw': dw_shard_g[None],
    }
    for nm, g in small_grads.items():
        res = _adamw(_as_rows(A[nm]) if A[nm].ndim > 1 else A[nm].reshape(1, -1),
                     A['m_' + nm].reshape(-1, A[nm].shape[-1]), A['v_' + nm].reshape(-1, A[nm].shape[-1]),
                     [g.reshape(-1, A[nm].shape[-1])], "adamw_" + nm)
        out[nm] = [r.reshape(A[nm].shape) for r in res]

    received = {}
    after = last_sent

    def scatter_wait(si, after):
        keys, flight = scatter_flights[si]
        landed = _exchange_wait(flight, after, True, f"scatter_wait_{si}")
        received.update(zip(keys, landed))
        return landed[0]

    def slots(nm, l):
        return received[nm, l].reshape(4, -1, received[nm, l].shape[-1])

    def finish(names, sums, tag):
        for nm, mine, theirs in zip(names, sums, _swap_with_sibling(sums, "swap_with_sibling_" + tag)):
            res = _adamw(_as_rows(P[nm]), _as_rows(P['m_' + nm]), _as_rows(P['v_' + nm]), [mine, theirs],
                         "adamw_" + nm)
            out[nm] = [unflip(nm, r.reshape(P[nm].shape)) for r in res]
        return res[1]

    last = len(scatter_flights) - 1
    for si in range(last):
        after = scatter_wait(si, after)
    late = ffn[:3]
    early = [nm for nm in BIG if nm not in late]
    late_l1 = [_sum_slots([(1, slots(nm, 1))], 2, "sum_" + nm) for nm in late]
    after = finish(early, [_sum_slots([(l, slots(nm, l)) for l in range(A[nm].shape[0])], A[nm].shape[0],
                                      "sum_" + nm) for nm in early], "early")
    scatter_wait(last, after)
    finish(late, [_sum_slots([(0, slots(nm, 0))], 2, "sum_" + nm, into=part) for nm, part in zip(late, late_l1)],
           "late")

    return (loss, dx[None], *[out[nm][0] for nm in WEIGHTS], *[out[nm][1] for nm in WEIGHTS],
            *[out[nm][2] for nm in WEIGHTS], *[out[nm][3] for nm in WEIGHTS])
```

```python
import jax
import jax.numpy as jnp
import numpy as np
from jax import lax
from jax.experimental import pallas as pl
from jax.experimental.pallas import tpu as pltpu

F32 = jnp.float32
BF16 = jnp.bfloat16
MESH = pl.DeviceIdType.MESH

RMS_EPS = 1e-6
HEADS = 8
NOPE = 128
ROPE = 64
HEAD_PAD = 256
V_HEAD = 128
Q_LORA = 512
KV_LORA = 256
A_PAD = 896
CHUNK = 64
CONV_WIDTH = 31
CONV_HALO = 32
CONV_ROWS = 16
ROPE_THETA = 10000.0
ATTN_SCALE = (NOPE + ROPE) ** -0.5
FFN_RES = 0.5

ADAM_LR = 0.001
ADAM_B1 = 0.9
ADAM_B2 = 0.999
ADAM_EPS = 1e-08
ADAM_WD = 0.01
ADAM_STEP = 10

VMEM_LIMIT_BYTES = 56 * 1024 * 1024

WEIGHTS = ['ffn_norm1', 'ffn1_w1', 'ffn1_w3', 'ffn1_w2', 'mix_norm', 'ffn_norm2', 'ffn2_w1', 'ffn2_w3', 'ffn2_w2',
           'conv_w_pw1', 'conv_w_dw', 'conv_norm', 'conv_w_pw2', 'mla_w_a', 'mla_q_norm', 'mla_kv_norm', 'mla_w_uq',
           'mla_w_ukv', 'mla_w_o', 'final_norm']
INPUTS = (['x', 'positions'] + WEIGHTS + ['loss_target'] + ['m_' + w for w in WEIGHTS] + ['v_' + w for w in WEIGHTS])
BIG = ['ffn1_w1', 'ffn1_w3', 'ffn1_w2', 'ffn2_w1', 'ffn2_w3', 'ffn2_w2', 'conv_w_pw1', 'conv_w_pw2', 'mla_w_a',
       'mla_w_uq', 'mla_w_ukv', 'mla_w_o']


def _params(*sem):
    return pltpu.CompilerParams(dimension_semantics=sem, vmem_limit_bytes=VMEM_LIMIT_BYTES)


def _bf(v):
    return v.astype(BF16)


def _rstd(x):
    return lax.rsqrt(jnp.mean(x * x, axis=-1, keepdims=True) + RMS_EPS)


def _sigmoid(x):
    return jax.nn.sigmoid(x)


def _rot(x):
    lane = lax.broadcasted_iota(jnp.int32, x.shape, 1)
    return jnp.where(lane < ROPE // 2, -pltpu.roll(x, 128 - ROPE // 2, 1), pltpu.roll(x, ROPE // 2, 1))


def _rot_t(y):
    lane = lax.broadcasted_iota(jnp.int32, y.shape, 1)
    return jnp.where(lane < ROPE // 2, pltpu.roll(y, 128 - ROPE // 2, 1), -pltpu.roll(y, ROPE // 2, 1))


def _pair_sum(a_refs, b_refs, trans_b):
    tot = None
    for a_r, b_r in zip(a_refs, b_refs):
        a, b = _bf(a_r[...]), _bf(b_r[...])
        if trans_b:
            d = lax.dot_general(a, b, (((1,), (1,)), ((), ())), preferred_element_type=F32)
        else:
            d = jnp.dot(a, b, preferred_element_type=F32)
        tot = d if tot is None else tot + d
    return tot


def _mm(pairs, out_dtype, name, *, trans_b=False, tm=512, tn=None, tk=None, res=None, dep=None):
    m, k = pairs[0][0].shape
    n = pairs[0][1].shape[0] if trans_b else pairs[0][1].shape[1]
    tm, tn, tk = min(tm, m), tn or n, tk or k
    nk, npair = k // tk, len(pairs)

    def kern(*refs):
        a_refs, b_refs = refs[:npair], refs[npair:2 * npair]
        rest = list(refs[2 * npair:])
        res_ref = rest.pop(0) if res is not None else None
        if dep is not None:
            rest.pop(0)
        o_ref = rest.pop(0)

        def finish(acc):
            if res_ref is not None:
                acc = res_ref[...] + acc
            o_ref[...] = acc.astype(o_ref.dtype)

        if nk == 1:
            finish(_pair_sum(a_refs, b_refs, trans_b))
        else:
            acc_ref = rest.pop(0)
            kk = pl.program_id(2)

            @pl.when(kk == 0)
            def _():
                acc_ref[...] = jnp.zeros_like(acc_ref)

            acc_ref[...] += _pair_sum(a_refs, b_refs, trans_b)

            @pl.when(kk == nk - 1)
            def _():
                finish(acc_ref[...])

    a_spec = pl.BlockSpec((tm, tk), lambda i, j, kk: (i, kk))
    b_spec = (pl.BlockSpec((tn, tk), lambda i, j, kk: (j, kk)) if trans_b
              else pl.BlockSpec((tk, tn), lambda i, j, kk: (kk, j)))
    io_spec = pl.BlockSpec((tm, tn), lambda i, j, kk: (i, j))
    in_specs = ([a_spec] * npair + [b_spec] * npair + ([io_spec] if res is not None else [])
                + ([pl.BlockSpec((8, 128), lambda i, j, kk: (0, 0))] if dep is not None else []))
    args = ([p[0] for p in pairs] + [p[1] for p in pairs] + ([res] if res is not None else [])
            + ([dep] if dep is not None else []))
    return pl.pallas_call(
        kern, name=name, grid=(m // tm, n // tn, nk), in_specs=in_specs, out_specs=io_spec,
        out_shape=jax.ShapeDtypeStruct((m, n), out_dtype),
        scratch_shapes=[pltpu.VMEM((tm, tn), F32)] if nk > 1 else [],
        compiler_params=_params("parallel", "parallel", "arbitrary"))(*args)


def _mm_normbwd(pairs, h, g, dres, dep, name, *, tm=512, tk=None):
    m, k = pairs[0][0].shape
    d = pairs[0][1].shape[0]
    tm, tk = min(tm, m), tk or k
    nk, npair = k // tk, len(pairs)

    def kern(*refs):
        a_refs, b_refs = refs[:npair], refs[npair:2 * npair]
        h_ref, g_ref, dres_ref, _, o_ref, dg_ref, acc_ref = refs[2 * npair:]
        i, kk = pl.program_id(0), pl.program_id(1)

        @pl.when(jnp.logical_and(i == 0, kk == 0))
        def _():
            dg_ref[...] = jnp.zeros_like(dg_ref)

        @pl.when(kk == 0)
        def _():
            acc_ref[...] = jnp.zeros_like(acc_ref)

        acc_ref[...] += _pair_sum(a_refs, b_refs, True)

        @pl.when(kk == nk - 1)
        def _():
            dn = acc_ref[...]
            x = h_ref[...]
            rstd = _rstd(x)
            xhat = x * rstd
            dg_ref[...] += jnp.broadcast_to(jnp.sum(dn * xhat, axis=0, keepdims=True), dg_ref.shape)
            dxh = dn * g_ref[...]
            dx = rstd * (dxh - xhat * jnp.mean(dxh * xhat, axis=-1, keepdims=True))
            o_ref[...] = dres_ref[...] + dx

    row = pl.BlockSpec((tm, d), lambda i, kk: (i, 0))
    in_specs = ([pl.BlockSpec((tm, tk), lambda i, kk: (i, kk))] * npair
                + [pl.BlockSpec((d, tk), lambda i, kk: (0, kk))] * npair
                + [row, pl.BlockSpec((1, d), lambda i, kk: (0, 0)), row, pl.BlockSpec((8, 128), lambda i, kk: (0, 0))])
    return pl.pallas_call(
        kern, name=name, grid=(m // tm, nk), in_specs=in_specs,
        out_specs=[row, pl.BlockSpec((8, d), lambda i, kk: (0, 0))],
        out_shape=[jax.ShapeDtypeStruct((m, d), F32), jax.ShapeDtypeStruct((8, d), F32)],
        scratch_shapes=[pltpu.VMEM((tm, d), F32)],
        compiler_params=_params("arbitrary", "arbitrary"))(
            *[p[0] for p in pairs], *[p[1] for p in pairs], h, g.reshape(1, d), dres, dep)


def _mm_tn(a, b, out_dtype, name, *, bm=None, bn=None, tk=1024):
    t, m = a.shape
    batched = b.ndim == 3
    n = b.shape[-1]
    nb = b.shape[0] if batched else 1
    bm, bn, tk = bm or m, bn or n, min(tk, t)
    nk = t // tk

    def kern(a_ref, b_ref, o_ref, acc_ref):
        kk = pl.program_id(3)

        @pl.when(kk == 0)
        def _():
            acc_ref[...] = jnp.zeros_like(acc_ref)

        acc_ref[...] += lax.dot_general(_bf(a_ref[...]), _bf(b_ref[...]), (((0,), (0,)), ((), ())),
                                        preferred_element_type=F32)

        @pl.when(kk == nk - 1)
        def _():
            o_ref[...] = acc_ref[...].astype(o_ref.dtype)

    a_spec = pl.BlockSpec((tk, bm), lambda h, i, j, kk: (kk, i))
    if batched:
        b_spec = pl.BlockSpec((None, tk, bn), lambda h, i, j, kk: (h, kk, j))
        o_spec = pl.BlockSpec((None, bm, bn), lambda h, i, j, kk: (h, i, j))
        out_shape = jax.ShapeDtypeStruct((nb, m, n), out_dtype)
    else:
        b_spec = pl.BlockSpec((tk, bn), lambda h, i, j, kk: (kk, j))
        o_spec = pl.BlockSpec((bm, bn), lambda h, i, j, kk: (i, j))
        out_shape = jax.ShapeDtypeStruct((m, n), out_dtype)
    return pl.pallas_call(
        kern, name=name, grid=(nb, m // bm, n // bn, nk), in_specs=[a_spec, b_spec], out_specs=o_spec,
        out_shape=out_shape, scratch_shapes=[pltpu.VMEM((bm, bn), F32)],
        compiler_params=_params("parallel", "parallel", "parallel", "arbitrary"))(a, b)


def _ffn_tile(f):
    return f // 2 if (f // 2) % 128 == 0 else f


def _ffn_fwd(h, g, w1t, w3t, w2, dep, name):
    t, d = h.shape
    f = w1t.shape[0]
    tm = min(256, t)
    nt = (((1,), (1,)), ((), ()))

    def kern(h_ref, g_ref, w1_hbm, w3_hbm, w2_hbm, dep_ref, ho_ref, n_ref, z1_ref, z3_ref,
             w1_ref, w3_ref, w2_ref, sems):
        @pl.when(pl.program_id(0) == 0)
        def _():
            copies = [pltpu.make_async_copy(src, dst, sems.at[k]) for k, (src, dst) in
                      enumerate(((w1_hbm, w1_ref), (w3_hbm, w3_ref), (w2_hbm, w2_ref)))]
            for cp in copies:
                cp.start()
            for cp in copies:
                cp.wait()

        x = h_ref[...]
        n = _bf(x * _rstd(x) * g_ref[...])
        n_ref[...] = n
        z1 = lax.dot_general(n, w1_ref[...], nt, preferred_element_type=F32)
        z3 = lax.dot_general(n, w3_ref[...], nt, preferred_element_type=F32)
        z1_ref[...] = _bf(z1)
        z3_ref[...] = _bf(z3)
        act = _bf(z1 * _sigmoid(z1) * z3)
        ho_ref[...] = x + FFN_RES * jnp.dot(act, w2_ref[...], preferred_element_type=F32)

    row = pl.BlockSpec((tm, d), lambda i: (i, 0))
    col = pl.BlockSpec((tm, f), lambda i: (i, 0))
    whole = pl.BlockSpec(memory_space=pl.ANY)
    return pl.pallas_call(
        kern, name=name, grid=(t // tm,),
        in_specs=[row, pl.BlockSpec((1, d), lambda i: (0, 0)), whole, whole, whole,
                  pl.BlockSpec((8, 128), lambda i: (0, 0))],
        out_specs=[row, row, col, col],
        out_shape=[jax.ShapeDtypeStruct((t, d), F32), jax.ShapeDtypeStruct((t, d), BF16),
                   jax.ShapeDtypeStruct((t, f), BF16), jax.ShapeDtypeStruct((t, f), BF16)],
        scratch_shapes=[pltpu.VMEM((f, d), BF16), pltpu.VMEM((f, d), BF16), pltpu.VMEM((f, d), BF16),
                        pltpu.SemaphoreType.DMA((3,))],
        compiler_params=_params("arbitrary"))(h, g.reshape(1, d), w1t, w3t, w2, dep)


def _ffn_bwd_x(dh, h_in, g, z1, z3, w1t, w3t, w2, dep, name):
    t, d = dh.shape
    f = z1.shape[1]
    tm = min(256, t)

    def kern(dh_ref, h_ref, g_ref, z1_ref, z3_ref, w2_hbm, w1_hbm, w3_hbm, dep_ref,
             o_ref, dg_ref, dz1_ref, dz3_ref, a_ref, df_ref, w2_ref, w1_ref, w3_ref, sems):
        @pl.when(pl.program_id(0) == 0)
        def _():
            copies = [pltpu.make_async_copy(src, dst, sems.at[k]) for k, (src, dst) in
                      enumerate(((w2_hbm, w2_ref), (w1_hbm, w1_ref), (w3_hbm, w3_ref)))]
            for cp in copies:
                cp.start()
            dg_ref[...] = jnp.zeros_like(dg_ref)
            for cp in copies:
                cp.wait()

        df = _bf(FFN_RES * dh_ref[...])
        df_ref[...] = df
        da = lax.dot_general(df, w2_ref[...], (((1,), (1,)), ((), ())), preferred_element_type=F32)
        z1v, z3v = z1_ref[...].astype(F32), z3_ref[...].astype(F32)
        sig = _sigmoid(z1v)
        silu = z1v * sig
        a_ref[...] = _bf(silu * z3v)
        dz1 = _bf(da * z3v * (sig * (1.0 + z1v * (1.0 - sig))))
        dz3 = _bf(da * silu)
        dz1_ref[...] = dz1
        dz3_ref[...] = dz3
        dn = (jnp.dot(dz1, w1_ref[...], preferred_element_type=F32)
              + jnp.dot(dz3, w3_ref[...], preferred_element_type=F32))
        x = h_ref[...]
        rstd = _rstd(x)
        xhat = x * rstd
        dg_ref[...] += jnp.broadcast_to(jnp.sum(dn * xhat, axis=0, keepdims=True), dg_ref.shape)
        dxh = dn * g_ref[...]
        o_ref[...] = dh_ref[...] + rstd * (dxh - xhat * jnp.mean(dxh * xhat, axis=-1, keepdims=True))

    row = pl.BlockSpec((tm, d), lambda i: (i, 0))
    col = pl.BlockSpec((tm, f), lambda i: (i, 0))
    whole = pl.BlockSpec(memory_space=pl.ANY)
    colshape = jax.ShapeDtypeStruct((t, f), BF16)
    return pl.pallas_call(
        kern, name=name, grid=(t // tm,),
        in_specs=[row, row, pl.BlockSpec((1, d), lambda i: (0, 0)), col, col, whole, whole, whole,
                  pl.BlockSpec((8, 128), lambda i: (0, 0))],
        out_specs=[row, pl.BlockSpec((8, d), lambda i: (0, 0)), col, col, col, row],
        out_shape=[jax.ShapeDtypeStruct((t, d), F32), jax.ShapeDtypeStruct((8, d), F32), colshape, colshape, colshape,
                   jax.ShapeDtypeStruct((t, d), BF16)],
        scratch_shapes=[pltpu.VMEM((f, d), BF16), pltpu.VMEM((f, d), BF16), pltpu.VMEM((f, d), BF16),
                        pltpu.SemaphoreType.DMA((3,))],
        compiler_params=_params("arbitrary"))(dh, h_in, g.reshape(1, d), z1, z3, w2, w1t, w3t, dep)


def _ffn_bwd(dh, h_in, g, n, z1, z3, w1t, w3t, w2, dep, tag):
    f = w2.shape[0]
    dh_in, dg, dz1, dz3, act, df = _ffn_bwd_x(dh, h_in, g, z1, z3, w1t, w3t, w2, dep, tag + "_bwd_x")
    dw1t = _mm_tn(dz1, n, BF16, tag + "_dw1", bm=_ffn_tile(f), tk=2048)
    dw3t = _mm_tn(dz3, n, BF16, tag + "_dw3", bm=_ffn_tile(f), tk=2048)
    dw2 = _mm_tn(act, df, BF16, tag + "_dw2", bm=_ffn_tile(f), tk=2048)
    return dh_in, dg, dw1t, dw3t, dw2


def _norm_fwd(h, g, dep, name):
    t, d = h.shape
    tm = min(512, t)

    def kern(h_ref, g_ref, dep_ref, o_ref):
        x = h_ref[...]
        o_ref[...] = _bf(x * _rstd(x) * g_ref[...])

    row = pl.BlockSpec((tm, d), lambda i: (i, 0))
    return pl.pallas_call(
        kern, name=name, grid=(t // tm,),
        in_specs=[row, pl.BlockSpec((1, d), lambda i: (0, 0)), pl.BlockSpec((8, 128), lambda i: (0, 0))],
        out_specs=row, out_shape=jax.ShapeDtypeStruct((t, d), BF16),
        compiler_params=_params("parallel"))(h, g.reshape(1, d), dep)


def _loss_bwd(h, target, g):
    t, d = h.shape
    tm = min(512, t)

    def kern(h_ref, t_ref, g_ref, dh_ref, dg_ref, loss_ref):
        @pl.when(pl.program_id(0) == 0)
        def _():
            dg_ref[...] = jnp.zeros_like(dg_ref)
            loss_ref[...] = jnp.zeros_like(loss_ref)

        x = h_ref[...]
        rstd = _rstd(x)
        xhat = x * rstd
        err = xhat * g_ref[...] - t_ref[...]
        row_loss = jnp.sum(err * err, axis=-1, keepdims=True) * (0.5 / d)
        loss_ref[...] += jnp.broadcast_to(jnp.sum(row_loss, axis=0, keepdims=True), loss_ref.shape)
        dy = err * (1.0 / d)
        dg_ref[...] += jnp.broadcast_to(jnp.sum(dy * xhat, axis=0, keepdims=True), dg_ref.shape)
        dxh = dy * g_ref[...]
        dh_ref[...] = rstd * (dxh - xhat * jnp.mean(dxh * xhat, axis=-1, keepdims=True))

    row = pl.BlockSpec((tm, d), lambda i: (i, 0))
    return pl.pallas_call(
        kern, name="loss_bwd", grid=(t // tm,),
        in_specs=[row, row, pl.BlockSpec((1, d), lambda i: (0, 0))],
        out_specs=[row, pl.BlockSpec((8, d), lambda i: (0, 0)), pl.BlockSpec((8, 128), lambda i: (0, 0))],
        out_shape=[jax.ShapeDtypeStruct((t, d), F32), jax.ShapeDtypeStruct((8, d), F32),
                   jax.ShapeDtypeStruct((8, 128), F32)],
        compiler_params=_params("arbitrary"))(h, target, g.reshape(1, d))


def _glu_fwd(m, wa, wb):
    t, d = m.shape
    c = wa.shape[1]
    tm, tc = min(512, t), c

    def kern(m_ref, wa_ref, wb_ref, a_ref, b_ref, glu_ref):
        mv = m_ref[...]
        a = jnp.dot(mv, wa_ref[...], preferred_element_type=F32)
        b = jnp.dot(mv, wb_ref[...], preferred_element_type=F32)
        a_ref[...] = _bf(a)
        b_ref[...] = _bf(b)
        glu_ref[...] = _bf(a * _sigmoid(b))

    col = pl.BlockSpec((tm, tc), lambda i, j: (i, j))
    wspec = pl.BlockSpec((d, tc), lambda i, j: (0, j))
    shape = jax.ShapeDtypeStruct((t, c), BF16)
    return pl.pallas_call(
        kern, name="conv_glu_fwd", grid=(t // tm, c // tc),
        in_specs=[pl.BlockSpec((tm, d), lambda i, j: (i, 0)), wspec, wspec], out_specs=[col, col, col],
        out_shape=[shape, shape, shape], compiler_params=_params("parallel", "parallel"))(m, wa, wb)


def _conv_tile(t):
    return min(256, t)


def _shift_copies(ext, shifted, rows):
    for s in range(8):
        shifted[s] = ext[pl.ds(s, rows), :]


def _shifted_rows(shifted, start, nrows):
    return shifted[start % 8, pl.ds(start - start % 8, nrows), :]


def _conv_fwd(glu, w_dw, g):
    t, c = glu.shape
    tm = _conv_tile(t)
    hb = tm // CONV_HALO

    def kern(cur_ref, halo_ref, w_ref, g_ref, cv_ref, s_ref, ext, shifted):
        i = pl.program_id(0)
        ext[0:CONV_HALO, :] = jnp.where(i > 0, halo_ref[...].astype(F32), 0.0)
        ext[CONV_HALO:tm + CONV_HALO, :] = cur_ref[...].astype(F32)
        ext[tm + CONV_HALO:, :] = jnp.zeros((8, c), F32)
        _shift_copies(ext, shifted, tm + CONV_HALO)
        gv = g_ref[...]
        for r0 in range(0, tm, CONV_ROWS):
            acc = jnp.zeros((CONV_ROWS, c), F32)
            for k in range(CONV_WIDTH):
                acc = acc + _shifted_rows(shifted, r0 + 2 + k, CONV_ROWS) * w_ref[k:k + 1, :]
            cv_ref[r0:r0 + CONV_ROWS, :] = acc
            rn = acc * _rstd(acc) * gv
            s_ref[r0:r0 + CONV_ROWS, :] = _bf(rn * _sigmoid(rn))

    row = pl.BlockSpec((tm, c), lambda i: (i, 0))
    return pl.pallas_call(
        kern, name="conv_fwd", grid=(t // tm,),
        in_specs=[row, pl.BlockSpec((CONV_HALO, c), lambda i: (jnp.maximum(i * hb - 1, 0), 0)),
                  pl.BlockSpec((CONV_HALO, c), lambda i: (0, 0)), pl.BlockSpec((1, c), lambda i: (0, 0))],
        out_specs=[row, row],
        out_shape=[jax.ShapeDtypeStruct((t, c), F32), jax.ShapeDtypeStruct((t, c), BF16)],
        scratch_shapes=[pltpu.VMEM((tm + CONV_HALO + 8, c), F32), pltpu.VMEM((8, tm + CONV_HALO, c), F32)],
        compiler_params=_params("parallel"))(glu, glu, w_dw, g.reshape(1, c))


def _conv_bwd_norm(dh, cv, w_pw2, g, dep):
    t, c = cv.shape
    tm = min(512, t)

    def kern(dh_ref, cv_ref, w_ref, g_ref, dep_ref, dcv_ref, dg_ref):
        @pl.when(pl.program_id(0) == 0)
        def _():
            dg_ref[...] = jnp.zeros_like(dg_ref)

        ds = lax.dot_general(_bf(dh_ref[...]), w_ref[...], (((1,), (1,)), ((), ())), preferred_element_type=F32)
        x = cv_ref[...]
        rstd = _rstd(x)
        xhat = x * rstd
        rn = xhat * g_ref[...]
        sig = _sigmoid(rn)
        drn = ds * (sig * (1.0 + rn * (1.0 - sig)))
        dg_ref[...] += jnp.broadcast_to(jnp.sum(drn * xhat, axis=0, keepdims=True), dg_ref.shape)
        dxh = drn * g_ref[...]
        dcv_ref[...] = rstd * (dxh - xhat * jnp.mean(dxh * xhat, axis=-1, keepdims=True))

    row = pl.BlockSpec((tm, c), lambda i: (i, 0))
    return pl.pallas_call(
        kern, name="conv_bwd_norm", grid=(t // tm,),
        in_specs=[pl.BlockSpec((tm, dh.shape[1]), lambda i: (i, 0)), row,
                  pl.BlockSpec(w_pw2.shape, lambda i: (0, 0)), pl.BlockSpec((1, c), lambda i: (0, 0)),
                  pl.BlockSpec((8, 128), lambda i: (0, 0))],
        out_specs=[row, pl.BlockSpec((8, c), lambda i: (0, 0))],
        out_shape=[jax.ShapeDtypeStruct((t, c), F32), jax.ShapeDtypeStruct((8, c), F32)],
        compiler_params=_params("arbitrary"))(dh, cv, w_pw2, g.reshape(1, c), dep)


def _conv_bwd_dw(dcv, glu, a, b, w_dw):
    t, c = dcv.shape
    tm = _conv_tile(t)
    hb = tm // CONV_HALO
    last = t // CONV_HALO - 1

    def kern(dcv_ref, dnext_ref, glu_ref, gprev_ref, a_ref, b_ref, w_ref, da_ref, db_ref, dw_ref,
             dext, gext, dshift, gshift):
        i = pl.program_id(0)

        @pl.when(i == 0)
        def _():
            dw_ref[...] = jnp.zeros_like(dw_ref)

        dext[0:tm, :] = dcv_ref[...]
        dext[tm:tm + CONV_HALO, :] = jnp.where(i < t // tm - 1, dnext_ref[...], 0.0)
        dext[tm + CONV_HALO:, :] = jnp.zeros((8, c), F32)
        gext[0:CONV_HALO, :] = jnp.where(i > 0, gprev_ref[...].astype(F32), 0.0)
        gext[CONV_HALO:tm + CONV_HALO, :] = glu_ref[...].astype(F32)
        gext[tm + CONV_HALO:, :] = jnp.zeros((8, c), F32)
        _shift_copies(dext, dshift, tm + CONV_HALO)
        _shift_copies(gext, gshift, tm + CONV_HALO)
        for r0 in range(0, tm, CONV_ROWS):
            acc = jnp.zeros((CONV_ROWS, c), F32)
            for k in range(CONV_WIDTH):
                acc = acc + _shifted_rows(dshift, r0 + CONV_WIDTH - 1 - k, CONV_ROWS) * w_ref[k:k + 1, :]
            av = a_ref[r0:r0 + CONV_ROWS, :].astype(F32)
            sig = _sigmoid(b_ref[r0:r0 + CONV_ROWS, :].astype(F32))
            da_ref[r0:r0 + CONV_ROWS, :] = _bf(acc * sig)
            db_ref[r0:r0 + CONV_ROWS, :] = _bf(acc * av * sig * (1.0 - sig))
        for k in range(CONV_WIDTH):
            acc = jnp.zeros((CONV_ROWS, c), F32)
            for r0 in range(0, tm, CONV_ROWS):
                acc = acc + _shifted_rows(gshift, r0 + 2 + k, CONV_ROWS) * dext[r0:r0 + CONV_ROWS, :]
            dw_ref[k:k + 1, :] += jnp.sum(acc, axis=0, keepdims=True)

    row = pl.BlockSpec((tm, c), lambda i: (i, 0))
    shape = jax.ShapeDtypeStruct((t, c), BF16)
    return pl.pallas_call(
        kern, name="conv_bwd_dw", grid=(t // tm,),
        in_specs=[row, pl.BlockSpec((CONV_HALO, c), lambda i: (jnp.minimum((i + 1) * hb, last), 0)),
                  row, pl.BlockSpec((CONV_HALO, c), lambda i: (jnp.maximum(i * hb - 1, 0), 0)),
                  row, row, pl.BlockSpec((CONV_HALO, c), lambda i: (0, 0))],
        out_specs=[row, row, pl.BlockSpec((CONV_HALO, c), lambda i: (0, 0))],
        out_shape=[shape, shape, jax.ShapeDtypeStruct((CONV_HALO, c), F32)],
        scratch_shapes=[pltpu.VMEM((tm + CONV_HALO + 8, c), F32), pltpu.VMEM((tm + CONV_HALO + 8, c), F32),
                        pltpu.VMEM((8, tm + CONV_HALO, c), F32), pltpu.VMEM((8, tm + CONV_HALO, c), F32)],
        compiler_params=_params("arbitrary"))(dcv, dcv, glu, glu, a, b, w_dw)


def _rope_tables(pos):
    t = pos.shape[0]
    tm = min(512, t)
    freq = (np.float32(ROPE_THETA) ** (np.float32(-2.0) * np.arange(ROPE // 2, dtype=np.float32)
                                       / np.float32(ROPE))).astype(np.float32)
    row = np.zeros((2, 128), np.float32)
    row[0, :ROPE] = np.concatenate([freq, freq])
    row[1, :ROPE] = 1.0

    def kern(pos_ref, f_ref, c_ref, s_ref):
        ang = pos_ref[...].astype(F32) * f_ref[0:1, :]
        mask = f_ref[1:2, :]
        c_ref[...] = jnp.cos(ang) * mask
        s_ref[...] = jnp.sin(ang) * mask

    out = pl.BlockSpec((tm, 128), lambda i: (i, 0))
    shape = jax.ShapeDtypeStruct((t, 128), F32)
    return pl.pallas_call(
        kern, name="rope_tables", grid=(t // tm,),
        in_specs=[pl.BlockSpec((tm, 1), lambda i: (i, 0)), pl.BlockSpec((2, 128), lambda i: (0, 0))],
        out_specs=[out, out], out_shape=[shape, shape], compiler_params=_params("parallel"))(pos, jnp.asarray(row))


def _mla_prep(a, gq, gkv, cs_c, cs_s):
    t = a.shape[0]
    tm = min(512, t)
    kv0, r0 = Q_LORA, Q_LORA + KV_LORA

    def kern(a_ref, gq_ref, gkv_ref, c_ref, s_ref, cq_ref, ckv_ref, kr_ref):
        aq = a_ref[:, 0:kv0]
        akv = a_ref[:, kv0:r0]
        ar = a_ref[:, r0:A_PAD]
        cq_ref[...] = _bf(aq * _rstd(aq) * gq_ref[...])
        ckv_ref[...] = _bf(akv * _rstd(akv) * gkv_ref[...])
        kr_ref[...] = _bf(ar * c_ref[...] + _rot(ar) * s_ref[...])

    def row(w):
        return pl.BlockSpec((tm, w), lambda i: (i, 0))

    def vec(w):
        return pl.BlockSpec((1, w), lambda i: (0, 0))

    return pl.pallas_call(
        kern, name="mla_prep", grid=(t // tm,),
        in_specs=[row(A_PAD), vec(Q_LORA), vec(KV_LORA), row(128), row(128)],
        out_specs=[row(Q_LORA), row(KV_LORA), row(128)],
        out_shape=[jax.ShapeDtypeStruct((t, Q_LORA), BF16), jax.ShapeDtypeStruct((t, KV_LORA), BF16),
                   jax.ShapeDtypeStruct((t, 128), BF16)],
        compiler_params=_params("parallel"))(a, gq.reshape(1, -1), gkv.reshape(1, -1), cs_c, cs_s)


def _mla_prep_bwd(a, dcq, dckv, dar, gq, gkv):
    t = a.shape[0]
    tm = min(512, t)
    kv0, r0 = Q_LORA, Q_LORA + KV_LORA

    def kern(a_ref, dcq_ref, dckv_ref, dar_ref, gq_ref, gkv_ref, da_ref, dgq_ref, dgkv_ref):
        @pl.when(pl.program_id(0) == 0)
        def _():
            dgq_ref[...] = jnp.zeros_like(dgq_ref)
            dgkv_ref[...] = jnp.zeros_like(dgkv_ref)

        def back(x, dy, g_ref, dg_ref):
            rstd = _rstd(x)
            xhat = x * rstd
            dg_ref[...] += jnp.broadcast_to(jnp.sum(dy * xhat, axis=0, keepdims=True), dg_ref.shape)
            dxh = dy * g_ref[...]
            return rstd * (dxh - xhat * jnp.mean(dxh * xhat, axis=-1, keepdims=True))

        da_ref[:, 0:kv0] = _bf(back(a_ref[:, 0:kv0], dcq_ref[...], gq_ref, dgq_ref))
        da_ref[:, kv0:r0] = _bf(back(a_ref[:, kv0:r0], dckv_ref[...], gkv_ref, dgkv_ref))
        da_ref[:, r0:A_PAD] = _bf(dar_ref[...])

    def row(w):
        return pl.BlockSpec((tm, w), lambda i: (i, 0))

    def vec(r, w):
        return pl.BlockSpec((r, w), lambda i: (0, 0))

    return pl.pallas_call(
        kern, name="mla_prep_bwd", grid=(t // tm,),
        in_specs=[row(A_PAD), row(Q_LORA), row(KV_LORA), row(128), vec(1, Q_LORA), vec(1, KV_LORA)],
        out_specs=[row(A_PAD), vec(8, Q_LORA), vec(8, KV_LORA)],
        out_shape=[jax.ShapeDtypeStruct((t, A_PAD), BF16), jax.ShapeDtypeStruct((8, Q_LORA), F32),
                   jax.ShapeDtypeStruct((8, KV_LORA), F32)],
        compiler_params=_params("arbitrary"))(a, dcq, dckv, dar, gq.reshape(1, -1), gkv.reshape(1, -1))


def _mla_qkv(cq, ckv, kr, cs_c, cs_s, wuq, wukv):
    t = cq.shape[0]
    tm = min(512, t)
    kvw = NOPE + V_HEAD

    def kern(cq_ref, ckv_ref, kr_ref, c_ref, s_ref, wq_ref, wkv_ref, q_ref, k_ref, v_ref):
        r = jnp.dot(cq_ref[...], wq_ref[...], preferred_element_type=F32)
        kv = jnp.dot(ckv_ref[...], wkv_ref[...], preferred_element_type=F32)
        cv, sv, krv = c_ref[...], s_ref[...], kr_ref[...]
        for h in range(HEADS):
            xr = r[:, h * HEAD_PAD + NOPE:(h + 1) * HEAD_PAD]
            q_ref[h, :, 0:NOPE] = _bf(r[:, h * HEAD_PAD:h * HEAD_PAD + NOPE] * ATTN_SCALE)
            q_ref[h, :, NOPE:] = _bf((xr * cv + _rot(xr) * sv) * ATTN_SCALE)
            k_ref[h, :, 0:NOPE] = _bf(kv[:, h * kvw:h * kvw + NOPE])
            k_ref[h, :, NOPE:] = krv
            v_ref[h] = _bf(kv[:, h * kvw + NOPE:(h + 1) * kvw])

    def row(w):
        return pl.BlockSpec((tm, w), lambda i: (i, 0))

    def heads(w):
        return pl.BlockSpec((HEADS, tm, w), lambda i: (0, i, 0))

    return pl.pallas_call(
        kern, name="mla_qkv", grid=(t // tm,),
        in_specs=[row(Q_LORA), row(KV_LORA), row(128), row(128), row(128),
                  pl.BlockSpec(wuq.shape, lambda i: (0, 0)), pl.BlockSpec(wukv.shape, lambda i: (0, 0))],
        out_specs=[heads(HEAD_PAD), heads(HEAD_PAD), heads(V_HEAD)],
        out_shape=[jax.ShapeDtypeStruct((HEADS, t, HEAD_PAD), BF16), jax.ShapeDtypeStruct((HEADS, t, HEAD_PAD), BF16),
                   jax.ShapeDtypeStruct((HEADS, t, V_HEAD), BF16)],
        compiler_params=_params("parallel"))(cq, ckv, kr, cs_c, cs_s, wuq, wukv)


def _mla_qkv_bwd(dq, dk, dv, cs_c, cs_s, wuq, wukv):
    t = dq.shape[1]
    tm = min(256, t)
    kvw = NOPE + V_HEAD

    def kern(dq_ref, dk_ref, dv_ref, c_ref, s_ref, wq_ref, wkv_ref, dr_ref, dkv_ref, dcq_ref, dckv_ref, dar_ref):
        cv, sv = c_ref[...], s_ref[...]
        dar = jnp.zeros_like(cv)
        for h in range(HEADS):
            dqx = dq_ref[h, :, NOPE:].astype(F32)
            dr_ref[:, h * HEAD_PAD:h * HEAD_PAD + NOPE] = _bf(dq_ref[h, :, 0:NOPE].astype(F32) * ATTN_SCALE)
            dr_ref[:, h * HEAD_PAD + NOPE:(h + 1) * HEAD_PAD] = _bf((dqx * cv + _rot_t(dqx * sv)) * ATTN_SCALE)
            dkx = dk_ref[h, :, NOPE:].astype(F32)
            dar = dar + (dkx * cv + _rot_t(dkx * sv))
            dkv_ref[:, h * kvw:h * kvw + NOPE] = dk_ref[h, :, 0:NOPE]
            dkv_ref[:, h * kvw + NOPE:(h + 1) * kvw] = dv_ref[h]
        dar_ref[...] = dar
        dcq_ref[...] = lax.dot_general(dr_ref[...], wq_ref[...], (((1,), (1,)), ((), ())),
                                       preferred_element_type=F32)
        dckv_ref[...] = lax.dot_general(dkv_ref[...], wkv_ref[...], (((1,), (1,)), ((), ())),
                                        preferred_element_type=F32)

    def row(w):
        return pl.BlockSpec((tm, w), lambda i: (i, 0))

    def heads(w):
        return pl.BlockSpec((HEADS, tm, w), lambda i: (0, i, 0))

    return pl.pallas_call(
        kern, name="mla_qkv_bwd", grid=(t // tm,),
        in_specs=[heads(HEAD_PAD), heads(HEAD_PAD), heads(V_HEAD), row(128), row(128),
                  pl.BlockSpec(wuq.shape, lambda i: (0, 0)), pl.BlockSpec(wukv.shape, lambda i: (0, 0))],
        out_specs=[row(HEADS * HEAD_PAD), row(HEADS * kvw), row(Q_LORA), row(KV_LORA), row(128)],
        out_shape=[jax.ShapeDtypeStruct((t, HEADS * HEAD_PAD), BF16), jax.ShapeDtypeStruct((t, HEADS * kvw), BF16),
                   jax.ShapeDtypeStruct((t, Q_LORA), F32), jax.ShapeDtypeStruct((t, KV_LORA), F32),
                   jax.ShapeDtypeStruct((t, 128), F32)],
        compiler_params=_params("parallel"))(dq, dk, dv, cs_c, cs_s, wuq, wukv)


def _attn_block(t):
    return 512 if t >= 4096 else 128


def _chunk_mask(bk, bq):
    kc = lax.broadcasted_iota(jnp.int32, (bk, bq), 0) // CHUNK
    qc = lax.broadcasted_iota(jnp.int32, (bk, bq), 1) // CHUNK
    return qc >= kc


def _flash_fwd(q, k, v):
    t = q.shape[1]
    bq = _attn_block(t)
    nq = t // bq
    nch = 2

    def kern(q_ref, k_ref, v_ref, o_ref, lse_ref, s_buf, p_buf, m_ref, l_ref, acc_ref):
        i = pl.program_id(1)
        queries = [q_ref[c * bq:(c + 1) * bq, :] for c in range(nch)]

        def block(j):
            rows = pl.ds(pl.multiple_of(j * bq, bq), bq)
            return k_ref[rows, :], v_ref[rows, :]

        def scores(kj, chain):
            return lax.dot_general(kj, queries[chain], (((1,), (1,)), ((), ())), preferred_element_type=F32)

        def softmax_block(chain, slot, vj):
            for c0 in range(0, bq, 128):
                cols = slice(c0, c0 + 128)
                s = s_buf[slot, chain, :, cols]
                m_old = m_ref[chain, 0:1, cols]
                m_new = jnp.maximum(m_old, jnp.max(s, axis=0, keepdims=True))
                alpha = jnp.exp(m_old - m_new)
                p = jnp.exp(s - m_new)
                l_ref[chain, 0:1, cols] = alpha * l_ref[chain, 0:1, cols] + jnp.sum(p, axis=0, keepdims=True)
                m_ref[chain, 0:1, cols] = m_new
                p_buf[chain, :, cols] = _bf(p)
                acc_ref[chain, :, cols] = acc_ref[chain, :, cols] * alpha
            acc_ref[chain] += lax.dot_general(vj, p_buf[chain], (((0,), (0,)), ((), ())),
                                              preferred_element_type=F32)

        m_ref[...] = jnp.full(m_ref.shape, -1e30, F32)
        l_ref[...] = jnp.zeros_like(l_ref)
        acc_ref[...] = jnp.zeros_like(acc_ref)
        mask = _chunk_mask(bq, bq)
        k0, v0 = block(nch * i)
        k1, v1 = block(nch * i + 1)
        s_buf[0, 0] = jnp.where(mask, scores(k0, 0), -1e30)
        s_buf[0, 1] = scores(k0, 1)
        s_buf[1, 1] = jnp.where(mask, scores(k1, 1), -1e30)
        softmax_block(0, 0, v0)
        softmax_block(1, 0, v0)
        softmax_block(1, 1, v1)
        kf = block(0)[0]
        for c in range(nch):
            s_buf[0, c] = scores(kf, c)

        def body(pair, carry):
            for cur in range(2):
                j = 2 * pair + cur
                kn = block(jnp.minimum(j + 1, jnp.maximum(nch * i - 1, 0)))[0]
                for c in range(nch):
                    s_buf[1 - cur, c] = scores(kn, c)
                vj = block(j)[1]
                for c in range(nch):
                    softmax_block(c, cur, vj)
            return carry

        lax.fori_loop(0, (nch // 2) * i, body, 0)
        for chain in range(nch):
            l = l_ref[chain, 0:1, :]
            o_ref[chain * bq:(chain + 1) * bq, :] = _bf((acc_ref[chain] / l).T)
            lse_ref[chain] = jnp.broadcast_to(m_ref[chain, 0:1, :] + jnp.log(l), (8, bq))

    return pl.pallas_call(
        kern, name="flash_fwd", grid=(HEADS, nq // nch),
        in_specs=[pl.BlockSpec((None, nch * bq, HEAD_PAD), lambda h, i: (h, i, 0)),
                  pl.BlockSpec((None, t, HEAD_PAD), lambda h, i: (h, 0, 0)),
                  pl.BlockSpec((None, t, V_HEAD), lambda h, i: (h, 0, 0))],
        out_specs=[pl.BlockSpec((nch * bq, V_HEAD), lambda h, i: (i, h)),
                   pl.BlockSpec((None, nch, 8, bq), lambda h, i: (h, i, 0, 0))],
        out_shape=[jax.ShapeDtypeStruct((t, HEADS * V_HEAD), BF16), jax.ShapeDtypeStruct((HEADS, nq, 8, bq), F32)],
        scratch_shapes=[pltpu.VMEM((2, nch, bq, bq), F32), pltpu.VMEM((nch, bq, bq), BF16),
                        pltpu.VMEM((nch, 8, bq), F32), pltpu.VMEM((nch, 8, bq), F32),
                        pltpu.VMEM((nch, V_HEAD, bq), F32)],
        compiler_params=_params("parallel", "arbitrary"))(q, k, v)


def _attn_delta(do, o):
    t = do.shape[0]
    bq = _attn_block(t)

    def kern(do_ref, o_ref, d_ref):
        for h in range(HEADS):
            cols = slice(h * V_HEAD, (h + 1) * V_HEAD)
            prod = do_ref[:, cols].astype(F32) * o_ref[:, cols].astype(F32)
            d_ref[h] = jnp.broadcast_to(jnp.sum(prod.T, axis=0, keepdims=True), (8, bq))

    blk = pl.BlockSpec((bq, HEADS * V_HEAD), lambda i: (i, 0))
    return pl.pallas_call(
        kern, name="attn_delta", grid=(t // bq,), in_specs=[blk, blk],
        out_specs=pl.BlockSpec((HEADS, None, 8, bq), lambda i: (0, i, 0, 0)),
        out_shape=jax.ShapeDtypeStruct((HEADS, t // bq, 8, bq), F32),
        compiler_params=_params("parallel"))(do, o)


def _flash_bwd(q, k, v, do, lse, delta):
    t = q.shape[1]
    bq = _attn_block(t)
    nq = t // bq

    def kern(q_ref, k_ref, v_ref, do_ref, lse_ref, del_ref, dq_out, dk_out, dv_out, dq_ref, dk_ref, dvt_ref):
        j = pl.program_id(1)

        @pl.when(j == 0)
        def _():
            dq_ref[...] = jnp.zeros_like(dq_ref)

        dk_ref[...] = jnp.zeros_like(dk_ref)
        dvt_ref[...] = jnp.zeros_like(dvt_ref)
        kj, vj = k_ref[...], v_ref[...]

        def step(i, masked):
            rows = pl.ds(pl.multiple_of(i * bq, bq), bq)
            qi, doi = q_ref[rows, :], do_ref[rows, :]
            st = lax.dot_general(kj, qi, (((1,), (1,)), ((), ())), preferred_element_type=F32)
            pt = jnp.exp(st - lse_ref[i][0:1, :])
            if masked:
                pt = jnp.where(_chunk_mask(bq, bq), pt, 0.0)
            dpt = lax.dot_general(vj, doi, (((1,), (1,)), ((), ())), preferred_element_type=F32)
            dst = _bf(pt * (dpt - del_ref[i][0:1, :]))
            dvt_ref[...] += lax.dot_general(doi, _bf(pt), (((0,), (1,)), ((), ())), preferred_element_type=F32)
            dk_ref[...] += jnp.dot(dst, qi, preferred_element_type=F32)
            dq_ref[rows, :] += lax.dot_general(dst, kj, (((0,), (0,)), ((), ())), preferred_element_type=F32)

        step(j, True)

        def body(pair, carry):
            step(j + 1 + 2 * pair, False)
            step(j + 2 + 2 * pair, False)
            return carry

        rest = nq - 1 - j
        lax.fori_loop(0, rest // 2, body, 0)

        @pl.when(rest % 2 == 1)
        def _():
            step(nq - 1, False)

        dk_out[...] = _bf(dk_ref[...])
        dv_out[...] = _bf(dvt_ref[...].T)

        @pl.when(j == nq - 1)
        def _():
            dq_out[...] = _bf(dq_ref[...])

    stat = pl.BlockSpec((None, nq, 8, bq), lambda h, j: (h, 0, 0, 0))
    return pl.pallas_call(
        kern, name="flash_bwd", grid=(HEADS, nq),
        in_specs=[pl.BlockSpec((None, t, HEAD_PAD), lambda h, j: (h, 0, 0)),
                  pl.BlockSpec((None, bq, HEAD_PAD), lambda h, j: (h, j, 0)),
                  pl.BlockSpec((None, bq, V_HEAD), lambda h, j: (h, j, 0)),
                  pl.BlockSpec((t, V_HEAD), lambda h, j: (0, h)), stat, stat],
        out_specs=[pl.BlockSpec((None, t, HEAD_PAD), lambda h, j: (h, 0, 0)),
                   pl.BlockSpec((None, bq, HEAD_PAD), lambda h, j: (h, j, 0)),
                   pl.BlockSpec((None, bq, V_HEAD), lambda h, j: (h, j, 0))],
        out_shape=[jax.ShapeDtypeStruct((HEADS, t, HEAD_PAD), BF16), jax.ShapeDtypeStruct((HEADS, t, HEAD_PAD), BF16),
                   jax.ShapeDtypeStruct((HEADS, t, V_HEAD), BF16)],
        scratch_shapes=[pltpu.VMEM((t, HEAD_PAD), F32), pltpu.VMEM((bq, HEAD_PAD), F32), pltpu.VMEM((V_HEAD, bq), F32)],
        compiler_params=_params("parallel", "arbitrary"))(q, k, v, do, lse, delta)


def _place():
    x, y, c = lax.axis_index("x"), lax.axis_index("y"), lax.axis_index("c")
    return x, y, c, [(1 - x, y), (x, 1 - y), (1 - x, 1 - y)]


def _all_gather_rows(block, name):
    m_per, n = block.shape

    def body(x_ref, out_ref, send_sems, recv_sems, local_sem):
        x, y, c, chips = _place()
        me, sibling = (x, y, c), (x, y, 1 - c)

        def rows(px, py, pc):
            return out_ref.at[pl.ds((4 * px + 2 * py + pc) * m_per, m_per), :]

        def copy(k, blk, to, src=None):
            return pltpu.make_async_remote_copy(
                src_ref=rows(*blk) if src is None else src, dst_ref=rows(*blk), send_sem=send_sems.at[k],
                recv_sem=recv_sems.at[k], device_id=to, device_id_type=MESH)

        mine = pltpu.make_async_copy(x_ref, rows(*me), local_sem)
        mine.start()
        first = [copy(0, me, sibling, src=x_ref)]
        first += [copy(1 + j, me, (*chip, c), src=x_ref) for j, chip in enumerate(chips)]
        for cp in first:
            cp.start()
        passed = [copy(4 + j, (*chip, c), sibling) for j, chip in enumerate(chips)]
        for j, chip in enumerate(chips):
            copy(1 + j, (*chip, c), me).wait_recv()
            passed[j].start()
        copy(0, sibling, me).wait_recv()
        for j, chip in enumerate(chips):
            copy(4 + j, (*chip, 1 - c), me).wait_recv()
        for cp in first + passed:
            cp.wait_send()
        mine.wait()

    return pl.pallas_call(
        body, name=name, out_shape=jax.ShapeDtypeStruct((8 * m_per, n), block.dtype),
        in_specs=[pl.BlockSpec(memory_space=pltpu.VMEM)], out_specs=pl.BlockSpec(memory_space=pltpu.VMEM),
        scratch_shapes=[pltpu.SemaphoreType.DMA((7,)), pltpu.SemaphoreType.DMA((7,)), pltpu.SemaphoreType.DMA],
        compiler_params=pltpu.CompilerParams(vmem_limit_bytes=VMEM_LIMIT_BYTES))(block)


HBM_SPEC = pl.BlockSpec(memory_space=pltpu.HBM)
SEM_SPEC = pl.BlockSpec(memory_space=pltpu.SEMAPHORE)
DATAFLOW = pltpu.SideEffectType.DATAFLOW_SIDE_EFFECTING


def _in_hbm(a):
    return pltpu.with_memory_space_constraint(a, pltpu.HBM)


def _chip_copies(ins, lands, send_sems, recv_sems, src_slot, half=False):
    n = len(ins)
    x, y, c, chips = _place()
    me = 2 * x + y

    def ends(w, chip):
        src = ins[w].at[2 * chip[0] + chip[1]] if src_slot else ins[w]
        if not half:
            return src, lands[w].at[me]
        rows = pl.ds(pl.multiple_of(c * (src.shape[0] // 2), 16), src.shape[0] // 2)
        return src.at[rows], lands[w].at[me, rows]

    copies = []
    for w in range(n):
        for p, chip in enumerate(chips):
            src, dst = ends(w, chip)
            copies.append(pltpu.make_async_remote_copy(
                src_ref=src, dst_ref=dst, send_sem=send_sems.at[p * n + w], recv_sem=recv_sems.at[p * n + w],
                device_id=(*chip, c), device_id_type=MESH))
    return copies


def _fill_halves(lands, name):
    n = len(lands)

    def body(*refs):
        bufs = refs[n:2 * n]
        send_sems, recv_sems = refs[2 * n:]
        x, y, c, chips = _place()
        copies = []
        for w in range(n):
            hr = bufs[w].shape[1] // 2
            for p, chip in enumerate(chips):
                part = bufs[w].at[2 * chip[0] + chip[1], pl.ds(pl.multiple_of(c * hr, 16), hr)]
                copies.append(pltpu.make_async_remote_copy(
                    src_ref=part, dst_ref=part, send_sem=send_sems.at[p * n + w], recv_sem=recv_sems.at[p * n + w],
                    device_id=(x, y, 1 - c), device_id_type=MESH))
        for cp in copies:
            cp.start()
        for cp in copies:
            cp.wait_send()
        for w in range(n):
            hr = bufs[w].shape[1] // 2
            for p, chip in enumerate(chips):
                part = bufs[w].at[2 * chip[0] + chip[1], pl.ds(pl.multiple_of((1 - c) * hr, 16), hr)]
                pltpu.make_async_remote_copy(
                    src_ref=part, dst_ref=part, send_sem=send_sems.at[p * n + w], recv_sem=recv_sems.at[p * n + w],
                    device_id=(x, y, 1 - c), device_id_type=MESH).wait_recv()

    any_spec = pl.BlockSpec(memory_space=pl.ANY)
    return list(pl.pallas_call(
        body, name=name, out_shape=[jax.ShapeDtypeStruct(a.shape, a.dtype) for a in lands],
        in_specs=[any_spec] * n, out_specs=[any_spec] * n, input_output_aliases={i: i for i in range(n)},
        scratch_shapes=[pltpu.SemaphoreType.DMA((3 * n,)), pltpu.SemaphoreType.DMA((3 * n,))])(*lands))


def _exchange_start(srcs, lands, src_slot, name, dep=None, half=False):
    n = len(srcs)
    first_out = 2 * n + (dep is not None)

    def body(*refs):
        for cp in _chip_copies(refs[:n], refs[n:2 * n], refs[first_out], refs[first_out + 1], src_slot, half):
            cp.start()
        token = refs[-1]
        token[...] = jnp.zeros_like(token)

    thru = [pltpu.HBM(a.shape, a.dtype) for a in list(srcs) + list(lands)]
    res = pl.pallas_call(
        body, name=name,
        out_shape=(pltpu.SemaphoreType.DMA((3 * n,)), pltpu.SemaphoreType.DMA((3 * n,)), *thru,
                   jax.ShapeDtypeStruct((8, 128), F32)),
        in_specs=[HBM_SPEC] * (2 * n) + ([pl.BlockSpec(memory_space=pl.ANY)] if dep is not None else []),
        out_specs=(SEM_SPEC, SEM_SPEC, *[HBM_SPEC] * (2 * n), pl.BlockSpec(memory_space=pltpu.VMEM)),
        input_output_aliases={i: 2 + i for i in range(2 * n)},
        compiler_params=pltpu.CompilerParams(has_side_effects=DATAFLOW))(
            *[_in_hbm(a) for a in srcs], *[_in_hbm(a) for a in lands], *([dep] if dep is not None else []))
    return (res[0], res[1], list(res[2:2 + n]), list(res[2 + n:2 + 2 * n])), res[-1]


def _exchange_wait(flight, after, src_slot, name, half=False):
    send_sems, recv_sems, srcs, lands = flight
    n = len(srcs)

    def body(*refs):
        for cp in _chip_copies(refs[:n], refs[n:2 * n], refs[2 * n], refs[2 * n + 1], src_slot, half):
            cp.wait_send()
            cp.wait_recv()

    thru = [pltpu.HBM(a.shape, a.dtype) for a in list(srcs) + list(lands)]
    res = pl.pallas_call(
        body, name=name, out_shape=thru,
        in_specs=[HBM_SPEC] * (2 * n) + [SEM_SPEC, SEM_SPEC, pl.BlockSpec(memory_space=pl.ANY)],
        out_specs=[HBM_SPEC] * (2 * n), input_output_aliases={i: i for i in range(2 * n)},
        compiler_params=pltpu.CompilerParams(has_side_effects=DATAFLOW))(*srcs, *lands, send_sems, recv_sems, after)
    return list(res[n:])


def _landing(own, me):
    return lax.dynamic_update_index_in_dim(lax.empty((4, *own.shape), own.dtype), own, me, 0)


def _swap_with_sibling(arrays, name):
    n = len(arrays)

    def body(*refs):
        ins, outs = refs[:n], refs[n:2 * n]
        send_sems, recv_sems = refs[2 * n:]
        x, y, c, _ = _place()
        copies = [pltpu.make_async_remote_copy(src_ref=ins[w], dst_ref=outs[w], send_sem=send_sems.at[w],
                                               recv_sem=recv_sems.at[w], device_id=(x, y, 1 - c), device_id_type=MESH)
                  for w in range(n)]
        for cp in copies:
            cp.start()
        for cp in copies:
            cp.wait()

    any_spec = pl.BlockSpec(memory_space=pl.ANY)
    return pl.pallas_call(
        body, name=name, out_shape=[jax.ShapeDtypeStruct(a.shape, a.dtype) for a in arrays],
        in_specs=[any_spec] * n, out_specs=[any_spec] * n,
        scratch_shapes=[pltpu.SemaphoreType.DMA((n,)), pltpu.SemaphoreType.DMA((n,))])(*arrays)


def _as_rows(a):
    return a.reshape(-1, a.shape[-1])


def _row_tile(r, c, budget_bytes=1 << 20):
    tr = r
    while tr % 16 == 0 and tr * c * 4 > budget_bytes:
        tr //= 2
    return tr


def _sum_slots(layers, nlayer, name, into=None):
    _, r, c = layers[0][1].shape
    tr = _row_tile(r, c)
    nt = r // tr
    acc = into
    for l, r4 in layers:
        def kern(r_ref, *rest):
            o_ref = rest[-1]
            o_ref[...] = _bf(((r_ref[0].astype(F32) + r_ref[1].astype(F32)) + r_ref[2].astype(F32))
                             + r_ref[3].astype(F32))

        out_spec = pl.BlockSpec((tr, c), lambda i, l=l: (l * nt + i, 0))
        first = acc is None
        acc = pl.pallas_call(
            kern, name=f"{name}_l{l}", grid=(nt,),
            in_specs=[pl.BlockSpec((4, tr, c), lambda i: (0, i, 0))]
            + ([] if first else [pl.BlockSpec(memory_space=pl.ANY)]),
            out_specs=out_spec, out_shape=jax.ShapeDtypeStruct((nlayer * r, c), BF16),
            input_output_aliases={} if first else {1: 0},
            compiler_params=_params("parallel"))(*([r4] if first else [r4, acc]))
    return acc


def _adamw(w, m, v, parts, name):
    r, c = w.shape
    tr = _row_tile(r, c, 3 << 19)
    npart = len(parts)
    c1 = 1.0 - ADAM_B1 ** ADAM_STEP
    c2 = 1.0 - ADAM_B2 ** ADAM_STEP

    def kern(*refs):
        w_ref, m_ref, v_ref = refs[:3]
        p_refs = refs[3:3 + npart]
        g_ref, d_ref, mo_ref, vo_ref = refs[3 + npart:]
        g = p_refs[0][...].astype(F32)
        for p in p_refs[1:]:
            g = g + p[...].astype(F32)
        mn = ADAM_B1 * m_ref[...] + (1.0 - ADAM_B1) * g
        vn = ADAM_B2 * v_ref[...] + (1.0 - ADAM_B2) * (g * g)
        g_ref[...] = g
        mo_ref[...] = mn
        vo_ref[...] = vn
        d_ref[...] = -ADAM_LR * ((mn / c1) / (jnp.sqrt(vn / c2) + ADAM_EPS) + ADAM_WD * w_ref[...])

    blk = pl.BlockSpec((tr, c), lambda i: (i, 0))
    shape = jax.ShapeDtypeStruct((r, c), F32)
    return pl.pallas_call(
        kern, name=name, grid=(r // tr,), in_specs=[blk] * (3 + npart), out_specs=[blk] * 4, out_shape=[shape] * 4,
        compiler_params=_params("parallel"))(w, m, v, *parts)


def _sum_devices(g8, name):
    _, r, c = g8.shape

    def kern(g_ref, o_ref):
        tot = g_ref[0]
        for dev in range(1, 8):
            tot = tot + g_ref[dev]
        o_ref[...] = tot

    return pl.pallas_call(
        kern, name=name, grid=(1,), in_specs=[pl.BlockSpec((8, r, c), lambda i: (0, 0, 0))],
        out_specs=pl.BlockSpec((r, c), lambda i: (0, 0)), out_shape=jax.ShapeDtypeStruct((r, c), F32),
        compiler_params=_params("arbitrary"))(g8)


def _pad_lanes(a, width):
    return jnp.pad(a, [(0, 0)] * (a.ndim - 1) + [(0, width - a.shape[-1])])


def kernel(x, positions, ffn_norm1, ffn1_w1, ffn1_w3, ffn1_w2, mix_norm, ffn_norm2, ffn2_w1, ffn2_w3, ffn2_w2, conv_w_pw1, conv_w_dw, conv_norm, conv_w_pw2, mla_w_a, mla_q_norm, mla_kv_norm, mla_w_uq, mla_w_ukv, mla_w_o, final_norm, loss_target, m_ffn_norm1, m_ffn1_w1, m_ffn1_w3, m_ffn1_w2, m_mix_norm, m_ffn_norm2, m_ffn2_w1, m_ffn2_w3, m_ffn2_w2, m_conv_w_pw1, m_conv_w_dw, m_conv_norm, m_conv_w_pw2, m_mla_w_a, m_mla_q_norm, m_mla_kv_norm, m_mla_w_uq, m_mla_w_ukv, m_mla_w_o, m_final_norm, v_ffn_norm1, v_ffn1_w1, v_ffn1_w3, v_ffn1_w2, v_mix_norm, v_ffn_norm2, v_ffn2_w1, v_ffn2_w3, v_ffn2_w2, v_conv_w_pw1, v_conv_w_dw, v_conv_norm, v_conv_w_pw2, v_mla_w_a, v_mla_q_norm, v_mla_kv_norm, v_mla_w_uq, v_mla_w_ukv, v_mla_w_o, v_final_norm):
    given = locals()
    return _step({nm: given[nm] for nm in INPUTS})


def _step(A):
    x = A['x'][0]
    target = A['loss_target'][0]
    t, d = x.shape
    pos = A['positions'].reshape(t, 1)
    me = 2 * lax.axis_index("x") + lax.axis_index("y")

    flipped = {f'ffn{k}_{w}' for k in (1, 2) for w in ('w1', 'w3')}
    P = {}
    for nm in BIG:
        for key in (nm, 'm_' + nm, 'v_' + nm):
            P[key] = jnp.swapaxes(A[key], 1, 2) if nm in flipped else A[key]

    def unflip(nm, a):
        return jnp.swapaxes(a, 1, 2) if nm in flipped else a

    ffn = [f'ffn{k}_{w}' for k in (1, 2) for w in ('w1', 'w3', 'w2')]
    gather_groups = [[(nm, 0) for nm in ffn[:3]],
                     [('conv_w_pw1', 0), ('conv_w_pw2', 0)] + [(nm, 0) for nm in ffn[3:]],
                     [(nm, 1) for nm in ffn[:3]] + [('mla_w_a', 0), ('mla_w_uq', 0), ('mla_w_ukv', 0), ('mla_w_o', 0)],
                     [(nm, 1) for nm in ffn[3:]]]
    halved = (0, 1)
    gather_flights = {}
    big = {}

    def gather_start(gi, dep):
        shards = [_bf(P[nm][l]) for nm, l in gather_groups[gi]]
        gather_flights[gi], token = _exchange_start(shards, [_landing(s, me) for s in shards], False,
                                                    f"gather_start_{gi}", dep, half=gi in halved)
        return token

    def gather_wait(gi, after):
        landed = _exchange_wait(gather_flights[gi], after, False, f"gather_wait_{gi}", half=gi in halved)
        if gi in halved:
            landed = _fill_halves(landed, f"gather_fill_{gi}")
        big.update(zip(gather_groups[gi], landed))
        return landed[0]

    dw_shard = A['conv_w_dw'][0]
    cw = dw_shard.shape[1]
    small = jnp.concatenate([
        jnp.pad(dw_shard, ((0, CONV_HALO - CONV_WIDTH), (0, 0))),
        jnp.pad(_pad_lanes(A['mla_q_norm'], cw), ((0, 7), (0, 0))),
        jnp.pad(_pad_lanes(A['mla_kv_norm'], cw), ((0, 7), (0, 0)))], axis=0)
    small = _all_gather_rows(small, "gather_small_weights").reshape(4, 2, 48, cw)[:, 0]
    w_dw = jnp.concatenate([small[j, :CONV_HALO] for j in range(4)], axis=1)
    gq = jnp.concatenate([small[j, CONV_HALO, :Q_LORA // 4] for j in range(4)])
    gkv = jnp.concatenate([small[j, CONV_HALO + 8, :KV_LORA // 4] for j in range(4)])

    def rows(nm, layer):
        g = big[nm, layer]
        return g.reshape(-1, g.shape[-1])

    ffn_w = {}

    def ffn_weights(k, l):
        ffn_w[k, l] = (rows(f'ffn{k}_w1', l), rows(f'ffn{k}_w3', l), rows(f'ffn{k}_w2', l))
        return ffn_w[k, l]

    token = gather_start(0, small)
    cs_c, cs_s = _rope_tables(pos)
    h0 = x
    token = gather_start(1, gather_wait(0, token))
    h1, n01, z01a, z01b = _ffn_fwd(h0, A['ffn_norm1'][0], *ffn_weights(1, 0), token, "ffn1_l0_fwd")
    token = gather_start(3, gather_start(2, gather_wait(1, h1)))
    pw1 = big['conv_w_pw1', 0]
    pw1_a = jnp.concatenate([pw1[0], pw1[1]], axis=1)
    pw1_b = jnp.concatenate([pw1[2], pw1[3]], axis=1)
    pw2 = rows('conv_w_pw2', 0)
    m0 = _norm_fwd(h1, A['mix_norm'][0], token, "mix_norm_l0")
    ca, cb, glu = _glu_fwd(m0, pw1_a, pw1_b)
    cv, cs = _conv_fwd(glu, w_dw, A['conv_norm'][0])
    h2 = _mm([(cs, pw2)], F32, "conv_pw2_fwd", res=h1)
    h3, n02, z02a, z02b = _ffn_fwd(h2, A['ffn_norm2'][0], *ffn_weights(2, 0), token, "ffn2_l0_fwd")
    gather_wait(2, h3)
    w_a = _pad_lanes(rows('mla_w_a', 0), A_PAD)
    wuq = _pad_lanes(big['mla_w_uq', 0].reshape(Q_LORA, HEADS, NOPE + ROPE), HEAD_PAD).reshape(Q_LORA, -1)
    wukv = big['mla_w_ukv', 0].reshape(KV_LORA, HEADS * (NOPE + V_HEAD))
    w_o = rows('mla_w_o', 0)
    h4, n11, z11a, z11b = _ffn_fwd(h3, A['ffn_norm1'][1], *ffn_weights(1, 1), token, "ffn1_l1_fwd")
    m1 = _norm_fwd(h4, A['mix_norm'][1], token, "mix_norm_l1")
    a_lat = _mm([(m1, w_a)], F32, "mla_down_fwd")
    cq, ckv, kr = _mla_prep(a_lat, gq, gkv, cs_c, cs_s)
    q, k, v = _mla_qkv(cq, ckv, kr, cs_c, cs_s, wuq, wukv)
    o, lse = _flash_fwd(q, k, v)
    h5 = _mm([(o, w_o)], F32, "mla_out_fwd", res=h4)
    gather_wait(3, h5)
    h6, n12, z12a, z12b = _ffn_fwd(h5, A['ffn_norm2'][1], *ffn_weights(2, 1), token, "ffn2_l1_fwd")

    def row_slots(g):
        return g.reshape(4, g.shape[0] // 4, g.shape[1])

    scatter_flights = []

    def scatter_start(named):
        srcs = [g for _, g in named]
        lands = [_landing(lax.dynamic_index_in_dim(g, me, 0, keepdims=False), me) for g in srcs]
        flight, token = _exchange_start(srcs, lands, True, f"scatter_start_{len(scatter_flights)}")
        scatter_flights.append(([key for key, _ in named], flight))
        return token

    def send_ffn(k, l, dw1t, dw3t, dw2):
        return scatter_start([((f'ffn{k}_w1', l), row_slots(dw1t)), ((f'ffn{k}_w3', l), row_slots(dw3t)),
                              ((f'ffn{k}_w2', l), row_slots(dw2))])

    dh6, dg_final, loss_part = _loss_bwd(h6, target, A['final_norm'])
    dh5, dg_n2_l1, *dws = _ffn_bwd(dh6, h5, A['ffn_norm2'][1], n12, z12a, z12b, *ffn_w[2, 1], loss_part, "ffn2_l1")
    token = send_ffn(2, 1, *dws)

    do = _mm([(dh5, w_o)], BF16, "mla_out_bwd", trans_b=True, dep=token)
    dw_o = _mm_tn(o, dh5, BF16, "mla_dw_o")
    delta = _attn_delta(do, o)
    dq, dk, dv = _flash_bwd(q, k, v, do, lse, delta)
    dr, dkv, dcq, dckv, dar = _mla_qkv_bwd(dq, dk, dv, cs_c, cs_s, wuq, wukv)
    dwuq = _mm_tn(cq, dr, BF16, "mla_dw_uq", bn=dr.shape[1] // 2)
    dwukv = _mm_tn(ckv, dkv, BF16, "mla_dw_ukv", bn=dkv.shape[1] // 2)
    da_lat, dgq, dgkv = _mla_prep_bwd(a_lat, dcq, dckv, dar, gq, gkv)
    dw_a = _mm_tn(m1, da_lat, BF16, "mla_dw_a")
    token = scatter_start([
        (('mla_w_a', 0), row_slots(dw_a[:, :Q_LORA + KV_LORA + ROPE])),
        (('mla_w_uq', 0), dwuq.reshape(4, Q_LORA // 4, HEADS, HEAD_PAD)[..., :NOPE + ROPE]),
        (('mla_w_ukv', 0), dwukv.reshape(4, KV_LORA // 4, HEADS, NOPE + V_HEAD)),
        (('mla_w_o', 0), row_slots(dw_o))])
    dh4, dg_mix_l1 = _mm_normbwd([(da_lat, w_a)], h4, A['mix_norm'][1], dh5, token, "mla_down_bwd")

    dh3, dg_n1_l1, *dws = _ffn_bwd(dh4, h3, A['ffn_norm1'][1], n11, z11a, z11b, *ffn_w[1, 1], token, "ffn1_l1")
    token = send_ffn(1, 1, *dws)
    dh2, dg_n2_l0, *dws = _ffn_bwd(dh3, h2, A['ffn_norm2'][0], n02, z02a, z02b, *ffn_w[2, 0], token, "ffn2_l0")
    token = send_ffn(2, 0, *dws)

    dcv, dg_conv = _conv_bwd_norm(dh2, cv, pw2, A['conv_norm'][0], token)
    dw_pw2 = _mm_tn(cs, dh2, BF16, "conv_dw_pw2")
    dca, dcb, ddw = _conv_bwd_dw(dcv, glu, ca, cb, w_dw)
    dpw1_a = _mm_tn(m0, dca, BF16, "conv_dw_pw1a")
    dpw1_b = _mm_tn(m0, dcb, BF16, "conv_dw_pw1b")
    half = dpw1_a.shape[1] // 2
    token = scatter_start([
        (('conv_w_pw1', 0), jnp.stack([dpw1_a[:, :half], dpw1_a[:, half:], dpw1_b[:, :half], dpw1_b[:, half:]])),
        (('conv_w_pw2', 0), row_slots(dw_pw2))])
    dh1, dg_mix_l0 = _mm_normbwd([(dca, pw1_a), (dcb, pw1_b)], h1, A['mix_norm'][0], dh2, token, "conv_pw1_bwd")

    dx, dg_n1_l0, *dws = _ffn_bwd(dh1, h0, A['ffn_norm1'][0], n01, z01a, z01b, *ffn_w[1, 0], token, "ffn1_l0")
    last_sent = send_ffn(1, 0, *dws)
    out = {}

    qkv_row = jnp.concatenate([dgq, dgkv, jnp.zeros((8, d - Q_LORA - KV_LORA), F32)], axis=1)
    loss_row = _pad_lanes(loss_part, d)
    small_g = jnp.concatenate([dg_n1_l0, dg_n1_l1, dg_mix_l0, dg_mix_l1, dg_n2_l0, dg_n2_l1, dg_conv, dg_final,
                               qkv_row, loss_row, ddw], axis=0)
    nrow = small_g.shape[0]
    tot = _sum_devices(_all_gather_rows(small_g, "gather_small_grads").reshape(8, nrow, d), "sum_small_grads")
    loss = tot[72, 0]
    q_shard = lax.dynamic_slice_in_dim(tot[64, :Q_LORA], me * (Q_LORA // 4), Q_LORA // 4)
    kv_shard = lax.dynamic_slice_in_dim(tot[64, Q_LORA:Q_LORA + KV_LORA], me * (KV_LORA // 4), KV_LORA // 4)
    dw_shard_g = lax.dynamic_slice_in_dim(tot[80:80 + CONV_WIDTH], me * cw, cw, axis=1)
    small_grads = {
        'ffn_norm1': jnp.stack([tot[0], tot[8]]), 'mix_norm': jnp.stack([tot[16], tot[24]]),
        'ffn_norm2': jnp.stack([tot[32], tot[40]]), 'conv_norm': tot[48][None], 'final_norm': tot[56],
        'mla_q_norm': q_shard[None], 'mla_kv_norm': kv_shard[None], 'conv_w_dw': dw_shard_g[None],
    }
    for nm, g in small_grads.items():
        res = _adamw(_as_rows(A[nm]) if A[nm].ndim > 1 else A[nm].reshape(1, -1),
                     A['m_' + nm].reshape(-1, A[nm].shape[-1]), A['v_' + nm].reshape(-1, A[nm].shape[-1]),
                     [g.reshape(-1, A[nm].shape[-1])], "adamw_" + nm)
        out[nm] = [r.reshape(A[nm].shape) for r in res]

    received = {}
    after = last_sent

    def scatter_wait(si, after):
        keys, flight = scatter_flights[si]
        landed = _exchange_wait(flight, after, True, f"scatter_wait_{si}")
        received.update(zip(keys, landed))
        return landed[0]

    def slots(nm, l):
        return received[nm, l].reshape(4, -1, received[nm, l].shape[-1])

    def finish(names, sums, tag):
        for nm, mine, theirs in zip(names, sums, _swap_with_sibling(sums, "swap_with_sibling_" + tag)):
            res = _adamw(_as_rows(P[nm]), _as_rows(P['m_' + nm]), _as_rows(P['v_' + nm]), [mine, theirs],
                         "adamw_" + nm)
            out[nm] = [unflip(nm, r.reshape(P[nm].shape)) for r in res]
        return res[1]

    last = len(scatter_flights) - 1
    for si in range(last):
        after = scatter_wait(si, after)
    late = ffn[:3]
    early = [nm for nm in BIG if nm not in late]
    late_l1 = [_sum_slots([(1, slots(nm, 1))], 2, "sum_" + nm) for nm in late]
    after = finish(early, [_sum_slots([(l, slots(nm, l)) for l in range(A[nm].shape[0])], A[nm].shape[0],
                                      "sum_" + nm) for nm in early], "early")
    scatter_wait(last, after)
    finish(late, [_sum_slots([(0, slots(nm, 0))], 2, "sum_" + nm, into=part) for nm, part in zip(late, late_l1)],
           "late")

    return (loss, dx[None], *[out[nm][0] for nm in WEIGHTS], *[out[nm][1] for nm in WEIGHTS],
            *[out[nm][2] for nm in WEIGHTS], *[out[nm][3] for nm in WEIGHTS])
```

```python
import jax
import jax.numpy as jnp
import numpy as np
from jax import lax
from jax.experimental import pallas as pl
from jax.experimental.pallas import tpu as pltpu

F32 = jnp.float32
BF16 = jnp.bfloat16
MESH = pl.DeviceIdType.MESH

RMS_EPS = 1e-6
HEADS = 8
NOPE = 128
ROPE = 64
HEAD_PAD = 256
V_HEAD = 128
Q_LORA = 512
KV_LORA = 256
A_PAD = 896
CHUNK = 64
CONV_WIDTH = 31
CONV_HALO = 32
CONV_ROWS = 16
ROPE_THETA = 10000.0
ATTN_SCALE = (NOPE + ROPE) ** -0.5
FFN_RES = 0.5

ADAM_LR = 0.001
ADAM_B1 = 0.9
ADAM_B2 = 0.999
ADAM_EPS = 1e-08
ADAM_WD = 0.01
ADAM_STEP = 10

VMEM_LIMIT_BYTES = 56 * 1024 * 1024

WEIGHTS = ['ffn_norm1', 'ffn1_w1', 'ffn1_w3', 'ffn1_w2', 'mix_norm', 'ffn_norm2', 'ffn2_w1', 'ffn2_w3', 'ffn2_w2',
           'conv_w_pw1', 'conv_w_dw', 'conv_norm', 'conv_w_pw2', 'mla_w_a', 'mla_q_norm', 'mla_kv_norm', 'mla_w_uq',
           'mla_w_ukv', 'mla_w_o', 'final_norm']
INPUTS = (['x', 'positions'] + WEIGHTS + ['loss_target'] + ['m_' + w for w in WEIGHTS] + ['v_' + w for w in WEIGHTS])
BIG = ['ffn1_w1', 'ffn1_w3', 'ffn1_w2', 'ffn2_w1', 'ffn2_w3', 'ffn2_w2', 'conv_w_pw1', 'conv_w_pw2', 'mla_w_a',
       'mla_w_uq', 'mla_w_ukv', 'mla_w_o']


def _params(*sem):
    return pltpu.CompilerParams(dimension_semantics=sem, vmem_limit_bytes=VMEM_LIMIT_BYTES)


def _bf(v):
    return v.astype(BF16)


def _rstd(x):
    return lax.rsqrt(jnp.mean(x * x, axis=-1, keepdims=True) + RMS_EPS)


def _sigmoid(x):
    return jax.nn.sigmoid(x)


def _rot(x):
    lane = lax.broadcasted_iota(jnp.int32, x.shape, 1)
    return jnp.where(lane < ROPE // 2, -pltpu.roll(x, 128 - ROPE // 2, 1), pltpu.roll(x, ROPE // 2, 1))


def _rot_t(y):
    lane = lax.broadcasted_iota(jnp.int32, y.shape, 1)
    return jnp.where(lane < ROPE // 2, pltpu.roll(y, 128 - ROPE // 2, 1), -pltpu.roll(y, ROPE // 2, 1))


def _pair_sum(a_refs, b_refs, trans_b):
    tot = None
    for a_r, b_r in zip(a_refs, b_refs):
        a, b = _bf(a_r[...]), _bf(b_r[...])
        if trans_b:
            d = lax.dot_general(a, b, (((1,), (1,)), ((), ())), preferred_element_type=F32)
        else:
            d = jnp.dot(a, b, preferred_element_type=F32)
        tot = d if tot is None else tot + d
    return tot


def _mm(pairs, out_dtype, name, *, trans_b=False, tm=512, tn=None, tk=None, res=None, dep=None):
    m, k = pairs[0][0].shape
    n = pairs[0][1].shape[0] if trans_b else pairs[0][1].shape[1]
    tm, tn, tk = min(tm, m), tn or n, tk or k
    nk, npair = k // tk, len(pairs)

    def kern(*refs):
        a_refs, b_refs = refs[:npair], refs[npair:2 * npair]
        rest = list(refs[2 * npair:])
        res_ref = rest.pop(0) if res is not None else None
        if dep is not None:
            rest.pop(0)
        o_ref = rest.pop(0)

        def finish(acc):
            if res_ref is not None:
                acc = res_ref[...] + acc
            o_ref[...] = acc.astype(o_ref.dtype)

        if nk == 1:
            finish(_pair_sum(a_refs, b_refs, trans_b))
        else:
            acc_ref = rest.pop(0)
            kk = pl.program_id(2)

            @pl.when(kk == 0)
            def _():
                acc_ref[...] = jnp.zeros_like(acc_ref)

            acc_ref[...] += _pair_sum(a_refs, b_refs, trans_b)

            @pl.when(kk == nk - 1)
            def _():
                finish(acc_ref[...])

    a_spec = pl.BlockSpec((tm, tk), lambda i, j, kk: (i, kk))
    b_spec = (pl.BlockSpec((tn, tk), lambda i, j, kk: (j, kk)) if trans_b
              else pl.BlockSpec((tk, tn), lambda i, j, kk: (kk, j)))
    io_spec = pl.BlockSpec((tm, tn), lambda i, j, kk: (i, j))
    in_specs = ([a_spec] * npair + [b_spec] * npair + ([io_spec] if res is not None else [])
                + ([pl.BlockSpec((8, 128), lambda i, j, kk: (0, 0))] if dep is not None else []))
    args = ([p[0] for p in pairs] + [p[1] for p in pairs] + ([res] if res is not None else [])
            + ([dep] if dep is not None else []))
    return pl.pallas_call(
        kern, name=name, grid=(m // tm, n // tn, nk), in_specs=in_specs, out_specs=io_spec,
        out_shape=jax.ShapeDtypeStruct((m, n), out_dtype),
        scratch_shapes=[pltpu.VMEM((tm, tn), F32)] if nk > 1 else [],
        compiler_params=_params("parallel", "parallel", "arbitrary"))(*args)


def _mm_normbwd(pairs, h, g, dres, dep, name, *, tm=512, tk=None):
    m, k = pairs[0][0].shape
    d = pairs[0][1].shape[0]
    tm, tk = min(tm, m), tk or k
    nk, npair = k // tk, len(pairs)

    def kern(*refs):
        a_refs, b_refs = refs[:npair], refs[npair:2 * npair]
        h_ref, g_ref, dres_ref, _, o_ref, dg_ref, acc_ref = refs[2 * npair:]
        i, kk = pl.program_id(0), pl.program_id(1)

        @pl.when(jnp.logical_and(i == 0, kk == 0))
        def _():
            dg_ref[...] = jnp.zeros_like(dg_ref)

        @pl.when(kk == 0)
        def _():
            acc_ref[...] = jnp.zeros_like(acc_ref)

        acc_ref[...] += _pair_sum(a_refs, b_refs, True)

        @pl.when(kk == nk - 1)
        def _():
            dn = acc_ref[...]
            x = h_ref[...]
            rstd = _rstd(x)
            xhat = x * rstd
            dg_ref[...] += jnp.broadcast_to(jnp.sum(dn * xhat, axis=0, keepdims=True), dg_ref.shape)
            dxh = dn * g_ref[...]
            dx = rstd * (dxh - xhat * jnp.mean(dxh * xhat, axis=-1, keepdims=True))
            o_ref[...] = dres_ref[...] + dx

    row = pl.BlockSpec((tm, d), lambda i, kk: (i, 0))
    in_specs = ([pl.BlockSpec((tm, tk), lambda i, kk: (i, kk))] * npair
                + [pl.BlockSpec((d, tk), lambda i, kk: (0, kk))] * npair
                + [row, pl.BlockSpec((1, d), lambda i, kk: (0, 0)), row, pl.BlockSpec((8, 128), lambda i, kk: (0, 0))])
    return pl.pallas_call(
        kern, name=name, grid=(m // tm, nk), in_specs=in_specs,
        out_specs=[row, pl.BlockSpec((8, d), lambda i, kk: (0, 0))],
        out_shape=[jax.ShapeDtypeStruct((m, d), F32), jax.ShapeDtypeStruct((8, d), F32)],
        scratch_shapes=[pltpu.VMEM((tm, d), F32)],
        compiler_params=_params("arbitrary", "arbitrary"))(
            *[p[0] for p in pairs], *[p[1] for p in pairs], h, g.reshape(1, d), dres, dep)


def _mm_tn(a, b, out_dtype, name, *, bm=None, bn=None, tk=1024):
    t, m = a.shape
    batched = b.ndim == 3
    n = b.shape[-1]
    nb = b.shape[0] if batched else 1
    bm, bn, tk = bm or m, bn or n, min(tk, t)
    nk = t // tk

    def kern(a_ref, b_ref, o_ref, acc_ref):
        kk = pl.program_id(3)

        @pl.when(kk == 0)
        def _():
            acc_ref[...] = jnp.zeros_like(acc_ref)

        acc_ref[...] += lax.dot_general(_bf(a_ref[...]), _bf(b_ref[...]), (((0,), (0,)), ((), ())),
                                        preferred_element_type=F32)

        @pl.when(kk == nk - 1)
        def _():
            o_ref[...] = acc_ref[...].astype(o_ref.dtype)

    a_spec = pl.BlockSpec((tk, bm), lambda h, i, j, kk: (kk, i))
    if batched:
        b_spec = pl.BlockSpec((None, tk, bn), lambda h, i, j, kk: (h, kk, j))
        o_spec = pl.BlockSpec((None, bm, bn), lambda h, i, j, kk: (h, i, j))
        out_shape = jax.ShapeDtypeStruct((nb, m, n), out_dtype)
    else:
        b_spec = pl.BlockSpec((tk, bn), lambda h, i, j, kk: (kk, j))
        o_spec = pl.BlockSpec((bm, bn), lambda h, i, j, kk: (i, j))
        out_shape = jax.ShapeDtypeStruct((m, n), out_dtype)
    return pl.pallas_call(
        kern, name=name, grid=(nb, m // bm, n // bn, nk), in_specs=[a_spec, b_spec], out_specs=o_spec,
        out_shape=out_shape, scratch_shapes=[pltpu.VMEM((bm, bn), F32)],
        compiler_params=_params("parallel", "parallel", "parallel", "arbitrary"))(a, b)


def _ffn_tile(f):
    return f // 2 if (f // 2) % 128 == 0 else f


def _ffn_fwd(h, g, w1t, w3t, w2, dep, name):
    t, d = h.shape
    f = w1t.shape[0]
    tm = min(256, t)
    nt = (((1,), (1,)), ((), ()))

    def kern(h_ref, g_ref, w1_hbm, w3_hbm, w2_hbm, dep_ref, ho_ref, n_ref, z1_ref, z3_ref,
             w1_ref, w3_ref, w2_ref, sems):
        @pl.when(pl.program_id(0) == 0)
        def _():
            copies = [pltpu.make_async_copy(src, dst, sems.at[k]) for k, (src, dst) in
                      enumerate(((w1_hbm, w1_ref), (w3_hbm, w3_ref), (w2_hbm, w2_ref)))]
            for cp in copies:
                cp.start()
            for cp in copies:
                cp.wait()

        x = h_ref[...]
        n = _bf(x * _rstd(x) * g_ref[...])
        n_ref[...] = n
        z1 = lax.dot_general(n, w1_ref[...], nt, preferred_element_type=F32)
        z3 = lax.dot_general(n, w3_ref[...], nt, preferred_element_type=F32)
        z1_ref[...] = _bf(z1)
        z3_ref[...] = _bf(z3)
        act = _bf(z1 * _sigmoid(z1) * z3)
        ho_ref[...] = x + FFN_RES * jnp.dot(act, w2_ref[...], preferred_element_type=F32)

    row = pl.BlockSpec((tm, d), lambda i: (i, 0))
    col = pl.BlockSpec((tm, f), lambda i: (i, 0))
    whole = pl.BlockSpec(memory_space=pl.ANY)
    return pl.pallas_call(
        kern, name=name, grid=(t // tm,),
        in_specs=[row, pl.BlockSpec((1, d), lambda i: (0, 0)), whole, whole, whole,
                  pl.BlockSpec((8, 128), lambda i: (0, 0))],
        out_specs=[row, row, col, col],
        out_shape=[jax.ShapeDtypeStruct((t, d), F32), jax.ShapeDtypeStruct((t, d), BF16),
                   jax.ShapeDtypeStruct((t, f), BF16), jax.ShapeDtypeStruct((t, f), BF16)],
        scratch_shapes=[pltpu.VMEM((f, d), BF16), pltpu.VMEM((f, d), BF16), pltpu.VMEM((f, d), BF16),
                        pltpu.SemaphoreType.DMA((3,))],
        compiler_params=_params("arbitrary"))(h, g.reshape(1, d), w1t, w3t, w2, dep)


def _ffn_bwd_x(dh, h_in, g, z1, z3, w1t, w3t, w2, dep, name):
    t, d = dh.shape
    f = z1.shape[1]
    tm = min(256, t)

    def kern(dh_ref, h_ref, g_ref, z1_ref, z3_ref, w2_hbm, w1_hbm, w3_hbm, dep_ref,
             o_ref, dg_ref, dz1_ref, dz3_ref, a_ref, df_ref, w2_ref, w1_ref, w3_ref, sems):
        @pl.when(pl.program_id(0) == 0)
        def _():
            copies = [pltpu.make_async_copy(src, dst, sems.at[k]) for k, (src, dst) in
                      enumerate(((w2_hbm, w2_ref), (w1_hbm, w1_ref), (w3_hbm, w3_ref)))]
            for cp in copies:
                cp.start()
            dg_ref[...] = jnp.zeros_like(dg_ref)
            for cp in copies:
                cp.wait()

        df = _bf(FFN_RES * dh_ref[...])
        df_ref[...] = df
        da = lax.dot_general(df, w2_ref[...], (((1,), (1,)), ((), ())), preferred_element_type=F32)
        z1v, z3v = z1_ref[...].astype(F32), z3_ref[...].astype(F32)
        sig = _sigmoid(z1v)
        silu = z1v * sig
        a_ref[...] = _bf(silu * z3v)
        dz1 = _bf(da * z3v * (sig * (1.0 + z1v * (1.0 - sig))))
        dz3 = _bf(da * silu)
        dz1_ref[...] = dz1
        dz3_ref[...] = dz3
        dn = (jnp.dot(dz1, w1_ref[...], preferred_element_type=F32)
              + jnp.dot(dz3, w3_ref[...], preferred_element_type=F32))
        x = h_ref[...]
        rstd = _rstd(x)
        xhat = x * rstd
        dg_ref[...] += jnp.broadcast_to(jnp.sum(dn * xhat, axis=0, keepdims=True), dg_ref.shape)
        dxh = dn * g_ref[...]
        o_ref[...] = dh_ref[...] + rstd * (dxh - xhat * jnp.mean(dxh * xhat, axis=-1, keepdims=True))

    row = pl.BlockSpec((tm, d), lambda i: (i, 0))
    col = pl.BlockSpec((tm, f), lambda i: (i, 0))
    whole = pl.BlockSpec(memory_space=pl.ANY)
    colshape = jax.ShapeDtypeStruct((t, f), BF16)
    return pl.pallas_call(
        kern, name=name, grid=(t // tm,),
        in_specs=[row, row, pl.BlockSpec((1, d), lambda i: (0, 0)), col, col, whole, whole, whole,
                  pl.BlockSpec((8, 128), lambda i: (0, 0))],
        out_specs=[row, pl.BlockSpec((8, d), lambda i: (0, 0)), col, col, col, row],
        out_shape=[jax.ShapeDtypeStruct((t, d), F32), jax.ShapeDtypeStruct((8, d), F32), colshape, colshape, colshape,
                   jax.ShapeDtypeStruct((t, d), BF16)],
        scratch_shapes=[pltpu.VMEM((f, d), BF16), pltpu.VMEM((f, d), BF16), pltpu.VMEM((f, d), BF16),
                        pltpu.SemaphoreType.DMA((3,))],
        compiler_params=_params("arbitrary"))(dh, h_in, g.reshape(1, d), z1, z3, w2, w1t, w3t, dep)


def _ffn_bwd(dh, h_in, g, n, z1, z3, w1t, w3t, w2, dep, tag):
    f = w2.shape[0]
    dh_in, dg, dz1, dz3, act, df = _ffn_bwd_x(dh, h_in, g, z1, z3, w1t, w3t, w2, dep, tag + "_bwd_x")
    dw1t = _mm_tn(dz1, n, BF16, tag + "_dw1", bm=_ffn_tile(f), tk=2048)
    dw3t = _mm_tn(dz3, n, BF16, tag + "_dw3", bm=_ffn_tile(f), tk=2048)
    dw2 = _mm_tn(act, df, BF16, tag + "_dw2", bm=_ffn_tile(f), tk=2048)
    return dh_in, dg, dw1t, dw3t, dw2


def _norm_fwd(h, g, dep, name):
    t, d = h.shape
    tm = min(512, t)

    def kern(h_ref, g_ref, dep_ref, o_ref):
        x = h_ref[...]
        o_ref[...] = _bf(x * _rstd(x) * g_ref[...])

    row = pl.BlockSpec((tm, d), lambda i: (i, 0))
    return pl.pallas_call(
        kern, name=name, grid=(t // tm,),
        in_specs=[row, pl.BlockSpec((1, d), lambda i: (0, 0)), pl.BlockSpec((8, 128), lambda i: (0, 0))],
        out_specs=row, out_shape=jax.ShapeDtypeStruct((t, d), BF16),
        compiler_params=_params("parallel"))(h, g.reshape(1, d), dep)


def _loss_bwd(h, target, g):
    t, d = h.shape
    tm = min(512, t)

    def kern(h_ref, t_ref, g_ref, dh_ref, dg_ref, loss_ref):
        @pl.when(pl.program_id(0) == 0)
        def _():
            dg_ref[...] = jnp.zeros_like(dg_ref)
            loss_ref[...] = jnp.zeros_like(loss_ref)

        x = h_ref[...]
        rstd = _rstd(x)
        xhat = x * rstd
        err = xhat * g_ref[...] - t_ref[...]
        row_loss = jnp.sum(err * err, axis=-1, keepdims=True) * (0.5 / d)
        loss_ref[...] += jnp.broadcast_to(jnp.sum(row_loss, axis=0, keepdims=True), loss_ref.shape)
        dy = err * (1.0 / d)
        dg_ref[...] += jnp.broadcast_to(jnp.sum(dy * xhat, axis=0, keepdims=True), dg_ref.shape)
        dxh = dy * g_ref[...]
        dh_ref[...] = rstd * (dxh - xhat * jnp.mean(dxh * xhat, axis=-1, keepdims=True))

    row = pl.BlockSpec((tm, d), lambda i: (i, 0))
    return pl.pallas_call(
        kern, name="loss_bwd", grid=(t // tm,),
        in_specs=[row, row, pl.BlockSpec((1, d), lambda i: (0, 0))],
        out_specs=[row, pl.BlockSpec((8, d), lambda i: (0, 0)), pl.BlockSpec((8, 128), lambda i: (0, 0))],
        out_shape=[jax.ShapeDtypeStruct((t, d), F32), jax.ShapeDtypeStruct((8, d), F32),
                   jax.ShapeDtypeStruct((8, 128), F32)],
        compiler_params=_params("arbitrary"))(h, target, g.reshape(1, d))


def _glu_fwd(m, wa, wb):
    t, d = m.shape
    c = wa.shape[1]
    tm, tc = min(512, t), c

    def kern(m_ref, wa_ref, wb_ref, a_ref, b_ref, glu_ref):
        mv = m_ref[...]
        a = jnp.dot(mv, wa_ref[...], preferred_element_type=F32)
        b = jnp.dot(mv, wb_ref[...], preferred_element_type=F32)
        a_ref[...] = _bf(a)
        b_ref[...] = _bf(b)
        glu_ref[...] = _bf(a * _sigmoid(b))

    col = pl.BlockSpec((tm, tc), lambda i, j: (i, j))
    wspec = pl.BlockSpec((d, tc), lambda i, j: (0, j))
    shape = jax.ShapeDtypeStruct((t, c), BF16)
    return pl.pallas_call(
        kern, name="conv_glu_fwd", grid=(t // tm, c // tc),
        in_specs=[pl.BlockSpec((tm, d), lambda i, j: (i, 0)), wspec, wspec], out_specs=[col, col, col],
        out_shape=[shape, shape, shape], compiler_params=_params("parallel", "parallel"))(m, wa, wb)


def _conv_tile(t):
    return min(256, t)


def _shift_copies(ext, shifted, rows):
    for s in range(8):
        shifted[s] = ext[pl.ds(s, rows), :]


def _shifted_rows(shifted, start, nrows):
    return shifted[start % 8, pl.ds(start - start % 8, nrows), :]


def _conv_fwd(glu, w_dw, g):
    t, c = glu.shape
    tm = _conv_tile(t)
    hb = tm // CONV_HALO

    def kern(cur_ref, halo_ref, w_ref, g_ref, cv_ref, s_ref, ext, shifted):
        i = pl.program_id(0)
        ext[0:CONV_HALO, :] = jnp.where(i > 0, halo_ref[...].astype(F32), 0.0)
        ext[CONV_HALO:tm + CONV_HALO, :] = cur_ref[...].astype(F32)
        ext[tm + CONV_HALO:, :] = jnp.zeros((8, c), F32)
        _shift_copies(ext, shifted, tm + CONV_HALO)
        gv = g_ref[...]
        for r0 in range(0, tm, CONV_ROWS):
            acc = jnp.zeros((CONV_ROWS, c), F32)
            for k in range(CONV_WIDTH):
                acc = acc + _shifted_rows(shifted, r0 + 2 + k, CONV_ROWS) * w_ref[k:k + 1, :]
            cv_ref[r0:r0 + CONV_ROWS, :] = acc
            rn = acc * _rstd(acc) * gv
            s_ref[r0:r0 + CONV_ROWS, :] = _bf(rn * _sigmoid(rn))

    row = pl.BlockSpec((tm, c), lambda i: (i, 0))
    return pl.pallas_call(
        kern, name="conv_fwd", grid=(t // tm,),
        in_specs=[row, pl.BlockSpec((CONV_HALO, c), lambda i: (jnp.maximum(i * hb - 1, 0), 0)),
                  pl.BlockSpec((CONV_HALO, c), lambda i: (0, 0)), pl.BlockSpec((1, c), lambda i: (0, 0))],
        out_specs=[row, row],
        out_shape=[jax.ShapeDtypeStruct((t, c), F32), jax.ShapeDtypeStruct((t, c), BF16)],
        scratch_shapes=[pltpu.VMEM((tm + CONV_HALO + 8, c), F32), pltpu.VMEM((8, tm + CONV_HALO, c), F32)],
        compiler_params=_params("parallel"))(glu, glu, w_dw, g.reshape(1, c))


def _conv_bwd_norm(dh, cv, w_pw2, g, dep):
    t, c = cv.shape
    tm = min(512, t)

    def kern(dh_ref, cv_ref, w_ref, g_ref, dep_ref, dcv_ref, dg_ref):
        @pl.when(pl.program_id(0) == 0)
        def _():
            dg_ref[...] = jnp.zeros_like(dg_ref)

        ds = lax.dot_general(_bf(dh_ref[...]), w_ref[...], (((1,), (1,)), ((), ())), preferred_element_type=F32)
        x = cv_ref[...]
        rstd = _rstd(x)
        xhat = x * rstd
        rn = xhat * g_ref[...]
        sig = _sigmoid(rn)
        drn = ds * (sig * (1.0 + rn * (1.0 - sig)))
        dg_ref[...] += jnp.broadcast_to(jnp.sum(drn * xhat, axis=0, keepdims=True), dg_ref.shape)
        dxh = drn * g_ref[...]
        dcv_ref[...] = rstd * (dxh - xhat * jnp.mean(dxh * xhat, axis=-1, keepdims=True))

    row = pl.BlockSpec((tm, c), lambda i: (i, 0))
    return pl.pallas_call(
        kern, name="conv_bwd_norm", grid=(t // tm,),
        in_specs=[pl.BlockSpec((tm, dh.shape[1]), lambda i: (i, 0)), row,
                  pl.BlockSpec(w_pw2.shape, lambda i: (0, 0)), pl.BlockSpec((1, c), lambda i: (0, 0)),
                  pl.BlockSpec((8, 128), lambda i: (0, 0))],
        out_specs=[row, pl.BlockSpec((8, c), lambda i: (0, 0))],
        out_shape=[jax.ShapeDtypeStruct((t, c), F32), jax.ShapeDtypeStruct((8, c), F32)],
        compiler_params=_params("arbitrary"))(dh, cv, w_pw2, g.reshape(1, c), dep)


def _conv_bwd_dw(dcv, glu, a, b, w_dw):
    t, c = dcv.shape
    tm = _conv_tile(t)
    hb = tm // CONV_HALO
    last = t // CONV_HALO - 1

    def kern(dcv_ref, dnext_ref, glu_ref, gprev_ref, a_ref, b_ref, w_ref, da_ref, db_ref, dw_ref,
             dext, gext, dshift, gshift):
        i = pl.program_id(0)

        @pl.when(i == 0)
        def _():
            dw_ref[...] = jnp.zeros_like(dw_ref)

        dext[0:tm, :] = dcv_ref[...]
        dext[tm:tm + CONV_HALO, :] = jnp.where(i < t // tm - 1, dnext_ref[...], 0.0)
        dext[tm + CONV_HALO:, :] = jnp.zeros((8, c), F32)
        gext[0:CONV_HALO, :] = jnp.where(i > 0, gprev_ref[...].astype(F32), 0.0)
        gext[CONV_HALO:tm + CONV_HALO, :] = glu_ref[...].astype(F32)
        gext[tm + CONV_HALO:, :] = jnp.zeros((8, c), F32)
        _shift_copies(dext, dshift, tm + CONV_HALO)
        _shift_copies(gext, gshift, tm + CONV_HALO)
        for r0 in range(0, tm, CONV_ROWS):
            acc = jnp.zeros((CONV_ROWS, c), F32)
            for k in range(CONV_WIDTH):
                acc = acc + _shifted_rows(dshift, r0 + CONV_WIDTH - 1 - k, CONV_ROWS) * w_ref[k:k + 1, :]
            av = a_ref[r0:r0 + CONV_ROWS, :].astype(F32)
            sig = _sigmoid(b_ref[r0:r0 + CONV_ROWS, :].astype(F32))
            da_ref[r0:r0 + CONV_ROWS, :] = _bf(acc * sig)
            db_ref[r0:r0 + CONV_ROWS, :] = _bf(acc * av * sig * (1.0 - sig))
        for k in range(CONV_WIDTH):
            acc = jnp.zeros((CONV_ROWS, c), F32)
            for r0 in range(0, tm, CONV_ROWS):
                acc = acc + _shifted_rows(gshift, r0 + 2 + k, CONV_ROWS) * dext[r0:r0 + CONV_ROWS, :]
            dw_ref[k:k + 1, :] += jnp.sum(acc, axis=0, keepdims=True)

    row = pl.BlockSpec((tm, c), lambda i: (i, 0))
    shape = jax.ShapeDtypeStruct((t, c), BF16)
    return pl.pallas_call(
        kern, name="conv_bwd_dw", grid=(t // tm,),
        in_specs=[row, pl.BlockSpec((CONV_HALO, c), lambda i: (jnp.minimum((i + 1) * hb, last), 0)),
                  row, pl.BlockSpec((CONV_HALO, c), lambda i: (jnp.maximum(i * hb - 1, 0), 0)),
                  row, row, pl.BlockSpec((CONV_HALO, c), lambda i: (0, 0))],
        out_specs=[row, row, pl.BlockSpec((CONV_HALO, c), lambda i: (0, 0))],
        out_shape=[shape, shape, jax.ShapeDtypeStruct((CONV_HALO, c), F32)],
        scratch_shapes=[pltpu.VMEM((tm + CONV_HALO + 8, c), F32), pltpu.VMEM((tm + CONV_HALO + 8, c), F32),
                        pltpu.VMEM((8, tm + CONV_HALO, c), F32), pltpu.VMEM((8, tm + CONV_HALO, c), F32)],
        compiler_params=_params("arbitrary"))(dcv, dcv, glu, glu, a, b, w_dw)


def _rope_tables(pos):
    t = pos.shape[0]
    tm = min(512, t)
    freq = (np.float32(ROPE_THETA) ** (np.float32(-2.0) * np.arange(ROPE // 2, dtype=np.float32)
                                       / np.float32(ROPE))).astype(np.float32)
    row = np.zeros((2, 128), np.float32)
    row[0, :ROPE] = np.concatenate([freq, freq])
    row[1, :ROPE] = 1.0

    def kern(pos_ref, f_ref, c_ref, s_ref):
        ang = pos_ref[...].astype(F32) * f_ref[0:1, :]
        mask = f_ref[1:2, :]
        c_ref[...] = jnp.cos(ang) * mask
        s_ref[...] = jnp.sin(ang) * mask

    out = pl.BlockSpec((tm, 128), lambda i: (i, 0))
    shape = jax.ShapeDtypeStruct((t, 128), F32)
    return pl.pallas_call(
        kern, name="rope_tables", grid=(t // tm,),
        in_specs=[pl.BlockSpec((tm, 1), lambda i: (i, 0)), pl.BlockSpec((2, 128), lambda i: (0, 0))],
        out_specs=[out, out], out_shape=[shape, shape], compiler_params=_params("parallel"))(pos, jnp.asarray(row))


def _mla_prep(a, gq, gkv, cs_c, cs_s):
    t = a.shape[0]
    tm = min(512, t)
    kv0, r0 = Q_LORA, Q_LORA + KV_LORA

    def kern(a_ref, gq_ref, gkv_ref, c_ref, s_ref, cq_ref, ckv_ref, kr_ref):
        aq = a_ref[:, 0:kv0]
        akv = a_ref[:, kv0:r0]
        ar = a_ref[:, r0:A_PAD]
        cq_ref[...] = _bf(aq * _rstd(aq) * gq_ref[...])
        ckv_ref[...] = _bf(akv * _rstd(akv) * gkv_ref[...])
        kr_ref[...] = _bf(ar * c_ref[...] + _rot(ar) * s_ref[...])

    def row(w):
        return pl.BlockSpec((tm, w), lambda i: (i, 0))

    def vec(w):
        return pl.BlockSpec((1, w), lambda i: (0, 0))

    return pl.pallas_call(
        kern, name="mla_prep", grid=(t // tm,),
        in_specs=[row(A_PAD), vec(Q_LORA), vec(KV_LORA), row(128), row(128)],
        out_specs=[row(Q_LORA), row(KV_LORA), row(128)],
        out_shape=[jax.ShapeDtypeStruct((t, Q_LORA), BF16), jax.ShapeDtypeStruct((t, KV_LORA), BF16),
                   jax.ShapeDtypeStruct((t, 128), BF16)],
        compiler_params=_params("parallel"))(a, gq.reshape(1, -1), gkv.reshape(1, -1), cs_c, cs_s)


def _mla_prep_bwd(a, dcq, dckv, dar, gq, gkv):
    t = a.shape[0]
    tm = min(512, t)
    kv0, r0 = Q_LORA, Q_LORA + KV_LORA

    def kern(a_ref, dcq_ref, dckv_ref, dar_ref, gq_ref, gkv_ref, da_ref, dgq_ref, dgkv_ref):
        @pl.when(pl.program_id(0) == 0)
        def _():
            dgq_ref[...] = jnp.zeros_like(dgq_ref)
            dgkv_ref[...] = jnp.zeros_like(dgkv_ref)

        def back(x, dy, g_ref, dg_ref):
            rstd = _rstd(x)
            xhat = x * rstd
            dg_ref[...] += jnp.broadcast_to(jnp.sum(dy * xhat, axis=0, keepdims=True), dg_ref.shape)
            dxh = dy * g_ref[...]
            return rstd * (dxh - xhat * jnp.mean(dxh * xhat, axis=-1, keepdims=True))

        da_ref[:, 0:kv0] = _bf(back(a_ref[:, 0:kv0], dcq_ref[...], gq_ref, dgq_ref))
        da_ref[:, kv0:r0] = _bf(back(a_ref[:, kv0:r0], dckv_ref[...], gkv_ref, dgkv_ref))
        da_ref[:, r0:A_PAD] = _bf(dar_ref[...])

    def row(w):
        return pl.BlockSpec((tm, w), lambda i: (i, 0))

    def vec(r, w):
        return pl.BlockSpec((r, w), lambda i: (0, 0))

    return pl.pallas_call(
        kern, name="mla_prep_bwd", grid=(t // tm,),
        in_specs=[row(A_PAD), row(Q_LORA), row(KV_LORA), row(128), vec(1, Q_LORA), vec(1, KV_LORA)],
        out_specs=[row(A_PAD), vec(8, Q_LORA), vec(8, KV_LORA)],
        out_shape=[jax.ShapeDtypeStruct((t, A_PAD), BF16), jax.ShapeDtypeStruct((8, Q_LORA), F32),
                   jax.ShapeDtypeStruct((8, KV_LORA), F32)],
        compiler_params=_params("arbitrary"))(a, dcq, dckv, dar, gq.reshape(1, -1), gkv.reshape(1, -1))


def _mla_qkv(cq, ckv, kr, cs_c, cs_s, wuq, wukv):
    t = cq.shape[0]
    tm = min(512, t)
    kvw = NOPE + V_HEAD

    def kern(cq_ref, ckv_ref, kr_ref, c_ref, s_ref, wq_ref, wkv_ref, q_ref, k_ref, v_ref):
        r = jnp.dot(cq_ref[...], wq_ref[...], preferred_element_type=F32)
        kv = jnp.dot(ckv_ref[...], wkv_ref[...], preferred_element_type=F32)
        cv, sv, krv = c_ref[...], s_ref[...], kr_ref[...]
        for h in range(HEADS):
            xr = r[:, h * HEAD_PAD + NOPE:(h + 1) * HEAD_PAD]
            q_ref[h, :, 0:NOPE] = _bf(r[:, h * HEAD_PAD:h * HEAD_PAD + NOPE] * ATTN_SCALE)
            q_ref[h, :, NOPE:] = _bf((xr * cv + _rot(xr) * sv) * ATTN_SCALE)
            k_ref[h, :, 0:NOPE] = _bf(kv[:, h * kvw:h * kvw + NOPE])
            k_ref[h, :, NOPE:] = krv
            v_ref[h] = _bf(kv[:, h * kvw + NOPE:(h + 1) * kvw])

    def row(w):
        return pl.BlockSpec((tm, w), lambda i: (i, 0))

    def heads(w):
        return pl.BlockSpec((HEADS, tm, w), lambda i: (0, i, 0))

    return pl.pallas_call(
        kern, name="mla_qkv", grid=(t // tm,),
        in_specs=[row(Q_LORA), row(KV_LORA), row(128), row(128), row(128),
                  pl.BlockSpec(wuq.shape, lambda i: (0, 0)), pl.BlockSpec(wukv.shape, lambda i: (0, 0))],
        out_specs=[heads(HEAD_PAD), heads(HEAD_PAD), heads(V_HEAD)],
        out_shape=[jax.ShapeDtypeStruct((HEADS, t, HEAD_PAD), BF16), jax.ShapeDtypeStruct((HEADS, t, HEAD_PAD), BF16),
                   jax.ShapeDtypeStruct((HEADS, t, V_HEAD), BF16)],
        compiler_params=_params("parallel"))(cq, ckv, kr, cs_c, cs_s, wuq, wukv)


def _mla_qkv_bwd(dq, dk, dv, cs_c, cs_s, wuq, wukv):
    t = dq.shape[1]
    tm = min(256, t)
    kvw = NOPE + V_HEAD

    def kern(dq_ref, dk_ref, dv_ref, c_ref, s_ref, wq_ref, wkv_ref, dr_ref, dkv_ref, dcq_ref, dckv_ref, dar_ref):
        cv, sv = c_ref[...], s_ref[...]
        dar = jnp.zeros_like(cv)
        for h in range(HEADS):
            dqx = dq_ref[h, :, NOPE:].astype(F32)
            dr_ref[:, h * HEAD_PAD:h * HEAD_PAD + NOPE] = _bf(dq_ref[h, :, 0:NOPE].astype(F32) * ATTN_SCALE)
            dr_ref[:, h * HEAD_PAD + NOPE:(h + 1) * HEAD_PAD] = _bf((dqx * cv + _rot_t(dqx * sv)) * ATTN_SCALE)
            dkx = dk_ref[h, :, NOPE:].astype(F32)
            dar = dar + (dkx * cv + _rot_t(dkx * sv))
            dkv_ref[:, h * kvw:h * kvw + NOPE] = dk_ref[h, :, 0:NOPE]
            dkv_ref[:, h * kvw + NOPE:(h + 1) * kvw] = dv_ref[h]
        dar_ref[...] = dar
        dcq_ref[...] = lax.dot_general(dr_ref[...], wq_ref[...], (((1,), (1,)), ((), ())),
                                       preferred_element_type=F32)
        dckv_ref[...] = lax.dot_general(dkv_ref[...], wkv_ref[...], (((1,), (1,)), ((), ())),
                                        preferred_element_type=F32)

    def row(w):
        return pl.BlockSpec((tm, w), lambda i: (i, 0))

    def heads(w):
        return pl.BlockSpec((HEADS, tm, w), lambda i: (0, i, 0))

    return pl.pallas_call(
        kern, name="mla_qkv_bwd", grid=(t // tm,),
        in_specs=[heads(HEAD_PAD), heads(HEAD_PAD), heads(V_HEAD), row(128), row(128),
                  pl.BlockSpec(wuq.shape, lambda i: (0, 0)), pl.BlockSpec(wukv.shape, lambda i: (0, 0))],
        out_specs=[row(HEADS * HEAD_PAD), row(HEADS * kvw), row(Q_LORA), row(KV_LORA), row(128)],
        out_shape=[jax.ShapeDtypeStruct((t, HEADS * HEAD_PAD), BF16), jax.ShapeDtypeStruct((t, HEADS * kvw), BF16),
                   jax.ShapeDtypeStruct((t, Q_LORA), F32), jax.ShapeDtypeStruct((t, KV_LORA), F32),
                   jax.ShapeDtypeStruct((t, 128), F32)],
        compiler_params=_params("parallel"))(dq, dk, dv, cs_c, cs_s, wuq, wukv)


def _attn_block(t):
    return 512 if t >= 4096 else 128


def _chunk_mask(bk, bq):
    kc = lax.broadcasted_iota(jnp.int32, (bk, bq), 0) // CHUNK
    qc = lax.broadcasted_iota(jnp.int32, (bk, bq), 1) // CHUNK
    return qc >= kc


def _flash_fwd(q, k, v):
    t = q.shape[1]
    bq = _attn_block(t)
    nq = t // bq
    nch = 2

    def kern(q_ref, k_ref, v_ref, o_ref, lse_ref, s_buf, p_buf, m_ref, l_ref, acc_ref):
        i = pl.program_id(1)
        queries = [q_ref[c * bq:(c + 1) * bq, :] for c in range(nch)]

        def block(j):
            rows = pl.ds(pl.multiple_of(j * bq, bq), bq)
            return k_ref[rows, :], v_ref[rows, :]

        def scores(kj, chain):
            return lax.dot_general(kj, queries[chain], (((1,), (1,)), ((), ())), preferred_element_type=F32)

        def softmax_block(chain, slot, vj):
            for c0 in range(0, bq, 128):
                cols = slice(c0, c0 + 128)
                s = s_buf[slot, chain, :, cols]
                m_old = m_ref[chain, 0:1, cols]
                m_new = jnp.maximum(m_old, jnp.max(s, axis=0, keepdims=True))
                alpha = jnp.exp(m_old - m_new)
                p = jnp.exp(s - m_new)
                l_ref[chain, 0:1, cols] = alpha * l_ref[chain, 0:1, cols] + jnp.sum(p, axis=0, keepdims=True)
                m_ref[chain, 0:1, cols] = m_new
                p_buf[chain, :, cols] = _bf(p)
                acc_ref[chain, :, cols] = acc_ref[chain, :, cols] * alpha
            acc_ref[chain] += lax.dot_general(vj, p_buf[chain], (((0,), (0,)), ((), ())),
                                              preferred_element_type=F32)

        m_ref[...] = jnp.full(m_ref.shape, -1e30, F32)
        l_ref[...] = jnp.zeros_like(l_ref)
        acc_ref[...] = jnp.zeros_like(acc_ref)
        mask = _chunk_mask(bq, bq)
        k0, v0 = block(nch * i)
        k1, v1 = block(nch * i + 1)
        s_buf[0, 0] = jnp.where(mask, scores(k0, 0), -1e30)
        s_buf[0, 1] = scores(k0, 1)
        s_buf[1, 1] = jnp.where(mask, scores(k1, 1), -1e30)
        softmax_block(0, 0, v0)
        softmax_block(1, 0, v0)
        softmax_block(1, 1, v1)
        kf = block(0)[0]
        for c in range(nch):
            s_buf[0, c] = scores(kf, c)

        def body(pair, carry):
            for cur in range(2):
                j = 2 * pair + cur
                kn = block(jnp.minimum(j + 1, jnp.maximum(nch * i - 1, 0)))[0]
                for c in range(nch):
                    s_buf[1 - cur, c] = scores(kn, c)
                vj = block(j)[1]
                for c in range(nch):
                    softmax_block(c, cur, vj)
            return carry

        lax.fori_loop(0, (nch // 2) * i, body, 0)
        for chain in range(nch):
            l = l_ref[chain, 0:1, :]
            o_ref[chain * bq:(chain + 1) * bq, :] = _bf((acc_ref[chain] / l).T)
            lse_ref[chain] = jnp.broadcast_to(m_ref[chain, 0:1, :] + jnp.log(l), (8, bq))

    return pl.pallas_call(
        kern, name="flash_fwd", grid=(HEADS, nq // nch),
        in_specs=[pl.BlockSpec((None, nch * bq, HEAD_PAD), lambda h, i: (h, i, 0)),
                  pl.BlockSpec((None, t, HEAD_PAD), lambda h, i: (h, 0, 0)),
                  pl.BlockSpec((None, t, V_HEAD), lambda h, i: (h, 0, 0))],
        out_specs=[pl.BlockSpec((nch * bq, V_HEAD), lambda h, i: (i, h)),
                   pl.BlockSpec((None, nch, 8, bq), lambda h, i: (h, i, 0, 0))],
        out_shape=[jax.ShapeDtypeStruct((t, HEADS * V_HEAD), BF16), jax.ShapeDtypeStruct((HEADS, nq, 8, bq), F32)],
        scratch_shapes=[pltpu.VMEM((2, nch, bq, bq), F32), pltpu.VMEM((nch, bq, bq), BF16),
                        pltpu.VMEM((nch, 8, bq), F32), pltpu.VMEM((nch, 8, bq), F32),
                        pltpu.VMEM((nch, V_HEAD, bq), F32)],
        compiler_params=_params("parallel", "arbitrary"))(q, k, v)


def _attn_delta(do, o):
    t = do.shape[0]
    bq = _attn_block(t)

    def kern(do_ref, o_ref, d_ref):
        for h in range(HEADS):
            cols = slice(h * V_HEAD, (h + 1) * V_HEAD)
            prod = do_ref[:, cols].astype(F32) * o_ref[:, cols].astype(F32)
            d_ref[h] = jnp.broadcast_to(jnp.sum(prod.T, axis=0, keepdims=True), (8, bq))

    blk = pl.BlockSpec((bq, HEADS * V_HEAD), lambda i: (i, 0))
    return pl.pallas_call(
        kern, name="attn_delta", grid=(t // bq,), in_specs=[blk, blk],
        out_specs=pl.BlockSpec((HEADS, None, 8, bq), lambda i: (0, i, 0, 0)),
        out_shape=jax.ShapeDtypeStruct((HEADS, t // bq, 8, bq), F32),
        compiler_params=_params("parallel"))(do, o)


def _flash_bwd(q, k, v, do, lse, delta):
    t = q.shape[1]
    bq = _attn_block(t)
    nq = t // bq

    def kern(q_ref, k_ref, v_ref, do_ref, lse_ref, del_ref, dq_out, dk_out, dv_out, dq_ref, dk_ref, dvt_ref):
        j = pl.program_id(1)

        @pl.when(j == 0)
        def _():
            dq_ref[...] = jnp.zeros_like(dq_ref)

        dk_ref[...] = jnp.zeros_like(dk_ref)
        dvt_ref[...] = jnp.zeros_like(dvt_ref)
        kj, vj = k_ref[...], v_ref[...]

        def step(i, masked):
            rows = pl.ds(pl.multiple_of(i * bq, bq), bq)
            qi, doi = q_ref[rows, :], do_ref[rows, :]
            st = lax.dot_general(kj, qi, (((1,), (1,)), ((), ())), preferred_element_type=F32)
            pt = jnp.exp(st - lse_ref[i][0:1, :])
            if masked:
                pt = jnp.where(_chunk_mask(bq, bq), pt, 0.0)
            dpt = lax.dot_general(vj, doi, (((1,), (1,)), ((), ())), preferred_element_type=F32)
            dst = _bf(pt * (dpt - del_ref[i][0:1, :]))
            dvt_ref[...] += lax.dot_general(doi, _bf(pt), (((0,), (1,)), ((), ())), preferred_element_type=F32)
            dk_ref[...] += jnp.dot(dst, qi, preferred_element_type=F32)
            dq_ref[rows, :] += lax.dot_general(dst, kj, (((0,), (0,)), ((), ())), preferred_element_type=F32)

        step(j, True)

        def body(pair, carry):
            step(j + 1 + 2 * pair, False)
            step(j + 2 + 2 * pair, False)
            return carry

        rest = nq - 1 - j
        lax.fori_loop(0, rest // 2, body, 0)

        @pl.when(rest % 2 == 1)
        def _():
            step(nq - 1, False)

        dk_out[...] = _bf(dk_ref[...])
        dv_out[...] = _bf(dvt_ref[...].T)

        @pl.when(j == nq - 1)
        def _():
            dq_out[...] = _bf(dq_ref[...])

    stat = pl.BlockSpec((None, nq, 8, bq), lambda h, j: (h, 0, 0, 0))
    return pl.pallas_call(
        kern, name="flash_bwd", grid=(HEADS, nq),
        in_specs=[pl.BlockSpec((None, t, HEAD_PAD), lambda h, j: (h, 0, 0)),
                  pl.BlockSpec((None, bq, HEAD_PAD), lambda h, j: (h, j, 0)),
                  pl.BlockSpec((None, bq, V_HEAD), lambda h, j: (h, j, 0)),
                  pl.BlockSpec((t, V_HEAD), lambda h, j: (0, h)), stat, stat],
        out_specs=[pl.BlockSpec((None, t, HEAD_PAD), lambda h, j: (h, 0, 0)),
                   pl.BlockSpec((None, bq, HEAD_PAD), lambda h, j: (h, j, 0)),
                   pl.BlockSpec((None, bq, V_HEAD), lambda h, j: (h, j, 0))],
        out_shape=[jax.ShapeDtypeStruct((HEADS, t, HEAD_PAD), BF16), jax.ShapeDtypeStruct((HEADS, t, HEAD_PAD), BF16),
                   jax.ShapeDtypeStruct((HEADS, t, V_HEAD), BF16)],
        scratch_shapes=[pltpu.VMEM((t, HEAD_PAD), F32), pltpu.VMEM((bq, HEAD_PAD), F32), pltpu.VMEM((V_HEAD, bq), F32)],
        compiler_params=_params("parallel", "arbitrary"))(q, k, v, do, lse, delta)


def _place():
    x, y, c = lax.axis_index("x"), lax.axis_index("y"), lax.axis_index("c")
    return x, y, c, [(1 - x, y), (x, 1 - y), (1 - x, 1 - y)]


def _all_gather_rows(block, name):
    m_per, n = block.shape

    def body(x_ref, out_ref, send_sems, recv_sems, local_sem):
        x, y, c, chips = _place()
        me, sibling = (x, y, c), (x, y, 1 - c)

        def rows(px, py, pc):
            return out_ref.at[pl.ds((4 * px + 2 * py + pc) * m_per, m_per), :]

        def copy(k, blk, to, src=None):
            return pltpu.make_async_remote_copy(
                src_ref=rows(*blk) if src is None else src, dst_ref=rows(*blk), send_sem=send_sems.at[k],
                recv_sem=recv_sems.at[k], device_id=to, device_id_type=MESH)

        mine = pltpu.make_async_copy(x_ref, rows(*me), local_sem)
        mine.start()
        first = [copy(0, me, sibling, src=x_ref)]
        first += [copy(1 + j, me, (*chip, c), src=x_ref) for j, chip in enumerate(chips)]
        for cp in first:
            cp.start()
        passed = [copy(4 + j, (*chip, c), sibling) for j, chip in enumerate(chips)]
        for j, chip in enumerate(chips):
            copy(1 + j, (*chip, c), me).wait_recv()
            passed[j].start()
        copy(0, sibling, me).wait_recv()
        for j, chip in enumerate(chips):
            copy(4 + j, (*chip, 1 - c), me).wait_recv()
        for cp in first + passed:
            cp.wait_send()
        mine.wait()

    return pl.pallas_call(
        body, name=name, out_shape=jax.ShapeDtypeStruct((8 * m_per, n), block.dtype),
        in_specs=[pl.BlockSpec(memory_space=pltpu.VMEM)], out_specs=pl.BlockSpec(memory_space=pltpu.VMEM),
        scratch_shapes=[pltpu.SemaphoreType.DMA((7,)), pltpu.SemaphoreType.DMA((7,)), pltpu.SemaphoreType.DMA],
        compiler_params=pltpu.CompilerParams(vmem_limit_bytes=VMEM_LIMIT_BYTES))(block)


HBM_SPEC = pl.BlockSpec(memory_space=pltpu.HBM)
SEM_SPEC = pl.BlockSpec(memory_space=pltpu.SEMAPHORE)
DATAFLOW = pltpu.SideEffectType.DATAFLOW_SIDE_EFFECTING


def _in_hbm(a):
    return pltpu.with_memory_space_constraint(a, pltpu.HBM)


def _chip_copies(ins, lands, send_sems, recv_sems, src_slot, half=False):
    n = len(ins)
    x, y, c, chips = _place()
    me = 2 * x + y

    def ends(w, chip):
        src = ins[w].at[2 * chip[0] + chip[1]] if src_slot else ins[w]
        if not half:
            return src, lands[w].at[me]
        rows = pl.ds(pl.multiple_of(c * (src.shape[0] // 2), 16), src.shape[0] // 2)
        return src.at[rows], lands[w].at[me, rows]

    copies = []
    for w in range(n):
        for p, chip in enumerate(chips):
            src, dst = ends(w, chip)
            copies.append(pltpu.make_async_remote_copy(
                src_ref=src, dst_ref=dst, send_sem=send_sems.at[p * n + w], recv_sem=recv_sems.at[p * n + w],
                device_id=(*chip, c), device_id_type=MESH))
    return copies


def _fill_halves(lands, name):
    n = len(lands)

    def body(*refs):
        bufs = refs[n:2 * n]
        send_sems, recv_sems = refs[2 * n:]
        x, y, c, chips = _place()
        copies = []
        for w in range(n):
            hr = bufs[w].shape[1] // 2
            for p, chip in enumerate(chips):
                part = bufs[w].at[2 * chip[0] + chip[1], pl.ds(pl.multiple_of(c * hr, 16), hr)]
                copies.append(pltpu.make_async_remote_copy(
                    src_ref=part, dst_ref=part, send_sem=send_sems.at[p * n + w], recv_sem=recv_sems.at[p * n + w],
                    device_id=(x, y, 1 - c), device_id_type=MESH))
        for cp in copies:
            cp.start()
        for cp in copies:
            cp.wait_send()
        for w in range(n):
            hr = bufs[w].shape[1] // 2
            for p, chip in enumerate(chips):
                part = bufs[w].at[2 * chip[0] + chip[1], pl.ds(pl.multiple_of((1 - c) * hr, 16), hr)]
                pltpu.make_async_remote_copy(
                    src_ref=part, dst_ref=part, send_sem=send_sems.at[p * n + w], recv_sem=recv_sems.at[p * n + w],
                    device_id=(x, y, 1 - c), device_id_type=MESH).wait_recv()

    any_spec = pl.BlockSpec(memory_space=pl.ANY)
    return list(pl.pallas_call(
        body, name=name, out_shape=[jax.ShapeDtypeStruct(a.shape, a.dtype) for a in lands],
        in_specs=[any_spec] * n, out_specs=[any_spec] * n, input_output_aliases={i: i for i in range(n)},
        scratch_shapes=[pltpu.SemaphoreType.DMA((3 * n,)), pltpu.SemaphoreType.DMA((3 * n,))])(*lands))


def _exchange_start(srcs, lands, src_slot, name, dep=None, half=False):
    n = len(srcs)
    first_out = 2 * n + (dep is not None)

    def body(*refs):
        for cp in _chip_copies(refs[:n], refs[n:2 * n], refs[first_out], refs[first_out + 1], src_slot, half):
            cp.start()
        token = refs[-1]
        token[...] = jnp.zeros_like(token)

    thru = [pltpu.HBM(a.shape, a.dtype) for a in list(srcs) + list(lands)]
    res = pl.pallas_call(
        body, name=name,
        out_shape=(pltpu.SemaphoreType.DMA((3 * n,)), pltpu.SemaphoreType.DMA((3 * n,)), *thru,
                   jax.ShapeDtypeStruct((8, 128), F32)),
        in_specs=[HBM_SPEC] * (2 * n) + ([pl.BlockSpec(memory_space=pl.ANY)] if dep is not None else []),
        out_specs=(SEM_SPEC, SEM_SPEC, *[HBM_SPEC] * (2 * n), pl.BlockSpec(memory_space=pltpu.VMEM)),
        input_output_aliases={i: 2 + i for i in range(2 * n)},
        compiler_params=pltpu.CompilerParams(has_side_effects=DATAFLOW))(
            *[_in_hbm(a) for a in srcs], *[_in_hbm(a) for a in lands], *([dep] if dep is not None else []))
    return (res[0], res[1], list(res[2:2 + n]), list(res[2 + n:2 + 2 * n])), res[-1]


def _exchange_wait(flight, after, src_slot, name, half=False):
    send_sems, recv_sems, srcs, lands = flight
    n = len(srcs)

    def body(*refs):
        for cp in _chip_copies(refs[:n], refs[n:2 * n], refs[2 * n], refs[2 * n + 1], src_slot, half):
            cp.wait_send()
            cp.wait_recv()

    thru = [pltpu.HBM(a.shape, a.dtype) for a in list(srcs) + list(lands)]
    res = pl.pallas_call(
        body, name=name, out_shape=thru,
        in_specs=[HBM_SPEC] * (2 * n) + [SEM_SPEC, SEM_SPEC, pl.BlockSpec(memory_space=pl.ANY)],
        out_specs=[HBM_SPEC] * (2 * n), input_output_aliases={i: i for i in range(2 * n)},
        compiler_params=pltpu.CompilerParams(has_side_effects=DATAFLOW))(*srcs, *lands, send_sems, recv_sems, after)
    return list(res[n:])


def _landing(own, me):
    return lax.dynamic_update_index_in_dim(lax.empty((4, *own.shape), own.dtype), own, me, 0)


def _swap_with_sibling(arrays, name):
    n = len(arrays)

    def body(*refs):
        ins, outs = refs[:n], refs[n:2 * n]
        send_sems, recv_sems = refs[2 * n:]
        x, y, c, _ = _place()
        copies = [pltpu.make_async_remote_copy(src_ref=ins[w], dst_ref=outs[w], send_sem=send_sems.at[w],
                                               recv_sem=recv_sems.at[w], device_id=(x, y, 1 - c), device_id_type=MESH)
                  for w in range(n)]
        for cp in copies:
            cp.start()
        for cp in copies:
            cp.wait()

    any_spec = pl.BlockSpec(memory_space=pl.ANY)
    return pl.pallas_call(
        body, name=name, out_shape=[jax.ShapeDtypeStruct(a.shape, a.dtype) for a in arrays],
        in_specs=[any_spec] * n, out_specs=[any_spec] * n,
        scratch_shapes=[pltpu.SemaphoreType.DMA((n,)), pltpu.SemaphoreType.DMA((n,))])(*arrays)


def _as_rows(a):
    return a.reshape(-1, a.shape[-1])


def _row_tile(r, c, budget_bytes=1 << 20):
    tr = r
    while tr % 16 == 0 and tr * c * 4 > budget_bytes:
        tr //= 2
    return tr


def _sum_slots(layers, nlayer, name, into=None):
    _, r, c = layers[0][1].shape
    tr = _row_tile(r, c)
    nt = r // tr
    acc = into
    for l, r4 in layers:
        def kern(r_ref, *rest):
            o_ref = rest[-1]
            o_ref[...] = _bf(((r_ref[0].astype(F32) + r_ref[1].astype(F32)) + r_ref[2].astype(F32))
                             + r_ref[3].astype(F32))

        out_spec = pl.BlockSpec((tr, c), lambda i, l=l: (l * nt + i, 0))
        first = acc is None
        acc = pl.pallas_call(
            kern, name=f"{name}_l{l}", grid=(nt,),
            in_specs=[pl.BlockSpec((4, tr, c), lambda i: (0, i, 0))]
            + ([] if first else [pl.BlockSpec(memory_space=pl.ANY)]),
            out_specs=out_spec, out_shape=jax.ShapeDtypeStruct((nlayer * r, c), BF16),
            input_output_aliases={} if first else {1: 0},
            compiler_params=_params("parallel"))(*([r4] if first else [r4, acc]))
    return acc


def _adamw_group(ws, ms, vs, partss, name, tile_bytes):
    r, c = ws[0].shape
    tr = _row_tile(r, c, tile_bytes)
    nw, npart = len(ws), len(partss[0])
    nin = 3 + npart
    c1 = 1.0 - ADAM_B1 ** ADAM_STEP
    c2 = 1.0 - ADAM_B2 ** ADAM_STEP

    def kern(*refs):
        for k in range(nw):
            w_ref, m_ref, v_ref = refs[k * nin:k * nin + 3]
            p_refs = refs[k * nin + 3:(k + 1) * nin]
            g_ref, d_ref, mo_ref, vo_ref = refs[nw * nin + 4 * k:nw * nin + 4 * k + 4]
            g = p_refs[0][...].astype(F32)
            for p in p_refs[1:]:
                g = g + p[...].astype(F32)
            mn = ADAM_B1 * m_ref[...] + (1.0 - ADAM_B1) * g
            vn = ADAM_B2 * v_ref[...] + (1.0 - ADAM_B2) * (g * g)
            g_ref[...] = g
            mo_ref[...] = mn
            vo_ref[...] = vn
            d_ref[...] = -ADAM_LR * ((mn / c1) / (jnp.sqrt(vn / c2) + ADAM_EPS) + ADAM_WD * w_ref[...])

    blk = pl.BlockSpec((tr, c), lambda i: (i, 0))
    shape = jax.ShapeDtypeStruct((r, c), F32)
    ins = [a for k in range(nw) for a in (ws[k], ms[k], vs[k], *partss[k])]
    res = pl.pallas_call(
        kern, name=name, grid=(r // tr,), in_specs=[blk] * (nw * nin), out_specs=[blk] * (4 * nw),
        out_shape=[shape] * (4 * nw), compiler_params=_params("parallel"))(*ins)
    return [res[4 * k:4 * k + 4] for k in range(nw)]


def _adamw(w, m, v, parts, name):
    return _adamw_group([w], [m], [v], [parts], name, 3 << 19)[0]


def _sum_devices(g8, name):
    _, r, c = g8.shape

    def kern(g_ref, o_ref):
        tot = g_ref[0]
        for dev in range(1, 8):
            tot = tot + g_ref[dev]
        o_ref[...] = tot

    return pl.pallas_call(
        kern, name=name, grid=(1,), in_specs=[pl.BlockSpec((8, r, c), lambda i: (0, 0, 0))],
        out_specs=pl.BlockSpec((r, c), lambda i: (0, 0)), out_shape=jax.ShapeDtypeStruct((r, c), F32),
        compiler_params=_params("arbitrary"))(g8)


def _pad_lanes(a, width):
    return jnp.pad(a, [(0, 0)] * (a.ndim - 1) + [(0, width - a.shape[-1])])


def kernel(x, positions, ffn_norm1, ffn1_w1, ffn1_w3, ffn1_w2, mix_norm, ffn_norm2, ffn2_w1, ffn2_w3, ffn2_w2, conv_w_pw1, conv_w_dw, conv_norm, conv_w_pw2, mla_w_a, mla_q_norm, mla_kv_norm, mla_w_uq, mla_w_ukv, mla_w_o, final_norm, loss_target, m_ffn_norm1, m_ffn1_w1, m_ffn1_w3, m_ffn1_w2, m_mix_norm, m_ffn_norm2, m_ffn2_w1, m_ffn2_w3, m_ffn2_w2, m_conv_w_pw1, m_conv_w_dw, m_conv_norm, m_conv_w_pw2, m_mla_w_a, m_mla_q_norm, m_mla_kv_norm, m_mla_w_uq, m_mla_w_ukv, m_mla_w_o, m_final_norm, v_ffn_norm1, v_ffn1_w1, v_ffn1_w3, v_ffn1_w2, v_mix_norm, v_ffn_norm2, v_ffn2_w1, v_ffn2_w3, v_ffn2_w2, v_conv_w_pw1, v_conv_w_dw, v_conv_norm, v_conv_w_pw2, v_mla_w_a, v_mla_q_norm, v_mla_kv_norm, v_mla_w_uq, v_mla_w_ukv, v_mla_w_o, v_final_norm):
    given = locals()
    return _step({nm: given[nm] for nm in INPUTS})


def _step(A):
    x = A['x'][0]
    target = A['loss_target'][0]
    t, d = x.shape
    pos = A['positions'].reshape(t, 1)
    me = 2 * lax.axis_index("x") + lax.axis_index("y")

    flipped = {f'ffn{k}_{w}' for k in (1, 2) for w in ('w1', 'w3')}
    P = {}
    for nm in BIG:
        for key in (nm, 'm_' + nm, 'v_' + nm):
            P[key] = jnp.swapaxes(A[key], 1, 2) if nm in flipped else A[key]

    def unflip(nm, a):
        return jnp.swapaxes(a, 1, 2) if nm in flipped else a

    ffn = [f'ffn{k}_{w}' for k in (1, 2) for w in ('w1', 'w3', 'w2')]
    gather_groups = [[(nm, 0) for nm in ffn[:3]],
                     [('conv_w_pw1', 0), ('conv_w_pw2', 0)] + [(nm, 0) for nm in ffn[3:]],
                     [(nm, 1) for nm in ffn[:3]] + [('mla_w_a', 0), ('mla_w_uq', 0), ('mla_w_ukv', 0), ('mla_w_o', 0)],
                     [(nm, 1) for nm in ffn[3:]]]
    halved = (0, 1)
    gather_flights = {}
    big = {}

    def gather_start(gi, dep):
        shards = [_bf(P[nm][l]) for nm, l in gather_groups[gi]]
        gather_flights[gi], token = _exchange_start(shards, [_landing(s, me) for s in shards], False,
                                                    f"gather_start_{gi}", dep, half=gi in halved)
        return token

    def gather_wait(gi, after):
        landed = _exchange_wait(gather_flights[gi], after, False, f"gather_wait_{gi}", half=gi in halved)
        if gi in halved:
            landed = _fill_halves(landed, f"gather_fill_{gi}")
        big.update(zip(gather_groups[gi], landed))
        return landed[0]

    dw_shard = A['conv_w_dw'][0]
    cw = dw_shard.shape[1]
    small = jnp.concatenate([
        jnp.pad(dw_shard, ((0, CONV_HALO - CONV_WIDTH), (0, 0))),
        jnp.pad(_pad_lanes(A['mla_q_norm'], cw), ((0, 7), (0, 0))),
        jnp.pad(_pad_lanes(A['mla_kv_norm'], cw), ((0, 7), (0, 0)))], axis=0)
    small = _all_gather_rows(small, "gather_small_weights").reshape(4, 2, 48, cw)[:, 0]
    w_dw = jnp.concatenate([small[j, :CONV_HALO] for j in range(4)], axis=1)
    gq = jnp.concatenate([small[j, CONV_HALO, :Q_LORA // 4] for j in range(4)])
    gkv = jnp.concatenate([small[j, CONV_HALO + 8, :KV_LORA // 4] for j in range(4)])

    def rows(nm, layer):
        g = big[nm, layer]
        return g.reshape(-1, g.shape[-1])

    ffn_w = {}

    def ffn_weights(k, l):
        ffn_w[k, l] = (rows(f'ffn{k}_w1', l), rows(f'ffn{k}_w3', l), rows(f'ffn{k}_w2', l))
        return ffn_w[k, l]

    token = gather_start(0, small)
    cs_c, cs_s = _rope_tables(pos)
    h0 = x
    token = gather_start(1, gather_wait(0, token))
    h1, n01, z01a, z01b = _ffn_fwd(h0, A['ffn_norm1'][0], *ffn_weights(1, 0), token, "ffn1_l0_fwd")
    token = gather_start(3, gather_start(2, gather_wait(1, h1)))
    pw1 = big['conv_w_pw1', 0]
    pw1_a = jnp.concatenate([pw1[0], pw1[1]], axis=1)
    pw1_b = jnp.concatenate([pw1[2], pw1[3]], axis=1)
    pw2 = rows('conv_w_pw2', 0)
    m0 = _norm_fwd(h1, A['mix_norm'][0], token, "mix_norm_l0")
    ca, cb, glu = _glu_fwd(m0, pw1_a, pw1_b)
    cv, cs = _conv_fwd(glu, w_dw, A['conv_norm'][0])
    h2 = _mm([(cs, pw2)], F32, "conv_pw2_fwd", res=h1)
    h3, n02, z02a, z02b = _ffn_fwd(h2, A['ffn_norm2'][0], *ffn_weights(2, 0), token, "ffn2_l0_fwd")
    gather_wait(2, h3)
    w_a = _pad_lanes(rows('mla_w_a', 0), A_PAD)
    wuq = _pad_lanes(big['mla_w_uq', 0].reshape(Q_LORA, HEADS, NOPE + ROPE), HEAD_PAD).reshape(Q_LORA, -1)
    wukv = big['mla_w_ukv', 0].reshape(KV_LORA, HEADS * (NOPE + V_HEAD))
    w_o = rows('mla_w_o', 0)
    h4, n11, z11a, z11b = _ffn_fwd(h3, A['ffn_norm1'][1], *ffn_weights(1, 1), token, "ffn1_l1_fwd")
    m1 = _norm_fwd(h4, A['mix_norm'][1], token, "mix_norm_l1")
    a_lat = _mm([(m1, w_a)], F32, "mla_down_fwd")
    cq, ckv, kr = _mla_prep(a_lat, gq, gkv, cs_c, cs_s)
    q, k, v = _mla_qkv(cq, ckv, kr, cs_c, cs_s, wuq, wukv)
    o, lse = _flash_fwd(q, k, v)
    h5 = _mm([(o, w_o)], F32, "mla_out_fwd", res=h4)
    gather_wait(3, h5)
    h6, n12, z12a, z12b = _ffn_fwd(h5, A['ffn_norm2'][1], *ffn_weights(2, 1), token, "ffn2_l1_fwd")

    def row_slots(g):
        return g.reshape(4, g.shape[0] // 4, g.shape[1])

    scatter_flights = []

    def scatter_start(named):
        srcs = [g for _, g in named]
        lands = [_landing(lax.dynamic_index_in_dim(g, me, 0, keepdims=False), me) for g in srcs]
        flight, token = _exchange_start(srcs, lands, True, f"scatter_start_{len(scatter_flights)}")
        scatter_flights.append(([key for key, _ in named], flight))
        return token

    def send_ffn(k, l, dw1t, dw3t, dw2):
        return scatter_start([((f'ffn{k}_w1', l), row_slots(dw1t)), ((f'ffn{k}_w3', l), row_slots(dw3t)),
                              ((f'ffn{k}_w2', l), row_slots(dw2))])

    dh6, dg_final, loss_part = _loss_bwd(h6, target, A['final_norm'])
    dh5, dg_n2_l1, *dws = _ffn_bwd(dh6, h5, A['ffn_norm2'][1], n12, z12a, z12b, *ffn_w[2, 1], loss_part, "ffn2_l1")
    token = send_ffn(2, 1, *dws)

    do = _mm([(dh5, w_o)], BF16, "mla_out_bwd", trans_b=True, dep=token)
    dw_o = _mm_tn(o, dh5, BF16, "mla_dw_o")
    delta = _attn_delta(do, o)
    dq, dk, dv = _flash_bwd(q, k, v, do, lse, delta)
    dr, dkv, dcq, dckv, dar = _mla_qkv_bwd(dq, dk, dv, cs_c, cs_s, wuq, wukv)
    dwuq = _mm_tn(cq, dr, BF16, "mla_dw_uq", bn=dr.shape[1] // 2)
    dwukv = _mm_tn(ckv, dkv, BF16, "mla_dw_ukv", bn=dkv.shape[1] // 2)
    da_lat, dgq, dgkv = _mla_prep_bwd(a_lat, dcq, dckv, dar, gq, gkv)
    dw_a = _mm_tn(m1, da_lat, BF16, "mla_dw_a")
    token = scatter_start([
        (('mla_w_a', 0), row_slots(dw_a[:, :Q_LORA + KV_LORA + ROPE])),
        (('mla_w_uq', 0), dwuq.reshape(4, Q_LORA // 4, HEADS, HEAD_PAD)[..., :NOPE + ROPE]),
        (('mla_w_ukv', 0), dwukv.reshape(4, KV_LORA // 4, HEADS, NOPE + V_HEAD)),
        (('mla_w_o', 0), row_slots(dw_o))])
    dh4, dg_mix_l1 = _mm_normbwd([(da_lat, w_a)], h4, A['mix_norm'][1], dh5, token, "mla_down_bwd")

    dh3, dg_n1_l1, *dws = _ffn_bwd(dh4, h3, A['ffn_norm1'][1], n11, z11a, z11b, *ffn_w[1, 1], token, "ffn1_l1")
    token = send_ffn(1, 1, *dws)
    dh2, dg_n2_l0, *dws = _ffn_bwd(dh3, h2, A['ffn_norm2'][0], n02, z02a, z02b, *ffn_w[2, 0], token, "ffn2_l0")
    token = send_ffn(2, 0, *dws)

    dcv, dg_conv = _conv_bwd_norm(dh2, cv, pw2, A['conv_norm'][0], token)
    dw_pw2 = _mm_tn(cs, dh2, BF16, "conv_dw_pw2")
    dca, dcb, ddw = _conv_bwd_dw(dcv, glu, ca, cb, w_dw)
    dpw1_a = _mm_tn(m0, dca, BF16, "conv_dw_pw1a")
    dpw1_b = _mm_tn(m0, dcb, BF16, "conv_dw_pw1b")
    half = dpw1_a.shape[1] // 2
    token = scatter_start([
        (('conv_w_pw1', 0), jnp.stack([dpw1_a[:, :half], dpw1_a[:, half:], dpw1_b[:, :half], dpw1_b[:, half:]])),
        (('conv_w_pw2', 0), row_slots(dw_pw2))])
    dh1, dg_mix_l0 = _mm_normbwd([(dca, pw1_a), (dcb, pw1_b)], h1, A['mix_norm'][0], dh2, token, "conv_pw1_bwd")

    dx, dg_n1_l0, *dws = _ffn_bwd(dh1, h0, A['ffn_norm1'][0], n01, z01a, z01b, *ffn_w[1, 0], token, "ffn1_l0")
    last_sent = send_ffn(1, 0, *dws)
    out = {}

    qkv_row = jnp.concatenate([dgq, dgkv, jnp.zeros((8, d - Q_LORA - KV_LORA), F32)], axis=1)
    loss_row = _pad_lanes(loss_part, d)
    small_g = jnp.concatenate([dg_n1_l0, dg_n1_l1, dg_mix_l0, dg_mix_l1, dg_n2_l0, dg_n2_l1, dg_conv, dg_final,
                               qkv_row, loss_row, ddw], axis=0)
    nrow = small_g.shape[0]
    tot = _sum_devices(_all_gather_rows(small_g, "gather_small_grads").reshape(8, nrow, d), "sum_small_grads")
    loss = tot[72, 0]
    q_shard = lax.dynamic_slice_in_dim(tot[64, :Q_LORA], me * (Q_LORA // 4), Q_LORA // 4)
    kv_shard = lax.dynamic_slice_in_dim(tot[64, Q_LORA:Q_LORA + KV_LORA], me * (KV_LORA // 4), KV_LORA // 4)
    dw_shard_g = lax.dynamic_slice_in_dim(tot[80:80 + CONV_WIDTH], me * cw, cw, axis=1)
    small_grads = {
        'ffn_norm1': jnp.stack([tot[0], tot[8]]), 'mix_norm': jnp.stack([tot[16], tot[24]]),
        'ffn_norm2': jnp.stack([tot[32], tot[40]]), 'conv_norm': tot[48][None], 'final_norm': tot[56],
        'mla_q_norm': q_shard[None], 'mla_kv_norm': kv_shard[None], 'conv_w_dw': dw_shard_g[None],
    }
    for nm, g in small_grads.items():
        res = _adamw(_as_rows(A[nm]) if A[nm].ndim > 1 else A[nm].reshape(1, -1),
                     A['m_' + nm].reshape(-1, A[nm].shape[-1]), A['v_' + nm].reshape(-1, A[nm].shape[-1]),
                     [g.reshape(-1, A[nm].shape[-1])], "adamw_" + nm)
        out[nm] = [r.reshape(A[nm].shape) for r in res]

    received = {}
    after = last_sent

    def scatter_wait(si, after):
        keys, flight = scatter_flights[si]
        landed = _exchange_wait(flight, after, True, f"scatter_wait_{si}")
        received.update(zip(keys, landed))
        return landed[0]

    def slots(nm, l):
        return received[nm, l].reshape(4, -1, received[nm, l].shape[-1])

    def finish(names, sums, tag):
        parts = dict(zip(names, zip(sums, _swap_with_sibling(sums, "swap_with_sibling_" + tag))))
        groups = [g for g in (ffn[:3], ffn[3:]) if g[0] in parts] + [[nm] for nm in names if nm not in ffn]
        for group in groups:
            ress = _adamw_group([_as_rows(P[nm]) for nm in group], [_as_rows(P['m_' + nm]) for nm in group],
                                [_as_rows(P['v_' + nm]) for nm in group], [list(parts[nm]) for nm in group],
                                "adamw_" + group[0] + ("_group" if len(group) > 1 else ""),
                                (3 << 19) // (2 if len(group) > 1 else 1))
            for nm, res in zip(group, ress):
                out[nm] = [unflip(nm, r.reshape(P[nm].shape)) for r in res]
        return res[1]

    last = len(scatter_flights) - 1
    for si in range(last):
        after = scatter_wait(si, after)
    late = ffn[:3]
    early = [nm for nm in BIG if nm not in late]
    late_l1 = [_sum_slots([(1, slots(nm, 1))], 2, "sum_" + nm) for nm in late]
    after = finish(early, [_sum_slots([(l, slots(nm, l)) for l in range(A[nm].shape[0])], A[nm].shape[0],
                                      "sum_" + nm) for nm in early], "early")
    scatter_wait(last, after)
    finish(late, [_sum_slots([(0, slots(nm, 0))], 2, "sum_" + nm, into=part) for nm, part in zip(late, late_l1)],
           "late")

    return (loss, dx[None], *[out[nm][0] for nm in WEIGHTS], *[out[nm][1] for nm in WEIGHTS],
            *[out[nm][2] for nm in WEIGHTS], *[out[nm][3] for nm in WEIGHTS])
```
